```python
import jax, jax.numpy as jnp
from jax import lax
import numpy as np

D_MODEL = 1024
BATCH = 8
SEQ = 2048
DEPTH = 4

EPS_RMS = 1e-6
EPS_LN = 1e-5
W_CONV = D_MODEL
CONV_K = 31
W_POOL = D_MODEL
POOL_WINDOWS = (2, 4, 8, 16)
POOL_GROUPS = len(POOL_WINDOWS)
POOL_GW = W_POOL // POOL_GROUPS
W_EVEN_IN = 3 * W_CONV + 2 * W_POOL
W_EVEN_MIX = W_CONV + W_POOL
LRU_HEADS = 12
LRU_HD = 128
W_LRU = LRU_HEADS * LRU_HD
LRU_CONV_K = 4
LRU_C = 8.0
N_EVEN = (DEPTH + 1) // 2
N_ODD = DEPTH // 2

kernel_name = "hybrid_conv_pool_rglru_trunk"


def rmsnorm(x, g):
    xf = x.astype(jnp.float32)
    y = xf * lax.rsqrt(jnp.mean(xf * xf, axis=-1, keepdims=True) + EPS_RMS)
    return (y * g.astype(jnp.float32)).astype(x.dtype)


def layernorm(x, g, b):
    xf = x.astype(jnp.float32)
    mu = jnp.mean(xf, axis=-1, keepdims=True)
    var = jnp.mean(jnp.square(xf - mu), axis=-1, keepdims=True)
    y = (xf - mu) * lax.rsqrt(var + EPS_LN)
    return (y * g.astype(jnp.float32) + b.astype(jnp.float32)).astype(x.dtype)


def causal_depthwise_conv(x, w, b):
    k = w.shape[0]
    y = lax.conv_general_dilated(
        x, w[:, None, :].astype(x.dtype), window_strides=(1,), padding=[(k - 1, 0)],
        dimension_numbers=("NWC", "WIO", "NWC"), feature_group_count=x.shape[-1])
    return y + b.astype(x.dtype)


def multiscale_pool_diff(v):
    bsz, t, _ = v.shape
    vf = v.astype(jnp.float32)
    cs_pad = jnp.pad(jnp.cumsum(vf, axis=1), ((0, 0), (1, 0), (0, 0)))
    pos = jnp.arange(1, t + 1, dtype=jnp.float32)
    outs = []
    for g, w in enumerate(POOL_WINDOWS):
        seg = cs_pad[:, :, g * POOL_GW:(g + 1) * POOL_GW]
        upper = seg[:, 1:]
        lower = jnp.pad(seg[:, :t - w + 1], ((0, 0), (w - 1, 0), (0, 0)))
        cnt = jnp.minimum(pos, jnp.float32(w))[None, :, None]
        outs.append((upper - lower) / cnt)
    return jnp.concatenate(outs, axis=-1) - vf


def even_mixer(h, w_in, conv_w, conv_b, ln_g, ln_b, pool_w, pool_b, pool_scale, w_out):
    bsz, t, _ = h.shape
    p = jnp.einsum("btd,dn->btn", h, w_in)
    a_val, a_glu, a_gate, b_val, b_gate = jnp.split(
        p, [W_CONV, 2 * W_CONV, 3 * W_CONV, 3 * W_CONV + W_POOL], axis=-1)
    u = a_val * jax.nn.sigmoid(a_glu)
    u = causal_depthwise_conv(u, conv_w, conv_b)
    u = jax.nn.silu(layernorm(u, ln_g, ln_b))
    ya = u * jax.nn.silu(a_gate)
    d = multiscale_pool_diff(b_val).reshape(bsz, t, POOL_GROUPS, POOL_GW)
    d = jnp.einsum("btgc,gce->btge", d, pool_w.astype(jnp.float32)) + pool_b.astype(jnp.float32)
    yb = (d.reshape(bsz, t, W_POOL) * pool_scale.astype(jnp.float32)).astype(h.dtype)
    yb = yb * jax.nn.silu(b_gate)
    y = jnp.concatenate([ya, yb], axis=-1)
    return jnp.einsum("btn,nd->btd", y, w_out)


def _lin_combine(c1, c2):
    a1, b1 = c1
    a2, b2 = c2
    return a1 * a2, a2 * b1 + b2


def odd_mixer(h, w_in, conv_w, conv_b, w_rg, b_rg, w_ig, b_ig, lam, w_out):
    bsz, t, _ = h.shape
    p = jnp.einsum("btd,dn->btn", h, w_in)
    xr, gate = jnp.split(p, [W_LRU], axis=-1)
    xc = causal_depthwise_conv(xr, conv_w, conv_b)
    xh = xc.reshape(bsz, t, LRU_HEADS, LRU_HD)
    r = jax.nn.sigmoid(jnp.einsum("bthi,hij->bthj", xh, w_rg).reshape(bsz, t, W_LRU).astype(jnp.float32)
                       + b_rg.astype(jnp.float32))
    i = jax.nn.sigmoid(jnp.einsum("bthi,hij->bthj", xh, w_ig).reshape(bsz, t, W_LRU).astype(jnp.float32)
                       + b_ig.astype(jnp.float32))
    log_a = -LRU_C * r * jax.nn.softplus(-lam.astype(jnp.float32))
    a = jnp.exp(log_a)
    mult = jnp.sqrt(-jnp.expm1(2.0 * log_a))
    bterm = mult * (i * xc.astype(jnp.float32))
    _, hs = lax.associative_scan(_lin_combine, (a, bterm), axis=1)
    y = hs.astype(h.dtype) * jax.nn.silu(gate)
    return jnp.einsum("btn,nd->btd", y, w_out)


def _fwd_setup_inputs(seed: int = 0) -> dict:
    key = jax.random.key(seed)
    ks = iter(jax.random.split(key, 32))
    f32 = jnp.float32

    def nrm(shape, scale):
        return jax.random.normal(next(ks), shape, f32) * scale

    x = jax.random.normal(next(ks), (BATCH, SEQ, D_MODEL), f32)
    a0 = jax.random.uniform(next(ks), (N_ODD, W_LRU), f32, 0.9, 0.999)
    s = a0 ** (1.0 / LRU_C)
    lru_lambda = jnp.log(s) - jnp.log1p(-s)
    return {
        "x": x,
        "norm_even": 1.0 + nrm((N_EVEN, D_MODEL), 0.02),
        "w_in_even": nrm((N_EVEN, D_MODEL, W_EVEN_IN), D_MODEL ** -0.5),
        "conv_a_w": nrm((N_EVEN, CONV_K, W_CONV), CONV_K ** -0.5),
        "conv_a_b": nrm((N_EVEN, W_CONV), 0.01),
        "ln_a_g": 1.0 + nrm((N_EVEN, W_CONV), 0.02),
        "ln_a_b": nrm((N_EVEN, W_CONV), 0.01),
        "pool_w": nrm((N_EVEN, POOL_GROUPS, POOL_GW, POOL_GW), POOL_GW ** -0.5),
        "pool_b": nrm((N_EVEN, POOL_GROUPS, POOL_GW), 0.01),
        "pool_scale": 1.0 + nrm((N_EVEN, W_POOL), 0.1),
        "w_out_even": nrm((N_EVEN, W_EVEN_MIX, D_MODEL), W_EVEN_MIX ** -0.5),
        "norm_odd": 1.0 + nrm((N_ODD, D_MODEL), 0.02),
        "w_in_odd": nrm((N_ODD, D_MODEL, 2 * W_LRU), D_MODEL ** -0.5),
        "conv_c_w": nrm((N_ODD, LRU_CONV_K, W_LRU), LRU_CONV_K ** -0.5),
        "conv_c_b": nrm((N_ODD, W_LRU), 0.01),
        "w_rg": nrm((N_ODD, LRU_HEADS, LRU_HD, LRU_HD), LRU_HD ** -0.5),
        "b_rg": nrm((N_ODD, W_LRU), 0.01),
        "w_ig": nrm((N_ODD, LRU_HEADS, LRU_HD, LRU_HD), LRU_HD ** -0.5),
        "b_ig": nrm((N_ODD, W_LRU), 0.01),
        "lru_lambda": lru_lambda,
        "w_out_odd": nrm((N_ODD, W_LRU, D_MODEL), W_LRU ** -0.5),
        "final_norm": 1.0 + nrm((D_MODEL,), 0.02),
    }


def _fwd_reference(x, norm_even, w_in_even, conv_a_w, conv_a_b, ln_a_g, ln_a_b, pool_w, pool_b,
              pool_scale, w_out_even, norm_odd, w_in_odd, conv_c_w, conv_c_b, w_rg, b_rg,
              w_ig, b_ig, lru_lambda, w_out_odd, final_norm):
    h = x
    for layer in range(DEPTH):
        if layer % 2 == 0:
            j = layer // 2
            h = h + even_mixer(rmsnorm(h, norm_even[j]), w_in_even[j], conv_a_w[j], conv_a_b[j],
                               ln_a_g[j], ln_a_b[j], pool_w[j], pool_b[j], pool_scale[j],
                               w_out_even[j])
        else:
            j = layer // 2
            h = h + odd_mixer(rmsnorm(h, norm_odd[j]), w_in_odd[j], conv_c_w[j], conv_c_b[j],
                              w_rg[j], b_rg[j], w_ig[j], b_ig[j], lru_lambda[j], w_out_odd[j])
    return rmsnorm(h, final_norm)


import jax as _jax
import jax.numpy as _jnp

TWIN_FORMAT = 'train_step'
FWD_PARAMS = ['x', 'norm_even', 'w_in_even', 'conv_a_w', 'conv_a_b', 'ln_a_g', 'ln_a_b', 'pool_w', 'pool_b', 'pool_scale', 'w_out_even', 'norm_odd', 'w_in_odd', 'conv_c_w', 'conv_c_b', 'w_rg', 'b_rg', 'w_ig', 'b_ig', 'lru_lambda', 'w_out_odd', 'final_norm']
TWIN_WEIGHTS = ['norm_even', 'w_in_even', 'conv_a_w', 'conv_a_b', 'ln_a_g', 'ln_a_b', 'pool_w', 'pool_b', 'pool_scale', 'w_out_even', 'norm_odd', 'w_in_odd', 'conv_c_w', 'conv_c_b', 'w_rg', 'b_rg', 'w_ig', 'b_ig', 'lru_lambda', 'w_out_odd', 'final_norm']
TWIN_DIFF_INPUT = 'x'
TWIN_INPUTS = ['x', 'norm_even', 'w_in_even', 'conv_a_w', 'conv_a_b', 'ln_a_g', 'ln_a_b', 'pool_w', 'pool_b', 'pool_scale', 'w_out_even', 'norm_odd', 'w_in_odd', 'conv_c_w', 'conv_c_b', 'w_rg', 'b_rg', 'w_ig', 'b_ig', 'lru_lambda', 'w_out_odd', 'final_norm', 'loss_target', 'm_norm_even', 'm_w_in_even', 'm_conv_a_w', 'm_conv_a_b', 'm_ln_a_g', 'm_ln_a_b', 'm_pool_w', 'm_pool_b', 'm_pool_scale', 'm_w_out_even', 'm_norm_odd', 'm_w_in_odd', 'm_conv_c_w', 'm_conv_c_b', 'm_w_rg', 'm_b_rg', 'm_w_ig', 'm_b_ig', 'm_lru_lambda', 'm_w_out_odd', 'm_final_norm', 'v_norm_even', 'v_w_in_even', 'v_conv_a_w', 'v_conv_a_b', 'v_ln_a_g', 'v_ln_a_b', 'v_pool_w', 'v_pool_b', 'v_pool_scale', 'v_w_out_even', 'v_norm_odd', 'v_w_in_odd', 'v_conv_c_w', 'v_conv_c_b', 'v_w_rg', 'v_b_rg', 'v_w_ig', 'v_b_ig', 'v_lru_lambda', 'v_w_out_odd', 'v_final_norm']
TWIN_OUTPUTS = ['loss', 'grad_x', 'grad_norm_even', 'grad_w_in_even', 'grad_conv_a_w', 'grad_conv_a_b', 'grad_ln_a_g', 'grad_ln_a_b', 'grad_pool_w', 'grad_pool_b', 'grad_pool_scale', 'grad_w_out_even', 'grad_norm_odd', 'grad_w_in_odd', 'grad_conv_c_w', 'grad_conv_c_b', 'grad_w_rg', 'grad_b_rg', 'grad_w_ig', 'grad_b_ig', 'grad_lru_lambda', 'grad_w_out_odd', 'grad_final_norm', 'delta_norm_even', 'delta_w_in_even', 'delta_conv_a_w', 'delta_conv_a_b', 'delta_ln_a_g', 'delta_ln_a_b', 'delta_pool_w', 'delta_pool_b', 'delta_pool_scale', 'delta_w_out_even', 'delta_norm_odd', 'delta_w_in_odd', 'delta_conv_c_w', 'delta_conv_c_b', 'delta_w_rg', 'delta_b_rg', 'delta_w_ig', 'delta_b_ig', 'delta_lru_lambda', 'delta_w_out_odd', 'delta_final_norm', 'new_m_norm_even', 'new_m_w_in_even', 'new_m_conv_a_w', 'new_m_conv_a_b', 'new_m_ln_a_g', 'new_m_ln_a_b', 'new_m_pool_w', 'new_m_pool_b', 'new_m_pool_scale', 'new_m_w_out_even', 'new_m_norm_odd', 'new_m_w_in_odd', 'new_m_conv_c_w', 'new_m_conv_c_b', 'new_m_w_rg', 'new_m_b_rg', 'new_m_w_ig', 'new_m_b_ig', 'new_m_lru_lambda', 'new_m_w_out_odd', 'new_m_final_norm', 'new_v_norm_even', 'new_v_w_in_even', 'new_v_conv_a_w', 'new_v_conv_a_b', 'new_v_ln_a_g', 'new_v_ln_a_b', 'new_v_pool_w', 'new_v_pool_b', 'new_v_pool_scale', 'new_v_w_out_even', 'new_v_norm_odd', 'new_v_w_in_odd', 'new_v_conv_c_w', 'new_v_conv_c_b', 'new_v_w_rg', 'new_v_b_rg', 'new_v_w_ig', 'new_v_b_ig', 'new_v_lru_lambda', 'new_v_w_out_odd', 'new_v_final_norm']
TWIN_LEAF_KINDS = {'loss': 'loss', 'grad_x': 'grad_x', 'grad_norm_even': 'grad_w', 'grad_w_in_even': 'grad_w', 'grad_conv_a_w': 'grad_w', 'grad_conv_a_b': 'grad_w', 'grad_ln_a_g': 'grad_w', 'grad_ln_a_b': 'grad_w', 'grad_pool_w': 'grad_w', 'grad_pool_b': 'grad_w', 'grad_pool_scale': 'grad_w', 'grad_w_out_even': 'grad_w', 'grad_norm_odd': 'grad_w', 'grad_w_in_odd': 'grad_w', 'grad_conv_c_w': 'grad_w', 'grad_conv_c_b': 'grad_w', 'grad_w_rg': 'grad_w', 'grad_b_rg': 'grad_w', 'grad_w_ig': 'grad_w', 'grad_b_ig': 'grad_w', 'grad_lru_lambda': 'grad_w', 'grad_w_out_odd': 'grad_w', 'grad_final_norm': 'grad_w', 'delta_norm_even': 'delta_w', 'delta_w_in_even': 'delta_w', 'delta_conv_a_w': 'delta_w', 'delta_conv_a_b': 'delta_w', 'delta_ln_a_g': 'delta_w', 'delta_ln_a_b': 'delta_w', 'delta_pool_w': 'delta_w', 'delta_pool_b': 'delta_w', 'delta_pool_scale': 'delta_w', 'delta_w_out_even': 'delta_w', 'delta_norm_odd': 'delta_w', 'delta_w_in_odd': 'delta_w', 'delta_conv_c_w': 'delta_w', 'delta_conv_c_b': 'delta_w', 'delta_w_rg': 'delta_w', 'delta_b_rg': 'delta_w', 'delta_w_ig': 'delta_w', 'delta_b_ig': 'delta_w', 'delta_lru_lambda': 'delta_w', 'delta_w_out_odd': 'delta_w', 'delta_final_norm': 'delta_w', 'new_m_norm_even': 'new_m', 'new_m_w_in_even': 'new_m', 'new_m_conv_a_w': 'new_m', 'new_m_conv_a_b': 'new_m', 'new_m_ln_a_g': 'new_m', 'new_m_ln_a_b': 'new_m', 'new_m_pool_w': 'new_m', 'new_m_pool_b': 'new_m', 'new_m_pool_scale': 'new_m', 'new_m_w_out_even': 'new_m', 'new_m_norm_odd': 'new_m', 'new_m_w_in_odd': 'new_m', 'new_m_conv_c_w': 'new_m', 'new_m_conv_c_b': 'new_m', 'new_m_w_rg': 'new_m', 'new_m_b_rg': 'new_m', 'new_m_w_ig': 'new_m', 'new_m_b_ig': 'new_m', 'new_m_lru_lambda': 'new_m', 'new_m_w_out_odd': 'new_m', 'new_m_final_norm': 'new_m', 'new_v_norm_even': 'new_v', 'new_v_w_in_even': 'new_v', 'new_v_conv_a_w': 'new_v', 'new_v_conv_a_b': 'new_v', 'new_v_ln_a_g': 'new_v', 'new_v_ln_a_b': 'new_v', 'new_v_pool_w': 'new_v', 'new_v_pool_b': 'new_v', 'new_v_pool_scale': 'new_v', 'new_v_w_out_even': 'new_v', 'new_v_norm_odd': 'new_v', 'new_v_w_in_odd': 'new_v', 'new_v_conv_c_w': 'new_v', 'new_v_conv_c_b': 'new_v', 'new_v_w_rg': 'new_v', 'new_v_b_rg': 'new_v', 'new_v_w_ig': 'new_v', 'new_v_b_ig': 'new_v', 'new_v_lru_lambda': 'new_v', 'new_v_w_out_odd': 'new_v', 'new_v_final_norm': 'new_v'}


def _forward(args):
    return _fwd_reference(*[args[k] for k in FWD_PARAMS])


def _output_shape():
    out = _jax.eval_shape(lambda: _forward(_fwd_setup_inputs(0)))
    return out.shape, out.dtype

N_MICROBATCH = 1
ADAM_LR = 0.001
ADAM_B1 = 0.9
ADAM_B2 = 0.999
ADAM_EPS = 1e-08
ADAM_WD = 0.01
ADAM_STEP = 10
PER_EXAMPLE_BATCH_AXIS = {'x': 0, 'loss_target': 0}
SHARED_INPUTS = []
_WEIGHT_DTYPES = {'norm_even': _jnp.float32, 'w_in_even': _jnp.float32, 'conv_a_w': _jnp.float32, 'conv_a_b': _jnp.float32, 'ln_a_g': _jnp.float32, 'ln_a_b': _jnp.float32, 'pool_w': _jnp.float32, 'pool_b': _jnp.float32, 'pool_scale': _jnp.float32, 'w_out_even': _jnp.float32, 'norm_odd': _jnp.float32, 'w_in_odd': _jnp.float32, 'conv_c_w': _jnp.float32, 'conv_c_b': _jnp.float32, 'w_rg': _jnp.float32, 'b_rg': _jnp.float32, 'w_ig': _jnp.float32, 'b_ig': _jnp.float32, 'lru_lambda': _jnp.float32, 'w_out_odd': _jnp.float32, 'final_norm': _jnp.float32}
MOMENT_SCALE = {'norm_even': 8.367462e-02, 'w_in_even': 3.574598e-02, 'conv_a_w': 3.185415e-02, 'conv_a_b': 8.044798e-02, 'ln_a_g': 4.365497e-02, 'ln_a_b': 4.743934e-02, 'pool_w': 4.531431e-02, 'pool_b': 7.857487e-02, 'pool_scale': 4.544444e-02, 'w_out_even': 5.622594e-02, 'norm_odd': 6.782498e-02, 'w_in_odd': 4.053492e-02, 'conv_c_w': 4.220113e-02, 'conv_c_b': 4.372314e-01, 'w_rg': 1.257923e-02, 'b_rg': 9.898503e-03, 'w_ig': 2.253242e-02, 'b_ig': 1.566541e-02, 'lru_lambda': 2.032671e-02, 'w_out_odd': 5.161129e-02, 'final_norm': 1.603404e+01}


def _to_microbatches(a, axis):
    t = _jnp.moveaxis(a, axis, 0)
    t = t.reshape((N_MICROBATCH, t.shape[0] // N_MICROBATCH) + t.shape[1:])
    return _jnp.moveaxis(t, 1, axis + 1)


def setup_inputs(seed: int = 0) -> dict:
    inp = _fwd_setup_inputs(seed)
    key = _jax.random.fold_in(_jax.random.key(seed), 7919)
    shape, _ = _output_shape()
    out = dict(inp)
    out["loss_target"] = _jax.random.normal(_jax.random.fold_in(key, 0), shape, _jnp.float32)
    for i, name in enumerate(TWIN_WEIGHTS):
        w = inp[name].astype(_jnp.float32)
        if MOMENT_SCALE is None:
            s = _jnp.sqrt(_jnp.mean(_jnp.square(w)) + 1e-30)
        else:
            s = MOMENT_SCALE[name]
        km, kv = _jax.random.split(_jax.random.fold_in(key, i + 1))
        out[name] = w
        out["m_" + name] = s * _jax.random.normal(km, w.shape, _jnp.float32)
        out["v_" + name] = (s * s) * _jax.random.uniform(kv, w.shape, _jnp.float32, 0.5, 1.5)
    if N_MICROBATCH > 1:
        for name, axis in PER_EXAMPLE_BATCH_AXIS.items():
            out[name] = _to_microbatches(out[name], axis)
    return {'x': out['x'], 'norm_even': out['norm_even'], 'w_in_even': out['w_in_even'], 'conv_a_w': out['conv_a_w'], 'conv_a_b': out['conv_a_b'], 'ln_a_g': out['ln_a_g'], 'ln_a_b': out['ln_a_b'], 'pool_w': out['pool_w'], 'pool_b': out['pool_b'], 'pool_scale': out['pool_scale'], 'w_out_even': out['w_out_even'], 'norm_odd': out['norm_odd'], 'w_in_odd': out['w_in_odd'], 'conv_c_w': out['conv_c_w'], 'conv_c_b': out['conv_c_b'], 'w_rg': out['w_rg'], 'b_rg': out['b_rg'], 'w_ig': out['w_ig'], 'b_ig': out['b_ig'], 'lru_lambda': out['lru_lambda'], 'w_out_odd': out['w_out_odd'], 'final_norm': out['final_norm'], 'loss_target': out['loss_target'], 'm_norm_even': out['m_norm_even'], 'm_w_in_even': out['m_w_in_even'], 'm_conv_a_w': out['m_conv_a_w'], 'm_conv_a_b': out['m_conv_a_b'], 'm_ln_a_g': out['m_ln_a_g'], 'm_ln_a_b': out['m_ln_a_b'], 'm_pool_w': out['m_pool_w'], 'm_pool_b': out['m_pool_b'], 'm_pool_scale': out['m_pool_scale'], 'm_w_out_even': out['m_w_out_even'], 'm_norm_odd': out['m_norm_odd'], 'm_w_in_odd': out['m_w_in_odd'], 'm_conv_c_w': out['m_conv_c_w'], 'm_conv_c_b': out['m_conv_c_b'], 'm_w_rg': out['m_w_rg'], 'm_b_rg': out['m_b_rg'], 'm_w_ig': out['m_w_ig'], 'm_b_ig': out['m_b_ig'], 'm_lru_lambda': out['m_lru_lambda'], 'm_w_out_odd': out['m_w_out_odd'], 'm_final_norm': out['m_final_norm'], 'v_norm_even': out['v_norm_even'], 'v_w_in_even': out['v_w_in_even'], 'v_conv_a_w': out['v_conv_a_w'], 'v_conv_a_b': out['v_conv_a_b'], 'v_ln_a_g': out['v_ln_a_g'], 'v_ln_a_b': out['v_ln_a_b'], 'v_pool_w': out['v_pool_w'], 'v_pool_b': out['v_pool_b'], 'v_pool_scale': out['v_pool_scale'], 'v_w_out_even': out['v_w_out_even'], 'v_norm_odd': out['v_norm_odd'], 'v_w_in_odd': out['v_w_in_odd'], 'v_conv_c_w': out['v_conv_c_w'], 'v_conv_c_b': out['v_conv_c_b'], 'v_w_rg': out['v_w_rg'], 'v_b_rg': out['v_b_rg'], 'v_w_ig': out['v_w_ig'], 'v_b_ig': out['v_b_ig'], 'v_lru_lambda': out['v_lru_lambda'], 'v_w_out_odd': out['v_w_out_odd'], 'v_final_norm': out['v_final_norm']}


def _loss(weights, diff, rest, loss_target):
    with _jax.named_scope("forward"):
        args = {**rest, TWIN_DIFF_INPUT: diff, **{k: w.astype(_WEIGHT_DTYPES[k]) for k, w in weights.items()}}
        y = _forward(args)
    with _jax.named_scope("loss_head"):
        err = _jnp.square(y.astype(_jnp.float32) - loss_target)
        return 0.5 * _jnp.sum(_jnp.mean(err, axis=-1)) if err.ndim else 0.5 * err


def _adamw(w, g, m, v):
    m = ADAM_B1 * m + (1.0 - ADAM_B1) * g
    v = ADAM_B2 * v + (1.0 - ADAM_B2) * _jnp.square(g)
    m_hat = m / (1.0 - ADAM_B1 ** ADAM_STEP)
    v_hat = v / (1.0 - ADAM_B2 ** ADAM_STEP)
    delta = -ADAM_LR * (m_hat / (_jnp.sqrt(v_hat) + ADAM_EPS) + ADAM_WD * w)
    return delta, m, v


def reference(x, norm_even, w_in_even, conv_a_w, conv_a_b, ln_a_g, ln_a_b, pool_w, pool_b, pool_scale, w_out_even, norm_odd, w_in_odd, conv_c_w, conv_c_b, w_rg, b_rg, w_ig, b_ig, lru_lambda, w_out_odd, final_norm, loss_target, m_norm_even, m_w_in_even, m_conv_a_w, m_conv_a_b, m_ln_a_g, m_ln_a_b, m_pool_w, m_pool_b, m_pool_scale, m_w_out_even, m_norm_odd, m_w_in_odd, m_conv_c_w, m_conv_c_b, m_w_rg, m_b_rg, m_w_ig, m_b_ig, m_lru_lambda, m_w_out_odd, m_final_norm, v_norm_even, v_w_in_even, v_conv_a_w, v_conv_a_b, v_ln_a_g, v_ln_a_b, v_pool_w, v_pool_b, v_pool_scale, v_w_out_even, v_norm_odd, v_w_in_odd, v_conv_c_w, v_conv_c_b, v_w_rg, v_b_rg, v_w_ig, v_b_ig, v_lru_lambda, v_w_out_odd, v_final_norm):
    given = dict(x=x, norm_even=norm_even, w_in_even=w_in_even, conv_a_w=conv_a_w, conv_a_b=conv_a_b, ln_a_g=ln_a_g, ln_a_b=ln_a_b, pool_w=pool_w, pool_b=pool_b, pool_scale=pool_scale, w_out_even=w_out_even, norm_odd=norm_odd, w_in_odd=w_in_odd, conv_c_w=conv_c_w, conv_c_b=conv_c_b, w_rg=w_rg, b_rg=b_rg, w_ig=w_ig, b_ig=b_ig, lru_lambda=lru_lambda, w_out_odd=w_out_odd, final_norm=final_norm, loss_target=loss_target, m_norm_even=m_norm_even, m_w_in_even=m_w_in_even, m_conv_a_w=m_conv_a_w, m_conv_a_b=m_conv_a_b, m_ln_a_g=m_ln_a_g, m_ln_a_b=m_ln_a_b, m_pool_w=m_pool_w, m_pool_b=m_pool_b, m_pool_scale=m_pool_scale, m_w_out_even=m_w_out_even, m_norm_odd=m_norm_odd, m_w_in_odd=m_w_in_odd, m_conv_c_w=m_conv_c_w, m_conv_c_b=m_conv_c_b, m_w_rg=m_w_rg, m_b_rg=m_b_rg, m_w_ig=m_w_ig, m_b_ig=m_b_ig, m_lru_lambda=m_lru_lambda, m_w_out_odd=m_w_out_odd, m_final_norm=m_final_norm, v_norm_even=v_norm_even, v_w_in_even=v_w_in_even, v_conv_a_w=v_conv_a_w, v_conv_a_b=v_conv_a_b, v_ln_a_g=v_ln_a_g, v_ln_a_b=v_ln_a_b, v_pool_w=v_pool_w, v_pool_b=v_pool_b, v_pool_scale=v_pool_scale, v_w_out_even=v_w_out_even, v_norm_odd=v_norm_odd, v_w_in_odd=v_w_in_odd, v_conv_c_w=v_conv_c_w, v_conv_c_b=v_conv_c_b, v_w_rg=v_w_rg, v_b_rg=v_b_rg, v_w_ig=v_w_ig, v_b_ig=v_b_ig, v_lru_lambda=v_lru_lambda, v_w_out_odd=v_w_out_odd, v_final_norm=v_final_norm)
    weights = {n: given[n] for n in TWIN_WEIGHTS}
    shared = {n: given[n] for n in SHARED_INPUTS}
    per_example = {n: given[n] for n in ['x']}
    grad_fn = _jax.value_and_grad(_loss, argnums=(0, 1))

    def one_microbatch(ex, loss_target):
        ex = dict(ex)
        diff = ex.pop(TWIN_DIFF_INPUT)
        return grad_fn(weights, diff, {**shared, **ex}, loss_target)

    if N_MICROBATCH == 1:
        loss, (grad_w, grad_x) = one_microbatch(per_example, given["loss_target"])
    else:
        def body(carry, xs):
            loss_sum, grad_sum = carry
            l_k, (gw_k, gx_k) = one_microbatch(xs[0], xs[1])
            with _jax.named_scope("update"):
                return (loss_sum + l_k, _jax.tree.map(_jnp.add, grad_sum, gw_k)), gx_k

        init = (_jnp.zeros((), _jnp.float32), _jax.tree.map(_jnp.zeros_like, weights))
        (loss, grad_w), grad_x = _jax.lax.scan(body, init, (per_example, given["loss_target"]))
    with _jax.named_scope("update"):
        delta_w, new_m, new_v = {}, {}, {}
        for n in TWIN_WEIGHTS:
            delta_w[n], new_m[n], new_v[n] = _adamw(weights[n], grad_w[n], given["m_" + n], given["v_" + n])
    return (loss, grad_x, *[grad_w[n] for n in TWIN_WEIGHTS], *[delta_w[n] for n in TWIN_WEIGHTS],
            *[new_m[n] for n in TWIN_WEIGHTS], *[new_v[n] for n in TWIN_WEIGHTS])
```

```python
import functools

import jax
import jax.numpy as jnp
from jax import lax
from jax.experimental import pallas as pl
from jax.experimental.pallas import tpu as pltpu

F32 = jnp.float32
BF16 = jnp.bfloat16

N_DEV = 8
D_MODEL = 1024
EPS_RMS = 1e-6
EPS_LN = 1e-5
W_CONV = 1024
CONV_K = 31
W_POOL = 1024
POOL_WINDOWS = (2, 4, 8, 16)
POOL_GW = 256
W_EVEN_IN = 5120
W_EVEN_MIX = 2048
LRU_HEADS = 12
LRU_HD = 128
W_LRU = 1536
LRU_CONV_K = 4
LRU_C = 8.0
ADAM_LR = 0.001
ADAM_B1 = 0.9
ADAM_B2 = 0.999
ADAM_EPS = 1e-08
ADAM_WD = 0.01
ADAM_STEP = 10

HALO = 32
HALO_C = 8
VMEM_LIMIT = 56 * 1024 * 1024
MESH = pl.DeviceIdType.MESH


def _params(*sem):
    return pltpu.CompilerParams(dimension_semantics=sem, vmem_limit_bytes=VMEM_LIMIT)


def _sigmoid(z):
    return jax.nn.sigmoid(z)


def _dsilu(z, s):
    return s * (1.0 + z * (1.0 - s))


def _full(shape):
    nd = len(shape)
    return pl.BlockSpec(shape, lambda *_: (0,) * nd)


def _norm_matmul(h, g, w, *, tm):
    t, d = h.shape
    nd, _, nb = w.shape

    def body(h_ref, g_ref, w_ref, p_ref, hn_ref):
        @pl.when(pl.program_id(1) == 0)
        def _():
            x = h_ref[...]
            r = lax.rsqrt(jnp.mean(x * x, axis=-1, keepdims=True) + EPS_RMS)
            hn_ref[...] = ((x * r) * g_ref[...]).astype(BF16)

        p_ref[...] = jnp.dot(hn_ref[...], w_ref[0], preferred_element_type=F32)

    return pl.pallas_call(
        body, name="norm_matmul", grid=(t // tm, nd),
        in_specs=[pl.BlockSpec((tm, d), lambda i, j: (i, 0)), _full((1, d)),
                  pl.BlockSpec((1, d, nb), lambda i, j: (j, 0, 0))],
        out_specs=[pl.BlockSpec((tm, nb), lambda i, j: (i, j)), pl.BlockSpec((tm, d), lambda i, j: (i, 0))],
        out_shape=[jax.ShapeDtypeStruct((t, nd * nb), F32), jax.ShapeDtypeStruct((t, d), BF16)],
        compiler_params=_params("arbitrary", "arbitrary"),
    )(h, g, w)


def _out_proj(h, y, w, *, tm):
    t, d = h.shape
    k = y.shape[1]

    def body(h_ref, y_ref, w_ref, o_ref):
        o_ref[...] = h_ref[...] + jnp.dot(y_ref[...], w_ref[...], preferred_element_type=F32)

    return pl.pallas_call(
        body, name="out_proj", grid=(t // tm,),
        in_specs=[pl.BlockSpec((tm, d), lambda i: (i, 0)), pl.BlockSpec((tm, k), lambda i: (i, 0)), _full((k, d))],
        out_specs=pl.BlockSpec((tm, d), lambda i: (i, 0)),
        out_shape=jax.ShapeDtypeStruct((t, d), F32),
        compiler_params=_params("arbitrary"),
    )(h, y, w)


def _out_proj_bwd(dh, y, w, *, tm):
    t, d = dh.shape
    k = y.shape[1]
    nt = t // tm

    def body(dh_ref, y_ref, w_ref, dy_ref, dw_ref, acc):
        i = pl.program_id(0)
        g = dh_ref[...].astype(BF16)
        dy_ref[...] = lax.dot_general(g, w_ref[...], (((1,), (1,)), ((), ())), preferred_element_type=F32)
        part = lax.dot_general(y_ref[...], g, (((0,), (0,)), ((), ())), preferred_element_type=F32)

        @pl.when(i == 0)
        def _():
            acc[...] = part

        @pl.when(i > 0)
        def _():
            acc[...] += part

        @pl.when(i == nt - 1)
        def _():
            dw_ref[...] = acc[...].astype(BF16)

    return pl.pallas_call(
        body, name="out_proj_bwd", grid=(nt,),
        in_specs=[pl.BlockSpec((tm, d), lambda i: (i, 0)), pl.BlockSpec((tm, k), lambda i: (i, 0)), _full((k, d))],
        out_specs=[pl.BlockSpec((tm, k), lambda i: (i, 0)), _full((k, d))],
        out_shape=[jax.ShapeDtypeStruct((t, k), F32), jax.ShapeDtypeStruct((k, d), BF16)],
        scratch_shapes=[pltpu.VMEM((k, d), F32)],
        compiler_params=_params("arbitrary"),
    )(dh, y, w)


def _in_proj_bwd_x(dp, w, h, g, dh_out, *, tm):
    t, d = h.shape
    nd, _, nb = w.shape
    nt = t // tm

    def body(dp_ref, w_ref, h_ref, g_ref, dho_ref, dh_ref, dg_ref, acc):
        i, j = pl.program_id(0), pl.program_id(1)
        part = lax.dot_general(dp_ref[...], w_ref[0], (((1,), (1,)), ((), ())), preferred_element_type=F32)

        @pl.when(j == 0)
        def _():
            acc[...] = part

        @pl.when(j > 0)
        def _():
            acc[...] += part

        @pl.when(j == nd - 1)
        def _():
            x = h_ref[...]
            r = lax.rsqrt(jnp.mean(x * x, axis=-1, keepdims=True) + EPS_RMS)
            dy = acc[...]
            gd = dy * g_ref[...]
            m = jnp.mean(gd * x, axis=-1, keepdims=True)
            dh_ref[...] = dho_ref[...] + r * gd - x * (r * r * r * m)
            dgp = jnp.sum(dy * x * r, axis=0, keepdims=True)

            @pl.when(i == 0)
            def _():
                dg_ref[...] = dgp

            @pl.when(i > 0)
            def _():
                dg_ref[...] += dgp

    return pl.pallas_call(
        body, name="in_proj_bwd_x", grid=(nt, nd),
        in_specs=[pl.BlockSpec((tm, nb), lambda i, j: (i, j)), pl.BlockSpec((1, d, nb), lambda i, j: (j, 0, 0)),
                  pl.BlockSpec((tm, d), lambda i, j: (i, 0)), _full((1, d)), pl.BlockSpec((tm, d), lambda i, j: (i, 0))],
        out_specs=[pl.BlockSpec((tm, d), lambda i, j: (i, 0)), _full((1, d))],
        out_shape=[jax.ShapeDtypeStruct((t, d), F32), jax.ShapeDtypeStruct((1, d), F32)],
        scratch_shapes=[pltpu.VMEM((tm, d), F32)],
        compiler_params=_params("arbitrary", "arbitrary"),
    )(dp, w, h, g, dh_out)


def _in_proj_bwd_w(hn, dp, nd, *, tm):
    t, d = hn.shape
    nb = dp.shape[1] // nd
    nt = t // tm

    def body(hn_ref, dp_ref, dw_ref, acc):
        i = pl.program_id(1)
        part = lax.dot_general(hn_ref[...], dp_ref[...], (((0,), (0,)), ((), ())), preferred_element_type=F32)

        @pl.when(i == 0)
        def _():
            acc[...] = part

        @pl.when(i > 0)
        def _():
            acc[...] += part

        @pl.when(i == nt - 1)
        def _():
            dw_ref[0] = acc[...].astype(BF16)

    return pl.pallas_call(
        body, name="in_proj_bwd_w", grid=(nd, nt),
        in_specs=[pl.BlockSpec((tm, d), lambda j, i: (i, 0)), pl.BlockSpec((tm, nb), lambda j, i: (i, j))],
        out_specs=pl.BlockSpec((1, d, nb), lambda j, i: (j, 0, 0)),
        out_shape=jax.ShapeDtypeStruct((nd, d, nb), BF16),
        scratch_shapes=[pltpu.VMEM((d, nb), F32)],
        compiler_params=_params("arbitrary", "arbitrary"),
    )(hn, dp)


def _loss_head(h, g, target, *, tm):
    t, d = h.shape
    nt = t // tm

    def body(h_ref, g_ref, t_ref, loss_ref, dh_ref, dg_ref):
        i = pl.program_id(0)
        x = h_ref[...]
        r = lax.rsqrt(jnp.mean(x * x, axis=-1, keepdims=True) + EPS_RMS)
        xr = x * r
        err = xr * g_ref[...] - t_ref[...]
        lp = 0.5 * jnp.sum(jnp.mean(err * err, axis=-1, keepdims=True), axis=0, keepdims=True)
        dy = err * (1.0 / d)
        gd = dy * g_ref[...]
        m = jnp.mean(gd * x, axis=-1, keepdims=True)
        dh_ref[...] = r * gd - x * (r * r * r * m)
        dgp = jnp.sum(dy * xr, axis=0, keepdims=True)

        @pl.when(i == 0)
        def _():
            loss_ref[...] = lp
            dg_ref[...] = dgp

        @pl.when(i > 0)
        def _():
            loss_ref[...] += lp
            dg_ref[...] += dgp

    return pl.pallas_call(
        body, name="loss_head", grid=(nt,),
        in_specs=[pl.BlockSpec((tm, d), lambda i: (i, 0)), _full((1, d)), pl.BlockSpec((tm, d), lambda i: (i, 0))],
        out_specs=[_full((1, 1)), pl.BlockSpec((tm, d), lambda i: (i, 0)), _full((1, d))],
        out_shape=[jax.ShapeDtypeStruct((1, 1), F32), jax.ShapeDtypeStruct((t, d), F32),
                   jax.ShapeDtypeStruct((1, d), F32)],
        compiler_params=_params("arbitrary"),
    )(h, g, target)


def _col(tm, w, c):
    return pl.BlockSpec((tm, w), lambda i: (i, c))


def _prev_halo(tm, rows, w, c):
    per = tm // rows
    return pl.BlockSpec((rows, w), lambda i: (jnp.maximum(i * per - 1, 0), c))


def _next_halo(tm, rows, w, c, t):
    per = tm // rows
    last = t // rows - 1
    return pl.BlockSpec((rows, w), lambda i: (jnp.minimum((i + 1) * per, last), c))


def _inv_count(first_row, rows, window):
    tpos = first_row + lax.broadcasted_iota(jnp.int32, (rows, 1), 0)
    return 1.0 / jnp.minimum(tpos + 1, window).astype(F32)


def _even_fwd(p, cw, cb, lg, lb, pw, pb, ps, *, tm):
    t = p.shape[0]
    wc = W_CONV

    def body(av, ag, agate, bv, bgate, avh, agh, bvh, cw_ref, cb_ref, lg_ref, lb_ref, pw_ref, pb_ref, ps_ref,
             y_ref, u1_ref, e_ref, d_ref, uext, vext):
        i = pl.program_id(0)
        keep = (i > 0).astype(F32)
        uext[0:HALO, :] = keep * (avh[...] * _sigmoid(agh[...]))
        uext[HALO:, :] = av[...] * _sigmoid(ag[...])
        vext[0:HALO, :] = keep * bvh[...]
        vext[HALO:, :] = bv[...]
        for c in range(0, wc, 128):
            acc = jnp.broadcast_to(cb_ref[:, c:c + 128], (tm, 128))
            for k in range(CONV_K):
                acc = acc + cw_ref[k:k + 1, c:c + 128] * uext[pl.ds(HALO - (CONV_K - 1) + k, tm), c:c + 128]
            u1_ref[:, c:c + 128] = acc
        u1 = u1_ref[...]
        mu = jnp.mean(u1, axis=-1, keepdims=True)
        xc = u1 - mu
        rs = lax.rsqrt(jnp.mean(xc * xc, axis=-1, keepdims=True) + EPS_LN)
        u2 = (xc * rs) * lg_ref[...] + lb_ref[...]
        u3 = u2 * _sigmoid(u2)
        ga = agate[...]
        y_ref[:, 0:wc] = (u3 * (ga * _sigmoid(ga))).astype(BF16)
        for g, win in enumerate(POOL_WINDOWS):
            cs = slice(g * POOL_GW, (g + 1) * POOL_GW)
            s = vext[pl.ds(HALO, tm), cs]
            for j in range(1, win):
                s = s + vext[pl.ds(HALO - j, tm), cs]
            dg = s * _inv_count(i * tm, tm, win) - vext[pl.ds(HALO, tm), cs]
            dgb = dg.astype(BF16)
            d_ref[:, cs] = dgb
            eg = jnp.dot(dgb, pw_ref[g], preferred_element_type=F32) + pb_ref[:, cs]
            e_ref[:, cs] = eg
            gb = bgate[:, cs]
            y_ref[:, wc + g * POOL_GW:wc + (g + 1) * POOL_GW] = ((eg * ps_ref[:, cs]) * (gb * _sigmoid(gb))).astype(BF16)

    row = lambda w: pl.BlockSpec((tm, w), lambda i: (i, 0))
    return pl.pallas_call(
        body, name="even_fwd", grid=(t // tm,),
        in_specs=[_col(tm, wc, 0), _col(tm, wc, 1), _col(tm, wc, 2), _col(tm, wc, 3), _col(tm, wc, 4),
                  _prev_halo(tm, HALO, wc, 0), _prev_halo(tm, HALO, wc, 1), _prev_halo(tm, HALO, wc, 3),
                  _full((32, wc)), _full((1, wc)), _full((1, wc)), _full((1, wc)),
                  _full((4, POOL_GW, POOL_GW)), _full((1, wc)), _full((1, wc))],
        out_specs=[row(2 * wc), row(wc), row(wc), row(wc)],
        out_shape=[jax.ShapeDtypeStruct((t, 2 * wc), BF16), jax.ShapeDtypeStruct((t, wc), F32),
                   jax.ShapeDtypeStruct((t, wc), F32), jax.ShapeDtypeStruct((t, wc), BF16)],
        scratch_shapes=[pltpu.VMEM((tm + HALO, wc), F32), pltpu.VMEM((tm + HALO, wc), F32)],
        compiler_params=_params("arbitrary"),
    )(p, p, p, p, p, p, p, p, cw, cb, lg, lb, pw, pb, ps)


def _acc_out(i, ref, val):
    @pl.when(i == 0)
    def _():
        ref[...] = val

    @pl.when(i > 0)
    def _():
        ref[...] += val


def _even_bwd_a(p, u1, e, dmat, dy, lg, lb, pw, ps, *, tm):
    t = p.shape[0]
    wc = W_CONV

    def body(agate, bgate, u1_ref, e_ref, d_ref, dya, dyb, lg_ref, lb_ref, pw_ref, ps_ref,
             du1_ref, dd_ref, dgat_ref, dlg_ref, dlb_ref, dpb_ref, dps_ref, dpw_ref):
        i = pl.program_id(0)

        @pl.when(i == 0)
        def _():
            dpw_ref[...] = jnp.zeros_like(dpw_ref)

        u1 = u1_ref[...]
        mu = jnp.mean(u1, axis=-1, keepdims=True)
        xc = u1 - mu
        rs = lax.rsqrt(jnp.mean(xc * xc, axis=-1, keepdims=True) + EPS_LN)
        xh = xc * rs
        u2 = xh * lg_ref[...] + lb_ref[...]
        s2 = _sigmoid(u2)
        ga = agate[...]
        sa = _sigmoid(ga)
        dy_a = dya[...]
        dgat_ref[:, 0:wc] = (dy_a * (u2 * s2) * _dsilu(ga, sa)).astype(BF16)
        du2 = dy_a * (ga * sa) * _dsilu(u2, s2)
        _acc_out(i, dlg_ref, jnp.sum(du2 * xh, axis=0, keepdims=True))
        _acc_out(i, dlb_ref, jnp.sum(du2, axis=0, keepdims=True))
        dxh = du2 * lg_ref[...]
        m1 = jnp.mean(dxh, axis=-1, keepdims=True)
        m2 = jnp.mean(dxh * xh, axis=-1, keepdims=True)
        du1_ref[...] = rs * (dxh - m1 - xh * m2)

        gb = bgate[...]
        sb = _sigmoid(gb)
        ev = e_ref[...]
        dy_b = dyb[...]
        dgat_ref[:, wc:2 * wc] = (dy_b * (ev * ps_ref[...]) * _dsilu(gb, sb)).astype(BF16)
        dz = dy_b * (gb * sb)
        _acc_out(i, dps_ref, jnp.sum(dz * ev, axis=0, keepdims=True))
        de = dz * ps_ref[...]
        _acc_out(i, dpb_ref, jnp.sum(de, axis=0, keepdims=True))
        for g in range(len(POOL_WINDOWS)):
            cs = slice(g * POOL_GW, (g + 1) * POOL_GW)
            deg = de[:, cs].astype(BF16)
            dd_ref[:, cs] = lax.dot_general(deg, pw_ref[g], (((1,), (1,)), ((), ())), preferred_element_type=F32)
            dpw_ref[g] += lax.dot_general(d_ref[:, cs], deg, (((0,), (0,)), ((), ())), preferred_element_type=F32)

    row = lambda w: pl.BlockSpec((tm, w), lambda i: (i, 0))
    return pl.pallas_call(
        body, name="even_bwd_a", grid=(t // tm,),
        in_specs=[_col(tm, wc, 2), _col(tm, wc, 4), row(wc), row(wc), row(wc), _col(tm, wc, 0), _col(tm, wc, 1),
                  _full((1, wc)), _full((1, wc)), _full((4, POOL_GW, POOL_GW)), _full((1, wc))],
        out_specs=[row(wc), row(wc), row(2 * wc), _full((1, wc)), _full((1, wc)), _full((1, wc)), _full((1, wc)),
                   _full((4, POOL_GW, POOL_GW))],
        out_shape=[jax.ShapeDtypeStruct((t, wc), F32), jax.ShapeDtypeStruct((t, wc), F32),
                   jax.ShapeDtypeStruct((t, 2 * wc), BF16)] + [jax.ShapeDtypeStruct((1, wc), F32)] * 4
                  + [jax.ShapeDtypeStruct((4, POOL_GW, POOL_GW), F32)],
        compiler_params=_params("arbitrary"),
    )(p, p, u1, e, dmat, dy, dy, lg, lb, pw, ps)


def _even_bwd_b(p, du1, dd, dgat, cw, *, tm):
    t = p.shape[0]
    wc = W_CONV
    nt = t // tm

    def body(av, ag, avh, agh, du1_ref, du1n, dd_ref, ddn, dgat_ref, cw_ref, dp_ref, dcw_ref, dcb_ref,
             uext, gext, dext, du0):
        i = pl.program_id(0)
        keep_p = (i > 0).astype(F32)
        keep_n = (i < nt - 1).astype(F32)

        @pl.when(i == 0)
        def _():
            dcw_ref[...] = jnp.zeros_like(dcw_ref)

        a = av[...]
        sg = _sigmoid(ag[...])
        uext[0:HALO, :] = keep_p * (avh[...] * _sigmoid(agh[...]))
        uext[HALO:, :] = a * sg
        gext[0:tm, :] = du1_ref[...]
        gext[tm:, :] = keep_n * du1n[...]
        for c in range(0, wc, 128):
            acc = jnp.zeros((tm, 128), F32)
            for s in range(CONV_K):
                k = CONV_K - 1 - s
                acc = acc + cw_ref[k:k + 1, c:c + 128] * gext[pl.ds(s, tm), c:c + 128]
            du0[:, c:c + 128] = acc
            gcur = du1_ref[:, c:c + 128]
            for k in range(CONV_K):
                dcw_ref[k:k + 1, c:c + 128] += jnp.sum(
                    gcur * uext[pl.ds(HALO - (CONV_K - 1) + k, tm), c:c + 128], axis=0, keepdims=True)

        _acc_out(i, dcb_ref, jnp.sum(du1_ref[...], axis=0, keepdims=True))
        g0 = du0[...]
        dp_ref[:, 0:wc] = (g0 * sg).astype(BF16)
        dp_ref[:, wc:2 * wc] = (g0 * a * sg * (1.0 - sg)).astype(BF16)
        dp_ref[:, 2 * wc:3 * wc] = dgat_ref[:, 0:wc]
        dp_ref[:, 4 * wc:5 * wc] = dgat_ref[:, wc:2 * wc]
        for g, win in enumerate(POOL_WINDOWS):
            cs = slice(g * POOL_GW, (g + 1) * POOL_GW)
            dext[0:tm, cs] = dd_ref[:, cs] * _inv_count(i * tm, tm, win)
            dext[tm:, cs] = keep_n * (ddn[:, cs] * _inv_count((i + 1) * tm, HALO, win))
            s = dext[pl.ds(0, tm), cs]
            for j in range(1, win):
                s = s + dext[pl.ds(j, tm), cs]
            dp_ref[:, 3 * wc + g * POOL_GW:3 * wc + (g + 1) * POOL_GW] = (s - dd_ref[:, cs]).astype(BF16)

    row = lambda w: pl.BlockSpec((tm, w), lambda i: (i, 0))
    return pl.pallas_call(
        body, name="even_bwd_b", grid=(nt,),
        in_specs=[_col(tm, wc, 0), _col(tm, wc, 1), _prev_halo(tm, HALO, wc, 0), _prev_halo(tm, HALO, wc, 1),
                  row(wc), _next_halo(tm, HALO, wc, 0, t), row(wc), _next_halo(tm, HALO, wc, 0, t), row(2 * wc),
                  _full((32, wc))],
        out_specs=[row(5 * wc), _full((32, wc)), _full((1, wc))],
        out_shape=[jax.ShapeDtypeStruct((t, 5 * wc), BF16), jax.ShapeDtypeStruct((32, wc), F32),
                   jax.ShapeDtypeStruct((1, wc), F32)],
        scratch_shapes=[pltpu.VMEM((tm + HALO, wc), F32), pltpu.VMEM((tm + HALO, wc), F32),
                        pltpu.VMEM((tm + HALO, wc), F32), pltpu.VMEM((tm, wc), F32)],
        compiler_params=_params("arbitrary"),
    )(p, p, p, p, du1, du1, dd, dd, dgat, cw)


def _softplus_neg(lam):
    z = -lam
    return jnp.maximum(z, 0.0) + jnp.log1p(jnp.exp(-jnp.abs(z)))


def _one_minus_exp(x):
    series = -x * (1.0 + x * (0.5 + x * (1.0 / 6.0 + x * (1.0 / 24.0))))
    return jnp.where(x > -0.02, series, 1.0 - jnp.exp(x))


def _odd_fwd(p, ccw, ccb, wrg, brg, wig, big, lam, *, tm):
    t = p.shape[0]
    wl = W_LRU
    ng = tm // 8

    def body(xr, gate, xrh, ccw_ref, ccb_ref, wrg_ref, brg_ref, wig_ref, big_ref, lam_ref,
             y_ref, xc_ref, r_ref, i_ref, hs_ref, xext, a_s, b_s, carry):
        i = pl.program_id(0)
        keep = (i > 0).astype(F32)
        xext[0:HALO_C, :] = keep * xrh[...]
        xext[HALO_C:, :] = xr[...]
        xc = jnp.broadcast_to(ccb_ref[...], (tm, wl))
        for k in range(LRU_CONV_K):
            xc = xc + ccw_ref[k:k + 1, :] * xext[pl.ds(HALO_C - (LRU_CONV_K - 1) + k, tm), :]
        xc_ref[...] = xc
        for h in range(LRU_HEADS):
            cs = slice(h * LRU_HD, (h + 1) * LRU_HD)
            xh = xc_ref[:, cs].astype(BF16)
            r_ref[:, cs] = _sigmoid(jnp.dot(xh, wrg_ref[h], preferred_element_type=F32) + brg_ref[:, cs])
            i_ref[:, cs] = _sigmoid(jnp.dot(xh, wig_ref[h], preferred_element_type=F32) + big_ref[:, cs])
        log_a = (-LRU_C * _softplus_neg(lam_ref[...])) * r_ref[...]
        a_s[...] = jnp.exp(log_a)
        b_s[...] = jnp.sqrt(_one_minus_exp(2.0 * log_a)) * (i_ref[...] * xc_ref[...])

        @pl.when(i == 0)
        def _():
            carry[...] = jnp.zeros_like(carry)

        rowi = lax.broadcasted_iota(jnp.int32, (8, wl), 0)

        def step(g, c):
            sl = pl.ds(pl.multiple_of(g * 8, 8), 8)
            aa, bb = a_s[sl, :], b_s[sl, :]
            for s in (1, 2, 4):
                m = rowi >= s
                a_sh = jnp.where(m, pltpu.roll(aa, s, 0), 1.0)
                b_sh = jnp.where(m, pltpu.roll(bb, s, 0), 0.0)
                bb = aa * b_sh + bb
                aa = aa * a_sh
            hv = bb + aa * c
            hs_ref[sl, :] = hv
            return hv[7:8, :]

        carry[...] = lax.fori_loop(0, ng, step, carry[...])
        gt = gate[...]
        y_ref[...] = (hs_ref[...] * (gt * _sigmoid(gt))).astype(BF16)

    row = lambda w: pl.BlockSpec((tm, w), lambda i: (i, 0))
    return pl.pallas_call(
        body, name="odd_fwd", grid=(t // tm,),
        in_specs=[_col(tm, wl, 0), _col(tm, wl, 1), _prev_halo(tm, HALO_C, wl, 0), _full((8, wl)), _full((1, wl)),
                  _full((LRU_HEADS, LRU_HD, LRU_HD)), _full((1, wl)), _full((LRU_HEADS, LRU_HD, LRU_HD)),
                  _full((1, wl)), _full((1, wl))],
        out_specs=[row(wl)] * 5,
        out_shape=[jax.ShapeDtypeStruct((t, wl), BF16)] + [jax.ShapeDtypeStruct((t, wl), F32)] * 4,
        scratch_shapes=[pltpu.VMEM((tm + HALO_C, wl), F32), pltpu.VMEM((tm, wl), F32), pltpu.VMEM((tm, wl), F32),
                        pltpu.VMEM((1, wl), F32)],
        compiler_params=_params("arbitrary"),
    )(p, p, p, ccw, ccb, wrg, brg, wig, big, lam)


def _odd_bwd_a(p, xc, r, ig, hs, dy, wrg, wig, lam, *, tm):
    t = p.shape[0]
    wl = W_LRU
    nt = t // tm
    ng = tm // 8
    per = tm // HALO_C

    def body(gate, xc_ref, r_ref, i_ref, hs_ref, hsh, dy_ref, wrg_ref, wig_ref, lam_ref,
             dxc_ref, dgate_ref, dwrg_ref, dwig_ref, dbrg_ref, dbig_ref, dlam_ref,
             hext, a_s, q_s, g_s, dpr_s, dpi_s, carry):
        i = pl.program_id(0)
        ti = nt - 1 - i
        keep = (ti > 0).astype(F32)
        hext[0:HALO_C, :] = keep * hsh[...]
        hext[HALO_C:, :] = hs_ref[...]
        gt = gate[...]
        sg = _sigmoid(gt)
        dyv = dy_ref[...]
        dgate_ref[...] = (dyv * hs_ref[...] * _dsilu(gt, sg)).astype(BF16)
        q_s[...] = dyv * (gt * sg)
        sp = _softplus_neg(lam_ref[...])
        log_a = (-LRU_C * sp) * r_ref[...]
        a_s[...] = jnp.exp(log_a)

        @pl.when(i == 0)
        def _():
            carry[...] = jnp.zeros_like(carry)
            dwrg_ref[...] = jnp.zeros_like(dwrg_ref)
            dwig_ref[...] = jnp.zeros_like(dwig_ref)

        rowi = lax.broadcasted_iota(jnp.int32, (8, wl), 0)

        def step(gr, c):
            sl = pl.ds(pl.multiple_of((ng - 1 - gr) * 8, 8), 8)
            a0 = a_s[sl, :]
            al = jnp.where(rowi < 7, pltpu.roll(a0, 7, 0), 1.0)
            be = q_s[sl, :]
            for s in (1, 2, 4):
                m = rowi + s <= 7
                al_sh = jnp.where(m, pltpu.roll(al, 8 - s, 0), 1.0)
                be_sh = jnp.where(m, pltpu.roll(be, 8 - s, 0), 0.0)
                be = be + al * be_sh
                al = al * al_sh
            gv = be + al * c
            g_s[sl, :] = gv
            return (a0 * gv)[0:1, :]

        carry[...] = lax.fori_loop(0, ng, step, carry[...])

        gv = g_s[...]
        a = a_s[...]
        mult = jnp.sqrt(_one_minus_exp(2.0 * log_a))
        iv = i_ref[...]
        rv = r_ref[...]
        xcv = xc_ref[...]
        hprev = hext[pl.ds(HALO_C - 1, tm), :]
        dla = gv * hprev * a - (gv * iv * xcv) * (a * a) / mult
        di = gv * mult * xcv
        dpr = (dla * (-LRU_C * sp)) * rv * (1.0 - rv)
        dpi = di * iv * (1.0 - iv)
        dpr_s[...] = dpr
        dpi_s[...] = dpi
        dxc_ref[...] = gv * mult * iv
        dsp = jnp.sum(dla * rv, axis=0, keepdims=True) * (-LRU_C)
        _acc_out(i, dlam_ref, -dsp * _sigmoid(-lam_ref[...]))
        _acc_out(i, dbrg_ref, jnp.sum(dpr, axis=0, keepdims=True))
        _acc_out(i, dbig_ref, jnp.sum(dpi, axis=0, keepdims=True))
        for h in range(LRU_HEADS):
            cs = slice(h * LRU_HD, (h + 1) * LRU_HD)
            xh = xc_ref[:, cs].astype(BF16)
            dr_h = dpr_s[:, cs].astype(BF16)
            di_h = dpi_s[:, cs].astype(BF16)
            dxc_ref[:, cs] += (
                lax.dot_general(dr_h, wrg_ref[h], (((1,), (1,)), ((), ())), preferred_element_type=F32)
                + lax.dot_general(di_h, wig_ref[h], (((1,), (1,)), ((), ())), preferred_element_type=F32))
            dwrg_ref[h] += lax.dot_general(xh, dr_h, (((0,), (0,)), ((), ())), preferred_element_type=F32)
            dwig_ref[h] += lax.dot_general(xh, di_h, (((0,), (0,)), ((), ())), preferred_element_type=F32)

    rrow = lambda w: pl.BlockSpec((tm, w), lambda i: (nt - 1 - i, 0))
    hspec = pl.BlockSpec((HALO_C, wl), lambda i: (jnp.maximum((nt - 1 - i) * per - 1, 0), 0))
    wspec = _full((LRU_HEADS, LRU_HD, LRU_HD))
    return pl.pallas_call(
        body, name="odd_bwd_a", grid=(nt,),
        in_specs=[pl.BlockSpec((tm, wl), lambda i: (nt - 1 - i, 1)), rrow(wl), rrow(wl), rrow(wl), rrow(wl), hspec,
                  rrow(wl), wspec, wspec, _full((1, wl))],
        out_specs=[rrow(wl), rrow(wl), wspec, wspec, _full((1, wl)), _full((1, wl)), _full((1, wl))],
        out_shape=[jax.ShapeDtypeStruct((t, wl), F32), jax.ShapeDtypeStruct((t, wl), BF16),
                   jax.ShapeDtypeStruct((LRU_HEADS, LRU_HD, LRU_HD), F32),
                   jax.ShapeDtypeStruct((LRU_HEADS, LRU_HD, LRU_HD), F32)] + [jax.ShapeDtypeStruct((1, wl), F32)] * 3,
        scratch_shapes=[pltpu.VMEM((tm + HALO_C, wl), F32)] + [pltpu.VMEM((tm, wl), F32)] * 5
                       + [pltpu.VMEM((1, wl), F32)],
        compiler_params=_params("arbitrary"),
    )(p, xc, r, ig, hs, hs, dy, wrg, wig, lam)


def _odd_bwd_b(p, dxc, dgate, ccw, *, tm):
    t = p.shape[0]
    wl = W_LRU
    nt = t // tm

    def body(xr, xrh, dxc_ref, dxcn, dgate_ref, ccw_ref, dp_ref, dcw_ref, dcb_ref, xext, gext):
        i = pl.program_id(0)

        @pl.when(i == 0)
        def _():
            dcw_ref[...] = jnp.zeros_like(dcw_ref)

        xext[0:HALO_C, :] = (i > 0).astype(F32) * xrh[...]
        xext[HALO_C:, :] = xr[...]
        gext[0:tm, :] = dxc_ref[...]
        gext[tm:, :] = (i < nt - 1).astype(F32) * dxcn[...]
        g = dxc_ref[...]
        acc = jnp.zeros((tm, wl), F32)
        for k in range(LRU_CONV_K):
            acc = acc + ccw_ref[k:k + 1, :] * gext[pl.ds(LRU_CONV_K - 1 - k, tm), :]
            dcw_ref[k:k + 1, :] += jnp.sum(
                g * xext[pl.ds(HALO_C - (LRU_CONV_K - 1) + k, tm), :], axis=0, keepdims=True)

        _acc_out(i, dcb_ref, jnp.sum(g, axis=0, keepdims=True))
        dp_ref[:, 0:wl] = acc.astype(BF16)
        dp_ref[:, wl:2 * wl] = dgate_ref[...]

    row = lambda w: pl.BlockSpec((tm, w), lambda i: (i, 0))
    return pl.pallas_call(
        body, name="odd_bwd_b", grid=(nt,),
        in_specs=[_col(tm, wl, 0), _prev_halo(tm, HALO_C, wl, 0), row(wl), _next_halo(tm, HALO_C, wl, 0, t), row(wl),
                  _full((8, wl))],
        out_specs=[row(2 * wl), _full((8, wl)), _full((1, wl))],
        out_shape=[jax.ShapeDtypeStruct((t, 2 * wl), BF16), jax.ShapeDtypeStruct((8, wl), F32),
                   jax.ShapeDtypeStruct((1, wl), F32)],
        scratch_shapes=[pltpu.VMEM((tm + HALO_C, wl), F32), pltpu.VMEM((tm + HALO_C, wl), F32)],
        compiler_params=_params("arbitrary"),
    )(p, p, dxc, dxc, dgate, ccw)


def _local_step(x, target, wts, *, tm_mm, tm_mix):
    n_even, n_odd = 2, 2
    h = x
    saved = []
    for layer in range(n_even + n_odd):
        j = layer // 2
        if layer % 2 == 0:
            p, hn = _norm_matmul(h, wts["norm_even"][j], wts["w_in_even"][j], tm=tm_mm)
            y, u1, e, dmat = _even_fwd(p, wts["conv_a_w"][j], wts["conv_a_b"][j], wts["ln_a_g"][j], wts["ln_a_b"][j],
                                       wts["pool_w"][j], wts["pool_b"][j], wts["pool_scale"][j], tm=tm_mix)
            saved.append((h, p, hn, y, u1, e, dmat))
            h = _out_proj(h, y, wts["w_out_even"][j], tm=tm_mm)
        else:
            p, hn = _norm_matmul(h, wts["norm_odd"][j], wts["w_in_odd"][j], tm=tm_mm)
            y, xc, r, ig, hs = _odd_fwd(p, wts["conv_c_w"][j], wts["conv_c_b"][j], wts["w_rg"][j], wts["b_rg"][j],
                                        wts["w_ig"][j], wts["b_ig"][j], wts["lru_lambda"][j], tm=tm_mix)
            saved.append((h, p, hn, y, xc, r, ig, hs))
            h = _out_proj(h, y, wts["w_out_odd"][j], tm=tm_mm)
    loss, dh, d_final = _loss_head(h, wts["final_norm"], target, tm=tm_mm)

    grads = {"final_norm": d_final}
    per_layer = {}
    for layer in reversed(range(n_even + n_odd)):
        j = layer // 2
        if layer % 2 == 0:
            h_in, p, hn, y, u1, e, dmat = saved[layer]
            dy, dw_out = _out_proj_bwd(dh, y, wts["w_out_even"][j], tm=tm_mm)
            du1, dd, dgat, dlg, dlb, dpb, dps, dpw = _even_bwd_a(
                p, u1, e, dmat, dy, wts["ln_a_g"][j], wts["ln_a_b"][j], wts["pool_w"][j], wts["pool_scale"][j],
                tm=tm_mix)
            dp, dcw, dcb = _even_bwd_b(p, du1, dd, dgat, wts["conv_a_w"][j], tm=tm_mix)
            dw_in = _in_proj_bwd_w(hn, dp, N_DEV, tm=tm_mm)
            dh, dg = _in_proj_bwd_x(dp, wts["w_in_even"][j], h_in, wts["norm_even"][j], dh, tm=tm_mm)
            per_layer[layer] = dict(norm_even=dg, w_in_even=dw_in, conv_a_w=dcw, conv_a_b=dcb, ln_a_g=dlg, ln_a_b=dlb,
                                    pool_w=dpw, pool_b=dpb, pool_scale=dps, w_out_even=dw_out)
        else:
            h_in, p, hn, y, xc, r, ig, hs = saved[layer]
            dy, dw_out = _out_proj_bwd(dh, y, wts["w_out_odd"][j], tm=tm_mm)
            dxc, dgate, dwrg, dwig, dbrg, dbig, dlam = _odd_bwd_a(
                p, xc, r, ig, hs, dy, wts["w_rg"][j], wts["w_ig"][j], wts["lru_lambda"][j], tm=tm_mix)
            dp, dccw, dccb = _odd_bwd_b(p, dxc, dgate, wts["conv_c_w"][j], tm=tm_mix)
            dw_in = _in_proj_bwd_w(hn, dp, N_DEV, tm=tm_mm)
            dh, dg = _in_proj_bwd_x(dp, wts["w_in_odd"][j], h_in, wts["norm_odd"][j], dh, tm=tm_mm)
            per_layer[layer] = dict(norm_odd=dg, w_in_odd=dw_in, conv_c_w=dccw, conv_c_b=dccb, w_rg=dwrg, b_rg=dbrg,
                                    w_ig=dwig, b_ig=dbig, lru_lambda=dlam, w_out_odd=dw_out)
    for name in per_layer[0]:
        grads[name] = [per_layer[0][name], per_layer[2][name]]
    for name in per_layer[1]:
        grads[name] = [per_layer[1][name], per_layer[3][name]]
    return loss, dh, grads


def _slot(px, py, pc):
    return 4 * px + 2 * py + pc


def _all_gather(arrs, name):
    n = len(arrs)

    def body(*refs):
        ins, outs = refs[:n], refs[n:2 * n]
        send_sems, recv_sems, local_sems = refs[2 * n:]
        x, y, c = lax.axis_index("x"), lax.axis_index("y"), lax.axis_index("c")
        me, sibling = (x, y, c), (x, y, 1 - c)
        chips = [(1 - x, y), (x, 1 - y), (1 - x, 1 - y)]

        def copy(a, k, block, to, src=None):
            rows = outs[a].at[_slot(*block)]
            return pltpu.make_async_remote_copy(
                src_ref=rows if src is None else src, dst_ref=rows, send_sem=send_sems.at[a, k],
                recv_sem=recv_sems.at[a, k], device_id=to, device_id_type=MESH)

        mine = [pltpu.make_async_copy(ins[a], outs[a].at[_slot(*me)], local_sems.at[a]) for a in range(n)]
        for cp in mine:
            cp.start()
        first = []
        for a in range(n):
            first.append(copy(a, 0, me, sibling, src=ins[a]))
            first += [copy(a, 1 + j, me, (*chip, c), src=ins[a]) for j, chip in enumerate(chips)]
        for cp in first:
            cp.start()
        passed = []
        for j, chip in enumerate(chips):
            for a in range(n):
                copy(a, 1 + j, (*chip, c), me).wait_recv()
                fwd = copy(a, 4 + j, (*chip, c), sibling)
                fwd.start()
                passed.append(fwd)
        for a in range(n):
            copy(a, 0, sibling, me).wait_recv()
            for j, chip in enumerate(chips):
                copy(a, 4 + j, (*chip, 1 - c), me).wait_recv()
        for cp in first + passed:
            cp.wait_send()
        for cp in mine:
            cp.wait()

    any_spec = pl.BlockSpec(memory_space=pl.ANY)
    return pl.pallas_call(
        body, name=name,
        in_specs=[any_spec] * n, out_specs=[any_spec] * n,
        out_shape=[jax.ShapeDtypeStruct((N_DEV,) + a.shape, a.dtype) for a in arrs],
        scratch_shapes=[pltpu.SemaphoreType.DMA((n, 7)), pltpu.SemaphoreType.DMA((n, 7)),
                        pltpu.SemaphoreType.DMA((n,))],
    )(*arrs)


def _all_to_all(arrs, name):
    n = len(arrs)

    def body(*refs):
        ins, outs = refs[:n], refs[n:2 * n]
        send_sems, recv_sems, local_sems = refs[2 * n:]
        x, y, c = lax.axis_index("x"), lax.axis_index("y"), lax.axis_index("c")
        me = _slot(x, y, c)
        mine = [pltpu.make_async_copy(ins[a].at[me], outs[a].at[me], local_sems.at[a]) for a in range(n)]
        for cp in mine:
            cp.start()
        sends, recvs = [], []
        for k in range(1, N_DEV):
            px = 1 - x if k & 4 else x
            py = 1 - y if k & 2 else y
            pc = 1 - c if k & 1 else c
            peer = _slot(px, py, pc)
            for a in range(n):
                sems = dict(send_sem=send_sems.at[a, k - 1], recv_sem=recv_sems.at[a, k - 1],
                            device_id=(px, py, pc), device_id_type=MESH)
                sends.append(pltpu.make_async_remote_copy(src_ref=ins[a].at[peer], dst_ref=outs[a].at[me], **sems))
                recvs.append(pltpu.make_async_remote_copy(src_ref=ins[a].at[peer], dst_ref=outs[a].at[peer], **sems))
        for cp in sends:
            cp.start()
        for cp in recvs:
            cp.wait_recv()
        for cp in sends:
            cp.wait_send()
        for cp in mine:
            cp.wait()

    any_spec = pl.BlockSpec(memory_space=pl.ANY)
    return pl.pallas_call(
        body, name=name,
        in_specs=[any_spec] * n, out_specs=[any_spec] * n,
        out_shape=[jax.ShapeDtypeStruct(a.shape, a.dtype) for a in arrs],
        scratch_shapes=[pltpu.SemaphoreType.DMA((n, 7)), pltpu.SemaphoreType.DMA((n, 7)),
                        pltpu.SemaphoreType.DMA((n,))],
    )(*arrs)


def _sum_parts(parts, *, tr):
    np_, r, c = parts.shape

    def body(p_ref, o_ref):
        acc = p_ref[0].astype(F32)
        for k in range(1, np_):
            acc = acc + p_ref[k].astype(F32)
        o_ref[...] = acc

    return pl.pallas_call(
        body, name="sum_parts", grid=(r // tr,),
        in_specs=[pl.BlockSpec((np_, tr, c), lambda i: (0, i, 0))],
        out_specs=pl.BlockSpec((tr, c), lambda i: (i, 0)),
        out_shape=jax.ShapeDtypeStruct((r, c), F32),
        compiler_params=_params("arbitrary"),
    )(parts)


def _adamw(w, parts, m, v):
    r, c = w.shape
    np_ = parts.shape[0]
    tr = r
    for cand in (256, 128, 64, 32, 16, 8):
        if r % cand == 0 and r > cand:
            tr = cand
            break
    c1 = 1.0 - ADAM_B1 ** ADAM_STEP
    c2 = 1.0 - ADAM_B2 ** ADAM_STEP

    def body(w_ref, p_ref, m_ref, v_ref, g_ref, d_ref, nm_ref, nv_ref):
        g = p_ref[0].astype(F32)
        for k in range(1, np_):
            g = g + p_ref[k].astype(F32)
        g_ref[...] = g
        nm = ADAM_B1 * m_ref[...] + (1.0 - ADAM_B1) * g
        nv = ADAM_B2 * v_ref[...] + (1.0 - ADAM_B2) * (g * g)
        nm_ref[...] = nm
        nv_ref[...] = nv
        d_ref[...] = -ADAM_LR * ((nm / c1) / (jnp.sqrt(nv / c2) + ADAM_EPS) + ADAM_WD * w_ref[...])

    blk = pl.BlockSpec((tr, c), lambda i: (i, 0))
    return pl.pallas_call(
        body, name="adamw", grid=(r // tr,),
        in_specs=[blk, pl.BlockSpec((np_, tr, c), lambda i: (0, i, 0)), blk, blk],
        out_specs=[blk] * 4,
        out_shape=[jax.ShapeDtypeStruct((r, c), F32)] * 4,
        compiler_params=_params("arbitrary"),
    )(w, parts, m, v)


BIG = ("w_in_even", "w_out_even", "w_in_odd", "w_out_odd")
SMALL_SHARDED = ("conv_a_w", "pool_b", "norm_odd", "conv_c_w", "conv_c_b", "b_rg", "b_ig", "lru_lambda")
REPLICATED = ("norm_even", "conv_a_b", "ln_a_g", "ln_a_b", "pool_scale", "w_rg", "w_ig", "final_norm")
PACK_ROW = 1024


def _pack(vecs):
    flat = jnp.concatenate([v.reshape(-1) for v in vecs])
    pad = (-flat.shape[0]) % PACK_ROW
    return jnp.pad(flat, (0, pad)).reshape(-1, 128)


def _unpack(flat, shapes):
    out, off = [], 0
    for s in shapes:
        n = 1
        for d in s:
            n *= d
        out.append(flat[off:off + n].reshape(s))
        off += n
    return out


def _to_global(name, g):
    if name in ("conv_a_w", "pool_b", "conv_c_w"):
        return jnp.transpose(g, (1, 2, 0, 3)).reshape(g.shape[1], g.shape[2], -1)
    if name == "pool_w":
        return jnp.transpose(g, (1, 2, 0, 3, 4)).reshape(2, 4, POOL_GW, POOL_GW)
    return jnp.transpose(g, (1, 0, 2)).reshape(g.shape[1], -1)


def _to_blocks(name, g):
    if name == "conv_a_w":
        return jnp.transpose(g.reshape(CONV_K, N_DEV, -1), (1, 0, 2))
    if name == "conv_c_w":
        return jnp.transpose(g.reshape(LRU_CONV_K, N_DEV, -1), (1, 0, 2))
    if name == "pool_b":
        return jnp.transpose(g.reshape(4, N_DEV, -1), (1, 0, 2))
    if name == "pool_w":
        return jnp.transpose(g.reshape(4, N_DEV, POOL_GW // N_DEV, POOL_GW), (1, 0, 2, 3))
    return g.reshape(N_DEV, -1)


def kernel(x, norm_even, w_in_even, conv_a_w, conv_a_b, ln_a_g, ln_a_b, pool_w, pool_b, pool_scale, w_out_even, norm_odd, w_in_odd, conv_c_w, conv_c_b, w_rg, b_rg, w_ig, b_ig, lru_lambda, w_out_odd, final_norm, loss_target, m_norm_even, m_w_in_even, m_conv_a_w, m_conv_a_b, m_ln_a_g, m_ln_a_b, m_pool_w, m_pool_b, m_pool_scale, m_w_out_even, m_norm_odd, m_w_in_odd, m_conv_c_w, m_conv_c_b, m_w_rg, m_b_rg, m_w_ig, m_b_ig, m_lru_lambda, m_w_out_odd, m_final_norm, v_norm_even, v_w_in_even, v_conv_a_w, v_conv_a_b, v_ln_a_g, v_ln_a_b, v_pool_w, v_pool_b, v_pool_scale, v_w_out_even, v_norm_odd, v_w_in_odd, v_conv_c_w, v_conv_c_b, v_w_rg, v_b_rg, v_w_ig, v_b_ig, v_lru_lambda, v_w_out_odd, v_final_norm):
    names = ["norm_even", "w_in_even", "conv_a_w", "conv_a_b", "ln_a_g", "ln_a_b", "pool_w", "pool_b", "pool_scale",
             "w_out_even", "norm_odd", "w_in_odd", "conv_c_w", "conv_c_b", "w_rg", "b_rg", "w_ig", "b_ig",
             "lru_lambda", "w_out_odd", "final_norm"]
    w_loc = dict(zip(names, [norm_even, w_in_even, conv_a_w, conv_a_b, ln_a_g, ln_a_b, pool_w, pool_b, pool_scale,
                             w_out_even, norm_odd, w_in_odd, conv_c_w, conv_c_b, w_rg, b_rg, w_ig, b_ig, lru_lambda,
                             w_out_odd, final_norm]))
    m_loc = dict(zip(names, [m_norm_even, m_w_in_even, m_conv_a_w, m_conv_a_b, m_ln_a_g, m_ln_a_b, m_pool_w, m_pool_b,
                             m_pool_scale, m_w_out_even, m_norm_odd, m_w_in_odd, m_conv_c_w, m_conv_c_b, m_w_rg,
                             m_b_rg, m_w_ig, m_b_ig, m_lru_lambda, m_w_out_odd, m_final_norm]))
    v_loc = dict(zip(names, [v_norm_even, v_w_in_even, v_conv_a_w, v_conv_a_b, v_ln_a_g, v_ln_a_b, v_pool_w, v_pool_b,
                             v_pool_scale, v_w_out_even, v_norm_odd, v_w_in_odd, v_conv_c_w, v_conv_c_b, v_w_rg,
                             v_b_rg, v_w_ig, v_b_ig, v_lru_lambda, v_w_out_odd, v_final_norm]))

    small_shapes = [w_loc[n].shape for n in SMALL_SHARDED]
    small = _pack([w_loc[n] for n in SMALL_SHARDED])
    gathered = _all_gather([w_loc[n].astype(BF16) for n in BIG] + [pool_w.astype(BF16), small], "gather_weights")
    g_big = dict(zip(BIG, gathered[:4]))
    g_small = gathered[5].reshape(N_DEV, -1)
    g_small = dict(zip(SMALL_SHARDED, [
        _to_global(n, jnp.stack(col)) for n, col in
        zip(SMALL_SHARDED, zip(*[_unpack(g_small[d], small_shapes) for d in range(N_DEV)]))]))
    pool_w_all = _to_global("pool_w", gathered[4])

    row = lambda a: [a[j][None] for j in range(2)]
    wts = {
        "norm_even": row(norm_even), "conv_a_b": row(conv_a_b), "ln_a_g": row(ln_a_g), "ln_a_b": row(ln_a_b),
        "pool_scale": row(pool_scale), "final_norm": final_norm[None],
        "w_rg": [w_rg[j].astype(BF16) for j in range(2)], "w_ig": [w_ig[j].astype(BF16) for j in range(2)],
        "w_in_even": [g_big["w_in_even"][:, j] for j in range(2)],
        "w_in_odd": [g_big["w_in_odd"][:, j] for j in range(2)],
        "w_out_even": [g_big["w_out_even"][:, j].reshape(W_EVEN_MIX, D_MODEL) for j in range(2)],
        "w_out_odd": [g_big["w_out_odd"][:, j].reshape(W_LRU, D_MODEL) for j in range(2)],
        "pool_w": [pool_w_all[j] for j in range(2)],
        "conv_a_w": [jnp.pad(g_small["conv_a_w"][j], ((0, 1), (0, 0))) for j in range(2)],
        "pool_b": [g_small["pool_b"][j].reshape(1, W_POOL) for j in range(2)],
        "norm_odd": row(g_small["norm_odd"]),
        "conv_c_w": [jnp.pad(g_small["conv_c_w"][j], ((0, 4), (0, 0))) for j in range(2)],
        "conv_c_b": row(g_small["conv_c_b"]), "b_rg": row(g_small["b_rg"]), "b_ig": row(g_small["b_ig"]),
        "lru_lambda": row(g_small["lru_lambda"]),
    }

    loss, grad_x, grads = _local_step(x[0], loss_target[0], wts, tm_mm=512, tm_mix=256)
    grads["conv_a_w"] = [g[:CONV_K] for g in grads["conv_a_w"]]
    grads["conv_c_w"] = [g[:LRU_CONV_K] for g in grads["conv_c_w"]]

    small_names = SMALL_SHARDED + ("pool_w",)
    small_blocks = [jnp.stack([_to_blocks(n, grads[n][j]) for j in range(2)], axis=1) for n in small_names]
    rep_flat = jnp.concatenate([jnp.stack(grads[n]).reshape(-1) if n != "final_norm" else grads[n].reshape(-1)
                                for n in REPLICATED])
    rep_len = rep_flat.shape[0]
    rep_pad = (-rep_len) % (N_DEV * PACK_ROW)
    rep_blocks = jnp.pad(rep_flat, (0, rep_pad)).reshape(N_DEV, -1)
    send_small = jnp.concatenate([b.reshape(N_DEV, -1) for b in small_blocks] + [rep_blocks], axis=1)
    pad = (-send_small.shape[1]) % PACK_ROW
    send_small = jnp.pad(send_small, ((0, 0), (0, pad))).reshape(N_DEV, -1, 128)
    send_big = [jnp.stack(grads["w_in_even"], axis=1),
                jnp.stack([g.reshape(N_DEV, -1, D_MODEL) for g in grads["w_out_even"]], axis=1),
                jnp.stack(grads["w_in_odd"], axis=1),
                jnp.stack([g.reshape(N_DEV, -1, D_MODEL) for g in grads["w_out_odd"]], axis=1)]
    got = _all_to_all(send_big + [send_small], "scatter_grads")
    got_big = dict(zip(BIG, got[:4]))
    got_small = got[4].reshape(N_DEV, -1)
    small_parts, off = {}, 0
    for n, b in zip(small_names, small_blocks):
        ln = b[0].size
        small_parts[n] = got_small[:, off:off + ln].reshape((N_DEV,) + b.shape[1:])
        off += ln
    rep_parts = got_small[:, off:off + rep_blocks.shape[1]].reshape(N_DEV, -1, 128)

    rep_mine = _sum_parts(rep_parts, tr=rep_parts.shape[1])
    rep_all = _all_gather([rep_mine], "gather_replicated")[0].reshape(-1)[:rep_len]
    rep_shapes = [w_loc[n].shape for n in REPLICATED]
    rep_grads = dict(zip(REPLICATED, _unpack(rep_all, rep_shapes)))

    def as2d(a):
        if a.ndim == 1:
            return a.reshape(1, -1)
        return a.reshape(-1, a.shape[-1])

    out_g, out_d, out_m, out_v = {}, {}, {}, {}
    for n in names:
        w2 = as2d(w_loc[n])
        if n in REPLICATED:
            parts = as2d(rep_grads[n])[None]
        elif n in BIG:
            parts = got_big[n].reshape((N_DEV,) + w2.shape)
        else:
            parts = small_parts[n].reshape((N_DEV,) + w2.shape)
        g, dlt, nm, nv = _adamw(w2, parts, as2d(m_loc[n]), as2d(v_loc[n]))
        shp = w_loc[n].shape
        out_g[n], out_d[n], out_m[n], out_v[n] = g.reshape(shp), dlt.reshape(shp), nm.reshape(shp), nv.reshape(shp)

    total = lax.psum(loss[0, 0], ("x", "y", "c"))
    return (total, grad_x[None], *[out_g[n] for n in names], *[out_d[n] for n in names],
            *[out_m[n] for n in names], *[out_v[n] for n in names])
```

```python
import functools

import jax
import jax.numpy as jnp
from jax import lax
from jax.experimental import pallas as pl
from jax.experimental.pallas import tpu as pltpu

F32 = jnp.float32
BF16 = jnp.bfloat16

N_DEV = 8
N_PEERS = N_DEV - 1
N_LAYERS = 4
D_MODEL = 1024
EPS_RMS = 1e-6
EPS_LN = 1e-5
W_CONV = 1024
CONV_K = 31
W_POOL = 1024
POOL_WINDOWS = (2, 4, 8, 16)
POOL_GW = 256
W_EVEN_IN = 5120
W_EVEN_MIX = 2048
LRU_HEADS = 12
LRU_HD = 128
W_LRU = 1536
LRU_CONV_K = 4
LRU_C = 8.0
ADAM_LR = 0.001
ADAM_B1 = 0.9
ADAM_B2 = 0.999
ADAM_EPS = 1e-08
ADAM_WD = 0.01
ADAM_STEP = 10

HALO = 32
HALO_C = 8
TM_MATMUL = 512
TM_MIXER = 256
VMEM_LIMIT = 56 * 1024 * 1024
MESH = pl.DeviceIdType.MESH
ANY = pl.BlockSpec(memory_space=pl.ANY)
HBM = pl.BlockSpec(memory_space=pltpu.HBM)
SEM = pl.BlockSpec(memory_space=pltpu.SEMAPHORE)


def _params(*sem):
    return pltpu.CompilerParams(dimension_semantics=sem, vmem_limit_bytes=VMEM_LIMIT)


def _sigmoid(z):
    return jax.nn.sigmoid(z)


def _dsilu(z, s):
    return s * (1.0 + z * (1.0 - s))


def _full(shape):
    nd = len(shape)
    return pl.BlockSpec(shape, lambda *_: (0,) * nd)


def _norm_matmul(h, g, w, *, tm):
    t, d = h.shape
    nd, _, nb = w.shape

    def body(h_ref, g_ref, w_ref, p_ref, hn_ref):
        @pl.when(pl.program_id(1) == 0)
        def _():
            x = h_ref[...]
            r = lax.rsqrt(jnp.mean(x * x, axis=-1, keepdims=True) + EPS_RMS)
            hn_ref[...] = ((x * r) * g_ref[...]).astype(BF16)

        p_ref[...] = jnp.dot(hn_ref[...], w_ref[0], preferred_element_type=F32)

    return pl.pallas_call(
        body, name="norm_matmul", grid=(t // tm, nd),
        in_specs=[pl.BlockSpec((tm, d), lambda i, j: (i, 0)), _full((1, d)),
                  pl.BlockSpec((1, d, nb), lambda i, j: (j, 0, 0))],
        out_specs=[pl.BlockSpec((tm, nb), lambda i, j: (i, j)), pl.BlockSpec((tm, d), lambda i, j: (i, 0))],
        out_shape=[jax.ShapeDtypeStruct((t, nd * nb), F32), jax.ShapeDtypeStruct((t, d), BF16)],
        compiler_params=_params("arbitrary", "arbitrary"),
    )(h, g, w)


def _out_proj(h, y, w, *, tm):
    t, d = h.shape
    k = y.shape[1]

    def body(h_ref, y_ref, w_ref, o_ref):
        o_ref[...] = h_ref[...] + jnp.dot(y_ref[...], w_ref[...], preferred_element_type=F32)

    return pl.pallas_call(
        body, name="out_proj", grid=(t // tm,),
        in_specs=[pl.BlockSpec((tm, d), lambda i: (i, 0)), pl.BlockSpec((tm, k), lambda i: (i, 0)), _full((k, d))],
        out_specs=pl.BlockSpec((tm, d), lambda i: (i, 0)),
        out_shape=jax.ShapeDtypeStruct((t, d), F32),
        compiler_params=_params("arbitrary"),
    )(h, y, w)


def _out_proj_bwd(dh, y, w, dep, *, tm):
    t, d = dh.shape
    k = y.shape[1]
    nt = t // tm

    def body(dh_ref, y_ref, w_ref, dep_ref, dy_ref, dw_ref, acc):
        i = pl.program_id(0)
        g = dh_ref[...].astype(BF16)
        dy_ref[...] = lax.dot_general(g, w_ref[...], (((1,), (1,)), ((), ())), preferred_element_type=F32)
        part = lax.dot_general(y_ref[...], g, (((0,), (0,)), ((), ())), preferred_element_type=F32)

        @pl.when(i == 0)
        def _():
            acc[...] = part

        @pl.when(i > 0)
        def _():
            acc[...] += part

        @pl.when(i == nt - 1)
        def _():
            dw_ref[...] = acc[...].astype(BF16)

    return pl.pallas_call(
        body, name="out_proj_bwd", grid=(nt,),
        in_specs=[pl.BlockSpec((tm, d), lambda i: (i, 0)), pl.BlockSpec((tm, k), lambda i: (i, 0)), _full((k, d)),
                  ANY],
        out_specs=[pl.BlockSpec((tm, k), lambda i: (i, 0)), _full((k, d))],
        out_shape=[jax.ShapeDtypeStruct((t, k), F32), jax.ShapeDtypeStruct((k, d), BF16)],
        scratch_shapes=[pltpu.VMEM((k, d), F32)],
        compiler_params=_params("arbitrary"),
    )(dh, y, w, dep)


def _in_proj_bwd_x(dp, w, h, g, dh_out, *, tm):
    t, d = h.shape
    nd, _, nb = w.shape
    nt = t // tm

    def body(dp_ref, w_ref, h_ref, g_ref, dho_ref, dh_ref, dg_ref, acc):
        i, j = pl.program_id(0), pl.program_id(1)
        part = lax.dot_general(dp_ref[...], w_ref[0], (((1,), (1,)), ((), ())), preferred_element_type=F32)

        @pl.when(j == 0)
        def _():
            acc[...] = part

        @pl.when(j > 0)
        def _():
            acc[...] += part

        @pl.when(j == nd - 1)
        def _():
            x = h_ref[...]
            r = lax.rsqrt(jnp.mean(x * x, axis=-1, keepdims=True) + EPS_RMS)
            dy = acc[...]
            gd = dy * g_ref[...]
            m = jnp.mean(gd * x, axis=-1, keepdims=True)
            dh_ref[...] = dho_ref[...] + r * gd - x * (r * r * r * m)
            dgp = jnp.sum(dy * x * r, axis=0, keepdims=True)

            @pl.when(i == 0)
            def _():
                dg_ref[...] = dgp

            @pl.when(i > 0)
            def _():
                dg_ref[...] += dgp

    return pl.pallas_call(
        body, name="in_proj_bwd_x", grid=(nt, nd),
        in_specs=[pl.BlockSpec((tm, nb), lambda i, j: (i, j)), pl.BlockSpec((1, d, nb), lambda i, j: (j, 0, 0)),
                  pl.BlockSpec((tm, d), lambda i, j: (i, 0)), _full((1, d)), pl.BlockSpec((tm, d), lambda i, j: (i, 0))],
        out_specs=[pl.BlockSpec((tm, d), lambda i, j: (i, 0)), _full((1, d))],
        out_shape=[jax.ShapeDtypeStruct((t, d), F32), jax.ShapeDtypeStruct((1, d), F32)],
        scratch_shapes=[pltpu.VMEM((tm, d), F32)],
        compiler_params=_params("arbitrary", "arbitrary"),
    )(dp, w, h, g, dh_out)


def _in_proj_bwd_w(hn, dp, nd, *, tm):
    t, d = hn.shape
    nb = dp.shape[1] // nd
    nt = t // tm

    def body(hn_ref, dp_ref, dw_ref, acc):
        i = pl.program_id(1)
        part = lax.dot_general(hn_ref[...], dp_ref[...], (((0,), (0,)), ((), ())), preferred_element_type=F32)

        @pl.when(i == 0)
        def _():
            acc[...] = part

        @pl.when(i > 0)
        def _():
            acc[...] += part

        @pl.when(i == nt - 1)
        def _():
            dw_ref[0] = acc[...].astype(BF16)

    return pl.pallas_call(
        body, name="in_proj_bwd_w", grid=(nd, nt),
        in_specs=[pl.BlockSpec((tm, d), lambda j, i: (i, 0)), pl.BlockSpec((tm, nb), lambda j, i: (i, j))],
        out_specs=pl.BlockSpec((1, d, nb), lambda j, i: (j, 0, 0)),
        out_shape=jax.ShapeDtypeStruct((nd, d, nb), BF16),
        scratch_shapes=[pltpu.VMEM((d, nb), F32)],
        compiler_params=_params("arbitrary", "arbitrary"),
    )(hn, dp)


def _loss_head(h, g, target, *, tm):
    t, d = h.shape
    nt = t // tm

    def body(h_ref, g_ref, t_ref, loss_ref, dh_ref, dg_ref):
        i = pl.program_id(0)
        x = h_ref[...]
        r = lax.rsqrt(jnp.mean(x * x, axis=-1, keepdims=True) + EPS_RMS)
        xr = x * r
        err = xr * g_ref[...] - t_ref[...]
        lp = 0.5 * jnp.sum(jnp.mean(err * err, axis=-1, keepdims=True), axis=0, keepdims=True)
        dy = err * (1.0 / d)
        gd = dy * g_ref[...]
        m = jnp.mean(gd * x, axis=-1, keepdims=True)
        dh_ref[...] = r * gd - x * (r * r * r * m)
        dgp = jnp.sum(dy * xr, axis=0, keepdims=True)

        @pl.when(i == 0)
        def _():
            loss_ref[...] = lp
            dg_ref[...] = dgp

        @pl.when(i > 0)
        def _():
            loss_ref[...] += lp
            dg_ref[...] += dgp

    return pl.pallas_call(
        body, name="loss_head", grid=(nt,),
        in_specs=[pl.BlockSpec((tm, d), lambda i: (i, 0)), _full((1, d)), pl.BlockSpec((tm, d), lambda i: (i, 0))],
        out_specs=[_full((1, 1)), pl.BlockSpec((tm, d), lambda i: (i, 0)), _full((1, d))],
        out_shape=[jax.ShapeDtypeStruct((1, 1), F32), jax.ShapeDtypeStruct((t, d), F32),
                   jax.ShapeDtypeStruct((1, d), F32)],
        compiler_params=_params("arbitrary"),
    )(h, g, target)


def _col(tm, w, c):
    return pl.BlockSpec((tm, w), lambda i: (i, c))


def _prev_halo(tm, rows, w, c):
    per = tm // rows
    return pl.BlockSpec((rows, w), lambda i: (jnp.maximum(i * per - 1, 0), c))


def _next_halo(tm, rows, w, c, t):
    per = tm // rows
    last = t // rows - 1
    return pl.BlockSpec((rows, w), lambda i: (jnp.minimum((i + 1) * per, last), c))


def _inv_count(first_row, rows, window):
    tpos = first_row + lax.broadcasted_iota(jnp.int32, (rows, 1), 0)
    return 1.0 / jnp.minimum(tpos + 1, window).astype(F32)


def _even_fwd(p, cw, cb, lg, lb, pw, pb, ps, *, tm):
    t = p.shape[0]
    wc = W_CONV

    def body(av, ag, agate, bv, bgate, avh, agh, bvh, cw_ref, cb_ref, lg_ref, lb_ref, pw_ref, pb_ref, ps_ref,
             y_ref, u1_ref, e_ref, d_ref, uext, vext):
        i = pl.program_id(0)
        keep = (i > 0).astype(F32)
        uext[0:HALO, :] = keep * (avh[...] * _sigmoid(agh[...]))
        uext[HALO:, :] = av[...] * _sigmoid(ag[...])
        vext[0:HALO, :] = keep * bvh[...]
        vext[HALO:, :] = bv[...]
        for c in range(0, wc, 128):
            acc = jnp.broadcast_to(cb_ref[:, c:c + 128], (tm, 128))
            for k in range(CONV_K):
                acc = acc + cw_ref[k:k + 1, c:c + 128] * uext[pl.ds(HALO - (CONV_K - 1) + k, tm), c:c + 128]
            u1_ref[:, c:c + 128] = acc
        u1 = u1_ref[...]
        mu = jnp.mean(u1, axis=-1, keepdims=True)
        xc = u1 - mu
        rs = lax.rsqrt(jnp.mean(xc * xc, axis=-1, keepdims=True) + EPS_LN)
        u2 = (xc * rs) * lg_ref[...] + lb_ref[...]
        u3 = u2 * _sigmoid(u2)
        ga = agate[...]
        y_ref[:, 0:wc] = (u3 * (ga * _sigmoid(ga))).astype(BF16)
        for g, win in enumerate(POOL_WINDOWS):
            cs = slice(g * POOL_GW, (g + 1) * POOL_GW)
            s = vext[pl.ds(HALO, tm), cs]
            for j in range(1, win):
                s = s + vext[pl.ds(HALO - j, tm), cs]
            dg = s * _inv_count(i * tm, tm, win) - vext[pl.ds(HALO, tm), cs]
            dgb = dg.astype(BF16)
            d_ref[:, cs] = dgb
            eg = jnp.dot(dgb, pw_ref[g], preferred_element_type=F32) + pb_ref[:, cs]
            e_ref[:, cs] = eg
            gb = bgate[:, cs]
            y_ref[:, wc + g * POOL_GW:wc + (g + 1) * POOL_GW] = ((eg * ps_ref[:, cs]) * (gb * _sigmoid(gb))).astype(BF16)

    row = lambda w: pl.BlockSpec((tm, w), lambda i: (i, 0))
    return pl.pallas_call(
        body, name="even_fwd", grid=(t // tm,),
        in_specs=[_col(tm, wc, 0), _col(tm, wc, 1), _col(tm, wc, 2), _col(tm, wc, 3), _col(tm, wc, 4),
                  _prev_halo(tm, HALO, wc, 0), _prev_halo(tm, HALO, wc, 1), _prev_halo(tm, HALO, wc, 3),
                  _full((32, wc)), _full((1, wc)), _full((1, wc)), _full((1, wc)),
                  _full((4, POOL_GW, POOL_GW)), _full((1, wc)), _full((1, wc))],
        out_specs=[row(2 * wc), row(wc), row(wc), row(wc)],
        out_shape=[jax.ShapeDtypeStruct((t, 2 * wc), BF16), jax.ShapeDtypeStruct((t, wc), F32),
                   jax.ShapeDtypeStruct((t, wc), F32), jax.ShapeDtypeStruct((t, wc), BF16)],
        scratch_shapes=[pltpu.VMEM((tm + HALO, wc), F32), pltpu.VMEM((tm + HALO, wc), F32)],
        compiler_params=_params("arbitrary"),
    )(p, p, p, p, p, p, p, p, cw, cb, lg, lb, pw, pb, ps)


def _acc_out(i, ref, val):
    @pl.when(i == 0)
    def _():
        ref[...] = val

    @pl.when(i > 0)
    def _():
        ref[...] += val


def _even_bwd_a(p, u1, e, dmat, dy, lg, lb, pw, ps, *, tm):
    t = p.shape[0]
    wc = W_CONV

    def body(agate, bgate, u1_ref, e_ref, d_ref, dya, dyb, lg_ref, lb_ref, pw_ref, ps_ref,
             du1_ref, dd_ref, dgat_ref, dlg_ref, dlb_ref, dpb_ref, dps_ref, dpw_ref):
        i = pl.program_id(0)

        @pl.when(i == 0)
        def _():
            dpw_ref[...] = jnp.zeros_like(dpw_ref)

        u1 = u1_ref[...]
        mu = jnp.mean(u1, axis=-1, keepdims=True)
        xc = u1 - mu
        rs = lax.rsqrt(jnp.mean(xc * xc, axis=-1, keepdims=True) + EPS_LN)
        xh = xc * rs
        u2 = xh * lg_ref[...] + lb_ref[...]
        s2 = _sigmoid(u2)
        ga = agate[...]
        sa = _sigmoid(ga)
        dy_a = dya[...]
        dgat_ref[:, 0:wc] = (dy_a * (u2 * s2) * _dsilu(ga, sa)).astype(BF16)
        du2 = dy_a * (ga * sa) * _dsilu(u2, s2)
        _acc_out(i, dlg_ref, jnp.sum(du2 * xh, axis=0, keepdims=True))
        _acc_out(i, dlb_ref, jnp.sum(du2, axis=0, keepdims=True))
        dxh = du2 * lg_ref[...]
        m1 = jnp.mean(dxh, axis=-1, keepdims=True)
        m2 = jnp.mean(dxh * xh, axis=-1, keepdims=True)
        du1_ref[...] = rs * (dxh - m1 - xh * m2)

        gb = bgate[...]
        sb = _sigmoid(gb)
        ev = e_ref[...]
        dy_b = dyb[...]
        dgat_ref[:, wc:2 * wc] = (dy_b * (ev * ps_ref[...]) * _dsilu(gb, sb)).astype(BF16)
        dz = dy_b * (gb * sb)
        _acc_out(i, dps_ref, jnp.sum(dz * ev, axis=0, keepdims=True))
        de = dz * ps_ref[...]
        _acc_out(i, dpb_ref, jnp.sum(de, axis=0, keepdims=True))
        for g in range(len(POOL_WINDOWS)):
            cs = slice(g * POOL_GW, (g + 1) * POOL_GW)
            deg = de[:, cs].astype(BF16)
            dd_ref[:, cs] = lax.dot_general(deg, pw_ref[g], (((1,), (1,)), ((), ())), preferred_element_type=F32)
            dpw_ref[g] += lax.dot_general(d_ref[:, cs], deg, (((0,), (0,)), ((), ())), preferred_element_type=F32)

    row = lambda w: pl.BlockSpec((tm, w), lambda i: (i, 0))
    return pl.pallas_call(
        body, name="even_bwd_a", grid=(t // tm,),
        in_specs=[_col(tm, wc, 2), _col(tm, wc, 4), row(wc), row(wc), row(wc), _col(tm, wc, 0), _col(tm, wc, 1),
                  _full((1, wc)), _full((1, wc)), _full((4, POOL_GW, POOL_GW)), _full((1, wc))],
        out_specs=[row(wc), row(wc), row(2 * wc), _full((1, wc)), _full((1, wc)), _full((1, wc)), _full((1, wc)),
                   _full((4, POOL_GW, POOL_GW))],
        out_shape=[jax.ShapeDtypeStruct((t, wc), F32), jax.ShapeDtypeStruct((t, wc), F32),
                   jax.ShapeDtypeStruct((t, 2 * wc), BF16)] + [jax.ShapeDtypeStruct((1, wc), F32)] * 4
                  + [jax.ShapeDtypeStruct((4, POOL_GW, POOL_GW), F32)],
        compiler_params=_params("arbitrary"),
    )(p, p, u1, e, dmat, dy, dy, lg, lb, pw, ps)


def _even_bwd_b(p, du1, dd, dgat, cw, *, tm):
    t = p.shape[0]
    wc = W_CONV
    nt = t // tm

    def body(av, ag, avh, agh, du1_ref, du1n, dd_ref, ddn, dgat_ref, cw_ref, dp_ref, dcw_ref, dcb_ref,
             uext, gext, dext, du0):
        i = pl.program_id(0)
        keep_p = (i > 0).astype(F32)
        keep_n = (i < nt - 1).astype(F32)

        @pl.when(i == 0)
        def _():
            dcw_ref[...] = jnp.zeros_like(dcw_ref)

        a = av[...]
        sg = _sigmoid(ag[...])
        uext[0:HALO, :] = keep_p * (avh[...] * _sigmoid(agh[...]))
        uext[HALO:, :] = a * sg
        gext[0:tm, :] = du1_ref[...]
        gext[tm:, :] = keep_n * du1n[...]
        for c in range(0, wc, 128):
            acc = jnp.zeros((tm, 128), F32)
            for s in range(CONV_K):
                k = CONV_K - 1 - s
                acc = acc + cw_ref[k:k + 1, c:c + 128] * gext[pl.ds(s, tm), c:c + 128]
            du0[:, c:c + 128] = acc
            gcur = du1_ref[:, c:c + 128]
            for k in range(CONV_K):
                dcw_ref[k:k + 1, c:c + 128] += jnp.sum(
                    gcur * uext[pl.ds(HALO - (CONV_K - 1) + k, tm), c:c + 128], axis=0, keepdims=True)

        _acc_out(i, dcb_ref, jnp.sum(du1_ref[...], axis=0, keepdims=True))
        g0 = du0[...]
        dp_ref[:, 0:wc] = (g0 * sg).astype(BF16)
        dp_ref[:, wc:2 * wc] = (g0 * a * sg * (1.0 - sg)).astype(BF16)
        dp_ref[:, 2 * wc:3 * wc] = dgat_ref[:, 0:wc]
        dp_ref[:, 4 * wc:5 * wc] = dgat_ref[:, wc:2 * wc]
        for g, win in enumerate(POOL_WINDOWS):
            cs = slice(g * POOL_GW, (g + 1) * POOL_GW)
            dext[0:tm, cs] = dd_ref[:, cs] * _inv_count(i * tm, tm, win)
            dext[tm:, cs] = keep_n * (ddn[:, cs] * _inv_count((i + 1) * tm, HALO, win))
            s = dext[pl.ds(0, tm), cs]
            for j in range(1, win):
                s = s + dext[pl.ds(j, tm), cs]
            dp_ref[:, 3 * wc + g * POOL_GW:3 * wc + (g + 1) * POOL_GW] = (s - dd_ref[:, cs]).astype(BF16)

    row = lambda w: pl.BlockSpec((tm, w), lambda i: (i, 0))
    return pl.pallas_call(
        body, name="even_bwd_b", grid=(nt,),
        in_specs=[_col(tm, wc, 0), _col(tm, wc, 1), _prev_halo(tm, HALO, wc, 0), _prev_halo(tm, HALO, wc, 1),
                  row(wc), _next_halo(tm, HALO, wc, 0, t), row(wc), _next_halo(tm, HALO, wc, 0, t), row(2 * wc),
                  _full((32, wc))],
        out_specs=[row(5 * wc), _full((32, wc)), _full((1, wc))],
        out_shape=[jax.ShapeDtypeStruct((t, 5 * wc), BF16), jax.ShapeDtypeStruct((32, wc), F32),
                   jax.ShapeDtypeStruct((1, wc), F32)],
        scratch_shapes=[pltpu.VMEM((tm + HALO, wc), F32), pltpu.VMEM((tm + HALO, wc), F32),
                        pltpu.VMEM((tm + HALO, wc), F32), pltpu.VMEM((tm, wc), F32)],
        compiler_params=_params("arbitrary"),
    )(p, p, p, p, du1, du1, dd, dd, dgat, cw)


def _softplus_neg(lam):
    z = -lam
    return jnp.maximum(z, 0.0) + jnp.log1p(jnp.exp(-jnp.abs(z)))


def _one_minus_exp(x):
    series = -x * (1.0 + x * (0.5 + x * (1.0 / 6.0 + x * (1.0 / 24.0))))
    return jnp.where(x > -0.02, series, 1.0 - jnp.exp(x))


def _odd_fwd(p, ccw, ccb, wrg, brg, wig, big, lam, *, tm):
    t = p.shape[0]
    wl = W_LRU
    ng = tm // 8

    def body(xr, gate, xrh, ccw_ref, ccb_ref, wrg_ref, brg_ref, wig_ref, big_ref, lam_ref,
             y_ref, xc_ref, r_ref, i_ref, hs_ref, xext, a_s, b_s, carry):
        i = pl.program_id(0)
        keep = (i > 0).astype(F32)
        xext[0:HALO_C, :] = keep * xrh[...]
        xext[HALO_C:, :] = xr[...]
        xc = jnp.broadcast_to(ccb_ref[...], (tm, wl))
        for k in range(LRU_CONV_K):
            xc = xc + ccw_ref[k:k + 1, :] * xext[pl.ds(HALO_C - (LRU_CONV_K - 1) + k, tm), :]
        xc_ref[...] = xc
        for h in range(LRU_HEADS):
            cs = slice(h * LRU_HD, (h + 1) * LRU_HD)
            xh = xc_ref[:, cs].astype(BF16)
            r_ref[:, cs] = _sigmoid(jnp.dot(xh, wrg_ref[h], preferred_element_type=F32) + brg_ref[:, cs])
            i_ref[:, cs] = _sigmoid(jnp.dot(xh, wig_ref[h], preferred_element_type=F32) + big_ref[:, cs])
        log_a = (-LRU_C * _softplus_neg(lam_ref[...])) * r_ref[...]
        a_s[...] = jnp.exp(log_a)
        b_s[...] = jnp.sqrt(_one_minus_exp(2.0 * log_a)) * (i_ref[...] * xc_ref[...])

        @pl.when(i == 0)
        def _():
            carry[...] = jnp.zeros_like(carry)

        rowi = lax.broadcasted_iota(jnp.int32, (8, wl), 0)

        def step(g, c):
            sl = pl.ds(pl.multiple_of(g * 8, 8), 8)
            aa, bb = a_s[sl, :], b_s[sl, :]
            for s in (1, 2, 4):
                m = rowi >= s
                a_sh = jnp.where(m, pltpu.roll(aa, s, 0), 1.0)
                b_sh = jnp.where(m, pltpu.roll(bb, s, 0), 0.0)
                bb = aa * b_sh + bb
                aa = aa * a_sh
            hv = bb + aa * c
            hs_ref[sl, :] = hv
            return hv[7:8, :]

        carry[...] = lax.fori_loop(0, ng, step, carry[...])
        gt = gate[...]
        y_ref[...] = (hs_ref[...] * (gt * _sigmoid(gt))).astype(BF16)

    row = lambda w: pl.BlockSpec((tm, w), lambda i: (i, 0))
    return pl.pallas_call(
        body, name="odd_fwd", grid=(t // tm,),
        in_specs=[_col(tm, wl, 0), _col(tm, wl, 1), _prev_halo(tm, HALO_C, wl, 0), _full((8, wl)), _full((1, wl)),
                  _full((LRU_HEADS, LRU_HD, LRU_HD)), _full((1, wl)), _full((LRU_HEADS, LRU_HD, LRU_HD)),
                  _full((1, wl)), _full((1, wl))],
        out_specs=[row(wl)] * 5,
        out_shape=[jax.ShapeDtypeStruct((t, wl), BF16)] + [jax.ShapeDtypeStruct((t, wl), F32)] * 4,
        scratch_shapes=[pltpu.VMEM((tm + HALO_C, wl), F32), pltpu.VMEM((tm, wl), F32), pltpu.VMEM((tm, wl), F32),
                        pltpu.VMEM((1, wl), F32)],
        compiler_params=_params("arbitrary"),
    )(p, p, p, ccw, ccb, wrg, brg, wig, big, lam)


def _odd_bwd_a(p, xc, r, ig, hs, dy, wrg, wig, lam, *, tm):
    t = p.shape[0]
    wl = W_LRU
    nt = t // tm
    ng = tm // 8
    per = tm // HALO_C

    def body(gate, xc_ref, r_ref, i_ref, hs_ref, hsh, dy_ref, wrg_ref, wig_ref, lam_ref,
             dxc_ref, dgate_ref, dwrg_ref, dwig_ref, dbrg_ref, dbig_ref, dlam_ref,
             hext, a_s, q_s, g_s, dpr_s, dpi_s, carry):
        i = pl.program_id(0)
        ti = nt - 1 - i
        keep = (ti > 0).astype(F32)
        hext[0:HALO_C, :] = keep * hsh[...]
        hext[HALO_C:, :] = hs_ref[...]
        gt = gate[...]
        sg = _sigmoid(gt)
        dyv = dy_ref[...]
        dgate_ref[...] = (dyv * hs_ref[...] * _dsilu(gt, sg)).astype(BF16)
        q_s[...] = dyv * (gt * sg)
        sp = _softplus_neg(lam_ref[...])
        log_a = (-LRU_C * sp) * r_ref[...]
        a_s[...] = jnp.exp(log_a)

        @pl.when(i == 0)
        def _():
            carry[...] = jnp.zeros_like(carry)
            dwrg_ref[...] = jnp.zeros_like(dwrg_ref)
            dwig_ref[...] = jnp.zeros_like(dwig_ref)

        rowi = lax.broadcasted_iota(jnp.int32, (8, wl), 0)

        def step(gr, c):
            sl = pl.ds(pl.multiple_of((ng - 1 - gr) * 8, 8), 8)
            a0 = a_s[sl, :]
            al = jnp.where(rowi < 7, pltpu.roll(a0, 7, 0), 1.0)
            be = q_s[sl, :]
            for s in (1, 2, 4):
                m = rowi + s <= 7
                al_sh = jnp.where(m, pltpu.roll(al, 8 - s, 0), 1.0)
                be_sh = jnp.where(m, pltpu.roll(be, 8 - s, 0), 0.0)
                be = be + al * be_sh
                al = al * al_sh
            gv = be + al * c
            g_s[sl, :] = gv
            return (a0 * gv)[0:1, :]

        carry[...] = lax.fori_loop(0, ng, step, carry[...])

        gv = g_s[...]
        a = a_s[...]
        mult = jnp.sqrt(_one_minus_exp(2.0 * log_a))
        iv = i_ref[...]
        rv = r_ref[...]
        xcv = xc_ref[...]
        hprev = hext[pl.ds(HALO_C - 1, tm), :]
        dla = gv * hprev * a - (gv * iv * xcv) * (a * a) / mult
        di = gv * mult * xcv
        dpr = (dla * (-LRU_C * sp)) * rv * (1.0 - rv)
        dpi = di * iv * (1.0 - iv)
        dpr_s[...] = dpr
        dpi_s[...] = dpi
        dxc_ref[...] = gv * mult * iv
        dsp = jnp.sum(dla * rv, axis=0, keepdims=True) * (-LRU_C)
        _acc_out(i, dlam_ref, -dsp * _sigmoid(-lam_ref[...]))
        _acc_out(i, dbrg_ref, jnp.sum(dpr, axis=0, keepdims=True))
        _acc_out(i, dbig_ref, jnp.sum(dpi, axis=0, keepdims=True))
        for h in range(LRU_HEADS):
            cs = slice(h * LRU_HD, (h + 1) * LRU_HD)
            xh = xc_ref[:, cs].astype(BF16)
            dr_h = dpr_s[:, cs].astype(BF16)
            di_h = dpi_s[:, cs].astype(BF16)
            dxc_ref[:, cs] += (
                lax.dot_general(dr_h, wrg_ref[h], (((1,), (1,)), ((), ())), preferred_element_type=F32)
                + lax.dot_general(di_h, wig_ref[h], (((1,), (1,)), ((), ())), preferred_element_type=F32))
            dwrg_ref[h] += lax.dot_general(xh, dr_h, (((0,), (0,)), ((), ())), preferred_element_type=F32)
            dwig_ref[h] += lax.dot_general(xh, di_h, (((0,), (0,)), ((), ())), preferred_element_type=F32)

    rrow = lambda w: pl.BlockSpec((tm, w), lambda i: (nt - 1 - i, 0))
    hspec = pl.BlockSpec((HALO_C, wl), lambda i: (jnp.maximum((nt - 1 - i) * per - 1, 0), 0))
    wspec = _full((LRU_HEADS, LRU_HD, LRU_HD))
    return pl.pallas_call(
        body, name="odd_bwd_a", grid=(nt,),
        in_specs=[pl.BlockSpec((tm, wl), lambda i: (nt - 1 - i, 1)), rrow(wl), rrow(wl), rrow(wl), rrow(wl), hspec,
                  rrow(wl), wspec, wspec, _full((1, wl))],
        out_specs=[rrow(wl), rrow(wl), wspec, wspec, _full((1, wl)), _full((1, wl)), _full((1, wl))],
        out_shape=[jax.ShapeDtypeStruct((t, wl), F32), jax.ShapeDtypeStruct((t, wl), BF16),
                   jax.ShapeDtypeStruct((LRU_HEADS, LRU_HD, LRU_HD), F32),
                   jax.ShapeDtypeStruct((LRU_HEADS, LRU_HD, LRU_HD), F32)] + [jax.ShapeDtypeStruct((1, wl), F32)] * 3,
        scratch_shapes=[pltpu.VMEM((tm + HALO_C, wl), F32)] + [pltpu.VMEM((tm, wl), F32)] * 5
                       + [pltpu.VMEM((1, wl), F32)],
        compiler_params=_params("arbitrary"),
    )(p, xc, r, ig, hs, hs, dy, wrg, wig, lam)


def _odd_bwd_b(p, dxc, dgate, ccw, *, tm):
    t = p.shape[0]
    wl = W_LRU
    nt = t // tm

    def body(xr, xrh, dxc_ref, dxcn, dgate_ref, ccw_ref, dp_ref, dcw_ref, dcb_ref, xext, gext):
        i = pl.program_id(0)

        @pl.when(i == 0)
        def _():
            dcw_ref[...] = jnp.zeros_like(dcw_ref)

        xext[0:HALO_C, :] = (i > 0).astype(F32) * xrh[...]
        xext[HALO_C:, :] = xr[...]
        gext[0:tm, :] = dxc_ref[...]
        gext[tm:, :] = (i < nt - 1).astype(F32) * dxcn[...]
        g = dxc_ref[...]
        acc = jnp.zeros((tm, wl), F32)
        for k in range(LRU_CONV_K):
            acc = acc + ccw_ref[k:k + 1, :] * gext[pl.ds(LRU_CONV_K - 1 - k, tm), :]
            dcw_ref[k:k + 1, :] += jnp.sum(
                g * xext[pl.ds(HALO_C - (LRU_CONV_K - 1) + k, tm), :], axis=0, keepdims=True)

        _acc_out(i, dcb_ref, jnp.sum(g, axis=0, keepdims=True))
        dp_ref[:, 0:wl] = acc.astype(BF16)
        dp_ref[:, wl:2 * wl] = dgate_ref[...]

    row = lambda w: pl.BlockSpec((tm, w), lambda i: (i, 0))
    return pl.pallas_call(
        body, name="odd_bwd_b", grid=(nt,),
        in_specs=[_col(tm, wl, 0), _prev_halo(tm, HALO_C, wl, 0), row(wl), _next_halo(tm, HALO_C, wl, 0, t), row(wl),
                  _full((8, wl))],
        out_specs=[row(2 * wl), _full((8, wl)), _full((1, wl))],
        out_shape=[jax.ShapeDtypeStruct((t, 2 * wl), BF16), jax.ShapeDtypeStruct((8, wl), F32),
                   jax.ShapeDtypeStruct((1, wl), F32)],
        scratch_shapes=[pltpu.VMEM((tm + HALO_C, wl), F32), pltpu.VMEM((tm + HALO_C, wl), F32)],
        compiler_params=_params("arbitrary"),
    )(p, p, dxc, dxc, dgate, ccw)


def _local_step(x, target, layer_weights, final_norm, on_grads):
    tm, tx = TM_MATMUL, TM_MIXER
    h = x
    saved = []
    for layer in range(N_LAYERS):
        w = layer_weights(layer, h)
        p, hn = _norm_matmul(h, w["norm"], w["w_in"], tm=tm)
        if layer % 2 == 0:
            y, *acts = _even_fwd(p, w["conv_w"], w["conv_b"], w["ln_g"], w["ln_b"], w["pool_w"], w["pool_b"],
                                 w["pool_scale"], tm=tx)
        else:
            y, *acts = _odd_fwd(p, w["conv_w"], w["conv_b"], w["w_rg"], w["b_rg"], w["w_ig"], w["b_ig"], w["lam"],
                                tm=tx)
        saved.append((w, h, p, hn, y, acts))
        h = _out_proj(h, y, w["w_out"], tm=tm)
    loss, dh, d_final = _loss_head(h, final_norm, target, tm=tm)

    dep = d_final
    for layer in reversed(range(N_LAYERS)):
        w, h_in, p, hn, y, acts = saved[layer]
        dy, dw_out = _out_proj_bwd(dh, y, w["w_out"], dep, tm=tm)
        if layer % 2 == 0:
            u1, e, dmat = acts
            du1, dd, dgat, dlg, dlb, dpb, dps, dpw = _even_bwd_a(p, u1, e, dmat, dy, w["ln_g"], w["ln_b"], w["pool_w"],
                                                                 w["pool_scale"], tm=tx)
            dp, dcw, dcb = _even_bwd_b(p, du1, dd, dgat, w["conv_w"], tm=tx)
            grads = dict(conv_a_w=dcw[:CONV_K], conv_a_b=dcb, ln_a_g=dlg, ln_a_b=dlb, pool_w=dpw, pool_b=dpb,
                         pool_scale=dps, w_out_even=dw_out)
            sfx = "even"
        else:
            xc, r, ig, hs = acts
            dxc, dgate, dwrg, dwig, dbrg, dbig, dlam = _odd_bwd_a(p, xc, r, ig, hs, dy, w["w_rg"], w["w_ig"],
                                                                  w["lam"], tm=tx)
            dp, dccw, dccb = _odd_bwd_b(p, dxc, dgate, w["conv_w"], tm=tx)
            grads = dict(conv_c_w=dccw[:LRU_CONV_K], conv_c_b=dccb, w_rg=dwrg, b_rg=dbrg, w_ig=dwig, b_ig=dbig,
                         lru_lambda=dlam, w_out_odd=dw_out)
            sfx = "odd"
        grads["w_in_" + sfx] = _in_proj_bwd_w(hn, dp, N_DEV, tm=tm)
        dh, grads["norm_" + sfx] = _in_proj_bwd_x(dp, w["w_in"], h_in, w["norm"], dh, tm=tm)
        if layer == N_LAYERS - 1:
            grads["final_norm"] = d_final
        dep = on_grads(layer, grads)
    return loss, dh


def _slot(px, py, pc):
    return 4 * px + 2 * py + pc


def _peers(x, y, c):
    return [(1 - x if k & 4 else x, 1 - y if k & 2 else y, 1 - c if k & 1 else c) for k in range(1, N_DEV)]


def _all_gather(arrs, name):
    n = len(arrs)

    def body(*refs):
        ins, outs = refs[:n], refs[n:2 * n]
        send_sems, recv_sems, local_sems = refs[2 * n:]
        x, y, c = lax.axis_index("x"), lax.axis_index("y"), lax.axis_index("c")
        me, sibling = (x, y, c), (x, y, 1 - c)
        chips = [(1 - x, y), (x, 1 - y), (1 - x, 1 - y)]

        def copy(a, k, block, to, src=None):
            rows = outs[a].at[_slot(*block)]
            return pltpu.make_async_remote_copy(
                src_ref=rows if src is None else src, dst_ref=rows, send_sem=send_sems.at[a, k],
                recv_sem=recv_sems.at[a, k], device_id=to, device_id_type=MESH)

        mine = [pltpu.make_async_copy(ins[a], outs[a].at[_slot(*me)], local_sems.at[a]) for a in range(n)]
        for cp in mine:
            cp.start()
        first = []
        for a in range(n):
            first.append(copy(a, 0, me, sibling, src=ins[a]))
            first += [copy(a, 1 + j, me, (*chip, c), src=ins[a]) for j, chip in enumerate(chips)]
        for cp in first:
            cp.start()
        passed = []
        for j, chip in enumerate(chips):
            for a in range(n):
                copy(a, 1 + j, (*chip, c), me).wait_recv()
                fwd = copy(a, 4 + j, (*chip, c), sibling)
                fwd.start()
                passed.append(fwd)
        for a in range(n):
            copy(a, 0, sibling, me).wait_recv()
            for j, chip in enumerate(chips):
                copy(a, 4 + j, (*chip, 1 - c), me).wait_recv()
        for cp in first + passed:
            cp.wait_send()
        for cp in mine:
            cp.wait()

    return pl.pallas_call(
        body, name=name,
        in_specs=[ANY] * n, out_specs=[ANY] * n,
        out_shape=[jax.ShapeDtypeStruct((N_DEV,) + a.shape, a.dtype) for a in arrs],
        scratch_shapes=[pltpu.SemaphoreType.DMA((n, 7)), pltpu.SemaphoreType.DMA((n, 7)),
                        pltpu.SemaphoreType.DMA((n,))],
    )(*arrs)


def _exchange_copy(scatter, src_ref, land_ref, peer, send_sem, recv_sem, landing):
    return pltpu.make_async_remote_copy(
        src_ref=src_ref.at[_slot(*peer)] if scatter else src_ref, dst_ref=land_ref.at[landing],
        send_sem=send_sem, recv_sem=recv_sem, device_id=peer, device_id_type=MESH)


def _exchange_start(groups, scatter, deps, name):
    flat = [pair for g in groups for pair in g]
    n, ng = len(flat), len(groups)

    def body(*refs):
        src_refs, land_refs = refs[:n], refs[n:2 * n]
        outs = refs[2 * n + len(deps):]
        sems, token = outs[:2 * ng], outs[2 * ng + 2 * n]
        x, y, c = lax.axis_index("x"), lax.axis_index("y"), lax.axis_index("c")
        me = _slot(x, y, c)
        base = 0
        for gi, g in enumerate(groups):
            for k, peer in enumerate(_peers(x, y, c)):
                for ai in range(len(g)):
                    _exchange_copy(scatter, src_refs[base + ai], land_refs[base + ai], peer,
                                   sems[2 * gi].at[ai * N_PEERS + k], sems[2 * gi + 1].at[ai * N_PEERS + k],
                                   me).start()
            base += len(g)
        token[...] = jnp.zeros_like(token)

    operands = [pltpu.with_memory_space_constraint(a, pltpu.HBM) for a in
                [s for s, _ in flat] + [l for _, l in flat]]
    out_shape = []
    for g in groups:
        out_shape += [pltpu.SemaphoreType.DMA((len(g) * N_PEERS,))] * 2
    out_shape += [pltpu.HBM(a.shape, a.dtype) for a in operands]
    out_shape.append(jax.ShapeDtypeStruct((8, 128), F32))
    outs = pl.pallas_call(
        body, name=name, out_shape=out_shape,
        in_specs=[HBM] * (2 * n) + [ANY] * len(deps),
        out_specs=[SEM] * (2 * ng) + [HBM] * (2 * n) + [pl.BlockSpec(memory_space=pltpu.VMEM)],
        input_output_aliases={i: 2 * ng + i for i in range(2 * n)},
        compiler_params=pltpu.CompilerParams(has_side_effects=pltpu.SideEffectType.DATAFLOW_SIDE_EFFECTING),
    )(*operands, *deps)
    handles, base = [], 0
    for gi, g in enumerate(groups):
        srcs = outs[2 * ng + base:2 * ng + base + len(g)]
        lands = outs[2 * ng + n + base:2 * ng + n + base + len(g)]
        handles.append((outs[2 * gi], outs[2 * gi + 1], list(srcs), list(lands)))
        base += len(g)
    return handles, outs[-1]


def _exchange_wait(handle, scatter, after, name):
    send_sems, recv_sems, srcs, lands = handle
    n = len(srcs)

    def body(*refs):
        src_refs, land_refs = refs[:n], refs[n:2 * n]
        send_ref, recv_ref = refs[2 * n], refs[2 * n + 1]
        x, y, c = lax.axis_index("x"), lax.axis_index("y"), lax.axis_index("c")
        for k, peer in enumerate(_peers(x, y, c)):
            for a in range(n):
                cp = _exchange_copy(scatter, src_refs[a], land_refs[a], peer, send_ref.at[a * N_PEERS + k],
                                    recv_ref.at[a * N_PEERS + k],
                                    _slot(*peer))
                cp.wait_send()
                cp.wait_recv()

    outs = pl.pallas_call(
        body, name=name,
        out_shape=[pltpu.HBM(a.shape, a.dtype) for a in srcs + lands],
        in_specs=[HBM] * (2 * n) + [SEM, SEM] + [ANY] * len(after),
        out_specs=[HBM] * (2 * n),
        input_output_aliases={i: i for i in range(2 * n)},
        compiler_params=pltpu.CompilerParams(has_side_effects=pltpu.SideEffectType.DATAFLOW_SIDE_EFFECTING),
    )(*srcs, *lands, send_sems, recv_sems, *after)
    return list(outs[n:])


def _sum_parts(parts, *, tr):
    np_, r, c = parts.shape

    def body(p_ref, o_ref):
        acc = p_ref[0].astype(F32)
        for k in range(1, np_):
            acc = acc + p_ref[k].astype(F32)
        o_ref[...] = acc

    return pl.pallas_call(
        body, name="sum_parts", grid=(r // tr,),
        in_specs=[pl.BlockSpec((np_, tr, c), lambda i: (0, i, 0))],
        out_specs=pl.BlockSpec((tr, c), lambda i: (i, 0)),
        out_shape=jax.ShapeDtypeStruct((r, c), F32),
        compiler_params=_params("arbitrary"),
    )(parts)


def _adamw_math(w, g, m, v):
    c1 = 1.0 - ADAM_B1 ** ADAM_STEP
    c2 = 1.0 - ADAM_B2 ** ADAM_STEP
    nm = ADAM_B1 * m + (1.0 - ADAM_B1) * g
    nv = ADAM_B2 * v + (1.0 - ADAM_B2) * (g * g)
    delta = -ADAM_LR * ((nm / c1) / (jnp.sqrt(nv / c2) + ADAM_EPS) + ADAM_WD * w)
    return delta, nm, nv


def _row_tile(r):
    for cand in (256, 128, 64, 32, 16, 8):
        if r % cand == 0 and r > cand:
            return cand
    return r


def _adamw(items, layer0, bufs, name):
    ni = len(items)
    nl = items[0][1].shape[1]
    tiles = [_row_tile(w.shape[1]) for w, _, _, _ in items]
    steps = [w.shape[1] // tr for (w, _, _, _), tr in zip(items, tiles)]
    ns = steps[0]
    assert all(s == ns for s in steps)
    nb = 0 if bufs is None else 4 * ni

    def body(*refs):
        ins, outs = refs[:4 * ni], refs[4 * ni + nb:]
        for k in range(ni):
            w_ref, p_ref, m_ref, v_ref = ins[4 * k:4 * k + 4]
            g = p_ref[0, 0].astype(F32)
            for s in range(1, p_ref.shape[0]):
                g = g + p_ref[s, 0].astype(F32)
            delta, nm, nv = _adamw_math(w_ref[0], g, m_ref[0], v_ref[0])
            g_ref, d_ref, nm_ref, nv_ref = outs[4 * k:4 * k + 4]
            g_ref[0], d_ref[0], nm_ref[0], nv_ref[0] = g, delta, nm, nv

    in_specs, out_specs, out_shape, operands = [], [], [], []
    for (w, parts, m, v), tr in zip(items, tiles):
        blk = pl.BlockSpec((1, tr, w.shape[2]), lambda l, i: (layer0 + l, i, 0))
        in_specs += [blk, pl.BlockSpec((parts.shape[0], 1, tr, w.shape[2]), lambda l, i: (0, l, i, 0)), blk, blk]
        operands += [w, parts, m, v]
        out_specs += [blk] * 4
        out_shape += [jax.ShapeDtypeStruct(w.shape, F32)] * 4
    if bufs is not None:
        in_specs += [ANY] * nb
        operands += [b for item in bufs for b in item]
    outs = pl.pallas_call(
        body, name=name, grid=(nl, ns), in_specs=in_specs, out_specs=out_specs, out_shape=out_shape,
        input_output_aliases={4 * ni + i: i for i in range(nb)},
        compiler_params=_params("arbitrary", "arbitrary"),
    )(*operands)
    return [tuple(outs[4 * k:4 * k + 4]) for k in range(ni)]


NAMES = ("norm_even", "w_in_even", "conv_a_w", "conv_a_b", "ln_a_g", "ln_a_b", "pool_w", "pool_b", "pool_scale",
         "w_out_even", "norm_odd", "w_in_odd", "conv_c_w", "conv_c_b", "w_rg", "b_rg", "w_ig", "b_ig", "lru_lambda",
         "w_out_odd", "final_norm")
SMALL_GATHERED = ("conv_a_w", "pool_b", "norm_odd", "conv_c_w", "conv_c_b", "b_rg", "b_ig", "lru_lambda")
BIG = (("w_in_even", "w_out_even"), ("w_in_odd", "w_out_odd"))
SMALL = (("conv_a_w", "pool_b", "pool_w"), ("norm_odd", "conv_c_w", "conv_c_b", "b_rg", "b_ig", "lru_lambda"))
REPLICATED = (("norm_even", "conv_a_b", "ln_a_g", "ln_a_b", "pool_scale"), ("w_rg", "w_ig"))
PACK_ROW = 1024


def _pack_rows(flat2d):
    pad = (-flat2d.shape[1]) % PACK_ROW
    return jnp.pad(flat2d, ((0, 0), (0, pad))).reshape(flat2d.shape[0], -1, 128)


def _unpack(flat, shapes):
    out, off = [], 0
    for s in shapes:
        n = 1
        for d in s:
            n *= d
        out.append(flat[..., off:off + n].reshape(flat.shape[:-1] + tuple(s)))
        off += n
    return out


def _to_global(name, g):
    if name in ("conv_a_w", "pool_b", "conv_c_w"):
        return jnp.transpose(g, (1, 2, 0, 3)).reshape(g.shape[1], g.shape[2], -1)
    if name == "pool_w":
        return jnp.transpose(g, (1, 2, 0, 3, 4)).reshape(2, 4, POOL_GW, POOL_GW)
    return jnp.transpose(g, (1, 0, 2)).reshape(g.shape[1], -1)


def _to_blocks(name, g):
    if name == "conv_a_w":
        return jnp.transpose(g.reshape(CONV_K, N_DEV, -1), (1, 0, 2))
    if name == "conv_c_w":
        return jnp.transpose(g.reshape(LRU_CONV_K, N_DEV, -1), (1, 0, 2))
    if name == "pool_b":
        return jnp.transpose(g.reshape(4, N_DEV, -1), (1, 0, 2))
    if name == "pool_w":
        return jnp.transpose(g.reshape(4, N_DEV, POOL_GW // N_DEV, POOL_GW), (1, 0, 2, 3))
    return g.reshape(N_DEV, -1)


def _as3d(a):
    if a.ndim == 1:
        return a.reshape(1, 1, -1)
    if a.ndim == 2:
        return a.reshape(a.shape[0], 1, a.shape[1])
    return a.reshape(a.shape[0], -1, a.shape[-1])


def kernel(x, norm_even, w_in_even, conv_a_w, conv_a_b, ln_a_g, ln_a_b, pool_w, pool_b, pool_scale, w_out_even, norm_odd, w_in_odd, conv_c_w, conv_c_b, w_rg, b_rg, w_ig, b_ig, lru_lambda, w_out_odd, final_norm, loss_target, m_norm_even, m_w_in_even, m_conv_a_w, m_conv_a_b, m_ln_a_g, m_ln_a_b, m_pool_w, m_pool_b, m_pool_scale, m_w_out_even, m_norm_odd, m_w_in_odd, m_conv_c_w, m_conv_c_b, m_w_rg, m_b_rg, m_w_ig, m_b_ig, m_lru_lambda, m_w_out_odd, m_final_norm, v_norm_even, v_w_in_even, v_conv_a_w, v_conv_a_b, v_ln_a_g, v_ln_a_b, v_pool_w, v_pool_b, v_pool_scale, v_w_out_even, v_norm_odd, v_w_in_odd, v_conv_c_w, v_conv_c_b, v_w_rg, v_b_rg, v_w_ig, v_b_ig, v_lru_lambda, v_w_out_odd, v_final_norm):
    w_loc = dict(zip(NAMES, [norm_even, w_in_even, conv_a_w, conv_a_b, ln_a_g, ln_a_b, pool_w, pool_b, pool_scale,
                             w_out_even, norm_odd, w_in_odd, conv_c_w, conv_c_b, w_rg, b_rg, w_ig, b_ig, lru_lambda,
                             w_out_odd, final_norm]))
    m_loc = dict(zip(NAMES, [m_norm_even, m_w_in_even, m_conv_a_w, m_conv_a_b, m_ln_a_g, m_ln_a_b, m_pool_w, m_pool_b,
                             m_pool_scale, m_w_out_even, m_norm_odd, m_w_in_odd, m_conv_c_w, m_conv_c_b, m_w_rg,
                             m_b_rg, m_w_ig, m_b_ig, m_lru_lambda, m_w_out_odd, m_final_norm]))
    v_loc = dict(zip(NAMES, [v_norm_even, v_w_in_even, v_conv_a_w, v_conv_a_b, v_ln_a_g, v_ln_a_b, v_pool_w, v_pool_b,
                             v_pool_scale, v_w_out_even, v_norm_odd, v_w_in_odd, v_conv_c_w, v_conv_c_b, v_w_rg,
                             v_b_rg, v_w_ig, v_b_ig, v_lru_lambda, v_w_out_odd, v_final_norm]))
    me = _slot(lax.axis_index("x"), lax.axis_index("y"), lax.axis_index("c"))

    def landing(own):
        zone = lax.empty((N_DEV,) + own.shape[1:], own.dtype)
        return lax.dynamic_update_slice(zone, own, (me,) + (0,) * (own.ndim - 1))

    small_shapes = [w_loc[n].shape for n in SMALL_GATHERED]
    small = jnp.concatenate([w_loc[n].reshape(1, -1) for n in SMALL_GATHERED], axis=1)
    first = _all_gather([w_in_even[0].astype(BF16), w_out_even[0].astype(BF16), pool_w.astype(BF16),
                         _pack_rows(small)[0]], "gather_first")
    g_small = dict(zip(SMALL_GATHERED, [_to_global(n, g) for n, g in
                                        zip(SMALL_GATHERED, _unpack(first[3].reshape(N_DEV, -1), small_shapes))]))
    pool_w_all = _to_global("pool_w", first[2])
    later = [(w_in_odd[0], w_out_odd[0]), (w_in_even[1], w_out_even[1]), (w_in_odd[1], w_out_odd[1])]
    groups = [[(s.astype(BF16), landing(s.astype(BF16)[None])) for s in pair] for pair in later]
    gather_handles, _ = _exchange_start(groups, False, [first[0]], "gather_rest_start")

    def layer_weights(layer, h):
        j = layer // 2
        if layer == 0:
            w_in, w_out = first[0], first[1]
        else:
            w_in, w_out = _exchange_wait(gather_handles[layer - 1], False, [h], f"gather_wait_{layer}")
        row = lambda a: a[j][None]
        if layer % 2 == 0:
            return dict(norm=row(norm_even), w_in=w_in, w_out=w_out.reshape(W_EVEN_MIX, D_MODEL),
                        conv_w=jnp.pad(g_small["conv_a_w"][j], ((0, 1), (0, 0))), conv_b=row(conv_a_b),
                        ln_g=row(ln_a_g), ln_b=row(ln_a_b), pool_w=pool_w_all[j],
                        pool_b=g_small["pool_b"][j].reshape(1, W_POOL), pool_scale=row(pool_scale))
        return dict(norm=row(g_small["norm_odd"]), w_in=w_in, w_out=w_out.reshape(W_LRU, D_MODEL),
                    conv_w=jnp.pad(g_small["conv_c_w"][j], ((0, 4), (0, 0))), conv_b=row(g_small["conv_c_b"]),
                    w_rg=w_rg[j].astype(BF16), b_rg=row(g_small["b_rg"]), w_ig=w_ig[j].astype(BF16),
                    b_ig=row(g_small["b_ig"]), lam=row(g_small["lru_lambda"]))

    scatter_handles, scatter_tokens, small_layout = {}, {}, {}

    def on_grads(layer, grads):
        par = layer % 2
        w_in_name, w_out_name = BIG[par]
        small_blocks = [_to_blocks(n, grads[n]) for n in SMALL[par]]
        rep_names = list(REPLICATED[par]) + (["final_norm"] if "final_norm" in grads else [])
        rep = jnp.concatenate([grads[n].reshape(-1) for n in rep_names]).reshape(N_DEV, -1)
        small_layout[layer] = ([b.shape[1:] for b in small_blocks], rep.shape[1])
        send = [grads[w_in_name], grads[w_out_name].reshape(N_DEV, -1, D_MODEL),
                _pack_rows(jnp.concatenate([b.reshape(N_DEV, -1) for b in small_blocks] + [rep], axis=1))]
        pairs = [(s, landing(lax.dynamic_slice_in_dim(s, me, 1, 0))) for s in send]
        (handle,), token = _exchange_start([pairs], True, [], f"scatter_start_{layer}")
        scatter_handles[layer] = handle
        scatter_tokens[layer] = token
        return token

    loss, grad_x = _local_step(x[0], loss_target[0], layer_weights, final_norm[None], on_grads)

    w3 = {n: _as3d(w_loc[n]) for n in NAMES}
    m3 = {n: _as3d(m_loc[n]) for n in NAMES}
    v3 = {n: _as3d(v_loc[n]) for n in NAMES}
    results, rep_parts = {}, {}
    after = [scatter_tokens[0]]
    for layer in (3, 2, 1, 0):
        par, j = layer % 2, layer // 2
        got_in, got_out, got_small = _exchange_wait(scatter_handles[layer], True, after, f"scatter_wait_{layer}")
        shapes, rep_len = small_layout[layer]
        *small_parts, rep_parts[layer] = _unpack(got_small.reshape(N_DEV, -1), list(shapes) + [(rep_len,)])
        names = list(BIG[par]) + list(SMALL[par])
        parts = [got_in, got_out] + small_parts
        big_items, small_items = [], []
        for n, pt in zip(names, parts):
            item = (w3[n], pt.reshape((N_DEV, 1) + w3[n].shape[1:]), m3[n], v3[n])
            (big_items if w3[n].shape[1] >= 128 else small_items).append((n, item))
        for n, item in big_items:
            results[n] = _adamw([item], j, [results[n]] if n in results else None, f"adamw_{n}_{layer}")[0]
        snames = [n for n, _ in small_items]
        prev = [results[n] for n in snames] if snames[0] in results else None
        for n, r in zip(snames, _adamw([it for _, it in small_items], j, prev, f"adamw_small_{layer}")):
            results[n] = r
        after = [results[names[0]][1]]

    rep_pack = _pack_rows(jnp.concatenate([rep_parts[l] for l in range(N_LAYERS)], axis=1))
    rep_mine = _sum_parts(rep_pack, tr=rep_pack.shape[1])
    rep_all = _all_gather([rep_mine], "gather_replicated")[0].reshape(N_DEV, -1)
    rep_grads, off = {}, 0
    for layer in range(N_LAYERS):
        rep_len = small_layout[layer][1]
        flat = rep_all[:, off:off + rep_len].reshape(-1)
        off += rep_len
        rep_names = list(REPLICATED[layer % 2]) + (["final_norm"] if layer == N_LAYERS - 1 else [])
        for n, g in zip(rep_names, _unpack(flat, [w_loc[n].shape[1:] if n != "final_norm" else w_loc[n].shape
                                                  for n in rep_names])):
            rep_grads.setdefault(n, []).append(g)
    rep_items = {n: (w3[n], _as3d(jnp.stack(rep_grads[n]) if n != "final_norm" else rep_grads[n][0])[None],
                     m3[n], v3[n]) for par in range(2) for n in REPLICATED[par]}
    rep_items["final_norm"] = (w3["final_norm"], _as3d(rep_grads["final_norm"][0])[None], m3["final_norm"],
                               v3["final_norm"])
    for n in ("w_rg", "w_ig"):
        results[n] = _adamw([rep_items[n]], 0, None, f"adamw_{n}")[0]
    vec_names = list(REPLICATED[0])
    for n, r in zip(vec_names, _adamw([rep_items[n] for n in vec_names], 0, None, "adamw_replicated")):
        results[n] = r
    results["final_norm"] = _adamw([rep_items["final_norm"]], 0, None, "adamw_final_norm")[0]

    total = lax.psum(loss[0, 0], ("x", "y", "c"))
    outs = [[results[n][k].reshape(w_loc[n].shape) for n in NAMES] for k in range(4)]
    return (total, grad_x[None], *outs[0], *outs[1], *outs[2], *outs[3])
```

```python
import functools

import jax
import jax.numpy as jnp
from jax import lax
from jax.experimental import pallas as pl
from jax.experimental.pallas import tpu as pltpu

F32 = jnp.float32
BF16 = jnp.bfloat16

N_DEV = 8
N_PEERS = N_DEV - 1
N_LAYERS = 4
D_MODEL = 1024
EPS_RMS = 1e-6
EPS_LN = 1e-5
W_CONV = 1024
CONV_K = 31
W_POOL = 1024
POOL_WINDOWS = (2, 4, 8, 16)
POOL_GW = 256
W_EVEN_IN = 5120
W_EVEN_MIX = 2048
LRU_HEADS = 12
LRU_HD = 128
W_LRU = 1536
LRU_CONV_K = 4
LRU_C = 8.0
ADAM_LR = 0.001
ADAM_B1 = 0.9
ADAM_B2 = 0.999
ADAM_EPS = 1e-08
ADAM_WD = 0.01
ADAM_STEP = 10

HALO = 32
HALO_C = 8
TM_MATMUL = 512
TM_MIXER = 256
VMEM_LIMIT = 56 * 1024 * 1024
MESH = pl.DeviceIdType.MESH
ANY = pl.BlockSpec(memory_space=pl.ANY)
HBM = pl.BlockSpec(memory_space=pltpu.HBM)
SEM = pl.BlockSpec(memory_space=pltpu.SEMAPHORE)


def _params(*sem):
    return pltpu.CompilerParams(dimension_semantics=sem, vmem_limit_bytes=VMEM_LIMIT)


def _sigmoid(z):
    return jax.nn.sigmoid(z)


def _dsilu(z, s):
    return s * (1.0 + z * (1.0 - s))


def _full(shape):
    nd = len(shape)
    return pl.BlockSpec(shape, lambda *_: (0,) * nd)


def _norm_matmul(h, g, w, *, tm):
    t, d = h.shape
    nd, _, nb = w.shape

    def body(h_ref, g_ref, w_ref, p_ref, hn_ref):
        @pl.when(pl.program_id(1) == 0)
        def _():
            x = h_ref[...]
            r = lax.rsqrt(jnp.mean(x * x, axis=-1, keepdims=True) + EPS_RMS)
            hn_ref[...] = ((x * r) * g_ref[...]).astype(BF16)

        p_ref[...] = jnp.dot(hn_ref[...], w_ref[0], preferred_element_type=F32)

    return pl.pallas_call(
        body, name="norm_matmul", grid=(t // tm, nd),
        in_specs=[pl.BlockSpec((tm, d), lambda i, j: (i, 0)), _full((1, d)),
                  pl.BlockSpec((1, d, nb), lambda i, j: (j, 0, 0))],
        out_specs=[pl.BlockSpec((tm, nb), lambda i, j: (i, j)), pl.BlockSpec((tm, d), lambda i, j: (i, 0))],
        out_shape=[jax.ShapeDtypeStruct((t, nd * nb), F32), jax.ShapeDtypeStruct((t, d), BF16)],
        compiler_params=_params("arbitrary", "arbitrary"),
    )(h, g, w)


def _out_proj(h, y, w, *, tm):
    t, d = h.shape
    k = y.shape[1]

    def body(h_ref, y_ref, w_ref, o_ref):
        o_ref[...] = h_ref[...] + jnp.dot(y_ref[...], w_ref[...], preferred_element_type=F32)

    return pl.pallas_call(
        body, name="out_proj", grid=(t // tm,),
        in_specs=[pl.BlockSpec((tm, d), lambda i: (i, 0)), pl.BlockSpec((tm, k), lambda i: (i, 0)), _full((k, d))],
        out_specs=pl.BlockSpec((tm, d), lambda i: (i, 0)),
        out_shape=jax.ShapeDtypeStruct((t, d), F32),
        compiler_params=_params("arbitrary"),
    )(h, y, w)


def _out_proj_bwd(dh, y, w, dep, *, tm):
    t, d = dh.shape
    k = y.shape[1]
    nt = t // tm

    def body(dh_ref, y_ref, w_ref, dep_ref, dy_ref, dw_ref, acc):
        i = pl.program_id(0)
        g = dh_ref[...].astype(BF16)
        dy_ref[...] = lax.dot_general(g, w_ref[...], (((1,), (1,)), ((), ())), preferred_element_type=F32)
        part = lax.dot_general(y_ref[...], g, (((0,), (0,)), ((), ())), preferred_element_type=F32)

        @pl.when(i == 0)
        def _():
            acc[...] = part

        @pl.when(i > 0)
        def _():
            acc[...] += part

        @pl.when(i == nt - 1)
        def _():
            dw_ref[...] = acc[...].astype(BF16)

    return pl.pallas_call(
        body, name="out_proj_bwd", grid=(nt,),
        in_specs=[pl.BlockSpec((tm, d), lambda i: (i, 0)), pl.BlockSpec((tm, k), lambda i: (i, 0)), _full((k, d)),
                  ANY],
        out_specs=[pl.BlockSpec((tm, k), lambda i: (i, 0)), _full((k, d))],
        out_shape=[jax.ShapeDtypeStruct((t, k), F32), jax.ShapeDtypeStruct((k, d), BF16)],
        scratch_shapes=[pltpu.VMEM((k, d), F32)],
        compiler_params=_params("arbitrary"),
    )(dh, y, w, dep)


def _in_proj_bwd_x(dp, w, h, g, dh_out, dep, *, tm):
    t, d = h.shape
    nd, _, nb = w.shape
    nt = t // tm

    def body(dp_ref, w_ref, h_ref, g_ref, dho_ref, dep_ref, dh_ref, dg_ref, acc):
        i, j = pl.program_id(0), pl.program_id(1)
        part = lax.dot_general(dp_ref[...], w_ref[0], (((1,), (1,)), ((), ())), preferred_element_type=F32)

        @pl.when(j == 0)
        def _():
            acc[...] = part

        @pl.when(j > 0)
        def _():
            acc[...] += part

        @pl.when(j == nd - 1)
        def _():
            x = h_ref[...]
            r = lax.rsqrt(jnp.mean(x * x, axis=-1, keepdims=True) + EPS_RMS)
            dy = acc[...]
            gd = dy * g_ref[...]
            m = jnp.mean(gd * x, axis=-1, keepdims=True)
            dh_ref[...] = dho_ref[...] + r * gd - x * (r * r * r * m)
            dgp = jnp.sum(dy * x * r, axis=0, keepdims=True)

            @pl.when(i == 0)
            def _():
                dg_ref[...] = dgp

            @pl.when(i > 0)
            def _():
                dg_ref[...] += dgp

    return pl.pallas_call(
        body, name="in_proj_bwd_x", grid=(nt, nd),
        in_specs=[pl.BlockSpec((tm, nb), lambda i, j: (i, j)), pl.BlockSpec((1, d, nb), lambda i, j: (j, 0, 0)),
                  pl.BlockSpec((tm, d), lambda i, j: (i, 0)), _full((1, d)), pl.BlockSpec((tm, d), lambda i, j: (i, 0)),
                  ANY],
        out_specs=[pl.BlockSpec((tm, d), lambda i, j: (i, 0)), _full((1, d))],
        out_shape=[jax.ShapeDtypeStruct((t, d), F32), jax.ShapeDtypeStruct((1, d), F32)],
        scratch_shapes=[pltpu.VMEM((tm, d), F32)],
        compiler_params=_params("arbitrary", "arbitrary"),
    )(dp, w, h, g, dh_out, dep)


def _in_proj_bwd_w(hn, dp, nd, *, tm):
    t, d = hn.shape
    nb = dp.shape[1] // nd
    nt = t // tm

    def body(hn_ref, dp_ref, dw_ref, acc):
        i = pl.program_id(1)
        part = lax.dot_general(hn_ref[...], dp_ref[...], (((0,), (0,)), ((), ())), preferred_element_type=F32)

        @pl.when(i == 0)
        def _():
            acc[...] = part

        @pl.when(i > 0)
        def _():
            acc[...] += part

        @pl.when(i == nt - 1)
        def _():
            dw_ref[0] = acc[...].astype(BF16)

    return pl.pallas_call(
        body, name="in_proj_bwd_w", grid=(nd, nt),
        in_specs=[pl.BlockSpec((tm, d), lambda j, i: (i, 0)), pl.BlockSpec((tm, nb), lambda j, i: (i, j))],
        out_specs=pl.BlockSpec((1, d, nb), lambda j, i: (j, 0, 0)),
        out_shape=jax.ShapeDtypeStruct((nd, d, nb), BF16),
        scratch_shapes=[pltpu.VMEM((d, nb), F32)],
        compiler_params=_params("arbitrary", "arbitrary"),
    )(hn, dp)


def _loss_head(h, g, target, *, tm):
    t, d = h.shape
    nt = t // tm

    def body(h_ref, g_ref, t_ref, loss_ref, dh_ref, dg_ref):
        i = pl.program_id(0)
        x = h_ref[...]
        r = lax.rsqrt(jnp.mean(x * x, axis=-1, keepdims=True) + EPS_RMS)
        xr = x * r
        err = xr * g_ref[...] - t_ref[...]
        lp = 0.5 * jnp.sum(jnp.mean(err * err, axis=-1, keepdims=True), axis=0, keepdims=True)
        dy = err * (1.0 / d)
        gd = dy * g_ref[...]
        m = jnp.mean(gd * x, axis=-1, keepdims=True)
        dh_ref[...] = r * gd - x * (r * r * r * m)
        dgp = jnp.sum(dy * xr, axis=0, keepdims=True)

        @pl.when(i == 0)
        def _():
            loss_ref[...] = lp
            dg_ref[...] = dgp

        @pl.when(i > 0)
        def _():
            loss_ref[...] += lp
            dg_ref[...] += dgp

    return pl.pallas_call(
        body, name="loss_head", grid=(nt,),
        in_specs=[pl.BlockSpec((tm, d), lambda i: (i, 0)), _full((1, d)), pl.BlockSpec((tm, d), lambda i: (i, 0))],
        out_specs=[_full((1, 1)), pl.BlockSpec((tm, d), lambda i: (i, 0)), _full((1, d))],
        out_shape=[jax.ShapeDtypeStruct((1, 1), F32), jax.ShapeDtypeStruct((t, d), F32),
                   jax.ShapeDtypeStruct((1, d), F32)],
        compiler_params=_params("arbitrary"),
    )(h, g, target)


def _col(tm, w, c):
    return pl.BlockSpec((tm, w), lambda i: (i, c))


def _prev_halo(tm, rows, w, c):
    per = tm // rows
    return pl.BlockSpec((rows, w), lambda i: (jnp.maximum(i * per - 1, 0), c))


def _next_halo(tm, rows, w, c, t):
    per = tm // rows
    last = t // rows - 1
    return pl.BlockSpec((rows, w), lambda i: (jnp.minimum((i + 1) * per, last), c))


def _inv_count(first_row, rows, window):
    tpos = first_row + lax.broadcasted_iota(jnp.int32, (rows, 1), 0)
    return 1.0 / jnp.minimum(tpos + 1, window).astype(F32)


def _even_fwd(p, cw, cb, lg, lb, pw, pb, ps, *, tm):
    t = p.shape[0]
    wc = W_CONV

    def body(av, ag, agate, bv, bgate, avh, agh, bvh, cw_ref, cb_ref, lg_ref, lb_ref, pw_ref, pb_ref, ps_ref,
             y_ref, u1_ref, e_ref, d_ref, uext, vext):
        i = pl.program_id(0)
        keep = (i > 0).astype(F32)
        uext[0:HALO, :] = keep * (avh[...] * _sigmoid(agh[...]))
        uext[HALO:, :] = av[...] * _sigmoid(ag[...])
        vext[0:HALO, :] = keep * bvh[...]
        vext[HALO:, :] = bv[...]
        for c in range(0, wc, 128):
            acc = jnp.broadcast_to(cb_ref[:, c:c + 128], (tm, 128))
            for k in range(CONV_K):
                acc = acc + cw_ref[k:k + 1, c:c + 128] * uext[pl.ds(HALO - (CONV_K - 1) + k, tm), c:c + 128]
            u1_ref[:, c:c + 128] = acc
        u1 = u1_ref[...]
        mu = jnp.mean(u1, axis=-1, keepdims=True)
        xc = u1 - mu
        rs = lax.rsqrt(jnp.mean(xc * xc, axis=-1, keepdims=True) + EPS_LN)
        u2 = (xc * rs) * lg_ref[...] + lb_ref[...]
        u3 = u2 * _sigmoid(u2)
        ga = agate[...]
        y_ref[:, 0:wc] = (u3 * (ga * _sigmoid(ga))).astype(BF16)
        for g, win in enumerate(POOL_WINDOWS):
            cs = slice(g * POOL_GW, (g + 1) * POOL_GW)
            s = vext[pl.ds(HALO, tm), cs]
            for j in range(1, win):
                s = s + vext[pl.ds(HALO - j, tm), cs]
            dg = s * _inv_count(i * tm, tm, win) - vext[pl.ds(HALO, tm), cs]
            dgb = dg.astype(BF16)
            d_ref[:, cs] = dgb
            eg = jnp.dot(dgb, pw_ref[g], preferred_element_type=F32) + pb_ref[:, cs]
            e_ref[:, cs] = eg
            gb = bgate[:, cs]
            y_ref[:, wc + g * POOL_GW:wc + (g + 1) * POOL_GW] = ((eg * ps_ref[:, cs]) * (gb * _sigmoid(gb))).astype(BF16)

    row = lambda w: pl.BlockSpec((tm, w), lambda i: (i, 0))
    return pl.pallas_call(
        body, name="even_fwd", grid=(t // tm,),
        in_specs=[_col(tm, wc, 0), _col(tm, wc, 1), _col(tm, wc, 2), _col(tm, wc, 3), _col(tm, wc, 4),
                  _prev_halo(tm, HALO, wc, 0), _prev_halo(tm, HALO, wc, 1), _prev_halo(tm, HALO, wc, 3),
                  _full((32, wc)), _full((1, wc)), _full((1, wc)), _full((1, wc)),
                  _full((4, POOL_GW, POOL_GW)), _full((1, wc)), _full((1, wc))],
        out_specs=[row(2 * wc), row(wc), row(wc), row(wc)],
        out_shape=[jax.ShapeDtypeStruct((t, 2 * wc), BF16), jax.ShapeDtypeStruct((t, wc), F32),
                   jax.ShapeDtypeStruct((t, wc), F32), jax.ShapeDtypeStruct((t, wc), BF16)],
        scratch_shapes=[pltpu.VMEM((tm + HALO, wc), F32), pltpu.VMEM((tm + HALO, wc), F32)],
        compiler_params=_params("arbitrary"),
    )(p, p, p, p, p, p, p, p, cw, cb, lg, lb, pw, pb, ps)


def _acc_out(i, ref, val):
    @pl.when(i == 0)
    def _():
        ref[...] = val

    @pl.when(i > 0)
    def _():
        ref[...] += val


def _even_bwd_a(p, u1, e, dmat, dy, lg, lb, pw, ps, dep, *, tm):
    t = p.shape[0]
    wc = W_CONV

    def body(agate, bgate, u1_ref, e_ref, d_ref, dya, dyb, lg_ref, lb_ref, pw_ref, ps_ref, dep_ref,
             du1_ref, dd_ref, dgat_ref, dlg_ref, dlb_ref, dpb_ref, dps_ref, dpw_ref):
        i = pl.program_id(0)

        @pl.when(i == 0)
        def _():
            dpw_ref[...] = jnp.zeros_like(dpw_ref)

        u1 = u1_ref[...]
        mu = jnp.mean(u1, axis=-1, keepdims=True)
        xc = u1 - mu
        rs = lax.rsqrt(jnp.mean(xc * xc, axis=-1, keepdims=True) + EPS_LN)
        xh = xc * rs
        u2 = xh * lg_ref[...] + lb_ref[...]
        s2 = _sigmoid(u2)
        ga = agate[...]
        sa = _sigmoid(ga)
        dy_a = dya[...]
        dgat_ref[:, 0:wc] = (dy_a * (u2 * s2) * _dsilu(ga, sa)).astype(BF16)
        du2 = dy_a * (ga * sa) * _dsilu(u2, s2)
        _acc_out(i, dlg_ref, jnp.sum(du2 * xh, axis=0, keepdims=True))
        _acc_out(i, dlb_ref, jnp.sum(du2, axis=0, keepdims=True))
        dxh = du2 * lg_ref[...]
        m1 = jnp.mean(dxh, axis=-1, keepdims=True)
        m2 = jnp.mean(dxh * xh, axis=-1, keepdims=True)
        du1_ref[...] = rs * (dxh - m1 - xh * m2)

        gb = bgate[...]
        sb = _sigmoid(gb)
        ev = e_ref[...]
        dy_b = dyb[...]
        dgat_ref[:, wc:2 * wc] = (dy_b * (ev * ps_ref[...]) * _dsilu(gb, sb)).astype(BF16)
        dz = dy_b * (gb * sb)
        _acc_out(i, dps_ref, jnp.sum(dz * ev, axis=0, keepdims=True))
        de = dz * ps_ref[...]
        _acc_out(i, dpb_ref, jnp.sum(de, axis=0, keepdims=True))
        for g in range(len(POOL_WINDOWS)):
            cs = slice(g * POOL_GW, (g + 1) * POOL_GW)
            deg = de[:, cs].astype(BF16)
            dd_ref[:, cs] = lax.dot_general(deg, pw_ref[g], (((1,), (1,)), ((), ())), preferred_element_type=F32)
            dpw_ref[g] += lax.dot_general(d_ref[:, cs], deg, (((0,), (0,)), ((), ())), preferred_element_type=F32)

    row = lambda w: pl.BlockSpec((tm, w), lambda i: (i, 0))
    return pl.pallas_call(
        body, name="even_bwd_a", grid=(t // tm,),
        in_specs=[_col(tm, wc, 2), _col(tm, wc, 4), row(wc), row(wc), row(wc), _col(tm, wc, 0), _col(tm, wc, 1),
                  _full((1, wc)), _full((1, wc)), _full((4, POOL_GW, POOL_GW)), _full((1, wc)), ANY],
        out_specs=[row(wc), row(wc), row(2 * wc), _full((1, wc)), _full((1, wc)), _full((1, wc)), _full((1, wc)),
                   _full((4, POOL_GW, POOL_GW))],
        out_shape=[jax.ShapeDtypeStruct((t, wc), F32), jax.ShapeDtypeStruct((t, wc), F32),
                   jax.ShapeDtypeStruct((t, 2 * wc), BF16)] + [jax.ShapeDtypeStruct((1, wc), F32)] * 4
                  + [jax.ShapeDtypeStruct((4, POOL_GW, POOL_GW), F32)],
        compiler_params=_params("arbitrary"),
    )(p, p, u1, e, dmat, dy, dy, lg, lb, pw, ps, dep)


def _even_bwd_b(p, du1, dd, dgat, cw, *, tm):
    t = p.shape[0]
    wc = W_CONV
    nt = t // tm

    def body(av, ag, avh, agh, du1_ref, du1n, dd_ref, ddn, dgat_ref, cw_ref, dp_ref, dcw_ref, dcb_ref,
             uext, gext, dext, du0):
        i = pl.program_id(0)
        keep_p = (i > 0).astype(F32)
        keep_n = (i < nt - 1).astype(F32)

        @pl.when(i == 0)
        def _():
            dcw_ref[...] = jnp.zeros_like(dcw_ref)

        a = av[...]
        sg = _sigmoid(ag[...])
        uext[0:HALO, :] = keep_p * (avh[...] * _sigmoid(agh[...]))
        uext[HALO:, :] = a * sg
        gext[0:tm, :] = du1_ref[...]
        gext[tm:, :] = keep_n * du1n[...]
        for c in range(0, wc, 128):
            acc = jnp.zeros((tm, 128), F32)
            for s in range(CONV_K):
                k = CONV_K - 1 - s
                acc = acc + cw_ref[k:k + 1, c:c + 128] * gext[pl.ds(s, tm), c:c + 128]
            du0[:, c:c + 128] = acc
            gcur = du1_ref[:, c:c + 128]
            for k in range(CONV_K):
                dcw_ref[k:k + 1, c:c + 128] += jnp.sum(
                    gcur * uext[pl.ds(HALO - (CONV_K - 1) + k, tm), c:c + 128], axis=0, keepdims=True)

        _acc_out(i, dcb_ref, jnp.sum(du1_ref[...], axis=0, keepdims=True))
        g0 = du0[...]
        dp_ref[:, 0:wc] = (g0 * sg).astype(BF16)
        dp_ref[:, wc:2 * wc] = (g0 * a * sg * (1.0 - sg)).astype(BF16)
        dp_ref[:, 2 * wc:3 * wc] = dgat_ref[:, 0:wc]
        dp_ref[:, 4 * wc:5 * wc] = dgat_ref[:, wc:2 * wc]
        for g, win in enumerate(POOL_WINDOWS):
            cs = slice(g * POOL_GW, (g + 1) * POOL_GW)
            dext[0:tm, cs] = dd_ref[:, cs] * _inv_count(i * tm, tm, win)
            dext[tm:, cs] = keep_n * (ddn[:, cs] * _inv_count((i + 1) * tm, HALO, win))
            s = dext[pl.ds(0, tm), cs]
            for j in range(1, win):
                s = s + dext[pl.ds(j, tm), cs]
            dp_ref[:, 3 * wc + g * POOL_GW:3 * wc + (g + 1) * POOL_GW] = (s - dd_ref[:, cs]).astype(BF16)

    row = lambda w: pl.BlockSpec((tm, w), lambda i: (i, 0))
    return pl.pallas_call(
        body, name="even_bwd_b", grid=(nt,),
        in_specs=[_col(tm, wc, 0), _col(tm, wc, 1), _prev_halo(tm, HALO, wc, 0), _prev_halo(tm, HALO, wc, 1),
                  row(wc), _next_halo(tm, HALO, wc, 0, t), row(wc), _next_halo(tm, HALO, wc, 0, t), row(2 * wc),
                  _full((32, wc))],
        out_specs=[row(5 * wc), _full((32, wc)), _full((1, wc))],
        out_shape=[jax.ShapeDtypeStruct((t, 5 * wc), BF16), jax.ShapeDtypeStruct((32, wc), F32),
                   jax.ShapeDtypeStruct((1, wc), F32)],
        scratch_shapes=[pltpu.VMEM((tm + HALO, wc), F32), pltpu.VMEM((tm + HALO, wc), F32),
                        pltpu.VMEM((tm + HALO, wc), F32), pltpu.VMEM((tm, wc), F32)],
        compiler_params=_params("arbitrary"),
    )(p, p, p, p, du1, du1, dd, dd, dgat, cw)


def _softplus_neg(lam):
    z = -lam
    return jnp.maximum(z, 0.0) + jnp.log1p(jnp.exp(-jnp.abs(z)))


def _one_minus_exp(x):
    series = -x * (1.0 + x * (0.5 + x * (1.0 / 6.0 + x * (1.0 / 24.0))))
    return jnp.where(x > -0.02, series, 1.0 - jnp.exp(x))


def _odd_fwd(p, ccw, ccb, wrg, brg, wig, big, lam, *, tm):
    t = p.shape[0]
    wl = W_LRU
    ng = tm // 8

    def body(xr, gate, xrh, ccw_ref, ccb_ref, wrg_ref, brg_ref, wig_ref, big_ref, lam_ref,
             y_ref, xc_ref, r_ref, i_ref, hs_ref, xext, a_s, b_s, carry):
        i = pl.program_id(0)
        keep = (i > 0).astype(F32)
        xext[0:HALO_C, :] = keep * xrh[...]
        xext[HALO_C:, :] = xr[...]
        xc = jnp.broadcast_to(ccb_ref[...], (tm, wl))
        for k in range(LRU_CONV_K):
            xc = xc + ccw_ref[k:k + 1, :] * xext[pl.ds(HALO_C - (LRU_CONV_K - 1) + k, tm), :]
        xc_ref[...] = xc
        for h in range(LRU_HEADS):
            cs = slice(h * LRU_HD, (h + 1) * LRU_HD)
            xh = xc_ref[:, cs].astype(BF16)
            r_ref[:, cs] = _sigmoid(jnp.dot(xh, wrg_ref[h], preferred_element_type=F32) + brg_ref[:, cs])
            i_ref[:, cs] = _sigmoid(jnp.dot(xh, wig_ref[h], preferred_element_type=F32) + big_ref[:, cs])
        log_a = (-LRU_C * _softplus_neg(lam_ref[...])) * r_ref[...]
        a_s[...] = jnp.exp(log_a)
        b_s[...] = jnp.sqrt(_one_minus_exp(2.0 * log_a)) * (i_ref[...] * xc_ref[...])

        @pl.when(i == 0)
        def _():
            carry[...] = jnp.zeros_like(carry)

        rowi = lax.broadcasted_iota(jnp.int32, (8, wl), 0)

        def step(g, c):
            sl = pl.ds(pl.multiple_of(g * 8, 8), 8)
            aa, bb = a_s[sl, :], b_s[sl, :]
            for s in (1, 2, 4):
                m = rowi >= s
                a_sh = jnp.where(m, pltpu.roll(aa, s, 0), 1.0)
                b_sh = jnp.where(m, pltpu.roll(bb, s, 0), 0.0)
                bb = aa * b_sh + bb
                aa = aa * a_sh
            hv = bb + aa * c
            hs_ref[sl, :] = hv
            return hv[7:8, :]

        carry[...] = lax.fori_loop(0, ng, step, carry[...])
        gt = gate[...]
        y_ref[...] = (hs_ref[...] * (gt * _sigmoid(gt))).astype(BF16)

    row = lambda w: pl.BlockSpec((tm, w), lambda i: (i, 0))
    return pl.pallas_call(
        body, name="odd_fwd", grid=(t // tm,),
        in_specs=[_col(tm, wl, 0), _col(tm, wl, 1), _prev_halo(tm, HALO_C, wl, 0), _full((8, wl)), _full((1, wl)),
                  _full((LRU_HEADS, LRU_HD, LRU_HD)), _full((1, wl)), _full((LRU_HEADS, LRU_HD, LRU_HD)),
                  _full((1, wl)), _full((1, wl))],
        out_specs=[row(wl)] * 5,
        out_shape=[jax.ShapeDtypeStruct((t, wl), BF16)] + [jax.ShapeDtypeStruct((t, wl), F32)] * 4,
        scratch_shapes=[pltpu.VMEM((tm + HALO_C, wl), F32), pltpu.VMEM((tm, wl), F32), pltpu.VMEM((tm, wl), F32),
                        pltpu.VMEM((1, wl), F32)],
        compiler_params=_params("arbitrary"),
    )(p, p, p, ccw, ccb, wrg, brg, wig, big, lam)


def _odd_bwd_a(p, xc, r, ig, hs, dy, wrg, wig, lam, dep, *, tm):
    t = p.shape[0]
    wl = W_LRU
    nt = t // tm
    ng = tm // 8
    per = tm // HALO_C

    def body(gate, xc_ref, r_ref, i_ref, hs_ref, hsh, dy_ref, wrg_ref, wig_ref, lam_ref, dep_ref,
             dxc_ref, dgate_ref, dwrg_ref, dwig_ref, dbrg_ref, dbig_ref, dlam_ref,
             hext, a_s, q_s, g_s, dpr_s, dpi_s, carry):
        i = pl.program_id(0)
        ti = nt - 1 - i
        keep = (ti > 0).astype(F32)
        hext[0:HALO_C, :] = keep * hsh[...]
        hext[HALO_C:, :] = hs_ref[...]
        gt = gate[...]
        sg = _sigmoid(gt)
        dyv = dy_ref[...]
        dgate_ref[...] = (dyv * hs_ref[...] * _dsilu(gt, sg)).astype(BF16)
        q_s[...] = dyv * (gt * sg)
        sp = _softplus_neg(lam_ref[...])
        log_a = (-LRU_C * sp) * r_ref[...]
        a_s[...] = jnp.exp(log_a)

        @pl.when(i == 0)
        def _():
            carry[...] = jnp.zeros_like(carry)
            dwrg_ref[...] = jnp.zeros_like(dwrg_ref)
            dwig_ref[...] = jnp.zeros_like(dwig_ref)

        rowi = lax.broadcasted_iota(jnp.int32, (8, wl), 0)

        def step(gr, c):
            sl = pl.ds(pl.multiple_of((ng - 1 - gr) * 8, 8), 8)
            a0 = a_s[sl, :]
            al = jnp.where(rowi < 7, pltpu.roll(a0, 7, 0), 1.0)
            be = q_s[sl, :]
            for s in (1, 2, 4):
                m = rowi + s <= 7
                al_sh = jnp.where(m, pltpu.roll(al, 8 - s, 0), 1.0)
                be_sh = jnp.where(m, pltpu.roll(be, 8 - s, 0), 0.0)
                be = be + al * be_sh
                al = al * al_sh
            gv = be + al * c
            g_s[sl, :] = gv
            return (a0 * gv)[0:1, :]

        carry[...] = lax.fori_loop(0, ng, step, carry[...])

        gv = g_s[...]
        a = a_s[...]
        mult = jnp.sqrt(_one_minus_exp(2.0 * log_a))
        iv = i_ref[...]
        rv = r_ref[...]
        xcv = xc_ref[...]
        hprev = hext[pl.ds(HALO_C - 1, tm), :]
        dla = gv * hprev * a - (gv * iv * xcv) * (a * a) / mult
        di = gv * mult * xcv
        dpr = (dla * (-LRU_C * sp)) * rv * (1.0 - rv)
        dpi = di * iv * (1.0 - iv)
        dpr_s[...] = dpr
        dpi_s[...] = dpi
        dxc_ref[...] = gv * mult * iv
        dsp = jnp.sum(dla * rv, axis=0, keepdims=True) * (-LRU_C)
        _acc_out(i, dlam_ref, -dsp * _sigmoid(-lam_ref[...]))
        _acc_out(i, dbrg_ref, jnp.sum(dpr, axis=0, keepdims=True))
        _acc_out(i, dbig_ref, jnp.sum(dpi, axis=0, keepdims=True))
        for h in range(LRU_HEADS):
            cs = slice(h * LRU_HD, (h + 1) * LRU_HD)
            xh = xc_ref[:, cs].astype(BF16)
            dr_h = dpr_s[:, cs].astype(BF16)
            di_h = dpi_s[:, cs].astype(BF16)
            dxc_ref[:, cs] += (
                lax.dot_general(dr_h, wrg_ref[h], (((1,), (1,)), ((), ())), preferred_element_type=F32)
                + lax.dot_general(di_h, wig_ref[h], (((1,), (1,)), ((), ())), preferred_element_type=F32))
            dwrg_ref[h] += lax.dot_general(xh, dr_h, (((0,), (0,)), ((), ())), preferred_element_type=F32)
            dwig_ref[h] += lax.dot_general(xh, di_h, (((0,), (0,)), ((), ())), preferred_element_type=F32)

    rrow = lambda w: pl.BlockSpec((tm, w), lambda i: (nt - 1 - i, 0))
    hspec = pl.BlockSpec((HALO_C, wl), lambda i: (jnp.maximum((nt - 1 - i) * per - 1, 0), 0))
    wspec = _full((LRU_HEADS, LRU_HD, LRU_HD))
    return pl.pallas_call(
        body, name="odd_bwd_a", grid=(nt,),
        in_specs=[pl.BlockSpec((tm, wl), lambda i: (nt - 1 - i, 1)), rrow(wl), rrow(wl), rrow(wl), rrow(wl), hspec,
                  rrow(wl), wspec, wspec, _full((1, wl)), ANY],
        out_specs=[rrow(wl), rrow(wl), wspec, wspec, _full((1, wl)), _full((1, wl)), _full((1, wl))],
        out_shape=[jax.ShapeDtypeStruct((t, wl), F32), jax.ShapeDtypeStruct((t, wl), BF16),
                   jax.ShapeDtypeStruct((LRU_HEADS, LRU_HD, LRU_HD), F32),
                   jax.ShapeDtypeStruct((LRU_HEADS, LRU_HD, LRU_HD), F32)] + [jax.ShapeDtypeStruct((1, wl), F32)] * 3,
        scratch_shapes=[pltpu.VMEM((tm + HALO_C, wl), F32)] + [pltpu.VMEM((tm, wl), F32)] * 5
                       + [pltpu.VMEM((1, wl), F32)],
        compiler_params=_params("arbitrary"),
    )(p, xc, r, ig, hs, hs, dy, wrg, wig, lam, dep)


def _odd_bwd_b(p, dxc, dgate, ccw, *, tm):
    t = p.shape[0]
    wl = W_LRU
    nt = t // tm

    def body(xr, xrh, dxc_ref, dxcn, dgate_ref, ccw_ref, dp_ref, dcw_ref, dcb_ref, xext, gext):
        i = pl.program_id(0)

        @pl.when(i == 0)
        def _():
            dcw_ref[...] = jnp.zeros_like(dcw_ref)

        xext[0:HALO_C, :] = (i > 0).astype(F32) * xrh[...]
        xext[HALO_C:, :] = xr[...]
        gext[0:tm, :] = dxc_ref[...]
        gext[tm:, :] = (i < nt - 1).astype(F32) * dxcn[...]
        g = dxc_ref[...]
        acc = jnp.zeros((tm, wl), F32)
        for k in range(LRU_CONV_K):
            acc = acc + ccw_ref[k:k + 1, :] * gext[pl.ds(LRU_CONV_K - 1 - k, tm), :]
            dcw_ref[k:k + 1, :] += jnp.sum(
                g * xext[pl.ds(HALO_C - (LRU_CONV_K - 1) + k, tm), :], axis=0, keepdims=True)

        _acc_out(i, dcb_ref, jnp.sum(g, axis=0, keepdims=True))
        dp_ref[:, 0:wl] = acc.astype(BF16)
        dp_ref[:, wl:2 * wl] = dgate_ref[...]

    row = lambda w: pl.BlockSpec((tm, w), lambda i: (i, 0))
    return pl.pallas_call(
        body, name="odd_bwd_b", grid=(nt,),
        in_specs=[_col(tm, wl, 0), _prev_halo(tm, HALO_C, wl, 0), row(wl), _next_halo(tm, HALO_C, wl, 0, t), row(wl),
                  _full((8, wl))],
        out_specs=[row(2 * wl), _full((8, wl)), _full((1, wl))],
        out_shape=[jax.ShapeDtypeStruct((t, 2 * wl), BF16), jax.ShapeDtypeStruct((8, wl), F32),
                   jax.ShapeDtypeStruct((1, wl), F32)],
        scratch_shapes=[pltpu.VMEM((tm + HALO_C, wl), F32), pltpu.VMEM((tm + HALO_C, wl), F32)],
        compiler_params=_params("arbitrary"),
    )(p, p, dxc, dxc, dgate, ccw)


def _local_step(x, target, layer_weights, final_norm, on_grads):
    tm, tx = TM_MATMUL, TM_MIXER
    h = x
    saved = []
    for layer in range(N_LAYERS):
        w = layer_weights(layer, h)
        p, hn = _norm_matmul(h, w["norm"], w["w_in"], tm=tm)
        if layer % 2 == 0:
            y, *acts = _even_fwd(p, w["conv_w"], w["conv_b"], w["ln_g"], w["ln_b"], w["pool_w"], w["pool_b"],
                                 w["pool_scale"], tm=tx)
        else:
            y, *acts = _odd_fwd(p, w["conv_w"], w["conv_b"], w["w_rg"], w["b_rg"], w["w_ig"], w["b_ig"], w["lam"],
                                tm=tx)
        w_out = w["w_out"](y)
        saved.append((w, w_out, h, p, hn, y, acts))
        h = _out_proj(h, y, w_out, tm=tm)
    loss, dh, d_final = _loss_head(h, final_norm, target, tm=tm)

    dep = d_final
    for layer in reversed(range(N_LAYERS)):
        w, w_out, h_in, p, hn, y, acts = saved[layer]
        sfx = "even" if layer % 2 == 0 else "odd"
        dy, dw_out = _out_proj_bwd(dh, y, w_out, dep, tm=tm)
        dep = on_grads(layer, {"w_out_" + sfx: dw_out}, dep, False)
        if layer % 2 == 0:
            u1, e, dmat = acts
            du1, dd, dgat, dlg, dlb, dpb, dps, dpw = _even_bwd_a(p, u1, e, dmat, dy, w["ln_g"], w["ln_b"], w["pool_w"],
                                                                 w["pool_scale"], dep, tm=tx)
            dp, dcw, dcb = _even_bwd_b(p, du1, dd, dgat, w["conv_w"], tm=tx)
            grads = dict(conv_a_w=dcw[:CONV_K], conv_a_b=dcb, ln_a_g=dlg, ln_a_b=dlb, pool_w=dpw, pool_b=dpb,
                         pool_scale=dps)
        else:
            xc, r, ig, hs = acts
            dxc, dgate, dwrg, dwig, dbrg, dbig, dlam = _odd_bwd_a(p, xc, r, ig, hs, dy, w["w_rg"], w["w_ig"],
                                                                  w["lam"], dep, tm=tx)
            dp, dccw, dccb = _odd_bwd_b(p, dxc, dgate, w["conv_w"], tm=tx)
            grads = dict(conv_c_w=dccw[:LRU_CONV_K], conv_c_b=dccb, w_rg=dwrg, b_rg=dbrg, w_ig=dwig, b_ig=dbig,
                         lru_lambda=dlam)
        dep = on_grads(layer, {"w_in_" + sfx: _in_proj_bwd_w(hn, dp, N_DEV, tm=tm)}, dep, False)
        dh, grads["norm_" + sfx] = _in_proj_bwd_x(dp, w["w_in"], h_in, w["norm"], dh, dep, tm=tm)
        if layer == N_LAYERS - 1:
            grads["final_norm"] = d_final
        dep = on_grads(layer, grads, dep, True)
    return loss, dh


def _slot(px, py, pc):
    return 4 * px + 2 * py + pc


def _peers(x, y, c):
    return [(1 - x if k & 4 else x, 1 - y if k & 2 else y, 1 - c if k & 1 else c) for k in range(1, N_DEV)]


def _all_gather(arrs, name):
    n = len(arrs)

    def body(*refs):
        ins, outs = refs[:n], refs[n:2 * n]
        send_sems, recv_sems, local_sems = refs[2 * n:]
        x, y, c = lax.axis_index("x"), lax.axis_index("y"), lax.axis_index("c")
        me, sibling = (x, y, c), (x, y, 1 - c)
        chips = [(1 - x, y), (x, 1 - y), (1 - x, 1 - y)]

        def copy(a, k, block, to, src=None):
            rows = outs[a].at[_slot(*block)]
            return pltpu.make_async_remote_copy(
                src_ref=rows if src is None else src, dst_ref=rows, send_sem=send_sems.at[a, k],
                recv_sem=recv_sems.at[a, k], device_id=to, device_id_type=MESH)

        mine = [pltpu.make_async_copy(ins[a], outs[a].at[_slot(*me)], local_sems.at[a]) for a in range(n)]
        for cp in mine:
            cp.start()
        first = []
        for a in range(n):
            first.append(copy(a, 0, me, sibling, src=ins[a]))
            first += [copy(a, 1 + j, me, (*chip, c), src=ins[a]) for j, chip in enumerate(chips)]
        for cp in first:
            cp.start()
        passed = []
        for j, chip in enumerate(chips):
            for a in range(n):
                copy(a, 1 + j, (*chip, c), me).wait_recv()
                fwd = copy(a, 4 + j, (*chip, c), sibling)
                fwd.start()
                passed.append(fwd)
        for a in range(n):
            copy(a, 0, sibling, me).wait_recv()
            for j, chip in enumerate(chips):
                copy(a, 4 + j, (*chip, 1 - c), me).wait_recv()
        for cp in first + passed:
            cp.wait_send()
        for cp in mine:
            cp.wait()

    return pl.pallas_call(
        body, name=name,
        in_specs=[ANY] * n, out_specs=[ANY] * n,
        out_shape=[jax.ShapeDtypeStruct((N_DEV,) + a.shape, a.dtype) for a in arrs],
        scratch_shapes=[pltpu.SemaphoreType.DMA((n, 7)), pltpu.SemaphoreType.DMA((n, 7)),
                        pltpu.SemaphoreType.DMA((n,))],
    )(*arrs)


def _exchange_copy(scatter, src_ref, land_ref, peer, send_sem, recv_sem, landing):
    return pltpu.make_async_remote_copy(
        src_ref=src_ref.at[_slot(*peer)] if scatter else src_ref, dst_ref=land_ref.at[landing],
        send_sem=send_sem, recv_sem=recv_sem, device_id=peer, device_id_type=MESH)


def _exchange_start(groups, scatter, deps, name):
    flat = [pair for g in groups for pair in g]
    n, ng = len(flat), len(groups)

    def body(*refs):
        src_refs, land_refs = refs[:n], refs[n:2 * n]
        outs = refs[2 * n + len(deps):]
        sems, token = outs[:2 * ng], outs[2 * ng + 2 * n]
        x, y, c = lax.axis_index("x"), lax.axis_index("y"), lax.axis_index("c")
        me = _slot(x, y, c)
        base = 0
        for gi, g in enumerate(groups):
            for k, peer in enumerate(_peers(x, y, c)):
                for ai in range(len(g)):
                    _exchange_copy(scatter, src_refs[base + ai], land_refs[base + ai], peer,
                                   sems[2 * gi].at[ai * N_PEERS + k], sems[2 * gi + 1].at[ai * N_PEERS + k],
                                   me).start()
            base += len(g)
        token[...] = jnp.zeros_like(token)

    operands = [pltpu.with_memory_space_constraint(a, pltpu.HBM) for a in
                [s for s, _ in flat] + [l for _, l in flat]]
    out_shape = []
    for g in groups:
        out_shape += [pltpu.SemaphoreType.DMA((len(g) * N_PEERS,))] * 2
    out_shape += [pltpu.HBM(a.shape, a.dtype) for a in operands]
    out_shape.append(jax.ShapeDtypeStruct((8, 128), F32))
    outs = pl.pallas_call(
        body, name=name, out_shape=out_shape,
        in_specs=[HBM] * (2 * n) + [ANY] * len(deps),
        out_specs=[SEM] * (2 * ng) + [HBM] * (2 * n) + [pl.BlockSpec(memory_space=pltpu.VMEM)],
        input_output_aliases={i: 2 * ng + i for i in range(2 * n)},
        compiler_params=pltpu.CompilerParams(has_side_effects=pltpu.SideEffectType.DATAFLOW_SIDE_EFFECTING),
    )(*operands, *deps)
    handles, base = [], 0
    for gi, g in enumerate(groups):
        srcs = outs[2 * ng + base:2 * ng + base + len(g)]
        lands = outs[2 * ng + n + base:2 * ng + n + base + len(g)]
        handles.append((outs[2 * gi], outs[2 * gi + 1], list(srcs), list(lands)))
        base += len(g)
    return handles, outs[-1]


def _exchange_wait(handle, scatter, after, name):
    send_sems, recv_sems, srcs, lands = handle
    n = len(srcs)

    def body(*refs):
        src_refs, land_refs = refs[:n], refs[n:2 * n]
        send_ref, recv_ref = refs[2 * n], refs[2 * n + 1]
        x, y, c = lax.axis_index("x"), lax.axis_index("y"), lax.axis_index("c")
        for k, peer in enumerate(_peers(x, y, c)):
            for a in range(n):
                cp = _exchange_copy(scatter, src_refs[a], land_refs[a], peer, send_ref.at[a * N_PEERS + k],
                                    recv_ref.at[a * N_PEERS + k],
                                    _slot(*peer))
                cp.wait_send()
                cp.wait_recv()

    outs = pl.pallas_call(
        body, name=name,
        out_shape=[pltpu.HBM(a.shape, a.dtype) for a in srcs + lands],
        in_specs=[HBM] * (2 * n) + [SEM, SEM] + [ANY] * len(after),
        out_specs=[HBM] * (2 * n),
        input_output_aliases={i: i for i in range(2 * n)},
        compiler_params=pltpu.CompilerParams(has_side_effects=pltpu.SideEffectType.DATAFLOW_SIDE_EFFECTING),
    )(*srcs, *lands, send_sems, recv_sems, *after)
    return list(outs[n:])


def _sum_parts(parts, *, tr):
    np_, r, c = parts.shape

    def body(p_ref, o_ref):
        acc = p_ref[0].astype(F32)
        for k in range(1, np_):
            acc = acc + p_ref[k].astype(F32)
        o_ref[...] = acc

    return pl.pallas_call(
        body, name="sum_parts", grid=(r // tr,),
        in_specs=[pl.BlockSpec((np_, tr, c), lambda i: (0, i, 0))],
        out_specs=pl.BlockSpec((tr, c), lambda i: (i, 0)),
        out_shape=jax.ShapeDtypeStruct((r, c), F32),
        compiler_params=_params("arbitrary"),
    )(parts)


def _adamw_math(w, g, m, v):
    c1 = 1.0 - ADAM_B1 ** ADAM_STEP
    c2 = 1.0 - ADAM_B2 ** ADAM_STEP
    nm = ADAM_B1 * m + (1.0 - ADAM_B1) * g
    nv = ADAM_B2 * v + (1.0 - ADAM_B2) * (g * g)
    delta = -ADAM_LR * ((nm / c1) / (jnp.sqrt(nv / c2) + ADAM_EPS) + ADAM_WD * w)
    return delta, nm, nv


def _row_tile(r):
    for cand in (256, 128, 64, 32, 16, 8):
        if r % cand == 0 and r > cand:
            return cand
    return r


def _adamw(items, layer0, bufs, name):
    ni = len(items)
    nl = items[0][1].shape[1]
    tiles = [_row_tile(w.shape[1]) for w, _, _, _ in items]
    steps = [w.shape[1] // tr for (w, _, _, _), tr in zip(items, tiles)]
    ns = steps[0]
    assert all(s == ns for s in steps)
    nb = 0 if bufs is None else 4 * ni

    def body(*refs):
        ins, outs = refs[:4 * ni], refs[4 * ni + nb:]
        for k in range(ni):
            w_ref, p_ref, m_ref, v_ref = ins[4 * k:4 * k + 4]
            g = p_ref[0, 0].astype(F32)
            for s in range(1, p_ref.shape[0]):
                g = g + p_ref[s, 0].astype(F32)
            delta, nm, nv = _adamw_math(w_ref[0], g, m_ref[0], v_ref[0])
            g_ref, d_ref, nm_ref, nv_ref = outs[4 * k:4 * k + 4]
            g_ref[0], d_ref[0], nm_ref[0], nv_ref[0] = g, delta, nm, nv

    in_specs, out_specs, out_shape, operands = [], [], [], []
    for (w, parts, m, v), tr in zip(items, tiles):
        blk = pl.BlockSpec((1, tr, w.shape[2]), lambda l, i: (layer0 + l, i, 0))
        in_specs += [blk, pl.BlockSpec((parts.shape[0], 1, tr, w.shape[2]), lambda l, i: (0, l, i, 0)), blk, blk]
        operands += [w, parts, m, v]
        out_specs += [blk] * 4
        out_shape += [jax.ShapeDtypeStruct(w.shape, F32)] * 4
    if bufs is not None:
        in_specs += [ANY] * nb
        operands += [b for item in bufs for b in item]
    outs = pl.pallas_call(
        body, name=name, grid=(nl, ns), in_specs=in_specs, out_specs=out_specs, out_shape=out_shape,
        input_output_aliases={4 * ni + i: i for i in range(nb)},
        compiler_params=_params("arbitrary", "arbitrary"),
    )(*operands)
    return [tuple(outs[4 * k:4 * k + 4]) for k in range(ni)]


NAMES = ("norm_even", "w_in_even", "conv_a_w", "conv_a_b", "ln_a_g", "ln_a_b", "pool_w", "pool_b", "pool_scale",
         "w_out_even", "norm_odd", "w_in_odd", "conv_c_w", "conv_c_b", "w_rg", "b_rg", "w_ig", "b_ig", "lru_lambda",
         "w_out_odd", "final_norm")
SMALL_GATHERED = ("conv_a_w", "pool_b", "norm_odd", "conv_c_w", "conv_c_b", "b_rg", "b_ig", "lru_lambda")
BIG = (("w_in_even", "w_out_even"), ("w_in_odd", "w_out_odd"))
SMALL = (("conv_a_w", "pool_b", "pool_w"), ("norm_odd", "conv_c_w", "conv_c_b", "b_rg", "b_ig", "lru_lambda"))
REPLICATED = (("norm_even", "conv_a_b", "ln_a_g", "ln_a_b", "pool_scale"), ("w_rg", "w_ig"))
PACK_ROW = 1024


def _pack_rows(flat2d):
    pad = (-flat2d.shape[1]) % PACK_ROW
    return jnp.pad(flat2d, ((0, 0), (0, pad))).reshape(flat2d.shape[0], -1, 128)


def _unpack(flat, shapes):
    out, off = [], 0
    for s in shapes:
        n = 1
        for d in s:
            n *= d
        out.append(flat[..., off:off + n].reshape(flat.shape[:-1] + tuple(s)))
        off += n
    return out


def _to_global(name, g):
    if name in ("conv_a_w", "pool_b", "conv_c_w"):
        return jnp.transpose(g, (1, 2, 0, 3)).reshape(g.shape[1], g.shape[2], -1)
    if name == "pool_w":
        return jnp.transpose(g, (1, 2, 0, 3, 4)).reshape(2, 4, POOL_GW, POOL_GW)
    return jnp.transpose(g, (1, 0, 2)).reshape(g.shape[1], -1)


def _to_blocks(name, g):
    if name == "conv_a_w":
        return jnp.transpose(g.reshape(CONV_K, N_DEV, -1), (1, 0, 2))
    if name == "conv_c_w":
        return jnp.transpose(g.reshape(LRU_CONV_K, N_DEV, -1), (1, 0, 2))
    if name == "pool_b":
        return jnp.transpose(g.reshape(4, N_DEV, -1), (1, 0, 2))
    if name == "pool_w":
        return jnp.transpose(g.reshape(4, N_DEV, POOL_GW // N_DEV, POOL_GW), (1, 0, 2, 3))
    return g.reshape(N_DEV, -1)


def _as3d(a):
    if a.ndim == 1:
        return a.reshape(1, 1, -1)
    if a.ndim == 2:
        return a.reshape(a.shape[0], 1, a.shape[1])
    return a.reshape(a.shape[0], -1, a.shape[-1])


def kernel(x, norm_even, w_in_even, conv_a_w, conv_a_b, ln_a_g, ln_a_b, pool_w, pool_b, pool_scale, w_out_even, norm_odd, w_in_odd, conv_c_w, conv_c_b, w_rg, b_rg, w_ig, b_ig, lru_lambda, w_out_odd, final_norm, loss_target, m_norm_even, m_w_in_even, m_conv_a_w, m_conv_a_b, m_ln_a_g, m_ln_a_b, m_pool_w, m_pool_b, m_pool_scale, m_w_out_even, m_norm_odd, m_w_in_odd, m_conv_c_w, m_conv_c_b, m_w_rg, m_b_rg, m_w_ig, m_b_ig, m_lru_lambda, m_w_out_odd, m_final_norm, v_norm_even, v_w_in_even, v_conv_a_w, v_conv_a_b, v_ln_a_g, v_ln_a_b, v_pool_w, v_pool_b, v_pool_scale, v_w_out_even, v_norm_odd, v_w_in_odd, v_conv_c_w, v_conv_c_b, v_w_rg, v_b_rg, v_w_ig, v_b_ig, v_lru_lambda, v_w_out_odd, v_final_norm):
    w_loc = dict(zip(NAMES, [norm_even, w_in_even, conv_a_w, conv_a_b, ln_a_g, ln_a_b, pool_w, pool_b, pool_scale,
                             w_out_even, norm_odd, w_in_odd, conv_c_w, conv_c_b, w_rg, b_rg, w_ig, b_ig, lru_lambda,
                             w_out_odd, final_norm]))
    m_loc = dict(zip(NAMES, [m_norm_even, m_w_in_even, m_conv_a_w, m_conv_a_b, m_ln_a_g, m_ln_a_b, m_pool_w, m_pool_b,
                             m_pool_scale, m_w_out_even, m_norm_odd, m_w_in_odd, m_conv_c_w, m_conv_c_b, m_w_rg,
                             m_b_rg, m_w_ig, m_b_ig, m_lru_lambda, m_w_out_odd, m_final_norm]))
    v_loc = dict(zip(NAMES, [v_norm_even, v_w_in_even, v_conv_a_w, v_conv_a_b, v_ln_a_g, v_ln_a_b, v_pool_w, v_pool_b,
                             v_pool_scale, v_w_out_even, v_norm_odd, v_w_in_odd, v_conv_c_w, v_conv_c_b, v_w_rg,
                             v_b_rg, v_w_ig, v_b_ig, v_lru_lambda, v_w_out_odd, v_final_norm]))
    me = _slot(lax.axis_index("x"), lax.axis_index("y"), lax.axis_index("c"))

    def landing(own):
        zone = lax.empty((N_DEV,) + own.shape[1:], own.dtype)
        return lax.dynamic_update_slice(zone, own, (me,) + (0,) * (own.ndim - 1))

    small_shapes = [w_loc[n].shape for n in SMALL_GATHERED]
    small = jnp.concatenate([w_loc[n].reshape(1, -1) for n in SMALL_GATHERED], axis=1)
    first = _all_gather([w_in_even[0].astype(BF16), pool_w.astype(BF16), _pack_rows(small)[0]], "gather_first")
    g_small = dict(zip(SMALL_GATHERED, [_to_global(n, g) for n, g in
                                        zip(SMALL_GATHERED, _unpack(first[2].reshape(N_DEV, -1), small_shapes))]))
    pool_w_all = _to_global("pool_w", first[1])

    def gather_start(shards, deps, name):
        groups = [[(s.astype(BF16), landing(s.astype(BF16)[None])) for s in g] for g in shards]
        return _exchange_start(groups, False, deps, name)[0]

    gathers = {}
    gathers["w_out_0"], gathers[1] = gather_start([[w_out_even[0]], [w_in_odd[0], w_out_odd[0]]], [first[0]],
                                                  "gather_start_1")
    next_shards = {1: [w_in_even[1], w_out_even[1]], 2: [w_in_odd[1], w_out_odd[1]]}

    def layer_weights(layer, h):
        j = layer // 2
        if layer == 0:
            w_in = first[0]
            w_out = lambda y: _exchange_wait(gathers["w_out_0"], False, [y], "gather_wait_out_0")[0]
        else:
            w_in, w_out_now = _exchange_wait(gathers[layer], False, [h], f"gather_wait_{layer}")
            w_out = lambda y: w_out_now
            if layer in next_shards:
                (gathers[layer + 1],) = gather_start([next_shards[layer]], [w_in], f"gather_start_{layer + 1}")
        row = lambda a: a[j][None]
        if layer % 2 == 0:
            return dict(norm=row(norm_even), w_in=w_in, w_out=lambda y: w_out(y).reshape(W_EVEN_MIX, D_MODEL),
                        conv_w=jnp.pad(g_small["conv_a_w"][j], ((0, 1), (0, 0))), conv_b=row(conv_a_b),
                        ln_g=row(ln_a_g), ln_b=row(ln_a_b), pool_w=pool_w_all[j],
                        pool_b=g_small["pool_b"][j].reshape(1, W_POOL), pool_scale=row(pool_scale))
        return dict(norm=row(g_small["norm_odd"]), w_in=w_in, w_out=lambda y: w_out(y).reshape(W_LRU, D_MODEL),
                    conv_w=jnp.pad(g_small["conv_c_w"][j], ((0, 4), (0, 0))), conv_b=row(g_small["conv_c_b"]),
                    w_rg=w_rg[j].astype(BF16), b_rg=row(g_small["b_rg"]), w_ig=w_ig[j].astype(BF16),
                    b_ig=row(g_small["b_ig"]), lam=row(g_small["lru_lambda"]))

    pending, scatters, small_layout = {}, {}, {}
    last_token = []

    def on_grads(layer, grads, dep, last):
        par = layer % 2
        w_in_name, w_out_name = BIG[par]
        pending.setdefault(layer, {}).update(grads)
        have = pending[layer]
        send = {}
        if w_in_name in have and (layer == 0 or last):
            send["in"] = have.pop(w_in_name)
        if w_out_name in have and (layer == 0 or last):
            send["out"] = have.pop(w_out_name).reshape(N_DEV, -1, D_MODEL)
        if last:
            small_blocks = [_to_blocks(n, have[n]) for n in SMALL[par]]
            rep_names = list(REPLICATED[par]) + (["final_norm"] if "final_norm" in have else [])
            rep = jnp.concatenate([have[n].reshape(-1) for n in rep_names]).reshape(N_DEV, -1)
            small_layout[layer] = ([b.shape[1:] for b in small_blocks], rep.shape[1])
            send["small"] = _pack_rows(jnp.concatenate([b.reshape(N_DEV, -1) for b in small_blocks] + [rep], axis=1))
        if not send:
            return dep
        pairs = [(s, landing(lax.dynamic_slice_in_dim(s, me, 1, 0))) for s in send.values()]
        (handle,), token = _exchange_start([pairs], True, [], f"scatter_start_{layer}_{'_'.join(send)}")
        scatters.setdefault(layer, []).append((list(send), handle))
        last_token[:] = [token]
        return token

    loss, grad_x = _local_step(x[0], loss_target[0], layer_weights, final_norm[None], on_grads)

    w3 = {n: _as3d(w_loc[n]) for n in NAMES}
    m3 = {n: _as3d(m_loc[n]) for n in NAMES}
    v3 = {n: _as3d(v_loc[n]) for n in NAMES}
    results, rep_parts = {}, {}
    after = list(last_token)
    for layer in (3, 2, 1, 0):
        par, j = layer % 2, layer // 2
        got = {}
        for keys, handle in scatters[layer]:
            got.update(zip(keys, _exchange_wait(handle, True, after, f"scatter_wait_{layer}_{'_'.join(keys)}")))
        got_in, got_out, got_small = got["in"], got["out"], got["small"]
        shapes, rep_len = small_layout[layer]
        *small_parts, rep_parts[layer] = _unpack(got_small.reshape(N_DEV, -1), list(shapes) + [(rep_len,)])
        names = list(BIG[par]) + list(SMALL[par])
        parts = [got_in, got_out] + small_parts
        big_items, small_items = [], []
        for n, pt in zip(names, parts):
            item = (w3[n], pt.reshape((N_DEV, 1) + w3[n].shape[1:]), m3[n], v3[n])
            (big_items if w3[n].shape[1] >= 128 else small_items).append((n, item))
        for n, item in big_items:
            results[n] = _adamw([item], j, [results[n]] if n in results else None, f"adamw_{n}_{layer}")[0]
        snames = [n for n, _ in small_items]
        prev = [results[n] for n in snames] if snames[0] in results else None
        for n, r in zip(snames, _adamw([it for _, it in small_items], j, prev, f"adamw_small_{layer}")):
            results[n] = r
        after = [results[names[0]][1]]

    rep_pack = _pack_rows(jnp.concatenate([rep_parts[l] for l in range(N_LAYERS)], axis=1))
    rep_mine = _sum_parts(rep_pack, tr=rep_pack.shape[1])
    rep_all = _all_gather([rep_mine], "gather_replicated")[0].reshape(N_DEV, -1)
    rep_grads, off = {}, 0
    for layer in range(N_LAYERS):
        rep_len = small_layout[layer][1]
        flat = rep_all[:, off:off + rep_len].reshape(-1)
        off += rep_len
        rep_names = list(REPLICATED[layer % 2]) + (["final_norm"] if layer == N_LAYERS - 1 else [])
        for n, g in zip(rep_names, _unpack(flat, [w_loc[n].shape[1:] if n != "final_norm" else w_loc[n].shape
                                                  for n in rep_names])):
            rep_grads.setdefault(n, []).append(g)
    rep_items = {n: (w3[n], _as3d(jnp.stack(rep_grads[n]) if n != "final_norm" else rep_grads[n][0])[None],
                     m3[n], v3[n]) for par in range(2) for n in REPLICATED[par]}
    rep_items["final_norm"] = (w3["final_norm"], _as3d(rep_grads["final_norm"][0])[None], m3["final_norm"],
                               v3["final_norm"])
    for n in ("w_rg", "w_ig"):
        results[n] = _adamw([rep_items[n]], 0, None, f"adamw_{n}")[0]
    vec_names = list(REPLICATED[0])
    for n, r in zip(vec_names, _adamw([rep_items[n] for n in vec_names], 0, None, "adamw_replicated")):
        results[n] = r
    results["final_norm"] = _adamw([rep_items["final_norm"]], 0, None, "adamw_final_norm")[0]

    total = lax.psum(loss[0, 0], ("x", "y", "c"))
    outs = [[results[n][k].reshape(w_loc[n].shape) for n in NAMES] for k in range(4)]
    return (total, grad_x[None], *outs[0], *outs[1], *outs[2], *outs[3])
```

```python
import functools

import jax
import jax.numpy as jnp
from jax import lax
from jax.experimental import pallas as pl
from jax.experimental.pallas import tpu as pltpu

F32 = jnp.float32
BF16 = jnp.bfloat16

N_DEV = 8
N_PEERS = N_DEV - 1
N_LAYERS = 4
D_MODEL = 1024
EPS_RMS = 1e-6
EPS_LN = 1e-5
W_CONV = 1024
CONV_K = 31
W_POOL = 1024
POOL_WINDOWS = (2, 4, 8, 16)
POOL_GW = 256
W_EVEN_IN = 5120
W_EVEN_MIX = 2048
LRU_HEADS = 12
LRU_HD = 128
W_LRU = 1536
LRU_CONV_K = 4
LRU_C = 8.0
ADAM_LR = 0.001
ADAM_B1 = 0.9
ADAM_B2 = 0.999
ADAM_EPS = 1e-08
ADAM_WD = 0.01
ADAM_STEP = 10

HALO = 32
HALO_C = 8
TM_MATMUL = 512
TM_MIXER = 256
VMEM_LIMIT = 56 * 1024 * 1024
MESH = pl.DeviceIdType.MESH
ANY = pl.BlockSpec(memory_space=pl.ANY)
HBM = pl.BlockSpec(memory_space=pltpu.HBM)
SEM = pl.BlockSpec(memory_space=pltpu.SEMAPHORE)


def _params(*sem):
    return pltpu.CompilerParams(dimension_semantics=sem, vmem_limit_bytes=VMEM_LIMIT)


def _sigmoid(z):
    return jax.nn.sigmoid(z)


def _dsilu(z, s):
    return s * (1.0 + z * (1.0 - s))


def _full(shape):
    nd = len(shape)
    return pl.BlockSpec(shape, lambda *_: (0,) * nd)


def _norm_matmul(h, g, w, dep, *, tm):
    t, d = h.shape
    nd, _, nb = w.shape

    def body(h_ref, g_ref, w_ref, dep_ref, p_ref, hn_ref):
        @pl.when(pl.program_id(1) == 0)
        def _():
            x = h_ref[...]
            r = lax.rsqrt(jnp.mean(x * x, axis=-1, keepdims=True) + EPS_RMS)
            hn_ref[...] = ((x * r) * g_ref[...]).astype(BF16)

        p_ref[...] = jnp.dot(hn_ref[...], w_ref[0], preferred_element_type=F32)

    return pl.pallas_call(
        body, name="norm_matmul", grid=(t // tm, nd),
        in_specs=[pl.BlockSpec((tm, d), lambda i, j: (i, 0)), _full((1, d)),
                  pl.BlockSpec((1, d, nb), lambda i, j: (j, 0, 0)), ANY],
        out_specs=[pl.BlockSpec((tm, nb), lambda i, j: (i, j)), pl.BlockSpec((tm, d), lambda i, j: (i, 0))],
        out_shape=[jax.ShapeDtypeStruct((t, nd * nb), F32), jax.ShapeDtypeStruct((t, d), BF16)],
        compiler_params=_params("arbitrary", "arbitrary"),
    )(h, g, w, dep)


def _out_proj(h, y, w, *, tm):
    t, d = h.shape
    k = y.shape[1]

    def body(h_ref, y_ref, w_ref, o_ref):
        o_ref[...] = h_ref[...] + jnp.dot(y_ref[...], w_ref[...], preferred_element_type=F32)

    return pl.pallas_call(
        body, name="out_proj", grid=(t // tm,),
        in_specs=[pl.BlockSpec((tm, d), lambda i: (i, 0)), pl.BlockSpec((tm, k), lambda i: (i, 0)), _full((k, d))],
        out_specs=pl.BlockSpec((tm, d), lambda i: (i, 0)),
        out_shape=jax.ShapeDtypeStruct((t, d), F32),
        compiler_params=_params("arbitrary"),
    )(h, y, w)


def _out_proj_bwd(dh, y, w, dep, *, tm):
    t, d = dh.shape
    k = y.shape[1]
    nt = t // tm

    def body(dh_ref, y_ref, w_ref, dep_ref, dy_ref, dw_ref, acc):
        i = pl.program_id(0)
        g = dh_ref[...].astype(BF16)
        dy_ref[...] = lax.dot_general(g, w_ref[...], (((1,), (1,)), ((), ())), preferred_element_type=F32)
        part = lax.dot_general(y_ref[...], g, (((0,), (0,)), ((), ())), preferred_element_type=F32)

        @pl.when(i == 0)
        def _():
            acc[...] = part

        @pl.when(i > 0)
        def _():
            acc[...] += part

        @pl.when(i == nt - 1)
        def _():
            dw_ref[...] = acc[...].astype(BF16)

    return pl.pallas_call(
        body, name="out_proj_bwd", grid=(nt,),
        in_specs=[pl.BlockSpec((tm, d), lambda i: (i, 0)), pl.BlockSpec((tm, k), lambda i: (i, 0)), _full((k, d)),
                  ANY],
        out_specs=[pl.BlockSpec((tm, k), lambda i: (i, 0)), _full((k, d))],
        out_shape=[jax.ShapeDtypeStruct((t, k), F32), jax.ShapeDtypeStruct((k, d), BF16)],
        scratch_shapes=[pltpu.VMEM((k, d), F32)],
        compiler_params=_params("arbitrary"),
    )(dh, y, w, dep)


def _in_proj_bwd_x(dp, w, h, g, dh_out, dep, *, tm):
    t, d = h.shape
    nd, _, nb = w.shape
    nt = t // tm

    def body(dp_ref, w_ref, h_ref, g_ref, dho_ref, dep_ref, dh_ref, dg_ref, acc):
        i, j = pl.program_id(0), pl.program_id(1)
        part = lax.dot_general(dp_ref[...], w_ref[0], (((1,), (1,)), ((), ())), preferred_element_type=F32)

        @pl.when(j == 0)
        def _():
            acc[...] = part

        @pl.when(j > 0)
        def _():
            acc[...] += part

        @pl.when(j == nd - 1)
        def _():
            x = h_ref[...]
            r = lax.rsqrt(jnp.mean(x * x, axis=-1, keepdims=True) + EPS_RMS)
            dy = acc[...]
            gd = dy * g_ref[...]
            m = jnp.mean(gd * x, axis=-1, keepdims=True)
            dh_ref[...] = dho_ref[...] + r * gd - x * (r * r * r * m)
            dgp = jnp.sum(dy * x * r, axis=0, keepdims=True)

            @pl.when(i == 0)
            def _():
                dg_ref[...] = dgp

            @pl.when(i > 0)
            def _():
                dg_ref[...] += dgp

    return pl.pallas_call(
        body, name="in_proj_bwd_x", grid=(nt, nd),
        in_specs=[pl.BlockSpec((tm, nb), lambda i, j: (i, j)), pl.BlockSpec((1, d, nb), lambda i, j: (j, 0, 0)),
                  pl.BlockSpec((tm, d), lambda i, j: (i, 0)), _full((1, d)), pl.BlockSpec((tm, d), lambda i, j: (i, 0)),
                  ANY],
        out_specs=[pl.BlockSpec((tm, d), lambda i, j: (i, 0)), _full((1, d))],
        out_shape=[jax.ShapeDtypeStruct((t, d), F32), jax.ShapeDtypeStruct((1, d), F32)],
        scratch_shapes=[pltpu.VMEM((tm, d), F32)],
        compiler_params=_params("arbitrary", "arbitrary"),
    )(dp, w, h, g, dh_out, dep)


def _in_proj_bwd_w(hn, dp, nd, *, tm):
    t, d = hn.shape
    nb = dp.shape[1] // nd
    nt = t // tm

    def body(hn_ref, dp_ref, dw_ref, acc):
        i = pl.program_id(1)
        part = lax.dot_general(hn_ref[...], dp_ref[...], (((0,), (0,)), ((), ())), preferred_element_type=F32)

        @pl.when(i == 0)
        def _():
            acc[...] = part

        @pl.when(i > 0)
        def _():
            acc[...] += part

        @pl.when(i == nt - 1)
        def _():
            dw_ref[0] = acc[...].astype(BF16)

    return pl.pallas_call(
        body, name="in_proj_bwd_w", grid=(nd, nt),
        in_specs=[pl.BlockSpec((tm, d), lambda j, i: (i, 0)), pl.BlockSpec((tm, nb), lambda j, i: (i, j))],
        out_specs=pl.BlockSpec((1, d, nb), lambda j, i: (j, 0, 0)),
        out_shape=jax.ShapeDtypeStruct((nd, d, nb), BF16),
        scratch_shapes=[pltpu.VMEM((d, nb), F32)],
        compiler_params=_params("arbitrary", "arbitrary"),
    )(hn, dp)


def _loss_head(h, g, target, *, tm):
    t, d = h.shape
    nt = t // tm

    def body(h_ref, g_ref, t_ref, loss_ref, dh_ref, dg_ref):
        i = pl.program_id(0)
        x = h_ref[...]
        r = lax.rsqrt(jnp.mean(x * x, axis=-1, keepdims=True) + EPS_RMS)
        xr = x * r
        err = xr * g_ref[...] - t_ref[...]
        lp = 0.5 * jnp.sum(jnp.mean(err * err, axis=-1, keepdims=True), axis=0, keepdims=True)
        dy = err * (1.0 / d)
        gd = dy * g_ref[...]
        m = jnp.mean(gd * x, axis=-1, keepdims=True)
        dh_ref[...] = r * gd - x * (r * r * r * m)
        dgp = jnp.sum(dy * xr, axis=0, keepdims=True)

        @pl.when(i == 0)
        def _():
            loss_ref[...] = lp
            dg_ref[...] = dgp

        @pl.when(i > 0)
        def _():
            loss_ref[...] += lp
            dg_ref[...] += dgp

    return pl.pallas_call(
        body, name="loss_head", grid=(nt,),
        in_specs=[pl.BlockSpec((tm, d), lambda i: (i, 0)), _full((1, d)), pl.BlockSpec((tm, d), lambda i: (i, 0))],
        out_specs=[_full((1, 1)), pl.BlockSpec((tm, d), lambda i: (i, 0)), _full((1, d))],
        out_shape=[jax.ShapeDtypeStruct((1, 1), F32), jax.ShapeDtypeStruct((t, d), F32),
                   jax.ShapeDtypeStruct((1, d), F32)],
        compiler_params=_params("arbitrary"),
    )(h, g, target)


def _col(tm, w, c):
    return pl.BlockSpec((tm, w), lambda i: (i, c))


def _prev_halo(tm, rows, w, c):
    per = tm // rows
    return pl.BlockSpec((rows, w), lambda i: (jnp.maximum(i * per - 1, 0), c))


def _next_halo(tm, rows, w, c, t):
    per = tm // rows
    last = t // rows - 1
    return pl.BlockSpec((rows, w), lambda i: (jnp.minimum((i + 1) * per, last), c))


def _inv_count(first_row, rows, window):
    tpos = first_row + lax.broadcasted_iota(jnp.int32, (rows, 1), 0)
    return 1.0 / jnp.minimum(tpos + 1, window).astype(F32)


def _even_fwd(p, cw, cb, lg, lb, pw, pb, ps, *, tm):
    t = p.shape[0]
    wc = W_CONV

    def body(av, ag, agate, bv, bgate, avh, agh, bvh, cw_ref, cb_ref, lg_ref, lb_ref, pw_ref, pb_ref, ps_ref,
             y_ref, u1_ref, e_ref, d_ref, uext, vext):
        i = pl.program_id(0)
        keep = (i > 0).astype(F32)
        uext[0:HALO, :] = keep * (avh[...] * _sigmoid(agh[...]))
        uext[HALO:, :] = av[...] * _sigmoid(ag[...])
        vext[0:HALO, :] = keep * bvh[...]
        vext[HALO:, :] = bv[...]
        for c in range(0, wc, 128):
            acc = jnp.broadcast_to(cb_ref[:, c:c + 128], (tm, 128))
            for k in range(CONV_K):
                acc = acc + cw_ref[k:k + 1, c:c + 128] * uext[pl.ds(HALO - (CONV_K - 1) + k, tm), c:c + 128]
            u1_ref[:, c:c + 128] = acc
        u1 = u1_ref[...]
        mu = jnp.mean(u1, axis=-1, keepdims=True)
        xc = u1 - mu
        rs = lax.rsqrt(jnp.mean(xc * xc, axis=-1, keepdims=True) + EPS_LN)
        u2 = (xc * rs) * lg_ref[...] + lb_ref[...]
        u3 = u2 * _sigmoid(u2)
        ga = agate[...]
        y_ref[:, 0:wc] = (u3 * (ga * _sigmoid(ga))).astype(BF16)
        for g, win in enumerate(POOL_WINDOWS):
            cs = slice(g * POOL_GW, (g + 1) * POOL_GW)
            s = vext[pl.ds(HALO, tm), cs]
            for j in range(1, win):
                s = s + vext[pl.ds(HALO - j, tm), cs]
            dg = s * _inv_count(i * tm, tm, win) - vext[pl.ds(HALO, tm), cs]
            dgb = dg.astype(BF16)
            d_ref[:, cs] = dgb
            eg = jnp.dot(dgb, pw_ref[g], preferred_element_type=F32) + pb_ref[:, cs]
            e_ref[:, cs] = eg
            gb = bgate[:, cs]
            y_ref[:, wc + g * POOL_GW:wc + (g + 1) * POOL_GW] = ((eg * ps_ref[:, cs]) * (gb * _sigmoid(gb))).astype(BF16)

    row = lambda w: pl.BlockSpec((tm, w), lambda i: (i, 0))
    return pl.pallas_call(
        body, name="even_fwd", grid=(t // tm,),
        in_specs=[_col(tm, wc, 0), _col(tm, wc, 1), _col(tm, wc, 2), _col(tm, wc, 3), _col(tm, wc, 4),
                  _prev_halo(tm, HALO, wc, 0), _prev_halo(tm, HALO, wc, 1), _prev_halo(tm, HALO, wc, 3),
                  _full((32, wc)), _full((1, wc)), _full((1, wc)), _full((1, wc)),
                  _full((4, POOL_GW, POOL_GW)), _full((1, wc)), _full((1, wc))],
        out_specs=[row(2 * wc), row(wc), row(wc), row(wc)],
        out_shape=[jax.ShapeDtypeStruct((t, 2 * wc), BF16), jax.ShapeDtypeStruct((t, wc), F32),
                   jax.ShapeDtypeStruct((t, wc), F32), jax.ShapeDtypeStruct((t, wc), BF16)],
        scratch_shapes=[pltpu.VMEM((tm + HALO, wc), F32), pltpu.VMEM((tm + HALO, wc), F32)],
        compiler_params=_params("arbitrary"),
    )(p, p, p, p, p, p, p, p, cw, cb, lg, lb, pw, pb, ps)


def _acc_out(i, ref, val):
    @pl.when(i == 0)
    def _():
        ref[...] = val

    @pl.when(i > 0)
    def _():
        ref[...] += val


def _even_bwd_a(p, u1, e, dmat, dy, lg, lb, pw, ps, dep, *, tm):
    t = p.shape[0]
    wc = W_CONV

    def body(agate, bgate, u1_ref, e_ref, d_ref, dya, dyb, lg_ref, lb_ref, pw_ref, ps_ref, dep_ref,
             du1_ref, dd_ref, dgat_ref, dlg_ref, dlb_ref, dpb_ref, dps_ref, dpw_ref):
        i = pl.program_id(0)

        @pl.when(i == 0)
        def _():
            dpw_ref[...] = jnp.zeros_like(dpw_ref)

        u1 = u1_ref[...]
        mu = jnp.mean(u1, axis=-1, keepdims=True)
        xc = u1 - mu
        rs = lax.rsqrt(jnp.mean(xc * xc, axis=-1, keepdims=True) + EPS_LN)
        xh = xc * rs
        u2 = xh * lg_ref[...] + lb_ref[...]
        s2 = _sigmoid(u2)
        ga = agate[...]
        sa = _sigmoid(ga)
        dy_a = dya[...]
        dgat_ref[:, 0:wc] = (dy_a * (u2 * s2) * _dsilu(ga, sa)).astype(BF16)
        du2 = dy_a * (ga * sa) * _dsilu(u2, s2)
        _acc_out(i, dlg_ref, jnp.sum(du2 * xh, axis=0, keepdims=True))
        _acc_out(i, dlb_ref, jnp.sum(du2, axis=0, keepdims=True))
        dxh = du2 * lg_ref[...]
        m1 = jnp.mean(dxh, axis=-1, keepdims=True)
        m2 = jnp.mean(dxh * xh, axis=-1, keepdims=True)
        du1_ref[...] = rs * (dxh - m1 - xh * m2)

        gb = bgate[...]
        sb = _sigmoid(gb)
        ev = e_ref[...]
        dy_b = dyb[...]
        dgat_ref[:, wc:2 * wc] = (dy_b * (ev * ps_ref[...]) * _dsilu(gb, sb)).astype(BF16)
        dz = dy_b * (gb * sb)
        _acc_out(i, dps_ref, jnp.sum(dz * ev, axis=0, keepdims=True))
        de = dz * ps_ref[...]
        _acc_out(i, dpb_ref, jnp.sum(de, axis=0, keepdims=True))
        for g in range(len(POOL_WINDOWS)):
            cs = slice(g * POOL_GW, (g + 1) * POOL_GW)
            deg = de[:, cs].astype(BF16)
            dd_ref[:, cs] = lax.dot_general(deg, pw_ref[g], (((1,), (1,)), ((), ())), preferred_element_type=F32)
            dpw_ref[g] += lax.dot_general(d_ref[:, cs], deg, (((0,), (0,)), ((), ())), preferred_element_type=F32)

    row = lambda w: pl.BlockSpec((tm, w), lambda i: (i, 0))
    return pl.pallas_call(
        body, name="even_bwd_a", grid=(t // tm,),
        in_specs=[_col(tm, wc, 2), _col(tm, wc, 4), row(wc), row(wc), row(wc), _col(tm, wc, 0), _col(tm, wc, 1),
                  _full((1, wc)), _full((1, wc)), _full((4, POOL_GW, POOL_GW)), _full((1, wc)), ANY],
        out_specs=[row(wc), row(wc), row(2 * wc), _full((1, wc)), _full((1, wc)), _full((1, wc)), _full((1, wc)),
                   _full((4, POOL_GW, POOL_GW))],
        out_shape=[jax.ShapeDtypeStruct((t, wc), F32), jax.ShapeDtypeStruct((t, wc), F32),
                   jax.ShapeDtypeStruct((t, 2 * wc), BF16)] + [jax.ShapeDtypeStruct((1, wc), F32)] * 4
                  + [jax.ShapeDtypeStruct((4, POOL_GW, POOL_GW), F32)],
        compiler_params=_params("arbitrary"),
    )(p, p, u1, e, dmat, dy, dy, lg, lb, pw, ps, dep)


def _even_bwd_b(p, du1, dd, dgat, cw, *, tm):
    t = p.shape[0]
    wc = W_CONV
    nt = t // tm

    def body(av, ag, avh, agh, du1_ref, du1n, dd_ref, ddn, dgat_ref, cw_ref, dp_ref, dcw_ref, dcb_ref,
             uext, gext, dext, du0):
        i = pl.program_id(0)
        keep_p = (i > 0).astype(F32)
        keep_n = (i < nt - 1).astype(F32)

        @pl.when(i == 0)
        def _():
            dcw_ref[...] = jnp.zeros_like(dcw_ref)

        a = av[...]
        sg = _sigmoid(ag[...])
        uext[0:HALO, :] = keep_p * (avh[...] * _sigmoid(agh[...]))
        uext[HALO:, :] = a * sg
        gext[0:tm, :] = du1_ref[...]
        gext[tm:, :] = keep_n * du1n[...]
        for c in range(0, wc, 128):
            acc = jnp.zeros((tm, 128), F32)
            for s in range(CONV_K):
                k = CONV_K - 1 - s
                acc = acc + cw_ref[k:k + 1, c:c + 128] * gext[pl.ds(s, tm), c:c + 128]
            du0[:, c:c + 128] = acc
            gcur = du1_ref[:, c:c + 128]
            for k in range(CONV_K):
                dcw_ref[k:k + 1, c:c + 128] += jnp.sum(
                    gcur * uext[pl.ds(HALO - (CONV_K - 1) + k, tm), c:c + 128], axis=0, keepdims=True)

        _acc_out(i, dcb_ref, jnp.sum(du1_ref[...], axis=0, keepdims=True))
        g0 = du0[...]
        dp_ref[:, 0:wc] = (g0 * sg).astype(BF16)
        dp_ref[:, wc:2 * wc] = (g0 * a * sg * (1.0 - sg)).astype(BF16)
        dp_ref[:, 2 * wc:3 * wc] = dgat_ref[:, 0:wc]
        dp_ref[:, 4 * wc:5 * wc] = dgat_ref[:, wc:2 * wc]
        for g, win in enumerate(POOL_WINDOWS):
            cs = slice(g * POOL_GW, (g + 1) * POOL_GW)
            dext[0:tm, cs] = dd_ref[:, cs] * _inv_count(i * tm, tm, win)
            dext[tm:, cs] = keep_n * (ddn[:, cs] * _inv_count((i + 1) * tm, HALO, win))
            s = dext[pl.ds(0, tm), cs]
            for j in range(1, win):
                s = s + dext[pl.ds(j, tm), cs]
            dp_ref[:, 3 * wc + g * POOL_GW:3 * wc + (g + 1) * POOL_GW] = (s - dd_ref[:, cs]).astype(BF16)

    row = lambda w: pl.BlockSpec((tm, w), lambda i: (i, 0))
    return pl.pallas_call(
        body, name="even_bwd_b", grid=(nt,),
        in_specs=[_col(tm, wc, 0), _col(tm, wc, 1), _prev_halo(tm, HALO, wc, 0), _prev_halo(tm, HALO, wc, 1),
                  row(wc), _next_halo(tm, HALO, wc, 0, t), row(wc), _next_halo(tm, HALO, wc, 0, t), row(2 * wc),
                  _full((32, wc))],
        out_specs=[row(5 * wc), _full((32, wc)), _full((1, wc))],
        out_shape=[jax.ShapeDtypeStruct((t, 5 * wc), BF16), jax.ShapeDtypeStruct((32, wc), F32),
                   jax.ShapeDtypeStruct((1, wc), F32)],
        scratch_shapes=[pltpu.VMEM((tm + HALO, wc), F32), pltpu.VMEM((tm + HALO, wc), F32),
                        pltpu.VMEM((tm + HALO, wc), F32), pltpu.VMEM((tm, wc), F32)],
        compiler_params=_params("arbitrary"),
    )(p, p, p, p, du1, du1, dd, dd, dgat, cw)


def _softplus_neg(lam):
    z = -lam
    return jnp.maximum(z, 0.0) + jnp.log1p(jnp.exp(-jnp.abs(z)))


def _one_minus_exp(x):
    series = -x * (1.0 + x * (0.5 + x * (1.0 / 6.0 + x * (1.0 / 24.0))))
    return jnp.where(x > -0.02, series, 1.0 - jnp.exp(x))


def _odd_fwd(p, ccw, ccb, wrg, brg, wig, big, lam, *, tm):
    t = p.shape[0]
    wl = W_LRU
    ng = tm // 8

    def body(xr, gate, xrh, ccw_ref, ccb_ref, wrg_ref, brg_ref, wig_ref, big_ref, lam_ref,
             y_ref, xc_ref, r_ref, i_ref, hs_ref, xext, a_s, b_s, carry):
        i = pl.program_id(0)
        keep = (i > 0).astype(F32)
        xext[0:HALO_C, :] = keep * xrh[...]
        xext[HALO_C:, :] = xr[...]
        xc = jnp.broadcast_to(ccb_ref[...], (tm, wl))
        for k in range(LRU_CONV_K):
            xc = xc + ccw_ref[k:k + 1, :] * xext[pl.ds(HALO_C - (LRU_CONV_K - 1) + k, tm), :]
        xc_ref[...] = xc
        for h in range(LRU_HEADS):
            cs = slice(h * LRU_HD, (h + 1) * LRU_HD)
            xh = xc_ref[:, cs].astype(BF16)
            r_ref[:, cs] = _sigmoid(jnp.dot(xh, wrg_ref[h], preferred_element_type=F32) + brg_ref[:, cs])
            i_ref[:, cs] = _sigmoid(jnp.dot(xh, wig_ref[h], preferred_element_type=F32) + big_ref[:, cs])
        log_a = (-LRU_C * _softplus_neg(lam_ref[...])) * r_ref[...]
        a_s[...] = jnp.exp(log_a)
        b_s[...] = jnp.sqrt(_one_minus_exp(2.0 * log_a)) * (i_ref[...] * xc_ref[...])

        @pl.when(i == 0)
        def _():
            carry[...] = jnp.zeros_like(carry)

        rowi = lax.broadcasted_iota(jnp.int32, (8, wl), 0)

        def step(g, c):
            sl = pl.ds(pl.multiple_of(g * 8, 8), 8)
            aa, bb = a_s[sl, :], b_s[sl, :]
            for s in (1, 2, 4):
                m = rowi >= s
                a_sh = jnp.where(m, pltpu.roll(aa, s, 0), 1.0)
                b_sh = jnp.where(m, pltpu.roll(bb, s, 0), 0.0)
                bb = aa * b_sh + bb
                aa = aa * a_sh
            hv = bb + aa * c
            hs_ref[sl, :] = hv
            return hv[7:8, :]

        carry[...] = lax.fori_loop(0, ng, step, carry[...])
        gt = gate[...]
        y_ref[...] = (hs_ref[...] * (gt * _sigmoid(gt))).astype(BF16)

    row = lambda w: pl.BlockSpec((tm, w), lambda i: (i, 0))
    return pl.pallas_call(
        body, name="odd_fwd", grid=(t // tm,),
        in_specs=[_col(tm, wl, 0), _col(tm, wl, 1), _prev_halo(tm, HALO_C, wl, 0), _full((8, wl)), _full((1, wl)),
                  _full((LRU_HEADS, LRU_HD, LRU_HD)), _full((1, wl)), _full((LRU_HEADS, LRU_HD, LRU_HD)),
                  _full((1, wl)), _full((1, wl))],
        out_specs=[row(wl)] * 5,
        out_shape=[jax.ShapeDtypeStruct((t, wl), BF16)] + [jax.ShapeDtypeStruct((t, wl), F32)] * 4,
        scratch_shapes=[pltpu.VMEM((tm + HALO_C, wl), F32), pltpu.VMEM((tm, wl), F32), pltpu.VMEM((tm, wl), F32),
                        pltpu.VMEM((1, wl), F32)],
        compiler_params=_params("arbitrary"),
    )(p, p, p, ccw, ccb, wrg, brg, wig, big, lam)


def _odd_bwd_a(p, xc, r, ig, hs, dy, wrg, wig, lam, dep, *, tm):
    t = p.shape[0]
    wl = W_LRU
    nt = t // tm
    ng = tm // 8
    per = tm // HALO_C

    def body(gate, xc_ref, r_ref, i_ref, hs_ref, hsh, dy_ref, wrg_ref, wig_ref, lam_ref, dep_ref,
             dxc_ref, dgate_ref, dwrg_ref, dwig_ref, dbrg_ref, dbig_ref, dlam_ref,
             hext, a_s, q_s, g_s, dpr_s, dpi_s, carry):
        i = pl.program_id(0)
        ti = nt - 1 - i
        keep = (ti > 0).astype(F32)
        hext[0:HALO_C, :] = keep * hsh[...]
        hext[HALO_C:, :] = hs_ref[...]
        gt = gate[...]
        sg = _sigmoid(gt)
        dyv = dy_ref[...]
        dgate_ref[...] = (dyv * hs_ref[...] * _dsilu(gt, sg)).astype(BF16)
        q_s[...] = dyv * (gt * sg)
        sp = _softplus_neg(lam_ref[...])
        log_a = (-LRU_C * sp) * r_ref[...]
        a_s[...] = jnp.exp(log_a)

        @pl.when(i == 0)
        def _():
            carry[...] = jnp.zeros_like(carry)
            dwrg_ref[...] = jnp.zeros_like(dwrg_ref)
            dwig_ref[...] = jnp.zeros_like(dwig_ref)

        rowi = lax.broadcasted_iota(jnp.int32, (8, wl), 0)

        def step(gr, c):
            sl = pl.ds(pl.multiple_of((ng - 1 - gr) * 8, 8), 8)
            a0 = a_s[sl, :]
            al = jnp.where(rowi < 7, pltpu.roll(a0, 7, 0), 1.0)
            be = q_s[sl, :]
            for s in (1, 2, 4):
                m = rowi + s <= 7
                al_sh = jnp.where(m, pltpu.roll(al, 8 - s, 0), 1.0)
                be_sh = jnp.where(m, pltpu.roll(be, 8 - s, 0), 0.0)
                be = be + al * be_sh
                al = al * al_sh
            gv = be + al * c
            g_s[sl, :] = gv
            return (a0 * gv)[0:1, :]

        carry[...] = lax.fori_loop(0, ng, step, carry[...])

        gv = g_s[...]
        a = a_s[...]
        mult = jnp.sqrt(_one_minus_exp(2.0 * log_a))
        iv = i_ref[...]
        rv = r_ref[...]
        xcv = xc_ref[...]
        hprev = hext[pl.ds(HALO_C - 1, tm), :]
        dla = gv * hprev * a - (gv * iv * xcv) * (a * a) / mult
        di = gv * mult * xcv
        dpr = (dla * (-LRU_C * sp)) * rv * (1.0 - rv)
        dpi = di * iv * (1.0 - iv)
        dpr_s[...] = dpr
        dpi_s[...] = dpi
        dxc_ref[...] = gv * mult * iv
        dsp = jnp.sum(dla * rv, axis=0, keepdims=True) * (-LRU_C)
        _acc_out(i, dlam_ref, -dsp * _sigmoid(-lam_ref[...]))
        _acc_out(i, dbrg_ref, jnp.sum(dpr, axis=0, keepdims=True))
        _acc_out(i, dbig_ref, jnp.sum(dpi, axis=0, keepdims=True))
        for h in range(LRU_HEADS):
            cs = slice(h * LRU_HD, (h + 1) * LRU_HD)
            xh = xc_ref[:, cs].astype(BF16)
            dr_h = dpr_s[:, cs].astype(BF16)
            di_h = dpi_s[:, cs].astype(BF16)
            dxc_ref[:, cs] += (
                lax.dot_general(dr_h, wrg_ref[h], (((1,), (1,)), ((), ())), preferred_element_type=F32)
                + lax.dot_general(di_h, wig_ref[h], (((1,), (1,)), ((), ())), preferred_element_type=F32))
            dwrg_ref[h] += lax.dot_general(xh, dr_h, (((0,), (0,)), ((), ())), preferred_element_type=F32)
            dwig_ref[h] += lax.dot_general(xh, di_h, (((0,), (0,)), ((), ())), preferred_element_type=F32)

    rrow = lambda w: pl.BlockSpec((tm, w), lambda i: (nt - 1 - i, 0))
    hspec = pl.BlockSpec((HALO_C, wl), lambda i: (jnp.maximum((nt - 1 - i) * per - 1, 0), 0))
    wspec = _full((LRU_HEADS, LRU_HD, LRU_HD))
    return pl.pallas_call(
        body, name="odd_bwd_a", grid=(nt,),
        in_specs=[pl.BlockSpec((tm, wl), lambda i: (nt - 1 - i, 1)), rrow(wl), rrow(wl), rrow(wl), rrow(wl), hspec,
                  rrow(wl), wspec, wspec, _full((1, wl)), ANY],
        out_specs=[rrow(wl), rrow(wl), wspec, wspec, _full((1, wl)), _full((1, wl)), _full((1, wl))],
        out_shape=[jax.ShapeDtypeStruct((t, wl), F32), jax.ShapeDtypeStruct((t, wl), BF16),
                   jax.ShapeDtypeStruct((LRU_HEADS, LRU_HD, LRU_HD), F32),
                   jax.ShapeDtypeStruct((LRU_HEADS, LRU_HD, LRU_HD), F32)] + [jax.ShapeDtypeStruct((1, wl), F32)] * 3,
        scratch_shapes=[pltpu.VMEM((tm + HALO_C, wl), F32)] + [pltpu.VMEM((tm, wl), F32)] * 5
                       + [pltpu.VMEM((1, wl), F32)],
        compiler_params=_params("arbitrary"),
    )(p, xc, r, ig, hs, hs, dy, wrg, wig, lam, dep)


def _odd_bwd_b(p, dxc, dgate, ccw, *, tm):
    t = p.shape[0]
    wl = W_LRU
    nt = t // tm

    def body(xr, xrh, dxc_ref, dxcn, dgate_ref, ccw_ref, dp_ref, dcw_ref, dcb_ref, xext, gext):
        i = pl.program_id(0)

        @pl.when(i == 0)
        def _():
            dcw_ref[...] = jnp.zeros_like(dcw_ref)

        xext[0:HALO_C, :] = (i > 0).astype(F32) * xrh[...]
        xext[HALO_C:, :] = xr[...]
        gext[0:tm, :] = dxc_ref[...]
        gext[tm:, :] = (i < nt - 1).astype(F32) * dxcn[...]
        g = dxc_ref[...]
        acc = jnp.zeros((tm, wl), F32)
        for k in range(LRU_CONV_K):
            acc = acc + ccw_ref[k:k + 1, :] * gext[pl.ds(LRU_CONV_K - 1 - k, tm), :]
            dcw_ref[k:k + 1, :] += jnp.sum(
                g * xext[pl.ds(HALO_C - (LRU_CONV_K - 1) + k, tm), :], axis=0, keepdims=True)

        _acc_out(i, dcb_ref, jnp.sum(g, axis=0, keepdims=True))
        dp_ref[:, 0:wl] = acc.astype(BF16)
        dp_ref[:, wl:2 * wl] = dgate_ref[...]

    row = lambda w: pl.BlockSpec((tm, w), lambda i: (i, 0))
    return pl.pallas_call(
        body, name="odd_bwd_b", grid=(nt,),
        in_specs=[_col(tm, wl, 0), _prev_halo(tm, HALO_C, wl, 0), row(wl), _next_halo(tm, HALO_C, wl, 0, t), row(wl),
                  _full((8, wl))],
        out_specs=[row(2 * wl), _full((8, wl)), _full((1, wl))],
        out_shape=[jax.ShapeDtypeStruct((t, 2 * wl), BF16), jax.ShapeDtypeStruct((8, wl), F32),
                   jax.ShapeDtypeStruct((1, wl), F32)],
        scratch_shapes=[pltpu.VMEM((tm + HALO_C, wl), F32), pltpu.VMEM((tm + HALO_C, wl), F32)],
        compiler_params=_params("arbitrary"),
    )(p, p, dxc, dxc, dgate, ccw)


def _local_step(x, target, layer_weights, final_norm, on_grads):
    tm, tx = TM_MATMUL, TM_MIXER
    h = x
    saved = []
    for layer in range(N_LAYERS):
        w = layer_weights(layer, h)
        p, hn = _norm_matmul(h, w["norm"], w["w_in"], w["dep"], tm=tm)
        if layer % 2 == 0:
            y, *acts = _even_fwd(p, w["conv_w"], w["conv_b"], w["ln_g"], w["ln_b"], w["pool_w"], w["pool_b"],
                                 w["pool_scale"], tm=tx)
        else:
            y, *acts = _odd_fwd(p, w["conv_w"], w["conv_b"], w["w_rg"], w["b_rg"], w["w_ig"], w["b_ig"], w["lam"],
                                tm=tx)
        w_out = w["w_out"](y)
        saved.append((w, w_out, h, p, hn, y, acts))
        h = _out_proj(h, y, w_out, tm=tm)
    loss, dh, d_final = _loss_head(h, final_norm, target, tm=tm)

    dep = d_final
    for layer in reversed(range(N_LAYERS)):
        w, w_out, h_in, p, hn, y, acts = saved[layer]
        sfx = "even" if layer % 2 == 0 else "odd"
        dy, dw_out = _out_proj_bwd(dh, y, w_out, dep, tm=tm)
        dep = on_grads(layer, {"w_out_" + sfx: dw_out}, dep, False)
        if layer % 2 == 0:
            u1, e, dmat = acts
            du1, dd, dgat, dlg, dlb, dpb, dps, dpw = _even_bwd_a(p, u1, e, dmat, dy, w["ln_g"], w["ln_b"], w["pool_w"],
                                                                 w["pool_scale"], dep, tm=tx)
            dp, dcw, dcb = _even_bwd_b(p, du1, dd, dgat, w["conv_w"], tm=tx)
            grads = dict(conv_a_w=dcw[:CONV_K], conv_a_b=dcb, ln_a_g=dlg, ln_a_b=dlb, pool_w=dpw, pool_b=dpb,
                         pool_scale=dps)
        else:
            xc, r, ig, hs = acts
            dxc, dgate, dwrg, dwig, dbrg, dbig, dlam = _odd_bwd_a(p, xc, r, ig, hs, dy, w["w_rg"], w["w_ig"],
                                                                  w["lam"], dep, tm=tx)
            dp, dccw, dccb = _odd_bwd_b(p, dxc, dgate, w["conv_w"], tm=tx)
            grads = dict(conv_c_w=dccw[:LRU_CONV_K], conv_c_b=dccb, w_rg=dwrg, b_rg=dbrg, w_ig=dwig, b_ig=dbig,
                         lru_lambda=dlam)
        dep = on_grads(layer, {"w_in_" + sfx: _in_proj_bwd_w(hn, dp, N_DEV, tm=tm)}, dep, False)
        dh, grads["norm_" + sfx] = _in_proj_bwd_x(dp, w["w_in"], h_in, w["norm"], dh, dep, tm=tm)
        if layer == N_LAYERS - 1:
            grads["final_norm"] = d_final
        dep = on_grads(layer, grads, dep, True)
    return loss, dh


def _slot(px, py, pc):
    return 4 * px + 2 * py + pc


def _peers(x, y, c):
    return [(1 - x if k & 4 else x, 1 - y if k & 2 else y, 1 - c if k & 1 else c) for k in range(1, N_DEV)]


def _all_gather(arrs, name):
    n = len(arrs)

    def body(*refs):
        ins, outs = refs[:n], refs[n:2 * n]
        send_sems, recv_sems, local_sems = refs[2 * n:]
        x, y, c = lax.axis_index("x"), lax.axis_index("y"), lax.axis_index("c")
        me, sibling = (x, y, c), (x, y, 1 - c)
        chips = [(1 - x, y), (x, 1 - y), (1 - x, 1 - y)]

        def copy(a, k, block, to, src=None):
            rows = outs[a].at[_slot(*block)]
            return pltpu.make_async_remote_copy(
                src_ref=rows if src is None else src, dst_ref=rows, send_sem=send_sems.at[a, k],
                recv_sem=recv_sems.at[a, k], device_id=to, device_id_type=MESH)

        mine = [pltpu.make_async_copy(ins[a], outs[a].at[_slot(*me)], local_sems.at[a]) for a in range(n)]
        for cp in mine:
            cp.start()
        first = []
        for a in range(n):
            first.append(copy(a, 0, me, sibling, src=ins[a]))
            first += [copy(a, 1 + j, me, (*chip, c), src=ins[a]) for j, chip in enumerate(chips)]
        for cp in first:
            cp.start()
        passed = []
        for j, chip in enumerate(chips):
            for a in range(n):
                copy(a, 1 + j, (*chip, c), me).wait_recv()
                fwd = copy(a, 4 + j, (*chip, c), sibling)
                fwd.start()
                passed.append(fwd)
        for a in range(n):
            copy(a, 0, sibling, me).wait_recv()
            for j, chip in enumerate(chips):
                copy(a, 4 + j, (*chip, 1 - c), me).wait_recv()
        for cp in first + passed:
            cp.wait_send()
        for cp in mine:
            cp.wait()

    return pl.pallas_call(
        body, name=name,
        in_specs=[ANY] * n, out_specs=[ANY] * n,
        out_shape=[jax.ShapeDtypeStruct((N_DEV,) + a.shape, a.dtype) for a in arrs],
        scratch_shapes=[pltpu.SemaphoreType.DMA((n, 7)), pltpu.SemaphoreType.DMA((n, 7)),
                        pltpu.SemaphoreType.DMA((n,))],
    )(*arrs)


def _exchange_copy(scatter, src_ref, land_ref, peer, send_sem, recv_sem, landing):
    return pltpu.make_async_remote_copy(
        src_ref=src_ref.at[_slot(*peer)] if scatter else src_ref, dst_ref=land_ref.at[landing],
        send_sem=send_sem, recv_sem=recv_sem, device_id=peer, device_id_type=MESH)


def _exchange_start(groups, scatter, deps, name):
    flat = [pair for g in groups for pair in g]
    n, ng = len(flat), len(groups)

    def body(*refs):
        src_refs, land_refs = refs[:n], refs[n:2 * n]
        outs = refs[2 * n + len(deps):]
        sems, token = outs[:2 * ng], outs[2 * ng + 2 * n]
        x, y, c = lax.axis_index("x"), lax.axis_index("y"), lax.axis_index("c")
        me = _slot(x, y, c)
        base = 0
        for gi, g in enumerate(groups):
            for k, peer in enumerate(_peers(x, y, c)):
                for ai in range(len(g)):
                    _exchange_copy(scatter, src_refs[base + ai], land_refs[base + ai], peer,
                                   sems[2 * gi].at[ai * N_PEERS + k], sems[2 * gi + 1].at[ai * N_PEERS + k],
                                   me).start()
            base += len(g)
        token[...] = jnp.zeros_like(token)

    operands = [pltpu.with_memory_space_constraint(a, pltpu.HBM) for a in
                [s for s, _ in flat] + [l for _, l in flat]]
    out_shape = []
    for g in groups:
        out_shape += [pltpu.SemaphoreType.DMA((len(g) * N_PEERS,))] * 2
    out_shape += [pltpu.HBM(a.shape, a.dtype) for a in operands]
    out_shape.append(jax.ShapeDtypeStruct((8, 128), F32))
    outs = pl.pallas_call(
        body, name=name, out_shape=out_shape,
        in_specs=[HBM] * (2 * n) + [ANY] * len(deps),
        out_specs=[SEM] * (2 * ng) + [HBM] * (2 * n) + [pl.BlockSpec(memory_space=pltpu.VMEM)],
        input_output_aliases={i: 2 * ng + i for i in range(2 * n)},
        compiler_params=pltpu.CompilerParams(has_side_effects=pltpu.SideEffectType.DATAFLOW_SIDE_EFFECTING),
    )(*operands, *deps)
    handles, base = [], 0
    for gi, g in enumerate(groups):
        srcs = outs[2 * ng + base:2 * ng + base + len(g)]
        lands = outs[2 * ng + n + base:2 * ng + n + base + len(g)]
        handles.append((outs[2 * gi], outs[2 * gi + 1], list(srcs), list(lands)))
        base += len(g)
    return handles, outs[-1]


def _exchange_wait(handle, scatter, after, name):
    send_sems, recv_sems, srcs, lands = handle
    n = len(srcs)

    def body(*refs):
        src_refs, land_refs = refs[:n], refs[n:2 * n]
        send_ref, recv_ref = refs[2 * n], refs[2 * n + 1]
        x, y, c = lax.axis_index("x"), lax.axis_index("y"), lax.axis_index("c")
        for k, peer in enumerate(_peers(x, y, c)):
            for a in range(n):
                cp = _exchange_copy(scatter, src_refs[a], land_refs[a], peer, send_ref.at[a * N_PEERS + k],
                                    recv_ref.at[a * N_PEERS + k],
                                    _slot(*peer))
                cp.wait_send()
                cp.wait_recv()

    outs = pl.pallas_call(
        body, name=name,
        out_shape=[pltpu.HBM(a.shape, a.dtype) for a in srcs + lands],
        in_specs=[HBM] * (2 * n) + [SEM, SEM] + [ANY] * len(after),
        out_specs=[HBM] * (2 * n),
        input_output_aliases={i: i for i in range(2 * n)},
        compiler_params=pltpu.CompilerParams(has_side_effects=pltpu.SideEffectType.DATAFLOW_SIDE_EFFECTING),
    )(*srcs, *lands, send_sems, recv_sems, *after)
    return list(outs[n:])


def _sum_parts(parts, *, tr):
    np_, r, c = parts.shape

    def body(p_ref, o_ref):
        acc = p_ref[0].astype(F32)
        for k in range(1, np_):
            acc = acc + p_ref[k].astype(F32)
        o_ref[...] = acc

    return pl.pallas_call(
        body, name="sum_parts", grid=(r // tr,),
        in_specs=[pl.BlockSpec((np_, tr, c), lambda i: (0, i, 0))],
        out_specs=pl.BlockSpec((tr, c), lambda i: (i, 0)),
        out_shape=jax.ShapeDtypeStruct((r, c), F32),
        compiler_params=_params("arbitrary"),
    )(parts)


def _adamw_math(w, g, m, v):
    c1 = 1.0 - ADAM_B1 ** ADAM_STEP
    c2 = 1.0 - ADAM_B2 ** ADAM_STEP
    nm = ADAM_B1 * m + (1.0 - ADAM_B1) * g
    nv = ADAM_B2 * v + (1.0 - ADAM_B2) * (g * g)
    delta = -ADAM_LR * ((nm / c1) / (jnp.sqrt(nv / c2) + ADAM_EPS) + ADAM_WD * w)
    return delta, nm, nv


def _row_tile(r):
    for cand in (256, 128, 64, 32, 16, 8):
        if r % cand == 0 and r > cand:
            return cand
    return r


def _adamw(items, layer0, bufs, name):
    ni = len(items)
    nl = items[0][1].shape[1]
    tiles = [_row_tile(w.shape[1]) for w, _, _, _ in items]
    steps = [w.shape[1] // tr for (w, _, _, _), tr in zip(items, tiles)]
    ns = steps[0]
    assert all(s == ns for s in steps)
    nb = 0 if bufs is None else 4 * ni

    def body(*refs):
        ins, outs = refs[:4 * ni], refs[4 * ni + nb:]
        for k in range(ni):
            w_ref, p_ref, m_ref, v_ref = ins[4 * k:4 * k + 4]
            g = p_ref[0, 0].astype(F32)
            for s in range(1, p_ref.shape[0]):
                g = g + p_ref[s, 0].astype(F32)
            delta, nm, nv = _adamw_math(w_ref[0], g, m_ref[0], v_ref[0])
            g_ref, d_ref, nm_ref, nv_ref = outs[4 * k:4 * k + 4]
            g_ref[0], d_ref[0], nm_ref[0], nv_ref[0] = g, delta, nm, nv

    in_specs, out_specs, out_shape, operands = [], [], [], []
    for (w, parts, m, v), tr in zip(items, tiles):
        blk = pl.BlockSpec((1, tr, w.shape[2]), lambda l, i: (layer0 + l, i, 0))
        in_specs += [blk, pl.BlockSpec((parts.shape[0], 1, tr, w.shape[2]), lambda l, i: (0, l, i, 0)), blk, blk]
        operands += [w, parts, m, v]
        out_specs += [blk] * 4
        out_shape += [jax.ShapeDtypeStruct(w.shape, F32)] * 4
    if bufs is not None:
        in_specs += [ANY] * nb
        operands += [b for item in bufs for b in item]
    outs = pl.pallas_call(
        body, name=name, grid=(nl, ns), in_specs=in_specs, out_specs=out_specs, out_shape=out_shape,
        input_output_aliases={4 * ni + i: i for i in range(nb)},
        compiler_params=_params("arbitrary", "arbitrary"),
    )(*operands)
    return [tuple(outs[4 * k:4 * k + 4]) for k in range(ni)]


NAMES = ("norm_even", "w_in_even", "conv_a_w", "conv_a_b", "ln_a_g", "ln_a_b", "pool_w", "pool_b", "pool_scale",
         "w_out_even", "norm_odd", "w_in_odd", "conv_c_w", "conv_c_b", "w_rg", "b_rg", "w_ig", "b_ig", "lru_lambda",
         "w_out_odd", "final_norm")
SMALL_GATHERED = ("conv_a_w", "pool_b", "norm_odd", "conv_c_w", "conv_c_b", "b_rg", "b_ig", "lru_lambda")
BIG = (("w_in_even", "w_out_even"), ("w_in_odd", "w_out_odd"))
SMALL = (("conv_a_w", "pool_b", "pool_w"), ("norm_odd", "conv_c_w", "conv_c_b", "b_rg", "b_ig", "lru_lambda"))
REPLICATED = (("norm_even", "conv_a_b", "ln_a_g", "ln_a_b", "pool_scale"), ("w_rg", "w_ig"))
PACK_ROW = 1024


def _pack_rows(flat2d):
    pad = (-flat2d.shape[1]) % PACK_ROW
    return jnp.pad(flat2d, ((0, 0), (0, pad))).reshape(flat2d.shape[0], -1, 128)


def _unpack(flat, shapes):
    out, off = [], 0
    for s in shapes:
        n = 1
        for d in s:
            n *= d
        out.append(flat[..., off:off + n].reshape(flat.shape[:-1] + tuple(s)))
        off += n
    return out


def _to_global(name, g):
    if name in ("conv_a_w", "pool_b", "conv_c_w"):
        return jnp.transpose(g, (1, 2, 0, 3)).reshape(g.shape[1], g.shape[2], -1)
    if name == "pool_w":
        return jnp.transpose(g, (1, 2, 0, 3, 4)).reshape(2, 4, POOL_GW, POOL_GW)
    return jnp.transpose(g, (1, 0, 2)).reshape(g.shape[1], -1)


def _to_blocks(name, g):
    if name == "conv_a_w":
        return jnp.transpose(g.reshape(CONV_K, N_DEV, -1), (1, 0, 2))
    if name == "conv_c_w":
        return jnp.transpose(g.reshape(LRU_CONV_K, N_DEV, -1), (1, 0, 2))
    if name == "pool_b":
        return jnp.transpose(g.reshape(4, N_DEV, -1), (1, 0, 2))
    if name == "pool_w":
        return jnp.transpose(g.reshape(4, N_DEV, POOL_GW // N_DEV, POOL_GW), (1, 0, 2, 3))
    return g.reshape(N_DEV, -1)


def _as3d(a):
    if a.ndim == 1:
        return a.reshape(1, 1, -1)
    if a.ndim == 2:
        return a.reshape(a.shape[0], 1, a.shape[1])
    return a.reshape(a.shape[0], -1, a.shape[-1])


def kernel(x, norm_even, w_in_even, conv_a_w, conv_a_b, ln_a_g, ln_a_b, pool_w, pool_b, pool_scale, w_out_even, norm_odd, w_in_odd, conv_c_w, conv_c_b, w_rg, b_rg, w_ig, b_ig, lru_lambda, w_out_odd, final_norm, loss_target, m_norm_even, m_w_in_even, m_conv_a_w, m_conv_a_b, m_ln_a_g, m_ln_a_b, m_pool_w, m_pool_b, m_pool_scale, m_w_out_even, m_norm_odd, m_w_in_odd, m_conv_c_w, m_conv_c_b, m_w_rg, m_b_rg, m_w_ig, m_b_ig, m_lru_lambda, m_w_out_odd, m_final_norm, v_norm_even, v_w_in_even, v_conv_a_w, v_conv_a_b, v_ln_a_g, v_ln_a_b, v_pool_w, v_pool_b, v_pool_scale, v_w_out_even, v_norm_odd, v_w_in_odd, v_conv_c_w, v_conv_c_b, v_w_rg, v_b_rg, v_w_ig, v_b_ig, v_lru_lambda, v_w_out_odd, v_final_norm):
    w_loc = dict(zip(NAMES, [norm_even, w_in_even, conv_a_w, conv_a_b, ln_a_g, ln_a_b, pool_w, pool_b, pool_scale,
                             w_out_even, norm_odd, w_in_odd, conv_c_w, conv_c_b, w_rg, b_rg, w_ig, b_ig, lru_lambda,
                             w_out_odd, final_norm]))
    m_loc = dict(zip(NAMES, [m_norm_even, m_w_in_even, m_conv_a_w, m_conv_a_b, m_ln_a_g, m_ln_a_b, m_pool_w, m_pool_b,
                             m_pool_scale, m_w_out_even, m_norm_odd, m_w_in_odd, m_conv_c_w, m_conv_c_b, m_w_rg,
                             m_b_rg, m_w_ig, m_b_ig, m_lru_lambda, m_w_out_odd, m_final_norm]))
    v_loc = dict(zip(NAMES, [v_norm_even, v_w_in_even, v_conv_a_w, v_conv_a_b, v_ln_a_g, v_ln_a_b, v_pool_w, v_pool_b,
                             v_pool_scale, v_w_out_even, v_norm_odd, v_w_in_odd, v_conv_c_w, v_conv_c_b, v_w_rg,
                             v_b_rg, v_w_ig, v_b_ig, v_lru_lambda, v_w_out_odd, v_final_norm]))
    me = _slot(lax.axis_index("x"), lax.axis_index("y"), lax.axis_index("c"))

    def landing(own):
        zone = lax.empty((N_DEV,) + own.shape[1:], own.dtype)
        return lax.dynamic_update_slice(zone, own, (me,) + (0,) * (own.ndim - 1))

    small_shapes = [w_loc[n].shape for n in SMALL_GATHERED]
    small = jnp.concatenate([w_loc[n].reshape(1, -1) for n in SMALL_GATHERED], axis=1)
    first = _all_gather([w_in_even[0].astype(BF16), pool_w.astype(BF16), _pack_rows(small)[0]], "gather_first")
    g_small = dict(zip(SMALL_GATHERED, [_to_global(n, g) for n, g in
                                        zip(SMALL_GATHERED, _unpack(first[2].reshape(N_DEV, -1), small_shapes))]))
    pool_w_all = _to_global("pool_w", first[1])

    def gather_start(shards, deps, name):
        groups = [[(s.astype(BF16), landing(s.astype(BF16)[None])) for s in g] for g in shards]
        return _exchange_start(groups, False, deps, name)

    gathers = {}
    (gathers["w_out_0"], gathers[1]), token_1 = gather_start([[w_out_even[0]], [w_in_odd[0], w_out_odd[0]]],
                                                             [first[0]], "gather_start_1")
    next_shards = {1: [w_in_even[1], w_out_even[1]], 2: [w_in_odd[1], w_out_odd[1]]}

    def layer_weights(layer, h):
        j = layer // 2
        dep = h
        if layer == 0:
            w_in, dep = first[0], token_1
            w_out = lambda y: _exchange_wait(gathers["w_out_0"], False, [y], "gather_wait_out_0")[0]
        else:
            w_in, w_out_now = _exchange_wait(gathers[layer], False, [h], f"gather_wait_{layer}")
            w_out = lambda y: w_out_now
            if layer in next_shards:
                (gathers[layer + 1],), dep = gather_start([next_shards[layer]], [w_in], f"gather_start_{layer + 1}")
        row = lambda a: a[j][None]
        if layer % 2 == 0:
            return dict(dep=dep, norm=row(norm_even), w_in=w_in, w_out=lambda y: w_out(y).reshape(W_EVEN_MIX, D_MODEL),
                        conv_w=jnp.pad(g_small["conv_a_w"][j], ((0, 1), (0, 0))), conv_b=row(conv_a_b),
                        ln_g=row(ln_a_g), ln_b=row(ln_a_b), pool_w=pool_w_all[j],
                        pool_b=g_small["pool_b"][j].reshape(1, W_POOL), pool_scale=row(pool_scale))
        return dict(dep=dep, norm=row(g_small["norm_odd"]), w_in=w_in, w_out=lambda y: w_out(y).reshape(W_LRU, D_MODEL),
                    conv_w=jnp.pad(g_small["conv_c_w"][j], ((0, 4), (0, 0))), conv_b=row(g_small["conv_c_b"]),
                    w_rg=w_rg[j].astype(BF16), b_rg=row(g_small["b_rg"]), w_ig=w_ig[j].astype(BF16),
                    b_ig=row(g_small["b_ig"]), lam=row(g_small["lru_lambda"]))

    pending, scatters, small_layout = {}, {}, {}
    last_token = []

    def on_grads(layer, grads, dep, last):
        par = layer % 2
        w_in_name, w_out_name = BIG[par]
        pending.setdefault(layer, {}).update(grads)
        have = pending[layer]
        send = {}
        if w_in_name in have and (layer == 0 or last):
            send["in"] = have.pop(w_in_name)
        if w_out_name in have and (layer == 0 or last):
            send["out"] = have.pop(w_out_name).reshape(N_DEV, -1, D_MODEL)
        if last:
            small_blocks = [_to_blocks(n, have[n]) for n in SMALL[par]]
            rep_names = list(REPLICATED[par]) + (["final_norm"] if "final_norm" in have else [])
            rep = jnp.concatenate([have[n].reshape(-1) for n in rep_names]).reshape(N_DEV, -1)
            small_layout[layer] = ([b.shape[1:] for b in small_blocks], rep.shape[1])
            send["small"] = _pack_rows(jnp.concatenate([b.reshape(N_DEV, -1) for b in small_blocks] + [rep], axis=1))
        if not send:
            return dep
        pairs = [(s, landing(lax.dynamic_slice_in_dim(s, me, 1, 0))) for s in send.values()]
        (handle,), token = _exchange_start([pairs], True, [], f"scatter_start_{layer}_{'_'.join(send)}")
        scatters.setdefault(layer, []).append((list(send), handle))
        last_token[:] = [token]
        return token

    loss, grad_x = _local_step(x[0], loss_target[0], layer_weights, final_norm[None], on_grads)

    w3 = {n: _as3d(w_loc[n]) for n in NAMES}
    m3 = {n: _as3d(m_loc[n]) for n in NAMES}
    v3 = {n: _as3d(v_loc[n]) for n in NAMES}
    results, rep_parts = {}, {}
    after = list(last_token)
    for layer in (3, 2, 1, 0):
        par, j = layer % 2, layer // 2
        got = {}
        for keys, handle in scatters[layer]:
            got.update(zip(keys, _exchange_wait(handle, True, after, f"scatter_wait_{layer}_{'_'.join(keys)}")))
        got_in, got_out, got_small = got["in"], got["out"], got["small"]
        shapes, rep_len = small_layout[layer]
        *small_parts, rep_parts[layer] = _unpack(got_small.reshape(N_DEV, -1), list(shapes) + [(rep_len,)])
        names = list(BIG[par]) + list(SMALL[par])
        parts = [got_in, got_out] + small_parts
        big_items, small_items = [], []
        for n, pt in zip(names, parts):
            item = (w3[n], pt.reshape((N_DEV, 1) + w3[n].shape[1:]), m3[n], v3[n])
            (big_items if w3[n].shape[1] >= 128 else small_items).append((n, item))
        for n, item in big_items:
            results[n] = _adamw([item], j, [results[n]] if n in results else None, f"adamw_{n}_{layer}")[0]
        snames = [n for n, _ in small_items]
        prev = [results[n] for n in snames] if snames[0] in results else None
        for n, r in zip(snames, _adamw([it for _, it in small_items], j, prev, f"adamw_small_{layer}")):
            results[n] = r
        after = [results[names[0]][1]]

    rep_pack = _pack_rows(jnp.concatenate([rep_parts[l] for l in range(N_LAYERS)], axis=1))
    rep_mine = _sum_parts(rep_pack, tr=rep_pack.shape[1])
    rep_all = _all_gather([rep_mine], "gather_replicated")[0].reshape(N_DEV, -1)
    rep_grads, off = {}, 0
    for layer in range(N_LAYERS):
        rep_len = small_layout[layer][1]
        flat = rep_all[:, off:off + rep_len].reshape(-1)
        off += rep_len
        rep_names = list(REPLICATED[layer % 2]) + (["final_norm"] if layer == N_LAYERS - 1 else [])
        for n, g in zip(rep_names, _unpack(flat, [w_loc[n].shape[1:] if n != "final_norm" else w_loc[n].shape
                                                  for n in rep_names])):
            rep_grads.setdefault(n, []).append(g)
    rep_items = {n: (w3[n], _as3d(jnp.stack(rep_grads[n]) if n != "final_norm" else rep_grads[n][0])[None],
                     m3[n], v3[n]) for par in range(2) for n in REPLICATED[par]}
    rep_items["final_norm"] = (w3["final_norm"], _as3d(rep_grads["final_norm"][0])[None], m3["final_norm"],
                               v3["final_norm"])
    for n in ("w_rg", "w_ig"):
        results[n] = _adamw([rep_items[n]], 0, None, f"adamw_{n}")[0]
    vec_names = list(REPLICATED[0])
    for n, r in zip(vec_names, _adamw([rep_items[n] for n in vec_names], 0, None, "adamw_replicated")):
        results[n] = r
    results["final_norm"] = _adamw([rep_items["final_norm"]], 0, None, "adamw_final_norm")[0]

    total = lax.psum(loss[0, 0], ("x", "y", "c"))
    outs = [[results[n][k].reshape(w_loc[n].shape) for n in NAMES] for k in range(4)]
    return (total, grad_x[None], *outs[0], *outs[1], *outs[2], *outs[3])
```

```python
import functools

import jax
import jax.numpy as jnp
from jax import lax
from jax.experimental import pallas as pl
from jax.experimental.pallas import tpu as pltpu

F32 = jnp.float32
BF16 = jnp.bfloat16

N_DEV = 8
N_PEERS = N_DEV - 1
N_LAYERS = 4
D_MODEL = 1024
EPS_RMS = 1e-6
EPS_LN = 1e-5
W_CONV = 1024
CONV_K = 31
W_POOL = 1024
POOL_WINDOWS = (2, 4, 8, 16)
POOL_GW = 256
W_EVEN_IN = 5120
W_EVEN_MIX = 2048
LRU_HEADS = 12
LRU_HD = 128
W_LRU = 1536
LRU_CONV_K = 4
LRU_C = 8.0
ADAM_LR = 0.001
ADAM_B1 = 0.9
ADAM_B2 = 0.999
ADAM_EPS = 1e-08
ADAM_WD = 0.01
ADAM_STEP = 10

HALO = 32
HALO_C = 8
TM_MATMUL = 512
TM_STREAM = 1024
TM_WGRAD = 2048
TM_MIXER = 256
CONV_ROWS = 128
VMEM_LIMIT = 56 * 1024 * 1024
MESH = pl.DeviceIdType.MESH
ANY = pl.BlockSpec(memory_space=pl.ANY)
HBM = pl.BlockSpec(memory_space=pltpu.HBM)
SEM = pl.BlockSpec(memory_space=pltpu.SEMAPHORE)


def _params(*sem):
    return pltpu.CompilerParams(dimension_semantics=sem, vmem_limit_bytes=VMEM_LIMIT)


def _sigmoid(z):
    return jax.nn.sigmoid(z)


def _dsilu(z, s):
    return s * (1.0 + z * (1.0 - s))


def _full(shape):
    nd = len(shape)
    return pl.BlockSpec(shape, lambda *_: (0,) * nd)


def _norm_matmul(h, g, w, dep, *, tm):
    t, d = h.shape
    nd, _, nb = w.shape

    def body(h_ref, g_ref, w_ref, dep_ref, p_ref, hn_ref):
        @pl.when(pl.program_id(1) == 0)
        def _():
            x = h_ref[...]
            r = lax.rsqrt(jnp.mean(x * x, axis=-1, keepdims=True) + EPS_RMS)
            hn_ref[...] = ((x * r) * g_ref[...]).astype(BF16)

        p_ref[...] = jnp.dot(hn_ref[...], w_ref[0], preferred_element_type=F32)

    return pl.pallas_call(
        body, name="norm_matmul", grid=(t // tm, nd),
        in_specs=[pl.BlockSpec((tm, d), lambda i, j: (i, 0)), _full((1, d)),
                  pl.BlockSpec((1, d, nb), lambda i, j: (j, 0, 0)), ANY],
        out_specs=[pl.BlockSpec((tm, nb), lambda i, j: (i, j)), pl.BlockSpec((tm, d), lambda i, j: (i, 0))],
        out_shape=[jax.ShapeDtypeStruct((t, nd * nb), F32), jax.ShapeDtypeStruct((t, d), BF16)],
        compiler_params=_params("arbitrary", "arbitrary"),
    )(h, g, w, dep)


def _out_proj(h, y, w, *, tm):
    t, d = h.shape
    k = y.shape[1]

    def body(h_ref, y_ref, w_ref, o_ref):
        o_ref[...] = h_ref[...] + jnp.dot(y_ref[...], w_ref[...], preferred_element_type=F32)

    return pl.pallas_call(
        body, name="out_proj", grid=(t // tm,),
        in_specs=[pl.BlockSpec((tm, d), lambda i: (i, 0)), pl.BlockSpec((tm, k), lambda i: (i, 0)), _full((k, d))],
        out_specs=pl.BlockSpec((tm, d), lambda i: (i, 0)),
        out_shape=jax.ShapeDtypeStruct((t, d), F32),
        compiler_params=_params("arbitrary"),
    )(h, y, w)


def _out_proj_bwd(dh, y, w, dep, *, tm):
    t, d = dh.shape
    k = y.shape[1]
    nt = t // tm

    def body(dh_ref, y_ref, w_ref, dep_ref, dy_ref, dw_ref, acc):
        i = pl.program_id(0)
        g = dh_ref[...].astype(BF16)
        dy_ref[...] = lax.dot_general(g, w_ref[...], (((1,), (1,)), ((), ())), preferred_element_type=F32)
        part = lax.dot_general(y_ref[...], g, (((0,), (0,)), ((), ())), preferred_element_type=F32)

        @pl.when(i == 0)
        def _():
            acc[...] = part

        @pl.when(i > 0)
        def _():
            acc[...] += part

        @pl.when(i == nt - 1)
        def _():
            dw_ref[...] = acc[...].astype(BF16)

    return pl.pallas_call(
        body, name="out_proj_bwd", grid=(nt,),
        in_specs=[pl.BlockSpec((tm, d), lambda i: (i, 0)), pl.BlockSpec((tm, k), lambda i: (i, 0)), _full((k, d)),
                  ANY],
        out_specs=[pl.BlockSpec((tm, k), lambda i: (i, 0)), _full((k, d))],
        out_shape=[jax.ShapeDtypeStruct((t, k), F32), jax.ShapeDtypeStruct((k, d), BF16)],
        scratch_shapes=[pltpu.VMEM((k, d), F32)],
        compiler_params=_params("arbitrary"),
    )(dh, y, w, dep)


def _in_proj_bwd_x(dp, w, h, g, dh_out, dep, *, tm):
    t, d = h.shape
    nd, _, nb = w.shape
    nt = t // tm

    def body(dp_ref, w_ref, h_ref, g_ref, dho_ref, dep_ref, dh_ref, dg_ref, acc):
        i, j = pl.program_id(0), pl.program_id(1)
        part = lax.dot_general(dp_ref[...], w_ref[0], (((1,), (1,)), ((), ())), preferred_element_type=F32)

        @pl.when(j == 0)
        def _():
            acc[...] = part

        @pl.when(j > 0)
        def _():
            acc[...] += part

        @pl.when(j == nd - 1)
        def _():
            x = h_ref[...]
            r = lax.rsqrt(jnp.mean(x * x, axis=-1, keepdims=True) + EPS_RMS)
            dy = acc[...]
            gd = dy * g_ref[...]
            m = jnp.mean(gd * x, axis=-1, keepdims=True)
            dh_ref[...] = dho_ref[...] + r * gd - x * (r * r * r * m)
            dgp = jnp.sum(dy * x * r, axis=0, keepdims=True)

            @pl.when(i == 0)
            def _():
                dg_ref[...] = dgp

            @pl.when(i > 0)
            def _():
                dg_ref[...] += dgp

    return pl.pallas_call(
        body, name="in_proj_bwd_x", grid=(nt, nd),
        in_specs=[pl.BlockSpec((tm, nb), lambda i, j: (i, j)), pl.BlockSpec((1, d, nb), lambda i, j: (j, 0, 0)),
                  pl.BlockSpec((tm, d), lambda i, j: (i, 0)), _full((1, d)), pl.BlockSpec((tm, d), lambda i, j: (i, 0)),
                  ANY],
        out_specs=[pl.BlockSpec((tm, d), lambda i, j: (i, 0)), _full((1, d))],
        out_shape=[jax.ShapeDtypeStruct((t, d), F32), jax.ShapeDtypeStruct((1, d), F32)],
        scratch_shapes=[pltpu.VMEM((tm, d), F32)],
        compiler_params=_params("arbitrary", "arbitrary"),
    )(dp, w, h, g, dh_out, dep)


def _in_proj_bwd_w(hn, dp, nd, *, tm):
    t, d = hn.shape
    nb = dp.shape[1] // nd
    nt = t // tm

    def body(hn_ref, dp_ref, dw_ref, acc):
        i = pl.program_id(1)
        part = lax.dot_general(hn_ref[...], dp_ref[...], (((0,), (0,)), ((), ())), preferred_element_type=F32)

        @pl.when(i == 0)
        def _():
            acc[...] = part

        @pl.when(i > 0)
        def _():
            acc[...] += part

        @pl.when(i == nt - 1)
        def _():
            dw_ref[0] = acc[...].astype(BF16)

    return pl.pallas_call(
        body, name="in_proj_bwd_w", grid=(nd, nt),
        in_specs=[pl.BlockSpec((tm, d), lambda j, i: (i, 0)), pl.BlockSpec((tm, nb), lambda j, i: (i, j))],
        out_specs=pl.BlockSpec((1, d, nb), lambda j, i: (j, 0, 0)),
        out_shape=jax.ShapeDtypeStruct((nd, d, nb), BF16),
        scratch_shapes=[pltpu.VMEM((d, nb), F32)],
        compiler_params=_params("arbitrary", "arbitrary"),
    )(hn, dp)


def _loss_head(h, g, target, *, tm):
    t, d = h.shape
    nt = t // tm

    def body(h_ref, g_ref, t_ref, loss_ref, dh_ref, dg_ref):
        i = pl.program_id(0)
        x = h_ref[...]
        r = lax.rsqrt(jnp.mean(x * x, axis=-1, keepdims=True) + EPS_RMS)
        xr = x * r
        err = xr * g_ref[...] - t_ref[...]
        lp = 0.5 * jnp.sum(jnp.mean(err * err, axis=-1, keepdims=True), axis=0, keepdims=True)
        dy = err * (1.0 / d)
        gd = dy * g_ref[...]
        m = jnp.mean(gd * x, axis=-1, keepdims=True)
        dh_ref[...] = r * gd - x * (r * r * r * m)
        dgp = jnp.sum(dy * xr, axis=0, keepdims=True)

        @pl.when(i == 0)
        def _():
            loss_ref[...] = lp
            dg_ref[...] = dgp

        @pl.when(i > 0)
        def _():
            loss_ref[...] += lp
            dg_ref[...] += dgp

    return pl.pallas_call(
        body, name="loss_head", grid=(nt,),
        in_specs=[pl.BlockSpec((tm, d), lambda i: (i, 0)), _full((1, d)), pl.BlockSpec((tm, d), lambda i: (i, 0))],
        out_specs=[_full((1, 1)), pl.BlockSpec((tm, d), lambda i: (i, 0)), _full((1, d))],
        out_shape=[jax.ShapeDtypeStruct((1, 1), F32), jax.ShapeDtypeStruct((t, d), F32),
                   jax.ShapeDtypeStruct((1, d), F32)],
        compiler_params=_params("arbitrary"),
    )(h, g, target)


def _col(tm, w, c):
    return pl.BlockSpec((tm, w), lambda i: (i, c))


def _prev_halo(tm, rows, w, c):
    per = tm // rows
    return pl.BlockSpec((rows, w), lambda i: (jnp.maximum(i * per - 1, 0), c))


def _next_halo(tm, rows, w, c, t):
    per = tm // rows
    last = t // rows - 1
    return pl.BlockSpec((rows, w), lambda i: (jnp.minimum((i + 1) * per, last), c))


def _inv_count(first_row, rows, window):
    tpos = first_row + lax.broadcasted_iota(jnp.int32, (rows, 1), 0)
    return 1.0 / jnp.minimum(tpos + 1, window).astype(F32)


def _even_fwd(p, cw, cb, lg, lb, pw, pb, ps, *, tm):
    t = p.shape[0]
    wc = W_CONV

    def body(av, ag, agate, bv, bgate, avh, agh, bvh, cw_ref, cb_ref, lg_ref, lb_ref, pw_ref, pb_ref, ps_ref,
             y_ref, u1_ref, e_ref, d_ref, uext, vext):
        i = pl.program_id(0)
        keep = (i > 0).astype(F32)
        uext[0:HALO, :] = keep * (avh[...] * _sigmoid(agh[...]))
        uext[HALO:, :] = av[...] * _sigmoid(ag[...])
        vext[0:HALO, :] = keep * bvh[...]
        vext[HALO:, :] = bv[...]
        for c in range(0, wc, 128):
            for rb in range(0, tm, CONV_ROWS):
                acc = jnp.broadcast_to(cb_ref[:, c:c + 128], (CONV_ROWS, 128))
                for r in range(8):
                    shifted = uext[pl.ds(rb + 8 - r, CONV_ROWS + HALO - 8), c:c + 128]
                    for q in range(HALO // 8):
                        s = 8 * q + r
                        if s < CONV_K:
                            acc = acc + cw_ref[CONV_K - 1 - s:CONV_K - s, c:c + 128] * shifted[24 - 8 * q:24 - 8 * q + CONV_ROWS]
                u1_ref[rb:rb + CONV_ROWS, c:c + 128] = acc
        u1 = u1_ref[...]
        mu = jnp.mean(u1, axis=-1, keepdims=True)
        xc = u1 - mu
        rs = lax.rsqrt(jnp.mean(xc * xc, axis=-1, keepdims=True) + EPS_LN)
        u2 = (xc * rs) * lg_ref[...] + lb_ref[...]
        u3 = u2 * _sigmoid(u2)
        ga = agate[...]
        y_ref[:, 0:wc] = (u3 * (ga * _sigmoid(ga))).astype(BF16)
        for g, win in enumerate(POOL_WINDOWS):
            cs = slice(g * POOL_GW, (g + 1) * POOL_GW)
            s = vext[pl.ds(HALO, tm), cs]
            for j in range(1, win):
                s = s + vext[pl.ds(HALO - j, tm), cs]
            dg = s * _inv_count(i * tm, tm, win) - vext[pl.ds(HALO, tm), cs]
            dgb = dg.astype(BF16)
            d_ref[:, cs] = dgb
            eg = jnp.dot(dgb, pw_ref[g], preferred_element_type=F32) + pb_ref[:, cs]
            e_ref[:, cs] = eg
            gb = bgate[:, cs]
            y_ref[:, wc + g * POOL_GW:wc + (g + 1) * POOL_GW] = ((eg * ps_ref[:, cs]) * (gb * _sigmoid(gb))).astype(BF16)

    row = lambda w: pl.BlockSpec((tm, w), lambda i: (i, 0))
    return pl.pallas_call(
        body, name="even_fwd", grid=(t // tm,),
        in_specs=[_col(tm, wc, 0), _col(tm, wc, 1), _col(tm, wc, 2), _col(tm, wc, 3), _col(tm, wc, 4),
                  _prev_halo(tm, HALO, wc, 0), _prev_halo(tm, HALO, wc, 1), _prev_halo(tm, HALO, wc, 3),
                  _full((32, wc)), _full((1, wc)), _full((1, wc)), _full((1, wc)),
                  _full((4, POOL_GW, POOL_GW)), _full((1, wc)), _full((1, wc))],
        out_specs=[row(2 * wc), row(wc), row(wc), row(wc)],
        out_shape=[jax.ShapeDtypeStruct((t, 2 * wc), BF16), jax.ShapeDtypeStruct((t, wc), F32),
                   jax.ShapeDtypeStruct((t, wc), F32), jax.ShapeDtypeStruct((t, wc), BF16)],
        scratch_shapes=[pltpu.VMEM((tm + HALO, wc), F32), pltpu.VMEM((tm + HALO, wc), F32)],
        compiler_params=_params("arbitrary"),
    )(p, p, p, p, p, p, p, p, cw, cb, lg, lb, pw, pb, ps)


def _acc_out(i, ref, val):
    @pl.when(i == 0)
    def _():
        ref[...] = val

    @pl.when(i > 0)
    def _():
        ref[...] += val


def _even_bwd_a(p, u1, e, dmat, dy, lg, lb, pw, ps, dep, *, tm):
    t = p.shape[0]
    wc = W_CONV

    def body(agate, bgate, u1_ref, e_ref, d_ref, dya, dyb, lg_ref, lb_ref, pw_ref, ps_ref, dep_ref,
             du1_ref, dd_ref, dgat_ref, dlg_ref, dlb_ref, dpb_ref, dps_ref, dpw_ref):
        i = pl.program_id(0)

        @pl.when(i == 0)
        def _():
            dpw_ref[...] = jnp.zeros_like(dpw_ref)

        u1 = u1_ref[...]
        mu = jnp.mean(u1, axis=-1, keepdims=True)
        xc = u1 - mu
        rs = lax.rsqrt(jnp.mean(xc * xc, axis=-1, keepdims=True) + EPS_LN)
        xh = xc * rs
        u2 = xh * lg_ref[...] + lb_ref[...]
        s2 = _sigmoid(u2)
        ga = agate[...]
        sa = _sigmoid(ga)
        dy_a = dya[...]
        dgat_ref[:, 0:wc] = (dy_a * (u2 * s2) * _dsilu(ga, sa)).astype(BF16)
        du2 = dy_a * (ga * sa) * _dsilu(u2, s2)
        _acc_out(i, dlg_ref, jnp.sum(du2 * xh, axis=0, keepdims=True))
        _acc_out(i, dlb_ref, jnp.sum(du2, axis=0, keepdims=True))
        dxh = du2 * lg_ref[...]
        m1 = jnp.mean(dxh, axis=-1, keepdims=True)
        m2 = jnp.mean(dxh * xh, axis=-1, keepdims=True)
        du1_ref[...] = rs * (dxh - m1 - xh * m2)

        gb = bgate[...]
        sb = _sigmoid(gb)
        ev = e_ref[...]
        dy_b = dyb[...]
        dgat_ref[:, wc:2 * wc] = (dy_b * (ev * ps_ref[...]) * _dsilu(gb, sb)).astype(BF16)
        dz = dy_b * (gb * sb)
        _acc_out(i, dps_ref, jnp.sum(dz * ev, axis=0, keepdims=True))
        de = dz * ps_ref[...]
        _acc_out(i, dpb_ref, jnp.sum(de, axis=0, keepdims=True))
        for g in range(len(POOL_WINDOWS)):
            cs = slice(g * POOL_GW, (g + 1) * POOL_GW)
            deg = de[:, cs].astype(BF16)
            dd_ref[:, cs] = lax.dot_general(deg, pw_ref[g], (((1,), (1,)), ((), ())), preferred_element_type=F32)
            dpw_ref[g] += lax.dot_general(d_ref[:, cs], deg, (((0,), (0,)), ((), ())), preferred_element_type=F32)

    row = lambda w: pl.BlockSpec((tm, w), lambda i: (i, 0))
    return pl.pallas_call(
        body, name="even_bwd_a", grid=(t // tm,),
        in_specs=[_col(tm, wc, 2), _col(tm, wc, 4), row(wc), row(wc), row(wc), _col(tm, wc, 0), _col(tm, wc, 1),
                  _full((1, wc)), _full((1, wc)), _full((4, POOL_GW, POOL_GW)), _full((1, wc)), ANY],
        out_specs=[row(wc), row(wc), row(2 * wc), _full((1, wc)), _full((1, wc)), _full((1, wc)), _full((1, wc)),
                   _full((4, POOL_GW, POOL_GW))],
        out_shape=[jax.ShapeDtypeStruct((t, wc), F32), jax.ShapeDtypeStruct((t, wc), F32),
                   jax.ShapeDtypeStruct((t, 2 * wc), BF16)] + [jax.ShapeDtypeStruct((1, wc), F32)] * 4
                  + [jax.ShapeDtypeStruct((4, POOL_GW, POOL_GW), F32)],
        compiler_params=_params("arbitrary"),
    )(p, p, u1, e, dmat, dy, dy, lg, lb, pw, ps, dep)


def _even_bwd_b(p, du1, dd, dgat, cw, *, tm):
    t = p.shape[0]
    wc = W_CONV
    nt = t // tm

    def body(av, ag, avh, agh, du1_ref, du1n, dd_ref, ddn, dgat_ref, cw_ref, dp_ref, dcw_ref, dcb_ref,
             uext, gext, dext, du0, dcw8):
        i = pl.program_id(0)
        keep_p = (i > 0).astype(F32)
        keep_n = (i < nt - 1).astype(F32)

        @pl.when(i == 0)
        def _():
            dcw8[...] = jnp.zeros_like(dcw8)

        a = av[...]
        sg = _sigmoid(ag[...])
        uext[0:HALO, :] = keep_p * (avh[...] * _sigmoid(agh[...]))
        uext[HALO:, :] = a * sg
        gext[0:tm, :] = du1_ref[...]
        gext[tm:, :] = keep_n * du1n[...]
        for c in range(0, wc, 128):
            for rb in range(0, tm, CONV_ROWS):
                acc = jnp.zeros((CONV_ROWS, 128), F32)
                for r in range(8):
                    ahead = gext[pl.ds(rb + r, CONV_ROWS + HALO - 8), c:c + 128]
                    for q in range(HALO // 8):
                        s = 8 * q + r
                        if s < CONV_K:
                            acc = acc + cw_ref[CONV_K - 1 - s:CONV_K - s, c:c + 128] * ahead[8 * q:8 * q + CONV_ROWS]
                du0[rb:rb + CONV_ROWS, c:c + 128] = acc
                gcur = du1_ref[rb:rb + CONV_ROWS, c:c + 128]
                for r in range(8):
                    behind = uext[pl.ds(rb + 8 - r, CONV_ROWS + HALO - 8), c:c + 128]
                    for q in range(HALO // 8):
                        s = 8 * q + r
                        if s < CONV_K:
                            prod = gcur * behind[24 - 8 * q:24 - 8 * q + CONV_ROWS]
                            part = prod[0:8]
                            for o in range(8, CONV_ROWS, 8):
                                part = part + prod[o:o + 8]
                            k = CONV_K - 1 - s
                            dcw8[8 * k:8 * k + 8, c:c + 128] += part

        @pl.when(i == nt - 1)
        def _():
            for k in range(CONV_K):
                dcw_ref[k:k + 1, :] = jnp.sum(dcw8[8 * k:8 * k + 8, :], axis=0, keepdims=True)
            dcw_ref[CONV_K:32, :] = jnp.zeros((32 - CONV_K, wc), F32)

        _acc_out(i, dcb_ref, jnp.sum(du1_ref[...], axis=0, keepdims=True))
        g0 = du0[...]
        dp_ref[:, 0:wc] = (g0 * sg).astype(BF16)
        dp_ref[:, wc:2 * wc] = (g0 * a * sg * (1.0 - sg)).astype(BF16)
        dp_ref[:, 2 * wc:3 * wc] = dgat_ref[:, 0:wc]
        dp_ref[:, 4 * wc:5 * wc] = dgat_ref[:, wc:2 * wc]
        for g, win in enumerate(POOL_WINDOWS):
            cs = slice(g * POOL_GW, (g + 1) * POOL_GW)
            dext[0:tm, cs] = dd_ref[:, cs] * _inv_count(i * tm, tm, win)
            dext[tm:, cs] = keep_n * (ddn[:, cs] * _inv_count((i + 1) * tm, HALO, win))
            s = dext[pl.ds(0, tm), cs]
            for j in range(1, win):
                s = s + dext[pl.ds(j, tm), cs]
            dp_ref[:, 3 * wc + g * POOL_GW:3 * wc + (g + 1) * POOL_GW] = (s - dd_ref[:, cs]).astype(BF16)

    row = lambda w: pl.BlockSpec((tm, w), lambda i: (i, 0))
    return pl.pallas_call(
        body, name="even_bwd_b", grid=(nt,),
        in_specs=[_col(tm, wc, 0), _col(tm, wc, 1), _prev_halo(tm, HALO, wc, 0), _prev_halo(tm, HALO, wc, 1),
                  row(wc), _next_halo(tm, HALO, wc, 0, t), row(wc), _next_halo(tm, HALO, wc, 0, t), row(2 * wc),
                  _full((32, wc))],
        out_specs=[row(5 * wc), _full((32, wc)), _full((1, wc))],
        out_shape=[jax.ShapeDtypeStruct((t, 5 * wc), BF16), jax.ShapeDtypeStruct((32, wc), F32),
                   jax.ShapeDtypeStruct((1, wc), F32)],
        scratch_shapes=[pltpu.VMEM((tm + HALO, wc), F32), pltpu.VMEM((tm + HALO, wc), F32),
                        pltpu.VMEM((tm + HALO, wc), F32), pltpu.VMEM((tm, wc), F32), pltpu.VMEM((8 * 32, wc), F32)],
        compiler_params=_params("arbitrary"),
    )(p, p, p, p, du1, du1, dd, dd, dgat, cw)


def _softplus_neg(lam):
    z = -lam
    return jnp.maximum(z, 0.0) + jnp.log1p(jnp.exp(-jnp.abs(z)))


def _one_minus_exp(x):
    series = -x * (1.0 + x * (0.5 + x * (1.0 / 6.0 + x * (1.0 / 24.0))))
    return jnp.where(x > -0.02, series, 1.0 - jnp.exp(x))


def _odd_fwd(p, ccw, ccb, wrg, brg, wig, big, lam, *, tm):
    t = p.shape[0]
    wl = W_LRU
    ng = tm // 8

    def body(xr, gate, xrh, ccw_ref, ccb_ref, wrg_ref, brg_ref, wig_ref, big_ref, lam_ref,
             y_ref, xc_ref, r_ref, i_ref, hs_ref, xext, a_s, b_s, carry):
        i = pl.program_id(0)
        keep = (i > 0).astype(F32)
        xext[0:HALO_C, :] = keep * xrh[...]
        xext[HALO_C:, :] = xr[...]
        xc = jnp.broadcast_to(ccb_ref[...], (tm, wl))
        for k in range(LRU_CONV_K):
            xc = xc + ccw_ref[k:k + 1, :] * xext[pl.ds(HALO_C - (LRU_CONV_K - 1) + k, tm), :]
        xc_ref[...] = xc
        for h in range(LRU_HEADS):
            cs = slice(h * LRU_HD, (h + 1) * LRU_HD)
            xh = xc_ref[:, cs].astype(BF16)
            r_ref[:, cs] = _sigmoid(jnp.dot(xh, wrg_ref[h], preferred_element_type=F32) + brg_ref[:, cs])
            i_ref[:, cs] = _sigmoid(jnp.dot(xh, wig_ref[h], preferred_element_type=F32) + big_ref[:, cs])
        log_a = (-LRU_C * _softplus_neg(lam_ref[...])) * r_ref[...]
        a_s[...] = jnp.exp(log_a)
        b_s[...] = jnp.sqrt(_one_minus_exp(2.0 * log_a)) * (i_ref[...] * xc_ref[...])

        @pl.when(i == 0)
        def _():
            carry[...] = jnp.zeros_like(carry)

        rowi = lax.broadcasted_iota(jnp.int32, (8, wl), 0)

        def step(g, c):
            sl = pl.ds(pl.multiple_of(g * 8, 8), 8)
            aa, bb = a_s[sl, :], b_s[sl, :]
            for s in (1, 2, 4):
                m = rowi >= s
                a_sh = jnp.where(m, pltpu.roll(aa, s, 0), 1.0)
                b_sh = jnp.where(m, pltpu.roll(bb, s, 0), 0.0)
                bb = aa * b_sh + bb
                aa = aa * a_sh
            hv = bb + aa * c
            hs_ref[sl, :] = hv
            return hv[7:8, :]

        carry[...] = lax.fori_loop(0, ng, step, carry[...])
        gt = gate[...]
        y_ref[...] = (hs_ref[...] * (gt * _sigmoid(gt))).astype(BF16)

    row = lambda w: pl.BlockSpec((tm, w), lambda i: (i, 0))
    return pl.pallas_call(
        body, name="odd_fwd", grid=(t // tm,),
        in_specs=[_col(tm, wl, 0), _col(tm, wl, 1), _prev_halo(tm, HALO_C, wl, 0), _full((8, wl)), _full((1, wl)),
                  _full((LRU_HEADS, LRU_HD, LRU_HD)), _full((1, wl)), _full((LRU_HEADS, LRU_HD, LRU_HD)),
                  _full((1, wl)), _full((1, wl))],
        out_specs=[row(wl)] * 5,
        out_shape=[jax.ShapeDtypeStruct((t, wl), BF16)] + [jax.ShapeDtypeStruct((t, wl), F32)] * 4,
        scratch_shapes=[pltpu.VMEM((tm + HALO_C, wl), F32), pltpu.VMEM((tm, wl), F32), pltpu.VMEM((tm, wl), F32),
                        pltpu.VMEM((1, wl), F32)],
        compiler_params=_params("arbitrary"),
    )(p, p, p, ccw, ccb, wrg, brg, wig, big, lam)


def _odd_bwd_a(p, xc, r, ig, hs, dy, wrg, wig, lam, dep, *, tm):
    t = p.shape[0]
    wl = W_LRU
    nt = t // tm
    ng = tm // 8
    per = tm // HALO_C

    def body(gate, xc_ref, r_ref, i_ref, hs_ref, hsh, dy_ref, wrg_ref, wig_ref, lam_ref, dep_ref,
             dxc_ref, dgate_ref, dwrg_ref, dwig_ref, dbrg_ref, dbig_ref, dlam_ref,
             hext, a_s, q_s, g_s, dpr_s, dpi_s, carry):
        i = pl.program_id(0)
        ti = nt - 1 - i
        keep = (ti > 0).astype(F32)
        hext[0:HALO_C, :] = keep * hsh[...]
        hext[HALO_C:, :] = hs_ref[...]
        gt = gate[...]
        sg = _sigmoid(gt)
        dyv = dy_ref[...]
        dgate_ref[...] = (dyv * hs_ref[...] * _dsilu(gt, sg)).astype(BF16)
        q_s[...] = dyv * (gt * sg)
        sp = _softplus_neg(lam_ref[...])
        log_a = (-LRU_C * sp) * r_ref[...]
        a_s[...] = jnp.exp(log_a)

        @pl.when(i == 0)
        def _():
            carry[...] = jnp.zeros_like(carry)
            dwrg_ref[...] = jnp.zeros_like(dwrg_ref)
            dwig_ref[...] = jnp.zeros_like(dwig_ref)

        rowi = lax.broadcasted_iota(jnp.int32, (8, wl), 0)

        def step(gr, c):
            sl = pl.ds(pl.multiple_of((ng - 1 - gr) * 8, 8), 8)
            a0 = a_s[sl, :]
            al = jnp.where(rowi < 7, pltpu.roll(a0, 7, 0), 1.0)
            be = q_s[sl, :]
            for s in (1, 2, 4):
                m = rowi + s <= 7
                al_sh = jnp.where(m, pltpu.roll(al, 8 - s, 0), 1.0)
                be_sh = jnp.where(m, pltpu.roll(be, 8 - s, 0), 0.0)
                be = be + al * be_sh
                al = al * al_sh
            gv = be + al * c
            g_s[sl, :] = gv
            return (a0 * gv)[0:1, :]

        carry[...] = lax.fori_loop(0, ng, step, carry[...])

        gv = g_s[...]
        a = a_s[...]
        mult = jnp.sqrt(_one_minus_exp(2.0 * log_a))
        iv = i_ref[...]
        rv = r_ref[...]
        xcv = xc_ref[...]
        hprev = hext[pl.ds(HALO_C - 1, tm), :]
        dla = gv * hprev * a - (gv * iv * xcv) * (a * a) / mult
        di = gv * mult * xcv
        dpr = (dla * (-LRU_C * sp)) * rv * (1.0 - rv)
        dpi = di * iv * (1.0 - iv)
        dpr_s[...] = dpr
        dpi_s[...] = dpi
        dxc_ref[...] = gv * mult * iv
        dsp = jnp.sum(dla * rv, axis=0, keepdims=True) * (-LRU_C)
        _acc_out(i, dlam_ref, -dsp * _sigmoid(-lam_ref[...]))
        _acc_out(i, dbrg_ref, jnp.sum(dpr, axis=0, keepdims=True))
        _acc_out(i, dbig_ref, jnp.sum(dpi, axis=0, keepdims=True))
        for h in range(LRU_HEADS):
            cs = slice(h * LRU_HD, (h + 1) * LRU_HD)
            xh = xc_ref[:, cs].astype(BF16)
            dr_h = dpr_s[:, cs].astype(BF16)
            di_h = dpi_s[:, cs].astype(BF16)
            dxc_ref[:, cs] += (
                lax.dot_general(dr_h, wrg_ref[h], (((1,), (1,)), ((), ())), preferred_element_type=F32)
                + lax.dot_general(di_h, wig_ref[h], (((1,), (1,)), ((), ())), preferred_element_type=F32))
            dwrg_ref[h] += lax.dot_general(xh, dr_h, (((0,), (0,)), ((), ())), preferred_element_type=F32)
            dwig_ref[h] += lax.dot_general(xh, di_h, (((0,), (0,)), ((), ())), preferred_element_type=F32)

    rrow = lambda w: pl.BlockSpec((tm, w), lambda i: (nt - 1 - i, 0))
    hspec = pl.BlockSpec((HALO_C, wl), lambda i: (jnp.maximum((nt - 1 - i) * per - 1, 0), 0))
    wspec = _full((LRU_HEADS, LRU_HD, LRU_HD))
    return pl.pallas_call(
        body, name="odd_bwd_a", grid=(nt,),
        in_specs=[pl.BlockSpec((tm, wl), lambda i: (nt - 1 - i, 1)), rrow(wl), rrow(wl), rrow(wl), rrow(wl), hspec,
                  rrow(wl), wspec, wspec, _full((1, wl)), ANY],
        out_specs=[rrow(wl), rrow(wl), wspec, wspec, _full((1, wl)), _full((1, wl)), _full((1, wl))],
        out_shape=[jax.ShapeDtypeStruct((t, wl), F32), jax.ShapeDtypeStruct((t, wl), BF16),
                   jax.ShapeDtypeStruct((LRU_HEADS, LRU_HD, LRU_HD), F32),
                   jax.ShapeDtypeStruct((LRU_HEADS, LRU_HD, LRU_HD), F32)] + [jax.ShapeDtypeStruct((1, wl), F32)] * 3,
        scratch_shapes=[pltpu.VMEM((tm + HALO_C, wl), F32)] + [pltpu.VMEM((tm, wl), F32)] * 5
                       + [pltpu.VMEM((1, wl), F32)],
        compiler_params=_params("arbitrary"),
    )(p, xc, r, ig, hs, hs, dy, wrg, wig, lam, dep)


def _odd_bwd_b(p, dxc, dgate, ccw, *, tm):
    t = p.shape[0]
    wl = W_LRU
    nt = t // tm

    def body(xr, xrh, dxc_ref, dxcn, dgate_ref, ccw_ref, dp_ref, dcw_ref, dcb_ref, xext, gext):
        i = pl.program_id(0)

        @pl.when(i == 0)
        def _():
            dcw_ref[...] = jnp.zeros_like(dcw_ref)

        xext[0:HALO_C, :] = (i > 0).astype(F32) * xrh[...]
        xext[HALO_C:, :] = xr[...]
        gext[0:tm, :] = dxc_ref[...]
        gext[tm:, :] = (i < nt - 1).astype(F32) * dxcn[...]
        g = dxc_ref[...]
        acc = jnp.zeros((tm, wl), F32)
        for k in range(LRU_CONV_K):
            acc = acc + ccw_ref[k:k + 1, :] * gext[pl.ds(LRU_CONV_K - 1 - k, tm), :]
            dcw_ref[k:k + 1, :] += jnp.sum(
                g * xext[pl.ds(HALO_C - (LRU_CONV_K - 1) + k, tm), :], axis=0, keepdims=True)

        _acc_out(i, dcb_ref, jnp.sum(g, axis=0, keepdims=True))
        dp_ref[:, 0:wl] = acc.astype(BF16)
        dp_ref[:, wl:2 * wl] = dgate_ref[...]

    row = lambda w: pl.BlockSpec((tm, w), lambda i: (i, 0))
    return pl.pallas_call(
        body, name="odd_bwd_b", grid=(nt,),
        in_specs=[_col(tm, wl, 0), _prev_halo(tm, HALO_C, wl, 0), row(wl), _next_halo(tm, HALO_C, wl, 0, t), row(wl),
                  _full((8, wl))],
        out_specs=[row(2 * wl), _full((8, wl)), _full((1, wl))],
        out_shape=[jax.ShapeDtypeStruct((t, 2 * wl), BF16), jax.ShapeDtypeStruct((8, wl), F32),
                   jax.ShapeDtypeStruct((1, wl), F32)],
        scratch_shapes=[pltpu.VMEM((tm + HALO_C, wl), F32), pltpu.VMEM((tm + HALO_C, wl), F32)],
        compiler_params=_params("arbitrary"),
    )(p, p, dxc, dxc, dgate, ccw)


def _local_step(x, target, layer_weights, final_norm, on_grads):
    t = x.shape[0]
    tm, tx, tl, tw = min(TM_MATMUL, t), min(TM_MIXER, t), min(TM_STREAM, t), min(TM_WGRAD, t)
    h = x
    saved = []
    for layer in range(N_LAYERS):
        w = layer_weights(layer, h)
        p, hn = _norm_matmul(h, w["norm"], w["w_in"], w["dep"], tm=tl)
        if layer % 2 == 0:
            y, *acts = _even_fwd(p, w["conv_w"], w["conv_b"], w["ln_g"], w["ln_b"], w["pool_w"], w["pool_b"],
                                 w["pool_scale"], tm=tx)
        else:
            y, *acts = _odd_fwd(p, w["conv_w"], w["conv_b"], w["w_rg"], w["b_rg"], w["w_ig"], w["b_ig"], w["lam"],
                                tm=tx)
        w_out = w["w_out"](y)
        saved.append((w, w_out, h, p, hn, y, acts))
        h = _out_proj(h, y, w_out, tm=tl)
    loss, dh, d_final = _loss_head(h, final_norm, target, tm=tm)

    dep = d_final
    for layer in reversed(range(N_LAYERS)):
        w, w_out, h_in, p, hn, y, acts = saved[layer]
        sfx = "even" if layer % 2 == 0 else "odd"
        dy, dw_out = _out_proj_bwd(dh, y, w_out, dep, tm=tm)
        dep = on_grads(layer, {"w_out_" + sfx: dw_out}, dep, False)
        if layer % 2 == 0:
            u1, e, dmat = acts
            du1, dd, dgat, dlg, dlb, dpb, dps, dpw = _even_bwd_a(p, u1, e, dmat, dy, w["ln_g"], w["ln_b"], w["pool_w"],
                                                                 w["pool_scale"], dep, tm=tx)
            dp, dcw, dcb = _even_bwd_b(p, du1, dd, dgat, w["conv_w"], tm=tx)
            grads = dict(conv_a_w=dcw[:CONV_K], conv_a_b=dcb, ln_a_g=dlg, ln_a_b=dlb, pool_w=dpw, pool_b=dpb,
                         pool_scale=dps)
        else:
            xc, r, ig, hs = acts
            dxc, dgate, dwrg, dwig, dbrg, dbig, dlam = _odd_bwd_a(p, xc, r, ig, hs, dy, w["w_rg"], w["w_ig"],
                                                                  w["lam"], dep, tm=tx)
            dp, dccw, dccb = _odd_bwd_b(p, dxc, dgate, w["conv_w"], tm=tx)
            grads = dict(conv_c_w=dccw[:LRU_CONV_K], conv_c_b=dccb, w_rg=dwrg, b_rg=dbrg, w_ig=dwig, b_ig=dbig,
                         lru_lambda=dlam)
        dep = on_grads(layer, {"w_in_" + sfx: _in_proj_bwd_w(hn, dp, N_DEV, tm=tw)}, dep, False)
        dh, grads["norm_" + sfx] = _in_proj_bwd_x(dp, w["w_in"], h_in, w["norm"], dh, dep, tm=tl)
        if layer == N_LAYERS - 1:
            grads["final_norm"] = d_final
        dep = on_grads(layer, grads, dep, True)
    return loss, dh


def _slot(px, py, pc):
    return 4 * px + 2 * py + pc


def _peers(x, y, c):
    return [(1 - x if k & 4 else x, 1 - y if k & 2 else y, 1 - c if k & 1 else c) for k in range(1, N_DEV)]


def _all_gather(arrs, name):
    n = len(arrs)

    def body(*refs):
        ins, outs = refs[:n], refs[n:2 * n]
        send_sems, recv_sems, local_sems = refs[2 * n:]
        x, y, c = lax.axis_index("x"), lax.axis_index("y"), lax.axis_index("c")
        me, sibling = (x, y, c), (x, y, 1 - c)
        chips = [(1 - x, y), (x, 1 - y), (1 - x, 1 - y)]

        def copy(a, k, block, to, src=None):
            rows = outs[a].at[_slot(*block)]
            return pltpu.make_async_remote_copy(
                src_ref=rows if src is None else src, dst_ref=rows, send_sem=send_sems.at[a, k],
                recv_sem=recv_sems.at[a, k], device_id=to, device_id_type=MESH)

        mine = [pltpu.make_async_copy(ins[a], outs[a].at[_slot(*me)], local_sems.at[a]) for a in range(n)]
        for cp in mine:
            cp.start()
        first = []
        for a in range(n):
            first.append(copy(a, 0, me, sibling, src=ins[a]))
            first += [copy(a, 1 + j, me, (*chip, c), src=ins[a]) for j, chip in enumerate(chips)]
        for cp in first:
            cp.start()
        passed = []
        for j, chip in enumerate(chips):
            for a in range(n):
                copy(a, 1 + j, (*chip, c), me).wait_recv()
                fwd = copy(a, 4 + j, (*chip, c), sibling)
                fwd.start()
                passed.append(fwd)
        for a in range(n):
            copy(a, 0, sibling, me).wait_recv()
            for j, chip in enumerate(chips):
                copy(a, 4 + j, (*chip, 1 - c), me).wait_recv()
        for cp in first + passed:
            cp.wait_send()
        for cp in mine:
            cp.wait()

    return pl.pallas_call(
        body, name=name,
        in_specs=[ANY] * n, out_specs=[ANY] * n,
        out_shape=[jax.ShapeDtypeStruct((N_DEV,) + a.shape, a.dtype) for a in arrs],
        scratch_shapes=[pltpu.SemaphoreType.DMA((n, 7)), pltpu.SemaphoreType.DMA((n, 7)),
                        pltpu.SemaphoreType.DMA((n,))],
    )(*arrs)


def _exchange_copy(scatter, src_ref, land_ref, peer, send_sem, recv_sem, landing):
    return pltpu.make_async_remote_copy(
        src_ref=src_ref.at[_slot(*peer)] if scatter else src_ref, dst_ref=land_ref.at[landing],
        send_sem=send_sem, recv_sem=recv_sem, device_id=peer, device_id_type=MESH)


def _exchange_start(groups, scatter, deps, name):
    flat = [pair for g in groups for pair in g]
    n, ng = len(flat), len(groups)

    def body(*refs):
        src_refs, land_refs = refs[:n], refs[n:2 * n]
        outs = refs[2 * n + len(deps):]
        sems, token = outs[:2 * ng], outs[2 * ng + 2 * n]
        x, y, c = lax.axis_index("x"), lax.axis_index("y"), lax.axis_index("c")
        me = _slot(x, y, c)
        base = 0
        for gi, g in enumerate(groups):
            for k, peer in enumerate(_peers(x, y, c)):
                for ai in range(len(g)):
                    _exchange_copy(scatter, src_refs[base + ai], land_refs[base + ai], peer,
                                   sems[2 * gi].at[ai * N_PEERS + k], sems[2 * gi + 1].at[ai * N_PEERS + k],
                                   me).start()
            base += len(g)
        token[...] = jnp.zeros_like(token)

    operands = [pltpu.with_memory_space_constraint(a, pltpu.HBM) for a in
                [s for s, _ in flat] + [l for _, l in flat]]
    out_shape = []
    for g in groups:
        out_shape += [pltpu.SemaphoreType.DMA((len(g) * N_PEERS,))] * 2
    out_shape += [pltpu.HBM(a.shape, a.dtype) for a in operands]
    out_shape.append(jax.ShapeDtypeStruct((8, 128), F32))
    outs = pl.pallas_call(
        body, name=name, out_shape=out_shape,
        in_specs=[HBM] * (2 * n) + [ANY] * len(deps),
        out_specs=[SEM] * (2 * ng) + [HBM] * (2 * n) + [pl.BlockSpec(memory_space=pltpu.VMEM)],
        input_output_aliases={i: 2 * ng + i for i in range(2 * n)},
        compiler_params=pltpu.CompilerParams(has_side_effects=pltpu.SideEffectType.DATAFLOW_SIDE_EFFECTING),
    )(*operands, *deps)
    handles, base = [], 0
    for gi, g in enumerate(groups):
        srcs = outs[2 * ng + base:2 * ng + base + len(g)]
        lands = outs[2 * ng + n + base:2 * ng + n + base + len(g)]
        handles.append((outs[2 * gi], outs[2 * gi + 1], list(srcs), list(lands)))
        base += len(g)
    return handles, outs[-1]


def _exchange_wait(handle, scatter, after, name):
    send_sems, recv_sems, srcs, lands = handle
    n = len(srcs)

    def body(*refs):
        src_refs, land_refs = refs[:n], refs[n:2 * n]
        send_ref, recv_ref = refs[2 * n], refs[2 * n + 1]
        x, y, c = lax.axis_index("x"), lax.axis_index("y"), lax.axis_index("c")
        for k, peer in enumerate(_peers(x, y, c)):
            for a in range(n):
                cp = _exchange_copy(scatter, src_refs[a], land_refs[a], peer, send_ref.at[a * N_PEERS + k],
                                    recv_ref.at[a * N_PEERS + k],
                                    _slot(*peer))
                cp.wait_send()
                cp.wait_recv()

    outs = pl.pallas_call(
        body, name=name,
        out_shape=[pltpu.HBM(a.shape, a.dtype) for a in srcs + lands],
        in_specs=[HBM] * (2 * n) + [SEM, SEM] + [ANY] * len(after),
        out_specs=[HBM] * (2 * n),
        input_output_aliases={i: i for i in range(2 * n)},
        compiler_params=pltpu.CompilerParams(has_side_effects=pltpu.SideEffectType.DATAFLOW_SIDE_EFFECTING),
    )(*srcs, *lands, send_sems, recv_sems, *after)
    return list(outs[n:])


def _sum_parts(parts, *, tr):
    np_, r, c = parts.shape

    def body(p_ref, o_ref):
        acc = p_ref[0].astype(F32)
        for k in range(1, np_):
            acc = acc + p_ref[k].astype(F32)
        o_ref[...] = acc

    return pl.pallas_call(
        body, name="sum_parts", grid=(r // tr,),
        in_specs=[pl.BlockSpec((np_, tr, c), lambda i: (0, i, 0))],
        out_specs=pl.BlockSpec((tr, c), lambda i: (i, 0)),
        out_shape=jax.ShapeDtypeStruct((r, c), F32),
        compiler_params=_params("arbitrary"),
    )(parts)


def _adamw_math(w, g, m, v):
    c1 = 1.0 - ADAM_B1 ** ADAM_STEP
    c2 = 1.0 - ADAM_B2 ** ADAM_STEP
    nm = ADAM_B1 * m + (1.0 - ADAM_B1) * g
    nv = ADAM_B2 * v + (1.0 - ADAM_B2) * (g * g)
    delta = -ADAM_LR * ((nm / c1) / (jnp.sqrt(nv / c2) + ADAM_EPS) + ADAM_WD * w)
    return delta, nm, nv


def _row_tile(r):
    for cand in (256, 128, 64, 32, 16, 8):
        if r % cand == 0 and r > cand:
            return cand
    return r


def _adamw(items, layer0, bufs, name):
    ni = len(items)
    nl = items[0][1].shape[1]
    tiles = [_row_tile(w.shape[1]) for w, _, _, _ in items]
    steps = [w.shape[1] // tr for (w, _, _, _), tr in zip(items, tiles)]
    ns = steps[0]
    assert all(s == ns for s in steps)
    nb = 0 if bufs is None else 4 * ni

    def body(*refs):
        ins, outs = refs[:4 * ni], refs[4 * ni + nb:]
        for k in range(ni):
            w_ref, p_ref, m_ref, v_ref = ins[4 * k:4 * k + 4]
            g = p_ref[0, 0].astype(F32)
            for s in range(1, p_ref.shape[0]):
                g = g + p_ref[s, 0].astype(F32)
            delta, nm, nv = _adamw_math(w_ref[0], g, m_ref[0], v_ref[0])
            g_ref, d_ref, nm_ref, nv_ref = outs[4 * k:4 * k + 4]
            g_ref[0], d_ref[0], nm_ref[0], nv_ref[0] = g, delta, nm, nv

    in_specs, out_specs, out_shape, operands = [], [], [], []
    for (w, parts, m, v), tr in zip(items, tiles):
        blk = pl.BlockSpec((1, tr, w.shape[2]), lambda l, i: (layer0 + l, i, 0))
        in_specs += [blk, pl.BlockSpec((parts.shape[0], 1, tr, w.shape[2]), lambda l, i: (0, l, i, 0)), blk, blk]
        operands += [w, parts, m, v]
        out_specs += [blk] * 4
        out_shape += [jax.ShapeDtypeStruct(w.shape, F32)] * 4
    if bufs is not None:
        in_specs += [ANY] * nb
        operands += [b for item in bufs for b in item]
    outs = pl.pallas_call(
        body, name=name, grid=(nl, ns), in_specs=in_specs, out_specs=out_specs, out_shape=out_shape,
        input_output_aliases={4 * ni + i: i for i in range(nb)},
        compiler_params=_params("arbitrary", "arbitrary"),
    )(*operands)
    return [tuple(outs[4 * k:4 * k + 4]) for k in range(ni)]


NAMES = ("norm_even", "w_in_even", "conv_a_w", "conv_a_b", "ln_a_g", "ln_a_b", "pool_w", "pool_b", "pool_scale",
         "w_out_even", "norm_odd", "w_in_odd", "conv_c_w", "conv_c_b", "w_rg", "b_rg", "w_ig", "b_ig", "lru_lambda",
         "w_out_odd", "final_norm")
SMALL_GATHERED = ("conv_a_w", "pool_b", "norm_odd", "conv_c_w", "conv_c_b", "b_rg", "b_ig", "lru_lambda")
BIG = (("w_in_even", "w_out_even"), ("w_in_odd", "w_out_odd"))
SMALL = (("conv_a_w", "pool_b", "pool_w"), ("norm_odd", "conv_c_w", "conv_c_b", "b_rg", "b_ig", "lru_lambda"))
REPLICATED = (("norm_even", "conv_a_b", "ln_a_g", "ln_a_b", "pool_scale"), ("w_rg", "w_ig"))
PACK_ROW = 1024


def _pack_rows(flat2d):
    pad = (-flat2d.shape[1]) % PACK_ROW
    return jnp.pad(flat2d, ((0, 0), (0, pad))).reshape(flat2d.shape[0], -1, 128)


def _unpack(flat, shapes):
    out, off = [], 0
    for s in shapes:
        n = 1
        for d in s:
            n *= d
        out.append(flat[..., off:off + n].reshape(flat.shape[:-1] + tuple(s)))
        off += n
    return out


def _to_global(name, g):
    if name in ("conv_a_w", "pool_b", "conv_c_w"):
        return jnp.transpose(g, (1, 2, 0, 3)).reshape(g.shape[1], g.shape[2], -1)
    if name == "pool_w":
        return jnp.transpose(g, (1, 2, 0, 3, 4)).reshape(2, 4, POOL_GW, POOL_GW)
    return jnp.transpose(g, (1, 0, 2)).reshape(g.shape[1], -1)


def _to_blocks(name, g):
    if name == "conv_a_w":
        return jnp.transpose(g.reshape(CONV_K, N_DEV, -1), (1, 0, 2))
    if name == "conv_c_w":
        return jnp.transpose(g.reshape(LRU_CONV_K, N_DEV, -1), (1, 0, 2))
    if name == "pool_b":
        return jnp.transpose(g.reshape(4, N_DEV, -1), (1, 0, 2))
    if name == "pool_w":
        return jnp.transpose(g.reshape(4, N_DEV, POOL_GW // N_DEV, POOL_GW), (1, 0, 2, 3))
    return g.reshape(N_DEV, -1)


def _as3d(a):
    if a.ndim == 1:
        return a.reshape(1, 1, -1)
    if a.ndim == 2:
        return a.reshape(a.shape[0], 1, a.shape[1])
    return a.reshape(a.shape[0], -1, a.shape[-1])


def kernel(x, norm_even, w_in_even, conv_a_w, conv_a_b, ln_a_g, ln_a_b, pool_w, pool_b, pool_scale, w_out_even, norm_odd, w_in_odd, conv_c_w, conv_c_b, w_rg, b_rg, w_ig, b_ig, lru_lambda, w_out_odd, final_norm, loss_target, m_norm_even, m_w_in_even, m_conv_a_w, m_conv_a_b, m_ln_a_g, m_ln_a_b, m_pool_w, m_pool_b, m_pool_scale, m_w_out_even, m_norm_odd, m_w_in_odd, m_conv_c_w, m_conv_c_b, m_w_rg, m_b_rg, m_w_ig, m_b_ig, m_lru_lambda, m_w_out_odd, m_final_norm, v_norm_even, v_w_in_even, v_conv_a_w, v_conv_a_b, v_ln_a_g, v_ln_a_b, v_pool_w, v_pool_b, v_pool_scale, v_w_out_even, v_norm_odd, v_w_in_odd, v_conv_c_w, v_conv_c_b, v_w_rg, v_b_rg, v_w_ig, v_b_ig, v_lru_lambda, v_w_out_odd, v_final_norm):
    w_loc = dict(zip(NAMES, [norm_even, w_in_even, conv_a_w, conv_a_b, ln_a_g, ln_a_b, pool_w, pool_b, pool_scale,
                             w_out_even, norm_odd, w_in_odd, conv_c_w, conv_c_b, w_rg, b_rg, w_ig, b_ig, lru_lambda,
                             w_out_odd, final_norm]))
    m_loc = dict(zip(NAMES, [m_norm_even, m_w_in_even, m_conv_a_w, m_conv_a_b, m_ln_a_g, m_ln_a_b, m_pool_w, m_pool_b,
                             m_pool_scale, m_w_out_even, m_norm_odd, m_w_in_odd, m_conv_c_w, m_conv_c_b, m_w_rg,
                             m_b_rg, m_w_ig, m_b_ig, m_lru_lambda, m_w_out_odd, m_final_norm]))
    v_loc = dict(zip(NAMES, [v_norm_even, v_w_in_even, v_conv_a_w, v_conv_a_b, v_ln_a_g, v_ln_a_b, v_pool_w, v_pool_b,
                             v_pool_scale, v_w_out_even, v_norm_odd, v_w_in_odd, v_conv_c_w, v_conv_c_b, v_w_rg,
                             v_b_rg, v_w_ig, v_b_ig, v_lru_lambda, v_w_out_odd, v_final_norm]))
    me = _slot(lax.axis_index("x"), lax.axis_index("y"), lax.axis_index("c"))

    def landing(own):
        zone = lax.empty((N_DEV,) + own.shape[1:], own.dtype)
        return lax.dynamic_update_slice(zone, own, (me,) + (0,) * (own.ndim - 1))

    small_shapes = [w_loc[n].shape for n in SMALL_GATHERED]
    small = jnp.concatenate([w_loc[n].reshape(1, -1) for n in SMALL_GATHERED], axis=1)
    first = _all_gather([w_in_even[0].astype(BF16), pool_w.astype(BF16), _pack_rows(small)[0]], "gather_first")
    g_small = dict(zip(SMALL_GATHERED, [_to_global(n, g) for n, g in
                                        zip(SMALL_GATHERED, _unpack(first[2].reshape(N_DEV, -1), small_shapes))]))
    pool_w_all = _to_global("pool_w", first[1])

    def gather_start(shards, deps, name):
        groups = [[(s.astype(BF16), landing(s.astype(BF16)[None])) for s in g] for g in shards]
        return _exchange_start(groups, False, deps, name)

    gathers = {}
    (gathers["w_out_0"], gathers[1]), token_1 = gather_start([[w_out_even[0]], [w_in_odd[0], w_out_odd[0]]],
                                                             [first[0]], "gather_start_1")
    next_shards = {1: [w_in_even[1], w_out_even[1]], 2: [w_in_odd[1], w_out_odd[1]]}

    def layer_weights(layer, h):
        j = layer // 2
        dep = h
        if layer == 0:
            w_in, dep = first[0], token_1
            w_out = lambda y: _exchange_wait(gathers["w_out_0"], False, [y], "gather_wait_out_0")[0]
        else:
            w_in, w_out_now = _exchange_wait(gathers[layer], False, [h], f"gather_wait_{layer}")
            w_out = lambda y: w_out_now
            if layer in next_shards:
                (gathers[layer + 1],), dep = gather_start([next_shards[layer]], [w_in], f"gather_start_{layer + 1}")
        row = lambda a: a[j][None]
        if layer % 2 == 0:
            return dict(dep=dep, norm=row(norm_even), w_in=w_in, w_out=lambda y: w_out(y).reshape(W_EVEN_MIX, D_MODEL),
                        conv_w=jnp.pad(g_small["conv_a_w"][j], ((0, 1), (0, 0))), conv_b=row(conv_a_b),
                        ln_g=row(ln_a_g), ln_b=row(ln_a_b), pool_w=pool_w_all[j],
                        pool_b=g_small["pool_b"][j].reshape(1, W_POOL), pool_scale=row(pool_scale))
        return dict(dep=dep, norm=row(g_small["norm_odd"]), w_in=w_in, w_out=lambda y: w_out(y).reshape(W_LRU, D_MODEL),
                    conv_w=jnp.pad(g_small["conv_c_w"][j], ((0, 4), (0, 0))), conv_b=row(g_small["conv_c_b"]),
                    w_rg=w_rg[j].astype(BF16), b_rg=row(g_small["b_rg"]), w_ig=w_ig[j].astype(BF16),
                    b_ig=row(g_small["b_ig"]), lam=row(g_small["lru_lambda"]))

    pending, scatters, small_layout = {}, {}, {}
    last_token = []

    def on_grads(layer, grads, dep, last):
        par = layer % 2
        w_in_name, w_out_name = BIG[par]
        pending.setdefault(layer, {}).update(grads)
        have = pending[layer]
        send = {}
        if w_in_name in have and (layer == 0 or last):
            send["in"] = have.pop(w_in_name)
        if w_out_name in have and (layer == 0 or last):
            send["out"] = have.pop(w_out_name).reshape(N_DEV, -1, D_MODEL)
        if last:
            small_blocks = [_to_blocks(n, have[n]) for n in SMALL[par]]
            rep_names = list(REPLICATED[par]) + (["final_norm"] if "final_norm" in have else [])
            rep = jnp.concatenate([have[n].reshape(-1) for n in rep_names]).reshape(N_DEV, -1)
            small_layout[layer] = ([b.shape[1:] for b in small_blocks], rep.shape[1])
            send["small"] = _pack_rows(jnp.concatenate([b.reshape(N_DEV, -1) for b in small_blocks] + [rep], axis=1))
        if not send:
            return dep
        pairs = [(s, landing(lax.dynamic_slice_in_dim(s, me, 1, 0))) for s in send.values()]
        (handle,), token = _exchange_start([pairs], True, [], f"scatter_start_{layer}_{'_'.join(send)}")
        scatters.setdefault(layer, []).append((list(send), handle))
        last_token[:] = [token]
        return token

    loss, grad_x = _local_step(x[0], loss_target[0], layer_weights, final_norm[None], on_grads)

    w3 = {n: _as3d(w_loc[n]) for n in NAMES}
    m3 = {n: _as3d(m_loc[n]) for n in NAMES}
    v3 = {n: _as3d(v_loc[n]) for n in NAMES}
    results, rep_parts = {}, {}
    after = list(last_token)
    for layer in (3, 2, 1, 0):
        par, j = layer % 2, layer // 2
        got = {}
        for keys, handle in scatters[layer]:
            got.update(zip(keys, _exchange_wait(handle, True, after, f"scatter_wait_{layer}_{'_'.join(keys)}")))
        got_in, got_out, got_small = got["in"], got["out"], got["small"]
        shapes, rep_len = small_layout[layer]
        *small_parts, rep_parts[layer] = _unpack(got_small.reshape(N_DEV, -1), list(shapes) + [(rep_len,)])
        names = list(BIG[par]) + list(SMALL[par])
        parts = [got_in, got_out] + small_parts
        big_items, small_items = [], []
        for n, pt in zip(names, parts):
            item = (w3[n], pt.reshape((N_DEV, 1) + w3[n].shape[1:]), m3[n], v3[n])
            (big_items if w3[n].shape[1] >= 128 else small_items).append((n, item))
        for n, item in big_items:
            results[n] = _adamw([item], j, [results[n]] if n in results else None, f"adamw_{n}_{layer}")[0]
        snames = [n for n, _ in small_items]
        prev = [results[n] for n in snames] if snames[0] in results else None
        for n, r in zip(snames, _adamw([it for _, it in small_items], j, prev, f"adamw_small_{layer}")):
            results[n] = r
        after = [results[names[0]][1]]

    rep_pack = _pack_rows(jnp.concatenate([rep_parts[l] for l in range(N_LAYERS)], axis=1))
    rep_mine = _sum_parts(rep_pack, tr=rep_pack.shape[1])
    rep_all = _all_gather([rep_mine], "gather_replicated")[0].reshape(N_DEV, -1)
    rep_grads, off = {}, 0
    for layer in range(N_LAYERS):
        rep_len = small_layout[layer][1]
        flat = rep_all[:, off:off + rep_len].reshape(-1)
        off += rep_len
        rep_names = list(REPLICATED[layer % 2]) + (["final_norm"] if layer == N_LAYERS - 1 else [])
        for n, g in zip(rep_names, _unpack(flat, [w_loc[n].shape[1:] if n != "final_norm" else w_loc[n].shape
                                                  for n in rep_names])):
            rep_grads.setdefault(n, []).append(g)
    rep_items = {n: (w3[n], _as3d(jnp.stack(rep_grads[n]) if n != "final_norm" else rep_grads[n][0])[None],
                     m3[n], v3[n]) for par in range(2) for n in REPLICATED[par]}
    rep_items["final_norm"] = (w3["final_norm"], _as3d(rep_grads["final_norm"][0])[None], m3["final_norm"],
                               v3["final_norm"])
    for n in ("w_rg", "w_ig"):
        results[n] = _adamw([rep_items[n]], 0, None, f"adamw_{n}")[0]
    vec_names = list(REPLICATED[0])
    for n, r in zip(vec_names, _adamw([rep_items[n] for n in vec_names], 0, None, "adamw_replicated")):
        results[n] = r
    results["final_norm"] = _adamw([rep_items["final_norm"]], 0, None, "adamw_final_norm")[0]

    total = lax.psum(loss[0, 0], ("x", "y", "c"))
    outs = [[results[n][k].reshape(w_loc[n].shape) for n in NAMES] for k in range(4)]
    return (total, grad_x[None], *outs[0], *outs[1], *outs[2], *outs[3])
```

```python
import functools

import jax
import jax.numpy as jnp
from jax import lax
from jax.experimental import pallas as pl
from jax.experimental.pallas import tpu as pltpu

F32 = jnp.float32
BF16 = jnp.bfloat16

N_DEV = 8
N_PEERS = N_DEV - 1
N_LAYERS = 4
D_MODEL = 1024
EPS_RMS = 1e-6
EPS_LN = 1e-5
W_CONV = 1024
CONV_K = 31
W_POOL = 1024
POOL_WINDOWS = (2, 4, 8, 16)
POOL_GW = 256
W_EVEN_IN = 5120
W_EVEN_MIX = 2048
LRU_HEADS = 12
LRU_HD = 128
W_LRU = 1536
LRU_CONV_K = 4
LRU_C = 8.0
ADAM_LR = 0.001
ADAM_B1 = 0.9
ADAM_B2 = 0.999
ADAM_EPS = 1e-08
ADAM_WD = 0.01
ADAM_STEP = 10

HALO = 32
HALO_C = 8
TM_MATMUL = 512
TM_STREAM = 1024
TM_WGRAD = 2048
TM_MIXER = 256
CONV_ROWS = 128
VMEM_LIMIT = 56 * 1024 * 1024
MESH = pl.DeviceIdType.MESH
ANY = pl.BlockSpec(memory_space=pl.ANY)
HBM = pl.BlockSpec(memory_space=pltpu.HBM)
SEM = pl.BlockSpec(memory_space=pltpu.SEMAPHORE)


def _params(*sem):
    return pltpu.CompilerParams(dimension_semantics=sem, vmem_limit_bytes=VMEM_LIMIT)


def _sigmoid(z):
    return jax.nn.sigmoid(z)


def _dsilu(z, s):
    return s * (1.0 + z * (1.0 - s))


def _full(shape):
    nd = len(shape)
    return pl.BlockSpec(shape, lambda *_: (0,) * nd)


def _norm_matmul(h, g, w, dep, *, tm):
    t, d = h.shape
    nd, _, nb = w.shape

    def body(h_ref, g_ref, w_ref, dep_ref, p_ref, hn_ref):
        @pl.when(pl.program_id(1) == 0)
        def _():
            x = h_ref[...]
            r = lax.rsqrt(jnp.mean(x * x, axis=-1, keepdims=True) + EPS_RMS)
            hn_ref[...] = ((x * r) * g_ref[...]).astype(BF16)

        p_ref[...] = jnp.dot(hn_ref[...], w_ref[0], preferred_element_type=F32)

    return pl.pallas_call(
        body, name="norm_matmul", grid=(t // tm, nd),
        in_specs=[pl.BlockSpec((tm, d), lambda i, j: (i, 0)), _full((1, d)),
                  pl.BlockSpec((1, d, nb), lambda i, j: (j, 0, 0)), ANY],
        out_specs=[pl.BlockSpec((tm, nb), lambda i, j: (i, j)), pl.BlockSpec((tm, d), lambda i, j: (i, 0))],
        out_shape=[jax.ShapeDtypeStruct((t, nd * nb), F32), jax.ShapeDtypeStruct((t, d), BF16)],
        compiler_params=_params("arbitrary", "arbitrary"),
    )(h, g, w, dep)


def _out_proj(h, y, w, dep, *, tm):
    t, d = h.shape
    k = y.shape[1]

    def body(h_ref, y_ref, w_ref, dep_ref, o_ref):
        o_ref[...] = h_ref[...] + jnp.dot(y_ref[...], w_ref[...], preferred_element_type=F32)

    return pl.pallas_call(
        body, name="out_proj", grid=(t // tm,),
        in_specs=[pl.BlockSpec((tm, d), lambda i: (i, 0)), pl.BlockSpec((tm, k), lambda i: (i, 0)), _full((k, d)),
                  ANY],
        out_specs=pl.BlockSpec((tm, d), lambda i: (i, 0)),
        out_shape=jax.ShapeDtypeStruct((t, d), F32),
        compiler_params=_params("arbitrary"),
    )(h, y, w, dep)


def _out_proj_bwd(dh, y, w, dep, *, tm):
    t, d = dh.shape
    k = y.shape[1]
    nt = t // tm

    def body(dh_ref, y_ref, w_ref, dep_ref, dy_ref, dw_ref, acc):
        i = pl.program_id(0)
        g = dh_ref[...].astype(BF16)
        dy_ref[...] = lax.dot_general(g, w_ref[...], (((1,), (1,)), ((), ())), preferred_element_type=F32)
        part = lax.dot_general(y_ref[...], g, (((0,), (0,)), ((), ())), preferred_element_type=F32)

        @pl.when(i == 0)
        def _():
            acc[...] = part

        @pl.when(i > 0)
        def _():
            acc[...] += part

        @pl.when(i == nt - 1)
        def _():
            dw_ref[...] = acc[...].astype(BF16)

    return pl.pallas_call(
        body, name="out_proj_bwd", grid=(nt,),
        in_specs=[pl.BlockSpec((tm, d), lambda i: (i, 0)), pl.BlockSpec((tm, k), lambda i: (i, 0)), _full((k, d)),
                  ANY],
        out_specs=[pl.BlockSpec((tm, k), lambda i: (i, 0)), _full((k, d))],
        out_shape=[jax.ShapeDtypeStruct((t, k), F32), jax.ShapeDtypeStruct((k, d), BF16)],
        scratch_shapes=[pltpu.VMEM((k, d), F32)],
        compiler_params=_params("arbitrary"),
    )(dh, y, w, dep)


def _in_proj_bwd_x(dp, w, h, g, dh_out, dep, *, tm):
    t, d = h.shape
    nd, _, nb = w.shape
    nt = t // tm

    def body(dp_ref, w_ref, h_ref, g_ref, dho_ref, dep_ref, dh_ref, dg_ref, acc):
        i, j = pl.program_id(0), pl.program_id(1)
        part = lax.dot_general(dp_ref[...], w_ref[0], (((1,), (1,)), ((), ())), preferred_element_type=F32)

        @pl.when(j == 0)
        def _():
            acc[...] = part

        @pl.when(j > 0)
        def _():
            acc[...] += part

        @pl.when(j == nd - 1)
        def _():
            x = h_ref[...]
            r = lax.rsqrt(jnp.mean(x * x, axis=-1, keepdims=True) + EPS_RMS)
            dy = acc[...]
            gd = dy * g_ref[...]
            m = jnp.mean(gd * x, axis=-1, keepdims=True)
            dh_ref[...] = dho_ref[...] + r * gd - x * (r * r * r * m)
            dgp = jnp.sum(dy * x * r, axis=0, keepdims=True)

            @pl.when(i == 0)
            def _():
                dg_ref[...] = dgp

            @pl.when(i > 0)
            def _():
                dg_ref[...] += dgp

    return pl.pallas_call(
        body, name="in_proj_bwd_x", grid=(nt, nd),
        in_specs=[pl.BlockSpec((tm, nb), lambda i, j: (i, j)), pl.BlockSpec((1, d, nb), lambda i, j: (j, 0, 0)),
                  pl.BlockSpec((tm, d), lambda i, j: (i, 0)), _full((1, d)), pl.BlockSpec((tm, d), lambda i, j: (i, 0)),
                  ANY],
        out_specs=[pl.BlockSpec((tm, d), lambda i, j: (i, 0)), _full((1, d))],
        out_shape=[jax.ShapeDtypeStruct((t, d), F32), jax.ShapeDtypeStruct((1, d), F32)],
        scratch_shapes=[pltpu.VMEM((tm, d), F32)],
        compiler_params=_params("arbitrary", "arbitrary"),
    )(dp, w, h, g, dh_out, dep)


def _in_proj_bwd_w(hn, dp, nd, *, tm):
    t, d = hn.shape
    nb = dp.shape[1] // nd
    nt = t // tm

    def body(hn_ref, dp_ref, dw_ref, acc):
        i = pl.program_id(1)
        part = lax.dot_general(hn_ref[...], dp_ref[...], (((0,), (0,)), ((), ())), preferred_element_type=F32)

        @pl.when(i == 0)
        def _():
            acc[...] = part

        @pl.when(i > 0)
        def _():
            acc[...] += part

        @pl.when(i == nt - 1)
        def _():
            dw_ref[0] = acc[...].astype(BF16)

    return pl.pallas_call(
        body, name="in_proj_bwd_w", grid=(nd, nt),
        in_specs=[pl.BlockSpec((tm, d), lambda j, i: (i, 0)), pl.BlockSpec((tm, nb), lambda j, i: (i, j))],
        out_specs=pl.BlockSpec((1, d, nb), lambda j, i: (j, 0, 0)),
        out_shape=jax.ShapeDtypeStruct((nd, d, nb), BF16),
        scratch_shapes=[pltpu.VMEM((d, nb), F32)],
        compiler_params=_params("arbitrary", "arbitrary"),
    )(hn, dp)


def _loss_head(h, g, target, *, tm):
    t, d = h.shape
    nt = t // tm

    def body(h_ref, g_ref, t_ref, loss_ref, dh_ref, dg_ref):
        i = pl.program_id(0)
        x = h_ref[...]
        r = lax.rsqrt(jnp.mean(x * x, axis=-1, keepdims=True) + EPS_RMS)
        xr = x * r
        err = xr * g_ref[...] - t_ref[...]
        lp = 0.5 * jnp.sum(jnp.mean(err * err, axis=-1, keepdims=True), axis=0, keepdims=True)
        dy = err * (1.0 / d)
        gd = dy * g_ref[...]
        m = jnp.mean(gd * x, axis=-1, keepdims=True)
        dh_ref[...] = r * gd - x * (r * r * r * m)
        dgp = jnp.sum(dy * xr, axis=0, keepdims=True)

        @pl.when(i == 0)
        def _():
            loss_ref[...] = lp
            dg_ref[...] = dgp

        @pl.when(i > 0)
        def _():
            loss_ref[...] += lp
            dg_ref[...] += dgp

    return pl.pallas_call(
        body, name="loss_head", grid=(nt,),
        in_specs=[pl.BlockSpec((tm, d), lambda i: (i, 0)), _full((1, d)), pl.BlockSpec((tm, d), lambda i: (i, 0))],
        out_specs=[_full((1, 1)), pl.BlockSpec((tm, d), lambda i: (i, 0)), _full((1, d))],
        out_shape=[jax.ShapeDtypeStruct((1, 1), F32), jax.ShapeDtypeStruct((t, d), F32),
                   jax.ShapeDtypeStruct((1, d), F32)],
        compiler_params=_params("arbitrary"),
    )(h, g, target)


def _col(tm, w, c):
    return pl.BlockSpec((tm, w), lambda i: (i, c))


def _prev_halo(tm, rows, w, c):
    per = tm // rows
    return pl.BlockSpec((rows, w), lambda i: (jnp.maximum(i * per - 1, 0), c))


def _next_halo(tm, rows, w, c, t):
    per = tm // rows
    last = t // rows - 1
    return pl.BlockSpec((rows, w), lambda i: (jnp.minimum((i + 1) * per, last), c))


def _inv_count(first_row, rows, window):
    tpos = first_row + lax.broadcasted_iota(jnp.int32, (rows, 1), 0)
    return 1.0 / jnp.minimum(tpos + 1, window).astype(F32)


def _even_fwd(p, cw, cb, lg, lb, pw, pb, ps, *, tm):
    t = p.shape[0]
    wc = W_CONV

    def body(av, ag, agate, bv, bgate, avh, agh, bvh, cw_ref, cb_ref, lg_ref, lb_ref, pw_ref, pb_ref, ps_ref,
             y_ref, u1_ref, e_ref, d_ref, uext, vext):
        i = pl.program_id(0)
        keep = (i > 0).astype(F32)
        uext[0:HALO, :] = keep * (avh[...] * _sigmoid(agh[...]))
        uext[HALO:, :] = av[...] * _sigmoid(ag[...])
        vext[0:HALO, :] = keep * bvh[...]
        vext[HALO:, :] = bv[...]
        for c in range(0, wc, 128):
            for rb in range(0, tm, CONV_ROWS):
                acc = jnp.broadcast_to(cb_ref[:, c:c + 128], (CONV_ROWS, 128))
                for r in range(8):
                    shifted = uext[pl.ds(rb + 8 - r, CONV_ROWS + HALO - 8), c:c + 128]
                    for q in range(HALO // 8):
                        s = 8 * q + r
                        if s < CONV_K:
                            acc = acc + cw_ref[CONV_K - 1 - s:CONV_K - s, c:c + 128] * shifted[24 - 8 * q:24 - 8 * q + CONV_ROWS]
                u1_ref[rb:rb + CONV_ROWS, c:c + 128] = acc
        u1 = u1_ref[...]
        mu = jnp.mean(u1, axis=-1, keepdims=True)
        xc = u1 - mu
        rs = lax.rsqrt(jnp.mean(xc * xc, axis=-1, keepdims=True) + EPS_LN)
        u2 = (xc * rs) * lg_ref[...] + lb_ref[...]
        u3 = u2 * _sigmoid(u2)
        ga = agate[...]
        y_ref[:, 0:wc] = (u3 * (ga * _sigmoid(ga))).astype(BF16)
        for g, win in enumerate(POOL_WINDOWS):
            cs = slice(g * POOL_GW, (g + 1) * POOL_GW)
            s = vext[pl.ds(HALO, tm), cs]
            for j in range(1, win):
                s = s + vext[pl.ds(HALO - j, tm), cs]
            dg = s * _inv_count(i * tm, tm, win) - vext[pl.ds(HALO, tm), cs]
            dgb = dg.astype(BF16)
            d_ref[:, cs] = dgb
            eg = jnp.dot(dgb, pw_ref[g], preferred_element_type=F32) + pb_ref[:, cs]
            e_ref[:, cs] = eg
            gb = bgate[:, cs]
            y_ref[:, wc + g * POOL_GW:wc + (g + 1) * POOL_GW] = ((eg * ps_ref[:, cs]) * (gb * _sigmoid(gb))).astype(BF16)

    row = lambda w: pl.BlockSpec((tm, w), lambda i: (i, 0))
    return pl.pallas_call(
        body, name="even_fwd", grid=(t // tm,),
        in_specs=[_col(tm, wc, 0), _col(tm, wc, 1), _col(tm, wc, 2), _col(tm, wc, 3), _col(tm, wc, 4),
                  _prev_halo(tm, HALO, wc, 0), _prev_halo(tm, HALO, wc, 1), _prev_halo(tm, HALO, wc, 3),
                  _full((32, wc)), _full((1, wc)), _full((1, wc)), _full((1, wc)),
                  _full((4, POOL_GW, POOL_GW)), _full((1, wc)), _full((1, wc))],
        out_specs=[row(2 * wc), row(wc), row(wc), row(wc)],
        out_shape=[jax.ShapeDtypeStruct((t, 2 * wc), BF16), jax.ShapeDtypeStruct((t, wc), F32),
                   jax.ShapeDtypeStruct((t, wc), F32), jax.ShapeDtypeStruct((t, wc), BF16)],
        scratch_shapes=[pltpu.VMEM((tm + HALO, wc), F32), pltpu.VMEM((tm + HALO, wc), F32)],
        compiler_params=_params("arbitrary"),
    )(p, p, p, p, p, p, p, p, cw, cb, lg, lb, pw, pb, ps)


def _acc_out(i, ref, val):
    @pl.when(i == 0)
    def _():
        ref[...] = val

    @pl.when(i > 0)
    def _():
        ref[...] += val


def _even_bwd_a(p, u1, e, dmat, dy, lg, lb, pw, ps, dep, *, tm):
    t = p.shape[0]
    wc = W_CONV

    def body(agate, bgate, u1_ref, e_ref, d_ref, dya, dyb, lg_ref, lb_ref, pw_ref, ps_ref, dep_ref,
             du1_ref, dd_ref, dgat_ref, dlg_ref, dlb_ref, dpb_ref, dps_ref, dpw_ref):
        i = pl.program_id(0)

        @pl.when(i == 0)
        def _():
            dpw_ref[...] = jnp.zeros_like(dpw_ref)

        u1 = u1_ref[...]
        mu = jnp.mean(u1, axis=-1, keepdims=True)
        xc = u1 - mu
        rs = lax.rsqrt(jnp.mean(xc * xc, axis=-1, keepdims=True) + EPS_LN)
        xh = xc * rs
        u2 = xh * lg_ref[...] + lb_ref[...]
        s2 = _sigmoid(u2)
        ga = agate[...]
        sa = _sigmoid(ga)
        dy_a = dya[...]
        dgat_ref[:, 0:wc] = (dy_a * (u2 * s2) * _dsilu(ga, sa)).astype(BF16)
        du2 = dy_a * (ga * sa) * _dsilu(u2, s2)
        _acc_out(i, dlg_ref, jnp.sum(du2 * xh, axis=0, keepdims=True))
        _acc_out(i, dlb_ref, jnp.sum(du2, axis=0, keepdims=True))
        dxh = du2 * lg_ref[...]
        m1 = jnp.mean(dxh, axis=-1, keepdims=True)
        m2 = jnp.mean(dxh * xh, axis=-1, keepdims=True)
        du1_ref[...] = rs * (dxh - m1 - xh * m2)

        gb = bgate[...]
        sb = _sigmoid(gb)
        ev = e_ref[...]
        dy_b = dyb[...]
        dgat_ref[:, wc:2 * wc] = (dy_b * (ev * ps_ref[...]) * _dsilu(gb, sb)).astype(BF16)
        dz = dy_b * (gb * sb)
        _acc_out(i, dps_ref, jnp.sum(dz * ev, axis=0, keepdims=True))
        de = dz * ps_ref[...]
        _acc_out(i, dpb_ref, jnp.sum(de, axis=0, keepdims=True))
        for g in range(len(POOL_WINDOWS)):
            cs = slice(g * POOL_GW, (g + 1) * POOL_GW)
            deg = de[:, cs].astype(BF16)
            dd_ref[:, cs] = lax.dot_general(deg, pw_ref[g], (((1,), (1,)), ((), ())), preferred_element_type=F32)
            dpw_ref[g] += lax.dot_general(d_ref[:, cs], deg, (((0,), (0,)), ((), ())), preferred_element_type=F32)

    row = lambda w: pl.BlockSpec((tm, w), lambda i: (i, 0))
    return pl.pallas_call(
        body, name="even_bwd_a", grid=(t // tm,),
        in_specs=[_col(tm, wc, 2), _col(tm, wc, 4), row(wc), row(wc), row(wc), _col(tm, wc, 0), _col(tm, wc, 1),
                  _full((1, wc)), _full((1, wc)), _full((4, POOL_GW, POOL_GW)), _full((1, wc)), ANY],
        out_specs=[row(wc), row(wc), row(2 * wc), _full((1, wc)), _full((1, wc)), _full((1, wc)), _full((1, wc)),
                   _full((4, POOL_GW, POOL_GW))],
        out_shape=[jax.ShapeDtypeStruct((t, wc), F32), jax.ShapeDtypeStruct((t, wc), F32),
                   jax.ShapeDtypeStruct((t, 2 * wc), BF16)] + [jax.ShapeDtypeStruct((1, wc), F32)] * 4
                  + [jax.ShapeDtypeStruct((4, POOL_GW, POOL_GW), F32)],
        compiler_params=_params("arbitrary"),
    )(p, p, u1, e, dmat, dy, dy, lg, lb, pw, ps, dep)


def _even_bwd_b(p, du1, dd, dgat, cw, *, tm):
    t = p.shape[0]
    wc = W_CONV
    nt = t // tm

    def body(av, ag, avh, agh, du1_ref, du1n, dd_ref, ddn, dgat_ref, cw_ref, dp_ref, dcw_ref, dcb_ref,
             uext, gext, dext, du0, dcw8):
        i = pl.program_id(0)
        keep_p = (i > 0).astype(F32)
        keep_n = (i < nt - 1).astype(F32)

        @pl.when(i == 0)
        def _():
            dcw8[...] = jnp.zeros_like(dcw8)

        a = av[...]
        sg = _sigmoid(ag[...])
        uext[0:HALO, :] = keep_p * (avh[...] * _sigmoid(agh[...]))
        uext[HALO:, :] = a * sg
        gext[0:tm, :] = du1_ref[...]
        gext[tm:, :] = keep_n * du1n[...]
        for c in range(0, wc, 128):
            for rb in range(0, tm, CONV_ROWS):
                acc = jnp.zeros((CONV_ROWS, 128), F32)
                for r in range(8):
                    ahead = gext[pl.ds(rb + r, CONV_ROWS + HALO - 8), c:c + 128]
                    for q in range(HALO // 8):
                        s = 8 * q + r
                        if s < CONV_K:
                            acc = acc + cw_ref[CONV_K - 1 - s:CONV_K - s, c:c + 128] * ahead[8 * q:8 * q + CONV_ROWS]
                du0[rb:rb + CONV_ROWS, c:c + 128] = acc
                gcur = du1_ref[rb:rb + CONV_ROWS, c:c + 128]
                for r in range(8):
                    behind = uext[pl.ds(rb + 8 - r, CONV_ROWS + HALO - 8), c:c + 128]
                    for q in range(HALO // 8):
                        s = 8 * q + r
                        if s < CONV_K:
                            prod = gcur * behind[24 - 8 * q:24 - 8 * q + CONV_ROWS]
                            part = prod[0:8]
                            for o in range(8, CONV_ROWS, 8):
                                part = part + prod[o:o + 8]
                            k = CONV_K - 1 - s
                            dcw8[8 * k:8 * k + 8, c:c + 128] += part

        @pl.when(i == nt - 1)
        def _():
            for k in range(CONV_K):
                dcw_ref[k:k + 1, :] = jnp.sum(dcw8[8 * k:8 * k + 8, :], axis=0, keepdims=True)
            dcw_ref[CONV_K:32, :] = jnp.zeros((32 - CONV_K, wc), F32)

        _acc_out(i, dcb_ref, jnp.sum(du1_ref[...], axis=0, keepdims=True))
        g0 = du0[...]
        dp_ref[:, 0:wc] = (g0 * sg).astype(BF16)
        dp_ref[:, wc:2 * wc] = (g0 * a * sg * (1.0 - sg)).astype(BF16)
        dp_ref[:, 2 * wc:3 * wc] = dgat_ref[:, 0:wc]
        dp_ref[:, 4 * wc:5 * wc] = dgat_ref[:, wc:2 * wc]
        for g, win in enumerate(POOL_WINDOWS):
            cs = slice(g * POOL_GW, (g + 1) * POOL_GW)
            dext[0:tm, cs] = dd_ref[:, cs] * _inv_count(i * tm, tm, win)
            dext[tm:, cs] = keep_n * (ddn[:, cs] * _inv_count((i + 1) * tm, HALO, win))
            s = dext[pl.ds(0, tm), cs]
            for j in range(1, win):
                s = s + dext[pl.ds(j, tm), cs]
            dp_ref[:, 3 * wc + g * POOL_GW:3 * wc + (g + 1) * POOL_GW] = (s - dd_ref[:, cs]).astype(BF16)

    row = lambda w: pl.BlockSpec((tm, w), lambda i: (i, 0))
    return pl.pallas_call(
        body, name="even_bwd_b", grid=(nt,),
        in_specs=[_col(tm, wc, 0), _col(tm, wc, 1), _prev_halo(tm, HALO, wc, 0), _prev_halo(tm, HALO, wc, 1),
                  row(wc), _next_halo(tm, HALO, wc, 0, t), row(wc), _next_halo(tm, HALO, wc, 0, t), row(2 * wc),
                  _full((32, wc))],
        out_specs=[row(5 * wc), _full((32, wc)), _full((1, wc))],
        out_shape=[jax.ShapeDtypeStruct((t, 5 * wc), BF16), jax.ShapeDtypeStruct((32, wc), F32),
                   jax.ShapeDtypeStruct((1, wc), F32)],
        scratch_shapes=[pltpu.VMEM((tm + HALO, wc), F32), pltpu.VMEM((tm + HALO, wc), F32),
                        pltpu.VMEM((tm + HALO, wc), F32), pltpu.VMEM((tm, wc), F32), pltpu.VMEM((8 * 32, wc), F32)],
        compiler_params=_params("arbitrary"),
    )(p, p, p, p, du1, du1, dd, dd, dgat, cw)


def _softplus_neg(lam):
    z = -lam
    return jnp.maximum(z, 0.0) + jnp.log1p(jnp.exp(-jnp.abs(z)))


def _one_minus_exp(x):
    series = -x * (1.0 + x * (0.5 + x * (1.0 / 6.0 + x * (1.0 / 24.0))))
    return jnp.where(x > -0.02, series, 1.0 - jnp.exp(x))


def _odd_fwd(p, ccw, ccb, wrg, brg, wig, big, lam, *, tm):
    t = p.shape[0]
    wl = W_LRU
    ng = tm // 8

    def body(xr, gate, xrh, ccw_ref, ccb_ref, wrg_ref, brg_ref, wig_ref, big_ref, lam_ref,
             y_ref, xc_ref, r_ref, i_ref, hs_ref, xext, a_s, b_s, carry):
        i = pl.program_id(0)
        keep = (i > 0).astype(F32)
        xext[0:HALO_C, :] = keep * xrh[...]
        xext[HALO_C:, :] = xr[...]
        xc = jnp.broadcast_to(ccb_ref[...], (tm, wl))
        for k in range(LRU_CONV_K):
            xc = xc + ccw_ref[k:k + 1, :] * xext[pl.ds(HALO_C - (LRU_CONV_K - 1) + k, tm), :]
        xc_ref[...] = xc
        for h in range(LRU_HEADS):
            cs = slice(h * LRU_HD, (h + 1) * LRU_HD)
            xh = xc_ref[:, cs].astype(BF16)
            r_ref[:, cs] = _sigmoid(jnp.dot(xh, wrg_ref[h], preferred_element_type=F32) + brg_ref[:, cs])
            i_ref[:, cs] = _sigmoid(jnp.dot(xh, wig_ref[h], preferred_element_type=F32) + big_ref[:, cs])
        log_a = (-LRU_C * _softplus_neg(lam_ref[...])) * r_ref[...]
        a_s[...] = jnp.exp(log_a)
        b_s[...] = jnp.sqrt(_one_minus_exp(2.0 * log_a)) * (i_ref[...] * xc_ref[...])

        @pl.when(i == 0)
        def _():
            carry[...] = jnp.zeros_like(carry)

        rowi = lax.broadcasted_iota(jnp.int32, (8, wl), 0)

        def step(g, c):
            sl = pl.ds(pl.multiple_of(g * 8, 8), 8)
            aa, bb = a_s[sl, :], b_s[sl, :]
            for s in (1, 2, 4):
                m = rowi >= s
                a_sh = jnp.where(m, pltpu.roll(aa, s, 0), 1.0)
                b_sh = jnp.where(m, pltpu.roll(bb, s, 0), 0.0)
                bb = aa * b_sh + bb
                aa = aa * a_sh
            hv = bb + aa * c
            hs_ref[sl, :] = hv
            return hv[7:8, :]

        carry[...] = lax.fori_loop(0, ng, step, carry[...])
        gt = gate[...]
        y_ref[...] = (hs_ref[...] * (gt * _sigmoid(gt))).astype(BF16)

    row = lambda w: pl.BlockSpec((tm, w), lambda i: (i, 0))
    return pl.pallas_call(
        body, name="odd_fwd", grid=(t // tm,),
        in_specs=[_col(tm, wl, 0), _col(tm, wl, 1), _prev_halo(tm, HALO_C, wl, 0), _full((8, wl)), _full((1, wl)),
                  _full((LRU_HEADS, LRU_HD, LRU_HD)), _full((1, wl)), _full((LRU_HEADS, LRU_HD, LRU_HD)),
                  _full((1, wl)), _full((1, wl))],
        out_specs=[row(wl)] * 5,
        out_shape=[jax.ShapeDtypeStruct((t, wl), BF16)] + [jax.ShapeDtypeStruct((t, wl), F32)] * 4,
        scratch_shapes=[pltpu.VMEM((tm + HALO_C, wl), F32), pltpu.VMEM((tm, wl), F32), pltpu.VMEM((tm, wl), F32),
                        pltpu.VMEM((1, wl), F32)],
        compiler_params=_params("arbitrary"),
    )(p, p, p, ccw, ccb, wrg, brg, wig, big, lam)


def _odd_bwd_a(p, xc, r, ig, hs, dy, wrg, wig, lam, dep, *, tm):
    t = p.shape[0]
    wl = W_LRU
    nt = t // tm
    ng = tm // 8
    per = tm // HALO_C

    def body(gate, xc_ref, r_ref, i_ref, hs_ref, hsh, dy_ref, wrg_ref, wig_ref, lam_ref, dep_ref,
             dxc_ref, dgate_ref, dwrg_ref, dwig_ref, dbrg_ref, dbig_ref, dlam_ref,
             hext, a_s, q_s, g_s, dpr_s, dpi_s, carry):
        i = pl.program_id(0)
        ti = nt - 1 - i
        keep = (ti > 0).astype(F32)
        hext[0:HALO_C, :] = keep * hsh[...]
        hext[HALO_C:, :] = hs_ref[...]
        gt = gate[...]
        sg = _sigmoid(gt)
        dyv = dy_ref[...]
        dgate_ref[...] = (dyv * hs_ref[...] * _dsilu(gt, sg)).astype(BF16)
        q_s[...] = dyv * (gt * sg)
        sp = _softplus_neg(lam_ref[...])
        log_a = (-LRU_C * sp) * r_ref[...]
        a_s[...] = jnp.exp(log_a)

        @pl.when(i == 0)
        def _():
            carry[...] = jnp.zeros_like(carry)
            dwrg_ref[...] = jnp.zeros_like(dwrg_ref)
            dwig_ref[...] = jnp.zeros_like(dwig_ref)

        rowi = lax.broadcasted_iota(jnp.int32, (8, wl), 0)

        def step(gr, c):
            sl = pl.ds(pl.multiple_of((ng - 1 - gr) * 8, 8), 8)
            a0 = a_s[sl, :]
            al = jnp.where(rowi < 7, pltpu.roll(a0, 7, 0), 1.0)
            be = q_s[sl, :]
            for s in (1, 2, 4):
                m = rowi + s <= 7
                al_sh = jnp.where(m, pltpu.roll(al, 8 - s, 0), 1.0)
                be_sh = jnp.where(m, pltpu.roll(be, 8 - s, 0), 0.0)
                be = be + al * be_sh
                al = al * al_sh
            gv = be + al * c
            g_s[sl, :] = gv
            return (a0 * gv)[0:1, :]

        carry[...] = lax.fori_loop(0, ng, step, carry[...])

        gv = g_s[...]
        a = a_s[...]
        mult = jnp.sqrt(_one_minus_exp(2.0 * log_a))
        iv = i_ref[...]
        rv = r_ref[...]
        xcv = xc_ref[...]
        hprev = hext[pl.ds(HALO_C - 1, tm), :]
        dla = gv * hprev * a - (gv * iv * xcv) * (a * a) / mult
        di = gv * mult * xcv
        dpr = (dla * (-LRU_C * sp)) * rv * (1.0 - rv)
        dpi = di * iv * (1.0 - iv)
        dpr_s[...] = dpr
        dpi_s[...] = dpi
        dxc_ref[...] = gv * mult * iv
        dsp = jnp.sum(dla * rv, axis=0, keepdims=True) * (-LRU_C)
        _acc_out(i, dlam_ref, -dsp * _sigmoid(-lam_ref[...]))
        _acc_out(i, dbrg_ref, jnp.sum(dpr, axis=0, keepdims=True))
        _acc_out(i, dbig_ref, jnp.sum(dpi, axis=0, keepdims=True))
        for h in range(LRU_HEADS):
            cs = slice(h * LRU_HD, (h + 1) * LRU_HD)
            xh = xc_ref[:, cs].astype(BF16)
            dr_h = dpr_s[:, cs].astype(BF16)
            di_h = dpi_s[:, cs].astype(BF16)
            dxc_ref[:, cs] += (
                lax.dot_general(dr_h, wrg_ref[h], (((1,), (1,)), ((), ())), preferred_element_type=F32)
                + lax.dot_general(di_h, wig_ref[h], (((1,), (1,)), ((), ())), preferred_element_type=F32))
            dwrg_ref[h] += lax.dot_general(xh, dr_h, (((0,), (0,)), ((), ())), preferred_element_type=F32)
            dwig_ref[h] += lax.dot_general(xh, di_h, (((0,), (0,)), ((), ())), preferred_element_type=F32)

    rrow = lambda w: pl.BlockSpec((tm, w), lambda i: (nt - 1 - i, 0))
    hspec = pl.BlockSpec((HALO_C, wl), lambda i: (jnp.maximum((nt - 1 - i) * per - 1, 0), 0))
    wspec = _full((LRU_HEADS, LRU_HD, LRU_HD))
    return pl.pallas_call(
        body, name="odd_bwd_a", grid=(nt,),
        in_specs=[pl.BlockSpec((tm, wl), lambda i: (nt - 1 - i, 1)), rrow(wl), rrow(wl), rrow(wl), rrow(wl), hspec,
                  rrow(wl), wspec, wspec, _full((1, wl)), ANY],
        out_specs=[rrow(wl), rrow(wl), wspec, wspec, _full((1, wl)), _full((1, wl)), _full((1, wl))],
        out_shape=[jax.ShapeDtypeStruct((t, wl), F32), jax.ShapeDtypeStruct((t, wl), BF16),
                   jax.ShapeDtypeStruct((LRU_HEADS, LRU_HD, LRU_HD), F32),
                   jax.ShapeDtypeStruct((LRU_HEADS, LRU_HD, LRU_HD), F32)] + [jax.ShapeDtypeStruct((1, wl), F32)] * 3,
        scratch_shapes=[pltpu.VMEM((tm + HALO_C, wl), F32)] + [pltpu.VMEM((tm, wl), F32)] * 5
                       + [pltpu.VMEM((1, wl), F32)],
        compiler_params=_params("arbitrary"),
    )(p, xc, r, ig, hs, hs, dy, wrg, wig, lam, dep)


def _odd_bwd_b(p, dxc, dgate, ccw, *, tm):
    t = p.shape[0]
    wl = W_LRU
    nt = t // tm

    def body(xr, xrh, dxc_ref, dxcn, dgate_ref, ccw_ref, dp_ref, dcw_ref, dcb_ref, xext, gext):
        i = pl.program_id(0)

        @pl.when(i == 0)
        def _():
            dcw_ref[...] = jnp.zeros_like(dcw_ref)

        xext[0:HALO_C, :] = (i > 0).astype(F32) * xrh[...]
        xext[HALO_C:, :] = xr[...]
        gext[0:tm, :] = dxc_ref[...]
        gext[tm:, :] = (i < nt - 1).astype(F32) * dxcn[...]
        g = dxc_ref[...]
        acc = jnp.zeros((tm, wl), F32)
        for k in range(LRU_CONV_K):
            acc = acc + ccw_ref[k:k + 1, :] * gext[pl.ds(LRU_CONV_K - 1 - k, tm), :]
            dcw_ref[k:k + 1, :] += jnp.sum(
                g * xext[pl.ds(HALO_C - (LRU_CONV_K - 1) + k, tm), :], axis=0, keepdims=True)

        _acc_out(i, dcb_ref, jnp.sum(g, axis=0, keepdims=True))
        dp_ref[:, 0:wl] = acc.astype(BF16)
        dp_ref[:, wl:2 * wl] = dgate_ref[...]

    row = lambda w: pl.BlockSpec((tm, w), lambda i: (i, 0))
    return pl.pallas_call(
        body, name="odd_bwd_b", grid=(nt,),
        in_specs=[_col(tm, wl, 0), _prev_halo(tm, HALO_C, wl, 0), row(wl), _next_halo(tm, HALO_C, wl, 0, t), row(wl),
                  _full((8, wl))],
        out_specs=[row(2 * wl), _full((8, wl)), _full((1, wl))],
        out_shape=[jax.ShapeDtypeStruct((t, 2 * wl), BF16), jax.ShapeDtypeStruct((8, wl), F32),
                   jax.ShapeDtypeStruct((1, wl), F32)],
        scratch_shapes=[pltpu.VMEM((tm + HALO_C, wl), F32), pltpu.VMEM((tm + HALO_C, wl), F32)],
        compiler_params=_params("arbitrary"),
    )(p, p, dxc, dxc, dgate, ccw)


def _local_step(x, target, layer_weights, final_norm, on_grads):
    t = x.shape[0]
    tm, tx, tl, tw = min(TM_MATMUL, t), min(TM_MIXER, t), min(TM_STREAM, t), min(TM_WGRAD, t)
    h = x
    saved = []
    for layer in range(N_LAYERS):
        w = layer_weights(layer, h)
        p, hn = _norm_matmul(h, w["norm"], w["w_in"], w["dep"], tm=tl)
        if layer % 2 == 0:
            y, *acts = _even_fwd(p, w["conv_w"], w["conv_b"], w["ln_g"], w["ln_b"], w["pool_w"], w["pool_b"],
                                 w["pool_scale"], tm=tx)
        else:
            y, *acts = _odd_fwd(p, w["conv_w"], w["conv_b"], w["w_rg"], w["b_rg"], w["w_ig"], w["b_ig"], w["lam"],
                                tm=tx)
        w_out, dep = w["w_out"](y)
        saved.append((w, w_out, h, p, hn, y, acts))
        h = _out_proj(h, y, w_out, dep, tm=tl)
    loss, dh, d_final = _loss_head(h, final_norm, target, tm=tm)

    dep = d_final
    for layer in reversed(range(N_LAYERS)):
        w, w_out, h_in, p, hn, y, acts = saved[layer]
        sfx = "even" if layer % 2 == 0 else "odd"
        dy, dw_out = _out_proj_bwd(dh, y, w_out, dep, tm=tm)
        dep = on_grads(layer, {"w_out_" + sfx: dw_out}, dep, False)
        if layer % 2 == 0:
            u1, e, dmat = acts
            du1, dd, dgat, dlg, dlb, dpb, dps, dpw = _even_bwd_a(p, u1, e, dmat, dy, w["ln_g"], w["ln_b"], w["pool_w"],
                                                                 w["pool_scale"], dep, tm=tx)
            dp, dcw, dcb = _even_bwd_b(p, du1, dd, dgat, w["conv_w"], tm=tx)
            grads = dict(conv_a_w=dcw[:CONV_K], conv_a_b=dcb, ln_a_g=dlg, ln_a_b=dlb, pool_w=dpw, pool_b=dpb,
                         pool_scale=dps)
        else:
            xc, r, ig, hs = acts
            dxc, dgate, dwrg, dwig, dbrg, dbig, dlam = _odd_bwd_a(p, xc, r, ig, hs, dy, w["w_rg"], w["w_ig"],
                                                                  w["lam"], dep, tm=tx)
            dp, dccw, dccb = _odd_bwd_b(p, dxc, dgate, w["conv_w"], tm=tx)
            grads = dict(conv_c_w=dccw[:LRU_CONV_K], conv_c_b=dccb, w_rg=dwrg, b_rg=dbrg, w_ig=dwig, b_ig=dbig,
                         lru_lambda=dlam)
        dep = on_grads(layer, {"w_in_" + sfx: _in_proj_bwd_w(hn, dp, N_DEV, tm=tw)}, dep, False)
        dh, grads["norm_" + sfx] = _in_proj_bwd_x(dp, w["w_in"], h_in, w["norm"], dh, dep, tm=tl)
        if layer == N_LAYERS - 1:
            grads["final_norm"] = d_final
        dep = on_grads(layer, grads, dep, True)
    return loss, dh


def _slot(px, py, pc):
    return 4 * px + 2 * py + pc


def _peers(x, y, c):
    return [(1 - x if k & 4 else x, 1 - y if k & 2 else y, 1 - c if k & 1 else c) for k in range(1, N_DEV)]


def _all_gather(arrs, name):
    n = len(arrs)

    def body(*refs):
        ins, outs = refs[:n], refs[n:2 * n]
        send_sems, recv_sems, local_sems = refs[2 * n:]
        x, y, c = lax.axis_index("x"), lax.axis_index("y"), lax.axis_index("c")
        me, sibling = (x, y, c), (x, y, 1 - c)
        chips = [(1 - x, y), (x, 1 - y), (1 - x, 1 - y)]

        def copy(a, k, block, to, src=None):
            rows = outs[a].at[_slot(*block)]
            return pltpu.make_async_remote_copy(
                src_ref=rows if src is None else src, dst_ref=rows, send_sem=send_sems.at[a, k],
                recv_sem=recv_sems.at[a, k], device_id=to, device_id_type=MESH)

        mine = [pltpu.make_async_copy(ins[a], outs[a].at[_slot(*me)], local_sems.at[a]) for a in range(n)]
        for cp in mine:
            cp.start()
        first = []
        for a in range(n):
            first.append(copy(a, 0, me, sibling, src=ins[a]))
            first += [copy(a, 1 + j, me, (*chip, c), src=ins[a]) for j, chip in enumerate(chips)]
        for cp in first:
            cp.start()
        passed = []
        for j, chip in enumerate(chips):
            for a in range(n):
                copy(a, 1 + j, (*chip, c), me).wait_recv()
                fwd = copy(a, 4 + j, (*chip, c), sibling)
                fwd.start()
                passed.append(fwd)
        for a in range(n):
            copy(a, 0, sibling, me).wait_recv()
            for j, chip in enumerate(chips):
                copy(a, 4 + j, (*chip, 1 - c), me).wait_recv()
        for cp in first + passed:
            cp.wait_send()
        for cp in mine:
            cp.wait()

    return pl.pallas_call(
        body, name=name,
        in_specs=[ANY] * n, out_specs=[ANY] * n,
        out_shape=[jax.ShapeDtypeStruct((N_DEV,) + a.shape, a.dtype) for a in arrs],
        scratch_shapes=[pltpu.SemaphoreType.DMA((n, 7)), pltpu.SemaphoreType.DMA((n, 7)),
                        pltpu.SemaphoreType.DMA((n,))],
    )(*arrs)


N_COPIES = {"gather": N_PEERS, "scatter": N_PEERS, "chip_gather": 4, "forward": 3}


def _exchange_plan(mode, x, y, c):
    me = _slot(x, y, c)
    if mode == "forward":
        chips = [(1 - x, y), (x, 1 - y), (1 - x, 1 - y)]
        return [((x, y, 1 - c), ("land", _slot(*chip, c)), _slot(*chip, c), _slot(*chip, 1 - c)) for chip in chips]
    peers = _peers(x, y, c)
    if mode == "chip_gather":
        peers = [(x, y, 1 - c), (1 - x, y, c), (x, 1 - y, c), (1 - x, 1 - y, c)]
    return [(p, ("src", _slot(*p)) if mode == "scatter" else ("src", None), me, _slot(*p)) for p in peers]


def _exchange_copy(src_ref, land_ref, plan, send_sem, recv_sem, start):
    peer, (which, block), there, here = plan
    src = land_ref if which == "land" else src_ref
    return pltpu.make_async_remote_copy(
        src_ref=src if block is None else src.at[block], dst_ref=land_ref.at[there if start else here],
        send_sem=send_sem, recv_sem=recv_sem, device_id=peer, device_id_type=MESH)


def _exchange_start(groups, deps, name):
    flat = [pair for _, g in groups for pair in g]
    n, ng = len(flat), len(groups)

    def body(*refs):
        src_refs, land_refs = refs[:n], refs[n:2 * n]
        outs = refs[2 * n + len(deps):]
        sems, token = outs[:2 * ng], outs[2 * ng + 2 * n]
        x, y, c = lax.axis_index("x"), lax.axis_index("y"), lax.axis_index("c")
        base = 0
        for gi, (mode, g) in enumerate(groups):
            nc = N_COPIES[mode]
            for k, plan in enumerate(_exchange_plan(mode, x, y, c)):
                for ai in range(len(g)):
                    _exchange_copy(src_refs[base + ai], land_refs[base + ai], plan, sems[2 * gi].at[ai * nc + k],
                                   sems[2 * gi + 1].at[ai * nc + k], True).start()
            base += len(g)
        token[...] = jnp.zeros_like(token)

    operands = [pltpu.with_memory_space_constraint(a, pltpu.HBM) for a in
                [s for s, _ in flat] + [l for _, l in flat]]
    out_shape = []
    for mode, g in groups:
        out_shape += [pltpu.SemaphoreType.DMA((len(g) * N_COPIES[mode],))] * 2
    out_shape += [pltpu.HBM(a.shape, a.dtype) for a in operands]
    out_shape.append(jax.ShapeDtypeStruct((8, 128), F32))
    outs = pl.pallas_call(
        body, name=name, out_shape=out_shape,
        in_specs=[HBM] * (2 * n) + [ANY] * len(deps),
        out_specs=[SEM] * (2 * ng) + [HBM] * (2 * n) + [pl.BlockSpec(memory_space=pltpu.VMEM)],
        input_output_aliases={i: 2 * ng + i for i in range(2 * n)},
        compiler_params=pltpu.CompilerParams(has_side_effects=pltpu.SideEffectType.DATAFLOW_SIDE_EFFECTING),
    )(*operands, *deps)
    handles, base = [], 0
    for gi, (mode, g) in enumerate(groups):
        srcs = outs[2 * ng + base:2 * ng + base + len(g)]
        lands = outs[2 * ng + n + base:2 * ng + n + base + len(g)]
        handles.append((mode, outs[2 * gi], outs[2 * gi + 1], list(srcs), list(lands)))
        base += len(g)
    return handles, outs[-1]


def _exchange_wait(handle, after, name):
    mode, send_sems, recv_sems, srcs, lands = handle
    n = len(srcs)
    nc = N_COPIES[mode]

    def body(*refs):
        src_refs, land_refs = refs[:n], refs[n:2 * n]
        send_ref, recv_ref = refs[2 * n], refs[2 * n + 1]
        x, y, c = lax.axis_index("x"), lax.axis_index("y"), lax.axis_index("c")
        for k, plan in enumerate(_exchange_plan(mode, x, y, c)):
            for a in range(n):
                cp = _exchange_copy(src_refs[a], land_refs[a], plan, send_ref.at[a * nc + k], recv_ref.at[a * nc + k],
                                    False)
                cp.wait_send()
                cp.wait_recv()

    outs = pl.pallas_call(
        body, name=name,
        out_shape=[pltpu.HBM(a.shape, a.dtype) for a in srcs + lands],
        in_specs=[HBM] * (2 * n) + [SEM, SEM] + [ANY] * len(after),
        out_specs=[HBM] * (2 * n),
        input_output_aliases={i: i for i in range(2 * n)},
        compiler_params=pltpu.CompilerParams(has_side_effects=pltpu.SideEffectType.DATAFLOW_SIDE_EFFECTING),
    )(*srcs, *lands, send_sems, recv_sems, *after)
    return list(outs[n:])


def _sum_parts(parts, *, tr):
    np_, r, c = parts.shape

    def body(p_ref, o_ref):
        acc = p_ref[0].astype(F32)
        for k in range(1, np_):
            acc = acc + p_ref[k].astype(F32)
        o_ref[...] = acc

    return pl.pallas_call(
        body, name="sum_parts", grid=(r // tr,),
        in_specs=[pl.BlockSpec((np_, tr, c), lambda i: (0, i, 0))],
        out_specs=pl.BlockSpec((tr, c), lambda i: (i, 0)),
        out_shape=jax.ShapeDtypeStruct((r, c), F32),
        compiler_params=_params("arbitrary"),
    )(parts)


def _adamw_math(w, g, m, v):
    c1 = 1.0 - ADAM_B1 ** ADAM_STEP
    c2 = 1.0 - ADAM_B2 ** ADAM_STEP
    nm = ADAM_B1 * m + (1.0 - ADAM_B1) * g
    nv = ADAM_B2 * v + (1.0 - ADAM_B2) * (g * g)
    delta = -ADAM_LR * ((nm / c1) / (jnp.sqrt(nv / c2) + ADAM_EPS) + ADAM_WD * w)
    return delta, nm, nv


def _row_tile(r):
    for cand in (256, 128, 64, 32, 16, 8):
        if r % cand == 0 and r > cand:
            return cand
    return r


def _adamw(items, layer0, bufs, name):
    ni = len(items)
    nl = items[0][1].shape[1]
    tiles = [_row_tile(w.shape[1]) for w, _, _, _ in items]
    steps = [w.shape[1] // tr for (w, _, _, _), tr in zip(items, tiles)]
    ns = steps[0]
    assert all(s == ns for s in steps)
    nb = 0 if bufs is None else 4 * ni

    def body(*refs):
        ins, outs = refs[:4 * ni], refs[4 * ni + nb:]
        for k in range(ni):
            w_ref, p_ref, m_ref, v_ref = ins[4 * k:4 * k + 4]
            g = p_ref[0, 0].astype(F32)
            for s in range(1, p_ref.shape[0]):
                g = g + p_ref[s, 0].astype(F32)
            delta, nm, nv = _adamw_math(w_ref[0], g, m_ref[0], v_ref[0])
            g_ref, d_ref, nm_ref, nv_ref = outs[4 * k:4 * k + 4]
            g_ref[0], d_ref[0], nm_ref[0], nv_ref[0] = g, delta, nm, nv

    in_specs, out_specs, out_shape, operands = [], [], [], []
    for (w, parts, m, v), tr in zip(items, tiles):
        blk = pl.BlockSpec((1, tr, w.shape[2]), lambda l, i: (layer0 + l, i, 0))
        in_specs += [blk, pl.BlockSpec((parts.shape[0], 1, tr, w.shape[2]), lambda l, i: (0, l, i, 0)), blk, blk]
        operands += [w, parts, m, v]
        out_specs += [blk] * 4
        out_shape += [jax.ShapeDtypeStruct(w.shape, F32)] * 4
    if bufs is not None:
        in_specs += [ANY] * nb
        operands += [b for item in bufs for b in item]
    outs = pl.pallas_call(
        body, name=name, grid=(nl, ns), in_specs=in_specs, out_specs=out_specs, out_shape=out_shape,
        input_output_aliases={4 * ni + i: i for i in range(nb)},
        compiler_params=_params("arbitrary", "arbitrary"),
    )(*operands)
    return [tuple(outs[4 * k:4 * k + 4]) for k in range(ni)]


NAMES = ("norm_even", "w_in_even", "conv_a_w", "conv_a_b", "ln_a_g", "ln_a_b", "pool_w", "pool_b", "pool_scale",
         "w_out_even", "norm_odd", "w_in_odd", "conv_c_w", "conv_c_b", "w_rg", "b_rg", "w_ig", "b_ig", "lru_lambda",
         "w_out_odd", "final_norm")
SMALL_GATHERED = ("conv_a_w", "pool_b", "norm_odd", "conv_c_w", "conv_c_b", "b_rg", "b_ig", "lru_lambda")
BIG = (("w_in_even", "w_out_even"), ("w_in_odd", "w_out_odd"))
SMALL = (("conv_a_w", "pool_b", "pool_w"), ("norm_odd", "conv_c_w", "conv_c_b", "b_rg", "b_ig", "lru_lambda"))
REPLICATED = (("norm_even", "conv_a_b", "ln_a_g", "ln_a_b", "pool_scale"), ("w_rg", "w_ig"))
PACK_ROW = 1024


def _pack_rows(flat2d):
    pad = (-flat2d.shape[1]) % PACK_ROW
    return jnp.pad(flat2d, ((0, 0), (0, pad))).reshape(flat2d.shape[0], -1, 128)


def _unpack(flat, shapes):
    out, off = [], 0
    for s in shapes:
        n = 1
        for d in s:
            n *= d
        out.append(flat[..., off:off + n].reshape(flat.shape[:-1] + tuple(s)))
        off += n
    return out


def _to_global(name, g):
    if name in ("conv_a_w", "pool_b", "conv_c_w"):
        return jnp.transpose(g, (1, 2, 0, 3)).reshape(g.shape[1], g.shape[2], -1)
    if name == "pool_w":
        return jnp.transpose(g, (1, 2, 0, 3, 4)).reshape(2, 4, POOL_GW, POOL_GW)
    return jnp.transpose(g, (1, 0, 2)).reshape(g.shape[1], -1)


def _to_blocks(name, g):
    if name == "conv_a_w":
        return jnp.transpose(g.reshape(CONV_K, N_DEV, -1), (1, 0, 2))
    if name == "conv_c_w":
        return jnp.transpose(g.reshape(LRU_CONV_K, N_DEV, -1), (1, 0, 2))
    if name == "pool_b":
        return jnp.transpose(g.reshape(4, N_DEV, -1), (1, 0, 2))
    if name == "pool_w":
        return jnp.transpose(g.reshape(4, N_DEV, POOL_GW // N_DEV, POOL_GW), (1, 0, 2, 3))
    return g.reshape(N_DEV, -1)


def _as3d(a):
    if a.ndim == 1:
        return a.reshape(1, 1, -1)
    if a.ndim == 2:
        return a.reshape(a.shape[0], 1, a.shape[1])
    return a.reshape(a.shape[0], -1, a.shape[-1])


def kernel(x, norm_even, w_in_even, conv_a_w, conv_a_b, ln_a_g, ln_a_b, pool_w, pool_b, pool_scale, w_out_even, norm_odd, w_in_odd, conv_c_w, conv_c_b, w_rg, b_rg, w_ig, b_ig, lru_lambda, w_out_odd, final_norm, loss_target, m_norm_even, m_w_in_even, m_conv_a_w, m_conv_a_b, m_ln_a_g, m_ln_a_b, m_pool_w, m_pool_b, m_pool_scale, m_w_out_even, m_norm_odd, m_w_in_odd, m_conv_c_w, m_conv_c_b, m_w_rg, m_b_rg, m_w_ig, m_b_ig, m_lru_lambda, m_w_out_odd, m_final_norm, v_norm_even, v_w_in_even, v_conv_a_w, v_conv_a_b, v_ln_a_g, v_ln_a_b, v_pool_w, v_pool_b, v_pool_scale, v_w_out_even, v_norm_odd, v_w_in_odd, v_conv_c_w, v_conv_c_b, v_w_rg, v_b_rg, v_w_ig, v_b_ig, v_lru_lambda, v_w_out_odd, v_final_norm):
    w_loc = dict(zip(NAMES, [norm_even, w_in_even, conv_a_w, conv_a_b, ln_a_g, ln_a_b, pool_w, pool_b, pool_scale,
                             w_out_even, norm_odd, w_in_odd, conv_c_w, conv_c_b, w_rg, b_rg, w_ig, b_ig, lru_lambda,
                             w_out_odd, final_norm]))
    m_loc = dict(zip(NAMES, [m_norm_even, m_w_in_even, m_conv_a_w, m_conv_a_b, m_ln_a_g, m_ln_a_b, m_pool_w, m_pool_b,
                             m_pool_scale, m_w_out_even, m_norm_odd, m_w_in_odd, m_conv_c_w, m_conv_c_b, m_w_rg,
                             m_b_rg, m_w_ig, m_b_ig, m_lru_lambda, m_w_out_odd, m_final_norm]))
    v_loc = dict(zip(NAMES, [v_norm_even, v_w_in_even, v_conv_a_w, v_conv_a_b, v_ln_a_g, v_ln_a_b, v_pool_w, v_pool_b,
                             v_pool_scale, v_w_out_even, v_norm_odd, v_w_in_odd, v_conv_c_w, v_conv_c_b, v_w_rg,
                             v_b_rg, v_w_ig, v_b_ig, v_lru_lambda, v_w_out_odd, v_final_norm]))
    me = _slot(lax.axis_index("x"), lax.axis_index("y"), lax.axis_index("c"))

    def landing(own):
        zone = lax.empty((N_DEV,) + own.shape[1:], own.dtype)
        return lax.dynamic_update_slice(zone, own, (me,) + (0,) * (own.ndim - 1))

    small_shapes = [w_loc[n].shape for n in SMALL_GATHERED]
    small = jnp.concatenate([w_loc[n].reshape(1, -1) for n in SMALL_GATHERED], axis=1)
    first = _all_gather([w_in_even[0].astype(BF16), pool_w.astype(BF16), _pack_rows(small)[0]], "gather_first")
    g_small = dict(zip(SMALL_GATHERED, [_to_global(n, g) for n, g in
                                        zip(SMALL_GATHERED, _unpack(first[2].reshape(N_DEV, -1), small_shapes))]))
    pool_w_all = _to_global("pool_w", first[1])

    def pairs_of(shards):
        return [(s.astype(BF16), landing(s.astype(BF16)[None])) for s in shards]

    shards = {1: [w_in_odd[0], w_out_odd[0]], 2: [w_in_even[1], w_out_even[1]], 3: [w_in_odd[1], w_out_odd[1]]}
    leg_a, leg_b = {}, {}
    (w_out_0, leg_a[1]), token_1 = _exchange_start(
        [("gather", pairs_of([w_out_even[0]])), ("chip_gather", pairs_of(shards[1]))], [first[0]], "gather_start_1")
    unused = jnp.zeros((8, 128), F32)

    def second_leg(layer, y):
        lands = _exchange_wait(leg_a[layer], [y], f"gather_wait_a_{layer}")
        (leg_b[layer],), token = _exchange_start([("forward", [(unused, l) for l in lands])], [],
                                                 f"gather_forward_{layer}")
        return token

    def layer_weights(layer, h):
        j = layer // 2
        dep = h
        if layer == 0:
            w_in, dep = first[0], token_1
            w_out_now = lambda y: _exchange_wait(w_out_0, [y], "gather_wait_out_0")[0]
        else:
            w_in, w_out_got = _exchange_wait(leg_b[layer], [h], f"gather_wait_b_{layer}")
            w_out_now = lambda y: w_out_got
            if layer + 1 in shards:
                (leg_a[layer + 1],), dep = _exchange_start([("chip_gather", pairs_of(shards[layer + 1]))], [w_in],
                                                           f"gather_start_{layer + 1}")

        def w_out(y):
            return w_out_now(y), (second_leg(layer + 1, y) if layer + 1 in shards else y)

        row = lambda a: a[j][None]
        if layer % 2 == 0:
            return dict(dep=dep, norm=row(norm_even), w_in=w_in,
                        w_out=lambda y: (lambda wo, d: (wo.reshape(W_EVEN_MIX, D_MODEL), d))(*w_out(y)),
                        conv_w=jnp.pad(g_small["conv_a_w"][j], ((0, 1), (0, 0))), conv_b=row(conv_a_b),
                        ln_g=row(ln_a_g), ln_b=row(ln_a_b), pool_w=pool_w_all[j],
                        pool_b=g_small["pool_b"][j].reshape(1, W_POOL), pool_scale=row(pool_scale))
        return dict(dep=dep, norm=row(g_small["norm_odd"]), w_in=w_in,
                    w_out=lambda y: (lambda wo, d: (wo.reshape(W_LRU, D_MODEL), d))(*w_out(y)),
                    conv_w=jnp.pad(g_small["conv_c_w"][j], ((0, 4), (0, 0))), conv_b=row(g_small["conv_c_b"]),
                    w_rg=w_rg[j].astype(BF16), b_rg=row(g_small["b_rg"]), w_ig=w_ig[j].astype(BF16),
                    b_ig=row(g_small["b_ig"]), lam=row(g_small["lru_lambda"]))

    pending, scatters, small_layout = {}, {}, {}
    last_token = []

    def on_grads(layer, grads, dep, last):
        par = layer % 2
        w_in_name, w_out_name = BIG[par]
        pending.setdefault(layer, {}).update(grads)
        have = pending[layer]
        send = {}
        if w_in_name in have and (layer == 0 or last):
            send["in"] = have.pop(w_in_name)
        if w_out_name in have and (layer == 0 or last):
            send["out"] = have.pop(w_out_name).reshape(N_DEV, -1, D_MODEL)
        if last:
            small_blocks = [_to_blocks(n, have[n]) for n in SMALL[par]]
            rep_names = list(REPLICATED[par]) + (["final_norm"] if "final_norm" in have else [])
            rep = jnp.concatenate([have[n].reshape(-1) for n in rep_names]).reshape(N_DEV, -1)
            small_layout[layer] = ([b.shape[1:] for b in small_blocks], rep.shape[1])
            send["small"] = _pack_rows(jnp.concatenate([b.reshape(N_DEV, -1) for b in small_blocks] + [rep], axis=1))
        if not send:
            return dep
        pairs = [(s, landing(lax.dynamic_slice_in_dim(s, me, 1, 0))) for s in send.values()]
        (handle,), token = _exchange_start([("scatter", pairs)], [], f"scatter_start_{layer}_{'_'.join(send)}")
        scatters.setdefault(layer, []).append((list(send), handle))
        last_token[:] = [token]
        return token

    loss, grad_x = _local_step(x[0], loss_target[0], layer_weights, final_norm[None], on_grads)

    w3 = {n: _as3d(w_loc[n]) for n in NAMES}
    m3 = {n: _as3d(m_loc[n]) for n in NAMES}
    v3 = {n: _as3d(v_loc[n]) for n in NAMES}
    results, rep_parts = {}, {}
    after = list(last_token)
    for layer in (3, 2, 1, 0):
        par, j = layer % 2, layer // 2
        got = {}
        for keys, handle in scatters[layer]:
            got.update(zip(keys, _exchange_wait(handle, after, f"scatter_wait_{layer}_{'_'.join(keys)}")))
        got_in, got_out, got_small = got["in"], got["out"], got["small"]
        shapes, rep_len = small_layout[layer]
        *small_parts, rep_parts[layer] = _unpack(got_small.reshape(N_DEV, -1), list(shapes) + [(rep_len,)])
        names = list(BIG[par]) + list(SMALL[par])
        parts = [got_in, got_out] + small_parts
        big_items, small_items = [], []
        for n, pt in zip(names, parts):
            item = (w3[n], pt.reshape((N_DEV, 1) + w3[n].shape[1:]), m3[n], v3[n])
            (big_items if w3[n].shape[1] >= 128 else small_items).append((n, item))
        for n, item in big_items:
            results[n] = _adamw([item], j, [results[n]] if n in results else None, f"adamw_{n}_{layer}")[0]
        snames = [n for n, _ in small_items]
        prev = [results[n] for n in snames] if snames[0] in results else None
        for n, r in zip(snames, _adamw([it for _, it in small_items], j, prev, f"adamw_small_{layer}")):
            results[n] = r
        after = [results[names[0]][1]]

    rep_pack = _pack_rows(jnp.concatenate([rep_parts[l] for l in range(N_LAYERS)], axis=1))
    rep_mine = _sum_parts(rep_pack, tr=rep_pack.shape[1])
    rep_all = _all_gather([rep_mine], "gather_replicated")[0].reshape(N_DEV, -1)
    rep_grads, off = {}, 0
    for layer in range(N_LAYERS):
        rep_len = small_layout[layer][1]
        flat = rep_all[:, off:off + rep_len].reshape(-1)
        off += rep_len
        rep_names = list(REPLICATED[layer % 2]) + (["final_norm"] if layer == N_LAYERS - 1 else [])
        for n, g in zip(rep_names, _unpack(flat, [w_loc[n].shape[1:] if n != "final_norm" else w_loc[n].shape
                                                  for n in rep_names])):
            rep_grads.setdefault(n, []).append(g)
    rep_items = {n: (w3[n], _as3d(jnp.stack(rep_grads[n]) if n != "final_norm" else rep_grads[n][0])[None],
                     m3[n], v3[n]) for par in range(2) for n in REPLICATED[par]}
    rep_items["final_norm"] = (w3["final_norm"], _as3d(rep_grads["final_norm"][0])[None], m3["final_norm"],
                               v3["final_norm"])
    for n in ("w_rg", "w_ig"):
        results[n] = _adamw([rep_items[n]], 0, None, f"adamw_{n}")[0]
    vec_names = list(REPLICATED[0])
    for n, r in zip(vec_names, _adamw([rep_items[n] for n in vec_names], 0, None, "adamw_replicated")):
        results[n] = r
    results["final_norm"] = _adamw([rep_items["final_norm"]], 0, None, "adamw_final_norm")[0]

    total = lax.psum(loss[0, 0], ("x", "y", "c"))
    outs = [[results[n][k].reshape(w_loc[n].shape) for n in NAMES] for k in range(4)]
    return (total, grad_x[None], *outs[0], *outs[1], *outs[2], *outs[3])
```

```python
import functools

import jax
import jax.numpy as jnp
from jax import lax
from jax.experimental import pallas as pl
from jax.experimental.pallas import tpu as pltpu

F32 = jnp.float32
BF16 = jnp.bfloat16

N_DEV = 8
N_PEERS = N_DEV - 1
N_LAYERS = 4
D_MODEL = 1024
EPS_RMS = 1e-6
EPS_LN = 1e-5
W_CONV = 1024
CONV_K = 31
W_POOL = 1024
POOL_WINDOWS = (2, 4, 8, 16)
POOL_GW = 256
W_EVEN_IN = 5120
W_EVEN_MIX = 2048
LRU_HEADS = 12
LRU_HD = 128
W_LRU = 1536
LRU_CONV_K = 4
LRU_C = 8.0
ADAM_LR = 0.001
ADAM_B1 = 0.9
ADAM_B2 = 0.999
ADAM_EPS = 1e-08
ADAM_WD = 0.01
ADAM_STEP = 10

HALO = 32
HALO_C = 8
TM_MATMUL = 512
TM_STREAM = 1024
TM_WGRAD = 2048
TM_MIXER = 256
CONV_ROWS = 128
VMEM_LIMIT = 56 * 1024 * 1024
MESH = pl.DeviceIdType.MESH
ANY = pl.BlockSpec(memory_space=pl.ANY)
HBM = pl.BlockSpec(memory_space=pltpu.HBM)
SEM = pl.BlockSpec(memory_space=pltpu.SEMAPHORE)


def _params(*sem):
    return pltpu.CompilerParams(dimension_semantics=sem, vmem_limit_bytes=VMEM_LIMIT)


def _sigmoid(z):
    return jax.nn.sigmoid(z)


def _dsilu(z, s):
    return s * (1.0 + z * (1.0 - s))


def _full(shape):
    nd = len(shape)
    return pl.BlockSpec(shape, lambda *_: (0,) * nd)


def _norm_matmul(h, g, w, dep, *, tm):
    t, d = h.shape
    nd, _, nb = w.shape

    def body(h_ref, g_ref, w_ref, dep_ref, p_ref, hn_ref):
        @pl.when(pl.program_id(1) == 0)
        def _():
            x = h_ref[...]
            r = lax.rsqrt(jnp.mean(x * x, axis=-1, keepdims=True) + EPS_RMS)
            hn_ref[...] = ((x * r) * g_ref[...]).astype(BF16)

        p_ref[...] = jnp.dot(hn_ref[...], w_ref[0], preferred_element_type=F32)

    return pl.pallas_call(
        body, name="norm_matmul", grid=(t // tm, nd),
        in_specs=[pl.BlockSpec((tm, d), lambda i, j: (i, 0)), _full((1, d)),
                  pl.BlockSpec((1, d, nb), lambda i, j: (j, 0, 0)), ANY],
        out_specs=[pl.BlockSpec((tm, nb), lambda i, j: (i, j)), pl.BlockSpec((tm, d), lambda i, j: (i, 0))],
        out_shape=[jax.ShapeDtypeStruct((t, nd * nb), F32), jax.ShapeDtypeStruct((t, d), BF16)],
        compiler_params=_params("arbitrary", "arbitrary"),
    )(h, g, w, dep)


def _out_proj(h, y, w, dep, *, tm):
    t, d = h.shape
    k = y.shape[1]

    def body(h_ref, y_ref, w_ref, dep_ref, o_ref):
        o_ref[...] = h_ref[...] + jnp.dot(y_ref[...], w_ref[...], preferred_element_type=F32)

    return pl.pallas_call(
        body, name="out_proj", grid=(t // tm,),
        in_specs=[pl.BlockSpec((tm, d), lambda i: (i, 0)), pl.BlockSpec((tm, k), lambda i: (i, 0)), _full((k, d)),
                  ANY],
        out_specs=pl.BlockSpec((tm, d), lambda i: (i, 0)),
        out_shape=jax.ShapeDtypeStruct((t, d), F32),
        compiler_params=_params("arbitrary"),
    )(h, y, w, dep)


def _out_proj_bwd(dh, y, w, dep, *, tm):
    t, d = dh.shape
    k = y.shape[1]
    nt = t // tm

    def body(dh_ref, y_ref, w_ref, dep_ref, dy_ref, dw_ref, acc):
        i = pl.program_id(0)
        g = dh_ref[...].astype(BF16)
        dy_ref[...] = lax.dot_general(g, w_ref[...], (((1,), (1,)), ((), ())), preferred_element_type=F32)
        part = lax.dot_general(y_ref[...], g, (((0,), (0,)), ((), ())), preferred_element_type=F32)

        @pl.when(i == 0)
        def _():
            acc[...] = part

        @pl.when(i > 0)
        def _():
            acc[...] += part

        @pl.when(i == nt - 1)
        def _():
            dw_ref[...] = acc[...].astype(BF16)

    return pl.pallas_call(
        body, name="out_proj_bwd", grid=(nt,),
        in_specs=[pl.BlockSpec((tm, d), lambda i: (i, 0)), pl.BlockSpec((tm, k), lambda i: (i, 0)), _full((k, d)),
                  ANY],
        out_specs=[pl.BlockSpec((tm, k), lambda i: (i, 0)), _full((k, d))],
        out_shape=[jax.ShapeDtypeStruct((t, k), F32), jax.ShapeDtypeStruct((k, d), BF16)],
        scratch_shapes=[pltpu.VMEM((k, d), F32)],
        compiler_params=_params("arbitrary"),
    )(dh, y, w, dep)


def _in_proj_bwd_x(dp, w, h, g, dh_out, dep, *, tm):
    t, d = h.shape
    nd, _, nb = w.shape
    nt = t // tm

    def body(dp_ref, w_ref, h_ref, g_ref, dho_ref, dep_ref, dh_ref, dg_ref, acc):
        i, j = pl.program_id(0), pl.program_id(1)
        part = lax.dot_general(dp_ref[...], w_ref[0], (((1,), (1,)), ((), ())), preferred_element_type=F32)

        @pl.when(j == 0)
        def _():
            acc[...] = part

        @pl.when(j > 0)
        def _():
            acc[...] += part

        @pl.when(j == nd - 1)
        def _():
            x = h_ref[...]
            r = lax.rsqrt(jnp.mean(x * x, axis=-1, keepdims=True) + EPS_RMS)
            dy = acc[...]
            gd = dy * g_ref[...]
            m = jnp.mean(gd * x, axis=-1, keepdims=True)
            dh_ref[...] = dho_ref[...] + r * gd - x * (r * r * r * m)
            dgp = jnp.sum(dy * x * r, axis=0, keepdims=True)

            @pl.when(i == 0)
            def _():
                dg_ref[...] = dgp

            @pl.when(i > 0)
            def _():
                dg_ref[...] += dgp

    return pl.pallas_call(
        body, name="in_proj_bwd_x", grid=(nt, nd),
        in_specs=[pl.BlockSpec((tm, nb), lambda i, j: (i, j)), pl.BlockSpec((1, d, nb), lambda i, j: (j, 0, 0)),
                  pl.BlockSpec((tm, d), lambda i, j: (i, 0)), _full((1, d)), pl.BlockSpec((tm, d), lambda i, j: (i, 0)),
                  ANY],
        out_specs=[pl.BlockSpec((tm, d), lambda i, j: (i, 0)), _full((1, d))],
        out_shape=[jax.ShapeDtypeStruct((t, d), F32), jax.ShapeDtypeStruct((1, d), F32)],
        scratch_shapes=[pltpu.VMEM((tm, d), F32)],
        compiler_params=_params("arbitrary", "arbitrary"),
    )(dp, w, h, g, dh_out, dep)


def _in_proj_bwd_w(hn, dp, nd, *, tm):
    t, d = hn.shape
    nb = dp.shape[1] // nd
    nt = t // tm

    def body(hn_ref, dp_ref, dw_ref, acc):
        i = pl.program_id(1)
        part = lax.dot_general(hn_ref[...], dp_ref[...], (((0,), (0,)), ((), ())), preferred_element_type=F32)

        @pl.when(i == 0)
        def _():
            acc[...] = part

        @pl.when(i > 0)
        def _():
            acc[...] += part

        @pl.when(i == nt - 1)
        def _():
            dw_ref[0] = acc[...].astype(BF16)

    return pl.pallas_call(
        body, name="in_proj_bwd_w", grid=(nd, nt),
        in_specs=[pl.BlockSpec((tm, d), lambda j, i: (i, 0)), pl.BlockSpec((tm, nb), lambda j, i: (i, j))],
        out_specs=pl.BlockSpec((1, d, nb), lambda j, i: (j, 0, 0)),
        out_shape=jax.ShapeDtypeStruct((nd, d, nb), BF16),
        scratch_shapes=[pltpu.VMEM((d, nb), F32)],
        compiler_params=_params("arbitrary", "arbitrary"),
    )(hn, dp)


def _loss_head(h, g, target, *, tm):
    t, d = h.shape
    nt = t // tm

    def body(h_ref, g_ref, t_ref, loss_ref, dh_ref, dg_ref):
        i = pl.program_id(0)
        x = h_ref[...]
        r = lax.rsqrt(jnp.mean(x * x, axis=-1, keepdims=True) + EPS_RMS)
        xr = x * r
        err = xr * g_ref[...] - t_ref[...]
        lp = 0.5 * jnp.sum(jnp.mean(err * err, axis=-1, keepdims=True), axis=0, keepdims=True)
        dy = err * (1.0 / d)
        gd = dy * g_ref[...]
        m = jnp.mean(gd * x, axis=-1, keepdims=True)
        dh_ref[...] = r * gd - x * (r * r * r * m)
        dgp = jnp.sum(dy * xr, axis=0, keepdims=True)

        @pl.when(i == 0)
        def _():
            loss_ref[...] = lp
            dg_ref[...] = dgp

        @pl.when(i > 0)
        def _():
            loss_ref[...] += lp
            dg_ref[...] += dgp

    return pl.pallas_call(
        body, name="loss_head", grid=(nt,),
        in_specs=[pl.BlockSpec((tm, d), lambda i: (i, 0)), _full((1, d)), pl.BlockSpec((tm, d), lambda i: (i, 0))],
        out_specs=[_full((1, 1)), pl.BlockSpec((tm, d), lambda i: (i, 0)), _full((1, d))],
        out_shape=[jax.ShapeDtypeStruct((1, 1), F32), jax.ShapeDtypeStruct((t, d), F32),
                   jax.ShapeDtypeStruct((1, d), F32)],
        compiler_params=_params("arbitrary"),
    )(h, g, target)


def _col(tm, w, c):
    return pl.BlockSpec((tm, w), lambda i: (i, c))


def _prev_halo(tm, rows, w, c):
    per = tm // rows
    return pl.BlockSpec((rows, w), lambda i: (jnp.maximum(i * per - 1, 0), c))


def _next_halo(tm, rows, w, c, t):
    per = tm // rows
    last = t // rows - 1
    return pl.BlockSpec((rows, w), lambda i: (jnp.minimum((i + 1) * per, last), c))


def _inv_count(first_row, rows, window):
    tpos = first_row + lax.broadcasted_iota(jnp.int32, (rows, 1), 0)
    return 1.0 / jnp.minimum(tpos + 1, window).astype(F32)


def _even_fwd(p, cw, cb, lg, lb, pw, pb, ps, *, tm):
    t = p.shape[0]
    wc = W_CONV

    def body(av, ag, agate, bv, bgate, avh, agh, bvh, cw_ref, cb_ref, lg_ref, lb_ref, pw_ref, pb_ref, ps_ref,
             y_ref, u1_ref, e_ref, d_ref, uext, vext):
        i = pl.program_id(0)
        keep = (i > 0).astype(F32)
        uext[0:HALO, :] = keep * (avh[...] * _sigmoid(agh[...]))
        uext[HALO:, :] = av[...] * _sigmoid(ag[...])
        vext[0:HALO, :] = keep * bvh[...]
        vext[HALO:, :] = bv[...]
        for c in range(0, wc, 128):
            for rb in range(0, tm, CONV_ROWS):
                acc = jnp.broadcast_to(cb_ref[:, c:c + 128], (CONV_ROWS, 128))
                for r in range(8):
                    shifted = uext[pl.ds(rb + 8 - r, CONV_ROWS + HALO - 8), c:c + 128]
                    for q in range(HALO // 8):
                        s = 8 * q + r
                        if s < CONV_K:
                            acc = acc + cw_ref[CONV_K - 1 - s:CONV_K - s, c:c + 128] * shifted[24 - 8 * q:24 - 8 * q + CONV_ROWS]
                u1_ref[rb:rb + CONV_ROWS, c:c + 128] = acc
        u1 = u1_ref[...]
        mu = jnp.mean(u1, axis=-1, keepdims=True)
        xc = u1 - mu
        rs = lax.rsqrt(jnp.mean(xc * xc, axis=-1, keepdims=True) + EPS_LN)
        u2 = (xc * rs) * lg_ref[...] + lb_ref[...]
        u3 = u2 * _sigmoid(u2)
        ga = agate[...]
        y_ref[:, 0:wc] = (u3 * (ga * _sigmoid(ga))).astype(BF16)
        for g, win in enumerate(POOL_WINDOWS):
            cs = slice(g * POOL_GW, (g + 1) * POOL_GW)
            s = vext[pl.ds(HALO, tm), cs]
            for j in range(1, win):
                s = s + vext[pl.ds(HALO - j, tm), cs]
            dg = s * _inv_count(i * tm, tm, win) - vext[pl.ds(HALO, tm), cs]
            dgb = dg.astype(BF16)
            d_ref[:, cs] = dgb
            eg = jnp.dot(dgb, pw_ref[g], preferred_element_type=F32) + pb_ref[:, cs]
            e_ref[:, cs] = eg
            gb = bgate[:, cs]
            y_ref[:, wc + g * POOL_GW:wc + (g + 1) * POOL_GW] = ((eg * ps_ref[:, cs]) * (gb * _sigmoid(gb))).astype(BF16)

    row = lambda w: pl.BlockSpec((tm, w), lambda i: (i, 0))
    return pl.pallas_call(
        body, name="even_fwd", grid=(t // tm,),
        in_specs=[_col(tm, wc, 0), _col(tm, wc, 1), _col(tm, wc, 2), _col(tm, wc, 3), _col(tm, wc, 4),
                  _prev_halo(tm, HALO, wc, 0), _prev_halo(tm, HALO, wc, 1), _prev_halo(tm, HALO, wc, 3),
                  _full((32, wc)), _full((1, wc)), _full((1, wc)), _full((1, wc)),
                  _full((4, POOL_GW, POOL_GW)), _full((1, wc)), _full((1, wc))],
        out_specs=[row(2 * wc), row(wc), row(wc), row(wc)],
        out_shape=[jax.ShapeDtypeStruct((t, 2 * wc), BF16), jax.ShapeDtypeStruct((t, wc), F32),
                   jax.ShapeDtypeStruct((t, wc), F32), jax.ShapeDtypeStruct((t, wc), BF16)],
        scratch_shapes=[pltpu.VMEM((tm + HALO, wc), F32), pltpu.VMEM((tm + HALO, wc), F32)],
        compiler_params=_params("arbitrary"),
    )(p, p, p, p, p, p, p, p, cw, cb, lg, lb, pw, pb, ps)


def _acc_out(i, ref, val):
    @pl.when(i == 0)
    def _():
        ref[...] = val

    @pl.when(i > 0)
    def _():
        ref[...] += val


def _even_bwd_a(p, u1, e, dmat, dy, lg, lb, pw, ps, dep, *, tm):
    t = p.shape[0]
    wc = W_CONV

    def body(agate, bgate, u1_ref, e_ref, d_ref, dya, dyb, lg_ref, lb_ref, pw_ref, ps_ref, dep_ref,
             du1_ref, dd_ref, dgat_ref, dlg_ref, dlb_ref, dpb_ref, dps_ref, dpw_ref):
        i = pl.program_id(0)

        @pl.when(i == 0)
        def _():
            dpw_ref[...] = jnp.zeros_like(dpw_ref)

        u1 = u1_ref[...]
        mu = jnp.mean(u1, axis=-1, keepdims=True)
        xc = u1 - mu
        rs = lax.rsqrt(jnp.mean(xc * xc, axis=-1, keepdims=True) + EPS_LN)
        xh = xc * rs
        u2 = xh * lg_ref[...] + lb_ref[...]
        s2 = _sigmoid(u2)
        ga = agate[...]
        sa = _sigmoid(ga)
        dy_a = dya[...]
        dgat_ref[:, 0:wc] = (dy_a * (u2 * s2) * _dsilu(ga, sa)).astype(BF16)
        du2 = dy_a * (ga * sa) * _dsilu(u2, s2)
        _acc_out(i, dlg_ref, jnp.sum(du2 * xh, axis=0, keepdims=True))
        _acc_out(i, dlb_ref, jnp.sum(du2, axis=0, keepdims=True))
        dxh = du2 * lg_ref[...]
        m1 = jnp.mean(dxh, axis=-1, keepdims=True)
        m2 = jnp.mean(dxh * xh, axis=-1, keepdims=True)
        du1_ref[...] = rs * (dxh - m1 - xh * m2)

        gb = bgate[...]
        sb = _sigmoid(gb)
        ev = e_ref[...]
        dy_b = dyb[...]
        dgat_ref[:, wc:2 * wc] = (dy_b * (ev * ps_ref[...]) * _dsilu(gb, sb)).astype(BF16)
        dz = dy_b * (gb * sb)
        _acc_out(i, dps_ref, jnp.sum(dz * ev, axis=0, keepdims=True))
        de = dz * ps_ref[...]
        _acc_out(i, dpb_ref, jnp.sum(de, axis=0, keepdims=True))
        for g in range(len(POOL_WINDOWS)):
            cs = slice(g * POOL_GW, (g + 1) * POOL_GW)
            deg = de[:, cs].astype(BF16)
            dd_ref[:, cs] = lax.dot_general(deg, pw_ref[g], (((1,), (1,)), ((), ())), preferred_element_type=F32)
            dpw_ref[g] += lax.dot_general(d_ref[:, cs], deg, (((0,), (0,)), ((), ())), preferred_element_type=F32)

    row = lambda w: pl.BlockSpec((tm, w), lambda i: (i, 0))
    return pl.pallas_call(
        body, name="even_bwd_a", grid=(t // tm,),
        in_specs=[_col(tm, wc, 2), _col(tm, wc, 4), row(wc), row(wc), row(wc), _col(tm, wc, 0), _col(tm, wc, 1),
                  _full((1, wc)), _full((1, wc)), _full((4, POOL_GW, POOL_GW)), _full((1, wc)), ANY],
        out_specs=[row(wc), row(wc), row(2 * wc), _full((1, wc)), _full((1, wc)), _full((1, wc)), _full((1, wc)),
                   _full((4, POOL_GW, POOL_GW))],
        out_shape=[jax.ShapeDtypeStruct((t, wc), F32), jax.ShapeDtypeStruct((t, wc), F32),
                   jax.ShapeDtypeStruct((t, 2 * wc), BF16)] + [jax.ShapeDtypeStruct((1, wc), F32)] * 4
                  + [jax.ShapeDtypeStruct((4, POOL_GW, POOL_GW), F32)],
        compiler_params=_params("arbitrary"),
    )(p, p, u1, e, dmat, dy, dy, lg, lb, pw, ps, dep)


def _even_bwd_b(p, du1, dd, dgat, cw, *, tm):
    t = p.shape[0]
    wc = W_CONV
    nt = t // tm

    def body(av, ag, avh, agh, du1_ref, du1n, dd_ref, ddn, dgat_ref, cw_ref, dp_ref, dcw_ref, dcb_ref,
             uext, gext, dext, du0, dcw8):
        i = pl.program_id(0)
        keep_p = (i > 0).astype(F32)
        keep_n = (i < nt - 1).astype(F32)

        @pl.when(i == 0)
        def _():
            dcw8[...] = jnp.zeros_like(dcw8)

        a = av[...]
        sg = _sigmoid(ag[...])
        uext[0:HALO, :] = keep_p * (avh[...] * _sigmoid(agh[...]))
        uext[HALO:, :] = a * sg
        gext[0:tm, :] = du1_ref[...]
        gext[tm:, :] = keep_n * du1n[...]
        for c in range(0, wc, 128):
            for rb in range(0, tm, CONV_ROWS):
                acc = jnp.zeros((CONV_ROWS, 128), F32)
                for r in range(8):
                    ahead = gext[pl.ds(rb + r, CONV_ROWS + HALO - 8), c:c + 128]
                    for q in range(HALO // 8):
                        s = 8 * q + r
                        if s < CONV_K:
                            acc = acc + cw_ref[CONV_K - 1 - s:CONV_K - s, c:c + 128] * ahead[8 * q:8 * q + CONV_ROWS]
                du0[rb:rb + CONV_ROWS, c:c + 128] = acc
                gcur = du1_ref[rb:rb + CONV_ROWS, c:c + 128]
                for r in range(8):
                    behind = uext[pl.ds(rb + 8 - r, CONV_ROWS + HALO - 8), c:c + 128]
                    for q in range(HALO // 8):
                        s = 8 * q + r
                        if s < CONV_K:
                            prod = gcur * behind[24 - 8 * q:24 - 8 * q + CONV_ROWS]
                            part = prod[0:8]
                            for o in range(8, CONV_ROWS, 8):
                                part = part + prod[o:o + 8]
                            k = CONV_K - 1 - s
                            dcw8[8 * k:8 * k + 8, c:c + 128] += part

        @pl.when(i == nt - 1)
        def _():
            for k in range(CONV_K):
                dcw_ref[k:k + 1, :] = jnp.sum(dcw8[8 * k:8 * k + 8, :], axis=0, keepdims=True)
            dcw_ref[CONV_K:32, :] = jnp.zeros((32 - CONV_K, wc), F32)

        _acc_out(i, dcb_ref, jnp.sum(du1_ref[...], axis=0, keepdims=True))
        g0 = du0[...]
        dp_ref[:, 0:wc] = (g0 * sg).astype(BF16)
        dp_ref[:, wc:2 * wc] = (g0 * a * sg * (1.0 - sg)).astype(BF16)
        dp_ref[:, 2 * wc:3 * wc] = dgat_ref[:, 0:wc]
        dp_ref[:, 4 * wc:5 * wc] = dgat_ref[:, wc:2 * wc]
        for g, win in enumerate(POOL_WINDOWS):
            cs = slice(g * POOL_GW, (g + 1) * POOL_GW)
            dext[0:tm, cs] = dd_ref[:, cs] * _inv_count(i * tm, tm, win)
            dext[tm:, cs] = keep_n * (ddn[:, cs] * _inv_count((i + 1) * tm, HALO, win))
            s = dext[pl.ds(0, tm), cs]
            for j in range(1, win):
                s = s + dext[pl.ds(j, tm), cs]
            dp_ref[:, 3 * wc + g * POOL_GW:3 * wc + (g + 1) * POOL_GW] = (s - dd_ref[:, cs]).astype(BF16)

    row = lambda w: pl.BlockSpec((tm, w), lambda i: (i, 0))
    return pl.pallas_call(
        body, name="even_bwd_b", grid=(nt,),
        in_specs=[_col(tm, wc, 0), _col(tm, wc, 1), _prev_halo(tm, HALO, wc, 0), _prev_halo(tm, HALO, wc, 1),
                  row(wc), _next_halo(tm, HALO, wc, 0, t), row(wc), _next_halo(tm, HALO, wc, 0, t), row(2 * wc),
                  _full((32, wc))],
        out_specs=[row(5 * wc), _full((32, wc)), _full((1, wc))],
        out_shape=[jax.ShapeDtypeStruct((t, 5 * wc), BF16), jax.ShapeDtypeStruct((32, wc), F32),
                   jax.ShapeDtypeStruct((1, wc), F32)],
        scratch_shapes=[pltpu.VMEM((tm + HALO, wc), F32), pltpu.VMEM((tm + HALO, wc), F32),
                        pltpu.VMEM((tm + HALO, wc), F32), pltpu.VMEM((tm, wc), F32), pltpu.VMEM((8 * 32, wc), F32)],
        compiler_params=_params("arbitrary"),
    )(p, p, p, p, du1, du1, dd, dd, dgat, cw)


def _softplus_neg(lam):
    z = -lam
    return jnp.maximum(z, 0.0) + jnp.log1p(jnp.exp(-jnp.abs(z)))


def _one_minus_exp(x):
    series = -x * (1.0 + x * (0.5 + x * (1.0 / 6.0 + x * (1.0 / 24.0))))
    return jnp.where(x > -0.02, series, 1.0 - jnp.exp(x))


def _odd_fwd(p, ccw, ccb, wrg, brg, wig, big, lam, *, tm):
    t = p.shape[0]
    wl = W_LRU
    ng = tm // 8

    def body(xr, gate, xrh, ccw_ref, ccb_ref, wrg_ref, brg_ref, wig_ref, big_ref, lam_ref,
             y_ref, xc_ref, r_ref, i_ref, hs_ref, xext, a_s, b_s, carry):
        i = pl.program_id(0)
        keep = (i > 0).astype(F32)
        xext[0:HALO_C, :] = keep * xrh[...]
        xext[HALO_C:, :] = xr[...]
        xc = jnp.broadcast_to(ccb_ref[...], (tm, wl))
        for k in range(LRU_CONV_K):
            xc = xc + ccw_ref[k:k + 1, :] * xext[pl.ds(HALO_C - (LRU_CONV_K - 1) + k, tm), :]
        xc_ref[...] = xc
        for h in range(LRU_HEADS):
            cs = slice(h * LRU_HD, (h + 1) * LRU_HD)
            xh = xc_ref[:, cs].astype(BF16)
            r_ref[:, cs] = _sigmoid(jnp.dot(xh, wrg_ref[h], preferred_element_type=F32) + brg_ref[:, cs])
            i_ref[:, cs] = _sigmoid(jnp.dot(xh, wig_ref[h], preferred_element_type=F32) + big_ref[:, cs])
        log_a = (-LRU_C * _softplus_neg(lam_ref[...])) * r_ref[...]
        a_s[...] = jnp.exp(log_a)
        b_s[...] = jnp.sqrt(_one_minus_exp(2.0 * log_a)) * (i_ref[...] * xc_ref[...])

        @pl.when(i == 0)
        def _():
            carry[...] = jnp.zeros_like(carry)

        rowi = lax.broadcasted_iota(jnp.int32, (8, wl), 0)

        def step(g, c):
            sl = pl.ds(pl.multiple_of(g * 8, 8), 8)
            aa, bb = a_s[sl, :], b_s[sl, :]
            for s in (1, 2, 4):
                m = rowi >= s
                a_sh = jnp.where(m, pltpu.roll(aa, s, 0), 1.0)
                b_sh = jnp.where(m, pltpu.roll(bb, s, 0), 0.0)
                bb = aa * b_sh + bb
                aa = aa * a_sh
            hv = bb + aa * c
            hs_ref[sl, :] = hv
            return hv[7:8, :]

        carry[...] = lax.fori_loop(0, ng, step, carry[...])
        gt = gate[...]
        y_ref[...] = (hs_ref[...] * (gt * _sigmoid(gt))).astype(BF16)

    row = lambda w: pl.BlockSpec((tm, w), lambda i: (i, 0))
    return pl.pallas_call(
        body, name="odd_fwd", grid=(t // tm,),
        in_specs=[_col(tm, wl, 0), _col(tm, wl, 1), _prev_halo(tm, HALO_C, wl, 0), _full((8, wl)), _full((1, wl)),
                  _full((LRU_HEADS, LRU_HD, LRU_HD)), _full((1, wl)), _full((LRU_HEADS, LRU_HD, LRU_HD)),
                  _full((1, wl)), _full((1, wl))],
        out_specs=[row(wl)] * 5,
        out_shape=[jax.ShapeDtypeStruct((t, wl), BF16)] + [jax.ShapeDtypeStruct((t, wl), F32)] * 4,
        scratch_shapes=[pltpu.VMEM((tm + HALO_C, wl), F32), pltpu.VMEM((tm, wl), F32), pltpu.VMEM((tm, wl), F32),
                        pltpu.VMEM((1, wl), F32)],
        compiler_params=_params("arbitrary"),
    )(p, p, p, ccw, ccb, wrg, brg, wig, big, lam)


def _odd_bwd_a(p, xc, r, ig, hs, dy, wrg, wig, lam, dep, *, tm):
    t = p.shape[0]
    wl = W_LRU
    nt = t // tm
    ng = tm // 8
    per = tm // HALO_C

    def body(gate, xc_ref, r_ref, i_ref, hs_ref, hsh, dy_ref, wrg_ref, wig_ref, lam_ref, dep_ref,
             dxc_ref, dgate_ref, dwrg_ref, dwig_ref, dbrg_ref, dbig_ref, dlam_ref,
             hext, a_s, q_s, g_s, dpr_s, dpi_s, carry):
        i = pl.program_id(0)
        ti = nt - 1 - i
        keep = (ti > 0).astype(F32)
        hext[0:HALO_C, :] = keep * hsh[...]
        hext[HALO_C:, :] = hs_ref[...]
        gt = gate[...]
        sg = _sigmoid(gt)
        dyv = dy_ref[...]
        dgate_ref[...] = (dyv * hs_ref[...] * _dsilu(gt, sg)).astype(BF16)
        q_s[...] = dyv * (gt * sg)
        sp = _softplus_neg(lam_ref[...])
        log_a = (-LRU_C * sp) * r_ref[...]
        a_s[...] = jnp.exp(log_a)

        @pl.when(i == 0)
        def _():
            carry[...] = jnp.zeros_like(carry)
            dwrg_ref[...] = jnp.zeros_like(dwrg_ref)
            dwig_ref[...] = jnp.zeros_like(dwig_ref)

        rowi = lax.broadcasted_iota(jnp.int32, (8, wl), 0)

        def step(gr, c):
            sl = pl.ds(pl.multiple_of((ng - 1 - gr) * 8, 8), 8)
            a0 = a_s[sl, :]
            al = jnp.where(rowi < 7, pltpu.roll(a0, 7, 0), 1.0)
            be = q_s[sl, :]
            for s in (1, 2, 4):
                m = rowi + s <= 7
                al_sh = jnp.where(m, pltpu.roll(al, 8 - s, 0), 1.0)
                be_sh = jnp.where(m, pltpu.roll(be, 8 - s, 0), 0.0)
                be = be + al * be_sh
                al = al * al_sh
            gv = be + al * c
            g_s[sl, :] = gv
            return (a0 * gv)[0:1, :]

        carry[...] = lax.fori_loop(0, ng, step, carry[...])

        gv = g_s[...]
        a = a_s[...]
        mult = jnp.sqrt(_one_minus_exp(2.0 * log_a))
        iv = i_ref[...]
        rv = r_ref[...]
        xcv = xc_ref[...]
        hprev = hext[pl.ds(HALO_C - 1, tm), :]
        dla = gv * hprev * a - (gv * iv * xcv) * (a * a) / mult
        di = gv * mult * xcv
        dpr = (dla * (-LRU_C * sp)) * rv * (1.0 - rv)
        dpi = di * iv * (1.0 - iv)
        dpr_s[...] = dpr
        dpi_s[...] = dpi
        dxc_ref[...] = gv * mult * iv
        dsp = jnp.sum(dla * rv, axis=0, keepdims=True) * (-LRU_C)
        _acc_out(i, dlam_ref, -dsp * _sigmoid(-lam_ref[...]))
        _acc_out(i, dbrg_ref, jnp.sum(dpr, axis=0, keepdims=True))
        _acc_out(i, dbig_ref, jnp.sum(dpi, axis=0, keepdims=True))
        for h in range(LRU_HEADS):
            cs = slice(h * LRU_HD, (h + 1) * LRU_HD)
            xh = xc_ref[:, cs].astype(BF16)
            dr_h = dpr_s[:, cs].astype(BF16)
            di_h = dpi_s[:, cs].astype(BF16)
            dxc_ref[:, cs] += (
                lax.dot_general(dr_h, wrg_ref[h], (((1,), (1,)), ((), ())), preferred_element_type=F32)
                + lax.dot_general(di_h, wig_ref[h], (((1,), (1,)), ((), ())), preferred_element_type=F32))
            dwrg_ref[h] += lax.dot_general(xh, dr_h, (((0,), (0,)), ((), ())), preferred_element_type=F32)
            dwig_ref[h] += lax.dot_general(xh, di_h, (((0,), (0,)), ((), ())), preferred_element_type=F32)

    rrow = lambda w: pl.BlockSpec((tm, w), lambda i: (nt - 1 - i, 0))
    hspec = pl.BlockSpec((HALO_C, wl), lambda i: (jnp.maximum((nt - 1 - i) * per - 1, 0), 0))
    wspec = _full((LRU_HEADS, LRU_HD, LRU_HD))
    return pl.pallas_call(
        body, name="odd_bwd_a", grid=(nt,),
        in_specs=[pl.BlockSpec((tm, wl), lambda i: (nt - 1 - i, 1)), rrow(wl), rrow(wl), rrow(wl), rrow(wl), hspec,
                  rrow(wl), wspec, wspec, _full((1, wl)), ANY],
        out_specs=[rrow(wl), rrow(wl), wspec, wspec, _full((1, wl)), _full((1, wl)), _full((1, wl))],
        out_shape=[jax.ShapeDtypeStruct((t, wl), F32), jax.ShapeDtypeStruct((t, wl), BF16),
                   jax.ShapeDtypeStruct((LRU_HEADS, LRU_HD, LRU_HD), F32),
                   jax.ShapeDtypeStruct((LRU_HEADS, LRU_HD, LRU_HD), F32)] + [jax.ShapeDtypeStruct((1, wl), F32)] * 3,
        scratch_shapes=[pltpu.VMEM((tm + HALO_C, wl), F32)] + [pltpu.VMEM((tm, wl), F32)] * 5
                       + [pltpu.VMEM((1, wl), F32)],
        compiler_params=_params("arbitrary"),
    )(p, xc, r, ig, hs, hs, dy, wrg, wig, lam, dep)


def _odd_bwd_b(p, dxc, dgate, ccw, *, tm):
    t = p.shape[0]
    wl = W_LRU
    nt = t // tm

    def body(xr, xrh, dxc_ref, dxcn, dgate_ref, ccw_ref, dp_ref, dcw_ref, dcb_ref, xext, gext):
        i = pl.program_id(0)

        @pl.when(i == 0)
        def _():
            dcw_ref[...] = jnp.zeros_like(dcw_ref)

        xext[0:HALO_C, :] = (i > 0).astype(F32) * xrh[...]
        xext[HALO_C:, :] = xr[...]
        gext[0:tm, :] = dxc_ref[...]
        gext[tm:, :] = (i < nt - 1).astype(F32) * dxcn[...]
        g = dxc_ref[...]
        acc = jnp.zeros((tm, wl), F32)
        for k in range(LRU_CONV_K):
            acc = acc + ccw_ref[k:k + 1, :] * gext[pl.ds(LRU_CONV_K - 1 - k, tm), :]
            dcw_ref[k:k + 1, :] += jnp.sum(
                g * xext[pl.ds(HALO_C - (LRU_CONV_K - 1) + k, tm), :], axis=0, keepdims=True)

        _acc_out(i, dcb_ref, jnp.sum(g, axis=0, keepdims=True))
        dp_ref[:, 0:wl] = acc.astype(BF16)
        dp_ref[:, wl:2 * wl] = dgate_ref[...]

    row = lambda w: pl.BlockSpec((tm, w), lambda i: (i, 0))
    return pl.pallas_call(
        body, name="odd_bwd_b", grid=(nt,),
        in_specs=[_col(tm, wl, 0), _prev_halo(tm, HALO_C, wl, 0), row(wl), _next_halo(tm, HALO_C, wl, 0, t), row(wl),
                  _full((8, wl))],
        out_specs=[row(2 * wl), _full((8, wl)), _full((1, wl))],
        out_shape=[jax.ShapeDtypeStruct((t, 2 * wl), BF16), jax.ShapeDtypeStruct((8, wl), F32),
                   jax.ShapeDtypeStruct((1, wl), F32)],
        scratch_shapes=[pltpu.VMEM((tm + HALO_C, wl), F32), pltpu.VMEM((tm + HALO_C, wl), F32)],
        compiler_params=_params("arbitrary"),
    )(p, p, dxc, dxc, dgate, ccw)


def _local_step(x, target, layer_weights, final_norm, on_grads):
    t = x.shape[0]
    tm, tx, tl, tw = min(TM_MATMUL, t), min(TM_MIXER, t), min(TM_STREAM, t), min(TM_WGRAD, t)
    h = x
    saved = []
    for layer in range(N_LAYERS):
        w = layer_weights(layer, h)
        p, hn = _norm_matmul(h, w["norm"], w["w_in"], w["dep"], tm=tl)
        if layer % 2 == 0:
            y, *acts = _even_fwd(p, w["conv_w"], w["conv_b"], w["ln_g"], w["ln_b"], w["pool_w"], w["pool_b"],
                                 w["pool_scale"], tm=tx)
        else:
            y, *acts = _odd_fwd(p, w["conv_w"], w["conv_b"], w["w_rg"], w["b_rg"], w["w_ig"], w["b_ig"], w["lam"],
                                tm=tx)
        w_out, dep = w["w_out"](y)
        saved.append((w, w_out, h, p, hn, y, acts))
        h = _out_proj(h, y, w_out, dep, tm=tl)
    loss, dh, d_final = _loss_head(h, final_norm, target, tm=tm)

    dep = d_final
    for layer in reversed(range(N_LAYERS)):
        w, w_out, h_in, p, hn, y, acts = saved[layer]
        sfx = "even" if layer % 2 == 0 else "odd"
        dy, dw_out = _out_proj_bwd(dh, y, w_out, dep, tm=tm)
        dep = on_grads(layer, {"w_out_" + sfx: dw_out}, dep, False)
        if layer % 2 == 0:
            u1, e, dmat = acts
            du1, dd, dgat, dlg, dlb, dpb, dps, dpw = _even_bwd_a(p, u1, e, dmat, dy, w["ln_g"], w["ln_b"], w["pool_w"],
                                                                 w["pool_scale"], dep, tm=tx)
            dp, dcw, dcb = _even_bwd_b(p, du1, dd, dgat, w["conv_w"], tm=tx)
            grads = dict(conv_a_w=dcw[:CONV_K], conv_a_b=dcb, ln_a_g=dlg, ln_a_b=dlb, pool_w=dpw, pool_b=dpb,
                         pool_scale=dps)
        else:
            xc, r, ig, hs = acts
            dxc, dgate, dwrg, dwig, dbrg, dbig, dlam = _odd_bwd_a(p, xc, r, ig, hs, dy, w["w_rg"], w["w_ig"],
                                                                  w["lam"], dep, tm=tx)
            dp, dccw, dccb = _odd_bwd_b(p, dxc, dgate, w["conv_w"], tm=tx)
            grads = dict(conv_c_w=dccw[:LRU_CONV_K], conv_c_b=dccb, w_rg=dwrg, b_rg=dbrg, w_ig=dwig, b_ig=dbig,
                         lru_lambda=dlam)
        grads["w_in_" + sfx] = _in_proj_bwd_w(hn, dp, N_DEV, tm=tw)
        dep = on_grads(layer, grads, dep, False)
        dh, dg = _in_proj_bwd_x(dp, w["w_in"], h_in, w["norm"], dh, dep, tm=tl)
        rest = {"norm_" + sfx: dg}
        if layer == N_LAYERS - 1:
            rest["final_norm"] = d_final
        dep = on_grads(layer, rest, dep, True)
    return loss, dh


def _slot(px, py, pc):
    return 4 * px + 2 * py + pc


def _peers(x, y, c):
    return [(1 - x if k & 4 else x, 1 - y if k & 2 else y, 1 - c if k & 1 else c) for k in range(1, N_DEV)]


def _all_gather(arrs, name):
    n = len(arrs)

    def body(*refs):
        ins, outs = refs[:n], refs[n:2 * n]
        send_sems, recv_sems, local_sems = refs[2 * n:]
        x, y, c = lax.axis_index("x"), lax.axis_index("y"), lax.axis_index("c")
        me, sibling = (x, y, c), (x, y, 1 - c)
        chips = [(1 - x, y), (x, 1 - y), (1 - x, 1 - y)]

        def copy(a, k, block, to, src=None):
            rows = outs[a].at[_slot(*block)]
            return pltpu.make_async_remote_copy(
                src_ref=rows if src is None else src, dst_ref=rows, send_sem=send_sems.at[a, k],
                recv_sem=recv_sems.at[a, k], device_id=to, device_id_type=MESH)

        mine = [pltpu.make_async_copy(ins[a], outs[a].at[_slot(*me)], local_sems.at[a]) for a in range(n)]
        for cp in mine:
            cp.start()
        first = []
        for a in range(n):
            first.append(copy(a, 0, me, sibling, src=ins[a]))
            first += [copy(a, 1 + j, me, (*chip, c), src=ins[a]) for j, chip in enumerate(chips)]
        for cp in first:
            cp.start()
        passed = []
        for j, chip in enumerate(chips):
            for a in range(n):
                copy(a, 1 + j, (*chip, c), me).wait_recv()
                fwd = copy(a, 4 + j, (*chip, c), sibling)
                fwd.start()
                passed.append(fwd)
        for a in range(n):
            copy(a, 0, sibling, me).wait_recv()
            for j, chip in enumerate(chips):
                copy(a, 4 + j, (*chip, 1 - c), me).wait_recv()
        for cp in first + passed:
            cp.wait_send()
        for cp in mine:
            cp.wait()

    return pl.pallas_call(
        body, name=name,
        in_specs=[ANY] * n, out_specs=[ANY] * n,
        out_shape=[jax.ShapeDtypeStruct((N_DEV,) + a.shape, a.dtype) for a in arrs],
        scratch_shapes=[pltpu.SemaphoreType.DMA((n, 7)), pltpu.SemaphoreType.DMA((n, 7)),
                        pltpu.SemaphoreType.DMA((n,))],
    )(*arrs)


N_COPIES = {"gather": N_PEERS, "scatter": N_PEERS, "chip_gather": 4, "forward": 3}


def _exchange_plan(mode, x, y, c):
    me = _slot(x, y, c)
    if mode == "forward":
        chips = [(1 - x, y), (x, 1 - y), (1 - x, 1 - y)]
        return [((x, y, 1 - c), ("land", _slot(*chip, c)), _slot(*chip, c), _slot(*chip, 1 - c)) for chip in chips]
    peers = _peers(x, y, c)
    if mode == "chip_gather":
        peers = [(x, y, 1 - c), (1 - x, y, c), (x, 1 - y, c), (1 - x, 1 - y, c)]
    return [(p, ("src", _slot(*p)) if mode == "scatter" else ("src", None), me, _slot(*p)) for p in peers]


def _exchange_copy(src_ref, land_ref, plan, send_sem, recv_sem, start):
    peer, (which, block), there, here = plan
    src = land_ref if which == "land" else src_ref
    return pltpu.make_async_remote_copy(
        src_ref=src if block is None else src.at[block], dst_ref=land_ref.at[there if start else here],
        send_sem=send_sem, recv_sem=recv_sem, device_id=peer, device_id_type=MESH)


def _exchange_start(groups, deps, name):
    flat = [pair for _, g in groups for pair in g]
    n, ng = len(flat), len(groups)

    def body(*refs):
        src_refs, land_refs = refs[:n], refs[n:2 * n]
        outs = refs[2 * n + len(deps):]
        sems, token = outs[:2 * ng], outs[2 * ng + 2 * n]
        x, y, c = lax.axis_index("x"), lax.axis_index("y"), lax.axis_index("c")
        base = 0
        for gi, (mode, g) in enumerate(groups):
            nc = N_COPIES[mode]
            for k, plan in enumerate(_exchange_plan(mode, x, y, c)):
                for ai in range(len(g)):
                    _exchange_copy(src_refs[base + ai], land_refs[base + ai], plan, sems[2 * gi].at[ai * nc + k],
                                   sems[2 * gi + 1].at[ai * nc + k], True).start()
            base += len(g)
        token[...] = jnp.zeros_like(token)

    operands = [pltpu.with_memory_space_constraint(a, pltpu.HBM) for a in
                [s for s, _ in flat] + [l for _, l in flat]]
    out_shape = []
    for mode, g in groups:
        out_shape += [pltpu.SemaphoreType.DMA((len(g) * N_COPIES[mode],))] * 2
    out_shape += [pltpu.HBM(a.shape, a.dtype) for a in operands]
    out_shape.append(jax.ShapeDtypeStruct((8, 128), F32))
    outs = pl.pallas_call(
        body, name=name, out_shape=out_shape,
        in_specs=[HBM] * (2 * n) + [ANY] * len(deps),
        out_specs=[SEM] * (2 * ng) + [HBM] * (2 * n) + [pl.BlockSpec(memory_space=pltpu.VMEM)],
        input_output_aliases={i: 2 * ng + i for i in range(2 * n)},
        compiler_params=pltpu.CompilerParams(has_side_effects=pltpu.SideEffectType.DATAFLOW_SIDE_EFFECTING),
    )(*operands, *deps)
    handles, base = [], 0
    for gi, (mode, g) in enumerate(groups):
        srcs = outs[2 * ng + base:2 * ng + base + len(g)]
        lands = outs[2 * ng + n + base:2 * ng + n + base + len(g)]
        handles.append((mode, outs[2 * gi], outs[2 * gi + 1], list(srcs), list(lands)))
        base += len(g)
    return handles, outs[-1]


def _exchange_wait(handle, after, name):
    mode, send_sems, recv_sems, srcs, lands = handle
    n = len(srcs)
    nc = N_COPIES[mode]

    def body(*refs):
        src_refs, land_refs = refs[:n], refs[n:2 * n]
        send_ref, recv_ref = refs[2 * n], refs[2 * n + 1]
        x, y, c = lax.axis_index("x"), lax.axis_index("y"), lax.axis_index("c")
        for k, plan in enumerate(_exchange_plan(mode, x, y, c)):
            for a in range(n):
                cp = _exchange_copy(src_refs[a], land_refs[a], plan, send_ref.at[a * nc + k], recv_ref.at[a * nc + k],
                                    False)
                cp.wait_send()
                cp.wait_recv()

    outs = pl.pallas_call(
        body, name=name,
        out_shape=[pltpu.HBM(a.shape, a.dtype) for a in srcs + lands],
        in_specs=[HBM] * (2 * n) + [SEM, SEM] + [ANY] * len(after),
        out_specs=[HBM] * (2 * n),
        input_output_aliases={i: i for i in range(2 * n)},
        compiler_params=pltpu.CompilerParams(has_side_effects=pltpu.SideEffectType.DATAFLOW_SIDE_EFFECTING),
    )(*srcs, *lands, send_sems, recv_sems, *after)
    return list(outs[n:])


def _adamw_math(w, g, m, v):
    c1 = 1.0 - ADAM_B1 ** ADAM_STEP
    c2 = 1.0 - ADAM_B2 ** ADAM_STEP
    nm = ADAM_B1 * m + (1.0 - ADAM_B1) * g
    nv = ADAM_B2 * v + (1.0 - ADAM_B2) * (g * g)
    delta = -ADAM_LR * ((nm / c1) / (jnp.sqrt(nv / c2) + ADAM_EPS) + ADAM_WD * w)
    return delta, nm, nv


def _row_tile(r):
    for cand in (256, 128, 64, 32, 16, 8):
        if r % cand == 0 and r > cand:
            return cand
    return r


def _adamw(items, layer0, bufs, name):
    ni = len(items)
    nl = items[0][1].shape[1]
    tiles = [_row_tile(w.shape[1]) for w, _, _, _ in items]
    steps = [w.shape[1] // tr for (w, _, _, _), tr in zip(items, tiles)]
    ns = steps[0]
    assert all(s == ns for s in steps)
    nb = 0 if bufs is None else 4 * ni

    def body(*refs):
        ins, outs = refs[:4 * ni], refs[4 * ni + nb:]
        for k in range(ni):
            w_ref, p_ref, m_ref, v_ref = ins[4 * k:4 * k + 4]
            g = p_ref[0, 0].astype(F32)
            for s in range(1, p_ref.shape[0]):
                g = g + p_ref[s, 0].astype(F32)
            delta, nm, nv = _adamw_math(w_ref[0], g, m_ref[0], v_ref[0])
            g_ref, d_ref, nm_ref, nv_ref = outs[4 * k:4 * k + 4]
            g_ref[0], d_ref[0], nm_ref[0], nv_ref[0] = g, delta, nm, nv

    in_specs, out_specs, out_shape, operands = [], [], [], []
    for (w, parts, m, v), tr in zip(items, tiles):
        blk = pl.BlockSpec((1, tr, w.shape[2]), lambda l, i: (layer0 + l, i, 0))
        in_specs += [blk, pl.BlockSpec((parts.shape[0], 1, tr, w.shape[2]), lambda l, i: (0, l, i, 0)), blk, blk]
        operands += [w, parts, m, v]
        out_specs += [blk] * 4
        out_shape += [jax.ShapeDtypeStruct(w.shape, F32)] * 4
    if bufs is not None:
        in_specs += [ANY] * nb
        operands += [b for item in bufs for b in item]
    outs = pl.pallas_call(
        body, name=name, grid=(nl, ns), in_specs=in_specs, out_specs=out_specs, out_shape=out_shape,
        input_output_aliases={4 * ni + i: i for i in range(nb)},
        compiler_params=_params("arbitrary", "arbitrary"),
    )(*operands)
    return [tuple(outs[4 * k:4 * k + 4]) for k in range(ni)]


NAMES = ("norm_even", "w_in_even", "conv_a_w", "conv_a_b", "ln_a_g", "ln_a_b", "pool_w", "pool_b", "pool_scale",
         "w_out_even", "norm_odd", "w_in_odd", "conv_c_w", "conv_c_b", "w_rg", "b_rg", "w_ig", "b_ig", "lru_lambda",
         "w_out_odd", "final_norm")
SMALL_GATHERED = ("conv_a_w", "pool_b", "norm_odd", "conv_c_w", "conv_c_b", "b_rg", "b_ig", "lru_lambda")
BIG = (("w_in_even", "w_out_even"), ("w_in_odd", "w_out_odd"))
SMALL = (("conv_a_w", "pool_b", "pool_w"), ("norm_odd", "conv_c_w", "conv_c_b", "b_rg", "b_ig", "lru_lambda"))
REPLICATED = (("norm_even", "conv_a_b", "ln_a_g", "ln_a_b", "pool_scale"), ("w_rg", "w_ig"))
PACK_ROW = 1024


def _pack_rows(flat2d):
    pad = (-flat2d.shape[1]) % PACK_ROW
    return jnp.pad(flat2d, ((0, 0), (0, pad))).reshape(flat2d.shape[0], -1, 128)


def _unpack(flat, shapes):
    out, off = [], 0
    for s in shapes:
        n = 1
        for d in s:
            n *= d
        out.append(flat[..., off:off + n].reshape(flat.shape[:-1] + tuple(s)))
        off += n
    return out


def _to_global(name, g):
    if name in ("conv_a_w", "pool_b", "conv_c_w"):
        return jnp.transpose(g, (1, 2, 0, 3)).reshape(g.shape[1], g.shape[2], -1)
    if name == "pool_w":
        return jnp.transpose(g, (1, 2, 0, 3, 4)).reshape(2, 4, POOL_GW, POOL_GW)
    return jnp.transpose(g, (1, 0, 2)).reshape(g.shape[1], -1)


def _to_blocks(name, g):
    if name == "conv_a_w":
        return jnp.transpose(g.reshape(CONV_K, N_DEV, -1), (1, 0, 2))
    if name == "conv_c_w":
        return jnp.transpose(g.reshape(LRU_CONV_K, N_DEV, -1), (1, 0, 2))
    if name == "pool_b":
        return jnp.transpose(g.reshape(4, N_DEV, -1), (1, 0, 2))
    if name == "pool_w":
        return jnp.transpose(g.reshape(4, N_DEV, POOL_GW // N_DEV, POOL_GW), (1, 0, 2, 3))
    return g.reshape(N_DEV, -1)


def _as3d(a):
    if a.ndim == 1:
        return a.reshape(1, 1, -1)
    if a.ndim == 2:
        return a.reshape(a.shape[0], 1, a.shape[1])
    return a.reshape(a.shape[0], -1, a.shape[-1])


def kernel(x, norm_even, w_in_even, conv_a_w, conv_a_b, ln_a_g, ln_a_b, pool_w, pool_b, pool_scale, w_out_even, norm_odd, w_in_odd, conv_c_w, conv_c_b, w_rg, b_rg, w_ig, b_ig, lru_lambda, w_out_odd, final_norm, loss_target, m_norm_even, m_w_in_even, m_conv_a_w, m_conv_a_b, m_ln_a_g, m_ln_a_b, m_pool_w, m_pool_b, m_pool_scale, m_w_out_even, m_norm_odd, m_w_in_odd, m_conv_c_w, m_conv_c_b, m_w_rg, m_b_rg, m_w_ig, m_b_ig, m_lru_lambda, m_w_out_odd, m_final_norm, v_norm_even, v_w_in_even, v_conv_a_w, v_conv_a_b, v_ln_a_g, v_ln_a_b, v_pool_w, v_pool_b, v_pool_scale, v_w_out_even, v_norm_odd, v_w_in_odd, v_conv_c_w, v_conv_c_b, v_w_rg, v_b_rg, v_w_ig, v_b_ig, v_lru_lambda, v_w_out_odd, v_final_norm):
    w_loc = dict(zip(NAMES, [norm_even, w_in_even, conv_a_w, conv_a_b, ln_a_g, ln_a_b, pool_w, pool_b, pool_scale,
                             w_out_even, norm_odd, w_in_odd, conv_c_w, conv_c_b, w_rg, b_rg, w_ig, b_ig, lru_lambda,
                             w_out_odd, final_norm]))
    m_loc = dict(zip(NAMES, [m_norm_even, m_w_in_even, m_conv_a_w, m_conv_a_b, m_ln_a_g, m_ln_a_b, m_pool_w, m_pool_b,
                             m_pool_scale, m_w_out_even, m_norm_odd, m_w_in_odd, m_conv_c_w, m_conv_c_b, m_w_rg,
                             m_b_rg, m_w_ig, m_b_ig, m_lru_lambda, m_w_out_odd, m_final_norm]))
    v_loc = dict(zip(NAMES, [v_norm_even, v_w_in_even, v_conv_a_w, v_conv_a_b, v_ln_a_g, v_ln_a_b, v_pool_w, v_pool_b,
                             v_pool_scale, v_w_out_even, v_norm_odd, v_w_in_odd, v_conv_c_w, v_conv_c_b, v_w_rg,
                             v_b_rg, v_w_ig, v_b_ig, v_lru_lambda, v_w_out_odd, v_final_norm]))
    me = _slot(lax.axis_index("x"), lax.axis_index("y"), lax.axis_index("c"))

    def landing(own):
        zone = lax.empty((N_DEV,) + own.shape[1:], own.dtype)
        return lax.dynamic_update_slice(zone, own, (me,) + (0,) * (own.ndim - 1))

    small_shapes = [w_loc[n].shape for n in SMALL_GATHERED]
    small = jnp.concatenate([w_loc[n].reshape(1, -1) for n in SMALL_GATHERED], axis=1)
    first = _all_gather([w_in_even[0].astype(BF16), pool_w.astype(BF16), _pack_rows(small)[0]], "gather_first")
    g_small = dict(zip(SMALL_GATHERED, [_to_global(n, g) for n, g in
                                        zip(SMALL_GATHERED, _unpack(first[2].reshape(N_DEV, -1), small_shapes))]))
    pool_w_all = _to_global("pool_w", first[1])

    def pairs_of(shards):
        return [(s.astype(BF16), landing(s.astype(BF16)[None])) for s in shards]

    shards = {1: [w_in_odd[0], w_out_odd[0]], 2: [w_in_even[1], w_out_even[1]], 3: [w_in_odd[1], w_out_odd[1]]}
    leg_a, leg_b = {}, {}
    (w_out_0, leg_a[1]), token_1 = _exchange_start(
        [("gather", pairs_of([w_out_even[0]])), ("chip_gather", pairs_of(shards[1]))], [first[0]], "gather_start_1")
    unused = jnp.zeros((8, 128), F32)

    def second_leg(layer, y):
        lands = _exchange_wait(leg_a[layer], [y], f"gather_wait_a_{layer}")
        (leg_b[layer],), token = _exchange_start([("forward", [(unused, l) for l in lands])], [],
                                                 f"gather_forward_{layer}")
        return token

    def layer_weights(layer, h):
        j = layer // 2
        dep = h
        if layer == 0:
            w_in, dep = first[0], token_1
            w_out_now = lambda y: _exchange_wait(w_out_0, [y], "gather_wait_out_0")[0]
        else:
            w_in, w_out_got = _exchange_wait(leg_b[layer], [h], f"gather_wait_b_{layer}")
            w_out_now = lambda y: w_out_got
            if layer + 1 in shards:
                (leg_a[layer + 1],), dep = _exchange_start([("chip_gather", pairs_of(shards[layer + 1]))], [w_in],
                                                           f"gather_start_{layer + 1}")

        def w_out(y):
            return w_out_now(y), (second_leg(layer + 1, y) if layer + 1 in shards else y)

        row = lambda a: a[j][None]
        if layer % 2 == 0:
            return dict(dep=dep, norm=row(norm_even), w_in=w_in,
                        w_out=lambda y: (lambda wo, d: (wo.reshape(W_EVEN_MIX, D_MODEL), d))(*w_out(y)),
                        conv_w=jnp.pad(g_small["conv_a_w"][j], ((0, 1), (0, 0))), conv_b=row(conv_a_b),
                        ln_g=row(ln_a_g), ln_b=row(ln_a_b), pool_w=pool_w_all[j],
                        pool_b=g_small["pool_b"][j].reshape(1, W_POOL), pool_scale=row(pool_scale))
        return dict(dep=dep, norm=row(g_small["norm_odd"]), w_in=w_in,
                    w_out=lambda y: (lambda wo, d: (wo.reshape(W_LRU, D_MODEL), d))(*w_out(y)),
                    conv_w=jnp.pad(g_small["conv_c_w"][j], ((0, 4), (0, 0))), conv_b=row(g_small["conv_c_b"]),
                    w_rg=w_rg[j].astype(BF16), b_rg=row(g_small["b_rg"]), w_ig=w_ig[j].astype(BF16),
                    b_ig=row(g_small["b_ig"]), lam=row(g_small["lru_lambda"]))

    pending, exchanges, small_layout = {}, {}, {}
    last_token = []

    def on_grads(layer, grads, dep, last):
        par = layer % 2
        w_in_name, w_out_name = BIG[par]
        have = pending.setdefault(layer, {})
        have.update(grads)
        eager = layer == 0
        scatter, gather = {}, {}
        if w_out_name in have and (eager or last):
            scatter["out"] = have.pop(w_out_name).reshape(N_DEV, -1, D_MODEL)
        if w_in_name in have and (eager or last):
            scatter["in"] = have.pop(w_in_name)
        if layer not in small_layout and all(n in have for n in SMALL[par]) and (eager or last):
            blocks = [_to_blocks(n, have[n]) for n in SMALL[par]]
            small_layout[layer] = [b.shape[1:] for b in blocks]
            scatter["small"] = _pack_rows(jnp.concatenate([b.reshape(N_DEV, -1) for b in blocks], axis=1))
        if last:
            vectors = [have[n] for n in REPLICATED[par]] if par == 0 else []
            if par == 1:
                gather["rep16"] = jnp.concatenate([have[n].reshape(-1) for n in REPLICATED[1]]).astype(BF16).reshape(-1, 128)
            if "final_norm" in have:
                vectors.append(have["final_norm"])
            if vectors:
                gather["rep32"] = jnp.concatenate([v.reshape(-1) for v in vectors]).reshape(-1, 128)
        groups = []
        if scatter:
            groups.append(("scatter", [(s, landing(lax.dynamic_slice_in_dim(s, me, 1, 0))) for s in scatter.values()]))
        if gather:
            groups.append(("gather", [(s, landing(s[None])) for s in gather.values()]))
        if not groups:
            return dep
        keys = [list(d) for d in (scatter, gather) if d]
        handles, token = _exchange_start(groups, [], f"grads_start_{layer}_{'_'.join(k for ks in keys for k in ks)}")
        exchanges.setdefault(layer, []).extend(zip(keys, handles))
        last_token[:] = [token]
        return token

    loss, grad_x = _local_step(x[0], loss_target[0], layer_weights, final_norm[None], on_grads)

    w3 = {n: _as3d(w_loc[n]) for n in NAMES}
    m3 = {n: _as3d(m_loc[n]) for n in NAMES}
    v3 = {n: _as3d(v_loc[n]) for n in NAMES}
    results = {}
    after = list(last_token)
    for layer in (3, 2, 1, 0):
        par, j = layer % 2, layer // 2
        got = {}
        for keys, handle in exchanges[layer]:
            got.update(zip(keys, _exchange_wait(handle, after, f"grads_wait_{layer}_{'_'.join(keys)}")))
        parts = {BIG[par][0]: got["in"], BIG[par][1]: got["out"]}
        parts.update(zip(SMALL[par], _unpack(got["small"].reshape(N_DEV, -1), small_layout[layer])))
        if par == 1:
            parts.update(zip(REPLICATED[1], _unpack(got["rep16"].reshape(N_DEV, -1),
                                                    [w_loc[n].shape[1:] for n in REPLICATED[1]])))
        else:
            parts.update(zip(REPLICATED[0], _unpack(got["rep32"].reshape(N_DEV, -1)[:, :len(REPLICATED[0]) * D_MODEL],
                                                    [w_loc[n].shape[1:] for n in REPLICATED[0]])))
        big_items, small_items = [], []
        for n, pt in parts.items():
            item = (w3[n], pt.reshape((N_DEV, 1) + w3[n].shape[1:]), m3[n], v3[n])
            (big_items if w3[n].shape[1] >= 128 else small_items).append((n, item))
        for n, item in big_items:
            results[n] = _adamw([item], j, [results[n]] if n in results else None, f"adamw_{n}_{layer}")[0]
        snames = [n for n, _ in small_items]
        prev = [results[n] for n in snames] if snames[0] in results else None
        for n, r in zip(snames, _adamw([it for _, it in small_items], j, prev, f"adamw_small_{layer}")):
            results[n] = r
        if layer == N_LAYERS - 1:
            item = (w3["final_norm"], got["rep32"].reshape(N_DEV, 1, 1, -1)[..., :D_MODEL], m3["final_norm"],
                    v3["final_norm"])
            results["final_norm"] = _adamw([item], 0, None, "adamw_final_norm")[0]
        after = [results[BIG[par][0]][1]]

    total = lax.psum(loss[0, 0], ("x", "y", "c"))
    outs = [[results[n][k].reshape(w_loc[n].shape) for n in NAMES] for k in range(4)]
    return (total, grad_x[None], *outs[0], *outs[1], *outs[2], *outs[3])
```

```python
import functools

import jax
import jax.numpy as jnp
from jax import lax
from jax.experimental import pallas as pl
from jax.experimental.pallas import tpu as pltpu

F32 = jnp.float32
BF16 = jnp.bfloat16

N_DEV = 8
N_PEERS = N_DEV - 1
N_LAYERS = 4
D_MODEL = 1024
EPS_RMS = 1e-6
EPS_LN = 1e-5
W_CONV = 1024
CONV_K = 31
W_POOL = 1024
POOL_WINDOWS = (2, 4, 8, 16)
POOL_GW = 256
W_EVEN_IN = 5120
W_EVEN_MIX = 2048
LRU_HEADS = 12
LRU_HD = 128
W_LRU = 1536
LRU_CONV_K = 4
LRU_C = 8.0
ADAM_LR = 0.001
ADAM_B1 = 0.9
ADAM_B2 = 0.999
ADAM_EPS = 1e-08
ADAM_WD = 0.01
ADAM_STEP = 10

HALO = 32
HALO_C = 8
TM_MATMUL = 512
TM_STREAM = 1024
TM_WGRAD = 2048
TM_MIXER = 256
CONV_ROWS = 128
VMEM_LIMIT = 56 * 1024 * 1024
MESH = pl.DeviceIdType.MESH
ANY = pl.BlockSpec(memory_space=pl.ANY)
HBM = pl.BlockSpec(memory_space=pltpu.HBM)
SEM = pl.BlockSpec(memory_space=pltpu.SEMAPHORE)


def _params(*sem):
    return pltpu.CompilerParams(dimension_semantics=sem, vmem_limit_bytes=VMEM_LIMIT)


def _sigmoid(z):
    return 0.5 * jnp.tanh(0.5 * z) + 0.5


def _dsilu(z, s):
    return s * (1.0 + z * (1.0 - s))


def _full(shape):
    nd = len(shape)
    return pl.BlockSpec(shape, lambda *_: (0,) * nd)


def _norm_matmul(h, g, w, dep, *, tm):
    t, d = h.shape
    nd, _, nb = w.shape

    def body(h_ref, g_ref, w_ref, dep_ref, p_ref, hn_ref):
        @pl.when(pl.program_id(1) == 0)
        def _():
            x = h_ref[...]
            r = lax.rsqrt(jnp.mean(x * x, axis=-1, keepdims=True) + EPS_RMS)
            hn_ref[...] = ((x * r) * g_ref[...]).astype(BF16)

        p_ref[...] = jnp.dot(hn_ref[...], w_ref[0], preferred_element_type=F32)

    return pl.pallas_call(
        body, name="norm_matmul", grid=(t // tm, nd),
        in_specs=[pl.BlockSpec((tm, d), lambda i, j: (i, 0)), _full((1, d)),
                  pl.BlockSpec((1, d, nb), lambda i, j: (j, 0, 0)), ANY],
        out_specs=[pl.BlockSpec((tm, nb), lambda i, j: (i, j)), pl.BlockSpec((tm, d), lambda i, j: (i, 0))],
        out_shape=[jax.ShapeDtypeStruct((t, nd * nb), F32), jax.ShapeDtypeStruct((t, d), BF16)],
        compiler_params=_params("arbitrary", "arbitrary"),
    )(h, g, w, dep)


def _out_proj(h, y, w, dep, *, tm):
    t, d = h.shape
    k = y.shape[1]

    def body(h_ref, y_ref, w_ref, dep_ref, o_ref):
        o_ref[...] = h_ref[...] + jnp.dot(y_ref[...], w_ref[...], preferred_element_type=F32)

    return pl.pallas_call(
        body, name="out_proj", grid=(t // tm,),
        in_specs=[pl.BlockSpec((tm, d), lambda i: (i, 0)), pl.BlockSpec((tm, k), lambda i: (i, 0)), _full((k, d)),
                  ANY],
        out_specs=pl.BlockSpec((tm, d), lambda i: (i, 0)),
        out_shape=jax.ShapeDtypeStruct((t, d), F32),
        compiler_params=_params("arbitrary"),
    )(h, y, w, dep)


def _out_proj_bwd(dh, y, w, dep, *, tm):
    t, d = dh.shape
    k = y.shape[1]
    nt = t // tm

    def body(dh_ref, y_ref, w_ref, dep_ref, dy_ref, dw_ref, acc):
        i = pl.program_id(0)
        g = dh_ref[...].astype(BF16)
        dy_ref[...] = lax.dot_general(g, w_ref[...], (((1,), (1,)), ((), ())), preferred_element_type=F32)
        part = lax.dot_general(y_ref[...], g, (((0,), (0,)), ((), ())), preferred_element_type=F32)

        @pl.when(i == 0)
        def _():
            acc[...] = part

        @pl.when(i > 0)
        def _():
            acc[...] += part

        @pl.when(i == nt - 1)
        def _():
            dw_ref[...] = acc[...].astype(BF16)

    return pl.pallas_call(
        body, name="out_proj_bwd", grid=(nt,),
        in_specs=[pl.BlockSpec((tm, d), lambda i: (i, 0)), pl.BlockSpec((tm, k), lambda i: (i, 0)), _full((k, d)),
                  ANY],
        out_specs=[pl.BlockSpec((tm, k), lambda i: (i, 0)), _full((k, d))],
        out_shape=[jax.ShapeDtypeStruct((t, k), F32), jax.ShapeDtypeStruct((k, d), BF16)],
        scratch_shapes=[pltpu.VMEM((k, d), F32)],
        compiler_params=_params("arbitrary"),
    )(dh, y, w, dep)


def _in_proj_bwd_x(dp, w, h, g, dh_out, dep, *, tm):
    t, d = h.shape
    nd, _, nb = w.shape
    nt = t // tm

    def body(dp_ref, w_ref, h_ref, g_ref, dho_ref, dep_ref, dh_ref, dg_ref, acc):
        i, j = pl.program_id(0), pl.program_id(1)
        part = lax.dot_general(dp_ref[...], w_ref[0], (((1,), (1,)), ((), ())), preferred_element_type=F32)

        @pl.when(j == 0)
        def _():
            acc[...] = part

        @pl.when(j > 0)
        def _():
            acc[...] += part

        @pl.when(j == nd - 1)
        def _():
            x = h_ref[...]
            r = lax.rsqrt(jnp.mean(x * x, axis=-1, keepdims=True) + EPS_RMS)
            dy = acc[...]
            gd = dy * g_ref[...]
            m = jnp.mean(gd * x, axis=-1, keepdims=True)
            dh_ref[...] = dho_ref[...] + r * gd - x * (r * r * r * m)
            dgp = jnp.sum(dy * x * r, axis=0, keepdims=True)

            @pl.when(i == 0)
            def _():
                dg_ref[...] = dgp

            @pl.when(i > 0)
            def _():
                dg_ref[...] += dgp

    return pl.pallas_call(
        body, name="in_proj_bwd_x", grid=(nt, nd),
        in_specs=[pl.BlockSpec((tm, nb), lambda i, j: (i, j)), pl.BlockSpec((1, d, nb), lambda i, j: (j, 0, 0)),
                  pl.BlockSpec((tm, d), lambda i, j: (i, 0)), _full((1, d)), pl.BlockSpec((tm, d), lambda i, j: (i, 0)),
                  ANY],
        out_specs=[pl.BlockSpec((tm, d), lambda i, j: (i, 0)), _full((1, d))],
        out_shape=[jax.ShapeDtypeStruct((t, d), F32), jax.ShapeDtypeStruct((1, d), F32)],
        scratch_shapes=[pltpu.VMEM((tm, d), F32)],
        compiler_params=_params("arbitrary", "arbitrary"),
    )(dp, w, h, g, dh_out, dep)


def _in_proj_bwd_w(hn, dp, nd, *, tm):
    t, d = hn.shape
    nb = dp.shape[1] // nd
    nt = t // tm

    def body(hn_ref, dp_ref, dw_ref, acc):
        i = pl.program_id(1)
        part = lax.dot_general(hn_ref[...], dp_ref[...], (((0,), (0,)), ((), ())), preferred_element_type=F32)

        @pl.when(i == 0)
        def _():
            acc[...] = part

        @pl.when(i > 0)
        def _():
            acc[...] += part

        @pl.when(i == nt - 1)
        def _():
            dw_ref[0] = acc[...].astype(BF16)

    return pl.pallas_call(
        body, name="in_proj_bwd_w", grid=(nd, nt),
        in_specs=[pl.BlockSpec((tm, d), lambda j, i: (i, 0)), pl.BlockSpec((tm, nb), lambda j, i: (i, j))],
        out_specs=pl.BlockSpec((1, d, nb), lambda j, i: (j, 0, 0)),
        out_shape=jax.ShapeDtypeStruct((nd, d, nb), BF16),
        scratch_shapes=[pltpu.VMEM((d, nb), F32)],
        compiler_params=_params("arbitrary", "arbitrary"),
    )(hn, dp)


def _loss_head(h, g, target, *, tm):
    t, d = h.shape
    nt = t // tm

    def body(h_ref, g_ref, t_ref, loss_ref, dh_ref, dg_ref):
        i = pl.program_id(0)
        x = h_ref[...]
        r = lax.rsqrt(jnp.mean(x * x, axis=-1, keepdims=True) + EPS_RMS)
        xr = x * r
        err = xr * g_ref[...] - t_ref[...]
        lp = 0.5 * jnp.sum(jnp.mean(err * err, axis=-1, keepdims=True), axis=0, keepdims=True)
        dy = err * (1.0 / d)
        gd = dy * g_ref[...]
        m = jnp.mean(gd * x, axis=-1, keepdims=True)
        dh_ref[...] = r * gd - x * (r * r * r * m)
        dgp = jnp.sum(dy * xr, axis=0, keepdims=True)

        @pl.when(i == 0)
        def _():
            loss_ref[...] = lp
            dg_ref[...] = dgp

        @pl.when(i > 0)
        def _():
            loss_ref[...] += lp
            dg_ref[...] += dgp

    return pl.pallas_call(
        body, name="loss_head", grid=(nt,),
        in_specs=[pl.BlockSpec((tm, d), lambda i: (i, 0)), _full((1, d)), pl.BlockSpec((tm, d), lambda i: (i, 0))],
        out_specs=[_full((1, 1)), pl.BlockSpec((tm, d), lambda i: (i, 0)), _full((1, d))],
        out_shape=[jax.ShapeDtypeStruct((1, 1), F32), jax.ShapeDtypeStruct((t, d), F32),
                   jax.ShapeDtypeStruct((1, d), F32)],
        compiler_params=_params("arbitrary"),
    )(h, g, target)


def _col(tm, w, c):
    return pl.BlockSpec((tm, w), lambda i: (i, c))


def _prev_halo(tm, rows, w, c):
    per = tm // rows
    return pl.BlockSpec((rows, w), lambda i: (jnp.maximum(i * per - 1, 0), c))


def _next_halo(tm, rows, w, c, t):
    per = tm // rows
    last = t // rows - 1
    return pl.BlockSpec((rows, w), lambda i: (jnp.minimum((i + 1) * per, last), c))


def _inv_count(first_row, rows, window):
    tpos = first_row + lax.broadcasted_iota(jnp.int32, (rows, 1), 0)
    return 1.0 / jnp.minimum(tpos + 1, window).astype(F32)


def _even_fwd(p, cw, cb, lg, lb, pw, pb, ps, *, tm):
    t = p.shape[0]
    wc = W_CONV

    def body(av, ag, agate, bv, bgate, avh, agh, bvh, cw_ref, cb_ref, lg_ref, lb_ref, pw_ref, pb_ref, ps_ref,
             y_ref, u1_ref, e_ref, d_ref, uext, vext):
        i = pl.program_id(0)
        keep = (i > 0).astype(F32)
        uext[0:HALO, :] = keep * (avh[...] * _sigmoid(agh[...]))
        uext[HALO:, :] = av[...] * _sigmoid(ag[...])
        vext[0:HALO, :] = keep * bvh[...]
        vext[HALO:, :] = bv[...]
        for c in range(0, wc, 128):
            for rb in range(0, tm, CONV_ROWS):
                acc = jnp.broadcast_to(cb_ref[:, c:c + 128], (CONV_ROWS, 128))
                for r in range(8):
                    shifted = uext[pl.ds(rb + 8 - r, CONV_ROWS + HALO - 8), c:c + 128]
                    for q in range(HALO // 8):
                        s = 8 * q + r
                        if s < CONV_K:
                            acc = acc + cw_ref[CONV_K - 1 - s:CONV_K - s, c:c + 128] * shifted[24 - 8 * q:24 - 8 * q + CONV_ROWS]
                u1_ref[rb:rb + CONV_ROWS, c:c + 128] = acc
        u1 = u1_ref[...]
        mu = jnp.mean(u1, axis=-1, keepdims=True)
        xc = u1 - mu
        rs = lax.rsqrt(jnp.mean(xc * xc, axis=-1, keepdims=True) + EPS_LN)
        u2 = (xc * rs) * lg_ref[...] + lb_ref[...]
        u3 = u2 * _sigmoid(u2)
        ga = agate[...]
        y_ref[:, 0:wc] = (u3 * (ga * _sigmoid(ga))).astype(BF16)
        for g, win in enumerate(POOL_WINDOWS):
            cs = slice(g * POOL_GW, (g + 1) * POOL_GW)
            s = vext[pl.ds(HALO, tm), cs]
            for j in range(1, win):
                s = s + vext[pl.ds(HALO - j, tm), cs]
            dg = s * _inv_count(i * tm, tm, win) - vext[pl.ds(HALO, tm), cs]
            dgb = dg.astype(BF16)
            d_ref[:, cs] = dgb
            eg = jnp.dot(dgb, pw_ref[g], preferred_element_type=F32) + pb_ref[:, cs]
            e_ref[:, cs] = eg
            gb = bgate[:, cs]
            y_ref[:, wc + g * POOL_GW:wc + (g + 1) * POOL_GW] = ((eg * ps_ref[:, cs]) * (gb * _sigmoid(gb))).astype(BF16)

    row = lambda w: pl.BlockSpec((tm, w), lambda i: (i, 0))
    return pl.pallas_call(
        body, name="even_fwd", grid=(t // tm,),
        in_specs=[_col(tm, wc, 0), _col(tm, wc, 1), _col(tm, wc, 2), _col(tm, wc, 3), _col(tm, wc, 4),
                  _prev_halo(tm, HALO, wc, 0), _prev_halo(tm, HALO, wc, 1), _prev_halo(tm, HALO, wc, 3),
                  _full((32, wc)), _full((1, wc)), _full((1, wc)), _full((1, wc)),
                  _full((4, POOL_GW, POOL_GW)), _full((1, wc)), _full((1, wc))],
        out_specs=[row(2 * wc), row(wc), row(wc), row(wc)],
        out_shape=[jax.ShapeDtypeStruct((t, 2 * wc), BF16), jax.ShapeDtypeStruct((t, wc), F32),
                   jax.ShapeDtypeStruct((t, wc), F32), jax.ShapeDtypeStruct((t, wc), BF16)],
        scratch_shapes=[pltpu.VMEM((tm + HALO, wc), F32), pltpu.VMEM((tm + HALO, wc), F32)],
        compiler_params=_params("arbitrary"),
    )(p, p, p, p, p, p, p, p, cw, cb, lg, lb, pw, pb, ps)


def _acc_out(i, ref, val):
    @pl.when(i == 0)
    def _():
        ref[...] = val

    @pl.when(i > 0)
    def _():
        ref[...] += val


def _even_bwd_a(p, u1, e, dmat, dy, lg, lb, pw, ps, dep, *, tm):
    t = p.shape[0]
    wc = W_CONV

    def body(agate, bgate, u1_ref, e_ref, d_ref, dya, dyb, lg_ref, lb_ref, pw_ref, ps_ref, dep_ref,
             du1_ref, dd_ref, dgat_ref, dlg_ref, dlb_ref, dpb_ref, dps_ref, dpw_ref):
        i = pl.program_id(0)

        @pl.when(i == 0)
        def _():
            dpw_ref[...] = jnp.zeros_like(dpw_ref)

        u1 = u1_ref[...]
        mu = jnp.mean(u1, axis=-1, keepdims=True)
        xc = u1 - mu
        rs = lax.rsqrt(jnp.mean(xc * xc, axis=-1, keepdims=True) + EPS_LN)
        xh = xc * rs
        u2 = xh * lg_ref[...] + lb_ref[...]
        s2 = _sigmoid(u2)
        ga = agate[...]
        sa = _sigmoid(ga)
        dy_a = dya[...]
        dgat_ref[:, 0:wc] = (dy_a * (u2 * s2) * _dsilu(ga, sa)).astype(BF16)
        du2 = dy_a * (ga * sa) * _dsilu(u2, s2)
        _acc_out(i, dlg_ref, jnp.sum(du2 * xh, axis=0, keepdims=True))
        _acc_out(i, dlb_ref, jnp.sum(du2, axis=0, keepdims=True))
        dxh = du2 * lg_ref[...]
        m1 = jnp.mean(dxh, axis=-1, keepdims=True)
        m2 = jnp.mean(dxh * xh, axis=-1, keepdims=True)
        du1_ref[...] = rs * (dxh - m1 - xh * m2)

        gb = bgate[...]
        sb = _sigmoid(gb)
        ev = e_ref[...]
        dy_b = dyb[...]
        dgat_ref[:, wc:2 * wc] = (dy_b * (ev * ps_ref[...]) * _dsilu(gb, sb)).astype(BF16)
        dz = dy_b * (gb * sb)
        _acc_out(i, dps_ref, jnp.sum(dz * ev, axis=0, keepdims=True))
        de = dz * ps_ref[...]
        _acc_out(i, dpb_ref, jnp.sum(de, axis=0, keepdims=True))
        for g in range(len(POOL_WINDOWS)):
            cs = slice(g * POOL_GW, (g + 1) * POOL_GW)
            deg = de[:, cs].astype(BF16)
            dd_ref[:, cs] = lax.dot_general(deg, pw_ref[g], (((1,), (1,)), ((), ())), preferred_element_type=F32)
            dpw_ref[g] += lax.dot_general(d_ref[:, cs], deg, (((0,), (0,)), ((), ())), preferred_element_type=F32)

    row = lambda w: pl.BlockSpec((tm, w), lambda i: (i, 0))
    return pl.pallas_call(
        body, name="even_bwd_a", grid=(t // tm,),
        in_specs=[_col(tm, wc, 2), _col(tm, wc, 4), row(wc), row(wc), row(wc), _col(tm, wc, 0), _col(tm, wc, 1),
                  _full((1, wc)), _full((1, wc)), _full((4, POOL_GW, POOL_GW)), _full((1, wc)), ANY],
        out_specs=[row(wc), row(wc), row(2 * wc), _full((1, wc)), _full((1, wc)), _full((1, wc)), _full((1, wc)),
                   _full((4, POOL_GW, POOL_GW))],
        out_shape=[jax.ShapeDtypeStruct((t, wc), F32), jax.ShapeDtypeStruct((t, wc), F32),
                   jax.ShapeDtypeStruct((t, 2 * wc), BF16)] + [jax.ShapeDtypeStruct((1, wc), F32)] * 4
                  + [jax.ShapeDtypeStruct((4, POOL_GW, POOL_GW), F32)],
        compiler_params=_params("arbitrary"),
    )(p, p, u1, e, dmat, dy, dy, lg, lb, pw, ps, dep)


def _even_bwd_b(p, du1, dd, dgat, cw, *, tm):
    t = p.shape[0]
    wc = W_CONV
    nt = t // tm

    def body(av, ag, avh, agh, du1_ref, du1n, dd_ref, ddn, dgat_ref, cw_ref, dp_ref, dcw_ref, dcb_ref,
             uext, gext, dext, du0, dcw8):
        i = pl.program_id(0)
        keep_p = (i > 0).astype(F32)
        keep_n = (i < nt - 1).astype(F32)

        @pl.when(i == 0)
        def _():
            dcw8[...] = jnp.zeros_like(dcw8)

        a = av[...]
        sg = _sigmoid(ag[...])
        uext[0:HALO, :] = keep_p * (avh[...] * _sigmoid(agh[...]))
        uext[HALO:, :] = a * sg
        gext[0:tm, :] = du1_ref[...]
        gext[tm:, :] = keep_n * du1n[...]
        for c in range(0, wc, 128):
            for rb in range(0, tm, CONV_ROWS):
                acc = jnp.zeros((CONV_ROWS, 128), F32)
                for r in range(8):
                    ahead = gext[pl.ds(rb + r, CONV_ROWS + HALO - 8), c:c + 128]
                    for q in range(HALO // 8):
                        s = 8 * q + r
                        if s < CONV_K:
                            acc = acc + cw_ref[CONV_K - 1 - s:CONV_K - s, c:c + 128] * ahead[8 * q:8 * q + CONV_ROWS]
                du0[rb:rb + CONV_ROWS, c:c + 128] = acc
                gcur = du1_ref[rb:rb + CONV_ROWS, c:c + 128]
                for r in range(8):
                    behind = uext[pl.ds(rb + 8 - r, CONV_ROWS + HALO - 8), c:c + 128]
                    for q in range(HALO // 8):
                        s = 8 * q + r
                        if s < CONV_K:
                            prod = gcur * behind[24 - 8 * q:24 - 8 * q + CONV_ROWS]
                            part = prod[0:8]
                            for o in range(8, CONV_ROWS, 8):
                                part = part + prod[o:o + 8]
                            k = CONV_K - 1 - s
                            dcw8[8 * k:8 * k + 8, c:c + 128] += part

        @pl.when(i == nt - 1)
        def _():
            for k in range(CONV_K):
                dcw_ref[k:k + 1, :] = jnp.sum(dcw8[8 * k:8 * k + 8, :], axis=0, keepdims=True)
            dcw_ref[CONV_K:32, :] = jnp.zeros((32 - CONV_K, wc), F32)

        _acc_out(i, dcb_ref, jnp.sum(du1_ref[...], axis=0, keepdims=True))
        g0 = du0[...]
        dp_ref[:, 0:wc] = (g0 * sg).astype(BF16)
        dp_ref[:, wc:2 * wc] = (g0 * a * sg * (1.0 - sg)).astype(BF16)
        dp_ref[:, 2 * wc:3 * wc] = dgat_ref[:, 0:wc]
        dp_ref[:, 4 * wc:5 * wc] = dgat_ref[:, wc:2 * wc]
        for g, win in enumerate(POOL_WINDOWS):
            cs = slice(g * POOL_GW, (g + 1) * POOL_GW)
            dext[0:tm, cs] = dd_ref[:, cs] * _inv_count(i * tm, tm, win)
            dext[tm:, cs] = keep_n * (ddn[:, cs] * _inv_count((i + 1) * tm, HALO, win))
            s = dext[pl.ds(0, tm), cs]
            for j in range(1, win):
                s = s + dext[pl.ds(j, tm), cs]
            dp_ref[:, 3 * wc + g * POOL_GW:3 * wc + (g + 1) * POOL_GW] = (s - dd_ref[:, cs]).astype(BF16)

    row = lambda w: pl.BlockSpec((tm, w), lambda i: (i, 0))
    return pl.pallas_call(
        body, name="even_bwd_b", grid=(nt,),
        in_specs=[_col(tm, wc, 0), _col(tm, wc, 1), _prev_halo(tm, HALO, wc, 0), _prev_halo(tm, HALO, wc, 1),
                  row(wc), _next_halo(tm, HALO, wc, 0, t), row(wc), _next_halo(tm, HALO, wc, 0, t), row(2 * wc),
                  _full((32, wc))],
        out_specs=[row(5 * wc), _full((32, wc)), _full((1, wc))],
        out_shape=[jax.ShapeDtypeStruct((t, 5 * wc), BF16), jax.ShapeDtypeStruct((32, wc), F32),
                   jax.ShapeDtypeStruct((1, wc), F32)],
        scratch_shapes=[pltpu.VMEM((tm + HALO, wc), F32), pltpu.VMEM((tm + HALO, wc), F32),
                        pltpu.VMEM((tm + HALO, wc), F32), pltpu.VMEM((tm, wc), F32), pltpu.VMEM((8 * 32, wc), F32)],
        compiler_params=_params("arbitrary"),
    )(p, p, p, p, du1, du1, dd, dd, dgat, cw)


def _softplus_neg(lam):
    z = -lam
    return jnp.maximum(z, 0.0) + jnp.log1p(jnp.exp(-jnp.abs(z)))


def _one_minus_exp(x):
    series = -x * (1.0 + x * (0.5 + x * (1.0 / 6.0 + x * (1.0 / 24.0))))
    return jnp.where(x > -0.02, series, 1.0 - jnp.exp(x))


def _odd_fwd(p, ccw, ccb, wrg, brg, wig, big, lam, *, tm):
    t = p.shape[0]
    wl = W_LRU
    ng = tm // 8

    def body(xr, gate, xrh, ccw_ref, ccb_ref, wrg_ref, brg_ref, wig_ref, big_ref, lam_ref,
             y_ref, xc_ref, r_ref, i_ref, hs_ref, xext, a_s, b_s, carry):
        i = pl.program_id(0)
        keep = (i > 0).astype(F32)
        xext[0:HALO_C, :] = keep * xrh[...]
        xext[HALO_C:, :] = xr[...]
        xc = jnp.broadcast_to(ccb_ref[...], (tm, wl))
        for k in range(LRU_CONV_K):
            xc = xc + ccw_ref[k:k + 1, :] * xext[pl.ds(HALO_C - (LRU_CONV_K - 1) + k, tm), :]
        xc_ref[...] = xc
        for h in range(LRU_HEADS):
            cs = slice(h * LRU_HD, (h + 1) * LRU_HD)
            xh = xc_ref[:, cs].astype(BF16)
            r_ref[:, cs] = _sigmoid(jnp.dot(xh, wrg_ref[h], preferred_element_type=F32) + brg_ref[:, cs])
            i_ref[:, cs] = _sigmoid(jnp.dot(xh, wig_ref[h], preferred_element_type=F32) + big_ref[:, cs])
        log_a = (-LRU_C * _softplus_neg(lam_ref[...])) * r_ref[...]
        a_s[...] = jnp.exp(log_a)
        b_s[...] = jnp.sqrt(_one_minus_exp(2.0 * log_a)) * (i_ref[...] * xc_ref[...])

        @pl.when(i == 0)
        def _():
            carry[...] = jnp.zeros_like(carry)

        rowi = lax.broadcasted_iota(jnp.int32, (8, wl), 0)

        def step(g, c):
            sl = pl.ds(pl.multiple_of(g * 8, 8), 8)
            aa, bb = a_s[sl, :], b_s[sl, :]
            for s in (1, 2, 4):
                m = rowi >= s
                a_sh = jnp.where(m, pltpu.roll(aa, s, 0), 1.0)
                b_sh = jnp.where(m, pltpu.roll(bb, s, 0), 0.0)
                bb = aa * b_sh + bb
                aa = aa * a_sh
            hv = bb + aa * c
            hs_ref[sl, :] = hv
            return hv[7:8, :]

        carry[...] = lax.fori_loop(0, ng, step, carry[...])
        gt = gate[...]
        y_ref[...] = (hs_ref[...] * (gt * _sigmoid(gt))).astype(BF16)

    row = lambda w: pl.BlockSpec((tm, w), lambda i: (i, 0))
    return pl.pallas_call(
        body, name="odd_fwd", grid=(t // tm,),
        in_specs=[_col(tm, wl, 0), _col(tm, wl, 1), _prev_halo(tm, HALO_C, wl, 0), _full((8, wl)), _full((1, wl)),
                  _full((LRU_HEADS, LRU_HD, LRU_HD)), _full((1, wl)), _full((LRU_HEADS, LRU_HD, LRU_HD)),
                  _full((1, wl)), _full((1, wl))],
        out_specs=[row(wl)] * 5,
        out_shape=[jax.ShapeDtypeStruct((t, wl), BF16)] + [jax.ShapeDtypeStruct((t, wl), F32)] * 4,
        scratch_shapes=[pltpu.VMEM((tm + HALO_C, wl), F32), pltpu.VMEM((tm, wl), F32), pltpu.VMEM((tm, wl), F32),
                        pltpu.VMEM((1, wl), F32)],
        compiler_params=_params("arbitrary"),
    )(p, p, p, ccw, ccb, wrg, brg, wig, big, lam)


def _odd_bwd_a(p, xc, r, ig, hs, dy, wrg, wig, lam, dep, *, tm):
    t = p.shape[0]
    wl = W_LRU
    nt = t // tm
    ng = tm // 8
    per = tm // HALO_C

    def body(gate, xc_ref, r_ref, i_ref, hs_ref, hsh, dy_ref, wrg_ref, wig_ref, lam_ref, dep_ref,
             dxc_ref, dgate_ref, dwrg_ref, dwig_ref, dbrg_ref, dbig_ref, dlam_ref,
             hext, a_s, q_s, g_s, dpr_s, dpi_s, carry):
        i = pl.program_id(0)
        ti = nt - 1 - i
        keep = (ti > 0).astype(F32)
        hext[0:HALO_C, :] = keep * hsh[...]
        hext[HALO_C:, :] = hs_ref[...]
        gt = gate[...]
        sg = _sigmoid(gt)
        dyv = dy_ref[...]
        dgate_ref[...] = (dyv * hs_ref[...] * _dsilu(gt, sg)).astype(BF16)
        q_s[...] = dyv * (gt * sg)
        sp = _softplus_neg(lam_ref[...])
        log_a = (-LRU_C * sp) * r_ref[...]
        a_s[...] = jnp.exp(log_a)

        @pl.when(i == 0)
        def _():
            carry[...] = jnp.zeros_like(carry)
            dwrg_ref[...] = jnp.zeros_like(dwrg_ref)
            dwig_ref[...] = jnp.zeros_like(dwig_ref)

        rowi = lax.broadcasted_iota(jnp.int32, (8, wl), 0)

        def step(gr, c):
            sl = pl.ds(pl.multiple_of((ng - 1 - gr) * 8, 8), 8)
            a0 = a_s[sl, :]
            al = jnp.where(rowi < 7, pltpu.roll(a0, 7, 0), 1.0)
            be = q_s[sl, :]
            for s in (1, 2, 4):
                m = rowi + s <= 7
                al_sh = jnp.where(m, pltpu.roll(al, 8 - s, 0), 1.0)
                be_sh = jnp.where(m, pltpu.roll(be, 8 - s, 0), 0.0)
                be = be + al * be_sh
                al = al * al_sh
            gv = be + al * c
            g_s[sl, :] = gv
            return (a0 * gv)[0:1, :]

        carry[...] = lax.fori_loop(0, ng, step, carry[...])

        gv = g_s[...]
        a = a_s[...]
        mult = jnp.sqrt(_one_minus_exp(2.0 * log_a))
        iv = i_ref[...]
        rv = r_ref[...]
        xcv = xc_ref[...]
        hprev = hext[pl.ds(HALO_C - 1, tm), :]
        dla = gv * hprev * a - (gv * iv * xcv) * (a * a) / mult
        di = gv * mult * xcv
        dpr = (dla * (-LRU_C * sp)) * rv * (1.0 - rv)
        dpi = di * iv * (1.0 - iv)
        dpr_s[...] = dpr
        dpi_s[...] = dpi
        dxc_ref[...] = gv * mult * iv
        dsp = jnp.sum(dla * rv, axis=0, keepdims=True) * (-LRU_C)
        _acc_out(i, dlam_ref, -dsp * jax.nn.sigmoid(-lam_ref[...]))
        _acc_out(i, dbrg_ref, jnp.sum(dpr, axis=0, keepdims=True))
        _acc_out(i, dbig_ref, jnp.sum(dpi, axis=0, keepdims=True))
        for h in range(LRU_HEADS):
            cs = slice(h * LRU_HD, (h + 1) * LRU_HD)
            xh = xc_ref[:, cs].astype(BF16)
            dr_h = dpr_s[:, cs].astype(BF16)
            di_h = dpi_s[:, cs].astype(BF16)
            dxc_ref[:, cs] += (
                lax.dot_general(dr_h, wrg_ref[h], (((1,), (1,)), ((), ())), preferred_element_type=F32)
                + lax.dot_general(di_h, wig_ref[h], (((1,), (1,)), ((), ())), preferred_element_type=F32))
            dwrg_ref[h] += lax.dot_general(xh, dr_h, (((0,), (0,)), ((), ())), preferred_element_type=F32)
            dwig_ref[h] += lax.dot_general(xh, di_h, (((0,), (0,)), ((), ())), preferred_element_type=F32)

    rrow = lambda w: pl.BlockSpec((tm, w), lambda i: (nt - 1 - i, 0))
    hspec = pl.BlockSpec((HALO_C, wl), lambda i: (jnp.maximum((nt - 1 - i) * per - 1, 0), 0))
    wspec = _full((LRU_HEADS, LRU_HD, LRU_HD))
    return pl.pallas_call(
        body, name="odd_bwd_a", grid=(nt,),
        in_specs=[pl.BlockSpec((tm, wl), lambda i: (nt - 1 - i, 1)), rrow(wl), rrow(wl), rrow(wl), rrow(wl), hspec,
                  rrow(wl), wspec, wspec, _full((1, wl)), ANY],
        out_specs=[rrow(wl), rrow(wl), wspec, wspec, _full((1, wl)), _full((1, wl)), _full((1, wl))],
        out_shape=[jax.ShapeDtypeStruct((t, wl), F32), jax.ShapeDtypeStruct((t, wl), BF16),
                   jax.ShapeDtypeStruct((LRU_HEADS, LRU_HD, LRU_HD), F32),
                   jax.ShapeDtypeStruct((LRU_HEADS, LRU_HD, LRU_HD), F32)] + [jax.ShapeDtypeStruct((1, wl), F32)] * 3,
        scratch_shapes=[pltpu.VMEM((tm + HALO_C, wl), F32)] + [pltpu.VMEM((tm, wl), F32)] * 5
                       + [pltpu.VMEM((1, wl), F32)],
        compiler_params=_params("arbitrary"),
    )(p, xc, r, ig, hs, hs, dy, wrg, wig, lam, dep)


def _odd_bwd_b(p, dxc, dgate, ccw, *, tm):
    t = p.shape[0]
    wl = W_LRU
    nt = t // tm

    def body(xr, xrh, dxc_ref, dxcn, dgate_ref, ccw_ref, dp_ref, dcw_ref, dcb_ref, xext, gext):
        i = pl.program_id(0)

        @pl.when(i == 0)
        def _():
            dcw_ref[...] = jnp.zeros_like(dcw_ref)

        xext[0:HALO_C, :] = (i > 0).astype(F32) * xrh[...]
        xext[HALO_C:, :] = xr[...]
        gext[0:tm, :] = dxc_ref[...]
        gext[tm:, :] = (i < nt - 1).astype(F32) * dxcn[...]
        g = dxc_ref[...]
        acc = jnp.zeros((tm, wl), F32)
        for k in range(LRU_CONV_K):
            acc = acc + ccw_ref[k:k + 1, :] * gext[pl.ds(LRU_CONV_K - 1 - k, tm), :]
            dcw_ref[k:k + 1, :] += jnp.sum(
                g * xext[pl.ds(HALO_C - (LRU_CONV_K - 1) + k, tm), :], axis=0, keepdims=True)

        _acc_out(i, dcb_ref, jnp.sum(g, axis=0, keepdims=True))
        dp_ref[:, 0:wl] = acc.astype(BF16)
        dp_ref[:, wl:2 * wl] = dgate_ref[...]

    row = lambda w: pl.BlockSpec((tm, w), lambda i: (i, 0))
    return pl.pallas_call(
        body, name="odd_bwd_b", grid=(nt,),
        in_specs=[_col(tm, wl, 0), _prev_halo(tm, HALO_C, wl, 0), row(wl), _next_halo(tm, HALO_C, wl, 0, t), row(wl),
                  _full((8, wl))],
        out_specs=[row(2 * wl), _full((8, wl)), _full((1, wl))],
        out_shape=[jax.ShapeDtypeStruct((t, 2 * wl), BF16), jax.ShapeDtypeStruct((8, wl), F32),
                   jax.ShapeDtypeStruct((1, wl), F32)],
        scratch_shapes=[pltpu.VMEM((tm + HALO_C, wl), F32), pltpu.VMEM((tm + HALO_C, wl), F32)],
        compiler_params=_params("arbitrary"),
    )(p, p, dxc, dxc, dgate, ccw)


def _local_step(x, target, layer_weights, final_norm, on_grads):
    t = x.shape[0]
    tm, tx, tl, tw = min(TM_MATMUL, t), min(TM_MIXER, t), min(TM_STREAM, t), min(TM_WGRAD, t)
    h = x
    saved = []
    for layer in range(N_LAYERS):
        w = layer_weights(layer, h)
        p, hn = _norm_matmul(h, w["norm"], w["w_in"], w["dep"], tm=tl)
        if layer % 2 == 0:
            y, *acts = _even_fwd(p, w["conv_w"], w["conv_b"], w["ln_g"], w["ln_b"], w["pool_w"], w["pool_b"],
                                 w["pool_scale"], tm=tx)
        else:
            y, *acts = _odd_fwd(p, w["conv_w"], w["conv_b"], w["w_rg"], w["b_rg"], w["w_ig"], w["b_ig"], w["lam"],
                                tm=tx)
        w_out, dep = w["w_out"](y)
        saved.append((w, w_out, h, p, hn, y, acts))
        h = _out_proj(h, y, w_out, dep, tm=tl)
    loss, dh, d_final = _loss_head(h, final_norm, target, tm=tm)

    dep = d_final
    for layer in reversed(range(N_LAYERS)):
        w, w_out, h_in, p, hn, y, acts = saved[layer]
        sfx = "even" if layer % 2 == 0 else "odd"
        dy, dw_out = _out_proj_bwd(dh, y, w_out, dep, tm=tm)
        dep = on_grads(layer, {"w_out_" + sfx: dw_out}, dep, False)
        if layer % 2 == 0:
            u1, e, dmat = acts
            du1, dd, dgat, dlg, dlb, dpb, dps, dpw = _even_bwd_a(p, u1, e, dmat, dy, w["ln_g"], w["ln_b"], w["pool_w"],
                                                                 w["pool_scale"], dep, tm=tx)
            dp, dcw, dcb = _even_bwd_b(p, du1, dd, dgat, w["conv_w"], tm=tx)
            grads = dict(conv_a_w=dcw[:CONV_K], conv_a_b=dcb, ln_a_g=dlg, ln_a_b=dlb, pool_w=dpw, pool_b=dpb,
                         pool_scale=dps)
        else:
            xc, r, ig, hs = acts
            dxc, dgate, dwrg, dwig, dbrg, dbig, dlam = _odd_bwd_a(p, xc, r, ig, hs, dy, w["w_rg"], w["w_ig"],
                                                                  w["lam"], dep, tm=tx)
            dp, dccw, dccb = _odd_bwd_b(p, dxc, dgate, w["conv_w"], tm=tx)
            grads = dict(conv_c_w=dccw[:LRU_CONV_K], conv_c_b=dccb, w_rg=dwrg, b_rg=dbrg, w_ig=dwig, b_ig=dbig,
                         lru_lambda=dlam)
        grads["w_in_" + sfx] = _in_proj_bwd_w(hn, dp, N_DEV, tm=tw)
        dep = on_grads(layer, grads, dep, False)
        dh, dg = _in_proj_bwd_x(dp, w["w_in"], h_in, w["norm"], dh, dep, tm=tl)
        rest = {"norm_" + sfx: dg}
        if layer == N_LAYERS - 1:
            rest["final_norm"] = d_final
        dep = on_grads(layer, rest, dep, True)
    return loss, dh


def _slot(px, py, pc):
    return 4 * px + 2 * py + pc


def _peers(x, y, c):
    return [(1 - x if k & 4 else x, 1 - y if k & 2 else y, 1 - c if k & 1 else c) for k in range(1, N_DEV)]


def _all_gather(arrs, name):
    n = len(arrs)

    def body(*refs):
        ins, outs = refs[:n], refs[n:2 * n]
        send_sems, recv_sems, local_sems = refs[2 * n:]
        x, y, c = lax.axis_index("x"), lax.axis_index("y"), lax.axis_index("c")
        me, sibling = (x, y, c), (x, y, 1 - c)
        chips = [(1 - x, y), (x, 1 - y), (1 - x, 1 - y)]

        def copy(a, k, block, to, src=None):
            rows = outs[a].at[_slot(*block)]
            return pltpu.make_async_remote_copy(
                src_ref=rows if src is None else src, dst_ref=rows, send_sem=send_sems.at[a, k],
                recv_sem=recv_sems.at[a, k], device_id=to, device_id_type=MESH)

        mine = [pltpu.make_async_copy(ins[a], outs[a].at[_slot(*me)], local_sems.at[a]) for a in range(n)]
        for cp in mine:
            cp.start()
        first = []
        for a in range(n):
            first.append(copy(a, 0, me, sibling, src=ins[a]))
            first += [copy(a, 1 + j, me, (*chip, c), src=ins[a]) for j, chip in enumerate(chips)]
        for cp in first:
            cp.start()
        passed = []
        for j, chip in enumerate(chips):
            for a in range(n):
                copy(a, 1 + j, (*chip, c), me).wait_recv()
                fwd = copy(a, 4 + j, (*chip, c), sibling)
                fwd.start()
                passed.append(fwd)
        for a in range(n):
            copy(a, 0, sibling, me).wait_recv()
            for j, chip in enumerate(chips):
                copy(a, 4 + j, (*chip, 1 - c), me).wait_recv()
        for cp in first + passed:
            cp.wait_send()
        for cp in mine:
            cp.wait()

    return pl.pallas_call(
        body, name=name,
        in_specs=[ANY] * n, out_specs=[ANY] * n,
        out_shape=[jax.ShapeDtypeStruct((N_DEV,) + a.shape, a.dtype) for a in arrs],
        scratch_shapes=[pltpu.SemaphoreType.DMA((n, 7)), pltpu.SemaphoreType.DMA((n, 7)),
                        pltpu.SemaphoreType.DMA((n,))],
    )(*arrs)


N_COPIES = {"gather": N_PEERS, "scatter": N_PEERS, "chip_gather": 4, "forward": 3}


def _exchange_plan(mode, x, y, c):
    me = _slot(x, y, c)
    if mode == "forward":
        chips = [(1 - x, y), (x, 1 - y), (1 - x, 1 - y)]
        return [((x, y, 1 - c), ("land", _slot(*chip, c)), _slot(*chip, c), _slot(*chip, 1 - c)) for chip in chips]
    peers = _peers(x, y, c)
    if mode == "chip_gather":
        peers = [(x, y, 1 - c), (1 - x, y, c), (x, 1 - y, c), (1 - x, 1 - y, c)]
    return [(p, ("src", _slot(*p)) if mode == "scatter" else ("src", None), me, _slot(*p)) for p in peers]


def _exchange_copy(src_ref, land_ref, plan, send_sem, recv_sem, start):
    peer, (which, block), there, here = plan
    src = land_ref if which == "land" else src_ref
    return pltpu.make_async_remote_copy(
        src_ref=src if block is None else src.at[block], dst_ref=land_ref.at[there if start else here],
        send_sem=send_sem, recv_sem=recv_sem, device_id=peer, device_id_type=MESH)


def _exchange_start(groups, deps, name):
    flat = [pair for _, g in groups for pair in g]
    n, ng = len(flat), len(groups)

    def body(*refs):
        src_refs, land_refs = refs[:n], refs[n:2 * n]
        outs = refs[2 * n + len(deps):]
        sems, token = outs[:2 * ng], outs[2 * ng + 2 * n]
        x, y, c = lax.axis_index("x"), lax.axis_index("y"), lax.axis_index("c")
        base = 0
        for gi, (mode, g) in enumerate(groups):
            nc = N_COPIES[mode]
            for k, plan in enumerate(_exchange_plan(mode, x, y, c)):
                for ai in range(len(g)):
                    _exchange_copy(src_refs[base + ai], land_refs[base + ai], plan, sems[2 * gi].at[ai * nc + k],
                                   sems[2 * gi + 1].at[ai * nc + k], True).start()
            base += len(g)
        token[...] = jnp.zeros_like(token)

    operands = [pltpu.with_memory_space_constraint(a, pltpu.HBM) for a in
                [s for s, _ in flat] + [l for _, l in flat]]
    out_shape = []
    for mode, g in groups:
        out_shape += [pltpu.SemaphoreType.DMA((len(g) * N_COPIES[mode],))] * 2
    out_shape += [pltpu.HBM(a.shape, a.dtype) for a in operands]
    out_shape.append(jax.ShapeDtypeStruct((8, 128), F32))
    outs = pl.pallas_call(
        body, name=name, out_shape=out_shape,
        in_specs=[HBM] * (2 * n) + [ANY] * len(deps),
        out_specs=[SEM] * (2 * ng) + [HBM] * (2 * n) + [pl.BlockSpec(memory_space=pltpu.VMEM)],
        input_output_aliases={i: 2 * ng + i for i in range(2 * n)},
        compiler_params=pltpu.CompilerParams(has_side_effects=pltpu.SideEffectType.DATAFLOW_SIDE_EFFECTING),
    )(*operands, *deps)
    handles, base = [], 0
    for gi, (mode, g) in enumerate(groups):
        srcs = outs[2 * ng + base:2 * ng + base + len(g)]
        lands = outs[2 * ng + n + base:2 * ng + n + base + len(g)]
        handles.append((mode, outs[2 * gi], outs[2 * gi + 1], list(srcs), list(lands)))
        base += len(g)
    return handles, outs[-1]


def _exchange_wait(handle, after, name):
    mode, send_sems, recv_sems, srcs, lands = handle
    n = len(srcs)
    nc = N_COPIES[mode]

    def body(*refs):
        src_refs, land_refs = refs[:n], refs[n:2 * n]
        send_ref, recv_ref = refs[2 * n], refs[2 * n + 1]
        x, y, c = lax.axis_index("x"), lax.axis_index("y"), lax.axis_index("c")
        for k, plan in enumerate(_exchange_plan(mode, x, y, c)):
            for a in range(n):
                cp = _exchange_copy(src_refs[a], land_refs[a], plan, send_ref.at[a * nc + k], recv_ref.at[a * nc + k],
                                    False)
                cp.wait_send()
                cp.wait_recv()

    outs = pl.pallas_call(
        body, name=name,
        out_shape=[pltpu.HBM(a.shape, a.dtype) for a in srcs + lands],
        in_specs=[HBM] * (2 * n) + [SEM, SEM] + [ANY] * len(after),
        out_specs=[HBM] * (2 * n),
        input_output_aliases={i: i for i in range(2 * n)},
        compiler_params=pltpu.CompilerParams(has_side_effects=pltpu.SideEffectType.DATAFLOW_SIDE_EFFECTING),
    )(*srcs, *lands, send_sems, recv_sems, *after)
    return list(outs[n:])


def _adamw_math(w, g, m, v):
    c1 = 1.0 - ADAM_B1 ** ADAM_STEP
    c2 = 1.0 - ADAM_B2 ** ADAM_STEP
    nm = ADAM_B1 * m + (1.0 - ADAM_B1) * g
    nv = ADAM_B2 * v + (1.0 - ADAM_B2) * (g * g)
    delta = -ADAM_LR * ((nm / c1) / (jnp.sqrt(nv / c2) + ADAM_EPS) + ADAM_WD * w)
    return delta, nm, nv


def _row_tile(r):
    for cand in (256, 128, 64, 32, 16, 8):
        if r % cand == 0 and r > cand:
            return cand
    return r


def _adamw(items, layer0, bufs, name):
    ni = len(items)
    nl = items[0][1].shape[1]
    tiles = [_row_tile(w.shape[1]) for w, _, _, _ in items]
    steps = [w.shape[1] // tr for (w, _, _, _), tr in zip(items, tiles)]
    ns = steps[0]
    assert all(s == ns for s in steps)
    nb = 0 if bufs is None else 4 * ni

    def body(*refs):
        ins, outs = refs[:4 * ni], refs[4 * ni + nb:]
        for k in range(ni):
            w_ref, p_ref, m_ref, v_ref = ins[4 * k:4 * k + 4]
            g = p_ref[0, 0].astype(F32)
            for s in range(1, p_ref.shape[0]):
                g = g + p_ref[s, 0].astype(F32)
            delta, nm, nv = _adamw_math(w_ref[0], g, m_ref[0], v_ref[0])
            g_ref, d_ref, nm_ref, nv_ref = outs[4 * k:4 * k + 4]
            g_ref[0], d_ref[0], nm_ref[0], nv_ref[0] = g, delta, nm, nv

    in_specs, out_specs, out_shape, operands = [], [], [], []
    for (w, parts, m, v), tr in zip(items, tiles):
        blk = pl.BlockSpec((1, tr, w.shape[2]), lambda l, i: (layer0 + l, i, 0))
        in_specs += [blk, pl.BlockSpec((parts.shape[0], 1, tr, w.shape[2]), lambda l, i: (0, l, i, 0)), blk, blk]
        operands += [w, parts, m, v]
        out_specs += [blk] * 4
        out_shape += [jax.ShapeDtypeStruct(w.shape, F32)] * 4
    if bufs is not None:
        in_specs += [ANY] * nb
        operands += [b for item in bufs for b in item]
    outs = pl.pallas_call(
        body, name=name, grid=(nl, ns), in_specs=in_specs, out_specs=out_specs, out_shape=out_shape,
        input_output_aliases={4 * ni + i: i for i in range(nb)},
        compiler_params=_params("arbitrary", "arbitrary"),
    )(*operands)
    return [tuple(outs[4 * k:4 * k + 4]) for k in range(ni)]


NAMES = ("norm_even", "w_in_even", "conv_a_w", "conv_a_b", "ln_a_g", "ln_a_b", "pool_w", "pool_b", "pool_scale",
         "w_out_even", "norm_odd", "w_in_odd", "conv_c_w", "conv_c_b", "w_rg", "b_rg", "w_ig", "b_ig", "lru_lambda",
         "w_out_odd", "final_norm")
SMALL_GATHERED = ("conv_a_w", "pool_b", "norm_odd", "conv_c_w", "conv_c_b", "b_rg", "b_ig", "lru_lambda")
BIG = (("w_in_even", "w_out_even"), ("w_in_odd", "w_out_odd"))
SMALL = (("conv_a_w", "pool_b", "pool_w"), ("norm_odd", "conv_c_w", "conv_c_b", "b_rg", "b_ig", "lru_lambda"))
REPLICATED = (("norm_even", "conv_a_b", "ln_a_g", "ln_a_b", "pool_scale"), ("w_rg", "w_ig"))
PACK_ROW = 1024


def _pack_rows(flat2d):
    pad = (-flat2d.shape[1]) % PACK_ROW
    return jnp.pad(flat2d, ((0, 0), (0, pad))).reshape(flat2d.shape[0], -1, 128)


def _unpack(flat, shapes):
    out, off = [], 0
    for s in shapes:
        n = 1
        for d in s:
            n *= d
        out.append(flat[..., off:off + n].reshape(flat.shape[:-1] + tuple(s)))
        off += n
    return out


def _to_global(name, g):
    if name in ("conv_a_w", "pool_b", "conv_c_w"):
        return jnp.transpose(g, (1, 2, 0, 3)).reshape(g.shape[1], g.shape[2], -1)
    if name == "pool_w":
        return jnp.transpose(g, (1, 2, 0, 3, 4)).reshape(2, 4, POOL_GW, POOL_GW)
    return jnp.transpose(g, (1, 0, 2)).reshape(g.shape[1], -1)


def _to_blocks(name, g):
    if name == "conv_a_w":
        return jnp.transpose(g.reshape(CONV_K, N_DEV, -1), (1, 0, 2))
    if name == "conv_c_w":
        return jnp.transpose(g.reshape(LRU_CONV_K, N_DEV, -1), (1, 0, 2))
    if name == "pool_b":
        return jnp.transpose(g.reshape(4, N_DEV, -1), (1, 0, 2))
    if name == "pool_w":
        return jnp.transpose(g.reshape(4, N_DEV, POOL_GW // N_DEV, POOL_GW), (1, 0, 2, 3))
    return g.reshape(N_DEV, -1)


def _as3d(a):
    if a.ndim == 1:
        return a.reshape(1, 1, -1)
    if a.ndim == 2:
        return a.reshape(a.shape[0], 1, a.shape[1])
    return a.reshape(a.shape[0], -1, a.shape[-1])


def kernel(x, norm_even, w_in_even, conv_a_w, conv_a_b, ln_a_g, ln_a_b, pool_w, pool_b, pool_scale, w_out_even, norm_odd, w_in_odd, conv_c_w, conv_c_b, w_rg, b_rg, w_ig, b_ig, lru_lambda, w_out_odd, final_norm, loss_target, m_norm_even, m_w_in_even, m_conv_a_w, m_conv_a_b, m_ln_a_g, m_ln_a_b, m_pool_w, m_pool_b, m_pool_scale, m_w_out_even, m_norm_odd, m_w_in_odd, m_conv_c_w, m_conv_c_b, m_w_rg, m_b_rg, m_w_ig, m_b_ig, m_lru_lambda, m_w_out_odd, m_final_norm, v_norm_even, v_w_in_even, v_conv_a_w, v_conv_a_b, v_ln_a_g, v_ln_a_b, v_pool_w, v_pool_b, v_pool_scale, v_w_out_even, v_norm_odd, v_w_in_odd, v_conv_c_w, v_conv_c_b, v_w_rg, v_b_rg, v_w_ig, v_b_ig, v_lru_lambda, v_w_out_odd, v_final_norm):
    w_loc = dict(zip(NAMES, [norm_even, w_in_even, conv_a_w, conv_a_b, ln_a_g, ln_a_b, pool_w, pool_b, pool_scale,
                             w_out_even, norm_odd, w_in_odd, conv_c_w, conv_c_b, w_rg, b_rg, w_ig, b_ig, lru_lambda,
                             w_out_odd, final_norm]))
    m_loc = dict(zip(NAMES, [m_norm_even, m_w_in_even, m_conv_a_w, m_conv_a_b, m_ln_a_g, m_ln_a_b, m_pool_w, m_pool_b,
                             m_pool_scale, m_w_out_even, m_norm_odd, m_w_in_odd, m_conv_c_w, m_conv_c_b, m_w_rg,
                             m_b_rg, m_w_ig, m_b_ig, m_lru_lambda, m_w_out_odd, m_final_norm]))
    v_loc = dict(zip(NAMES, [v_norm_even, v_w_in_even, v_conv_a_w, v_conv_a_b, v_ln_a_g, v_ln_a_b, v_pool_w, v_pool_b,
                             v_pool_scale, v_w_out_even, v_norm_odd, v_w_in_odd, v_conv_c_w, v_conv_c_b, v_w_rg,
                             v_b_rg, v_w_ig, v_b_ig, v_lru_lambda, v_w_out_odd, v_final_norm]))
    me = _slot(lax.axis_index("x"), lax.axis_index("y"), lax.axis_index("c"))

    def landing(own):
        zone = lax.empty((N_DEV,) + own.shape[1:], own.dtype)
        return lax.dynamic_update_slice(zone, own, (me,) + (0,) * (own.ndim - 1))

    small_shapes = [w_loc[n].shape for n in SMALL_GATHERED]
    small = jnp.concatenate([w_loc[n].reshape(1, -1) for n in SMALL_GATHERED], axis=1)
    first = _all_gather([w_in_even[0].astype(BF16), pool_w.astype(BF16), _pack_rows(small)[0]], "gather_first")
    g_small = dict(zip(SMALL_GATHERED, [_to_global(n, g) for n, g in
                                        zip(SMALL_GATHERED, _unpack(first[2].reshape(N_DEV, -1), small_shapes))]))
    pool_w_all = _to_global("pool_w", first[1])

    def pairs_of(shards):
        return [(s.astype(BF16), landing(s.astype(BF16)[None])) for s in shards]

    shards = {1: [w_in_odd[0], w_out_odd[0]], 2: [w_in_even[1], w_out_even[1]], 3: [w_in_odd[1], w_out_odd[1]]}
    leg_a, leg_b = {}, {}
    (w_out_0, leg_a[1]), token_1 = _exchange_start(
        [("gather", pairs_of([w_out_even[0]])), ("chip_gather", pairs_of(shards[1]))], [first[0]], "gather_start_1")
    unused = jnp.zeros((8, 128), F32)

    def second_leg(layer, y):
        lands = _exchange_wait(leg_a[layer], [y], f"gather_wait_a_{layer}")
        (leg_b[layer],), token = _exchange_start([("forward", [(unused, l) for l in lands])], [],
                                                 f"gather_forward_{layer}")
        return token

    def layer_weights(layer, h):
        j = layer // 2
        dep = h
        if layer == 0:
            w_in, dep = first[0], token_1
            w_out_now = lambda y: _exchange_wait(w_out_0, [y], "gather_wait_out_0")[0]
        else:
            w_in, w_out_got = _exchange_wait(leg_b[layer], [h], f"gather_wait_b_{layer}")
            w_out_now = lambda y: w_out_got
            if layer + 1 in shards:
                (leg_a[layer + 1],), dep = _exchange_start([("chip_gather", pairs_of(shards[layer + 1]))], [w_in],
                                                           f"gather_start_{layer + 1}")

        def w_out(y):
            return w_out_now(y), (second_leg(layer + 1, y) if layer + 1 in shards else y)

        row = lambda a: a[j][None]
        if layer % 2 == 0:
            return dict(dep=dep, norm=row(norm_even), w_in=w_in,
                        w_out=lambda y: (lambda wo, d: (wo.reshape(W_EVEN_MIX, D_MODEL), d))(*w_out(y)),
                        conv_w=jnp.pad(g_small["conv_a_w"][j], ((0, 1), (0, 0))), conv_b=row(conv_a_b),
                        ln_g=row(ln_a_g), ln_b=row(ln_a_b), pool_w=pool_w_all[j],
                        pool_b=g_small["pool_b"][j].reshape(1, W_POOL), pool_scale=row(pool_scale))
        return dict(dep=dep, norm=row(g_small["norm_odd"]), w_in=w_in,
                    w_out=lambda y: (lambda wo, d: (wo.reshape(W_LRU, D_MODEL), d))(*w_out(y)),
                    conv_w=jnp.pad(g_small["conv_c_w"][j], ((0, 4), (0, 0))), conv_b=row(g_small["conv_c_b"]),
                    w_rg=w_rg[j].astype(BF16), b_rg=row(g_small["b_rg"]), w_ig=w_ig[j].astype(BF16),
                    b_ig=row(g_small["b_ig"]), lam=row(g_small["lru_lambda"]))

    pending, exchanges, small_layout = {}, {}, {}
    last_token = []

    def on_grads(layer, grads, dep, last):
        par = layer % 2
        w_in_name, w_out_name = BIG[par]
        have = pending.setdefault(layer, {})
        have.update(grads)
        eager = layer == 0
        scatter, gather = {}, {}
        if w_out_name in have and (eager or last):
            scatter["out"] = have.pop(w_out_name).reshape(N_DEV, -1, D_MODEL)
        if w_in_name in have and (eager or last):
            scatter["in"] = have.pop(w_in_name)
        if layer not in small_layout and all(n in have for n in SMALL[par]) and (eager or last):
            blocks = [_to_blocks(n, have[n]) for n in SMALL[par]]
            small_layout[layer] = [b.shape[1:] for b in blocks]
            scatter["small"] = _pack_rows(jnp.concatenate([b.reshape(N_DEV, -1) for b in blocks], axis=1))
        for n in REPLICATED[1]:
            if n in have:
                gather[n] = have.pop(n).astype(BF16).reshape(-1, LRU_HD)
        if last:
            vectors = [have[n] for n in REPLICATED[par]] if par == 0 else []
            if "final_norm" in have:
                vectors.append(have["final_norm"])
            if vectors:
                gather["rep32"] = jnp.concatenate([v.reshape(-1) for v in vectors]).reshape(-1, 128)
        groups = []
        if scatter:
            groups.append(("scatter", [(s, landing(lax.dynamic_slice_in_dim(s, me, 1, 0))) for s in scatter.values()]))
        if gather:
            groups.append(("gather", [(s, landing(s[None])) for s in gather.values()]))
        if not groups:
            return dep
        keys = [list(d) for d in (scatter, gather) if d]
        handles, token = _exchange_start(groups, [], f"grads_start_{layer}_{'_'.join(k for ks in keys for k in ks)}")
        exchanges.setdefault(layer, []).extend(zip(keys, handles))
        last_token[:] = [token]
        return token

    loss, grad_x = _local_step(x[0], loss_target[0], layer_weights, final_norm[None], on_grads)

    w3 = {n: _as3d(w_loc[n]) for n in NAMES}
    m3 = {n: _as3d(m_loc[n]) for n in NAMES}
    v3 = {n: _as3d(v_loc[n]) for n in NAMES}
    results = {}
    after = list(last_token)
    for layer in (3, 2, 1, 0):
        par, j = layer % 2, layer // 2
        got = {}
        for keys, handle in exchanges[layer]:
            got.update(zip(keys, _exchange_wait(handle, after, f"grads_wait_{layer}_{'_'.join(keys)}")))
        parts = {BIG[par][0]: got["in"], BIG[par][1]: got["out"]}
        parts.update(zip(SMALL[par], _unpack(got["small"].reshape(N_DEV, -1), small_layout[layer])))
        if par == 1:
            parts.update({n: got[n] for n in REPLICATED[1]})
        else:
            parts.update(zip(REPLICATED[0], _unpack(got["rep32"].reshape(N_DEV, -1)[:, :len(REPLICATED[0]) * D_MODEL],
                                                    [w_loc[n].shape[1:] for n in REPLICATED[0]])))
        big_items, small_items = [], []
        for n, pt in parts.items():
            item = (w3[n], pt.reshape((N_DEV, 1) + w3[n].shape[1:]), m3[n], v3[n])
            (big_items if w3[n].shape[1] >= 128 else small_items).append((n, item))
        for n, item in big_items:
            results[n] = _adamw([item], j, [results[n]] if n in results else None, f"adamw_{n}_{layer}")[0]
        snames = [n for n, _ in small_items]
        prev = [results[n] for n in snames] if snames[0] in results else None
        for n, r in zip(snames, _adamw([it for _, it in small_items], j, prev, f"adamw_small_{layer}")):
            results[n] = r
        if layer == N_LAYERS - 1:
            item = (w3["final_norm"], got["rep32"].reshape(N_DEV, 1, 1, -1)[..., :D_MODEL], m3["final_norm"],
                    v3["final_norm"])
            results["final_norm"] = _adamw([item], 0, None, "adamw_final_norm")[0]
        after = [results[BIG[par][0]][1]]

    total = lax.psum(loss[0, 0], ("x", "y", "c"))
    outs = [[results[n][k].reshape(w_loc[n].shape) for n in NAMES] for k in range(4)]
    return (total, grad_x[None], *outs[0], *outs[1], *outs[2], *outs[3])
```

```python
import functools

import jax
import jax.numpy as jnp
from jax import lax
from jax.experimental import pallas as pl
from jax.experimental.pallas import tpu as pltpu

F32 = jnp.float32
BF16 = jnp.bfloat16

N_DEV = 8
N_PEERS = N_DEV - 1
N_LAYERS = 4
D_MODEL = 1024
EPS_RMS = 1e-6
EPS_LN = 1e-5
W_CONV = 1024
CONV_K = 31
W_POOL = 1024
POOL_WINDOWS = (2, 4, 8, 16)
POOL_GW = 256
W_EVEN_IN = 5120
W_EVEN_MIX = 2048
LRU_HEADS = 12
LRU_HD = 128
W_LRU = 1536
LRU_CONV_K = 4
LRU_C = 8.0
ADAM_LR = 0.001
ADAM_B1 = 0.9
ADAM_B2 = 0.999
ADAM_EPS = 1e-08
ADAM_WD = 0.01
ADAM_STEP = 10

HALO = 32
HALO_C = 8
TM_MATMUL = 512
TM_STREAM = 1024
TM_WGRAD = 2048
TM_MIXER = 256
CONV_ROWS = 128
VMEM_LIMIT = 56 * 1024 * 1024
MESH = pl.DeviceIdType.MESH
ANY = pl.BlockSpec(memory_space=pl.ANY)
HBM = pl.BlockSpec(memory_space=pltpu.HBM)
SEM = pl.BlockSpec(memory_space=pltpu.SEMAPHORE)


def _params(*sem):
    return pltpu.CompilerParams(dimension_semantics=sem, vmem_limit_bytes=VMEM_LIMIT)


def _sigmoid(z):
    return 0.5 * jnp.tanh(0.5 * z) + 0.5


def _dsilu(z, s):
    return s * (1.0 + z * (1.0 - s))


def _full(shape):
    nd = len(shape)
    return pl.BlockSpec(shape, lambda *_: (0,) * nd)


def _norm_matmul(h, g, w, dep, *, tm):
    t, d = h.shape
    n = w.shape[1]
    tn = n // 4

    def body(h_ref, g_ref, w_ref, dep_ref, p_ref, hn_ref):
        @pl.when(pl.program_id(1) == 0)
        def _():
            x = h_ref[...]
            r = lax.rsqrt(jnp.mean(x * x, axis=-1, keepdims=True) + EPS_RMS)
            hn_ref[...] = ((x * r) * g_ref[...]).astype(BF16)

        p_ref[...] = jnp.dot(hn_ref[...], w_ref[...], preferred_element_type=F32)

    return pl.pallas_call(
        body, name="norm_matmul", grid=(t // tm, n // tn),
        in_specs=[pl.BlockSpec((tm, d), lambda i, j: (i, 0)), _full((1, d)),
                  pl.BlockSpec((d, tn), lambda i, j: (0, j)), ANY],
        out_specs=[pl.BlockSpec((tm, tn), lambda i, j: (i, j)), pl.BlockSpec((tm, d), lambda i, j: (i, 0))],
        out_shape=[jax.ShapeDtypeStruct((t, n), F32), jax.ShapeDtypeStruct((t, d), BF16)],
        compiler_params=_params("arbitrary", "arbitrary"),
    )(h, g, w, dep)


def _out_proj(h, y, w, dep, *, tm):
    t, d = h.shape
    k = y.shape[1]

    def body(h_ref, y_ref, w_ref, dep_ref, o_ref):
        o_ref[...] = h_ref[...] + jnp.dot(y_ref[...], w_ref[...], preferred_element_type=F32)

    return pl.pallas_call(
        body, name="out_proj", grid=(t // tm,),
        in_specs=[pl.BlockSpec((tm, d), lambda i: (i, 0)), pl.BlockSpec((tm, k), lambda i: (i, 0)), _full((k, d)),
                  ANY],
        out_specs=pl.BlockSpec((tm, d), lambda i: (i, 0)),
        out_shape=jax.ShapeDtypeStruct((t, d), F32),
        compiler_params=_params("arbitrary"),
    )(h, y, w, dep)


def _out_proj_bwd(dh, y, w, dep, *, tm):
    t, d = dh.shape
    k = y.shape[1]
    nt = t // tm

    def body(dh_ref, y_ref, w_ref, dep_ref, dy_ref, dw_ref, acc):
        i = pl.program_id(0)
        g = dh_ref[...].astype(BF16)
        dy_ref[...] = lax.dot_general(g, w_ref[...], (((1,), (1,)), ((), ())), preferred_element_type=F32)
        part = lax.dot_general(y_ref[...], g, (((0,), (0,)), ((), ())), preferred_element_type=F32)

        @pl.when(i == 0)
        def _():
            acc[...] = part

        @pl.when(i > 0)
        def _():
            acc[...] += part

        @pl.when(i == nt - 1)
        def _():
            dw_ref[...] = acc[...].astype(BF16)

    return pl.pallas_call(
        body, name="out_proj_bwd", grid=(nt,),
        in_specs=[pl.BlockSpec((tm, d), lambda i: (i, 0)), pl.BlockSpec((tm, k), lambda i: (i, 0)), _full((k, d)),
                  ANY],
        out_specs=[pl.BlockSpec((tm, k), lambda i: (i, 0)), _full((k, d))],
        out_shape=[jax.ShapeDtypeStruct((t, k), F32), jax.ShapeDtypeStruct((k, d), BF16)],
        scratch_shapes=[pltpu.VMEM((k, d), F32)],
        compiler_params=_params("arbitrary"),
    )(dh, y, w, dep)


def _in_proj_bwd_x(dp, w, h, g, dh_out, dep, *, tm):
    t, d = h.shape
    n = w.shape[1]
    nt = t // tm

    def body(dp_ref, w_ref, h_ref, g_ref, dho_ref, dep_ref, dh_ref, dg_ref):
        i = pl.program_id(0)
        dy = lax.dot_general(dp_ref[...], w_ref[...], (((1,), (1,)), ((), ())), preferred_element_type=F32)
        x = h_ref[...]
        r = lax.rsqrt(jnp.mean(x * x, axis=-1, keepdims=True) + EPS_RMS)
        gd = dy * g_ref[...]
        m = jnp.mean(gd * x, axis=-1, keepdims=True)
        dh_ref[...] = dho_ref[...] + r * gd - x * (r * r * r * m)
        _acc_out(i, dg_ref, jnp.sum(dy * x * r, axis=0, keepdims=True))

    return pl.pallas_call(
        body, name="in_proj_bwd_x", grid=(nt,),
        in_specs=[pl.BlockSpec((tm, n), lambda i: (i, 0)), _full((d, n)),
                  pl.BlockSpec((tm, d), lambda i: (i, 0)), _full((1, d)), pl.BlockSpec((tm, d), lambda i: (i, 0)),
                  ANY],
        out_specs=[pl.BlockSpec((tm, d), lambda i: (i, 0)), _full((1, d))],
        out_shape=[jax.ShapeDtypeStruct((t, d), F32), jax.ShapeDtypeStruct((1, d), F32)],
        compiler_params=_params("arbitrary"),
    )(dp, w, h, g, dh_out, dep)


def _in_proj_bwd_w(hn, dp, nd, *, tm):
    t, d = hn.shape
    nb = dp.shape[1] // nd
    nt = t // tm

    def body(hn_ref, dp_ref, dw_ref, acc):
        i = pl.program_id(1)
        part = lax.dot_general(hn_ref[...], dp_ref[...], (((0,), (0,)), ((), ())), preferred_element_type=F32)

        @pl.when(i == 0)
        def _():
            acc[...] = part

        @pl.when(i > 0)
        def _():
            acc[...] += part

        @pl.when(i == nt - 1)
        def _():
            dw_ref[0] = acc[...].astype(BF16)

    return pl.pallas_call(
        body, name="in_proj_bwd_w", grid=(nd, nt),
        in_specs=[pl.BlockSpec((tm, d), lambda j, i: (i, 0)), pl.BlockSpec((tm, nb), lambda j, i: (i, j))],
        out_specs=pl.BlockSpec((1, d, nb), lambda j, i: (j, 0, 0)),
        out_shape=jax.ShapeDtypeStruct((nd, d, nb), BF16),
        scratch_shapes=[pltpu.VMEM((d, nb), F32)],
        compiler_params=_params("arbitrary", "arbitrary"),
    )(hn, dp)


def _loss_head(h, g, target, *, tm):
    t, d = h.shape
    nt = t // tm

    def body(h_ref, g_ref, t_ref, loss_ref, dh_ref, dg_ref):
        i = pl.program_id(0)
        x = h_ref[...]
        r = lax.rsqrt(jnp.mean(x * x, axis=-1, keepdims=True) + EPS_RMS)
        xr = x * r
        err = xr * g_ref[...] - t_ref[...]
        lp = 0.5 * jnp.sum(jnp.mean(err * err, axis=-1, keepdims=True), axis=0, keepdims=True)
        dy = err * (1.0 / d)
        gd = dy * g_ref[...]
        m = jnp.mean(gd * x, axis=-1, keepdims=True)
        dh_ref[...] = r * gd - x * (r * r * r * m)
        dgp = jnp.sum(dy * xr, axis=0, keepdims=True)

        @pl.when(i == 0)
        def _():
            loss_ref[...] = lp
            dg_ref[...] = dgp

        @pl.when(i > 0)
        def _():
            loss_ref[...] += lp
            dg_ref[...] += dgp

    return pl.pallas_call(
        body, name="loss_head", grid=(nt,),
        in_specs=[pl.BlockSpec((tm, d), lambda i: (i, 0)), _full((1, d)), pl.BlockSpec((tm, d), lambda i: (i, 0))],
        out_specs=[_full((1, 1)), pl.BlockSpec((tm, d), lambda i: (i, 0)), _full((1, d))],
        out_shape=[jax.ShapeDtypeStruct((1, 1), F32), jax.ShapeDtypeStruct((t, d), F32),
                   jax.ShapeDtypeStruct((1, d), F32)],
        compiler_params=_params("arbitrary"),
    )(h, g, target)


def _col(tm, w, c):
    return pl.BlockSpec((tm, w), lambda i: (i, c))


def _prev_halo(tm, rows, w, c):
    per = tm // rows
    return pl.BlockSpec((rows, w), lambda i: (jnp.maximum(i * per - 1, 0), c))


def _next_halo(tm, rows, w, c, t):
    per = tm // rows
    last = t // rows - 1
    return pl.BlockSpec((rows, w), lambda i: (jnp.minimum((i + 1) * per, last), c))


def _inv_count(first_row, rows, window):
    tpos = first_row + lax.broadcasted_iota(jnp.int32, (rows, 1), 0)
    return 1.0 / jnp.minimum(tpos + 1, window).astype(F32)


def _even_fwd(p, cw, cb, lg, lb, pw, pb, ps, *, tm):
    t = p.shape[0]
    wc = W_CONV

    def body(av, ag, agate, bv, bgate, avh, agh, bvh, cw_ref, cb_ref, lg_ref, lb_ref, pw_ref, pb_ref, ps_ref,
             y_ref, u1_ref, e_ref, d_ref, uext, vext):
        i = pl.program_id(0)
        keep = (i > 0).astype(F32)
        uext[0:HALO, :] = keep * (avh[...] * _sigmoid(agh[...]))
        uext[HALO:, :] = av[...] * _sigmoid(ag[...])
        vext[0:HALO, :] = keep * bvh[...]
        vext[HALO:, :] = bv[...]
        for c in range(0, wc, 128):
            for rb in range(0, tm, CONV_ROWS):
                acc = jnp.broadcast_to(cb_ref[:, c:c + 128], (CONV_ROWS, 128))
                for r in range(8):
                    shifted = uext[pl.ds(rb + 8 - r, CONV_ROWS + HALO - 8), c:c + 128]
                    for q in range(HALO // 8):
                        s = 8 * q + r
                        if s < CONV_K:
                            acc = acc + cw_ref[CONV_K - 1 - s:CONV_K - s, c:c + 128] * shifted[24 - 8 * q:24 - 8 * q + CONV_ROWS]
                u1_ref[rb:rb + CONV_ROWS, c:c + 128] = acc
        u1 = u1_ref[...]
        mu = jnp.mean(u1, axis=-1, keepdims=True)
        xc = u1 - mu
        rs = lax.rsqrt(jnp.mean(xc * xc, axis=-1, keepdims=True) + EPS_LN)
        u2 = (xc * rs) * lg_ref[...] + lb_ref[...]
        u3 = u2 * _sigmoid(u2)
        ga = agate[...]
        y_ref[:, 0:wc] = (u3 * (ga * _sigmoid(ga))).astype(BF16)
        for g, win in enumerate(POOL_WINDOWS):
            cs = slice(g * POOL_GW, (g + 1) * POOL_GW)
            s = vext[pl.ds(HALO, tm), cs]
            for j in range(1, win):
                s = s + vext[pl.ds(HALO - j, tm), cs]
            dg = s * _inv_count(i * tm, tm, win) - vext[pl.ds(HALO, tm), cs]
            dgb = dg.astype(BF16)
            d_ref[:, cs] = dgb
            eg = jnp.dot(dgb, pw_ref[g], preferred_element_type=F32) + pb_ref[:, cs]
            e_ref[:, cs] = eg
            gb = bgate[:, cs]
            y_ref[:, wc + g * POOL_GW:wc + (g + 1) * POOL_GW] = ((eg * ps_ref[:, cs]) * (gb * _sigmoid(gb))).astype(BF16)

    row = lambda w: pl.BlockSpec((tm, w), lambda i: (i, 0))
    return pl.pallas_call(
        body, name="even_fwd", grid=(t // tm,),
        in_specs=[_col(tm, wc, 0), _col(tm, wc, 1), _col(tm, wc, 2), _col(tm, wc, 3), _col(tm, wc, 4),
                  _prev_halo(tm, HALO, wc, 0), _prev_halo(tm, HALO, wc, 1), _prev_halo(tm, HALO, wc, 3),
                  _full((32, wc)), _full((1, wc)), _full((1, wc)), _full((1, wc)),
                  _full((4, POOL_GW, POOL_GW)), _full((1, wc)), _full((1, wc))],
        out_specs=[row(2 * wc), row(wc), row(wc), row(wc)],
        out_shape=[jax.ShapeDtypeStruct((t, 2 * wc), BF16), jax.ShapeDtypeStruct((t, wc), F32),
                   jax.ShapeDtypeStruct((t, wc), F32), jax.ShapeDtypeStruct((t, wc), BF16)],
        scratch_shapes=[pltpu.VMEM((tm + HALO, wc), F32), pltpu.VMEM((tm + HALO, wc), F32)],
        compiler_params=_params("arbitrary"),
    )(p, p, p, p, p, p, p, p, cw, cb, lg, lb, pw, pb, ps)


def _acc_out(i, ref, val):
    @pl.when(i == 0)
    def _():
        ref[...] = val

    @pl.when(i > 0)
    def _():
        ref[...] += val


def _even_bwd_a(p, u1, e, dmat, dy, lg, lb, pw, ps, dep, *, tm):
    t = p.shape[0]
    wc = W_CONV

    def body(agate, bgate, u1_ref, e_ref, d_ref, dya, dyb, lg_ref, lb_ref, pw_ref, ps_ref, dep_ref,
             du1_ref, dd_ref, dgat_ref, dlg_ref, dlb_ref, dpb_ref, dps_ref, dpw_ref):
        i = pl.program_id(0)

        @pl.when(i == 0)
        def _():
            dpw_ref[...] = jnp.zeros_like(dpw_ref)

        u1 = u1_ref[...]
        mu = jnp.mean(u1, axis=-1, keepdims=True)
        xc = u1 - mu
        rs = lax.rsqrt(jnp.mean(xc * xc, axis=-1, keepdims=True) + EPS_LN)
        xh = xc * rs
        u2 = xh * lg_ref[...] + lb_ref[...]
        s2 = _sigmoid(u2)
        ga = agate[...]
        sa = _sigmoid(ga)
        dy_a = dya[...]
        dgat_ref[:, 0:wc] = (dy_a * (u2 * s2) * _dsilu(ga, sa)).astype(BF16)
        du2 = dy_a * (ga * sa) * _dsilu(u2, s2)
        _acc_out(i, dlg_ref, jnp.sum(du2 * xh, axis=0, keepdims=True))
        _acc_out(i, dlb_ref, jnp.sum(du2, axis=0, keepdims=True))
        dxh = du2 * lg_ref[...]
        m1 = jnp.mean(dxh, axis=-1, keepdims=True)
        m2 = jnp.mean(dxh * xh, axis=-1, keepdims=True)
        du1_ref[...] = rs * (dxh - m1 - xh * m2)

        gb = bgate[...]
        sb = _sigmoid(gb)
        ev = e_ref[...]
        dy_b = dyb[...]
        dgat_ref[:, wc:2 * wc] = (dy_b * (ev * ps_ref[...]) * _dsilu(gb, sb)).astype(BF16)
        dz = dy_b * (gb * sb)
        _acc_out(i, dps_ref, jnp.sum(dz * ev, axis=0, keepdims=True))
        de = dz * ps_ref[...]
        _acc_out(i, dpb_ref, jnp.sum(de, axis=0, keepdims=True))
        for g in range(len(POOL_WINDOWS)):
            cs = slice(g * POOL_GW, (g + 1) * POOL_GW)
            deg = de[:, cs].astype(BF16)
            dd_ref[:, cs] = lax.dot_general(deg, pw_ref[g], (((1,), (1,)), ((), ())), preferred_element_type=F32)
            dpw_ref[g] += lax.dot_general(d_ref[:, cs], deg, (((0,), (0,)), ((), ())), preferred_element_type=F32)

    row = lambda w: pl.BlockSpec((tm, w), lambda i: (i, 0))
    return pl.pallas_call(
        body, name="even_bwd_a", grid=(t // tm,),
        in_specs=[_col(tm, wc, 2), _col(tm, wc, 4), row(wc), row(wc), row(wc), _col(tm, wc, 0), _col(tm, wc, 1),
                  _full((1, wc)), _full((1, wc)), _full((4, POOL_GW, POOL_GW)), _full((1, wc)), ANY],
        out_specs=[row(wc), row(wc), row(2 * wc), _full((1, wc)), _full((1, wc)), _full((1, wc)), _full((1, wc)),
                   _full((4, POOL_GW, POOL_GW))],
        out_shape=[jax.ShapeDtypeStruct((t, wc), F32), jax.ShapeDtypeStruct((t, wc), F32),
                   jax.ShapeDtypeStruct((t, 2 * wc), BF16)] + [jax.ShapeDtypeStruct((1, wc), F32)] * 4
                  + [jax.ShapeDtypeStruct((4, POOL_GW, POOL_GW), F32)],
        compiler_params=_params("arbitrary"),
    )(p, p, u1, e, dmat, dy, dy, lg, lb, pw, ps, dep)


def _even_bwd_b(p, du1, dd, dgat, cw, *, tm):
    t = p.shape[0]
    wc = W_CONV
    nt = t // tm

    def body(av, ag, avh, agh, du1_ref, du1n, dd_ref, ddn, dgat_ref, cw_ref, dp_ref, dcw_ref, dcb_ref,
             uext, gext, dext, du0, dcw8):
        i = pl.program_id(0)
        keep_p = (i > 0).astype(F32)
        keep_n = (i < nt - 1).astype(F32)

        @pl.when(i == 0)
        def _():
            dcw8[...] = jnp.zeros_like(dcw8)

        a = av[...]
        sg = _sigmoid(ag[...])
        uext[0:HALO, :] = keep_p * (avh[...] * _sigmoid(agh[...]))
        uext[HALO:, :] = a * sg
        gext[0:tm, :] = du1_ref[...]
        gext[tm:, :] = keep_n * du1n[...]
        for c in range(0, wc, 128):
            for rb in range(0, tm, CONV_ROWS):
                acc = jnp.zeros((CONV_ROWS, 128), F32)
                for r in range(8):
                    ahead = gext[pl.ds(rb + r, CONV_ROWS + HALO - 8), c:c + 128]
                    for q in range(HALO // 8):
                        s = 8 * q + r
                        if s < CONV_K:
                            acc = acc + cw_ref[CONV_K - 1 - s:CONV_K - s, c:c + 128] * ahead[8 * q:8 * q + CONV_ROWS]
                du0[rb:rb + CONV_ROWS, c:c + 128] = acc
                gcur = du1_ref[rb:rb + CONV_ROWS, c:c + 128]
                for r in range(8):
                    behind = uext[pl.ds(rb + 8 - r, CONV_ROWS + HALO - 8), c:c + 128]
                    for q in range(HALO // 8):
                        s = 8 * q + r
                        if s < CONV_K:
                            prod = gcur * behind[24 - 8 * q:24 - 8 * q + CONV_ROWS]
                            part = prod[0:8]
                            for o in range(8, CONV_ROWS, 8):
                                part = part + prod[o:o + 8]
                            k = CONV_K - 1 - s
                            dcw8[8 * k:8 * k + 8, c:c + 128] += part

        @pl.when(i == nt - 1)
        def _():
            for k in range(CONV_K):
                dcw_ref[k:k + 1, :] = jnp.sum(dcw8[8 * k:8 * k + 8, :], axis=0, keepdims=True)
            dcw_ref[CONV_K:32, :] = jnp.zeros((32 - CONV_K, wc), F32)

        _acc_out(i, dcb_ref, jnp.sum(du1_ref[...], axis=0, keepdims=True))
        g0 = du0[...]
        dp_ref[:, 0:wc] = (g0 * sg).astype(BF16)
        dp_ref[:, wc:2 * wc] = (g0 * a * sg * (1.0 - sg)).astype(BF16)
        dp_ref[:, 2 * wc:3 * wc] = dgat_ref[:, 0:wc]
        dp_ref[:, 4 * wc:5 * wc] = dgat_ref[:, wc:2 * wc]
        for g, win in enumerate(POOL_WINDOWS):
            cs = slice(g * POOL_GW, (g + 1) * POOL_GW)
            dext[0:tm, cs] = dd_ref[:, cs] * _inv_count(i * tm, tm, win)
            dext[tm:, cs] = keep_n * (ddn[:, cs] * _inv_count((i + 1) * tm, HALO, win))
            s = dext[pl.ds(0, tm), cs]
            for j in range(1, win):
                s = s + dext[pl.ds(j, tm), cs]
            dp_ref[:, 3 * wc + g * POOL_GW:3 * wc + (g + 1) * POOL_GW] = (s - dd_ref[:, cs]).astype(BF16)

    row = lambda w: pl.BlockSpec((tm, w), lambda i: (i, 0))
    return pl.pallas_call(
        body, name="even_bwd_b", grid=(nt,),
        in_specs=[_col(tm, wc, 0), _col(tm, wc, 1), _prev_halo(tm, HALO, wc, 0), _prev_halo(tm, HALO, wc, 1),
                  row(wc), _next_halo(tm, HALO, wc, 0, t), row(wc), _next_halo(tm, HALO, wc, 0, t), row(2 * wc),
                  _full((32, wc))],
        out_specs=[row(5 * wc), _full((32, wc)), _full((1, wc))],
        out_shape=[jax.ShapeDtypeStruct((t, 5 * wc), BF16), jax.ShapeDtypeStruct((32, wc), F32),
                   jax.ShapeDtypeStruct((1, wc), F32)],
        scratch_shapes=[pltpu.VMEM((tm + HALO, wc), F32), pltpu.VMEM((tm + HALO, wc), F32),
                        pltpu.VMEM((tm + HALO, wc), F32), pltpu.VMEM((tm, wc), F32), pltpu.VMEM((8 * 32, wc), F32)],
        compiler_params=_params("arbitrary"),
    )(p, p, p, p, du1, du1, dd, dd, dgat, cw)


def _softplus_neg(lam):
    z = -lam
    return jnp.maximum(z, 0.0) + jnp.log1p(jnp.exp(-jnp.abs(z)))


def _one_minus_exp(x):
    series = -x * (1.0 + x * (0.5 + x * (1.0 / 6.0 + x * (1.0 / 24.0))))
    return jnp.where(x > -0.02, series, 1.0 - jnp.exp(x))


def _odd_fwd(p, ccw, ccb, wrg, brg, wig, big, lam, *, tm):
    t = p.shape[0]
    wl = W_LRU
    ng = tm // 8

    def body(xr, gate, xrh, ccw_ref, ccb_ref, wrg_ref, brg_ref, wig_ref, big_ref, lam_ref,
             y_ref, xc_ref, r_ref, i_ref, hs_ref, xext, a_s, b_s, carry):
        i = pl.program_id(0)
        keep = (i > 0).astype(F32)
        xext[0:HALO_C, :] = keep * xrh[...]
        xext[HALO_C:, :] = xr[...]
        xc = jnp.broadcast_to(ccb_ref[...], (tm, wl))
        for k in range(LRU_CONV_K):
            xc = xc + ccw_ref[k:k + 1, :] * xext[pl.ds(HALO_C - (LRU_CONV_K - 1) + k, tm), :]
        xc_ref[...] = xc
        for h in range(LRU_HEADS):
            cs = slice(h * LRU_HD, (h + 1) * LRU_HD)
            xh = xc_ref[:, cs].astype(BF16)
            r_ref[:, cs] = _sigmoid(jnp.dot(xh, wrg_ref[h], preferred_element_type=F32) + brg_ref[:, cs])
            i_ref[:, cs] = _sigmoid(jnp.dot(xh, wig_ref[h], preferred_element_type=F32) + big_ref[:, cs])
        log_a = (-LRU_C * _softplus_neg(lam_ref[...])) * r_ref[...]
        a_s[...] = jnp.exp(log_a)
        b_s[...] = jnp.sqrt(_one_minus_exp(2.0 * log_a)) * (i_ref[...] * xc_ref[...])

        @pl.when(i == 0)
        def _():
            carry[...] = jnp.zeros_like(carry)

        rowi = lax.broadcasted_iota(jnp.int32, (8, wl), 0)

        def step(g, c):
            sl = pl.ds(pl.multiple_of(g * 8, 8), 8)
            aa, bb = a_s[sl, :], b_s[sl, :]
            for s in (1, 2, 4):
                m = rowi >= s
                a_sh = jnp.where(m, pltpu.roll(aa, s, 0), 1.0)
                b_sh = jnp.where(m, pltpu.roll(bb, s, 0), 0.0)
                bb = aa * b_sh + bb
                aa = aa * a_sh
            hv = bb + aa * c
            hs_ref[sl, :] = hv
            return hv[7:8, :]

        carry[...] = lax.fori_loop(0, ng, step, carry[...])
        gt = gate[...]
        y_ref[...] = (hs_ref[...] * (gt * _sigmoid(gt))).astype(BF16)

    row = lambda w: pl.BlockSpec((tm, w), lambda i: (i, 0))
    return pl.pallas_call(
        body, name="odd_fwd", grid=(t // tm,),
        in_specs=[_col(tm, wl, 0), _col(tm, wl, 1), _prev_halo(tm, HALO_C, wl, 0), _full((8, wl)), _full((1, wl)),
                  _full((LRU_HEADS, LRU_HD, LRU_HD)), _full((1, wl)), _full((LRU_HEADS, LRU_HD, LRU_HD)),
                  _full((1, wl)), _full((1, wl))],
        out_specs=[row(wl)] * 5,
        out_shape=[jax.ShapeDtypeStruct((t, wl), BF16)] + [jax.ShapeDtypeStruct((t, wl), F32)] * 4,
        scratch_shapes=[pltpu.VMEM((tm + HALO_C, wl), F32), pltpu.VMEM((tm, wl), F32), pltpu.VMEM((tm, wl), F32),
                        pltpu.VMEM((1, wl), F32)],
        compiler_params=_params("arbitrary"),
    )(p, p, p, ccw, ccb, wrg, brg, wig, big, lam)


def _odd_bwd_a(p, xc, r, ig, hs, dy, wrg, wig, lam, dep, *, tm):
    t = p.shape[0]
    wl = W_LRU
    nt = t // tm
    ng = tm // 8
    per = tm // HALO_C

    def body(gate, xc_ref, r_ref, i_ref, hs_ref, hsh, dy_ref, wrg_ref, wig_ref, lam_ref, dep_ref,
             dxc_ref, dgate_ref, dwrg_ref, dwig_ref, dbrg_ref, dbig_ref, dlam_ref,
             hext, a_s, q_s, g_s, dpr_s, dpi_s, carry):
        i = pl.program_id(0)
        ti = nt - 1 - i
        keep = (ti > 0).astype(F32)
        hext[0:HALO_C, :] = keep * hsh[...]
        hext[HALO_C:, :] = hs_ref[...]
        gt = gate[...]
        sg = _sigmoid(gt)
        dyv = dy_ref[...]
        dgate_ref[...] = (dyv * hs_ref[...] * _dsilu(gt, sg)).astype(BF16)
        q_s[...] = dyv * (gt * sg)
        sp = _softplus_neg(lam_ref[...])
        log_a = (-LRU_C * sp) * r_ref[...]
        a_s[...] = jnp.exp(log_a)

        @pl.when(i == 0)
        def _():
            carry[...] = jnp.zeros_like(carry)
            dwrg_ref[...] = jnp.zeros_like(dwrg_ref)
            dwig_ref[...] = jnp.zeros_like(dwig_ref)

        rowi = lax.broadcasted_iota(jnp.int32, (8, wl), 0)

        def step(gr, c):
            sl = pl.ds(pl.multiple_of((ng - 1 - gr) * 8, 8), 8)
            a0 = a_s[sl, :]
            al = jnp.where(rowi < 7, pltpu.roll(a0, 7, 0), 1.0)
            be = q_s[sl, :]
            for s in (1, 2, 4):
                m = rowi + s <= 7
                al_sh = jnp.where(m, pltpu.roll(al, 8 - s, 0), 1.0)
                be_sh = jnp.where(m, pltpu.roll(be, 8 - s, 0), 0.0)
                be = be + al * be_sh
                al = al * al_sh
            gv = be + al * c
            g_s[sl, :] = gv
            return (a0 * gv)[0:1, :]

        carry[...] = lax.fori_loop(0, ng, step, carry[...])

        gv = g_s[...]
        a = a_s[...]
        mult = jnp.sqrt(_one_minus_exp(2.0 * log_a))
        iv = i_ref[...]
        rv = r_ref[...]
        xcv = xc_ref[...]
        hprev = hext[pl.ds(HALO_C - 1, tm), :]
        dla = gv * hprev * a - (gv * iv * xcv) * (a * a) / mult
        di = gv * mult * xcv
        dpr = (dla * (-LRU_C * sp)) * rv * (1.0 - rv)
        dpi = di * iv * (1.0 - iv)
        dpr_s[...] = dpr
        dpi_s[...] = dpi
        dxc_ref[...] = gv * mult * iv
        dsp = jnp.sum(dla * rv, axis=0, keepdims=True) * (-LRU_C)
        _acc_out(i, dlam_ref, -dsp * jax.nn.sigmoid(-lam_ref[...]))
        _acc_out(i, dbrg_ref, jnp.sum(dpr, axis=0, keepdims=True))
        _acc_out(i, dbig_ref, jnp.sum(dpi, axis=0, keepdims=True))
        for h in range(LRU_HEADS):
            cs = slice(h * LRU_HD, (h + 1) * LRU_HD)
            xh = xc_ref[:, cs].astype(BF16)
            dr_h = dpr_s[:, cs].astype(BF16)
            di_h = dpi_s[:, cs].astype(BF16)
            dxc_ref[:, cs] += (
                lax.dot_general(dr_h, wrg_ref[h], (((1,), (1,)), ((), ())), preferred_element_type=F32)
                + lax.dot_general(di_h, wig_ref[h], (((1,), (1,)), ((), ())), preferred_element_type=F32))
            dwrg_ref[h] += lax.dot_general(xh, dr_h, (((0,), (0,)), ((), ())), preferred_element_type=F32)
            dwig_ref[h] += lax.dot_general(xh, di_h, (((0,), (0,)), ((), ())), preferred_element_type=F32)

    rrow = lambda w: pl.BlockSpec((tm, w), lambda i: (nt - 1 - i, 0))
    hspec = pl.BlockSpec((HALO_C, wl), lambda i: (jnp.maximum((nt - 1 - i) * per - 1, 0), 0))
    wspec = _full((LRU_HEADS, LRU_HD, LRU_HD))
    return pl.pallas_call(
        body, name="odd_bwd_a", grid=(nt,),
        in_specs=[pl.BlockSpec((tm, wl), lambda i: (nt - 1 - i, 1)), rrow(wl), rrow(wl), rrow(wl), rrow(wl), hspec,
                  rrow(wl), wspec, wspec, _full((1, wl)), ANY],
        out_specs=[rrow(wl), rrow(wl), wspec, wspec, _full((1, wl)), _full((1, wl)), _full((1, wl))],
        out_shape=[jax.ShapeDtypeStruct((t, wl), F32), jax.ShapeDtypeStruct((t, wl), BF16),
                   jax.ShapeDtypeStruct((LRU_HEADS, LRU_HD, LRU_HD), F32),
                   jax.ShapeDtypeStruct((LRU_HEADS, LRU_HD, LRU_HD), F32)] + [jax.ShapeDtypeStruct((1, wl), F32)] * 3,
        scratch_shapes=[pltpu.VMEM((tm + HALO_C, wl), F32)] + [pltpu.VMEM((tm, wl), F32)] * 5
                       + [pltpu.VMEM((1, wl), F32)],
        compiler_params=_params("arbitrary"),
    )(p, xc, r, ig, hs, hs, dy, wrg, wig, lam, dep)


def _odd_bwd_b(p, dxc, dgate, ccw, *, tm):
    t = p.shape[0]
    wl = W_LRU
    nt = t // tm

    def body(xr, xrh, dxc_ref, dxcn, dgate_ref, ccw_ref, dp_ref, dcw_ref, dcb_ref, xext, gext):
        i = pl.program_id(0)

        @pl.when(i == 0)
        def _():
            dcw_ref[...] = jnp.zeros_like(dcw_ref)

        xext[0:HALO_C, :] = (i > 0).astype(F32) * xrh[...]
        xext[HALO_C:, :] = xr[...]
        gext[0:tm, :] = dxc_ref[...]
        gext[tm:, :] = (i < nt - 1).astype(F32) * dxcn[...]
        g = dxc_ref[...]
        acc = jnp.zeros((tm, wl), F32)
        for k in range(LRU_CONV_K):
            acc = acc + ccw_ref[k:k + 1, :] * gext[pl.ds(LRU_CONV_K - 1 - k, tm), :]
            dcw_ref[k:k + 1, :] += jnp.sum(
                g * xext[pl.ds(HALO_C - (LRU_CONV_K - 1) + k, tm), :], axis=0, keepdims=True)

        _acc_out(i, dcb_ref, jnp.sum(g, axis=0, keepdims=True))
        dp_ref[:, 0:wl] = acc.astype(BF16)
        dp_ref[:, wl:2 * wl] = dgate_ref[...]

    row = lambda w: pl.BlockSpec((tm, w), lambda i: (i, 0))
    return pl.pallas_call(
        body, name="odd_bwd_b", grid=(nt,),
        in_specs=[_col(tm, wl, 0), _prev_halo(tm, HALO_C, wl, 0), row(wl), _next_halo(tm, HALO_C, wl, 0, t), row(wl),
                  _full((8, wl))],
        out_specs=[row(2 * wl), _full((8, wl)), _full((1, wl))],
        out_shape=[jax.ShapeDtypeStruct((t, 2 * wl), BF16), jax.ShapeDtypeStruct((8, wl), F32),
                   jax.ShapeDtypeStruct((1, wl), F32)],
        scratch_shapes=[pltpu.VMEM((tm + HALO_C, wl), F32), pltpu.VMEM((tm + HALO_C, wl), F32)],
        compiler_params=_params("arbitrary"),
    )(p, p, dxc, dxc, dgate, ccw)


def _local_step(x, target, layer_weights, final_norm, on_grads):
    t = x.shape[0]
    tm, tx, tl, tw = min(TM_MATMUL, t), min(TM_MIXER, t), min(TM_STREAM, t), min(TM_WGRAD, t)
    h = x
    saved = []
    for layer in range(N_LAYERS):
        w = layer_weights(layer, h)
        p, hn = _norm_matmul(h, w["norm"], w["w_in"], w["dep"], tm=tl)
        if layer % 2 == 0:
            y, *acts = _even_fwd(p, w["conv_w"], w["conv_b"], w["ln_g"], w["ln_b"], w["pool_w"], w["pool_b"],
                                 w["pool_scale"], tm=tx)
        else:
            y, *acts = _odd_fwd(p, w["conv_w"], w["conv_b"], w["w_rg"], w["b_rg"], w["w_ig"], w["b_ig"], w["lam"],
                                tm=tx)
        w_out, dep = w["w_out"](y)
        saved.append((w, w_out, h, p, hn, y, acts))
        h = _out_proj(h, y, w_out, dep, tm=tl)
    loss, dh, d_final = _loss_head(h, final_norm, target, tm=tm)

    dep = d_final
    for layer in reversed(range(N_LAYERS)):
        w, w_out, h_in, p, hn, y, acts = saved[layer]
        sfx = "even" if layer % 2 == 0 else "odd"
        dy, dw_out = _out_proj_bwd(dh, y, w_out, dep, tm=tm)
        dep = on_grads(layer, {"w_out_" + sfx: dw_out}, dep, False)
        if layer % 2 == 0:
            u1, e, dmat = acts
            du1, dd, dgat, dlg, dlb, dpb, dps, dpw = _even_bwd_a(p, u1, e, dmat, dy, w["ln_g"], w["ln_b"], w["pool_w"],
                                                                 w["pool_scale"], dep, tm=tx)
            dp, dcw, dcb = _even_bwd_b(p, du1, dd, dgat, w["conv_w"], tm=tx)
            grads = dict(conv_a_w=dcw[:CONV_K], conv_a_b=dcb, ln_a_g=dlg, ln_a_b=dlb, pool_w=dpw, pool_b=dpb,
                         pool_scale=dps)
        else:
            xc, r, ig, hs = acts
            dxc, dgate, dwrg, dwig, dbrg, dbig, dlam = _odd_bwd_a(p, xc, r, ig, hs, dy, w["w_rg"], w["w_ig"],
                                                                  w["lam"], dep, tm=tx)
            dp, dccw, dccb = _odd_bwd_b(p, dxc, dgate, w["conv_w"], tm=tx)
            grads = dict(conv_c_w=dccw[:LRU_CONV_K], conv_c_b=dccb, w_rg=dwrg, b_rg=dbrg, w_ig=dwig, b_ig=dbig,
                         lru_lambda=dlam)
        grads["w_in_" + sfx] = _in_proj_bwd_w(hn, dp, N_DEV, tm=tw)
        dep = on_grads(layer, grads, dep, False)
        dh, dg = _in_proj_bwd_x(dp, w["w_in"], h_in, w["norm"], dh, dep, tm=tm)
        rest = {"norm_" + sfx: dg}
        if layer == N_LAYERS - 1:
            rest["final_norm"] = d_final
        dep = on_grads(layer, rest, dep, True)
    return loss, dh


def _slot(px, py, pc):
    return 4 * px + 2 * py + pc


def _peers(x, y, c):
    return [(1 - x if k & 4 else x, 1 - y if k & 2 else y, 1 - c if k & 1 else c) for k in range(1, N_DEV)]


def _all_gather(arrs, name):
    n = len(arrs)

    def body(*refs):
        ins, outs = refs[:n], refs[n:2 * n]
        send_sems, recv_sems, local_sems = refs[2 * n:]
        x, y, c = lax.axis_index("x"), lax.axis_index("y"), lax.axis_index("c")
        me, sibling = (x, y, c), (x, y, 1 - c)
        chips = [(1 - x, y), (x, 1 - y), (1 - x, 1 - y)]

        def copy(a, k, block, to, src=None):
            rows = outs[a].at[_slot(*block)]
            return pltpu.make_async_remote_copy(
                src_ref=rows if src is None else src, dst_ref=rows, send_sem=send_sems.at[a, k],
                recv_sem=recv_sems.at[a, k], device_id=to, device_id_type=MESH)

        mine = [pltpu.make_async_copy(ins[a], outs[a].at[_slot(*me)], local_sems.at[a]) for a in range(n)]
        for cp in mine:
            cp.start()
        first = []
        for a in range(n):
            first.append(copy(a, 0, me, sibling, src=ins[a]))
            first += [copy(a, 1 + j, me, (*chip, c), src=ins[a]) for j, chip in enumerate(chips)]
        for cp in first:
            cp.start()
        passed = []
        for j, chip in enumerate(chips):
            for a in range(n):
                copy(a, 1 + j, (*chip, c), me).wait_recv()
                fwd = copy(a, 4 + j, (*chip, c), sibling)
                fwd.start()
                passed.append(fwd)
        for a in range(n):
            copy(a, 0, sibling, me).wait_recv()
            for j, chip in enumerate(chips):
                copy(a, 4 + j, (*chip, 1 - c), me).wait_recv()
        for cp in first + passed:
            cp.wait_send()
        for cp in mine:
            cp.wait()

    return pl.pallas_call(
        body, name=name,
        in_specs=[ANY] * n, out_specs=[ANY] * n,
        out_shape=[jax.ShapeDtypeStruct((N_DEV,) + a.shape, a.dtype) for a in arrs],
        scratch_shapes=[pltpu.SemaphoreType.DMA((n, 7)), pltpu.SemaphoreType.DMA((n, 7)),
                        pltpu.SemaphoreType.DMA((n,))],
    )(*arrs)


N_COPIES = {"gather": N_PEERS, "scatter": N_PEERS, "chip_gather": 4, "forward": 3}


def _exchange_plan(mode, x, y, c):
    me = _slot(x, y, c)
    if mode == "forward":
        chips = [(1 - x, y), (x, 1 - y), (1 - x, 1 - y)]
        return [((x, y, 1 - c), ("land", _slot(*chip, c)), _slot(*chip, c), _slot(*chip, 1 - c)) for chip in chips]
    peers = _peers(x, y, c)
    if mode == "chip_gather":
        peers = [(x, y, 1 - c), (1 - x, y, c), (x, 1 - y, c), (1 - x, 1 - y, c)]
    return [(p, ("src", _slot(*p)) if mode == "scatter" else ("src", None), me, _slot(*p)) for p in peers]


def _exchange_copy(src_ref, land_ref, plan, send_sem, recv_sem, start):
    peer, (which, block), there, here = plan
    src = land_ref if which == "land" else src_ref
    return pltpu.make_async_remote_copy(
        src_ref=src if block is None else src.at[block], dst_ref=land_ref.at[there if start else here],
        send_sem=send_sem, recv_sem=recv_sem, device_id=peer, device_id_type=MESH)


def _exchange_start(groups, deps, name):
    flat = [pair for _, g in groups for pair in g]
    n, ng = len(flat), len(groups)

    def body(*refs):
        src_refs, land_refs = refs[:n], refs[n:2 * n]
        outs = refs[2 * n + len(deps):]
        sems, token = outs[:2 * ng], outs[2 * ng + 2 * n]
        x, y, c = lax.axis_index("x"), lax.axis_index("y"), lax.axis_index("c")
        base = 0
        for gi, (mode, g) in enumerate(groups):
            nc = N_COPIES[mode]
            for k, plan in enumerate(_exchange_plan(mode, x, y, c)):
                for ai in range(len(g)):
                    _exchange_copy(src_refs[base + ai], land_refs[base + ai], plan, sems[2 * gi].at[ai * nc + k],
                                   sems[2 * gi + 1].at[ai * nc + k], True).start()
            base += len(g)
        token[...] = jnp.zeros_like(token)

    operands = [pltpu.with_memory_space_constraint(a, pltpu.HBM) for a in
                [s for s, _ in flat] + [l for _, l in flat]]
    out_shape = []
    for mode, g in groups:
        out_shape += [pltpu.SemaphoreType.DMA((len(g) * N_COPIES[mode],))] * 2
    out_shape += [pltpu.HBM(a.shape, a.dtype) for a in operands]
    out_shape.append(jax.ShapeDtypeStruct((8, 128), F32))
    outs = pl.pallas_call(
        body, name=name, out_shape=out_shape,
        in_specs=[HBM] * (2 * n) + [ANY] * len(deps),
        out_specs=[SEM] * (2 * ng) + [HBM] * (2 * n) + [pl.BlockSpec(memory_space=pltpu.VMEM)],
        input_output_aliases={i: 2 * ng + i for i in range(2 * n)},
        compiler_params=pltpu.CompilerParams(has_side_effects=pltpu.SideEffectType.DATAFLOW_SIDE_EFFECTING),
    )(*operands, *deps)
    handles, base = [], 0
    for gi, (mode, g) in enumerate(groups):
        srcs = outs[2 * ng + base:2 * ng + base + len(g)]
        lands = outs[2 * ng + n + base:2 * ng + n + base + len(g)]
        handles.append((mode, outs[2 * gi], outs[2 * gi + 1], list(srcs), list(lands)))
        base += len(g)
    return handles, outs[-1]


def _exchange_wait(handle, after, name):
    mode, send_sems, recv_sems, srcs, lands = handle
    n = len(srcs)
    nc = N_COPIES[mode]

    def body(*refs):
        src_refs, land_refs = refs[:n], refs[n:2 * n]
        send_ref, recv_ref = refs[2 * n], refs[2 * n + 1]
        x, y, c = lax.axis_index("x"), lax.axis_index("y"), lax.axis_index("c")
        for k, plan in enumerate(_exchange_plan(mode, x, y, c)):
            for a in range(n):
                cp = _exchange_copy(src_refs[a], land_refs[a], plan, send_ref.at[a * nc + k], recv_ref.at[a * nc + k],
                                    False)
                cp.wait_send()
                cp.wait_recv()

    outs = pl.pallas_call(
        body, name=name,
        out_shape=[pltpu.HBM(a.shape, a.dtype) for a in srcs + lands],
        in_specs=[HBM] * (2 * n) + [SEM, SEM] + [ANY] * len(after),
        out_specs=[HBM] * (2 * n),
        input_output_aliases={i: i for i in range(2 * n)},
        compiler_params=pltpu.CompilerParams(has_side_effects=pltpu.SideEffectType.DATAFLOW_SIDE_EFFECTING),
    )(*srcs, *lands, send_sems, recv_sems, *after)
    return list(outs[n:])


def _adamw_math(w, g, m, v):
    c1 = 1.0 - ADAM_B1 ** ADAM_STEP
    c2 = 1.0 - ADAM_B2 ** ADAM_STEP
    nm = ADAM_B1 * m + (1.0 - ADAM_B1) * g
    nv = ADAM_B2 * v + (1.0 - ADAM_B2) * (g * g)
    delta = -ADAM_LR * ((nm / c1) / (jnp.sqrt(nv / c2) + ADAM_EPS) + ADAM_WD * w)
    return delta, nm, nv


def _row_tile(r):
    for cand in (256, 128, 64, 32, 16, 8):
        if r % cand == 0 and r > cand:
            return cand
    return r


def _adamw(items, layer0, bufs, name):
    ni = len(items)
    nl = items[0][1].shape[1]
    tiles = [_row_tile(w.shape[1]) for w, _, _, _ in items]
    steps = [w.shape[1] // tr for (w, _, _, _), tr in zip(items, tiles)]
    ns = steps[0]
    assert all(s == ns for s in steps)
    nb = 0 if bufs is None else 4 * ni

    def body(*refs):
        ins, outs = refs[:4 * ni], refs[4 * ni + nb:]
        for k in range(ni):
            w_ref, p_ref, m_ref, v_ref = ins[4 * k:4 * k + 4]
            g = p_ref[0, 0].astype(F32)
            for s in range(1, p_ref.shape[0]):
                g = g + p_ref[s, 0].astype(F32)
            delta, nm, nv = _adamw_math(w_ref[0], g, m_ref[0], v_ref[0])
            g_ref, d_ref, nm_ref, nv_ref = outs[4 * k:4 * k + 4]
            g_ref[0], d_ref[0], nm_ref[0], nv_ref[0] = g, delta, nm, nv

    in_specs, out_specs, out_shape, operands = [], [], [], []
    for (w, parts, m, v), tr in zip(items, tiles):
        blk = pl.BlockSpec((1, tr, w.shape[2]), lambda l, i: (layer0 + l, i, 0))
        in_specs += [blk, pl.BlockSpec((parts.shape[0], 1, tr, w.shape[2]), lambda l, i: (0, l, i, 0)), blk, blk]
        operands += [w, parts, m, v]
        out_specs += [blk] * 4
        out_shape += [jax.ShapeDtypeStruct(w.shape, F32)] * 4
    if bufs is not None:
        in_specs += [ANY] * nb
        operands += [b for item in bufs for b in item]
    outs = pl.pallas_call(
        body, name=name, grid=(nl, ns), in_specs=in_specs, out_specs=out_specs, out_shape=out_shape,
        input_output_aliases={4 * ni + i: i for i in range(nb)},
        compiler_params=_params("arbitrary", "arbitrary"),
    )(*operands)
    return [tuple(outs[4 * k:4 * k + 4]) for k in range(ni)]


NAMES = ("norm_even", "w_in_even", "conv_a_w", "conv_a_b", "ln_a_g", "ln_a_b", "pool_w", "pool_b", "pool_scale",
         "w_out_even", "norm_odd", "w_in_odd", "conv_c_w", "conv_c_b", "w_rg", "b_rg", "w_ig", "b_ig", "lru_lambda",
         "w_out_odd", "final_norm")
SMALL_GATHERED = ("conv_a_w", "pool_b", "norm_odd", "conv_c_w", "conv_c_b", "b_rg", "b_ig", "lru_lambda")
BIG = (("w_in_even", "w_out_even"), ("w_in_odd", "w_out_odd"))
SMALL = (("conv_a_w", "pool_b", "pool_w"), ("norm_odd", "conv_c_w", "conv_c_b", "b_rg", "b_ig", "lru_lambda"))
REPLICATED = (("norm_even", "conv_a_b", "ln_a_g", "ln_a_b", "pool_scale"), ("w_rg", "w_ig"))
PACK_ROW = 1024


def _pack_rows(flat2d):
    pad = (-flat2d.shape[1]) % PACK_ROW
    return jnp.pad(flat2d, ((0, 0), (0, pad))).reshape(flat2d.shape[0], -1, 128)


def _unpack(flat, shapes):
    out, off = [], 0
    for s in shapes:
        n = 1
        for d in s:
            n *= d
        out.append(flat[..., off:off + n].reshape(flat.shape[:-1] + tuple(s)))
        off += n
    return out


def _to_global(name, g):
    if name in ("conv_a_w", "pool_b", "conv_c_w"):
        return jnp.transpose(g, (1, 2, 0, 3)).reshape(g.shape[1], g.shape[2], -1)
    if name == "pool_w":
        return jnp.transpose(g, (1, 2, 0, 3, 4)).reshape(2, 4, POOL_GW, POOL_GW)
    return jnp.transpose(g, (1, 0, 2)).reshape(g.shape[1], -1)


def _to_blocks(name, g):
    if name == "conv_a_w":
        return jnp.transpose(g.reshape(CONV_K, N_DEV, -1), (1, 0, 2))
    if name == "conv_c_w":
        return jnp.transpose(g.reshape(LRU_CONV_K, N_DEV, -1), (1, 0, 2))
    if name == "pool_b":
        return jnp.transpose(g.reshape(4, N_DEV, -1), (1, 0, 2))
    if name == "pool_w":
        return jnp.transpose(g.reshape(4, N_DEV, POOL_GW // N_DEV, POOL_GW), (1, 0, 2, 3))
    return g.reshape(N_DEV, -1)


def _as3d(a):
    if a.ndim == 1:
        return a.reshape(1, 1, -1)
    if a.ndim == 2:
        return a.reshape(a.shape[0], 1, a.shape[1])
    return a.reshape(a.shape[0], -1, a.shape[-1])


def kernel(x, norm_even, w_in_even, conv_a_w, conv_a_b, ln_a_g, ln_a_b, pool_w, pool_b, pool_scale, w_out_even, norm_odd, w_in_odd, conv_c_w, conv_c_b, w_rg, b_rg, w_ig, b_ig, lru_lambda, w_out_odd, final_norm, loss_target, m_norm_even, m_w_in_even, m_conv_a_w, m_conv_a_b, m_ln_a_g, m_ln_a_b, m_pool_w, m_pool_b, m_pool_scale, m_w_out_even, m_norm_odd, m_w_in_odd, m_conv_c_w, m_conv_c_b, m_w_rg, m_b_rg, m_w_ig, m_b_ig, m_lru_lambda, m_w_out_odd, m_final_norm, v_norm_even, v_w_in_even, v_conv_a_w, v_conv_a_b, v_ln_a_g, v_ln_a_b, v_pool_w, v_pool_b, v_pool_scale, v_w_out_even, v_norm_odd, v_w_in_odd, v_conv_c_w, v_conv_c_b, v_w_rg, v_b_rg, v_w_ig, v_b_ig, v_lru_lambda, v_w_out_odd, v_final_norm):
    w_loc = dict(zip(NAMES, [norm_even, w_in_even, conv_a_w, conv_a_b, ln_a_g, ln_a_b, pool_w, pool_b, pool_scale,
                             w_out_even, norm_odd, w_in_odd, conv_c_w, conv_c_b, w_rg, b_rg, w_ig, b_ig, lru_lambda,
                             w_out_odd, final_norm]))
    m_loc = dict(zip(NAMES, [m_norm_even, m_w_in_even, m_conv_a_w, m_conv_a_b, m_ln_a_g, m_ln_a_b, m_pool_w, m_pool_b,
                             m_pool_scale, m_w_out_even, m_norm_odd, m_w_in_odd, m_conv_c_w, m_conv_c_b, m_w_rg,
                             m_b_rg, m_w_ig, m_b_ig, m_lru_lambda, m_w_out_odd, m_final_norm]))
    v_loc = dict(zip(NAMES, [v_norm_even, v_w_in_even, v_conv_a_w, v_conv_a_b, v_ln_a_g, v_ln_a_b, v_pool_w, v_pool_b,
                             v_pool_scale, v_w_out_even, v_norm_odd, v_w_in_odd, v_conv_c_w, v_conv_c_b, v_w_rg,
                             v_b_rg, v_w_ig, v_b_ig, v_lru_lambda, v_w_out_odd, v_final_norm]))
    me = _slot(lax.axis_index("x"), lax.axis_index("y"), lax.axis_index("c"))

    def landing(own):
        zone = lax.empty((N_DEV,) + own.shape[1:], own.dtype)
        return lax.dynamic_update_slice(zone, own, (me,) + (0,) * (own.ndim - 1))

    small_shapes = [w_loc[n].shape for n in SMALL_GATHERED]
    small = jnp.concatenate([w_loc[n].reshape(1, -1) for n in SMALL_GATHERED], axis=1)
    first = _all_gather([w_in_even[0].astype(BF16), pool_w.astype(BF16), _pack_rows(small)[0]], "gather_first")
    g_small = dict(zip(SMALL_GATHERED, [_to_global(n, g) for n, g in
                                        zip(SMALL_GATHERED, _unpack(first[2].reshape(N_DEV, -1), small_shapes))]))
    pool_w_all = _to_global("pool_w", first[1])

    def pairs_of(shards):
        return [(s.astype(BF16), landing(s.astype(BF16)[None])) for s in shards]

    shards = {1: [w_in_odd[0], w_out_odd[0]], 2: [w_in_even[1], w_out_even[1]], 3: [w_in_odd[1], w_out_odd[1]]}
    leg_a, leg_b = {}, {}
    (w_out_0, leg_a[1]), token_1 = _exchange_start(
        [("gather", pairs_of([w_out_even[0]])), ("chip_gather", pairs_of(shards[1]))], [first[0]], "gather_start_1")
    unused = jnp.zeros((8, 128), F32)

    def second_leg(layer, y):
        lands = _exchange_wait(leg_a[layer], [y], f"gather_wait_a_{layer}")
        (leg_b[layer],), token = _exchange_start([("forward", [(unused, l) for l in lands])], [],
                                                 f"gather_forward_{layer}")
        return token

    def layer_weights(layer, h):
        j = layer // 2
        dep = h
        if layer == 0:
            w_in, dep = first[0], token_1
            w_out_now = lambda y: _exchange_wait(w_out_0, [y], "gather_wait_out_0")[0]
        else:
            w_in, w_out_got = _exchange_wait(leg_b[layer], [h], f"gather_wait_b_{layer}")
            w_out_now = lambda y: w_out_got
            if layer + 1 in shards:
                (leg_a[layer + 1],), dep = _exchange_start([("chip_gather", pairs_of(shards[layer + 1]))], [w_in],
                                                           f"gather_start_{layer + 1}")

        def w_out(y):
            return w_out_now(y), (second_leg(layer + 1, y) if layer + 1 in shards else y)

        w_in = jnp.transpose(w_in, (1, 0, 2)).reshape(D_MODEL, -1)
        row = lambda a: a[j][None]
        if layer % 2 == 0:
            return dict(dep=dep, norm=row(norm_even), w_in=w_in,
                        w_out=lambda y: (lambda wo, d: (wo.reshape(W_EVEN_MIX, D_MODEL), d))(*w_out(y)),
                        conv_w=jnp.pad(g_small["conv_a_w"][j], ((0, 1), (0, 0))), conv_b=row(conv_a_b),
                        ln_g=row(ln_a_g), ln_b=row(ln_a_b), pool_w=pool_w_all[j],
                        pool_b=g_small["pool_b"][j].reshape(1, W_POOL), pool_scale=row(pool_scale))
        return dict(dep=dep, norm=row(g_small["norm_odd"]), w_in=w_in,
                    w_out=lambda y: (lambda wo, d: (wo.reshape(W_LRU, D_MODEL), d))(*w_out(y)),
                    conv_w=jnp.pad(g_small["conv_c_w"][j], ((0, 4), (0, 0))), conv_b=row(g_small["conv_c_b"]),
                    w_rg=w_rg[j].astype(BF16), b_rg=row(g_small["b_rg"]), w_ig=w_ig[j].astype(BF16),
                    b_ig=row(g_small["b_ig"]), lam=row(g_small["lru_lambda"]))

    pending, exchanges, small_layout = {}, {}, {}
    last_token = []

    def on_grads(layer, grads, dep, last):
        par = layer % 2
        w_in_name, w_out_name = BIG[par]
        have = pending.setdefault(layer, {})
        have.update(grads)
        eager = layer == 0
        scatter, gather = {}, {}
        if w_out_name in have and (eager or last):
            scatter["out"] = have.pop(w_out_name).reshape(N_DEV, -1, D_MODEL)
        if w_in_name in have and (eager or last):
            scatter["in"] = have.pop(w_in_name)
        if layer not in small_layout and all(n in have for n in SMALL[par]) and (eager or last):
            blocks = [_to_blocks(n, have[n]) for n in SMALL[par]]
            small_layout[layer] = [b.shape[1:] for b in blocks]
            scatter["small"] = _pack_rows(jnp.concatenate([b.reshape(N_DEV, -1) for b in blocks], axis=1))
        for n in REPLICATED[1]:
            if n in have:
                gather[n] = have.pop(n).astype(BF16).reshape(-1, LRU_HD)
        if last:
            vectors = [have[n] for n in REPLICATED[par]] if par == 0 else []
            if "final_norm" in have:
                vectors.append(have["final_norm"])
            if vectors:
                gather["rep32"] = jnp.concatenate([v.reshape(-1) for v in vectors]).reshape(-1, 128)
        groups = []
        if scatter:
            groups.append(("scatter", [(s, landing(lax.dynamic_slice_in_dim(s, me, 1, 0))) for s in scatter.values()]))
        if gather:
            groups.append(("gather", [(s, landing(s[None])) for s in gather.values()]))
        if not groups:
            return dep
        keys = [list(d) for d in (scatter, gather) if d]
        handles, token = _exchange_start(groups, [], f"grads_start_{layer}_{'_'.join(k for ks in keys for k in ks)}")
        exchanges.setdefault(layer, []).extend(zip(keys, handles))
        last_token[:] = [token]
        return token

    loss, grad_x = _local_step(x[0], loss_target[0], layer_weights, final_norm[None], on_grads)

    w3 = {n: _as3d(w_loc[n]) for n in NAMES}
    m3 = {n: _as3d(m_loc[n]) for n in NAMES}
    v3 = {n: _as3d(v_loc[n]) for n in NAMES}
    results = {}
    after = list(last_token)
    for layer in (3, 2, 1, 0):
        par, j = layer % 2, layer // 2
        got = {}
        for keys, handle in exchanges[layer]:
            got.update(zip(keys, _exchange_wait(handle, after, f"grads_wait_{layer}_{'_'.join(keys)}")))
        parts = {BIG[par][0]: got["in"], BIG[par][1]: got["out"]}
        parts.update(zip(SMALL[par], _unpack(got["small"].reshape(N_DEV, -1), small_layout[layer])))
        if par == 1:
            parts.update({n: got[n] for n in REPLICATED[1]})
        else:
            parts.update(zip(REPLICATED[0], _unpack(got["rep32"].reshape(N_DEV, -1)[:, :len(REPLICATED[0]) * D_MODEL],
                                                    [w_loc[n].shape[1:] for n in REPLICATED[0]])))
        big_items, small_items = [], []
        for n, pt in parts.items():
            item = (w3[n], pt.reshape((N_DEV, 1) + w3[n].shape[1:]), m3[n], v3[n])
            (big_items if w3[n].shape[1] >= 128 else small_items).append((n, item))
        for n, item in big_items:
            results[n] = _adamw([item], j, [results[n]] if n in results else None, f"adamw_{n}_{layer}")[0]
        snames = [n for n, _ in small_items]
        prev = [results[n] for n in snames] if snames[0] in results else None
        for n, r in zip(snames, _adamw([it for _, it in small_items], j, prev, f"adamw_small_{layer}")):
            results[n] = r
        if layer == N_LAYERS - 1:
            item = (w3["final_norm"], got["rep32"].reshape(N_DEV, 1, 1, -1)[..., :D_MODEL], m3["final_norm"],
                    v3["final_norm"])
            results["final_norm"] = _adamw([item], 0, None, "adamw_final_norm")[0]
        after = [results[BIG[par][0]][1]]

    total = lax.psum(loss[0, 0], ("x", "y", "c"))
    outs = [[results[n][k].reshape(w_loc[n].shape) for n in NAMES] for k in range(4)]
    return (total, grad_x[None], *outs[0], *outs[1], *outs[2], *outs[3])
```

```python
import functools

import jax
import jax.numpy as jnp
from jax import lax
from jax.experimental import pallas as pl
from jax.experimental.pallas import tpu as pltpu

F32 = jnp.float32
BF16 = jnp.bfloat16

N_DEV = 8
N_PEERS = N_DEV - 1
N_LAYERS = 4
D_MODEL = 1024
EPS_RMS = 1e-6
EPS_LN = 1e-5
W_CONV = 1024
CONV_K = 31
W_POOL = 1024
POOL_WINDOWS = (2, 4, 8, 16)
POOL_GW = 256
W_EVEN_IN = 5120
W_EVEN_MIX = 2048
LRU_HEADS = 12
LRU_HD = 128
W_LRU = 1536
LRU_CONV_K = 4
LRU_C = 8.0
ADAM_LR = 0.001
ADAM_B1 = 0.9
ADAM_B2 = 0.999
ADAM_EPS = 1e-08
ADAM_WD = 0.01
ADAM_STEP = 10

HALO = 32
HALO_C = 8
TM_MATMUL = 512
TM_STREAM = 1024
TM_WGRAD = 2048
TM_MIXER = 256
CONV_ROWS = 128
VMEM_LIMIT = 56 * 1024 * 1024
MESH = pl.DeviceIdType.MESH
ANY = pl.BlockSpec(memory_space=pl.ANY)
HBM = pl.BlockSpec(memory_space=pltpu.HBM)
SEM = pl.BlockSpec(memory_space=pltpu.SEMAPHORE)


def _params(*sem):
    return pltpu.CompilerParams(dimension_semantics=sem, vmem_limit_bytes=VMEM_LIMIT)


def _sigmoid(z):
    return 0.5 * jnp.tanh(0.5 * z) + 0.5


def _dsilu(z, s):
    return s * (1.0 + z * (1.0 - s))


def _full(shape):
    nd = len(shape)
    return pl.BlockSpec(shape, lambda *_: (0,) * nd)


def _norm_matmul(h, g, w, dep, *, tm):
    t, d = h.shape
    n = w.shape[1]
    tn = n // 4

    def body(h_ref, g_ref, w_ref, dep_ref, p_ref, hn_ref):
        @pl.when(pl.program_id(1) == 0)
        def _():
            x = h_ref[...]
            r = lax.rsqrt(jnp.mean(x * x, axis=-1, keepdims=True) + EPS_RMS)
            hn_ref[...] = ((x * r) * g_ref[...]).astype(BF16)

        p_ref[...] = jnp.dot(hn_ref[...], w_ref[...], preferred_element_type=F32)

    return pl.pallas_call(
        body, name="norm_matmul", grid=(t // tm, n // tn),
        in_specs=[pl.BlockSpec((tm, d), lambda i, j: (i, 0)), _full((1, d)),
                  pl.BlockSpec((d, tn), lambda i, j: (0, j)), ANY],
        out_specs=[pl.BlockSpec((tm, tn), lambda i, j: (i, j)), pl.BlockSpec((tm, d), lambda i, j: (i, 0))],
        out_shape=[jax.ShapeDtypeStruct((t, n), F32), jax.ShapeDtypeStruct((t, d), BF16)],
        compiler_params=_params("arbitrary", "arbitrary"),
    )(h, g, w, dep)


def _out_proj(h, y, w, dep, *, tm):
    t, d = h.shape
    k = y.shape[1]

    def body(h_ref, y_ref, w_ref, dep_ref, o_ref):
        o_ref[...] = h_ref[...] + jnp.dot(y_ref[...], w_ref[...], preferred_element_type=F32)

    return pl.pallas_call(
        body, name="out_proj", grid=(t // tm,),
        in_specs=[pl.BlockSpec((tm, d), lambda i: (i, 0)), pl.BlockSpec((tm, k), lambda i: (i, 0)), _full((k, d)),
                  ANY],
        out_specs=pl.BlockSpec((tm, d), lambda i: (i, 0)),
        out_shape=jax.ShapeDtypeStruct((t, d), F32),
        compiler_params=_params("arbitrary"),
    )(h, y, w, dep)


def _out_proj_bwd(dh, y, w, dep, *, tm):
    t, d = dh.shape
    k = y.shape[1]
    nt = t // tm

    def body(dh_ref, y_ref, w_ref, dep_ref, dy_ref, dw_ref, acc):
        i = pl.program_id(0)
        g = dh_ref[...].astype(BF16)
        dy_ref[...] = lax.dot_general(g, w_ref[...], (((1,), (1,)), ((), ())), preferred_element_type=F32)
        part = lax.dot_general(y_ref[...], g, (((0,), (0,)), ((), ())), preferred_element_type=F32)

        @pl.when(i == 0)
        def _():
            acc[...] = part

        @pl.when(i > 0)
        def _():
            acc[...] += part

        @pl.when(i == nt - 1)
        def _():
            dw_ref[...] = acc[...].astype(BF16)

    return pl.pallas_call(
        body, name="out_proj_bwd", grid=(nt,),
        in_specs=[pl.BlockSpec((tm, d), lambda i: (i, 0)), pl.BlockSpec((tm, k), lambda i: (i, 0)), _full((k, d)),
                  ANY],
        out_specs=[pl.BlockSpec((tm, k), lambda i: (i, 0)), _full((k, d))],
        out_shape=[jax.ShapeDtypeStruct((t, k), F32), jax.ShapeDtypeStruct((k, d), BF16)],
        scratch_shapes=[pltpu.VMEM((k, d), F32)],
        compiler_params=_params("arbitrary"),
    )(dh, y, w, dep)


def _in_proj_bwd_x(dp, w, h, g, dh_out, dep, *, tm):
    t, d = h.shape
    n = w.shape[1]
    nt = t // tm

    def body(dp_ref, w_ref, h_ref, g_ref, dho_ref, dep_ref, dh_ref, dg_ref):
        i = pl.program_id(0)
        dy = lax.dot_general(dp_ref[...], w_ref[...], (((1,), (1,)), ((), ())), preferred_element_type=F32)
        x = h_ref[...]
        r = lax.rsqrt(jnp.mean(x * x, axis=-1, keepdims=True) + EPS_RMS)
        gd = dy * g_ref[...]
        m = jnp.mean(gd * x, axis=-1, keepdims=True)
        dh_ref[...] = dho_ref[...] + r * gd - x * (r * r * r * m)
        _acc_out(i, dg_ref, jnp.sum(dy * x * r, axis=0, keepdims=True))

    return pl.pallas_call(
        body, name="in_proj_bwd_x", grid=(nt,),
        in_specs=[pl.BlockSpec((tm, n), lambda i: (i, 0)), _full((d, n)),
                  pl.BlockSpec((tm, d), lambda i: (i, 0)), _full((1, d)), pl.BlockSpec((tm, d), lambda i: (i, 0)),
                  ANY],
        out_specs=[pl.BlockSpec((tm, d), lambda i: (i, 0)), _full((1, d))],
        out_shape=[jax.ShapeDtypeStruct((t, d), F32), jax.ShapeDtypeStruct((1, d), F32)],
        compiler_params=_params("arbitrary"),
    )(dp, w, h, g, dh_out, dep)


def _in_proj_bwd_w(hn, dp, nd, *, tm):
    t, d = hn.shape
    nb = dp.shape[1] // nd
    nt = t // tm

    def body(hn_ref, dp_ref, dw_ref, acc):
        i = pl.program_id(1)
        part = lax.dot_general(hn_ref[...], dp_ref[...], (((0,), (0,)), ((), ())), preferred_element_type=F32)

        @pl.when(i == 0)
        def _():
            acc[...] = part

        @pl.when(i > 0)
        def _():
            acc[...] += part

        @pl.when(i == nt - 1)
        def _():
            dw_ref[0] = acc[...].astype(BF16)

    return pl.pallas_call(
        body, name="in_proj_bwd_w", grid=(nd, nt),
        in_specs=[pl.BlockSpec((tm, d), lambda j, i: (i, 0)), pl.BlockSpec((tm, nb), lambda j, i: (i, j))],
        out_specs=pl.BlockSpec((1, d, nb), lambda j, i: (j, 0, 0)),
        out_shape=jax.ShapeDtypeStruct((nd, d, nb), BF16),
        scratch_shapes=[pltpu.VMEM((d, nb), F32)],
        compiler_params=_params("arbitrary", "arbitrary"),
    )(hn, dp)


def _loss_head(h, g, target, *, tm):
    t, d = h.shape
    nt = t // tm

    def body(h_ref, g_ref, t_ref, loss_ref, dh_ref, dg_ref):
        i = pl.program_id(0)
        x = h_ref[...]
        r = lax.rsqrt(jnp.mean(x * x, axis=-1, keepdims=True) + EPS_RMS)
        xr = x * r
        err = xr * g_ref[...] - t_ref[...]
        lp = 0.5 * jnp.sum(jnp.mean(err * err, axis=-1, keepdims=True), axis=0, keepdims=True)
        dy = err * (1.0 / d)
        gd = dy * g_ref[...]
        m = jnp.mean(gd * x, axis=-1, keepdims=True)
        dh_ref[...] = r * gd - x * (r * r * r * m)
        dgp = jnp.sum(dy * xr, axis=0, keepdims=True)

        @pl.when(i == 0)
        def _():
            loss_ref[...] = lp
            dg_ref[...] = dgp

        @pl.when(i > 0)
        def _():
            loss_ref[...] += lp
            dg_ref[...] += dgp

    return pl.pallas_call(
        body, name="loss_head", grid=(nt,),
        in_specs=[pl.BlockSpec((tm, d), lambda i: (i, 0)), _full((1, d)), pl.BlockSpec((tm, d), lambda i: (i, 0))],
        out_specs=[_full((1, 1)), pl.BlockSpec((tm, d), lambda i: (i, 0)), _full((1, d))],
        out_shape=[jax.ShapeDtypeStruct((1, 1), F32), jax.ShapeDtypeStruct((t, d), F32),
                   jax.ShapeDtypeStruct((1, d), F32)],
        compiler_params=_params("arbitrary"),
    )(h, g, target)


def _col(tm, w, c):
    return pl.BlockSpec((tm, w), lambda i: (i, c))


def _prev_halo(tm, rows, w, c):
    per = tm // rows
    return pl.BlockSpec((rows, w), lambda i: (jnp.maximum(i * per - 1, 0), c))


def _next_halo(tm, rows, w, c, t):
    per = tm // rows
    last = t // rows - 1
    return pl.BlockSpec((rows, w), lambda i: (jnp.minimum((i + 1) * per, last), c))


def _inv_count(first_row, rows, window):
    tpos = first_row + lax.broadcasted_iota(jnp.int32, (rows, 1), 0)
    return 1.0 / jnp.minimum(tpos + 1, window).astype(F32)


def _even_fwd(p, cw, cb, lg, lb, pw, pb, ps, *, tm):
    t = p.shape[0]
    wc = W_CONV

    def body(av, ag, agate, bv, bgate, avh, agh, bvh, cw_ref, cb_ref, lg_ref, lb_ref, pw_ref, pb_ref, ps_ref,
             y_ref, u1_ref, e_ref, d_ref, uext, vext):
        i = pl.program_id(0)
        keep = (i > 0).astype(F32)
        uext[0:HALO, :] = keep * (avh[...] * _sigmoid(agh[...]))
        uext[HALO:, :] = av[...] * _sigmoid(ag[...])
        vext[0:HALO, :] = keep * bvh[...]
        vext[HALO:, :] = bv[...]
        for c in range(0, wc, 128):
            for rb in range(0, tm, CONV_ROWS):
                acc = jnp.broadcast_to(cb_ref[:, c:c + 128], (CONV_ROWS, 128))
                for r in range(8):
                    shifted = uext[pl.ds(rb + 8 - r, CONV_ROWS + HALO - 8), c:c + 128]
                    for q in range(HALO // 8):
                        s = 8 * q + r
                        if s < CONV_K:
                            acc = acc + cw_ref[CONV_K - 1 - s:CONV_K - s, c:c + 128] * shifted[24 - 8 * q:24 - 8 * q + CONV_ROWS]
                u1_ref[rb:rb + CONV_ROWS, c:c + 128] = acc
        u1 = u1_ref[...]
        mu = jnp.mean(u1, axis=-1, keepdims=True)
        xc = u1 - mu
        rs = lax.rsqrt(jnp.mean(xc * xc, axis=-1, keepdims=True) + EPS_LN)
        u2 = (xc * rs) * lg_ref[...] + lb_ref[...]
        u3 = u2 * _sigmoid(u2)
        ga = agate[...]
        y_ref[:, 0:wc] = (u3 * (ga * _sigmoid(ga))).astype(BF16)
        for g, win in enumerate(POOL_WINDOWS):
            cs = slice(g * POOL_GW, (g + 1) * POOL_GW)
            s = vext[pl.ds(HALO, tm), cs]
            for j in range(1, win):
                s = s + vext[pl.ds(HALO - j, tm), cs]
            dg = s * _inv_count(i * tm, tm, win) - vext[pl.ds(HALO, tm), cs]
            dgb = dg.astype(BF16)
            d_ref[:, cs] = dgb
            eg = jnp.dot(dgb, pw_ref[g], preferred_element_type=F32) + pb_ref[:, cs]
            e_ref[:, cs] = eg
            gb = bgate[:, cs]
            y_ref[:, wc + g * POOL_GW:wc + (g + 1) * POOL_GW] = ((eg * ps_ref[:, cs]) * (gb * _sigmoid(gb))).astype(BF16)

    row = lambda w: pl.BlockSpec((tm, w), lambda i: (i, 0))
    return pl.pallas_call(
        body, name="even_fwd", grid=(t // tm,),
        in_specs=[_col(tm, wc, 0), _col(tm, wc, 1), _col(tm, wc, 2), _col(tm, wc, 3), _col(tm, wc, 4),
                  _prev_halo(tm, HALO, wc, 0), _prev_halo(tm, HALO, wc, 1), _prev_halo(tm, HALO, wc, 3),
                  _full((32, wc)), _full((1, wc)), _full((1, wc)), _full((1, wc)),
                  _full((4, POOL_GW, POOL_GW)), _full((1, wc)), _full((1, wc))],
        out_specs=[row(2 * wc), row(wc), row(wc), row(wc)],
        out_shape=[jax.ShapeDtypeStruct((t, 2 * wc), BF16), jax.ShapeDtypeStruct((t, wc), F32),
                   jax.ShapeDtypeStruct((t, wc), F32), jax.ShapeDtypeStruct((t, wc), BF16)],
        scratch_shapes=[pltpu.VMEM((tm + HALO, wc), F32), pltpu.VMEM((tm + HALO, wc), F32)],
        compiler_params=_params("arbitrary"),
    )(p, p, p, p, p, p, p, p, cw, cb, lg, lb, pw, pb, ps)


def _acc_out(i, ref, val):
    @pl.when(i == 0)
    def _():
        ref[...] = val

    @pl.when(i > 0)
    def _():
        ref[...] += val


def _even_bwd_a(p, u1, e, dmat, dy, lg, lb, pw, ps, dep, *, tm):
    t = p.shape[0]
    wc = W_CONV

    def body(agate, bgate, u1_ref, e_ref, d_ref, dya, dyb, lg_ref, lb_ref, pw_ref, ps_ref, dep_ref,
             du1_ref, dd_ref, dgat_ref, dlg_ref, dlb_ref, dpb_ref, dps_ref, dpw_ref):
        i = pl.program_id(0)

        @pl.when(i == 0)
        def _():
            dpw_ref[...] = jnp.zeros_like(dpw_ref)

        u1 = u1_ref[...]
        mu = jnp.mean(u1, axis=-1, keepdims=True)
        xc = u1 - mu
        rs = lax.rsqrt(jnp.mean(xc * xc, axis=-1, keepdims=True) + EPS_LN)
        xh = xc * rs
        u2 = xh * lg_ref[...] + lb_ref[...]
        s2 = _sigmoid(u2)
        ga = agate[...]
        sa = _sigmoid(ga)
        dy_a = dya[...]
        dgat_ref[:, 0:wc] = (dy_a * (u2 * s2) * _dsilu(ga, sa)).astype(BF16)
        du2 = dy_a * (ga * sa) * _dsilu(u2, s2)
        _acc_out(i, dlg_ref, jnp.sum(du2 * xh, axis=0, keepdims=True))
        _acc_out(i, dlb_ref, jnp.sum(du2, axis=0, keepdims=True))
        dxh = du2 * lg_ref[...]
        m1 = jnp.mean(dxh, axis=-1, keepdims=True)
        m2 = jnp.mean(dxh * xh, axis=-1, keepdims=True)
        du1_ref[...] = rs * (dxh - m1 - xh * m2)

        gb = bgate[...]
        sb = _sigmoid(gb)
        ev = e_ref[...]
        dy_b = dyb[...]
        dgat_ref[:, wc:2 * wc] = (dy_b * (ev * ps_ref[...]) * _dsilu(gb, sb)).astype(BF16)
        dz = dy_b * (gb * sb)
        _acc_out(i, dps_ref, jnp.sum(dz * ev, axis=0, keepdims=True))
        de = dz * ps_ref[...]
        _acc_out(i, dpb_ref, jnp.sum(de, axis=0, keepdims=True))
        for g in range(len(POOL_WINDOWS)):
            cs = slice(g * POOL_GW, (g + 1) * POOL_GW)
            deg = de[:, cs].astype(BF16)
            dd_ref[:, cs] = lax.dot_general(deg, pw_ref[g], (((1,), (1,)), ((), ())), preferred_element_type=F32)
            dpw_ref[g] += lax.dot_general(d_ref[:, cs], deg, (((0,), (0,)), ((), ())), preferred_element_type=F32)

    row = lambda w: pl.BlockSpec((tm, w), lambda i: (i, 0))
    return pl.pallas_call(
        body, name="even_bwd_a", grid=(t // tm,),
        in_specs=[_col(tm, wc, 2), _col(tm, wc, 4), row(wc), row(wc), row(wc), _col(tm, wc, 0), _col(tm, wc, 1),
                  _full((1, wc)), _full((1, wc)), _full((4, POOL_GW, POOL_GW)), _full((1, wc)), ANY],
        out_specs=[row(wc), row(wc), row(2 * wc), _full((1, wc)), _full((1, wc)), _full((1, wc)), _full((1, wc)),
                   _full((4, POOL_GW, POOL_GW))],
        out_shape=[jax.ShapeDtypeStruct((t, wc), F32), jax.ShapeDtypeStruct((t, wc), F32),
                   jax.ShapeDtypeStruct((t, 2 * wc), BF16)] + [jax.ShapeDtypeStruct((1, wc), F32)] * 4
                  + [jax.ShapeDtypeStruct((4, POOL_GW, POOL_GW), F32)],
        compiler_params=_params("arbitrary"),
    )(p, p, u1, e, dmat, dy, dy, lg, lb, pw, ps, dep)


def _even_bwd_b(p, du1, dd, dgat, cw, *, tm):
    t = p.shape[0]
    wc = W_CONV
    nt = t // tm

    def body(av, ag, avh, agh, du1_ref, du1n, dd_ref, ddn, dgat_ref, cw_ref, dp_ref, dcw_ref, dcb_ref,
             uext, gext, dext, du0, dcw8):
        i = pl.program_id(0)
        keep_p = (i > 0).astype(F32)
        keep_n = (i < nt - 1).astype(F32)

        @pl.when(i == 0)
        def _():
            dcw8[...] = jnp.zeros_like(dcw8)

        a = av[...]
        sg = _sigmoid(ag[...])
        uext[0:HALO, :] = keep_p * (avh[...] * _sigmoid(agh[...]))
        uext[HALO:, :] = a * sg
        gext[0:tm, :] = du1_ref[...]
        gext[tm:, :] = keep_n * du1n[...]
        for c in range(0, wc, 128):
            for rb in range(0, tm, CONV_ROWS):
                acc = jnp.zeros((CONV_ROWS, 128), F32)
                for r in range(8):
                    ahead = gext[pl.ds(rb + r, CONV_ROWS + HALO - 8), c:c + 128]
                    for q in range(HALO // 8):
                        s = 8 * q + r
                        if s < CONV_K:
                            acc = acc + cw_ref[CONV_K - 1 - s:CONV_K - s, c:c + 128] * ahead[8 * q:8 * q + CONV_ROWS]
                du0[rb:rb + CONV_ROWS, c:c + 128] = acc
                gcur = du1_ref[rb:rb + CONV_ROWS, c:c + 128]
                for r in range(8):
                    behind = uext[pl.ds(rb + 8 - r, CONV_ROWS + HALO - 8), c:c + 128]
                    for q in range(HALO // 8):
                        s = 8 * q + r
                        if s < CONV_K:
                            prod = gcur * behind[24 - 8 * q:24 - 8 * q + CONV_ROWS]
                            part = prod[0:8]
                            for o in range(8, CONV_ROWS, 8):
                                part = part + prod[o:o + 8]
                            k = CONV_K - 1 - s
                            dcw8[8 * k:8 * k + 8, c:c + 128] += part

        @pl.when(i == nt - 1)
        def _():
            for k in range(CONV_K):
                dcw_ref[k:k + 1, :] = jnp.sum(dcw8[8 * k:8 * k + 8, :], axis=0, keepdims=True)
            dcw_ref[CONV_K:32, :] = jnp.zeros((32 - CONV_K, wc), F32)

        _acc_out(i, dcb_ref, jnp.sum(du1_ref[...], axis=0, keepdims=True))
        g0 = du0[...]
        dp_ref[:, 0:wc] = (g0 * sg).astype(BF16)
        dp_ref[:, wc:2 * wc] = (g0 * a * sg * (1.0 - sg)).astype(BF16)
        dp_ref[:, 2 * wc:3 * wc] = dgat_ref[:, 0:wc]
        dp_ref[:, 4 * wc:5 * wc] = dgat_ref[:, wc:2 * wc]
        for g, win in enumerate(POOL_WINDOWS):
            cs = slice(g * POOL_GW, (g + 1) * POOL_GW)
            dext[0:tm, cs] = dd_ref[:, cs] * _inv_count(i * tm, tm, win)
            dext[tm:, cs] = keep_n * (ddn[:, cs] * _inv_count((i + 1) * tm, HALO, win))
            s = dext[pl.ds(0, tm), cs]
            for j in range(1, win):
                s = s + dext[pl.ds(j, tm), cs]
            dp_ref[:, 3 * wc + g * POOL_GW:3 * wc + (g + 1) * POOL_GW] = (s - dd_ref[:, cs]).astype(BF16)

    row = lambda w: pl.BlockSpec((tm, w), lambda i: (i, 0))
    return pl.pallas_call(
        body, name="even_bwd_b", grid=(nt,),
        in_specs=[_col(tm, wc, 0), _col(tm, wc, 1), _prev_halo(tm, HALO, wc, 0), _prev_halo(tm, HALO, wc, 1),
                  row(wc), _next_halo(tm, HALO, wc, 0, t), row(wc), _next_halo(tm, HALO, wc, 0, t), row(2 * wc),
                  _full((32, wc))],
        out_specs=[row(5 * wc), _full((32, wc)), _full((1, wc))],
        out_shape=[jax.ShapeDtypeStruct((t, 5 * wc), BF16), jax.ShapeDtypeStruct((32, wc), F32),
                   jax.ShapeDtypeStruct((1, wc), F32)],
        scratch_shapes=[pltpu.VMEM((tm + HALO, wc), F32), pltpu.VMEM((tm + HALO, wc), F32),
                        pltpu.VMEM((tm + HALO, wc), F32), pltpu.VMEM((tm, wc), F32), pltpu.VMEM((8 * 32, wc), F32)],
        compiler_params=_params("arbitrary"),
    )(p, p, p, p, du1, du1, dd, dd, dgat, cw)


def _softplus_neg(lam):
    z = -lam
    return jnp.maximum(z, 0.0) + jnp.log1p(jnp.exp(-jnp.abs(z)))


def _one_minus_exp(x):
    series = -x * (1.0 + x * (0.5 + x * (1.0 / 6.0 + x * (1.0 / 24.0))))
    return jnp.where(x > -0.02, series, 1.0 - jnp.exp(x))


def _odd_fwd(p, ccw, ccb, wrg, brg, wig, big, lam, *, tm):
    t = p.shape[0]
    wl = W_LRU
    ng = tm // 8

    def body(xr, gate, xrh, ccw_ref, ccb_ref, wrg_ref, brg_ref, wig_ref, big_ref, lam_ref,
             y_ref, xc_ref, r_ref, i_ref, hs_ref, xext, a_s, b_s, carry):
        i = pl.program_id(0)
        keep = (i > 0).astype(F32)
        xext[0:HALO_C, :] = keep * xrh[...]
        xext[HALO_C:, :] = xr[...]
        xc = jnp.broadcast_to(ccb_ref[...], (tm, wl))
        for k in range(LRU_CONV_K):
            xc = xc + ccw_ref[k:k + 1, :] * xext[pl.ds(HALO_C - (LRU_CONV_K - 1) + k, tm), :]
        xc_ref[...] = xc
        for h in range(LRU_HEADS):
            cs = slice(h * LRU_HD, (h + 1) * LRU_HD)
            xh = xc_ref[:, cs].astype(BF16)
            r_ref[:, cs] = _sigmoid(jnp.dot(xh, wrg_ref[h], preferred_element_type=F32) + brg_ref[:, cs])
            i_ref[:, cs] = _sigmoid(jnp.dot(xh, wig_ref[h], preferred_element_type=F32) + big_ref[:, cs])
        log_a = (-LRU_C * _softplus_neg(lam_ref[...])) * r_ref[...]
        a_s[...] = jnp.exp(log_a)
        b_s[...] = jnp.sqrt(_one_minus_exp(2.0 * log_a)) * (i_ref[...] * xc_ref[...])

        @pl.when(i == 0)
        def _():
            carry[...] = jnp.zeros_like(carry)

        rowi = lax.broadcasted_iota(jnp.int32, (8, wl), 0)

        def step(g, c):
            sl = pl.ds(pl.multiple_of(g * 8, 8), 8)
            aa, bb = a_s[sl, :], b_s[sl, :]
            for s in (1, 2, 4):
                m = rowi >= s
                a_sh = jnp.where(m, pltpu.roll(aa, s, 0), 1.0)
                b_sh = jnp.where(m, pltpu.roll(bb, s, 0), 0.0)
                bb = aa * b_sh + bb
                aa = aa * a_sh
            hv = bb + aa * c
            hs_ref[sl, :] = hv
            return hv[7:8, :]

        carry[...] = lax.fori_loop(0, ng, step, carry[...])
        gt = gate[...]
        y_ref[...] = (hs_ref[...] * (gt * _sigmoid(gt))).astype(BF16)

    row = lambda w: pl.BlockSpec((tm, w), lambda i: (i, 0))
    return pl.pallas_call(
        body, name="odd_fwd", grid=(t // tm,),
        in_specs=[_col(tm, wl, 0), _col(tm, wl, 1), _prev_halo(tm, HALO_C, wl, 0), _full((8, wl)), _full((1, wl)),
                  _full((LRU_HEADS, LRU_HD, LRU_HD)), _full((1, wl)), _full((LRU_HEADS, LRU_HD, LRU_HD)),
                  _full((1, wl)), _full((1, wl))],
        out_specs=[row(wl)] * 5,
        out_shape=[jax.ShapeDtypeStruct((t, wl), BF16)] + [jax.ShapeDtypeStruct((t, wl), F32)] * 4,
        scratch_shapes=[pltpu.VMEM((tm + HALO_C, wl), F32), pltpu.VMEM((tm, wl), F32), pltpu.VMEM((tm, wl), F32),
                        pltpu.VMEM((1, wl), F32)],
        compiler_params=_params("arbitrary"),
    )(p, p, p, ccw, ccb, wrg, brg, wig, big, lam)


def _odd_bwd_a(p, xc, r, ig, hs, dy, wrg, wig, lam, dep, *, tm):
    t = p.shape[0]
    wl = W_LRU
    nt = t // tm
    ng = tm // 8
    per = tm // HALO_C

    def body(gate, xc_ref, r_ref, i_ref, hs_ref, hsh, dy_ref, wrg_ref, wig_ref, lam_ref, dep_ref,
             dxc_ref, dgate_ref, dwrg_ref, dwig_ref, dbrg_ref, dbig_ref, dlam_ref,
             hext, a_s, q_s, g_s, dpr_s, dpi_s, carry):
        i = pl.program_id(0)
        ti = nt - 1 - i
        keep = (ti > 0).astype(F32)
        hext[0:HALO_C, :] = keep * hsh[...]
        hext[HALO_C:, :] = hs_ref[...]
        gt = gate[...]
        sg = _sigmoid(gt)
        dyv = dy_ref[...]
        dgate_ref[...] = (dyv * hs_ref[...] * _dsilu(gt, sg)).astype(BF16)
        q_s[...] = dyv * (gt * sg)
        sp = _softplus_neg(lam_ref[...])
        log_a = (-LRU_C * sp) * r_ref[...]
        a_s[...] = jnp.exp(log_a)

        @pl.when(i == 0)
        def _():
            carry[...] = jnp.zeros_like(carry)
            dwrg_ref[...] = jnp.zeros_like(dwrg_ref)
            dwig_ref[...] = jnp.zeros_like(dwig_ref)

        rowi = lax.broadcasted_iota(jnp.int32, (8, wl), 0)

        def step(gr, c):
            sl = pl.ds(pl.multiple_of((ng - 1 - gr) * 8, 8), 8)
            a0 = a_s[sl, :]
            al = jnp.where(rowi < 7, pltpu.roll(a0, 7, 0), 1.0)
            be = q_s[sl, :]
            for s in (1, 2, 4):
                m = rowi + s <= 7
                al_sh = jnp.where(m, pltpu.roll(al, 8 - s, 0), 1.0)
                be_sh = jnp.where(m, pltpu.roll(be, 8 - s, 0), 0.0)
                be = be + al * be_sh
                al = al * al_sh
            gv = be + al * c
            g_s[sl, :] = gv
            return (a0 * gv)[0:1, :]

        carry[...] = lax.fori_loop(0, ng, step, carry[...])

        gv = g_s[...]
        a = a_s[...]
        mult = jnp.sqrt(_one_minus_exp(2.0 * log_a))
        iv = i_ref[...]
        rv = r_ref[...]
        xcv = xc_ref[...]
        hprev = hext[pl.ds(HALO_C - 1, tm), :]
        dla = gv * hprev * a - (gv * iv * xcv) * (a * a) / mult
        di = gv * mult * xcv
        dpr = (dla * (-LRU_C * sp)) * rv * (1.0 - rv)
        dpi = di * iv * (1.0 - iv)
        dpr_s[...] = dpr
        dpi_s[...] = dpi
        dxc_ref[...] = gv * mult * iv
        dsp = jnp.sum(dla * rv, axis=0, keepdims=True) * (-LRU_C)
        _acc_out(i, dlam_ref, -dsp * jax.nn.sigmoid(-lam_ref[...]))
        _acc_out(i, dbrg_ref, jnp.sum(dpr, axis=0, keepdims=True))
        _acc_out(i, dbig_ref, jnp.sum(dpi, axis=0, keepdims=True))
        for h in range(LRU_HEADS):
            cs = slice(h * LRU_HD, (h + 1) * LRU_HD)
            xh = xc_ref[:, cs].astype(BF16)
            dr_h = dpr_s[:, cs].astype(BF16)
            di_h = dpi_s[:, cs].astype(BF16)
            dxc_ref[:, cs] += (
                lax.dot_general(dr_h, wrg_ref[h], (((1,), (1,)), ((), ())), preferred_element_type=F32)
                + lax.dot_general(di_h, wig_ref[h], (((1,), (1,)), ((), ())), preferred_element_type=F32))
            dwrg_ref[h] += lax.dot_general(xh, dr_h, (((0,), (0,)), ((), ())), preferred_element_type=F32)
            dwig_ref[h] += lax.dot_general(xh, di_h, (((0,), (0,)), ((), ())), preferred_element_type=F32)

    rrow = lambda w: pl.BlockSpec((tm, w), lambda i: (nt - 1 - i, 0))
    hspec = pl.BlockSpec((HALO_C, wl), lambda i: (jnp.maximum((nt - 1 - i) * per - 1, 0), 0))
    wspec = _full((LRU_HEADS, LRU_HD, LRU_HD))
    return pl.pallas_call(
        body, name="odd_bwd_a", grid=(nt,),
        in_specs=[pl.BlockSpec((tm, wl), lambda i: (nt - 1 - i, 1)), rrow(wl), rrow(wl), rrow(wl), rrow(wl), hspec,
                  rrow(wl), wspec, wspec, _full((1, wl)), ANY],
        out_specs=[rrow(wl), rrow(wl), wspec, wspec, _full((1, wl)), _full((1, wl)), _full((1, wl))],
        out_shape=[jax.ShapeDtypeStruct((t, wl), F32), jax.ShapeDtypeStruct((t, wl), BF16),
                   jax.ShapeDtypeStruct((LRU_HEADS, LRU_HD, LRU_HD), F32),
                   jax.ShapeDtypeStruct((LRU_HEADS, LRU_HD, LRU_HD), F32)] + [jax.ShapeDtypeStruct((1, wl), F32)] * 3,
        scratch_shapes=[pltpu.VMEM((tm + HALO_C, wl), F32)] + [pltpu.VMEM((tm, wl), F32)] * 5
                       + [pltpu.VMEM((1, wl), F32)],
        compiler_params=_params("arbitrary"),
    )(p, xc, r, ig, hs, hs, dy, wrg, wig, lam, dep)


def _odd_bwd_b(p, dxc, dgate, ccw, *, tm):
    t = p.shape[0]
    wl = W_LRU
    nt = t // tm

    def body(xr, xrh, dxc_ref, dxcn, dgate_ref, ccw_ref, dp_ref, dcw_ref, dcb_ref, xext, gext):
        i = pl.program_id(0)

        @pl.when(i == 0)
        def _():
            dcw_ref[...] = jnp.zeros_like(dcw_ref)

        xext[0:HALO_C, :] = (i > 0).astype(F32) * xrh[...]
        xext[HALO_C:, :] = xr[...]
        gext[0:tm, :] = dxc_ref[...]
        gext[tm:, :] = (i < nt - 1).astype(F32) * dxcn[...]
        g = dxc_ref[...]
        acc = jnp.zeros((tm, wl), F32)
        for k in range(LRU_CONV_K):
            acc = acc + ccw_ref[k:k + 1, :] * gext[pl.ds(LRU_CONV_K - 1 - k, tm), :]
            dcw_ref[k:k + 1, :] += jnp.sum(
                g * xext[pl.ds(HALO_C - (LRU_CONV_K - 1) + k, tm), :], axis=0, keepdims=True)

        _acc_out(i, dcb_ref, jnp.sum(g, axis=0, keepdims=True))
        dp_ref[:, 0:wl] = acc.astype(BF16)
        dp_ref[:, wl:2 * wl] = dgate_ref[...]

    row = lambda w: pl.BlockSpec((tm, w), lambda i: (i, 0))
    return pl.pallas_call(
        body, name="odd_bwd_b", grid=(nt,),
        in_specs=[_col(tm, wl, 0), _prev_halo(tm, HALO_C, wl, 0), row(wl), _next_halo(tm, HALO_C, wl, 0, t), row(wl),
                  _full((8, wl))],
        out_specs=[row(2 * wl), _full((8, wl)), _full((1, wl))],
        out_shape=[jax.ShapeDtypeStruct((t, 2 * wl), BF16), jax.ShapeDtypeStruct((8, wl), F32),
                   jax.ShapeDtypeStruct((1, wl), F32)],
        scratch_shapes=[pltpu.VMEM((tm + HALO_C, wl), F32), pltpu.VMEM((tm + HALO_C, wl), F32)],
        compiler_params=_params("arbitrary"),
    )(p, p, dxc, dxc, dgate, ccw)


def _local_step(x, target, layer_weights, final_norm, on_grads):
    t = x.shape[0]
    tm, tx, tl, tw = min(TM_MATMUL, t), min(TM_MIXER, t), min(TM_STREAM, t), min(TM_WGRAD, t)
    h = x
    saved = []
    for layer in range(N_LAYERS):
        w = layer_weights(layer, h)
        p, hn = _norm_matmul(h, w["norm"], w["w_in"], w["dep"], tm=tl)
        if layer % 2 == 0:
            y, *acts = _even_fwd(p, w["conv_w"], w["conv_b"], w["ln_g"], w["ln_b"], w["pool_w"], w["pool_b"],
                                 w["pool_scale"], tm=tx)
        else:
            y, *acts = _odd_fwd(p, w["conv_w"], w["conv_b"], w["w_rg"], w["b_rg"], w["w_ig"], w["b_ig"], w["lam"],
                                tm=tx)
        w_out, dep = w["w_out"](y)
        saved.append((w, w_out, h, p, hn, y, acts))
        h = _out_proj(h, y, w_out, dep, tm=tl)
    loss, dh, d_final = _loss_head(h, final_norm, target, tm=tm)

    dep = d_final
    for layer in reversed(range(N_LAYERS)):
        w, w_out, h_in, p, hn, y, acts = saved[layer]
        sfx = "even" if layer % 2 == 0 else "odd"
        dy, dw_out = _out_proj_bwd(dh, y, w_out, dep, tm=tm)
        dep = on_grads(layer, {"w_out_" + sfx: dw_out}, dep, False)
        if layer % 2 == 0:
            u1, e, dmat = acts
            du1, dd, dgat, dlg, dlb, dpb, dps, dpw = _even_bwd_a(p, u1, e, dmat, dy, w["ln_g"], w["ln_b"], w["pool_w"],
                                                                 w["pool_scale"], dep, tm=tx)
            dp, dcw, dcb = _even_bwd_b(p, du1, dd, dgat, w["conv_w"], tm=tx)
            grads = dict(conv_a_w=dcw[:CONV_K], conv_a_b=dcb, ln_a_g=dlg, ln_a_b=dlb, pool_w=dpw, pool_b=dpb,
                         pool_scale=dps)
        else:
            xc, r, ig, hs = acts
            dxc, dgate, dwrg, dwig, dbrg, dbig, dlam = _odd_bwd_a(p, xc, r, ig, hs, dy, w["w_rg"], w["w_ig"],
                                                                  w["lam"], dep, tm=tx)
            dp, dccw, dccb = _odd_bwd_b(p, dxc, dgate, w["conv_w"], tm=tx)
            grads = dict(conv_c_w=dccw[:LRU_CONV_K], conv_c_b=dccb, w_rg=dwrg, b_rg=dbrg, w_ig=dwig, b_ig=dbig,
                         lru_lambda=dlam)
        grads["w_in_" + sfx] = _in_proj_bwd_w(hn, dp, N_DEV, tm=tw)
        dep = on_grads(layer, grads, dep, False)
        dh, dg = _in_proj_bwd_x(dp, w["w_in"], h_in, w["norm"], dh, dep, tm=tm)
        rest = {"norm_" + sfx: dg}
        if layer == N_LAYERS - 1:
            rest["final_norm"] = d_final
        dep = on_grads(layer, rest, dep, True)
    return loss, dh


def _slot(px, py, pc):
    return 4 * px + 2 * py + pc


def _peers(x, y, c):
    return [(1 - x if k & 4 else x, 1 - y if k & 2 else y, 1 - c if k & 1 else c) for k in range(1, N_DEV)]


def _all_gather(arrs, name):
    n = len(arrs)

    def body(*refs):
        ins, outs = refs[:n], refs[n:2 * n]
        send_sems, recv_sems, local_sems = refs[2 * n:]
        x, y, c = lax.axis_index("x"), lax.axis_index("y"), lax.axis_index("c")
        me, sibling = (x, y, c), (x, y, 1 - c)
        chips = [(1 - x, y), (x, 1 - y), (1 - x, 1 - y)]

        def copy(a, k, block, to, src=None):
            rows = outs[a].at[_slot(*block)]
            return pltpu.make_async_remote_copy(
                src_ref=rows if src is None else src, dst_ref=rows, send_sem=send_sems.at[a, k],
                recv_sem=recv_sems.at[a, k], device_id=to, device_id_type=MESH)

        mine = [pltpu.make_async_copy(ins[a], outs[a].at[_slot(*me)], local_sems.at[a]) for a in range(n)]
        for cp in mine:
            cp.start()
        first = []
        for a in range(n):
            first.append(copy(a, 0, me, sibling, src=ins[a]))
            first += [copy(a, 1 + j, me, (*chip, c), src=ins[a]) for j, chip in enumerate(chips)]
        for cp in first:
            cp.start()
        passed = []
        for j, chip in enumerate(chips):
            for a in range(n):
                copy(a, 1 + j, (*chip, c), me).wait_recv()
                fwd = copy(a, 4 + j, (*chip, c), sibling)
                fwd.start()
                passed.append(fwd)
        for a in range(n):
            copy(a, 0, sibling, me).wait_recv()
            for j, chip in enumerate(chips):
                copy(a, 4 + j, (*chip, 1 - c), me).wait_recv()
        for cp in first + passed:
            cp.wait_send()
        for cp in mine:
            cp.wait()

    return pl.pallas_call(
        body, name=name,
        in_specs=[ANY] * n, out_specs=[ANY] * n,
        out_shape=[jax.ShapeDtypeStruct((N_DEV,) + a.shape, a.dtype) for a in arrs],
        scratch_shapes=[pltpu.SemaphoreType.DMA((n, 7)), pltpu.SemaphoreType.DMA((n, 7)),
                        pltpu.SemaphoreType.DMA((n,))],
    )(*arrs)


N_COPIES = {"gather": N_PEERS, "scatter": N_PEERS, "chip_gather": 4, "forward": 3}


def _exchange_plan(mode, x, y, c):
    me = _slot(x, y, c)
    if mode == "forward":
        chips = [(1 - x, y), (x, 1 - y), (1 - x, 1 - y)]
        return [((x, y, 1 - c), ("land", _slot(*chip, c)), _slot(*chip, c), _slot(*chip, 1 - c)) for chip in chips]
    peers = _peers(x, y, c)
    if mode == "chip_gather":
        peers = [(x, y, 1 - c), (1 - x, y, c), (x, 1 - y, c), (1 - x, 1 - y, c)]
    return [(p, ("src", _slot(*p)) if mode == "scatter" else ("src", None), me, _slot(*p)) for p in peers]


def _exchange_copy(src_ref, land_ref, plan, send_sem, recv_sem, start):
    peer, (which, block), there, here = plan
    src = land_ref if which == "land" else src_ref
    return pltpu.make_async_remote_copy(
        src_ref=src if block is None else src.at[block], dst_ref=land_ref.at[there if start else here],
        send_sem=send_sem, recv_sem=recv_sem, device_id=peer, device_id_type=MESH)


def _exchange_start(groups, deps, name):
    flat = [pair for _, g in groups for pair in g]
    n, ng = len(flat), len(groups)

    def body(*refs):
        src_refs, land_refs = refs[:n], refs[n:2 * n]
        outs = refs[2 * n + len(deps):]
        sems, token = outs[:2 * ng], outs[2 * ng + 2 * n]
        x, y, c = lax.axis_index("x"), lax.axis_index("y"), lax.axis_index("c")
        base = 0
        for gi, (mode, g) in enumerate(groups):
            nc = N_COPIES[mode]
            for k, plan in enumerate(_exchange_plan(mode, x, y, c)):
                for ai in range(len(g)):
                    _exchange_copy(src_refs[base + ai], land_refs[base + ai], plan, sems[2 * gi].at[ai * nc + k],
                                   sems[2 * gi + 1].at[ai * nc + k], True).start()
            base += len(g)
        token[...] = jnp.zeros_like(token)

    operands = [pltpu.with_memory_space_constraint(a, pltpu.HBM) for a in
                [s for s, _ in flat] + [l for _, l in flat]]
    out_shape = []
    for mode, g in groups:
        out_shape += [pltpu.SemaphoreType.DMA((len(g) * N_COPIES[mode],))] * 2
    out_shape += [pltpu.HBM(a.shape, a.dtype) for a in operands]
    out_shape.append(jax.ShapeDtypeStruct((8, 128), F32))
    outs = pl.pallas_call(
        body, name=name, out_shape=out_shape,
        in_specs=[HBM] * (2 * n) + [ANY] * len(deps),
        out_specs=[SEM] * (2 * ng) + [HBM] * (2 * n) + [pl.BlockSpec(memory_space=pltpu.VMEM)],
        input_output_aliases={i: 2 * ng + i for i in range(2 * n)},
        compiler_params=pltpu.CompilerParams(has_side_effects=pltpu.SideEffectType.DATAFLOW_SIDE_EFFECTING),
    )(*operands, *deps)
    handles, base = [], 0
    for gi, (mode, g) in enumerate(groups):
        srcs = outs[2 * ng + base:2 * ng + base + len(g)]
        lands = outs[2 * ng + n + base:2 * ng + n + base + len(g)]
        handles.append((mode, outs[2 * gi], outs[2 * gi + 1], list(srcs), list(lands)))
        base += len(g)
    return handles, outs[-1]


def _exchange_wait(handle, after, name):
    mode, send_sems, recv_sems, srcs, lands = handle
    n = len(srcs)
    nc = N_COPIES[mode]

    def body(*refs):
        src_refs, land_refs = refs[:n], refs[n:2 * n]
        send_ref, recv_ref = refs[2 * n], refs[2 * n + 1]
        x, y, c = lax.axis_index("x"), lax.axis_index("y"), lax.axis_index("c")
        for k, plan in enumerate(_exchange_plan(mode, x, y, c)):
            for a in range(n):
                cp = _exchange_copy(src_refs[a], land_refs[a], plan, send_ref.at[a * nc + k], recv_ref.at[a * nc + k],
                                    False)
                cp.wait_send()
                cp.wait_recv()

    outs = pl.pallas_call(
        body, name=name,
        out_shape=[pltpu.HBM(a.shape, a.dtype) for a in srcs + lands],
        in_specs=[HBM] * (2 * n) + [SEM, SEM] + [ANY] * len(after),
        out_specs=[HBM] * (2 * n),
        input_output_aliases={i: i for i in range(2 * n)},
        compiler_params=pltpu.CompilerParams(has_side_effects=pltpu.SideEffectType.DATAFLOW_SIDE_EFFECTING),
    )(*srcs, *lands, send_sems, recv_sems, *after)
    return list(outs[n:])


def _adamw_math(w, g, m, v):
    c1 = 1.0 - ADAM_B1 ** ADAM_STEP
    c2 = 1.0 - ADAM_B2 ** ADAM_STEP
    nm = ADAM_B1 * m + (1.0 - ADAM_B1) * g
    nv = ADAM_B2 * v + (1.0 - ADAM_B2) * (g * g)
    delta = -ADAM_LR * ((nm / c1) / (jnp.sqrt(nv / c2) + ADAM_EPS) + ADAM_WD * w)
    return delta, nm, nv


def _row_tile(r):
    for cand in (256, 128, 64, 32, 16, 8):
        if r % cand == 0 and r > cand:
            return cand
    return r


def _adamw(items, layer0, bufs, name):
    ni = len(items)
    nl = items[0][1].shape[1]
    tiles = [_row_tile(w.shape[1]) for w, _, _, _ in items]
    steps = [w.shape[1] // tr for (w, _, _, _), tr in zip(items, tiles)]
    ns = steps[0]
    assert all(s == ns for s in steps)
    nb = 0 if bufs is None else 4 * ni

    def body(*refs):
        ins, outs = refs[:4 * ni], refs[4 * ni + nb:]
        for k in range(ni):
            w_ref, p_ref, m_ref, v_ref = ins[4 * k:4 * k + 4]
            g = p_ref[0, 0].astype(F32)
            for s in range(1, p_ref.shape[0]):
                g = g + p_ref[s, 0].astype(F32)
            delta, nm, nv = _adamw_math(w_ref[0], g, m_ref[0], v_ref[0])
            g_ref, d_ref, nm_ref, nv_ref = outs[4 * k:4 * k + 4]
            g_ref[0], d_ref[0], nm_ref[0], nv_ref[0] = g, delta, nm, nv

    in_specs, out_specs, out_shape, operands = [], [], [], []
    for (w, parts, m, v), tr in zip(items, tiles):
        blk = pl.BlockSpec((1, tr, w.shape[2]), lambda l, i: (layer0 + l, i, 0))
        in_specs += [blk, pl.BlockSpec((parts.shape[0], 1, tr, w.shape[2]), lambda l, i: (0, l, i, 0)), blk, blk]
        operands += [w, parts, m, v]
        out_specs += [blk] * 4
        out_shape += [jax.ShapeDtypeStruct(w.shape, F32)] * 4
    if bufs is not None:
        in_specs += [ANY] * nb
        operands += [b for item in bufs for b in item]
    outs = pl.pallas_call(
        body, name=name, grid=(nl, ns), in_specs=in_specs, out_specs=out_specs, out_shape=out_shape,
        input_output_aliases={4 * ni + i: i for i in range(nb)},
        compiler_params=_params("arbitrary", "arbitrary"),
    )(*operands)
    return [tuple(outs[4 * k:4 * k + 4]) for k in range(ni)]


NAMES = ("norm_even", "w_in_even", "conv_a_w", "conv_a_b", "ln_a_g", "ln_a_b", "pool_w", "pool_b", "pool_scale",
         "w_out_even", "norm_odd", "w_in_odd", "conv_c_w", "conv_c_b", "w_rg", "b_rg", "w_ig", "b_ig", "lru_lambda",
         "w_out_odd", "final_norm")
SMALL_GATHERED = ("conv_a_w", "pool_b", "norm_odd", "conv_c_w", "conv_c_b", "b_rg", "b_ig", "lru_lambda")
BIG = (("w_in_even", "w_out_even"), ("w_in_odd", "w_out_odd"))
SMALL = (("conv_a_w", "pool_b", "pool_w"), ("norm_odd", "conv_c_w", "conv_c_b", "b_rg", "b_ig", "lru_lambda"))
REPLICATED = (("norm_even", "conv_a_b", "ln_a_g", "ln_a_b", "pool_scale"), ("w_rg", "w_ig"))
PACK_ROW = 1024


def _pack_rows(flat2d):
    pad = (-flat2d.shape[1]) % PACK_ROW
    return jnp.pad(flat2d, ((0, 0), (0, pad))).reshape(flat2d.shape[0], -1, 128)


def _unpack(flat, shapes):
    out, off = [], 0
    for s in shapes:
        n = 1
        for d in s:
            n *= d
        out.append(flat[..., off:off + n].reshape(flat.shape[:-1] + tuple(s)))
        off += n
    return out


def _to_global(name, g):
    if name in ("conv_a_w", "pool_b", "conv_c_w"):
        return jnp.transpose(g, (1, 2, 0, 3)).reshape(g.shape[1], g.shape[2], -1)
    if name == "pool_w":
        return jnp.transpose(g, (1, 2, 0, 3, 4)).reshape(2, 4, POOL_GW, POOL_GW)
    return jnp.transpose(g, (1, 0, 2)).reshape(g.shape[1], -1)


def _to_blocks(name, g):
    if name == "conv_a_w":
        return jnp.transpose(g.reshape(CONV_K, N_DEV, -1), (1, 0, 2))
    if name == "conv_c_w":
        return jnp.transpose(g.reshape(LRU_CONV_K, N_DEV, -1), (1, 0, 2))
    if name == "pool_b":
        return jnp.transpose(g.reshape(4, N_DEV, -1), (1, 0, 2))
    if name == "pool_w":
        return jnp.transpose(g.reshape(4, N_DEV, POOL_GW // N_DEV, POOL_GW), (1, 0, 2, 3))
    return g.reshape(N_DEV, -1)


def _as3d(a):
    if a.ndim == 1:
        return a.reshape(1, 1, -1)
    if a.ndim == 2:
        return a.reshape(a.shape[0], 1, a.shape[1])
    return a.reshape(a.shape[0], -1, a.shape[-1])


def kernel(x, norm_even, w_in_even, conv_a_w, conv_a_b, ln_a_g, ln_a_b, pool_w, pool_b, pool_scale, w_out_even, norm_odd, w_in_odd, conv_c_w, conv_c_b, w_rg, b_rg, w_ig, b_ig, lru_lambda, w_out_odd, final_norm, loss_target, m_norm_even, m_w_in_even, m_conv_a_w, m_conv_a_b, m_ln_a_g, m_ln_a_b, m_pool_w, m_pool_b, m_pool_scale, m_w_out_even, m_norm_odd, m_w_in_odd, m_conv_c_w, m_conv_c_b, m_w_rg, m_b_rg, m_w_ig, m_b_ig, m_lru_lambda, m_w_out_odd, m_final_norm, v_norm_even, v_w_in_even, v_conv_a_w, v_conv_a_b, v_ln_a_g, v_ln_a_b, v_pool_w, v_pool_b, v_pool_scale, v_w_out_even, v_norm_odd, v_w_in_odd, v_conv_c_w, v_conv_c_b, v_w_rg, v_b_rg, v_w_ig, v_b_ig, v_lru_lambda, v_w_out_odd, v_final_norm):
    w_loc = dict(zip(NAMES, [norm_even, w_in_even, conv_a_w, conv_a_b, ln_a_g, ln_a_b, pool_w, pool_b, pool_scale,
                             w_out_even, norm_odd, w_in_odd, conv_c_w, conv_c_b, w_rg, b_rg, w_ig, b_ig, lru_lambda,
                             w_out_odd, final_norm]))
    m_loc = dict(zip(NAMES, [m_norm_even, m_w_in_even, m_conv_a_w, m_conv_a_b, m_ln_a_g, m_ln_a_b, m_pool_w, m_pool_b,
                             m_pool_scale, m_w_out_even, m_norm_odd, m_w_in_odd, m_conv_c_w, m_conv_c_b, m_w_rg,
                             m_b_rg, m_w_ig, m_b_ig, m_lru_lambda, m_w_out_odd, m_final_norm]))
    v_loc = dict(zip(NAMES, [v_norm_even, v_w_in_even, v_conv_a_w, v_conv_a_b, v_ln_a_g, v_ln_a_b, v_pool_w, v_pool_b,
                             v_pool_scale, v_w_out_even, v_norm_odd, v_w_in_odd, v_conv_c_w, v_conv_c_b, v_w_rg,
                             v_b_rg, v_w_ig, v_b_ig, v_lru_lambda, v_w_out_odd, v_final_norm]))
    me = _slot(lax.axis_index("x"), lax.axis_index("y"), lax.axis_index("c"))

    def landing(own):
        zone = lax.empty((N_DEV,) + own.shape[1:], own.dtype)
        return lax.dynamic_update_slice(zone, own, (me,) + (0,) * (own.ndim - 1))

    small_shapes = [w_loc[n].shape for n in SMALL_GATHERED]
    small = jnp.concatenate([w_loc[n].reshape(1, -1) for n in SMALL_GATHERED], axis=1)
    first = _all_gather([w_in_even[0].astype(BF16), pool_w.astype(BF16), _pack_rows(small)[0]], "gather_first")
    g_small = dict(zip(SMALL_GATHERED, [_to_global(n, g) for n, g in
                                        zip(SMALL_GATHERED, _unpack(first[2].reshape(N_DEV, -1), small_shapes))]))
    pool_w_all = _to_global("pool_w", first[1])

    def pairs_of(shards):
        return [(s.astype(BF16), landing(s.astype(BF16)[None])) for s in shards]

    shards = {1: [w_in_odd[0], w_out_odd[0]], 2: [w_in_even[1], w_out_even[1]], 3: [w_in_odd[1], w_out_odd[1]]}
    leg_a, leg_b = {}, {}
    (w_out_0, leg_a[1]), token_1 = _exchange_start(
        [("gather", pairs_of([w_out_even[0]])), ("chip_gather", pairs_of(shards[1]))], [first[0]], "gather_start_1")
    unused = jnp.zeros((8, 128), F32)

    def second_leg(layer, y):
        lands = _exchange_wait(leg_a[layer], [y], f"gather_wait_a_{layer}")
        (leg_b[layer],), token = _exchange_start([("forward", [(unused, l) for l in lands])], [],
                                                 f"gather_forward_{layer}")
        return token

    def layer_weights(layer, h):
        j = layer // 2
        dep = h
        if layer == 0:
            w_in, dep = first[0], token_1
            w_out_now = lambda y: _exchange_wait(w_out_0, [y], "gather_wait_out_0")[0]
        else:
            w_in, w_out_got = _exchange_wait(leg_b[layer], [h], f"gather_wait_b_{layer}")
            w_out_now = lambda y: w_out_got
            if layer + 1 in shards:
                (leg_a[layer + 1],), dep = _exchange_start([("chip_gather", pairs_of(shards[layer + 1]))], [w_in],
                                                           f"gather_start_{layer + 1}")

        def w_out(y):
            return w_out_now(y), (second_leg(layer + 1, y) if layer + 1 in shards else y)

        w_in = jnp.transpose(w_in, (1, 0, 2)).reshape(D_MODEL, -1)
        row = lambda a: a[j][None]
        if layer % 2 == 0:
            return dict(dep=dep, norm=row(norm_even), w_in=w_in,
                        w_out=lambda y: (lambda wo, d: (wo.reshape(W_EVEN_MIX, D_MODEL), d))(*w_out(y)),
                        conv_w=jnp.pad(g_small["conv_a_w"][j], ((0, 1), (0, 0))), conv_b=row(conv_a_b),
                        ln_g=row(ln_a_g), ln_b=row(ln_a_b), pool_w=pool_w_all[j],
                        pool_b=g_small["pool_b"][j].reshape(1, W_POOL), pool_scale=row(pool_scale))
        return dict(dep=dep, norm=row(g_small["norm_odd"]), w_in=w_in,
                    w_out=lambda y: (lambda wo, d: (wo.reshape(W_LRU, D_MODEL), d))(*w_out(y)),
                    conv_w=jnp.pad(g_small["conv_c_w"][j], ((0, 4), (0, 0))), conv_b=row(g_small["conv_c_b"]),
                    w_rg=w_rg[j].astype(BF16), b_rg=row(g_small["b_rg"]), w_ig=w_ig[j].astype(BF16),
                    b_ig=row(g_small["b_ig"]), lam=row(g_small["lru_lambda"]))

    pending, exchanges, small_layout, first_legs = {}, {}, {}, {}
    last_token = []

    def on_grads(layer, grads, dep, last):
        par = layer % 2
        w_in_name, w_out_name = BIG[par]
        have = pending.setdefault(layer, {})
        have.update(grads)
        eager = layer == 0
        scatter, gather = {}, {}
        if w_out_name in have and (eager or last):
            scatter["out"] = have.pop(w_out_name).reshape(N_DEV, -1, D_MODEL)
        if w_in_name in have and (eager or last):
            scatter["in"] = have.pop(w_in_name)
        if layer not in small_layout and all(n in have for n in SMALL[par]) and (eager or last):
            blocks = [_to_blocks(n, have[n]) for n in SMALL[par]]
            small_layout[layer] = [b.shape[1:] for b in blocks]
            scatter["small"] = _pack_rows(jnp.concatenate([b.reshape(N_DEV, -1) for b in blocks], axis=1))
        chip = {n: have.pop(n).astype(BF16).reshape(-1, LRU_HD) for n in REPLICATED[1] if n in have}
        forward = {}
        if last:
            vectors = [have[n] for n in REPLICATED[par]] if par == 0 else []
            if "final_norm" in have:
                vectors.append(have["final_norm"])
            if vectors:
                gather["rep32"] = jnp.concatenate([v.reshape(-1) for v in vectors]).reshape(-1, 128)
            if layer in first_legs:
                names, handle = first_legs.pop(layer)
                forward = dict(zip(names, _exchange_wait(handle, list(grads.values()), f"grads_wait_a_{layer}")))
        groups, keys = [], []
        for mode, arrays, pairs in (
                ("scatter", scatter, [(s, landing(lax.dynamic_slice_in_dim(s, me, 1, 0))) for s in scatter.values()]),
                ("gather", gather, [(s, landing(s[None])) for s in gather.values()]),
                ("chip_gather", chip, [(s, landing(s[None])) for s in chip.values()]),
                ("forward", forward, [(unused, l) for l in forward.values()])):
            if arrays:
                groups.append((mode, pairs))
                keys.append(list(arrays))
        if not groups:
            return dep
        handles, token = _exchange_start(groups, [], f"grads_start_{layer}_{'_'.join(k for ks in keys for k in ks)}")
        if chip:
            first_legs[layer] = keys.pop(), handles.pop()
        exchanges.setdefault(layer, []).extend(zip(keys, handles))
        last_token[:] = [token]
        return token

    loss, grad_x = _local_step(x[0], loss_target[0], layer_weights, final_norm[None], on_grads)

    w3 = {n: _as3d(w_loc[n]) for n in NAMES}
    m3 = {n: _as3d(m_loc[n]) for n in NAMES}
    v3 = {n: _as3d(v_loc[n]) for n in NAMES}
    results = {}
    after = list(last_token)
    for layer in (3, 2, 1, 0):
        par, j = layer % 2, layer // 2
        got = {}
        for keys, handle in exchanges[layer]:
            got.update(zip(keys, _exchange_wait(handle, after, f"grads_wait_{layer}_{'_'.join(keys)}")))
        parts = {BIG[par][0]: got["in"], BIG[par][1]: got["out"]}
        parts.update(zip(SMALL[par], _unpack(got["small"].reshape(N_DEV, -1), small_layout[layer])))
        if par == 1:
            parts.update({n: got[n] for n in REPLICATED[1]})
        else:
            parts.update(zip(REPLICATED[0], _unpack(got["rep32"].reshape(N_DEV, -1)[:, :len(REPLICATED[0]) * D_MODEL],
                                                    [w_loc[n].shape[1:] for n in REPLICATED[0]])))
        big_items, small_items = [], []
        for n, pt in parts.items():
            item = (w3[n], pt.reshape((N_DEV, 1) + w3[n].shape[1:]), m3[n], v3[n])
            (big_items if w3[n].shape[1] >= 128 else small_items).append((n, item))
        for n, item in big_items:
            results[n] = _adamw([item], j, [results[n]] if n in results else None, f"adamw_{n}_{layer}")[0]
        snames = [n for n, _ in small_items]
        prev = [results[n] for n in snames] if snames[0] in results else None
        for n, r in zip(snames, _adamw([it for _, it in small_items], j, prev, f"adamw_small_{layer}")):
            results[n] = r
        if layer == N_LAYERS - 1:
            item = (w3["final_norm"], got["rep32"].reshape(N_DEV, 1, 1, -1)[..., :D_MODEL], m3["final_norm"],
                    v3["final_norm"])
            results["final_norm"] = _adamw([item], 0, None, "adamw_final_norm")[0]
        after = [results[BIG[par][0]][1]]

    total = lax.psum(loss[0, 0], ("x", "y", "c"))
    outs = [[results[n][k].reshape(w_loc[n].shape) for n in NAMES] for k in range(4)]
    return (total, grad_x[None], *outs[0], *outs[1], *outs[2], *outs[3])
```

```python
import functools

import jax
import jax.numpy as jnp
from jax import lax
from jax.experimental import pallas as pl
from jax.experimental.pallas import tpu as pltpu

F32 = jnp.float32
BF16 = jnp.bfloat16

N_DEV = 8
N_PEERS = N_DEV - 1
N_LAYERS = 4
D_MODEL = 1024
EPS_RMS = 1e-6
EPS_LN = 1e-5
W_CONV = 1024
CONV_K = 31
W_POOL = 1024
POOL_WINDOWS = (2, 4, 8, 16)
POOL_GW = 256
W_EVEN_IN = 5120
W_EVEN_MIX = 2048
LRU_HEADS = 12
LRU_HD = 128
W_LRU = 1536
LRU_CONV_K = 4
LRU_C = 8.0
ADAM_LR = 0.001
ADAM_B1 = 0.9
ADAM_B2 = 0.999
ADAM_EPS = 1e-08
ADAM_WD = 0.01
ADAM_STEP = 10

HALO = 32
HALO_C = 8
TM_MATMUL = 512
TM_STREAM = 1024
TM_WGRAD = 2048
TM_MIXER = 256
CONV_ROWS = 128
VMEM_LIMIT = 56 * 1024 * 1024
MESH = pl.DeviceIdType.MESH
ANY = pl.BlockSpec(memory_space=pl.ANY)
HBM = pl.BlockSpec(memory_space=pltpu.HBM)
SEM = pl.BlockSpec(memory_space=pltpu.SEMAPHORE)


def _params(*sem):
    return pltpu.CompilerParams(dimension_semantics=sem, vmem_limit_bytes=VMEM_LIMIT)


def _sigmoid(z):
    return 0.5 * jnp.tanh(0.5 * z) + 0.5


def _dsilu(z, s):
    return s * (1.0 + z * (1.0 - s))


def _full(shape):
    nd = len(shape)
    return pl.BlockSpec(shape, lambda *_: (0,) * nd)


def _norm_matmul(h, g, w, dep, *, tm):
    t, d = h.shape
    n = w.shape[1]
    tn = n // 4

    def body(h_ref, g_ref, w_ref, dep_ref, p_ref, hn_ref):
        @pl.when(pl.program_id(1) == 0)
        def _():
            x = h_ref[...]
            r = lax.rsqrt(jnp.mean(x * x, axis=-1, keepdims=True) + EPS_RMS)
            hn_ref[...] = ((x * r) * g_ref[...]).astype(BF16)

        p_ref[...] = jnp.dot(hn_ref[...], w_ref[...], preferred_element_type=F32)

    return pl.pallas_call(
        body, name="norm_matmul", grid=(t // tm, n // tn),
        in_specs=[pl.BlockSpec((tm, d), lambda i, j: (i, 0)), _full((1, d)),
                  pl.BlockSpec((d, tn), lambda i, j: (0, j)), ANY],
        out_specs=[pl.BlockSpec((tm, tn), lambda i, j: (i, j)), pl.BlockSpec((tm, d), lambda i, j: (i, 0))],
        out_shape=[jax.ShapeDtypeStruct((t, n), F32), jax.ShapeDtypeStruct((t, d), BF16)],
        compiler_params=_params("arbitrary", "arbitrary"),
    )(h, g, w, dep)


def _out_proj(h, y, w, dep, *, tm):
    t, d = h.shape
    k = y.shape[1]

    def body(h_ref, y_ref, w_ref, dep_ref, o_ref):
        o_ref[...] = h_ref[...] + jnp.dot(y_ref[...], w_ref[...], preferred_element_type=F32)

    return pl.pallas_call(
        body, name="out_proj", grid=(t // tm,),
        in_specs=[pl.BlockSpec((tm, d), lambda i: (i, 0)), pl.BlockSpec((tm, k), lambda i: (i, 0)), _full((k, d)),
                  ANY],
        out_specs=pl.BlockSpec((tm, d), lambda i: (i, 0)),
        out_shape=jax.ShapeDtypeStruct((t, d), F32),
        compiler_params=_params("arbitrary"),
    )(h, y, w, dep)


def _out_proj_bwd(dh, y, w, dep, *, tm):
    t, d = dh.shape
    k = y.shape[1]
    nt = t // tm

    def body(dh_ref, y_ref, w_ref, dep_ref, dy_ref, dw_ref, acc):
        i = pl.program_id(0)
        g = dh_ref[...].astype(BF16)
        dy_ref[...] = lax.dot_general(g, w_ref[...], (((1,), (1,)), ((), ())), preferred_element_type=F32)
        part = lax.dot_general(y_ref[...], g, (((0,), (0,)), ((), ())), preferred_element_type=F32)

        @pl.when(i == 0)
        def _():
            acc[...] = part

        @pl.when(i > 0)
        def _():
            acc[...] += part

        @pl.when(i == nt - 1)
        def _():
            dw_ref[...] = acc[...].astype(BF16)

    return pl.pallas_call(
        body, name="out_proj_bwd", grid=(nt,),
        in_specs=[pl.BlockSpec((tm, d), lambda i: (i, 0)), pl.BlockSpec((tm, k), lambda i: (i, 0)), _full((k, d)),
                  ANY],
        out_specs=[pl.BlockSpec((tm, k), lambda i: (i, 0)), _full((k, d))],
        out_shape=[jax.ShapeDtypeStruct((t, k), F32), jax.ShapeDtypeStruct((k, d), BF16)],
        scratch_shapes=[pltpu.VMEM((k, d), F32)],
        compiler_params=_params("arbitrary"),
    )(dh, y, w, dep)


def _in_proj_bwd_x(dp, w, h, g, dh_out, dep, *, tm):
    t, d = h.shape
    n = w.shape[1]
    nt = t // tm

    def body(dp_ref, w_ref, h_ref, g_ref, dho_ref, dep_ref, dh_ref, dg_ref):
        i = pl.program_id(0)
        dy = lax.dot_general(dp_ref[...], w_ref[...], (((1,), (1,)), ((), ())), preferred_element_type=F32)
        x = h_ref[...]
        r = lax.rsqrt(jnp.mean(x * x, axis=-1, keepdims=True) + EPS_RMS)
        gd = dy * g_ref[...]
        m = jnp.mean(gd * x, axis=-1, keepdims=True)
        dh_ref[...] = dho_ref[...] + r * gd - x * (r * r * r * m)
        _acc_out(i, dg_ref, jnp.sum(dy * x * r, axis=0, keepdims=True))

    return pl.pallas_call(
        body, name="in_proj_bwd_x", grid=(nt,),
        in_specs=[pl.BlockSpec((tm, n), lambda i: (i, 0)), _full((d, n)),
                  pl.BlockSpec((tm, d), lambda i: (i, 0)), _full((1, d)), pl.BlockSpec((tm, d), lambda i: (i, 0)),
                  ANY],
        out_specs=[pl.BlockSpec((tm, d), lambda i: (i, 0)), _full((1, d))],
        out_shape=[jax.ShapeDtypeStruct((t, d), F32), jax.ShapeDtypeStruct((1, d), F32)],
        compiler_params=_params("arbitrary"),
    )(dp, w, h, g, dh_out, dep)


def _in_proj_bwd_w(hn, dp, nd, *, tm):
    t, d = hn.shape
    nb = dp.shape[1] // nd
    nt = t // tm

    def body(hn_ref, dp_ref, dw_ref, acc):
        i = pl.program_id(1)
        part = lax.dot_general(hn_ref[...], dp_ref[...], (((0,), (0,)), ((), ())), preferred_element_type=F32)

        @pl.when(i == 0)
        def _():
            acc[...] = part

        @pl.when(i > 0)
        def _():
            acc[...] += part

        @pl.when(i == nt - 1)
        def _():
            dw_ref[0] = acc[...].astype(BF16)

    return pl.pallas_call(
        body, name="in_proj_bwd_w", grid=(nd, nt),
        in_specs=[pl.BlockSpec((tm, d), lambda j, i: (i, 0)), pl.BlockSpec((tm, nb), lambda j, i: (i, j))],
        out_specs=pl.BlockSpec((1, d, nb), lambda j, i: (j, 0, 0)),
        out_shape=jax.ShapeDtypeStruct((nd, d, nb), BF16),
        scratch_shapes=[pltpu.VMEM((d, nb), F32)],
        compiler_params=_params("arbitrary", "arbitrary"),
    )(hn, dp)


def _loss_head(h, g, target, *, tm):
    t, d = h.shape
    nt = t // tm

    def body(h_ref, g_ref, t_ref, loss_ref, dh_ref, dg_ref):
        i = pl.program_id(0)
        x = h_ref[...]
        r = lax.rsqrt(jnp.mean(x * x, axis=-1, keepdims=True) + EPS_RMS)
        xr = x * r
        err = xr * g_ref[...] - t_ref[...]
        lp = 0.5 * jnp.sum(jnp.mean(err * err, axis=-1, keepdims=True), axis=0, keepdims=True)
        dy = err * (1.0 / d)
        gd = dy * g_ref[...]
        m = jnp.mean(gd * x, axis=-1, keepdims=True)
        dh_ref[...] = r * gd - x * (r * r * r * m)
        dgp = jnp.sum(dy * xr, axis=0, keepdims=True)

        @pl.when(i == 0)
        def _():
            loss_ref[...] = lp
            dg_ref[...] = dgp

        @pl.when(i > 0)
        def _():
            loss_ref[...] += lp
            dg_ref[...] += dgp

    return pl.pallas_call(
        body, name="loss_head", grid=(nt,),
        in_specs=[pl.BlockSpec((tm, d), lambda i: (i, 0)), _full((1, d)), pl.BlockSpec((tm, d), lambda i: (i, 0))],
        out_specs=[_full((1, 1)), pl.BlockSpec((tm, d), lambda i: (i, 0)), _full((1, d))],
        out_shape=[jax.ShapeDtypeStruct((1, 1), F32), jax.ShapeDtypeStruct((t, d), F32),
                   jax.ShapeDtypeStruct((1, d), F32)],
        compiler_params=_params("arbitrary"),
    )(h, g, target)


def _col(tm, w, c):
    return pl.BlockSpec((tm, w), lambda i: (i, c))


def _prev_halo(tm, rows, w, c):
    per = tm // rows
    return pl.BlockSpec((rows, w), lambda i: (jnp.maximum(i * per - 1, 0), c))


def _next_halo(tm, rows, w, c, t):
    per = tm // rows
    last = t // rows - 1
    return pl.BlockSpec((rows, w), lambda i: (jnp.minimum((i + 1) * per, last), c))


def _inv_count(first_row, rows, window):
    tpos = first_row + lax.broadcasted_iota(jnp.int32, (rows, 1), 0)
    return 1.0 / jnp.minimum(tpos + 1, window).astype(F32)


def _even_fwd(p, cw, cb, lg, lb, pw, pb, ps, *, tm):
    t = p.shape[0]
    wc = W_CONV

    def body(av, ag, agate, bv, bgate, avh, agh, bvh, cw_ref, cb_ref, lg_ref, lb_ref, pw_ref, pb_ref, ps_ref,
             y_ref, u1_ref, e_ref, d_ref, uext, vext):
        i = pl.program_id(0)
        keep = (i > 0).astype(F32)
        uext[0:HALO, :] = keep * (avh[...] * _sigmoid(agh[...]))
        uext[HALO:, :] = av[...] * _sigmoid(ag[...])
        vext[0:HALO, :] = keep * bvh[...]
        vext[HALO:, :] = bv[...]
        for c in range(0, wc, 128):
            for rb in range(0, tm, CONV_ROWS):
                acc = jnp.broadcast_to(cb_ref[:, c:c + 128], (CONV_ROWS, 128))
                for r in range(8):
                    shifted = uext[pl.ds(rb + 8 - r, CONV_ROWS + HALO - 8), c:c + 128]
                    for q in range(HALO // 8):
                        s = 8 * q + r
                        if s < CONV_K:
                            acc = acc + cw_ref[CONV_K - 1 - s:CONV_K - s, c:c + 128] * shifted[24 - 8 * q:24 - 8 * q + CONV_ROWS]
                u1_ref[rb:rb + CONV_ROWS, c:c + 128] = acc
        u1 = u1_ref[...]
        mu = jnp.mean(u1, axis=-1, keepdims=True)
        xc = u1 - mu
        rs = lax.rsqrt(jnp.mean(xc * xc, axis=-1, keepdims=True) + EPS_LN)
        u2 = (xc * rs) * lg_ref[...] + lb_ref[...]
        u3 = u2 * _sigmoid(u2)
        ga = agate[...]
        y_ref[:, 0:wc] = (u3 * (ga * _sigmoid(ga))).astype(BF16)
        for g, win in enumerate(POOL_WINDOWS):
            cs = slice(g * POOL_GW, (g + 1) * POOL_GW)
            s = vext[pl.ds(HALO, tm), cs]
            for j in range(1, win):
                s = s + vext[pl.ds(HALO - j, tm), cs]
            dg = s * _inv_count(i * tm, tm, win) - vext[pl.ds(HALO, tm), cs]
            dgb = dg.astype(BF16)
            d_ref[:, cs] = dgb
            eg = jnp.dot(dgb, pw_ref[g], preferred_element_type=F32) + pb_ref[:, cs]
            e_ref[:, cs] = eg
            gb = bgate[:, cs]
            y_ref[:, wc + g * POOL_GW:wc + (g + 1) * POOL_GW] = ((eg * ps_ref[:, cs]) * (gb * _sigmoid(gb))).astype(BF16)

    row = lambda w: pl.BlockSpec((tm, w), lambda i: (i, 0))
    return pl.pallas_call(
        body, name="even_fwd", grid=(t // tm,),
        in_specs=[_col(tm, wc, 0), _col(tm, wc, 1), _col(tm, wc, 2), _col(tm, wc, 3), _col(tm, wc, 4),
                  _prev_halo(tm, HALO, wc, 0), _prev_halo(tm, HALO, wc, 1), _prev_halo(tm, HALO, wc, 3),
                  _full((32, wc)), _full((1, wc)), _full((1, wc)), _full((1, wc)),
                  _full((4, POOL_GW, POOL_GW)), _full((1, wc)), _full((1, wc))],
        out_specs=[row(2 * wc), row(wc), row(wc), row(wc)],
        out_shape=[jax.ShapeDtypeStruct((t, 2 * wc), BF16), jax.ShapeDtypeStruct((t, wc), F32),
                   jax.ShapeDtypeStruct((t, wc), F32), jax.ShapeDtypeStruct((t, wc), BF16)],
        scratch_shapes=[pltpu.VMEM((tm + HALO, wc), F32), pltpu.VMEM((tm + HALO, wc), F32)],
        compiler_params=_params("arbitrary"),
    )(p, p, p, p, p, p, p, p, cw, cb, lg, lb, pw, pb, ps)


def _acc_out(i, ref, val):
    @pl.when(i == 0)
    def _():
        ref[...] = val

    @pl.when(i > 0)
    def _():
        ref[...] += val


def _even_bwd_a(p, u1, e, dmat, dy, lg, lb, pw, ps, dep, *, tm):
    t = p.shape[0]
    wc = W_CONV

    def body(agate, bgate, u1_ref, e_ref, d_ref, dya, dyb, lg_ref, lb_ref, pw_ref, ps_ref, dep_ref,
             du1_ref, dd_ref, dgat_ref, dlg_ref, dlb_ref, dpb_ref, dps_ref, dpw_ref):
        i = pl.program_id(0)

        @pl.when(i == 0)
        def _():
            dpw_ref[...] = jnp.zeros_like(dpw_ref)

        u1 = u1_ref[...]
        mu = jnp.mean(u1, axis=-1, keepdims=True)
        xc = u1 - mu
        rs = lax.rsqrt(jnp.mean(xc * xc, axis=-1, keepdims=True) + EPS_LN)
        xh = xc * rs
        u2 = xh * lg_ref[...] + lb_ref[...]
        s2 = _sigmoid(u2)
        ga = agate[...]
        sa = _sigmoid(ga)
        dy_a = dya[...]
        dgat_ref[:, 0:wc] = (dy_a * (u2 * s2) * _dsilu(ga, sa)).astype(BF16)
        du2 = dy_a * (ga * sa) * _dsilu(u2, s2)
        _acc_out(i, dlg_ref, jnp.sum(du2 * xh, axis=0, keepdims=True))
        _acc_out(i, dlb_ref, jnp.sum(du2, axis=0, keepdims=True))
        dxh = du2 * lg_ref[...]
        m1 = jnp.mean(dxh, axis=-1, keepdims=True)
        m2 = jnp.mean(dxh * xh, axis=-1, keepdims=True)
        du1_ref[...] = rs * (dxh - m1 - xh * m2)

        gb = bgate[...]
        sb = _sigmoid(gb)
        ev = e_ref[...]
        dy_b = dyb[...]
        dgat_ref[:, wc:2 * wc] = (dy_b * (ev * ps_ref[...]) * _dsilu(gb, sb)).astype(BF16)
        dz = dy_b * (gb * sb)
        _acc_out(i, dps_ref, jnp.sum(dz * ev, axis=0, keepdims=True))
        de = dz * ps_ref[...]
        _acc_out(i, dpb_ref, jnp.sum(de, axis=0, keepdims=True))
        for g in range(len(POOL_WINDOWS)):
            cs = slice(g * POOL_GW, (g + 1) * POOL_GW)
            deg = de[:, cs].astype(BF16)
            dd_ref[:, cs] = lax.dot_general(deg, pw_ref[g], (((1,), (1,)), ((), ())), preferred_element_type=F32)
            dpw_ref[g] += lax.dot_general(d_ref[:, cs], deg, (((0,), (0,)), ((), ())), preferred_element_type=F32)

    row = lambda w: pl.BlockSpec((tm, w), lambda i: (i, 0))
    return pl.pallas_call(
        body, name="even_bwd_a", grid=(t // tm,),
        in_specs=[_col(tm, wc, 2), _col(tm, wc, 4), row(wc), row(wc), row(wc), _col(tm, wc, 0), _col(tm, wc, 1),
                  _full((1, wc)), _full((1, wc)), _full((4, POOL_GW, POOL_GW)), _full((1, wc)), ANY],
        out_specs=[row(wc), row(wc), row(2 * wc), _full((1, wc)), _full((1, wc)), _full((1, wc)), _full((1, wc)),
                   _full((4, POOL_GW, POOL_GW))],
        out_shape=[jax.ShapeDtypeStruct((t, wc), F32), jax.ShapeDtypeStruct((t, wc), F32),
                   jax.ShapeDtypeStruct((t, 2 * wc), BF16)] + [jax.ShapeDtypeStruct((1, wc), F32)] * 4
                  + [jax.ShapeDtypeStruct((4, POOL_GW, POOL_GW), F32)],
        compiler_params=_params("arbitrary"),
    )(p, p, u1, e, dmat, dy, dy, lg, lb, pw, ps, dep)


def _even_bwd_b(p, du1, dd, dgat, cw, *, tm):
    t = p.shape[0]
    wc = W_CONV
    nt = t // tm

    def body(av, ag, avh, agh, du1_ref, du1n, dd_ref, ddn, dgat_ref, cw_ref, dp_ref, dcw_ref, dcb_ref,
             uext, gext, dext, du0, dcw8):
        i = pl.program_id(0)
        keep_p = (i > 0).astype(F32)
        keep_n = (i < nt - 1).astype(F32)

        @pl.when(i == 0)
        def _():
            dcw8[...] = jnp.zeros_like(dcw8)

        a = av[...]
        sg = _sigmoid(ag[...])
        uext[0:HALO, :] = keep_p * (avh[...] * _sigmoid(agh[...]))
        uext[HALO:, :] = a * sg
        gext[0:tm, :] = du1_ref[...]
        gext[tm:, :] = keep_n * du1n[...]
        for c in range(0, wc, 128):
            for rb in range(0, tm, CONV_ROWS):
                acc = jnp.zeros((CONV_ROWS, 128), F32)
                for r in range(8):
                    ahead = gext[pl.ds(rb + r, CONV_ROWS + HALO - 8), c:c + 128]
                    for q in range(HALO // 8):
                        s = 8 * q + r
                        if s < CONV_K:
                            acc = acc + cw_ref[CONV_K - 1 - s:CONV_K - s, c:c + 128] * ahead[8 * q:8 * q + CONV_ROWS]
                du0[rb:rb + CONV_ROWS, c:c + 128] = acc
                gcur = du1_ref[rb:rb + CONV_ROWS, c:c + 128]
                for r in range(8):
                    behind = uext[pl.ds(rb + 8 - r, CONV_ROWS + HALO - 8), c:c + 128]
                    for q in range(HALO // 8):
                        s = 8 * q + r
                        if s < CONV_K:
                            prod = gcur * behind[24 - 8 * q:24 - 8 * q + CONV_ROWS]
                            part = prod[0:8]
                            for o in range(8, CONV_ROWS, 8):
                                part = part + prod[o:o + 8]
                            k = CONV_K - 1 - s
                            dcw8[8 * k:8 * k + 8, c:c + 128] += part

        @pl.when(i == nt - 1)
        def _():
            for k in range(CONV_K):
                dcw_ref[k:k + 1, :] = jnp.sum(dcw8[8 * k:8 * k + 8, :], axis=0, keepdims=True)
            dcw_ref[CONV_K:32, :] = jnp.zeros((32 - CONV_K, wc), F32)

        _acc_out(i, dcb_ref, jnp.sum(du1_ref[...], axis=0, keepdims=True))
        g0 = du0[...]
        dp_ref[:, 0:wc] = (g0 * sg).astype(BF16)
        dp_ref[:, wc:2 * wc] = (g0 * a * sg * (1.0 - sg)).astype(BF16)
        dp_ref[:, 2 * wc:3 * wc] = dgat_ref[:, 0:wc]
        dp_ref[:, 4 * wc:5 * wc] = dgat_ref[:, wc:2 * wc]
        for g, win in enumerate(POOL_WINDOWS):
            cs = slice(g * POOL_GW, (g + 1) * POOL_GW)
            dext[0:tm, cs] = dd_ref[:, cs] * _inv_count(i * tm, tm, win)
            dext[tm:, cs] = keep_n * (ddn[:, cs] * _inv_count((i + 1) * tm, HALO, win))
            s = dext[pl.ds(0, tm), cs]
            for j in range(1, win):
                s = s + dext[pl.ds(j, tm), cs]
            dp_ref[:, 3 * wc + g * POOL_GW:3 * wc + (g + 1) * POOL_GW] = (s - dd_ref[:, cs]).astype(BF16)

    row = lambda w: pl.BlockSpec((tm, w), lambda i: (i, 0))
    return pl.pallas_call(
        body, name="even_bwd_b", grid=(nt,),
        in_specs=[_col(tm, wc, 0), _col(tm, wc, 1), _prev_halo(tm, HALO, wc, 0), _prev_halo(tm, HALO, wc, 1),
                  row(wc), _next_halo(tm, HALO, wc, 0, t), row(wc), _next_halo(tm, HALO, wc, 0, t), row(2 * wc),
                  _full((32, wc))],
        out_specs=[row(5 * wc), _full((32, wc)), _full((1, wc))],
        out_shape=[jax.ShapeDtypeStruct((t, 5 * wc), BF16), jax.ShapeDtypeStruct((32, wc), F32),
                   jax.ShapeDtypeStruct((1, wc), F32)],
        scratch_shapes=[pltpu.VMEM((tm + HALO, wc), F32), pltpu.VMEM((tm + HALO, wc), F32),
                        pltpu.VMEM((tm + HALO, wc), F32), pltpu.VMEM((tm, wc), F32), pltpu.VMEM((8 * 32, wc), F32)],
        compiler_params=_params("arbitrary"),
    )(p, p, p, p, du1, du1, dd, dd, dgat, cw)


def _softplus_neg(lam):
    z = -lam
    return jnp.maximum(z, 0.0) + jnp.log1p(jnp.exp(-jnp.abs(z)))


def _one_minus_exp(x):
    series = -x * (1.0 + x * (0.5 + x * (1.0 / 6.0 + x * (1.0 / 24.0))))
    return jnp.where(x > -0.02, series, 1.0 - jnp.exp(x))


def _odd_fwd(p, ccw, ccb, wrg, brg, wig, big, lam, *, tm):
    t = p.shape[0]
    wl = W_LRU
    ng = tm // 8

    def body(xr, gate, xrh, ccw_ref, ccb_ref, wrg_ref, brg_ref, wig_ref, big_ref, lam_ref,
             y_ref, xc_ref, r_ref, i_ref, hs_ref, xext, a_s, b_s, carry):
        i = pl.program_id(0)
        keep = (i > 0).astype(F32)
        xext[0:HALO_C, :] = keep * xrh[...]
        xext[HALO_C:, :] = xr[...]
        xc = jnp.broadcast_to(ccb_ref[...], (tm, wl))
        for k in range(LRU_CONV_K):
            xc = xc + ccw_ref[k:k + 1, :] * xext[pl.ds(HALO_C - (LRU_CONV_K - 1) + k, tm), :]
        xc_ref[...] = xc
        for h in range(LRU_HEADS):
            cs = slice(h * LRU_HD, (h + 1) * LRU_HD)
            xh = xc_ref[:, cs].astype(BF16)
            r_ref[:, cs] = _sigmoid(jnp.dot(xh, wrg_ref[h], preferred_element_type=F32) + brg_ref[:, cs])
            i_ref[:, cs] = _sigmoid(jnp.dot(xh, wig_ref[h], preferred_element_type=F32) + big_ref[:, cs])
        log_a = (-LRU_C * _softplus_neg(lam_ref[...])) * r_ref[...]
        a_s[...] = jnp.exp(log_a)
        b_s[...] = jnp.sqrt(_one_minus_exp(2.0 * log_a)) * (i_ref[...] * xc_ref[...])

        @pl.when(i == 0)
        def _():
            carry[...] = jnp.zeros_like(carry)

        rowi = lax.broadcasted_iota(jnp.int32, (8, wl), 0)

        def step(g, c):
            sl = pl.ds(pl.multiple_of(g * 8, 8), 8)
            aa, bb = a_s[sl, :], b_s[sl, :]
            for s in (1, 2, 4):
                m = rowi >= s
                a_sh = jnp.where(m, pltpu.roll(aa, s, 0), 1.0)
                b_sh = jnp.where(m, pltpu.roll(bb, s, 0), 0.0)
                bb = aa * b_sh + bb
                aa = aa * a_sh
            hv = bb + aa * c
            hs_ref[sl, :] = hv
            return hv[7:8, :]

        carry[...] = lax.fori_loop(0, ng, step, carry[...])
        gt = gate[...]
        y_ref[...] = (hs_ref[...] * (gt * _sigmoid(gt))).astype(BF16)

    row = lambda w: pl.BlockSpec((tm, w), lambda i: (i, 0))
    return pl.pallas_call(
        body, name="odd_fwd", grid=(t // tm,),
        in_specs=[_col(tm, wl, 0), _col(tm, wl, 1), _prev_halo(tm, HALO_C, wl, 0), _full((8, wl)), _full((1, wl)),
                  _full((LRU_HEADS, LRU_HD, LRU_HD)), _full((1, wl)), _full((LRU_HEADS, LRU_HD, LRU_HD)),
                  _full((1, wl)), _full((1, wl))],
        out_specs=[row(wl)] * 5,
        out_shape=[jax.ShapeDtypeStruct((t, wl), BF16)] + [jax.ShapeDtypeStruct((t, wl), F32)] * 4,
        scratch_shapes=[pltpu.VMEM((tm + HALO_C, wl), F32), pltpu.VMEM((tm, wl), F32), pltpu.VMEM((tm, wl), F32),
                        pltpu.VMEM((1, wl), F32)],
        compiler_params=_params("arbitrary"),
    )(p, p, p, ccw, ccb, wrg, brg, wig, big, lam)


def _odd_bwd_a(p, xc, r, ig, hs, dy, wrg, wig, lam, dep, *, tm):
    t = p.shape[0]
    wl = W_LRU
    nt = t // tm
    ng = tm // 8
    per = tm // HALO_C

    def body(gate, xc_ref, r_ref, i_ref, hs_ref, hsh, dy_ref, wrg_ref, wig_ref, lam_ref, dep_ref,
             dxc_ref, dgate_ref, dwrg_ref, dwig_ref, dbrg_ref, dbig_ref, dlam_ref,
             hext, a_s, q_s, g_s, dpr_s, dpi_s, carry):
        i = pl.program_id(0)
        ti = nt - 1 - i
        keep = (ti > 0).astype(F32)
        hext[0:HALO_C, :] = keep * hsh[...]
        hext[HALO_C:, :] = hs_ref[...]
        gt = gate[...]
        sg = _sigmoid(gt)
        dyv = dy_ref[...]
        dgate_ref[...] = (dyv * hs_ref[...] * _dsilu(gt, sg)).astype(BF16)
        q_s[...] = dyv * (gt * sg)
        sp = _softplus_neg(lam_ref[...])
        log_a = (-LRU_C * sp) * r_ref[...]
        a_s[...] = jnp.exp(log_a)

        @pl.when(i == 0)
        def _():
            carry[...] = jnp.zeros_like(carry)
            dwrg_ref[...] = jnp.zeros_like(dwrg_ref)
            dwig_ref[...] = jnp.zeros_like(dwig_ref)

        rowi = lax.broadcasted_iota(jnp.int32, (8, wl), 0)

        def step(gr, c):
            sl = pl.ds(pl.multiple_of((ng - 1 - gr) * 8, 8), 8)
            a0 = a_s[sl, :]
            al = jnp.where(rowi < 7, pltpu.roll(a0, 7, 0), 1.0)
            be = q_s[sl, :]
            for s in (1, 2, 4):
                m = rowi + s <= 7
                al_sh = jnp.where(m, pltpu.roll(al, 8 - s, 0), 1.0)
                be_sh = jnp.where(m, pltpu.roll(be, 8 - s, 0), 0.0)
                be = be + al * be_sh
                al = al * al_sh
            gv = be + al * c
            g_s[sl, :] = gv
            return (a0 * gv)[0:1, :]

        carry[...] = lax.fori_loop(0, ng, step, carry[...])

        gv = g_s[...]
        a = a_s[...]
        mult = jnp.sqrt(_one_minus_exp(2.0 * log_a))
        iv = i_ref[...]
        rv = r_ref[...]
        xcv = xc_ref[...]
        hprev = hext[pl.ds(HALO_C - 1, tm), :]
        dla = gv * hprev * a - (gv * iv * xcv) * (a * a) / mult
        di = gv * mult * xcv
        dpr = (dla * (-LRU_C * sp)) * rv * (1.0 - rv)
        dpi = di * iv * (1.0 - iv)
        dpr_s[...] = dpr
        dpi_s[...] = dpi
        dxc_ref[...] = gv * mult * iv
        dsp = jnp.sum(dla * rv, axis=0, keepdims=True) * (-LRU_C)
        _acc_out(i, dlam_ref, -dsp * jax.nn.sigmoid(-lam_ref[...]))
        _acc_out(i, dbrg_ref, jnp.sum(dpr, axis=0, keepdims=True))
        _acc_out(i, dbig_ref, jnp.sum(dpi, axis=0, keepdims=True))
        for h in range(LRU_HEADS):
            cs = slice(h * LRU_HD, (h + 1) * LRU_HD)
            xh = xc_ref[:, cs].astype(BF16)
            dr_h = dpr_s[:, cs].astype(BF16)
            di_h = dpi_s[:, cs].astype(BF16)
            dxc_ref[:, cs] += (
                lax.dot_general(dr_h, wrg_ref[h], (((1,), (1,)), ((), ())), preferred_element_type=F32)
                + lax.dot_general(di_h, wig_ref[h], (((1,), (1,)), ((), ())), preferred_element_type=F32))
            dwrg_ref[h] += lax.dot_general(xh, dr_h, (((0,), (0,)), ((), ())), preferred_element_type=F32)
            dwig_ref[h] += lax.dot_general(xh, di_h, (((0,), (0,)), ((), ())), preferred_element_type=F32)

    rrow = lambda w: pl.BlockSpec((tm, w), lambda i: (nt - 1 - i, 0))
    hspec = pl.BlockSpec((HALO_C, wl), lambda i: (jnp.maximum((nt - 1 - i) * per - 1, 0), 0))
    wspec = _full((LRU_HEADS, LRU_HD, LRU_HD))
    return pl.pallas_call(
        body, name="odd_bwd_a", grid=(nt,),
        in_specs=[pl.BlockSpec((tm, wl), lambda i: (nt - 1 - i, 1)), rrow(wl), rrow(wl), rrow(wl), rrow(wl), hspec,
                  rrow(wl), wspec, wspec, _full((1, wl)), ANY],
        out_specs=[rrow(wl), rrow(wl), wspec, wspec, _full((1, wl)), _full((1, wl)), _full((1, wl))],
        out_shape=[jax.ShapeDtypeStruct((t, wl), F32), jax.ShapeDtypeStruct((t, wl), BF16),
                   jax.ShapeDtypeStruct((LRU_HEADS, LRU_HD, LRU_HD), F32),
                   jax.ShapeDtypeStruct((LRU_HEADS, LRU_HD, LRU_HD), F32)] + [jax.ShapeDtypeStruct((1, wl), F32)] * 3,
        scratch_shapes=[pltpu.VMEM((tm + HALO_C, wl), F32)] + [pltpu.VMEM((tm, wl), F32)] * 5
                       + [pltpu.VMEM((1, wl), F32)],
        compiler_params=_params("arbitrary"),
    )(p, xc, r, ig, hs, hs, dy, wrg, wig, lam, dep)


def _odd_bwd_b(p, dxc, dgate, ccw, *, tm):
    t = p.shape[0]
    wl = W_LRU
    nt = t // tm

    def body(xr, xrh, dxc_ref, dxcn, dgate_ref, ccw_ref, dp_ref, dcw_ref, dcb_ref, xext, gext):
        i = pl.program_id(0)

        @pl.when(i == 0)
        def _():
            dcw_ref[...] = jnp.zeros_like(dcw_ref)

        xext[0:HALO_C, :] = (i > 0).astype(F32) * xrh[...]
        xext[HALO_C:, :] = xr[...]
        gext[0:tm, :] = dxc_ref[...]
        gext[tm:, :] = (i < nt - 1).astype(F32) * dxcn[...]
        g = dxc_ref[...]
        acc = jnp.zeros((tm, wl), F32)
        for k in range(LRU_CONV_K):
            acc = acc + ccw_ref[k:k + 1, :] * gext[pl.ds(LRU_CONV_K - 1 - k, tm), :]
            dcw_ref[k:k + 1, :] += jnp.sum(
                g * xext[pl.ds(HALO_C - (LRU_CONV_K - 1) + k, tm), :], axis=0, keepdims=True)

        _acc_out(i, dcb_ref, jnp.sum(g, axis=0, keepdims=True))
        dp_ref[:, 0:wl] = acc.astype(BF16)
        dp_ref[:, wl:2 * wl] = dgate_ref[...]

    row = lambda w: pl.BlockSpec((tm, w), lambda i: (i, 0))
    return pl.pallas_call(
        body, name="odd_bwd_b", grid=(nt,),
        in_specs=[_col(tm, wl, 0), _prev_halo(tm, HALO_C, wl, 0), row(wl), _next_halo(tm, HALO_C, wl, 0, t), row(wl),
                  _full((8, wl))],
        out_specs=[row(2 * wl), _full((8, wl)), _full((1, wl))],
        out_shape=[jax.ShapeDtypeStruct((t, 2 * wl), BF16), jax.ShapeDtypeStruct((8, wl), F32),
                   jax.ShapeDtypeStruct((1, wl), F32)],
        scratch_shapes=[pltpu.VMEM((tm + HALO_C, wl), F32), pltpu.VMEM((tm + HALO_C, wl), F32)],
        compiler_params=_params("arbitrary"),
    )(p, p, dxc, dxc, dgate, ccw)


def _local_step(x, target, layer_weights, final_norm, on_grads):
    t = x.shape[0]
    tm, tx, tl, tw = min(TM_MATMUL, t), min(TM_MIXER, t), min(TM_STREAM, t), min(TM_WGRAD, t)
    h = x
    saved = []
    for layer in range(N_LAYERS):
        w = layer_weights(layer, h)
        p, hn = _norm_matmul(h, w["norm"], w["w_in"], w["dep"], tm=tl)
        if layer % 2 == 0:
            y, *acts = _even_fwd(p, w["conv_w"], w["conv_b"], w["ln_g"], w["ln_b"], w["pool_w"], w["pool_b"],
                                 w["pool_scale"], tm=tx)
        else:
            y, *acts = _odd_fwd(p, w["conv_w"], w["conv_b"], w["w_rg"], w["b_rg"], w["w_ig"], w["b_ig"], w["lam"],
                                tm=tx)
        w_out, dep = w["w_out"](y)
        saved.append((w, w_out, h, p, hn, y, acts))
        h = _out_proj(h, y, w_out, dep, tm=tm)
    loss, dh, d_final = _loss_head(h, final_norm, target, tm=tm)

    dep = d_final
    for layer in reversed(range(N_LAYERS)):
        w, w_out, h_in, p, hn, y, acts = saved[layer]
        sfx = "even" if layer % 2 == 0 else "odd"
        dy, dw_out = _out_proj_bwd(dh, y, w_out, dep, tm=tm)
        dep = on_grads(layer, {"w_out_" + sfx: dw_out}, dep, False)
        if layer % 2 == 0:
            u1, e, dmat = acts
            du1, dd, dgat, dlg, dlb, dpb, dps, dpw = _even_bwd_a(p, u1, e, dmat, dy, w["ln_g"], w["ln_b"], w["pool_w"],
                                                                 w["pool_scale"], dep, tm=tx)
            dp, dcw, dcb = _even_bwd_b(p, du1, dd, dgat, w["conv_w"], tm=tx)
            grads = dict(conv_a_w=dcw[:CONV_K], conv_a_b=dcb, ln_a_g=dlg, ln_a_b=dlb, pool_w=dpw, pool_b=dpb,
                         pool_scale=dps)
        else:
            xc, r, ig, hs = acts
            dxc, dgate, dwrg, dwig, dbrg, dbig, dlam = _odd_bwd_a(p, xc, r, ig, hs, dy, w["w_rg"], w["w_ig"],
                                                                  w["lam"], dep, tm=tx)
            dp, dccw, dccb = _odd_bwd_b(p, dxc, dgate, w["conv_w"], tm=tx)
            grads = dict(conv_c_w=dccw[:LRU_CONV_K], conv_c_b=dccb, w_rg=dwrg, b_rg=dbrg, w_ig=dwig, b_ig=dbig,
                         lru_lambda=dlam)
        grads["w_in_" + sfx] = _in_proj_bwd_w(hn, dp, N_DEV, tm=tw)
        dep = on_grads(layer, grads, dep, False)
        dh, dg = _in_proj_bwd_x(dp, w["w_in"], h_in, w["norm"], dh, dep, tm=tm)
        rest = {"norm_" + sfx: dg}
        if layer == N_LAYERS - 1:
            rest["final_norm"] = d_final
        dep = on_grads(layer, rest, dep, True)
    return loss, dh


def _slot(px, py, pc):
    return 4 * px + 2 * py + pc


def _peers(x, y, c):
    return [(1 - x if k & 4 else x, 1 - y if k & 2 else y, 1 - c if k & 1 else c) for k in range(1, N_DEV)]


def _all_gather(arrs, name):
    n = len(arrs)

    def body(*refs):
        ins, outs = refs[:n], refs[n:2 * n]
        send_sems, recv_sems, local_sems = refs[2 * n:]
        x, y, c = lax.axis_index("x"), lax.axis_index("y"), lax.axis_index("c")
        me, sibling = (x, y, c), (x, y, 1 - c)
        chips = [(1 - x, y), (x, 1 - y), (1 - x, 1 - y)]

        def copy(a, k, block, to, src=None):
            rows = outs[a].at[_slot(*block)]
            return pltpu.make_async_remote_copy(
                src_ref=rows if src is None else src, dst_ref=rows, send_sem=send_sems.at[a, k],
                recv_sem=recv_sems.at[a, k], device_id=to, device_id_type=MESH)

        mine = [pltpu.make_async_copy(ins[a], outs[a].at[_slot(*me)], local_sems.at[a]) for a in range(n)]
        for cp in mine:
            cp.start()
        first = []
        for a in range(n):
            first.append(copy(a, 0, me, sibling, src=ins[a]))
            first += [copy(a, 1 + j, me, (*chip, c), src=ins[a]) for j, chip in enumerate(chips)]
        for cp in first:
            cp.start()
        passed = []
        for j, chip in enumerate(chips):
            for a in range(n):
                copy(a, 1 + j, (*chip, c), me).wait_recv()
                fwd = copy(a, 4 + j, (*chip, c), sibling)
                fwd.start()
                passed.append(fwd)
        for a in range(n):
            copy(a, 0, sibling, me).wait_recv()
            for j, chip in enumerate(chips):
                copy(a, 4 + j, (*chip, 1 - c), me).wait_recv()
        for cp in first + passed:
            cp.wait_send()
        for cp in mine:
            cp.wait()

    return pl.pallas_call(
        body, name=name,
        in_specs=[ANY] * n, out_specs=[ANY] * n,
        out_shape=[jax.ShapeDtypeStruct((N_DEV,) + a.shape, a.dtype) for a in arrs],
        scratch_shapes=[pltpu.SemaphoreType.DMA((n, 7)), pltpu.SemaphoreType.DMA((n, 7)),
                        pltpu.SemaphoreType.DMA((n,))],
    )(*arrs)


N_COPIES = {"gather": N_PEERS, "scatter": N_PEERS, "chip_gather": 4, "forward": 3, "pair_scatter": 4,
            "chip_scatter": 3}


def _exchange_plan(mode, x, y, c):
    me = _slot(x, y, c)
    chips = [(1 - x, y), (x, 1 - y), (1 - x, 1 - y)]
    if mode == "forward":
        return [((x, y, 1 - c), ("land", _slot(*chip, c)), _slot(*chip, c), _slot(*chip, 1 - c)) for chip in chips]
    if mode == "pair_scatter":
        return [((x, y, 1 - c), ("src", _slot(q // 2, q % 2, 1 - c)), q, q) for q in range(4)]
    if mode == "chip_scatter":
        return [((*chip, c), ("src", 2 * chip[0] + chip[1]), 2 * x + y, 2 * chip[0] + chip[1]) for chip in chips]
    peers = _peers(x, y, c)
    if mode == "chip_gather":
        peers = [(x, y, 1 - c), (1 - x, y, c), (x, 1 - y, c), (1 - x, 1 - y, c)]
    return [(p, ("src", _slot(*p)) if mode == "scatter" else ("src", None), me, _slot(*p)) for p in peers]


def _exchange_copy(src_ref, land_ref, plan, send_sem, recv_sem, start):
    peer, (which, block), there, here = plan
    src = land_ref if which == "land" else src_ref
    return pltpu.make_async_remote_copy(
        src_ref=src if block is None else src.at[block], dst_ref=land_ref.at[there if start else here],
        send_sem=send_sem, recv_sem=recv_sem, device_id=peer, device_id_type=MESH)


def _exchange_start(groups, deps, name):
    flat = [pair for _, g in groups for pair in g]
    n, ng = len(flat), len(groups)

    def body(*refs):
        src_refs, land_refs = refs[:n], refs[n:2 * n]
        outs = refs[2 * n + len(deps):]
        sems, token = outs[:2 * ng], outs[2 * ng + 2 * n]
        x, y, c = lax.axis_index("x"), lax.axis_index("y"), lax.axis_index("c")
        base = 0
        for gi, (mode, g) in enumerate(groups):
            nc = N_COPIES[mode]
            for k, plan in enumerate(_exchange_plan(mode, x, y, c)):
                for ai in range(len(g)):
                    _exchange_copy(src_refs[base + ai], land_refs[base + ai], plan, sems[2 * gi].at[ai * nc + k],
                                   sems[2 * gi + 1].at[ai * nc + k], True).start()
            base += len(g)
        token[...] = jnp.zeros_like(token)

    operands = [pltpu.with_memory_space_constraint(a, pltpu.HBM) for a in
                [s for s, _ in flat] + [l for _, l in flat]]
    out_shape = []
    for mode, g in groups:
        out_shape += [pltpu.SemaphoreType.DMA((len(g) * N_COPIES[mode],))] * 2
    out_shape += [pltpu.HBM(a.shape, a.dtype) for a in operands]
    out_shape.append(jax.ShapeDtypeStruct((8, 128), F32))
    outs = pl.pallas_call(
        body, name=name, out_shape=out_shape,
        in_specs=[HBM] * (2 * n) + [ANY] * len(deps),
        out_specs=[SEM] * (2 * ng) + [HBM] * (2 * n) + [pl.BlockSpec(memory_space=pltpu.VMEM)],
        input_output_aliases={i: 2 * ng + i for i in range(2 * n)},
        compiler_params=pltpu.CompilerParams(has_side_effects=pltpu.SideEffectType.DATAFLOW_SIDE_EFFECTING),
    )(*operands, *deps)
    handles, base = [], 0
    for gi, (mode, g) in enumerate(groups):
        srcs = outs[2 * ng + base:2 * ng + base + len(g)]
        lands = outs[2 * ng + n + base:2 * ng + n + base + len(g)]
        handles.append((mode, outs[2 * gi], outs[2 * gi + 1], list(srcs), list(lands)))
        base += len(g)
    return handles, outs[-1]


def _exchange_wait(handle, after, name):
    mode, send_sems, recv_sems, srcs, lands = handle
    n = len(srcs)
    nc = N_COPIES[mode]

    def body(*refs):
        src_refs, land_refs = refs[:n], refs[n:2 * n]
        send_ref, recv_ref = refs[2 * n], refs[2 * n + 1]
        x, y, c = lax.axis_index("x"), lax.axis_index("y"), lax.axis_index("c")
        for k, plan in enumerate(_exchange_plan(mode, x, y, c)):
            for a in range(n):
                cp = _exchange_copy(src_refs[a], land_refs[a], plan, send_ref.at[a * nc + k], recv_ref.at[a * nc + k],
                                    False)
                cp.wait_send()
                cp.wait_recv()

    outs = pl.pallas_call(
        body, name=name,
        out_shape=[pltpu.HBM(a.shape, a.dtype) for a in srcs + lands],
        in_specs=[HBM] * (2 * n) + [SEM, SEM] + [ANY] * len(after),
        out_specs=[HBM] * (2 * n),
        input_output_aliases={i: i for i in range(2 * n)},
        compiler_params=pltpu.CompilerParams(has_side_effects=pltpu.SideEffectType.DATAFLOW_SIDE_EFFECTING),
    )(*srcs, *lands, send_sems, recv_sems, *after)
    return list(outs[n:])


def _pair_sum(mine, theirs):
    nq, r, c = mine.shape
    tr = _row_tile(r)

    def body(a_ref, b_ref, o_ref):
        o_ref[...] = (a_ref[...].astype(F32) + b_ref[...].astype(F32)).astype(BF16)

    blk = pl.BlockSpec((1, tr, c), lambda q, i: (q, i, 0))
    return pl.pallas_call(
        body, name="pair_sum", grid=(nq, r // tr), in_specs=[blk, blk], out_specs=blk,
        out_shape=jax.ShapeDtypeStruct(mine.shape, BF16),
        compiler_params=_params("arbitrary", "arbitrary"),
    )(mine, theirs)


def _adamw_math(w, g, m, v):
    c1 = 1.0 - ADAM_B1 ** ADAM_STEP
    c2 = 1.0 - ADAM_B2 ** ADAM_STEP
    nm = ADAM_B1 * m + (1.0 - ADAM_B1) * g
    nv = ADAM_B2 * v + (1.0 - ADAM_B2) * (g * g)
    delta = -ADAM_LR * ((nm / c1) / (jnp.sqrt(nv / c2) + ADAM_EPS) + ADAM_WD * w)
    return delta, nm, nv


def _row_tile(r):
    for cand in (256, 128, 64, 32, 16, 8):
        if r % cand == 0 and r > cand:
            return cand
    return r


def _adamw(items, layer0, bufs, name):
    ni = len(items)
    nl = items[0][1].shape[1]
    tiles = [_row_tile(w.shape[1]) for w, _, _, _ in items]
    steps = [w.shape[1] // tr for (w, _, _, _), tr in zip(items, tiles)]
    ns = steps[0]
    assert all(s == ns for s in steps)
    nb = 0 if bufs is None else 4 * ni

    def body(*refs):
        ins, outs = refs[:4 * ni], refs[4 * ni + nb:]
        for k in range(ni):
            w_ref, p_ref, m_ref, v_ref = ins[4 * k:4 * k + 4]
            g = p_ref[0, 0].astype(F32)
            for s in range(1, p_ref.shape[0]):
                g = g + p_ref[s, 0].astype(F32)
            delta, nm, nv = _adamw_math(w_ref[0], g, m_ref[0], v_ref[0])
            g_ref, d_ref, nm_ref, nv_ref = outs[4 * k:4 * k + 4]
            g_ref[0], d_ref[0], nm_ref[0], nv_ref[0] = g, delta, nm, nv

    in_specs, out_specs, out_shape, operands = [], [], [], []
    for (w, parts, m, v), tr in zip(items, tiles):
        blk = pl.BlockSpec((1, tr, w.shape[2]), lambda l, i: (layer0 + l, i, 0))
        in_specs += [blk, pl.BlockSpec((parts.shape[0], 1, tr, w.shape[2]), lambda l, i: (0, l, i, 0)), blk, blk]
        operands += [w, parts, m, v]
        out_specs += [blk] * 4
        out_shape += [jax.ShapeDtypeStruct(w.shape, F32)] * 4
    if bufs is not None:
        in_specs += [ANY] * nb
        operands += [b for item in bufs for b in item]
    outs = pl.pallas_call(
        body, name=name, grid=(nl, ns), in_specs=in_specs, out_specs=out_specs, out_shape=out_shape,
        input_output_aliases={4 * ni + i: i for i in range(nb)},
        compiler_params=_params("arbitrary", "arbitrary"),
    )(*operands)
    return [tuple(outs[4 * k:4 * k + 4]) for k in range(ni)]


NAMES = ("norm_even", "w_in_even", "conv_a_w", "conv_a_b", "ln_a_g", "ln_a_b", "pool_w", "pool_b", "pool_scale",
         "w_out_even", "norm_odd", "w_in_odd", "conv_c_w", "conv_c_b", "w_rg", "b_rg", "w_ig", "b_ig", "lru_lambda",
         "w_out_odd", "final_norm")
SMALL_GATHERED = ("conv_a_w", "pool_b", "norm_odd", "conv_c_w", "conv_c_b", "b_rg", "b_ig", "lru_lambda")
BIG = (("w_in_even", "w_out_even"), ("w_in_odd", "w_out_odd"))
SMALL = (("conv_a_w", "pool_b", "pool_w"), ("norm_odd", "conv_c_w", "conv_c_b", "b_rg", "b_ig", "lru_lambda"))
REPLICATED = (("norm_even", "conv_a_b", "ln_a_g", "ln_a_b", "pool_scale"), ("w_rg", "w_ig"))
PACK_ROW = 1024


def _pack_rows(flat2d):
    pad = (-flat2d.shape[1]) % PACK_ROW
    return jnp.pad(flat2d, ((0, 0), (0, pad))).reshape(flat2d.shape[0], -1, 128)


def _unpack(flat, shapes):
    out, off = [], 0
    for s in shapes:
        n = 1
        for d in s:
            n *= d
        out.append(flat[..., off:off + n].reshape(flat.shape[:-1] + tuple(s)))
        off += n
    return out


def _to_global(name, g):
    if name in ("conv_a_w", "pool_b", "conv_c_w"):
        return jnp.transpose(g, (1, 2, 0, 3)).reshape(g.shape[1], g.shape[2], -1)
    if name == "pool_w":
        return jnp.transpose(g, (1, 2, 0, 3, 4)).reshape(2, 4, POOL_GW, POOL_GW)
    return jnp.transpose(g, (1, 0, 2)).reshape(g.shape[1], -1)


def _to_blocks(name, g):
    if name == "conv_a_w":
        return jnp.transpose(g.reshape(CONV_K, N_DEV, -1), (1, 0, 2))
    if name == "conv_c_w":
        return jnp.transpose(g.reshape(LRU_CONV_K, N_DEV, -1), (1, 0, 2))
    if name == "pool_b":
        return jnp.transpose(g.reshape(4, N_DEV, -1), (1, 0, 2))
    if name == "pool_w":
        return jnp.transpose(g.reshape(4, N_DEV, POOL_GW // N_DEV, POOL_GW), (1, 0, 2, 3))
    return g.reshape(N_DEV, -1)


def _as3d(a):
    if a.ndim == 1:
        return a.reshape(1, 1, -1)
    if a.ndim == 2:
        return a.reshape(a.shape[0], 1, a.shape[1])
    return a.reshape(a.shape[0], -1, a.shape[-1])


def kernel(x, norm_even, w_in_even, conv_a_w, conv_a_b, ln_a_g, ln_a_b, pool_w, pool_b, pool_scale, w_out_even, norm_odd, w_in_odd, conv_c_w, conv_c_b, w_rg, b_rg, w_ig, b_ig, lru_lambda, w_out_odd, final_norm, loss_target, m_norm_even, m_w_in_even, m_conv_a_w, m_conv_a_b, m_ln_a_g, m_ln_a_b, m_pool_w, m_pool_b, m_pool_scale, m_w_out_even, m_norm_odd, m_w_in_odd, m_conv_c_w, m_conv_c_b, m_w_rg, m_b_rg, m_w_ig, m_b_ig, m_lru_lambda, m_w_out_odd, m_final_norm, v_norm_even, v_w_in_even, v_conv_a_w, v_conv_a_b, v_ln_a_g, v_ln_a_b, v_pool_w, v_pool_b, v_pool_scale, v_w_out_even, v_norm_odd, v_w_in_odd, v_conv_c_w, v_conv_c_b, v_w_rg, v_b_rg, v_w_ig, v_b_ig, v_lru_lambda, v_w_out_odd, v_final_norm):
    w_loc = dict(zip(NAMES, [norm_even, w_in_even, conv_a_w, conv_a_b, ln_a_g, ln_a_b, pool_w, pool_b, pool_scale,
                             w_out_even, norm_odd, w_in_odd, conv_c_w, conv_c_b, w_rg, b_rg, w_ig, b_ig, lru_lambda,
                             w_out_odd, final_norm]))
    m_loc = dict(zip(NAMES, [m_norm_even, m_w_in_even, m_conv_a_w, m_conv_a_b, m_ln_a_g, m_ln_a_b, m_pool_w, m_pool_b,
                             m_pool_scale, m_w_out_even, m_norm_odd, m_w_in_odd, m_conv_c_w, m_conv_c_b, m_w_rg,
                             m_b_rg, m_w_ig, m_b_ig, m_lru_lambda, m_w_out_odd, m_final_norm]))
    v_loc = dict(zip(NAMES, [v_norm_even, v_w_in_even, v_conv_a_w, v_conv_a_b, v_ln_a_g, v_ln_a_b, v_pool_w, v_pool_b,
                             v_pool_scale, v_w_out_even, v_norm_odd, v_w_in_odd, v_conv_c_w, v_conv_c_b, v_w_rg,
                             v_b_rg, v_w_ig, v_b_ig, v_lru_lambda, v_w_out_odd, v_final_norm]))
    me = _slot(lax.axis_index("x"), lax.axis_index("y"), lax.axis_index("c"))

    def landing(own):
        zone = lax.empty((N_DEV,) + own.shape[1:], own.dtype)
        return lax.dynamic_update_slice(zone, own, (me,) + (0,) * (own.ndim - 1))

    small_shapes = [w_loc[n].shape for n in SMALL_GATHERED]
    small = jnp.concatenate([w_loc[n].reshape(1, -1) for n in SMALL_GATHERED], axis=1)
    first = _all_gather([w_in_even[0].astype(BF16), pool_w.astype(BF16), _pack_rows(small)[0]], "gather_first")
    g_small = dict(zip(SMALL_GATHERED, [_to_global(n, g) for n, g in
                                        zip(SMALL_GATHERED, _unpack(first[2].reshape(N_DEV, -1), small_shapes))]))
    pool_w_all = _to_global("pool_w", first[1])

    def pairs_of(shards):
        return [(s.astype(BF16), landing(s.astype(BF16)[None])) for s in shards]

    shards = {1: [w_in_odd[0], w_out_odd[0]], 2: [w_in_even[1], w_out_even[1]], 3: [w_in_odd[1], w_out_odd[1]]}
    leg_a, leg_b = {}, {}
    (w_out_0, leg_a[1]), token_1 = _exchange_start(
        [("gather", pairs_of([w_out_even[0]])), ("chip_gather", pairs_of(shards[1]))], [first[0]], "gather_start_1")
    unused = jnp.zeros((8, 128), F32)

    def second_leg(layer, y):
        lands = _exchange_wait(leg_a[layer], [y], f"gather_wait_a_{layer}")
        (leg_b[layer],), token = _exchange_start([("forward", [(unused, l) for l in lands])], [],
                                                 f"gather_forward_{layer}")
        return token

    def layer_weights(layer, h):
        j = layer // 2
        dep = h
        if layer == 0:
            w_in, dep = first[0], token_1
            w_out_now = lambda y: _exchange_wait(w_out_0, [y], "gather_wait_out_0")[0]
        else:
            w_in, w_out_got = _exchange_wait(leg_b[layer], [h], f"gather_wait_b_{layer}")
            w_out_now = lambda y: w_out_got
            if layer + 1 in shards:
                (leg_a[layer + 1],), dep = _exchange_start([("chip_gather", pairs_of(shards[layer + 1]))], [w_in],
                                                           f"gather_start_{layer + 1}")

        def w_out(y):
            return w_out_now(y), (second_leg(layer + 1, y) if layer + 1 in shards else y)

        w_in = jnp.transpose(w_in, (1, 0, 2)).reshape(D_MODEL, -1)
        row = lambda a: a[j][None]
        if layer % 2 == 0:
            return dict(dep=dep, norm=row(norm_even), w_in=w_in,
                        w_out=lambda y: (lambda wo, d: (wo.reshape(W_EVEN_MIX, D_MODEL), d))(*w_out(y)),
                        conv_w=jnp.pad(g_small["conv_a_w"][j], ((0, 1), (0, 0))), conv_b=row(conv_a_b),
                        ln_g=row(ln_a_g), ln_b=row(ln_a_b), pool_w=pool_w_all[j],
                        pool_b=g_small["pool_b"][j].reshape(1, W_POOL), pool_scale=row(pool_scale))
        return dict(dep=dep, norm=row(g_small["norm_odd"]), w_in=w_in,
                    w_out=lambda y: (lambda wo, d: (wo.reshape(W_LRU, D_MODEL), d))(*w_out(y)),
                    conv_w=jnp.pad(g_small["conv_c_w"][j], ((0, 4), (0, 0))), conv_b=row(g_small["conv_c_b"]),
                    w_rg=w_rg[j].astype(BF16), b_rg=row(g_small["b_rg"]), w_ig=w_ig[j].astype(BF16),
                    b_ig=row(g_small["b_ig"]), lam=row(g_small["lru_lambda"]))

    pending, exchanges, small_layout = {}, {}, {}
    last_token, pair_leg = [], []

    def on_grads(layer, grads, dep, last):
        par = layer % 2
        w_in_name, w_out_name = BIG[par]
        have = pending.setdefault(layer, {})
        have.update(grads)
        eager = layer == 0
        scatter, gather = {}, {}
        if w_out_name in have and (eager or last):
            scatter["out"] = have.pop(w_out_name).reshape(N_DEV, -1, D_MODEL)
        pair, chip = {}, {}
        if w_in_name in have and eager:
            pair["in"] = have.pop(w_in_name)
        elif w_in_name in have and last:
            scatter["in"] = have.pop(w_in_name)
        deps = []
        if pair:
            whole = pair["in"]
            deps = [lax.dynamic_index_in_dim(whole.reshape((4, 2) + whole.shape[1:]), lax.axis_index("c"), 1,
                                             keepdims=False)]
        if eager and last:
            mine, theirs = pair_leg.pop()
            chip["in"] = _pair_sum(mine, _exchange_wait(theirs, list(grads.values()), "grads_wait_pair")[0])
        if layer not in small_layout and all(n in have for n in SMALL[par]) and (eager or last):
            blocks = [_to_blocks(n, have[n]) for n in SMALL[par]]
            small_layout[layer] = [b.shape[1:] for b in blocks]
            scatter["small"] = _pack_rows(jnp.concatenate([b.reshape(N_DEV, -1) for b in blocks], axis=1))
        for n in REPLICATED[1]:
            if n in have:
                gather[n] = have.pop(n).astype(BF16).reshape(-1, LRU_HD)
        if last:
            vectors = [have[n] for n in REPLICATED[par]] if par == 0 else []
            if "final_norm" in have:
                vectors.append(have["final_norm"])
            if vectors:
                gather["rep32"] = jnp.concatenate([v.reshape(-1) for v in vectors]).reshape(-1, 128)
        my_chip = 2 * lax.axis_index("x") + lax.axis_index("y")
        groups, keys = [], []
        for mode, arrays, pairs in (
                ("scatter", scatter, [(s, landing(lax.dynamic_slice_in_dim(s, me, 1, 0))) for s in scatter.values()]),
                ("gather", gather, [(s, landing(s[None])) for s in gather.values()]),
                ("pair_scatter", pair, [(s, lax.empty((4,) + s.shape[1:], s.dtype)) for s in pair.values()]),
                ("chip_scatter", chip, [(s, lax.dynamic_update_slice(
                    lax.empty(s.shape, s.dtype), lax.dynamic_slice_in_dim(s, my_chip, 1, 0),
                    (my_chip,) + (0,) * (s.ndim - 1))) for s in chip.values()])):
            if arrays:
                groups.append((mode, pairs))
                keys.append(list(arrays))
        if not groups:
            return dep
        handles, token = _exchange_start(groups, deps, f"grads_start_{layer}_{'_'.join(k for ks in keys for k in ks)}")
        if pair:
            pair_leg.append((deps[0], handles.pop()))
            keys.pop()
        exchanges.setdefault(layer, []).extend(zip(keys, handles))
        last_token[:] = [token]
        return token

    loss, grad_x = _local_step(x[0], loss_target[0], layer_weights, final_norm[None], on_grads)

    w3 = {n: _as3d(w_loc[n]) for n in NAMES}
    m3 = {n: _as3d(m_loc[n]) for n in NAMES}
    v3 = {n: _as3d(v_loc[n]) for n in NAMES}
    results = {}
    after = list(last_token)
    for layer in (3, 2, 1, 0):
        par, j = layer % 2, layer // 2
        got = {}
        for keys, handle in exchanges[layer]:
            got.update(zip(keys, _exchange_wait(handle, after, f"grads_wait_{layer}_{'_'.join(keys)}")))
        parts = {BIG[par][0]: got["in"], BIG[par][1]: got["out"]}
        parts.update(zip(SMALL[par], _unpack(got["small"].reshape(N_DEV, -1), small_layout[layer])))
        if par == 1:
            parts.update({n: got[n] for n in REPLICATED[1]})
        else:
            parts.update(zip(REPLICATED[0], _unpack(got["rep32"].reshape(N_DEV, -1)[:, :len(REPLICATED[0]) * D_MODEL],
                                                    [w_loc[n].shape[1:] for n in REPLICATED[0]])))
        big_items, small_items = [], []
        for n, pt in parts.items():
            item = (w3[n], pt.reshape((pt.shape[0], 1) + w3[n].shape[1:]), m3[n], v3[n])
            (big_items if w3[n].shape[1] >= 128 else small_items).append((n, item))
        for n, item in big_items:
            results[n] = _adamw([item], j, [results[n]] if n in results else None, f"adamw_{n}_{layer}")[0]
        snames = [n for n, _ in small_items]
        prev = [results[n] for n in snames] if snames[0] in results else None
        for n, r in zip(snames, _adamw([it for _, it in small_items], j, prev, f"adamw_small_{layer}")):
            results[n] = r
        if layer == N_LAYERS - 1:
            item = (w3["final_norm"], got["rep32"].reshape(N_DEV, 1, 1, -1)[..., :D_MODEL], m3["final_norm"],
                    v3["final_norm"])
            results["final_norm"] = _adamw([item], 0, None, "adamw_final_norm")[0]
        after = [results[BIG[par][0]][1]]

    total = lax.psum(loss[0, 0], ("x", "y", "c"))
    outs = [[results[n][k].reshape(w_loc[n].shape) for n in NAMES] for k in range(4)]
    return (total, grad_x[None], *outs[0], *outs[1], *outs[2], *outs[3])
```

```python
import functools

import jax
import jax.numpy as jnp
from jax import lax
from jax.experimental import pallas as pl
from jax.experimental.pallas import tpu as pltpu

F32 = jnp.float32
BF16 = jnp.bfloat16

N_DEV = 8
N_PEERS = N_DEV - 1
N_LAYERS = 4
D_MODEL = 1024
EPS_RMS = 1e-6
EPS_LN = 1e-5
W_CONV = 1024
CONV_K = 31
W_POOL = 1024
POOL_WINDOWS = (2, 4, 8, 16)
POOL_GW = 256
W_EVEN_IN = 5120
W_EVEN_MIX = 2048
LRU_HEADS = 12
LRU_HD = 128
W_LRU = 1536
LRU_CONV_K = 4
LRU_C = 8.0
ADAM_LR = 0.001
ADAM_B1 = 0.9
ADAM_B2 = 0.999
ADAM_EPS = 1e-08
ADAM_WD = 0.01
ADAM_STEP = 10

HALO = 32
HALO_C = 8
TM_MATMUL = 512
TM_STREAM = 1024
TM_WGRAD = 2048
TM_MIXER = 256
CONV_ROWS = 128
MIX_ROWS = 32
VMEM_LIMIT = 56 * 1024 * 1024
MESH = pl.DeviceIdType.MESH
ANY = pl.BlockSpec(memory_space=pl.ANY)
HBM = pl.BlockSpec(memory_space=pltpu.HBM)
SEM = pl.BlockSpec(memory_space=pltpu.SEMAPHORE)


def _params(*sem):
    return pltpu.CompilerParams(dimension_semantics=sem, vmem_limit_bytes=VMEM_LIMIT)


def _sigmoid(z):
    return 0.5 * jnp.tanh(0.5 * z) + 0.5


def _dsilu(z, s):
    return s * (1.0 + z * (1.0 - s))


def _full(shape):
    nd = len(shape)
    return pl.BlockSpec(shape, lambda *_: (0,) * nd)


def _norm_matmul(h, g, w, dep, *, tm):
    t, d = h.shape
    n = w.shape[1]
    tn = n // 4

    def body(h_ref, g_ref, w_ref, dep_ref, p_ref, hn_ref):
        @pl.when(pl.program_id(1) == 0)
        def _():
            x = h_ref[...]
            r = lax.rsqrt(jnp.mean(x * x, axis=-1, keepdims=True) + EPS_RMS)
            hn_ref[...] = ((x * r) * g_ref[...]).astype(BF16)

        p_ref[...] = jnp.dot(hn_ref[...], w_ref[...], preferred_element_type=F32)

    return pl.pallas_call(
        body, name="norm_matmul", grid=(t // tm, n // tn),
        in_specs=[pl.BlockSpec((tm, d), lambda i, j: (i, 0)), _full((1, d)),
                  pl.BlockSpec((d, tn), lambda i, j: (0, j)), ANY],
        out_specs=[pl.BlockSpec((tm, tn), lambda i, j: (i, j)), pl.BlockSpec((tm, d), lambda i, j: (i, 0))],
        out_shape=[jax.ShapeDtypeStruct((t, n), F32), jax.ShapeDtypeStruct((t, d), BF16)],
        compiler_params=_params("arbitrary", "arbitrary"),
    )(h, g, w, dep)


def _out_proj(h, y, w, dep, *, tm):
    t, d = h.shape
    k = y.shape[1]

    def body(h_ref, y_ref, w_ref, dep_ref, o_ref):
        o_ref[...] = h_ref[...] + jnp.dot(y_ref[...], w_ref[...], preferred_element_type=F32)

    return pl.pallas_call(
        body, name="out_proj", grid=(t // tm,),
        in_specs=[pl.BlockSpec((tm, d), lambda i: (i, 0)), pl.BlockSpec((tm, k), lambda i: (i, 0)), _full((k, d)),
                  ANY],
        out_specs=pl.BlockSpec((tm, d), lambda i: (i, 0)),
        out_shape=jax.ShapeDtypeStruct((t, d), F32),
        compiler_params=_params("arbitrary"),
    )(h, y, w, dep)


def _out_proj_bwd(dh, y, w, dep, *, tm):
    t, d = dh.shape
    k = y.shape[1]
    nt = t // tm

    def body(dh_ref, y_ref, w_ref, dep_ref, dy_ref, dw_ref, acc):
        i = pl.program_id(0)
        g = dh_ref[...].astype(BF16)
        dy_ref[...] = lax.dot_general(g, w_ref[...], (((1,), (1,)), ((), ())), preferred_element_type=F32)
        part = lax.dot_general(y_ref[...], g, (((0,), (0,)), ((), ())), preferred_element_type=F32)

        @pl.when(i == 0)
        def _():
            acc[...] = part

        @pl.when(i > 0)
        def _():
            acc[...] += part

        @pl.when(i == nt - 1)
        def _():
            dw_ref[...] = acc[...].astype(BF16)

    return pl.pallas_call(
        body, name="out_proj_bwd", grid=(nt,),
        in_specs=[pl.BlockSpec((tm, d), lambda i: (i, 0)), pl.BlockSpec((tm, k), lambda i: (i, 0)), _full((k, d)),
                  ANY],
        out_specs=[pl.BlockSpec((tm, k), lambda i: (i, 0)), _full((k, d))],
        out_shape=[jax.ShapeDtypeStruct((t, k), F32), jax.ShapeDtypeStruct((k, d), BF16)],
        scratch_shapes=[pltpu.VMEM((k, d), F32)],
        compiler_params=_params("arbitrary"),
    )(dh, y, w, dep)


def _in_proj_bwd_x(dp, w, h, g, dh_out, dep, *, tm):
    t, d = h.shape
    n = w.shape[1]
    nt = t // tm

    def body(dp_ref, w_ref, h_ref, g_ref, dho_ref, dep_ref, dh_ref, dg_ref):
        i = pl.program_id(0)
        dy = lax.dot_general(dp_ref[...], w_ref[...], (((1,), (1,)), ((), ())), preferred_element_type=F32)
        x = h_ref[...]
        r = lax.rsqrt(jnp.mean(x * x, axis=-1, keepdims=True) + EPS_RMS)
        gd = dy * g_ref[...]
        m = jnp.mean(gd * x, axis=-1, keepdims=True)
        dh_ref[...] = dho_ref[...] + r * gd - x * (r * r * r * m)
        _acc_out(i, dg_ref, jnp.sum(dy * x * r, axis=0, keepdims=True))

    return pl.pallas_call(
        body, name="in_proj_bwd_x", grid=(nt,),
        in_specs=[pl.BlockSpec((tm, n), lambda i: (i, 0)), _full((d, n)),
                  pl.BlockSpec((tm, d), lambda i: (i, 0)), _full((1, d)), pl.BlockSpec((tm, d), lambda i: (i, 0)),
                  ANY],
        out_specs=[pl.BlockSpec((tm, d), lambda i: (i, 0)), _full((1, d))],
        out_shape=[jax.ShapeDtypeStruct((t, d), F32), jax.ShapeDtypeStruct((1, d), F32)],
        compiler_params=_params("arbitrary"),
    )(dp, w, h, g, dh_out, dep)


def _in_proj_bwd_w(hn, dp, nd, *, tm):
    t, d = hn.shape
    nb = dp.shape[1] // nd
    nt = t // tm

    def body(hn_ref, dp_ref, dw_ref, acc):
        i = pl.program_id(1)
        part = lax.dot_general(hn_ref[...], dp_ref[...], (((0,), (0,)), ((), ())), preferred_element_type=F32)

        @pl.when(i == 0)
        def _():
            acc[...] = part

        @pl.when(i > 0)
        def _():
            acc[...] += part

        @pl.when(i == nt - 1)
        def _():
            dw_ref[0] = acc[...].astype(BF16)

    return pl.pallas_call(
        body, name="in_proj_bwd_w", grid=(nd, nt),
        in_specs=[pl.BlockSpec((tm, d), lambda j, i: (i, 0)), pl.BlockSpec((tm, nb), lambda j, i: (i, j))],
        out_specs=pl.BlockSpec((1, d, nb), lambda j, i: (j, 0, 0)),
        out_shape=jax.ShapeDtypeStruct((nd, d, nb), BF16),
        scratch_shapes=[pltpu.VMEM((d, nb), F32)],
        compiler_params=_params("arbitrary", "arbitrary"),
    )(hn, dp)


def _loss_head(h, g, target, *, tm):
    t, d = h.shape
    nt = t // tm

    def body(h_ref, g_ref, t_ref, loss_ref, dh_ref, dg_ref):
        i = pl.program_id(0)
        x = h_ref[...]
        r = lax.rsqrt(jnp.mean(x * x, axis=-1, keepdims=True) + EPS_RMS)
        xr = x * r
        err = xr * g_ref[...] - t_ref[...]
        lp = 0.5 * jnp.sum(jnp.mean(err * err, axis=-1, keepdims=True), axis=0, keepdims=True)
        dy = err * (1.0 / d)
        gd = dy * g_ref[...]
        m = jnp.mean(gd * x, axis=-1, keepdims=True)
        dh_ref[...] = r * gd - x * (r * r * r * m)
        dgp = jnp.sum(dy * xr, axis=0, keepdims=True)

        @pl.when(i == 0)
        def _():
            loss_ref[...] = lp
            dg_ref[...] = dgp

        @pl.when(i > 0)
        def _():
            loss_ref[...] += lp
            dg_ref[...] += dgp

    return pl.pallas_call(
        body, name="loss_head", grid=(nt,),
        in_specs=[pl.BlockSpec((tm, d), lambda i: (i, 0)), _full((1, d)), pl.BlockSpec((tm, d), lambda i: (i, 0))],
        out_specs=[_full((1, 1)), pl.BlockSpec((tm, d), lambda i: (i, 0)), _full((1, d))],
        out_shape=[jax.ShapeDtypeStruct((1, 1), F32), jax.ShapeDtypeStruct((t, d), F32),
                   jax.ShapeDtypeStruct((1, d), F32)],
        compiler_params=_params("arbitrary"),
    )(h, g, target)


def _col(tm, w, c):
    return pl.BlockSpec((tm, w), lambda i: (i, c))


def _prev_halo(tm, rows, w, c):
    per = tm // rows
    return pl.BlockSpec((rows, w), lambda i: (jnp.maximum(i * per - 1, 0), c))


def _next_halo(tm, rows, w, c, t):
    per = tm // rows
    last = t // rows - 1
    return pl.BlockSpec((rows, w), lambda i: (jnp.minimum((i + 1) * per, last), c))


def _inv_count(first_row, rows, window):
    tpos = first_row + lax.broadcasted_iota(jnp.int32, (rows, 1), 0)
    return 1.0 / jnp.minimum(tpos + 1, window).astype(F32)


def _even_fwd(p, cw, cb, lg, lb, pw, pb, ps, *, tm):
    t = p.shape[0]
    wc = W_CONV

    def body(av, ag, agate, bv, bgate, avh, agh, bvh, cw_ref, cb_ref, lg_ref, lb_ref, pw_ref, pb_ref, ps_ref,
             y_ref, u1_ref, e_ref, d_ref, uext, vext):
        i = pl.program_id(0)
        keep = (i > 0).astype(F32)
        uext[0:HALO, :] = keep * (avh[...] * _sigmoid(agh[...]))
        uext[HALO:, :] = av[...] * _sigmoid(ag[...])
        vext[0:HALO, :] = keep * bvh[...]
        vext[HALO:, :] = bv[...]
        for c in range(0, wc, 128):
            for rb in range(0, tm, CONV_ROWS):
                acc = jnp.broadcast_to(cb_ref[:, c:c + 128], (CONV_ROWS, 128))
                for r in range(8):
                    shifted = uext[pl.ds(rb + 8 - r, CONV_ROWS + HALO - 8), c:c + 128]
                    for q in range(HALO // 8):
                        s = 8 * q + r
                        if s < CONV_K:
                            acc = acc + cw_ref[CONV_K - 1 - s:CONV_K - s, c:c + 128] * shifted[24 - 8 * q:24 - 8 * q + CONV_ROWS]
                u1_ref[rb:rb + CONV_ROWS, c:c + 128] = acc
        u1 = u1_ref[...]
        mu = jnp.mean(u1, axis=-1, keepdims=True)
        xc = u1 - mu
        rs = lax.rsqrt(jnp.mean(xc * xc, axis=-1, keepdims=True) + EPS_LN)
        u2 = (xc * rs) * lg_ref[...] + lb_ref[...]
        u3 = u2 * _sigmoid(u2)
        ga = agate[...]
        y_ref[:, 0:wc] = (u3 * (ga * _sigmoid(ga))).astype(BF16)
        for g, win in enumerate(POOL_WINDOWS):
            cs = slice(g * POOL_GW, (g + 1) * POOL_GW)
            s = vext[pl.ds(HALO, tm), cs]
            for j in range(1, win):
                s = s + vext[pl.ds(HALO - j, tm), cs]
            dg = s * _inv_count(i * tm, tm, win) - vext[pl.ds(HALO, tm), cs]
            dgb = dg.astype(BF16)
            d_ref[:, cs] = dgb
            eg = jnp.dot(dgb, pw_ref[g], preferred_element_type=F32) + pb_ref[:, cs]
            e_ref[:, cs] = eg
            gb = bgate[:, cs]
            y_ref[:, wc + g * POOL_GW:wc + (g + 1) * POOL_GW] = ((eg * ps_ref[:, cs]) * (gb * _sigmoid(gb))).astype(BF16)

    row = lambda w: pl.BlockSpec((tm, w), lambda i: (i, 0))
    return pl.pallas_call(
        body, name="even_fwd", grid=(t // tm,),
        in_specs=[_col(tm, wc, 0), _col(tm, wc, 1), _col(tm, wc, 2), _col(tm, wc, 3), _col(tm, wc, 4),
                  _prev_halo(tm, HALO, wc, 0), _prev_halo(tm, HALO, wc, 1), _prev_halo(tm, HALO, wc, 3),
                  _full((32, wc)), _full((1, wc)), _full((1, wc)), _full((1, wc)),
                  _full((4, POOL_GW, POOL_GW)), _full((1, wc)), _full((1, wc))],
        out_specs=[row(2 * wc), row(wc), row(wc), row(wc)],
        out_shape=[jax.ShapeDtypeStruct((t, 2 * wc), BF16), jax.ShapeDtypeStruct((t, wc), F32),
                   jax.ShapeDtypeStruct((t, wc), F32), jax.ShapeDtypeStruct((t, wc), BF16)],
        scratch_shapes=[pltpu.VMEM((tm + HALO, wc), F32), pltpu.VMEM((tm + HALO, wc), F32)],
        compiler_params=_params("arbitrary"),
    )(p, p, p, p, p, p, p, p, cw, cb, lg, lb, pw, pb, ps)


def _acc_out(i, ref, val):
    @pl.when(i == 0)
    def _():
        ref[...] = val

    @pl.when(i > 0)
    def _():
        ref[...] += val


def _even_bwd_a(p, u1, e, dmat, dy, lg, lb, pw, ps, dep, *, tm):
    t = p.shape[0]
    wc = W_CONV

    def body(agate, bgate, u1_ref, e_ref, d_ref, dya, dyb, lg_ref, lb_ref, pw_ref, ps_ref, dep_ref,
             du1_ref, dd_ref, dgat_ref, dlg_ref, dlb_ref, dpb_ref, dps_ref, dpw_ref, de_s):
        i = pl.program_id(0)

        @pl.when(i == 0)
        def _():
            for ref in (dpw_ref, dlg_ref, dlb_ref, dpb_ref, dps_ref):
                ref[...] = jnp.zeros_like(ref)

        def rows(k, carry):
            rs_ = pl.ds(pl.multiple_of(k * MIX_ROWS, MIX_ROWS), MIX_ROWS)
            u1 = u1_ref[rs_, :]
            mu = jnp.mean(u1, axis=-1, keepdims=True)
            xc = u1 - mu
            rs = lax.rsqrt(jnp.mean(xc * xc, axis=-1, keepdims=True) + EPS_LN)
            xh = xc * rs
            u2 = xh * lg_ref[...] + lb_ref[...]
            s2 = _sigmoid(u2)
            ga = agate[rs_, :]
            sa = _sigmoid(ga)
            dy_a = dya[rs_, :]
            dgat_ref[rs_, 0:wc] = (dy_a * (u2 * s2) * _dsilu(ga, sa)).astype(BF16)
            du2 = dy_a * (ga * sa) * _dsilu(u2, s2)
            dlg_ref[...] += jnp.sum(du2 * xh, axis=0, keepdims=True)
            dlb_ref[...] += jnp.sum(du2, axis=0, keepdims=True)
            dxh = du2 * lg_ref[...]
            m1 = jnp.mean(dxh, axis=-1, keepdims=True)
            m2 = jnp.mean(dxh * xh, axis=-1, keepdims=True)
            du1_ref[rs_, :] = rs * (dxh - m1 - xh * m2)

            gb = bgate[rs_, :]
            sb = _sigmoid(gb)
            ev = e_ref[rs_, :]
            dy_b = dyb[rs_, :]
            dgat_ref[rs_, wc:2 * wc] = (dy_b * (ev * ps_ref[...]) * _dsilu(gb, sb)).astype(BF16)
            dz = dy_b * (gb * sb)
            dps_ref[...] += jnp.sum(dz * ev, axis=0, keepdims=True)
            de = dz * ps_ref[...]
            dpb_ref[...] += jnp.sum(de, axis=0, keepdims=True)
            de_s[rs_, :] = de.astype(BF16)
            return carry

        lax.fori_loop(0, tm // MIX_ROWS, rows, 0)
        for g in range(len(POOL_WINDOWS)):
            cs = slice(g * POOL_GW, (g + 1) * POOL_GW)
            deg = de_s[:, cs]
            dd_ref[:, cs] = lax.dot_general(deg, pw_ref[g], (((1,), (1,)), ((), ())), preferred_element_type=F32)
            dpw_ref[g] += lax.dot_general(d_ref[:, cs], deg, (((0,), (0,)), ((), ())), preferred_element_type=F32)

    row = lambda w: pl.BlockSpec((tm, w), lambda i: (i, 0))
    return pl.pallas_call(
        body, name="even_bwd_a", grid=(t // tm,),
        in_specs=[_col(tm, wc, 2), _col(tm, wc, 4), row(wc), row(wc), row(wc), _col(tm, wc, 0), _col(tm, wc, 1),
                  _full((1, wc)), _full((1, wc)), _full((4, POOL_GW, POOL_GW)), _full((1, wc)), ANY],
        out_specs=[row(wc), row(wc), row(2 * wc), _full((1, wc)), _full((1, wc)), _full((1, wc)), _full((1, wc)),
                   _full((4, POOL_GW, POOL_GW))],
        out_shape=[jax.ShapeDtypeStruct((t, wc), F32), jax.ShapeDtypeStruct((t, wc), F32),
                   jax.ShapeDtypeStruct((t, 2 * wc), BF16)] + [jax.ShapeDtypeStruct((1, wc), F32)] * 4
                  + [jax.ShapeDtypeStruct((4, POOL_GW, POOL_GW), F32)],
        scratch_shapes=[pltpu.VMEM((tm, wc), BF16)],
        compiler_params=_params("arbitrary"),
    )(p, p, u1, e, dmat, dy, dy, lg, lb, pw, ps, dep)


def _even_bwd_b(p, du1, dd, dgat, cw, *, tm):
    t = p.shape[0]
    wc = W_CONV
    nt = t // tm

    def body(av, ag, avh, agh, du1_ref, du1n, dd_ref, ddn, dgat_ref, cw_ref, dp_ref, dcw_ref, dcb_ref,
             uext, gext, dext, du0, dcw8):
        i = pl.program_id(0)
        keep_p = (i > 0).astype(F32)
        keep_n = (i < nt - 1).astype(F32)

        @pl.when(i == 0)
        def _():
            dcw8[...] = jnp.zeros_like(dcw8)

        a = av[...]
        sg = _sigmoid(ag[...])
        uext[0:HALO, :] = keep_p * (avh[...] * _sigmoid(agh[...]))
        uext[HALO:, :] = a * sg
        gext[0:tm, :] = du1_ref[...]
        gext[tm:, :] = keep_n * du1n[...]
        for c in range(0, wc, 128):
            for rb in range(0, tm, CONV_ROWS):
                acc = jnp.zeros((CONV_ROWS, 128), F32)
                for r in range(8):
                    ahead = gext[pl.ds(rb + r, CONV_ROWS + HALO - 8), c:c + 128]
                    for q in range(HALO // 8):
                        s = 8 * q + r
                        if s < CONV_K:
                            acc = acc + cw_ref[CONV_K - 1 - s:CONV_K - s, c:c + 128] * ahead[8 * q:8 * q + CONV_ROWS]
                du0[rb:rb + CONV_ROWS, c:c + 128] = acc
                gcur = du1_ref[rb:rb + CONV_ROWS, c:c + 128]
                for r in range(8):
                    behind = uext[pl.ds(rb + 8 - r, CONV_ROWS + HALO - 8), c:c + 128]
                    for q in range(HALO // 8):
                        s = 8 * q + r
                        if s < CONV_K:
                            prod = gcur * behind[24 - 8 * q:24 - 8 * q + CONV_ROWS]
                            part = prod[0:8]
                            for o in range(8, CONV_ROWS, 8):
                                part = part + prod[o:o + 8]
                            k = CONV_K - 1 - s
                            dcw8[8 * k:8 * k + 8, c:c + 128] += part

        @pl.when(i == nt - 1)
        def _():
            for k in range(CONV_K):
                dcw_ref[k:k + 1, :] = jnp.sum(dcw8[8 * k:8 * k + 8, :], axis=0, keepdims=True)
            dcw_ref[CONV_K:32, :] = jnp.zeros((32 - CONV_K, wc), F32)

        _acc_out(i, dcb_ref, jnp.sum(du1_ref[...], axis=0, keepdims=True))
        g0 = du0[...]
        dp_ref[:, 0:wc] = (g0 * sg).astype(BF16)
        dp_ref[:, wc:2 * wc] = (g0 * a * sg * (1.0 - sg)).astype(BF16)
        dp_ref[:, 2 * wc:3 * wc] = dgat_ref[:, 0:wc]
        dp_ref[:, 4 * wc:5 * wc] = dgat_ref[:, wc:2 * wc]
        for g, win in enumerate(POOL_WINDOWS):
            cs = slice(g * POOL_GW, (g + 1) * POOL_GW)
            dext[0:tm, cs] = dd_ref[:, cs] * _inv_count(i * tm, tm, win)
            dext[tm:, cs] = keep_n * (ddn[:, cs] * _inv_count((i + 1) * tm, HALO, win))
            s = dext[pl.ds(0, tm), cs]
            for j in range(1, win):
                s = s + dext[pl.ds(j, tm), cs]
            dp_ref[:, 3 * wc + g * POOL_GW:3 * wc + (g + 1) * POOL_GW] = (s - dd_ref[:, cs]).astype(BF16)

    row = lambda w: pl.BlockSpec((tm, w), lambda i: (i, 0))
    return pl.pallas_call(
        body, name="even_bwd_b", grid=(nt,),
        in_specs=[_col(tm, wc, 0), _col(tm, wc, 1), _prev_halo(tm, HALO, wc, 0), _prev_halo(tm, HALO, wc, 1),
                  row(wc), _next_halo(tm, HALO, wc, 0, t), row(wc), _next_halo(tm, HALO, wc, 0, t), row(2 * wc),
                  _full((32, wc))],
        out_specs=[row(5 * wc), _full((32, wc)), _full((1, wc))],
        out_shape=[jax.ShapeDtypeStruct((t, 5 * wc), BF16), jax.ShapeDtypeStruct((32, wc), F32),
                   jax.ShapeDtypeStruct((1, wc), F32)],
        scratch_shapes=[pltpu.VMEM((tm + HALO, wc), F32), pltpu.VMEM((tm + HALO, wc), F32),
                        pltpu.VMEM((tm + HALO, wc), F32), pltpu.VMEM((tm, wc), F32), pltpu.VMEM((8 * 32, wc), F32)],
        compiler_params=_params("arbitrary"),
    )(p, p, p, p, du1, du1, dd, dd, dgat, cw)


def _softplus_neg(lam):
    z = -lam
    return jnp.maximum(z, 0.0) + jnp.log1p(jnp.exp(-jnp.abs(z)))


def _one_minus_exp(x):
    series = -x * (1.0 + x * (0.5 + x * (1.0 / 6.0 + x * (1.0 / 24.0))))
    return jnp.where(x > -0.02, series, 1.0 - jnp.exp(x))


def _odd_fwd(p, ccw, ccb, wrg, brg, wig, big, lam, *, tm):
    t = p.shape[0]
    wl = W_LRU
    ng = tm // 8

    def body(xr, gate, xrh, ccw_ref, ccb_ref, wrg_ref, brg_ref, wig_ref, big_ref, lam_ref,
             y_ref, xc_ref, r_ref, i_ref, hs_ref, xext, a_s, b_s, carry):
        i = pl.program_id(0)
        keep = (i > 0).astype(F32)
        xext[0:HALO_C, :] = keep * xrh[...]
        xext[HALO_C:, :] = xr[...]
        xc = jnp.broadcast_to(ccb_ref[...], (tm, wl))
        for k in range(LRU_CONV_K):
            xc = xc + ccw_ref[k:k + 1, :] * xext[pl.ds(HALO_C - (LRU_CONV_K - 1) + k, tm), :]
        xc_ref[...] = xc
        for h in range(LRU_HEADS):
            cs = slice(h * LRU_HD, (h + 1) * LRU_HD)
            xh = xc_ref[:, cs].astype(BF16)
            r_ref[:, cs] = _sigmoid(jnp.dot(xh, wrg_ref[h], preferred_element_type=F32) + brg_ref[:, cs])
            i_ref[:, cs] = _sigmoid(jnp.dot(xh, wig_ref[h], preferred_element_type=F32) + big_ref[:, cs])
        log_a = (-LRU_C * _softplus_neg(lam_ref[...])) * r_ref[...]
        a_s[...] = jnp.exp(log_a)
        b_s[...] = jnp.sqrt(_one_minus_exp(2.0 * log_a)) * (i_ref[...] * xc_ref[...])

        @pl.when(i == 0)
        def _():
            carry[...] = jnp.zeros_like(carry)

        rowi = lax.broadcasted_iota(jnp.int32, (8, wl), 0)

        def step(g, c):
            sl = pl.ds(pl.multiple_of(g * 8, 8), 8)
            aa, bb = a_s[sl, :], b_s[sl, :]
            for s in (1, 2, 4):
                m = rowi >= s
                a_sh = jnp.where(m, pltpu.roll(aa, s, 0), 1.0)
                b_sh = jnp.where(m, pltpu.roll(bb, s, 0), 0.0)
                bb = aa * b_sh + bb
                aa = aa * a_sh
            hv = bb + aa * c
            hs_ref[sl, :] = hv
            return hv[7:8, :]

        carry[...] = lax.fori_loop(0, ng, step, carry[...])
        gt = gate[...]
        y_ref[...] = (hs_ref[...] * (gt * _sigmoid(gt))).astype(BF16)

    row = lambda w: pl.BlockSpec((tm, w), lambda i: (i, 0))
    return pl.pallas_call(
        body, name="odd_fwd", grid=(t // tm,),
        in_specs=[_col(tm, wl, 0), _col(tm, wl, 1), _prev_halo(tm, HALO_C, wl, 0), _full((8, wl)), _full((1, wl)),
                  _full((LRU_HEADS, LRU_HD, LRU_HD)), _full((1, wl)), _full((LRU_HEADS, LRU_HD, LRU_HD)),
                  _full((1, wl)), _full((1, wl))],
        out_specs=[row(wl)] * 5,
        out_shape=[jax.ShapeDtypeStruct((t, wl), BF16)] + [jax.ShapeDtypeStruct((t, wl), F32)] * 4,
        scratch_shapes=[pltpu.VMEM((tm + HALO_C, wl), F32), pltpu.VMEM((tm, wl), F32), pltpu.VMEM((tm, wl), F32),
                        pltpu.VMEM((1, wl), F32)],
        compiler_params=_params("arbitrary"),
    )(p, p, p, ccw, ccb, wrg, brg, wig, big, lam)


def _odd_bwd_a(p, xc, r, ig, hs, dy, wrg, wig, lam, dep, *, tm):
    t = p.shape[0]
    wl = W_LRU
    nt = t // tm
    ng = tm // 8
    per = tm // HALO_C

    def body(gate, xc_ref, r_ref, i_ref, hs_ref, hsh, dy_ref, wrg_ref, wig_ref, lam_ref, dep_ref,
             dxc_ref, dgate_ref, dwrg_ref, dwig_ref, dbrg_ref, dbig_ref, dlam_ref,
             hext, a_s, q_s, g_s, dpr_s, dpi_s, carry):
        i = pl.program_id(0)
        ti = nt - 1 - i
        keep = (ti > 0).astype(F32)
        hext[0:HALO_C, :] = keep * hsh[...]
        hext[HALO_C:, :] = hs_ref[...]
        gt = gate[...]
        sg = _sigmoid(gt)
        dyv = dy_ref[...]
        dgate_ref[...] = (dyv * hs_ref[...] * _dsilu(gt, sg)).astype(BF16)
        q_s[...] = dyv * (gt * sg)
        sp = _softplus_neg(lam_ref[...])
        log_a = (-LRU_C * sp) * r_ref[...]
        a_s[...] = jnp.exp(log_a)

        @pl.when(i == 0)
        def _():
            carry[...] = jnp.zeros_like(carry)
            dwrg_ref[...] = jnp.zeros_like(dwrg_ref)
            dwig_ref[...] = jnp.zeros_like(dwig_ref)

        rowi = lax.broadcasted_iota(jnp.int32, (8, wl), 0)

        def step(gr, c):
            sl = pl.ds(pl.multiple_of((ng - 1 - gr) * 8, 8), 8)
            a0 = a_s[sl, :]
            al = jnp.where(rowi < 7, pltpu.roll(a0, 7, 0), 1.0)
            be = q_s[sl, :]
            for s in (1, 2, 4):
                m = rowi + s <= 7
                al_sh = jnp.where(m, pltpu.roll(al, 8 - s, 0), 1.0)
                be_sh = jnp.where(m, pltpu.roll(be, 8 - s, 0), 0.0)
                be = be + al * be_sh
                al = al * al_sh
            gv = be + al * c
            g_s[sl, :] = gv
            return (a0 * gv)[0:1, :]

        carry[...] = lax.fori_loop(0, ng, step, carry[...])

        gv = g_s[...]
        a = a_s[...]
        mult = jnp.sqrt(_one_minus_exp(2.0 * log_a))
        iv = i_ref[...]
        rv = r_ref[...]
        xcv = xc_ref[...]
        hprev = hext[pl.ds(HALO_C - 1, tm), :]
        dla = gv * hprev * a - (gv * iv * xcv) * (a * a) / mult
        di = gv * mult * xcv
        dpr = (dla * (-LRU_C * sp)) * rv * (1.0 - rv)
        dpi = di * iv * (1.0 - iv)
        dpr_s[...] = dpr
        dpi_s[...] = dpi
        dxc_ref[...] = gv * mult * iv
        dsp = jnp.sum(dla * rv, axis=0, keepdims=True) * (-LRU_C)
        _acc_out(i, dlam_ref, -dsp * jax.nn.sigmoid(-lam_ref[...]))
        _acc_out(i, dbrg_ref, jnp.sum(dpr, axis=0, keepdims=True))
        _acc_out(i, dbig_ref, jnp.sum(dpi, axis=0, keepdims=True))
        for h in range(LRU_HEADS):
            cs = slice(h * LRU_HD, (h + 1) * LRU_HD)
            xh = xc_ref[:, cs].astype(BF16)
            dr_h = dpr_s[:, cs].astype(BF16)
            di_h = dpi_s[:, cs].astype(BF16)
            dxc_ref[:, cs] += (
                lax.dot_general(dr_h, wrg_ref[h], (((1,), (1,)), ((), ())), preferred_element_type=F32)
                + lax.dot_general(di_h, wig_ref[h], (((1,), (1,)), ((), ())), preferred_element_type=F32))
            dwrg_ref[h] += lax.dot_general(xh, dr_h, (((0,), (0,)), ((), ())), preferred_element_type=F32)
            dwig_ref[h] += lax.dot_general(xh, di_h, (((0,), (0,)), ((), ())), preferred_element_type=F32)

    rrow = lambda w: pl.BlockSpec((tm, w), lambda i: (nt - 1 - i, 0))
    hspec = pl.BlockSpec((HALO_C, wl), lambda i: (jnp.maximum((nt - 1 - i) * per - 1, 0), 0))
    wspec = _full((LRU_HEADS, LRU_HD, LRU_HD))
    return pl.pallas_call(
        body, name="odd_bwd_a", grid=(nt,),
        in_specs=[pl.BlockSpec((tm, wl), lambda i: (nt - 1 - i, 1)), rrow(wl), rrow(wl), rrow(wl), rrow(wl), hspec,
                  rrow(wl), wspec, wspec, _full((1, wl)), ANY],
        out_specs=[rrow(wl), rrow(wl), wspec, wspec, _full((1, wl)), _full((1, wl)), _full((1, wl))],
        out_shape=[jax.ShapeDtypeStruct((t, wl), F32), jax.ShapeDtypeStruct((t, wl), BF16),
                   jax.ShapeDtypeStruct((LRU_HEADS, LRU_HD, LRU_HD), F32),
                   jax.ShapeDtypeStruct((LRU_HEADS, LRU_HD, LRU_HD), F32)] + [jax.ShapeDtypeStruct((1, wl), F32)] * 3,
        scratch_shapes=[pltpu.VMEM((tm + HALO_C, wl), F32)] + [pltpu.VMEM((tm, wl), F32)] * 5
                       + [pltpu.VMEM((1, wl), F32)],
        compiler_params=_params("arbitrary"),
    )(p, xc, r, ig, hs, hs, dy, wrg, wig, lam, dep)


def _odd_bwd_b(p, dxc, dgate, ccw, *, tm):
    t = p.shape[0]
    wl = W_LRU
    nt = t // tm

    def body(xr, xrh, dxc_ref, dxcn, dgate_ref, ccw_ref, dp_ref, dcw_ref, dcb_ref, xext, gext):
        i = pl.program_id(0)

        @pl.when(i == 0)
        def _():
            dcw_ref[...] = jnp.zeros_like(dcw_ref)

        xext[0:HALO_C, :] = (i > 0).astype(F32) * xrh[...]
        xext[HALO_C:, :] = xr[...]
        gext[0:tm, :] = dxc_ref[...]
        gext[tm:, :] = (i < nt - 1).astype(F32) * dxcn[...]
        g = dxc_ref[...]
        acc = jnp.zeros((tm, wl), F32)
        for k in range(LRU_CONV_K):
            acc = acc + ccw_ref[k:k + 1, :] * gext[pl.ds(LRU_CONV_K - 1 - k, tm), :]
            dcw_ref[k:k + 1, :] += jnp.sum(
                g * xext[pl.ds(HALO_C - (LRU_CONV_K - 1) + k, tm), :], axis=0, keepdims=True)

        _acc_out(i, dcb_ref, jnp.sum(g, axis=0, keepdims=True))
        dp_ref[:, 0:wl] = acc.astype(BF16)
        dp_ref[:, wl:2 * wl] = dgate_ref[...]

    row = lambda w: pl.BlockSpec((tm, w), lambda i: (i, 0))
    return pl.pallas_call(
        body, name="odd_bwd_b", grid=(nt,),
        in_specs=[_col(tm, wl, 0), _prev_halo(tm, HALO_C, wl, 0), row(wl), _next_halo(tm, HALO_C, wl, 0, t), row(wl),
                  _full((8, wl))],
        out_specs=[row(2 * wl), _full((8, wl)), _full((1, wl))],
        out_shape=[jax.ShapeDtypeStruct((t, 2 * wl), BF16), jax.ShapeDtypeStruct((8, wl), F32),
                   jax.ShapeDtypeStruct((1, wl), F32)],
        scratch_shapes=[pltpu.VMEM((tm + HALO_C, wl), F32), pltpu.VMEM((tm + HALO_C, wl), F32)],
        compiler_params=_params("arbitrary"),
    )(p, p, dxc, dxc, dgate, ccw)


def _local_step(x, target, layer_weights, final_norm, on_grads):
    t = x.shape[0]
    tm, tx, tl, tw = min(TM_MATMUL, t), min(TM_MIXER, t), min(TM_STREAM, t), min(TM_WGRAD, t)
    h = x
    saved = []
    for layer in range(N_LAYERS):
        w = layer_weights(layer, h)
        p, hn = _norm_matmul(h, w["norm"], w["w_in"], w["dep"], tm=tl)
        if layer % 2 == 0:
            y, *acts = _even_fwd(p, w["conv_w"], w["conv_b"], w["ln_g"], w["ln_b"], w["pool_w"], w["pool_b"],
                                 w["pool_scale"], tm=tx)
        else:
            y, *acts = _odd_fwd(p, w["conv_w"], w["conv_b"], w["w_rg"], w["b_rg"], w["w_ig"], w["b_ig"], w["lam"],
                                tm=tx)
        w_out, dep = w["w_out"](y)
        saved.append((w, w_out, h, p, hn, y, acts))
        h = _out_proj(h, y, w_out, dep, tm=tm)
    loss, dh, d_final = _loss_head(h, final_norm, target, tm=tm)

    dep = d_final
    for layer in reversed(range(N_LAYERS)):
        w, w_out, h_in, p, hn, y, acts = saved[layer]
        sfx = "even" if layer % 2 == 0 else "odd"
        dy, dw_out = _out_proj_bwd(dh, y, w_out, dep, tm=tm)
        dep = on_grads(layer, {"w_out_" + sfx: dw_out}, dep, False)
        if layer % 2 == 0:
            u1, e, dmat = acts
            du1, dd, dgat, dlg, dlb, dpb, dps, dpw = _even_bwd_a(p, u1, e, dmat, dy, w["ln_g"], w["ln_b"], w["pool_w"],
                                                                 w["pool_scale"], dep, tm=tx)
            dp, dcw, dcb = _even_bwd_b(p, du1, dd, dgat, w["conv_w"], tm=tx)
            grads = dict(conv_a_w=dcw[:CONV_K], conv_a_b=dcb, ln_a_g=dlg, ln_a_b=dlb, pool_w=dpw, pool_b=dpb,
                         pool_scale=dps)
        else:
            xc, r, ig, hs = acts
            dxc, dgate, dwrg, dwig, dbrg, dbig, dlam = _odd_bwd_a(p, xc, r, ig, hs, dy, w["w_rg"], w["w_ig"],
                                                                  w["lam"], dep, tm=tx)
            dp, dccw, dccb = _odd_bwd_b(p, dxc, dgate, w["conv_w"], tm=tx)
            grads = dict(conv_c_w=dccw[:LRU_CONV_K], conv_c_b=dccb, w_rg=dwrg, b_rg=dbrg, w_ig=dwig, b_ig=dbig,
                         lru_lambda=dlam)
        grads["w_in_" + sfx] = _in_proj_bwd_w(hn, dp, N_DEV, tm=tw)
        dep = on_grads(layer, grads, dep, False)
        dh, dg = _in_proj_bwd_x(dp, w["w_in"], h_in, w["norm"], dh, dep, tm=tm)
        rest = {"norm_" + sfx: dg}
        if layer == N_LAYERS - 1:
            rest["final_norm"] = d_final
        dep = on_grads(layer, rest, dep, True)
    return loss, dh


def _slot(px, py, pc):
    return 4 * px + 2 * py + pc


def _peers(x, y, c):
    return [(1 - x if k & 4 else x, 1 - y if k & 2 else y, 1 - c if k & 1 else c) for k in range(1, N_DEV)]


def _all_gather(arrs, name):
    n = len(arrs)

    def body(*refs):
        ins, outs = refs[:n], refs[n:2 * n]
        send_sems, recv_sems, local_sems = refs[2 * n:]
        x, y, c = lax.axis_index("x"), lax.axis_index("y"), lax.axis_index("c")
        me, sibling = (x, y, c), (x, y, 1 - c)
        chips = [(1 - x, y), (x, 1 - y), (1 - x, 1 - y)]

        def copy(a, k, block, to, src=None):
            rows = outs[a].at[_slot(*block)]
            return pltpu.make_async_remote_copy(
                src_ref=rows if src is None else src, dst_ref=rows, send_sem=send_sems.at[a, k],
                recv_sem=recv_sems.at[a, k], device_id=to, device_id_type=MESH)

        mine = [pltpu.make_async_copy(ins[a], outs[a].at[_slot(*me)], local_sems.at[a]) for a in range(n)]
        for cp in mine:
            cp.start()
        first = []
        for a in range(n):
            first.append(copy(a, 0, me, sibling, src=ins[a]))
            first += [copy(a, 1 + j, me, (*chip, c), src=ins[a]) for j, chip in enumerate(chips)]
        for cp in first:
            cp.start()
        passed = []
        for j, chip in enumerate(chips):
            for a in range(n):
                copy(a, 1 + j, (*chip, c), me).wait_recv()
                fwd = copy(a, 4 + j, (*chip, c), sibling)
                fwd.start()
                passed.append(fwd)
        for a in range(n):
            copy(a, 0, sibling, me).wait_recv()
            for j, chip in enumerate(chips):
                copy(a, 4 + j, (*chip, 1 - c), me).wait_recv()
        for cp in first + passed:
            cp.wait_send()
        for cp in mine:
            cp.wait()

    return pl.pallas_call(
        body, name=name,
        in_specs=[ANY] * n, out_specs=[ANY] * n,
        out_shape=[jax.ShapeDtypeStruct((N_DEV,) + a.shape, a.dtype) for a in arrs],
        scratch_shapes=[pltpu.SemaphoreType.DMA((n, 7)), pltpu.SemaphoreType.DMA((n, 7)),
                        pltpu.SemaphoreType.DMA((n,))],
    )(*arrs)


N_COPIES = {"gather": N_PEERS, "scatter": N_PEERS, "chip_gather": 4, "forward": 3, "pair_scatter": 4,
            "chip_scatter": 3}


def _exchange_plan(mode, x, y, c):
    me = _slot(x, y, c)
    chips = [(1 - x, y), (x, 1 - y), (1 - x, 1 - y)]
    if mode == "forward":
        return [((x, y, 1 - c), ("land", _slot(*chip, c)), _slot(*chip, c), _slot(*chip, 1 - c)) for chip in chips]
    if mode == "pair_scatter":
        return [((x, y, 1 - c), ("src", _slot(q // 2, q % 2, 1 - c)), q, q) for q in range(4)]
    if mode == "chip_scatter":
        return [((*chip, c), ("src", 2 * chip[0] + chip[1]), 2 * x + y, 2 * chip[0] + chip[1]) for chip in chips]
    peers = _peers(x, y, c)
    if mode == "chip_gather":
        peers = [(x, y, 1 - c), (1 - x, y, c), (x, 1 - y, c), (1 - x, 1 - y, c)]
    return [(p, ("src", _slot(*p)) if mode == "scatter" else ("src", None), me, _slot(*p)) for p in peers]


def _exchange_copy(src_ref, land_ref, plan, send_sem, recv_sem, start):
    peer, (which, block), there, here = plan
    src = land_ref if which == "land" else src_ref
    return pltpu.make_async_remote_copy(
        src_ref=src if block is None else src.at[block], dst_ref=land_ref.at[there if start else here],
        send_sem=send_sem, recv_sem=recv_sem, device_id=peer, device_id_type=MESH)


def _exchange_start(groups, deps, name):
    flat = [pair for _, g in groups for pair in g]
    n, ng = len(flat), len(groups)

    def body(*refs):
        src_refs, land_refs = refs[:n], refs[n:2 * n]
        outs = refs[2 * n + len(deps):]
        sems, token = outs[:2 * ng], outs[2 * ng + 2 * n]
        x, y, c = lax.axis_index("x"), lax.axis_index("y"), lax.axis_index("c")
        base = 0
        for gi, (mode, g) in enumerate(groups):
            nc = N_COPIES[mode]
            for k, plan in enumerate(_exchange_plan(mode, x, y, c)):
                for ai in range(len(g)):
                    _exchange_copy(src_refs[base + ai], land_refs[base + ai], plan, sems[2 * gi].at[ai * nc + k],
                                   sems[2 * gi + 1].at[ai * nc + k], True).start()
            base += len(g)
        token[...] = jnp.zeros_like(token)

    operands = [pltpu.with_memory_space_constraint(a, pltpu.HBM) for a in
                [s for s, _ in flat] + [l for _, l in flat]]
    out_shape = []
    for mode, g in groups:
        out_shape += [pltpu.SemaphoreType.DMA((len(g) * N_COPIES[mode],))] * 2
    out_shape += [pltpu.HBM(a.shape, a.dtype) for a in operands]
    out_shape.append(jax.ShapeDtypeStruct((8, 128), F32))
    outs = pl.pallas_call(
        body, name=name, out_shape=out_shape,
        in_specs=[HBM] * (2 * n) + [ANY] * len(deps),
        out_specs=[SEM] * (2 * ng) + [HBM] * (2 * n) + [pl.BlockSpec(memory_space=pltpu.VMEM)],
        input_output_aliases={i: 2 * ng + i for i in range(2 * n)},
        compiler_params=pltpu.CompilerParams(has_side_effects=pltpu.SideEffectType.DATAFLOW_SIDE_EFFECTING),
    )(*operands, *deps)
    handles, base = [], 0
    for gi, (mode, g) in enumerate(groups):
        srcs = outs[2 * ng + base:2 * ng + base + len(g)]
        lands = outs[2 * ng + n + base:2 * ng + n + base + len(g)]
        handles.append((mode, outs[2 * gi], outs[2 * gi + 1], list(srcs), list(lands)))
        base += len(g)
    return handles, outs[-1]


def _exchange_wait(handle, after, name):
    mode, send_sems, recv_sems, srcs, lands = handle
    n = len(srcs)
    nc = N_COPIES[mode]

    def body(*refs):
        src_refs, land_refs = refs[:n], refs[n:2 * n]
        send_ref, recv_ref = refs[2 * n], refs[2 * n + 1]
        x, y, c = lax.axis_index("x"), lax.axis_index("y"), lax.axis_index("c")
        for k, plan in enumerate(_exchange_plan(mode, x, y, c)):
            for a in range(n):
                cp = _exchange_copy(src_refs[a], land_refs[a], plan, send_ref.at[a * nc + k], recv_ref.at[a * nc + k],
                                    False)
                cp.wait_send()
                cp.wait_recv()

    outs = pl.pallas_call(
        body, name=name,
        out_shape=[pltpu.HBM(a.shape, a.dtype) for a in srcs + lands],
        in_specs=[HBM] * (2 * n) + [SEM, SEM] + [ANY] * len(after),
        out_specs=[HBM] * (2 * n),
        input_output_aliases={i: i for i in range(2 * n)},
        compiler_params=pltpu.CompilerParams(has_side_effects=pltpu.SideEffectType.DATAFLOW_SIDE_EFFECTING),
    )(*srcs, *lands, send_sems, recv_sems, *after)
    return list(outs[n:])


def _pair_sum(mine, theirs):
    nq, r, c = mine.shape
    tr = r

    def body(a_ref, b_ref, o_ref):
        o_ref[...] = (a_ref[...].astype(F32) + b_ref[...].astype(F32)).astype(BF16)

    blk = pl.BlockSpec((1, tr, c), lambda q, i: (q, i, 0))
    return pl.pallas_call(
        body, name="pair_sum", grid=(nq, r // tr), in_specs=[blk, blk], out_specs=blk,
        out_shape=jax.ShapeDtypeStruct(mine.shape, BF16),
        compiler_params=_params("arbitrary", "arbitrary"),
    )(mine, theirs)


def _adamw_math(w, g, m, v):
    c1 = 1.0 - ADAM_B1 ** ADAM_STEP
    c2 = 1.0 - ADAM_B2 ** ADAM_STEP
    nm = ADAM_B1 * m + (1.0 - ADAM_B1) * g
    nv = ADAM_B2 * v + (1.0 - ADAM_B2) * (g * g)
    delta = -ADAM_LR * ((nm / c1) / (jnp.sqrt(nv / c2) + ADAM_EPS) + ADAM_WD * w)
    return delta, nm, nv


def _row_tile(r):
    for cand in (256, 128, 64, 32, 16, 8):
        if r % cand == 0 and r > cand:
            return cand
    return r


def _adamw(items, layer0, bufs, name):
    ni = len(items)
    nl = items[0][1].shape[1]
    tiles = [_row_tile(w.shape[1]) for w, _, _, _ in items]
    steps = [w.shape[1] // tr for (w, _, _, _), tr in zip(items, tiles)]
    ns = steps[0]
    assert all(s == ns for s in steps)
    nb = 0 if bufs is None else 4 * ni

    def body(*refs):
        ins, outs = refs[:4 * ni], refs[4 * ni + nb:]
        for k in range(ni):
            w_ref, p_ref, m_ref, v_ref = ins[4 * k:4 * k + 4]
            g = p_ref[0, 0].astype(F32)
            for s in range(1, p_ref.shape[0]):
                g = g + p_ref[s, 0].astype(F32)
            delta, nm, nv = _adamw_math(w_ref[0], g, m_ref[0], v_ref[0])
            g_ref, d_ref, nm_ref, nv_ref = outs[4 * k:4 * k + 4]
            g_ref[0], d_ref[0], nm_ref[0], nv_ref[0] = g, delta, nm, nv

    in_specs, out_specs, out_shape, operands = [], [], [], []
    for (w, parts, m, v), tr in zip(items, tiles):
        blk = pl.BlockSpec((1, tr, w.shape[2]), lambda l, i: (layer0 + l, i, 0))
        in_specs += [blk, pl.BlockSpec((parts.shape[0], 1, tr, w.shape[2]), lambda l, i: (0, l, i, 0)), blk, blk]
        operands += [w, parts, m, v]
        out_specs += [blk] * 4
        out_shape += [jax.ShapeDtypeStruct(w.shape, F32)] * 4
    if bufs is not None:
        in_specs += [ANY] * nb
        operands += [b for item in bufs for b in item]
    outs = pl.pallas_call(
        body, name=name, grid=(nl, ns), in_specs=in_specs, out_specs=out_specs, out_shape=out_shape,
        input_output_aliases={4 * ni + i: i for i in range(nb)},
        compiler_params=_params("arbitrary", "arbitrary"),
    )(*operands)
    return [tuple(outs[4 * k:4 * k + 4]) for k in range(ni)]


NAMES = ("norm_even", "w_in_even", "conv_a_w", "conv_a_b", "ln_a_g", "ln_a_b", "pool_w", "pool_b", "pool_scale",
         "w_out_even", "norm_odd", "w_in_odd", "conv_c_w", "conv_c_b", "w_rg", "b_rg", "w_ig", "b_ig", "lru_lambda",
         "w_out_odd", "final_norm")
SMALL_GATHERED = ("conv_a_w", "pool_b", "norm_odd", "conv_c_w", "conv_c_b", "b_rg", "b_ig", "lru_lambda")
BIG = (("w_in_even", "w_out_even"), ("w_in_odd", "w_out_odd"))
SMALL = (("conv_a_w", "pool_b", "pool_w"), ("norm_odd", "conv_c_w", "conv_c_b", "b_rg", "b_ig", "lru_lambda"))
REPLICATED = (("norm_even", "conv_a_b", "ln_a_g", "ln_a_b", "pool_scale"), ("w_rg", "w_ig"))
PACK_ROW = 1024


def _pack_rows(flat2d):
    pad = (-flat2d.shape[1]) % PACK_ROW
    return jnp.pad(flat2d, ((0, 0), (0, pad))).reshape(flat2d.shape[0], -1, 128)


def _unpack(flat, shapes):
    out, off = [], 0
    for s in shapes:
        n = 1
        for d in s:
            n *= d
        out.append(flat[..., off:off + n].reshape(flat.shape[:-1] + tuple(s)))
        off += n
    return out


def _to_global(name, g):
    if name in ("conv_a_w", "pool_b", "conv_c_w"):
        return jnp.transpose(g, (1, 2, 0, 3)).reshape(g.shape[1], g.shape[2], -1)
    if name == "pool_w":
        return jnp.transpose(g, (1, 2, 0, 3, 4)).reshape(2, 4, POOL_GW, POOL_GW)
    return jnp.transpose(g, (1, 0, 2)).reshape(g.shape[1], -1)


def _to_blocks(name, g):
    if name == "conv_a_w":
        return jnp.transpose(g.reshape(CONV_K, N_DEV, -1), (1, 0, 2))
    if name == "conv_c_w":
        return jnp.transpose(g.reshape(LRU_CONV_K, N_DEV, -1), (1, 0, 2))
    if name == "pool_b":
        return jnp.transpose(g.reshape(4, N_DEV, -1), (1, 0, 2))
    if name == "pool_w":
        return jnp.transpose(g.reshape(4, N_DEV, POOL_GW // N_DEV, POOL_GW), (1, 0, 2, 3))
    return g.reshape(N_DEV, -1)


def _as3d(a):
    if a.ndim == 1:
        return a.reshape(1, 1, -1)
    if a.ndim == 2:
        return a.reshape(a.shape[0], 1, a.shape[1])
    return a.reshape(a.shape[0], -1, a.shape[-1])


def kernel(x, norm_even, w_in_even, conv_a_w, conv_a_b, ln_a_g, ln_a_b, pool_w, pool_b, pool_scale, w_out_even, norm_odd, w_in_odd, conv_c_w, conv_c_b, w_rg, b_rg, w_ig, b_ig, lru_lambda, w_out_odd, final_norm, loss_target, m_norm_even, m_w_in_even, m_conv_a_w, m_conv_a_b, m_ln_a_g, m_ln_a_b, m_pool_w, m_pool_b, m_pool_scale, m_w_out_even, m_norm_odd, m_w_in_odd, m_conv_c_w, m_conv_c_b, m_w_rg, m_b_rg, m_w_ig, m_b_ig, m_lru_lambda, m_w_out_odd, m_final_norm, v_norm_even, v_w_in_even, v_conv_a_w, v_conv_a_b, v_ln_a_g, v_ln_a_b, v_pool_w, v_pool_b, v_pool_scale, v_w_out_even, v_norm_odd, v_w_in_odd, v_conv_c_w, v_conv_c_b, v_w_rg, v_b_rg, v_w_ig, v_b_ig, v_lru_lambda, v_w_out_odd, v_final_norm):
    w_loc = dict(zip(NAMES, [norm_even, w_in_even, conv_a_w, conv_a_b, ln_a_g, ln_a_b, pool_w, pool_b, pool_scale,
                             w_out_even, norm_odd, w_in_odd, conv_c_w, conv_c_b, w_rg, b_rg, w_ig, b_ig, lru_lambda,
                             w_out_odd, final_norm]))
    m_loc = dict(zip(NAMES, [m_norm_even, m_w_in_even, m_conv_a_w, m_conv_a_b, m_ln_a_g, m_ln_a_b, m_pool_w, m_pool_b,
                             m_pool_scale, m_w_out_even, m_norm_odd, m_w_in_odd, m_conv_c_w, m_conv_c_b, m_w_rg,
                             m_b_rg, m_w_ig, m_b_ig, m_lru_lambda, m_w_out_odd, m_final_norm]))
    v_loc = dict(zip(NAMES, [v_norm_even, v_w_in_even, v_conv_a_w, v_conv_a_b, v_ln_a_g, v_ln_a_b, v_pool_w, v_pool_b,
                             v_pool_scale, v_w_out_even, v_norm_odd, v_w_in_odd, v_conv_c_w, v_conv_c_b, v_w_rg,
                             v_b_rg, v_w_ig, v_b_ig, v_lru_lambda, v_w_out_odd, v_final_norm]))
    me = _slot(lax.axis_index("x"), lax.axis_index("y"), lax.axis_index("c"))

    def landing(own):
        zone = lax.empty((N_DEV,) + own.shape[1:], own.dtype)
        return lax.dynamic_update_slice(zone, own, (me,) + (0,) * (own.ndim - 1))

    small_shapes = [w_loc[n].shape for n in SMALL_GATHERED]
    small = jnp.concatenate([w_loc[n].reshape(1, -1) for n in SMALL_GATHERED], axis=1)
    first = _all_gather([w_in_even[0].astype(BF16), pool_w.astype(BF16), _pack_rows(small)[0]], "gather_first")
    g_small = dict(zip(SMALL_GATHERED, [_to_global(n, g) for n, g in
                                        zip(SMALL_GATHERED, _unpack(first[2].reshape(N_DEV, -1), small_shapes))]))
    pool_w_all = _to_global("pool_w", first[1])

    def pairs_of(shards):
        return [(s.astype(BF16), landing(s.astype(BF16)[None])) for s in shards]

    shards = {1: [w_in_odd[0], w_out_odd[0]], 2: [w_in_even[1], w_out_even[1]], 3: [w_in_odd[1], w_out_odd[1]]}
    leg_a, leg_b = {}, {}
    (w_out_0, leg_a[1]), token_1 = _exchange_start(
        [("gather", pairs_of([w_out_even[0]])), ("chip_gather", pairs_of(shards[1]))], [first[0]], "gather_start_1")
    unused = jnp.zeros((8, 128), F32)

    def second_leg(layer, y):
        lands = _exchange_wait(leg_a[layer], [y], f"gather_wait_a_{layer}")
        (leg_b[layer],), token = _exchange_start([("forward", [(unused, l) for l in lands])], [],
                                                 f"gather_forward_{layer}")
        return token

    def layer_weights(layer, h):
        j = layer // 2
        dep = h
        if layer == 0:
            w_in, dep = first[0], token_1
            w_out_now = lambda y: _exchange_wait(w_out_0, [y], "gather_wait_out_0")[0]
        else:
            w_in, w_out_got = _exchange_wait(leg_b[layer], [h], f"gather_wait_b_{layer}")
            w_out_now = lambda y: w_out_got
            if layer + 1 in shards:
                (leg_a[layer + 1],), dep = _exchange_start([("chip_gather", pairs_of(shards[layer + 1]))], [w_in],
                                                           f"gather_start_{layer + 1}")

        def w_out(y):
            return w_out_now(y), (second_leg(layer + 1, y) if layer + 1 in shards else y)

        w_in = jnp.transpose(w_in, (1, 0, 2)).reshape(D_MODEL, -1)
        row = lambda a: a[j][None]
        if layer % 2 == 0:
            return dict(dep=dep, norm=row(norm_even), w_in=w_in,
                        w_out=lambda y: (lambda wo, d: (wo.reshape(W_EVEN_MIX, D_MODEL), d))(*w_out(y)),
                        conv_w=jnp.pad(g_small["conv_a_w"][j], ((0, 1), (0, 0))), conv_b=row(conv_a_b),
                        ln_g=row(ln_a_g), ln_b=row(ln_a_b), pool_w=pool_w_all[j],
                        pool_b=g_small["pool_b"][j].reshape(1, W_POOL), pool_scale=row(pool_scale))
        return dict(dep=dep, norm=row(g_small["norm_odd"]), w_in=w_in,
                    w_out=lambda y: (lambda wo, d: (wo.reshape(W_LRU, D_MODEL), d))(*w_out(y)),
                    conv_w=jnp.pad(g_small["conv_c_w"][j], ((0, 4), (0, 0))), conv_b=row(g_small["conv_c_b"]),
                    w_rg=w_rg[j].astype(BF16), b_rg=row(g_small["b_rg"]), w_ig=w_ig[j].astype(BF16),
                    b_ig=row(g_small["b_ig"]), lam=row(g_small["lru_lambda"]))

    pending, exchanges, small_layout = {}, {}, {}
    last_token, pair_leg = [], []

    def on_grads(layer, grads, dep, last):
        par = layer % 2
        w_in_name, w_out_name = BIG[par]
        have = pending.setdefault(layer, {})
        have.update(grads)
        eager = layer == 0
        scatter, gather = {}, {}
        if w_out_name in have and (eager or last):
            scatter["out"] = have.pop(w_out_name).reshape(N_DEV, -1, D_MODEL)
        pair, chip = {}, {}
        if w_in_name in have and eager:
            pair["in"] = have.pop(w_in_name)
        elif w_in_name in have and last:
            scatter["in"] = have.pop(w_in_name)
        deps = []
        if pair:
            whole = pair["in"]
            deps = [lax.dynamic_index_in_dim(whole.reshape((4, 2) + whole.shape[1:]), lax.axis_index("c"), 1,
                                             keepdims=False)]
        if eager and last:
            mine, theirs = pair_leg.pop()
            chip["in"] = _pair_sum(mine, _exchange_wait(theirs, list(grads.values()), "grads_wait_pair")[0])
        if layer not in small_layout and all(n in have for n in SMALL[par]) and (eager or last):
            blocks = [_to_blocks(n, have[n]) for n in SMALL[par]]
            small_layout[layer] = [b.shape[1:] for b in blocks]
            scatter["small"] = _pack_rows(jnp.concatenate([b.reshape(N_DEV, -1) for b in blocks], axis=1))
        for n in REPLICATED[1]:
            if n in have:
                gather[n] = have.pop(n).astype(BF16).reshape(-1, LRU_HD)
        if last:
            vectors = [have[n] for n in REPLICATED[par]] if par == 0 else []
            if "final_norm" in have:
                vectors.append(have["final_norm"])
            if vectors:
                gather["rep32"] = jnp.concatenate([v.reshape(-1) for v in vectors]).reshape(-1, 128)
        my_chip = 2 * lax.axis_index("x") + lax.axis_index("y")
        groups, keys = [], []
        for mode, arrays, pairs in (
                ("scatter", scatter, [(s, landing(lax.dynamic_slice_in_dim(s, me, 1, 0))) for s in scatter.values()]),
                ("gather", gather, [(s, landing(s[None])) for s in gather.values()]),
                ("pair_scatter", pair, [(s, lax.empty((4,) + s.shape[1:], s.dtype)) for s in pair.values()]),
                ("chip_scatter", chip, [(s, lax.dynamic_update_slice(
                    lax.empty(s.shape, s.dtype), lax.dynamic_slice_in_dim(s, my_chip, 1, 0),
                    (my_chip,) + (0,) * (s.ndim - 1))) for s in chip.values()])):
            if arrays:
                groups.append((mode, pairs))
                keys.append(list(arrays))
        if not groups:
            return dep
        handles, token = _exchange_start(groups, deps, f"grads_start_{layer}_{'_'.join(k for ks in keys for k in ks)}")
        if pair:
            pair_leg.append((deps[0], handles.pop()))
            keys.pop()
        exchanges.setdefault(layer, []).extend(zip(keys, handles))
        last_token[:] = [token]
        return token

    loss, grad_x = _local_step(x[0], loss_target[0], layer_weights, final_norm[None], on_grads)

    w3 = {n: _as3d(w_loc[n]) for n in NAMES}
    m3 = {n: _as3d(m_loc[n]) for n in NAMES}
    v3 = {n: _as3d(v_loc[n]) for n in NAMES}
    results = {}
    after = list(last_token)
    for layer in (3, 2, 1, 0):
        par, j = layer % 2, layer // 2
        got = {}
        for keys, handle in exchanges[layer]:
            got.update(zip(keys, _exchange_wait(handle, after, f"grads_wait_{layer}_{'_'.join(keys)}")))
        parts = {BIG[par][0]: got["in"], BIG[par][1]: got["out"]}
        parts.update(zip(SMALL[par], _unpack(got["small"].reshape(N_DEV, -1), small_layout[layer])))
        if par == 1:
            parts.update({n: got[n] for n in REPLICATED[1]})
        else:
            parts.update(zip(REPLICATED[0], _unpack(got["rep32"].reshape(N_DEV, -1)[:, :len(REPLICATED[0]) * D_MODEL],
                                                    [w_loc[n].shape[1:] for n in REPLICATED[0]])))
        big_items, small_items = [], []
        for n, pt in parts.items():
            item = (w3[n], pt.reshape((pt.shape[0], 1) + w3[n].shape[1:]), m3[n], v3[n])
            (big_items if w3[n].shape[1] >= 128 else small_items).append((n, item))
        for n, item in big_items:
            results[n] = _adamw([item], j, [results[n]] if n in results else None, f"adamw_{n}_{layer}")[0]
        snames = [n for n, _ in small_items]
        prev = [results[n] for n in snames] if snames[0] in results else None
        for n, r in zip(snames, _adamw([it for _, it in small_items], j, prev, f"adamw_small_{layer}")):
            results[n] = r
        if layer == N_LAYERS - 1:
            item = (w3["final_norm"], got["rep32"].reshape(N_DEV, 1, 1, -1)[..., :D_MODEL], m3["final_norm"],
                    v3["final_norm"])
            results["final_norm"] = _adamw([item], 0, None, "adamw_final_norm")[0]
        after = [results[BIG[par][0]][1]]

    total = lax.psum(loss[0, 0], ("x", "y", "c"))
    outs = [[results[n][k].reshape(w_loc[n].shape) for n in NAMES] for k in range(4)]
    return (total, grad_x[None], *outs[0], *outs[1], *outs[2], *outs[3])
```

```python
import functools

import jax
import jax.numpy as jnp
from jax import lax
from jax.experimental import pallas as pl
from jax.experimental.pallas import tpu as pltpu

F32 = jnp.float32
BF16 = jnp.bfloat16

N_DEV = 8
N_PEERS = N_DEV - 1
N_LAYERS = 4
D_MODEL = 1024
EPS_RMS = 1e-6
EPS_LN = 1e-5
W_CONV = 1024
CONV_K = 31
W_POOL = 1024
POOL_WINDOWS = (2, 4, 8, 16)
POOL_GW = 256
W_EVEN_IN = 5120
W_EVEN_MIX = 2048
LRU_HEADS = 12
LRU_HD = 128
W_LRU = 1536
LRU_CONV_K = 4
LRU_C = 8.0
ADAM_LR = 0.001
ADAM_B1 = 0.9
ADAM_B2 = 0.999
ADAM_EPS = 1e-08
ADAM_WD = 0.01
ADAM_STEP = 10

HALO = 32
HALO_C = 8
TM_MATMUL = 512
TM_STREAM = 1024
TM_WGRAD = 2048
TM_MIXER = 256
CONV_ROWS = 128
MIX_ROWS = 32
VMEM_LIMIT = 56 * 1024 * 1024
MESH = pl.DeviceIdType.MESH
ANY = pl.BlockSpec(memory_space=pl.ANY)
HBM = pl.BlockSpec(memory_space=pltpu.HBM)
SEM = pl.BlockSpec(memory_space=pltpu.SEMAPHORE)


def _params(*sem):
    return pltpu.CompilerParams(dimension_semantics=sem, vmem_limit_bytes=VMEM_LIMIT)


def _sigmoid(z):
    return 0.5 * jnp.tanh(0.5 * z) + 0.5


def _dsilu(z, s):
    return s * (1.0 + z * (1.0 - s))


def _full(shape):
    nd = len(shape)
    return pl.BlockSpec(shape, lambda *_: (0,) * nd)


def _norm_matmul(h, g, w, dep, *, tm):
    t, d = h.shape
    n = w.shape[1]
    tn = n // 4

    def body(h_ref, g_ref, w_ref, dep_ref, p_ref, hn_ref):
        @pl.when(pl.program_id(1) == 0)
        def _():
            x = h_ref[...]
            r = lax.rsqrt(jnp.mean(x * x, axis=-1, keepdims=True) + EPS_RMS)
            hn_ref[...] = ((x * r) * g_ref[...]).astype(BF16)

        p_ref[...] = jnp.dot(hn_ref[...], w_ref[...], preferred_element_type=F32)

    return pl.pallas_call(
        body, name="norm_matmul", grid=(t // tm, n // tn),
        in_specs=[pl.BlockSpec((tm, d), lambda i, j: (i, 0)), _full((1, d)),
                  pl.BlockSpec((d, tn), lambda i, j: (0, j)), ANY],
        out_specs=[pl.BlockSpec((tm, tn), lambda i, j: (i, j)), pl.BlockSpec((tm, d), lambda i, j: (i, 0))],
        out_shape=[jax.ShapeDtypeStruct((t, n), F32), jax.ShapeDtypeStruct((t, d), BF16)],
        compiler_params=_params("arbitrary", "arbitrary"),
    )(h, g, w, dep)


def _out_proj(h, y, w, dep, *, tm):
    t, d = h.shape
    k = y.shape[1]

    def body(h_ref, y_ref, w_ref, dep_ref, o_ref):
        o_ref[...] = h_ref[...] + jnp.dot(y_ref[...], w_ref[...], preferred_element_type=F32)

    return pl.pallas_call(
        body, name="out_proj", grid=(t // tm,),
        in_specs=[pl.BlockSpec((tm, d), lambda i: (i, 0)), pl.BlockSpec((tm, k), lambda i: (i, 0)), _full((k, d)),
                  ANY],
        out_specs=pl.BlockSpec((tm, d), lambda i: (i, 0)),
        out_shape=jax.ShapeDtypeStruct((t, d), F32),
        compiler_params=_params("arbitrary"),
    )(h, y, w, dep)


def _out_proj_bwd(dh, y, w, dep, *, tm):
    t, d = dh.shape
    k = y.shape[1]
    nt = t // tm

    def body(dh_ref, y_ref, w_ref, dep_ref, dy_ref, dw_ref, acc):
        i = pl.program_id(0)
        g = dh_ref[...].astype(BF16)
        dy_ref[...] = lax.dot_general(g, w_ref[...], (((1,), (1,)), ((), ())), preferred_element_type=F32)
        part = lax.dot_general(y_ref[...], g, (((0,), (0,)), ((), ())), preferred_element_type=F32)

        @pl.when(i == 0)
        def _():
            acc[...] = part

        @pl.when(i > 0)
        def _():
            acc[...] += part

        @pl.when(i == nt - 1)
        def _():
            dw_ref[...] = acc[...].astype(BF16)

    return pl.pallas_call(
        body, name="out_proj_bwd", grid=(nt,),
        in_specs=[pl.BlockSpec((tm, d), lambda i: (i, 0)), pl.BlockSpec((tm, k), lambda i: (i, 0)), _full((k, d)),
                  ANY],
        out_specs=[pl.BlockSpec((tm, k), lambda i: (i, 0)), _full((k, d))],
        out_shape=[jax.ShapeDtypeStruct((t, k), F32), jax.ShapeDtypeStruct((k, d), BF16)],
        scratch_shapes=[pltpu.VMEM((k, d), F32)],
        compiler_params=_params("arbitrary"),
    )(dh, y, w, dep)


def _in_proj_bwd_x(dp, w, h, g, dh_out, dep, *, tm):
    t, d = h.shape
    n = w.shape[1]
    nt = t // tm

    def body(dp_ref, w_ref, h_ref, g_ref, dho_ref, dep_ref, dh_ref, dg_ref):
        i = pl.program_id(0)
        dy = lax.dot_general(dp_ref[...], w_ref[...], (((1,), (1,)), ((), ())), preferred_element_type=F32)
        x = h_ref[...]
        r = lax.rsqrt(jnp.mean(x * x, axis=-1, keepdims=True) + EPS_RMS)
        gd = dy * g_ref[...]
        m = jnp.mean(gd * x, axis=-1, keepdims=True)
        dh_ref[...] = dho_ref[...] + r * gd - x * (r * r * r * m)
        _acc_out(i, dg_ref, jnp.sum(dy * x * r, axis=0, keepdims=True))

    return pl.pallas_call(
        body, name="in_proj_bwd_x", grid=(nt,),
        in_specs=[pl.BlockSpec((tm, n), lambda i: (i, 0)), _full((d, n)),
                  pl.BlockSpec((tm, d), lambda i: (i, 0)), _full((1, d)), pl.BlockSpec((tm, d), lambda i: (i, 0)),
                  ANY],
        out_specs=[pl.BlockSpec((tm, d), lambda i: (i, 0)), _full((1, d))],
        out_shape=[jax.ShapeDtypeStruct((t, d), F32), jax.ShapeDtypeStruct((1, d), F32)],
        compiler_params=_params("arbitrary"),
    )(dp, w, h, g, dh_out, dep)


def _in_proj_bwd_w(hn, dp, nd, *, tm):
    t, d = hn.shape
    nb = dp.shape[1] // nd
    nt = t // tm

    def body(hn_ref, dp_ref, dw_ref, acc):
        i = pl.program_id(1)
        part = lax.dot_general(hn_ref[...], dp_ref[...], (((0,), (0,)), ((), ())), preferred_element_type=F32)

        @pl.when(i == 0)
        def _():
            acc[...] = part

        @pl.when(i > 0)
        def _():
            acc[...] += part

        @pl.when(i == nt - 1)
        def _():
            dw_ref[0] = acc[...].astype(BF16)

    return pl.pallas_call(
        body, name="in_proj_bwd_w", grid=(nd, nt),
        in_specs=[pl.BlockSpec((tm, d), lambda j, i: (i, 0)), pl.BlockSpec((tm, nb), lambda j, i: (i, j))],
        out_specs=pl.BlockSpec((1, d, nb), lambda j, i: (j, 0, 0)),
        out_shape=jax.ShapeDtypeStruct((nd, d, nb), BF16),
        scratch_shapes=[pltpu.VMEM((d, nb), F32)],
        compiler_params=_params("arbitrary", "arbitrary"),
    )(hn, dp)


def _loss_head(h, g, target, *, tm):
    t, d = h.shape
    nt = t // tm

    def body(h_ref, g_ref, t_ref, loss_ref, dh_ref, dg_ref):
        i = pl.program_id(0)
        x = h_ref[...]
        r = lax.rsqrt(jnp.mean(x * x, axis=-1, keepdims=True) + EPS_RMS)
        xr = x * r
        err = xr * g_ref[...] - t_ref[...]
        lp = 0.5 * jnp.sum(jnp.mean(err * err, axis=-1, keepdims=True), axis=0, keepdims=True)
        dy = err * (1.0 / d)
        gd = dy * g_ref[...]
        m = jnp.mean(gd * x, axis=-1, keepdims=True)
        dh_ref[...] = r * gd - x * (r * r * r * m)
        dgp = jnp.sum(dy * xr, axis=0, keepdims=True)

        @pl.when(i == 0)
        def _():
            loss_ref[...] = lp
            dg_ref[...] = dgp

        @pl.when(i > 0)
        def _():
            loss_ref[...] += lp
            dg_ref[...] += dgp

    return pl.pallas_call(
        body, name="loss_head", grid=(nt,),
        in_specs=[pl.BlockSpec((tm, d), lambda i: (i, 0)), _full((1, d)), pl.BlockSpec((tm, d), lambda i: (i, 0))],
        out_specs=[_full((1, 1)), pl.BlockSpec((tm, d), lambda i: (i, 0)), _full((1, d))],
        out_shape=[jax.ShapeDtypeStruct((1, 1), F32), jax.ShapeDtypeStruct((t, d), F32),
                   jax.ShapeDtypeStruct((1, d), F32)],
        compiler_params=_params("arbitrary"),
    )(h, g, target)


def _col(tm, w, c):
    return pl.BlockSpec((tm, w), lambda i: (i, c))


def _prev_halo(tm, rows, w, c):
    per = tm // rows
    return pl.BlockSpec((rows, w), lambda i: (jnp.maximum(i * per - 1, 0), c))


def _next_halo(tm, rows, w, c, t):
    per = tm // rows
    last = t // rows - 1
    return pl.BlockSpec((rows, w), lambda i: (jnp.minimum((i + 1) * per, last), c))


def _inv_count(first_row, rows, window):
    tpos = first_row + lax.broadcasted_iota(jnp.int32, (rows, 1), 0)
    return 1.0 / jnp.minimum(tpos + 1, window).astype(F32)


def _even_fwd(p, cw, cb, lg, lb, pw, pb, ps, *, tm):
    t = p.shape[0]
    wc = W_CONV

    def body(av, ag, agate, bv, bgate, avh, agh, bvh, cw_ref, cb_ref, lg_ref, lb_ref, pw_ref, pb_ref, ps_ref,
             y_ref, u1_ref, e_ref, d_ref, uext, vext):
        i = pl.program_id(0)
        keep = (i > 0).astype(F32)
        uext[0:HALO, :] = keep * (avh[...] * _sigmoid(agh[...]))
        uext[HALO:, :] = av[...] * _sigmoid(ag[...])
        vext[0:HALO, :] = keep * bvh[...]
        vext[HALO:, :] = bv[...]
        for c in range(0, wc, 128):
            for rb in range(0, tm, CONV_ROWS):
                acc = jnp.broadcast_to(cb_ref[:, c:c + 128], (CONV_ROWS, 128))
                for r in range(8):
                    shifted = uext[pl.ds(rb + 8 - r, CONV_ROWS + HALO - 8), c:c + 128]
                    for q in range(HALO // 8):
                        s = 8 * q + r
                        if s < CONV_K:
                            acc = acc + cw_ref[CONV_K - 1 - s:CONV_K - s, c:c + 128] * shifted[24 - 8 * q:24 - 8 * q + CONV_ROWS]
                u1_ref[rb:rb + CONV_ROWS, c:c + 128] = acc
        u1 = u1_ref[...]
        mu = jnp.mean(u1, axis=-1, keepdims=True)
        xc = u1 - mu
        rs = lax.rsqrt(jnp.mean(xc * xc, axis=-1, keepdims=True) + EPS_LN)
        u2 = (xc * rs) * lg_ref[...] + lb_ref[...]
        u3 = u2 * _sigmoid(u2)
        ga = agate[...]
        y_ref[:, 0:wc] = (u3 * (ga * _sigmoid(ga))).astype(BF16)
        for g, win in enumerate(POOL_WINDOWS):
            cs = slice(g * POOL_GW, (g + 1) * POOL_GW)
            s = vext[pl.ds(HALO, tm), cs]
            for j in range(1, win):
                s = s + vext[pl.ds(HALO - j, tm), cs]
            dg = s * _inv_count(i * tm, tm, win) - vext[pl.ds(HALO, tm), cs]
            dgb = dg.astype(BF16)
            d_ref[:, cs] = dgb
            eg = jnp.dot(dgb, pw_ref[g], preferred_element_type=F32) + pb_ref[:, cs]
            e_ref[:, cs] = eg
            gb = bgate[:, cs]
            y_ref[:, wc + g * POOL_GW:wc + (g + 1) * POOL_GW] = ((eg * ps_ref[:, cs]) * (gb * _sigmoid(gb))).astype(BF16)

    row = lambda w: pl.BlockSpec((tm, w), lambda i: (i, 0))
    return pl.pallas_call(
        body, name="even_fwd", grid=(t // tm,),
        in_specs=[_col(tm, wc, 0), _col(tm, wc, 1), _col(tm, wc, 2), _col(tm, wc, 3), _col(tm, wc, 4),
                  _prev_halo(tm, HALO, wc, 0), _prev_halo(tm, HALO, wc, 1), _prev_halo(tm, HALO, wc, 3),
                  _full((32, wc)), _full((1, wc)), _full((1, wc)), _full((1, wc)),
                  _full((4, POOL_GW, POOL_GW)), _full((1, wc)), _full((1, wc))],
        out_specs=[row(2 * wc), row(wc), row(wc), row(wc)],
        out_shape=[jax.ShapeDtypeStruct((t, 2 * wc), BF16), jax.ShapeDtypeStruct((t, wc), F32),
                   jax.ShapeDtypeStruct((t, wc), F32), jax.ShapeDtypeStruct((t, wc), BF16)],
        scratch_shapes=[pltpu.VMEM((tm + HALO, wc), F32), pltpu.VMEM((tm + HALO, wc), F32)],
        compiler_params=_params("arbitrary"),
    )(p, p, p, p, p, p, p, p, cw, cb, lg, lb, pw, pb, ps)


def _acc_out(i, ref, val):
    @pl.when(i == 0)
    def _():
        ref[...] = val

    @pl.when(i > 0)
    def _():
        ref[...] += val


def _even_bwd_a(p, u1, e, dmat, dy, lg, lb, pw, ps, dep, *, tm):
    t = p.shape[0]
    wc = W_CONV

    def body(agate, bgate, u1_ref, e_ref, d_ref, dya, dyb, lg_ref, lb_ref, pw_ref, ps_ref, dep_ref,
             du1_ref, dd_ref, dgat_ref, dlg_ref, dlb_ref, dpb_ref, dps_ref, dpw_ref, de_s):
        i = pl.program_id(0)

        @pl.when(i == 0)
        def _():
            for ref in (dpw_ref, dlg_ref, dlb_ref, dpb_ref, dps_ref):
                ref[...] = jnp.zeros_like(ref)

        def rows(k, carry):
            rs_ = pl.ds(pl.multiple_of(k * MIX_ROWS, MIX_ROWS), MIX_ROWS)
            u1 = u1_ref[rs_, :]
            mu = jnp.mean(u1, axis=-1, keepdims=True)
            xc = u1 - mu
            rs = lax.rsqrt(jnp.mean(xc * xc, axis=-1, keepdims=True) + EPS_LN)
            xh = xc * rs
            u2 = xh * lg_ref[...] + lb_ref[...]
            s2 = _sigmoid(u2)
            ga = agate[rs_, :]
            sa = _sigmoid(ga)
            dy_a = dya[rs_, :]
            dgat_ref[rs_, 0:wc] = (dy_a * (u2 * s2) * _dsilu(ga, sa)).astype(BF16)
            du2 = dy_a * (ga * sa) * _dsilu(u2, s2)
            dlg_ref[...] += jnp.sum(du2 * xh, axis=0, keepdims=True)
            dlb_ref[...] += jnp.sum(du2, axis=0, keepdims=True)
            dxh = du2 * lg_ref[...]
            m1 = jnp.mean(dxh, axis=-1, keepdims=True)
            m2 = jnp.mean(dxh * xh, axis=-1, keepdims=True)
            du1_ref[rs_, :] = rs * (dxh - m1 - xh * m2)

            gb = bgate[rs_, :]
            sb = _sigmoid(gb)
            ev = e_ref[rs_, :]
            dy_b = dyb[rs_, :]
            dgat_ref[rs_, wc:2 * wc] = (dy_b * (ev * ps_ref[...]) * _dsilu(gb, sb)).astype(BF16)
            dz = dy_b * (gb * sb)
            dps_ref[...] += jnp.sum(dz * ev, axis=0, keepdims=True)
            de = dz * ps_ref[...]
            dpb_ref[...] += jnp.sum(de, axis=0, keepdims=True)
            de_s[rs_, :] = de.astype(BF16)
            return carry

        lax.fori_loop(0, tm // MIX_ROWS, rows, 0)
        for g in range(len(POOL_WINDOWS)):
            cs = slice(g * POOL_GW, (g + 1) * POOL_GW)
            deg = de_s[:, cs]
            dd_ref[:, cs] = lax.dot_general(deg, pw_ref[g], (((1,), (1,)), ((), ())), preferred_element_type=F32)
            dpw_ref[g] += lax.dot_general(d_ref[:, cs], deg, (((0,), (0,)), ((), ())), preferred_element_type=F32)

    row = lambda w: pl.BlockSpec((tm, w), lambda i: (i, 0))
    return pl.pallas_call(
        body, name="even_bwd_a", grid=(t // tm,),
        in_specs=[_col(tm, wc, 2), _col(tm, wc, 4), row(wc), row(wc), row(wc), _col(tm, wc, 0), _col(tm, wc, 1),
                  _full((1, wc)), _full((1, wc)), _full((4, POOL_GW, POOL_GW)), _full((1, wc)), ANY],
        out_specs=[row(wc), row(wc), row(2 * wc), _full((1, wc)), _full((1, wc)), _full((1, wc)), _full((1, wc)),
                   _full((4, POOL_GW, POOL_GW))],
        out_shape=[jax.ShapeDtypeStruct((t, wc), F32), jax.ShapeDtypeStruct((t, wc), F32),
                   jax.ShapeDtypeStruct((t, 2 * wc), BF16)] + [jax.ShapeDtypeStruct((1, wc), F32)] * 4
                  + [jax.ShapeDtypeStruct((4, POOL_GW, POOL_GW), F32)],
        scratch_shapes=[pltpu.VMEM((tm, wc), BF16)],
        compiler_params=_params("arbitrary"),
    )(p, p, u1, e, dmat, dy, dy, lg, lb, pw, ps, dep)


def _even_bwd_b(p, du1, dd, dgat, cw, *, tm):
    t = p.shape[0]
    wc = W_CONV
    nt = t // tm

    def body(av, ag, avh, agh, du1_ref, du1n, dd_ref, ddn, dgat_ref, cw_ref, dp_ref, dcw_ref, dcb_ref,
             uext, gext, dext, du0, dcw8):
        i = pl.program_id(0)
        keep_p = (i > 0).astype(F32)
        keep_n = (i < nt - 1).astype(F32)

        @pl.when(i == 0)
        def _():
            dcw8[...] = jnp.zeros_like(dcw8)

        a = av[...]
        sg = _sigmoid(ag[...])
        uext[0:HALO, :] = keep_p * (avh[...] * _sigmoid(agh[...]))
        uext[HALO:, :] = a * sg
        gext[0:tm, :] = du1_ref[...]
        gext[tm:, :] = keep_n * du1n[...]
        for c in range(0, wc, 128):
            for rb in range(0, tm, CONV_ROWS):
                acc = jnp.zeros((CONV_ROWS, 128), F32)
                for r in range(8):
                    ahead = gext[pl.ds(rb + r, CONV_ROWS + HALO - 8), c:c + 128]
                    for q in range(HALO // 8):
                        s = 8 * q + r
                        if s < CONV_K:
                            acc = acc + cw_ref[CONV_K - 1 - s:CONV_K - s, c:c + 128] * ahead[8 * q:8 * q + CONV_ROWS]
                du0[rb:rb + CONV_ROWS, c:c + 128] = acc
                gcur = du1_ref[rb:rb + CONV_ROWS, c:c + 128]
                for r in range(8):
                    behind = uext[pl.ds(rb + 8 - r, CONV_ROWS + HALO - 8), c:c + 128]
                    for q in range(HALO // 8):
                        s = 8 * q + r
                        if s < CONV_K:
                            prod = gcur * behind[24 - 8 * q:24 - 8 * q + CONV_ROWS]
                            part = prod[0:8]
                            for o in range(8, CONV_ROWS, 8):
                                part = part + prod[o:o + 8]
                            k = CONV_K - 1 - s
                            dcw8[8 * k:8 * k + 8, c:c + 128] += part

        @pl.when(i == nt - 1)
        def _():
            for k in range(CONV_K):
                dcw_ref[k:k + 1, :] = jnp.sum(dcw8[8 * k:8 * k + 8, :], axis=0, keepdims=True)
            dcw_ref[CONV_K:32, :] = jnp.zeros((32 - CONV_K, wc), F32)

        _acc_out(i, dcb_ref, jnp.sum(du1_ref[...], axis=0, keepdims=True))
        g0 = du0[...]
        dp_ref[:, 0:wc] = (g0 * sg).astype(BF16)
        dp_ref[:, wc:2 * wc] = (g0 * a * sg * (1.0 - sg)).astype(BF16)
        dp_ref[:, 2 * wc:3 * wc] = dgat_ref[:, 0:wc]
        dp_ref[:, 4 * wc:5 * wc] = dgat_ref[:, wc:2 * wc]
        for g, win in enumerate(POOL_WINDOWS):
            cs = slice(g * POOL_GW, (g + 1) * POOL_GW)
            dext[0:tm, cs] = dd_ref[:, cs] * _inv_count(i * tm, tm, win)
            dext[tm:, cs] = keep_n * (ddn[:, cs] * _inv_count((i + 1) * tm, HALO, win))
            s = dext[pl.ds(0, tm), cs]
            for j in range(1, win):
                s = s + dext[pl.ds(j, tm), cs]
            dp_ref[:, 3 * wc + g * POOL_GW:3 * wc + (g + 1) * POOL_GW] = (s - dd_ref[:, cs]).astype(BF16)

    row = lambda w: pl.BlockSpec((tm, w), lambda i: (i, 0))
    return pl.pallas_call(
        body, name="even_bwd_b", grid=(nt,),
        in_specs=[_col(tm, wc, 0), _col(tm, wc, 1), _prev_halo(tm, HALO, wc, 0), _prev_halo(tm, HALO, wc, 1),
                  row(wc), _next_halo(tm, HALO, wc, 0, t), row(wc), _next_halo(tm, HALO, wc, 0, t), row(2 * wc),
                  _full((32, wc))],
        out_specs=[row(5 * wc), _full((32, wc)), _full((1, wc))],
        out_shape=[jax.ShapeDtypeStruct((t, 5 * wc), BF16), jax.ShapeDtypeStruct((32, wc), F32),
                   jax.ShapeDtypeStruct((1, wc), F32)],
        scratch_shapes=[pltpu.VMEM((tm + HALO, wc), F32), pltpu.VMEM((tm + HALO, wc), F32),
                        pltpu.VMEM((tm + HALO, wc), F32), pltpu.VMEM((tm, wc), F32), pltpu.VMEM((8 * 32, wc), F32)],
        compiler_params=_params("arbitrary"),
    )(p, p, p, p, du1, du1, dd, dd, dgat, cw)


def _softplus_neg(lam):
    z = -lam
    return jnp.maximum(z, 0.0) + jnp.log1p(jnp.exp(-jnp.abs(z)))


def _one_minus_exp(x):
    series = -x * (1.0 + x * (0.5 + x * (1.0 / 6.0 + x * (1.0 / 24.0))))
    return jnp.where(x > -0.02, series, 1.0 - jnp.exp(x))


def _odd_fwd(p, ccw, ccb, wrg, brg, wig, big, lam, *, tm):
    t = p.shape[0]
    wl = W_LRU
    ng = tm // 8

    def body(xr, gate, xrh, ccw_ref, ccb_ref, wrg_ref, brg_ref, wig_ref, big_ref, lam_ref,
             y_ref, xc_ref, r_ref, i_ref, hs_ref, xext, a_s, b_s, carry):
        i = pl.program_id(0)
        keep = (i > 0).astype(F32)
        xext[0:HALO_C, :] = keep * xrh[...]
        xext[HALO_C:, :] = xr[...]
        xc = jnp.broadcast_to(ccb_ref[...], (tm, wl))
        for k in range(LRU_CONV_K):
            xc = xc + ccw_ref[k:k + 1, :] * xext[pl.ds(HALO_C - (LRU_CONV_K - 1) + k, tm), :]
        xc_ref[...] = xc
        for h in range(LRU_HEADS):
            cs = slice(h * LRU_HD, (h + 1) * LRU_HD)
            xh = xc_ref[:, cs].astype(BF16)
            r_ref[:, cs] = _sigmoid(jnp.dot(xh, wrg_ref[h], preferred_element_type=F32) + brg_ref[:, cs])
            i_ref[:, cs] = _sigmoid(jnp.dot(xh, wig_ref[h], preferred_element_type=F32) + big_ref[:, cs])
        log_a = (-LRU_C * _softplus_neg(lam_ref[...])) * r_ref[...]
        a_s[...] = jnp.exp(log_a)
        b_s[...] = jnp.sqrt(_one_minus_exp(2.0 * log_a)) * (i_ref[...] * xc_ref[...])

        @pl.when(i == 0)
        def _():
            carry[...] = jnp.zeros_like(carry)

        rowi = lax.broadcasted_iota(jnp.int32, (8, wl), 0)

        def step(g, c):
            sl = pl.ds(pl.multiple_of(g * 8, 8), 8)
            aa, bb = a_s[sl, :], b_s[sl, :]
            for s in (1, 2, 4):
                m = rowi >= s
                a_sh = jnp.where(m, pltpu.roll(aa, s, 0), 1.0)
                b_sh = jnp.where(m, pltpu.roll(bb, s, 0), 0.0)
                bb = aa * b_sh + bb
                aa = aa * a_sh
            hv = bb + aa * c
            hs_ref[sl, :] = hv
            return hv[7:8, :]

        carry[...] = lax.fori_loop(0, ng, step, carry[...])
        gt = gate[...]
        y_ref[...] = (hs_ref[...] * (gt * _sigmoid(gt))).astype(BF16)

    row = lambda w: pl.BlockSpec((tm, w), lambda i: (i, 0))
    return pl.pallas_call(
        body, name="odd_fwd", grid=(t // tm,),
        in_specs=[_col(tm, wl, 0), _col(tm, wl, 1), _prev_halo(tm, HALO_C, wl, 0), _full((8, wl)), _full((1, wl)),
                  _full((LRU_HEADS, LRU_HD, LRU_HD)), _full((1, wl)), _full((LRU_HEADS, LRU_HD, LRU_HD)),
                  _full((1, wl)), _full((1, wl))],
        out_specs=[row(wl)] * 5,
        out_shape=[jax.ShapeDtypeStruct((t, wl), BF16)] + [jax.ShapeDtypeStruct((t, wl), F32)] * 4,
        scratch_shapes=[pltpu.VMEM((tm + HALO_C, wl), F32), pltpu.VMEM((tm, wl), F32), pltpu.VMEM((tm, wl), F32),
                        pltpu.VMEM((1, wl), F32)],
        compiler_params=_params("arbitrary"),
    )(p, p, p, ccw, ccb, wrg, brg, wig, big, lam)


def _odd_bwd_a(p, xc, r, ig, hs, dy, wrg, wig, lam, dep, *, tm):
    t = p.shape[0]
    wl = W_LRU
    nt = t // tm
    ng = tm // 8
    per = tm // HALO_C

    def body(gate, xc_ref, r_ref, i_ref, hs_ref, hsh, dy_ref, wrg_ref, wig_ref, lam_ref, dep_ref,
             dxc_ref, dgate_ref, dwrg_ref, dwig_ref, dbrg_ref, dbig_ref, dlam_ref,
             hext, a_s, q_s, g_s, dpr_s, dpi_s, carry):
        i = pl.program_id(0)
        ti = nt - 1 - i
        keep = (ti > 0).astype(F32)
        hext[0:HALO_C, :] = keep * hsh[...]
        hext[HALO_C:, :] = hs_ref[...]
        gt = gate[...]
        sg = _sigmoid(gt)
        dyv = dy_ref[...]
        dgate_ref[...] = (dyv * hs_ref[...] * _dsilu(gt, sg)).astype(BF16)
        q_s[...] = dyv * (gt * sg)
        sp = _softplus_neg(lam_ref[...])
        log_a = (-LRU_C * sp) * r_ref[...]
        a_s[...] = jnp.exp(log_a)

        @pl.when(i == 0)
        def _():
            carry[...] = jnp.zeros_like(carry)
            dwrg_ref[...] = jnp.zeros_like(dwrg_ref)
            dwig_ref[...] = jnp.zeros_like(dwig_ref)

        rowi = lax.broadcasted_iota(jnp.int32, (8, wl), 0)

        def step(gr, c):
            sl = pl.ds(pl.multiple_of((ng - 1 - gr) * 8, 8), 8)
            a0 = a_s[sl, :]
            al = jnp.where(rowi < 7, pltpu.roll(a0, 7, 0), 1.0)
            be = q_s[sl, :]
            for s in (1, 2, 4):
                m = rowi + s <= 7
                al_sh = jnp.where(m, pltpu.roll(al, 8 - s, 0), 1.0)
                be_sh = jnp.where(m, pltpu.roll(be, 8 - s, 0), 0.0)
                be = be + al * be_sh
                al = al * al_sh
            gv = be + al * c
            g_s[sl, :] = gv
            return (a0 * gv)[0:1, :]

        carry[...] = lax.fori_loop(0, ng, step, carry[...])

        gv = g_s[...]
        a = a_s[...]
        mult = jnp.sqrt(_one_minus_exp(2.0 * log_a))
        iv = i_ref[...]
        rv = r_ref[...]
        xcv = xc_ref[...]
        hprev = hext[pl.ds(HALO_C - 1, tm), :]
        dla = gv * hprev * a - (gv * iv * xcv) * (a * a) / mult
        di = gv * mult * xcv
        dpr = (dla * (-LRU_C * sp)) * rv * (1.0 - rv)
        dpi = di * iv * (1.0 - iv)
        dpr_s[...] = dpr
        dpi_s[...] = dpi
        dxc_ref[...] = gv * mult * iv
        dsp = jnp.sum(dla * rv, axis=0, keepdims=True) * (-LRU_C)
        _acc_out(i, dlam_ref, -dsp * jax.nn.sigmoid(-lam_ref[...]))
        _acc_out(i, dbrg_ref, jnp.sum(dpr, axis=0, keepdims=True))
        _acc_out(i, dbig_ref, jnp.sum(dpi, axis=0, keepdims=True))
        for h in range(LRU_HEADS):
            cs = slice(h * LRU_HD, (h + 1) * LRU_HD)
            xh = xc_ref[:, cs].astype(BF16)
            dr_h = dpr_s[:, cs].astype(BF16)
            di_h = dpi_s[:, cs].astype(BF16)
            dxc_ref[:, cs] += (
                lax.dot_general(dr_h, wrg_ref[h], (((1,), (1,)), ((), ())), preferred_element_type=F32)
                + lax.dot_general(di_h, wig_ref[h], (((1,), (1,)), ((), ())), preferred_element_type=F32))
            dwrg_ref[h] += lax.dot_general(xh, dr_h, (((0,), (0,)), ((), ())), preferred_element_type=F32)
            dwig_ref[h] += lax.dot_general(xh, di_h, (((0,), (0,)), ((), ())), preferred_element_type=F32)

    rrow = lambda w: pl.BlockSpec((tm, w), lambda i: (nt - 1 - i, 0))
    hspec = pl.BlockSpec((HALO_C, wl), lambda i: (jnp.maximum((nt - 1 - i) * per - 1, 0), 0))
    wspec = _full((LRU_HEADS, LRU_HD, LRU_HD))
    return pl.pallas_call(
        body, name="odd_bwd_a", grid=(nt,),
        in_specs=[pl.BlockSpec((tm, wl), lambda i: (nt - 1 - i, 1)), rrow(wl), rrow(wl), rrow(wl), rrow(wl), hspec,
                  rrow(wl), wspec, wspec, _full((1, wl)), ANY],
        out_specs=[rrow(wl), rrow(wl), wspec, wspec, _full((1, wl)), _full((1, wl)), _full((1, wl))],
        out_shape=[jax.ShapeDtypeStruct((t, wl), F32), jax.ShapeDtypeStruct((t, wl), BF16),
                   jax.ShapeDtypeStruct((LRU_HEADS, LRU_HD, LRU_HD), F32),
                   jax.ShapeDtypeStruct((LRU_HEADS, LRU_HD, LRU_HD), F32)] + [jax.ShapeDtypeStruct((1, wl), F32)] * 3,
        scratch_shapes=[pltpu.VMEM((tm + HALO_C, wl), F32)] + [pltpu.VMEM((tm, wl), F32)] * 5
                       + [pltpu.VMEM((1, wl), F32)],
        compiler_params=_params("arbitrary"),
    )(p, xc, r, ig, hs, hs, dy, wrg, wig, lam, dep)


def _odd_bwd_b(p, dxc, dgate, ccw, *, tm):
    t = p.shape[0]
    wl = W_LRU
    nt = t // tm

    def body(xr, xrh, dxc_ref, dxcn, dgate_ref, ccw_ref, dp_ref, dcw_ref, dcb_ref, xext, gext):
        i = pl.program_id(0)

        @pl.when(i == 0)
        def _():
            dcw_ref[...] = jnp.zeros_like(dcw_ref)

        xext[0:HALO_C, :] = (i > 0).astype(F32) * xrh[...]
        xext[HALO_C:, :] = xr[...]
        gext[0:tm, :] = dxc_ref[...]
        gext[tm:, :] = (i < nt - 1).astype(F32) * dxcn[...]
        g = dxc_ref[...]
        acc = jnp.zeros((tm, wl), F32)
        for k in range(LRU_CONV_K):
            acc = acc + ccw_ref[k:k + 1, :] * gext[pl.ds(LRU_CONV_K - 1 - k, tm), :]
            dcw_ref[k:k + 1, :] += jnp.sum(
                g * xext[pl.ds(HALO_C - (LRU_CONV_K - 1) + k, tm), :], axis=0, keepdims=True)

        _acc_out(i, dcb_ref, jnp.sum(g, axis=0, keepdims=True))
        dp_ref[:, 0:wl] = acc.astype(BF16)
        dp_ref[:, wl:2 * wl] = dgate_ref[...]

    row = lambda w: pl.BlockSpec((tm, w), lambda i: (i, 0))
    return pl.pallas_call(
        body, name="odd_bwd_b", grid=(nt,),
        in_specs=[_col(tm, wl, 0), _prev_halo(tm, HALO_C, wl, 0), row(wl), _next_halo(tm, HALO_C, wl, 0, t), row(wl),
                  _full((8, wl))],
        out_specs=[row(2 * wl), _full((8, wl)), _full((1, wl))],
        out_shape=[jax.ShapeDtypeStruct((t, 2 * wl), BF16), jax.ShapeDtypeStruct((8, wl), F32),
                   jax.ShapeDtypeStruct((1, wl), F32)],
        scratch_shapes=[pltpu.VMEM((tm + HALO_C, wl), F32), pltpu.VMEM((tm + HALO_C, wl), F32)],
        compiler_params=_params("arbitrary"),
    )(p, p, dxc, dxc, dgate, ccw)


def _local_step(x, target, layer_weights, final_norm, on_grads):
    t = x.shape[0]
    tm, tx, tl, tw = min(TM_MATMUL, t), min(TM_MIXER, t), min(TM_STREAM, t), min(TM_WGRAD, t)
    h = x
    saved = []
    for layer in range(N_LAYERS):
        w = layer_weights(layer, h)
        p, hn = _norm_matmul(h, w["norm"], w["w_in"], w["dep"], tm=tl)
        if layer % 2 == 0:
            y, *acts = _even_fwd(p, w["conv_w"], w["conv_b"], w["ln_g"], w["ln_b"], w["pool_w"], w["pool_b"],
                                 w["pool_scale"], tm=tx)
        else:
            y, *acts = _odd_fwd(p, w["conv_w"], w["conv_b"], w["w_rg"], w["b_rg"], w["w_ig"], w["b_ig"], w["lam"],
                                tm=tx)
        w_out, dep = w["w_out"](y)
        saved.append((w, w_out, h, p, hn, y, acts))
        h = _out_proj(h, y, w_out, dep, tm=tm)
    loss, dh, d_final = _loss_head(h, final_norm, target, tm=tm)

    dep = d_final
    for layer in reversed(range(N_LAYERS)):
        w, w_out, h_in, p, hn, y, acts = saved[layer]
        sfx = "even" if layer % 2 == 0 else "odd"
        dy, dw_out = _out_proj_bwd(dh, y, w_out, dep, tm=tm)
        dep = on_grads(layer, {"w_out_" + sfx: dw_out}, dep, False)
        if layer % 2 == 0:
            u1, e, dmat = acts
            du1, dd, dgat, dlg, dlb, dpb, dps, dpw = _even_bwd_a(p, u1, e, dmat, dy, w["ln_g"], w["ln_b"], w["pool_w"],
                                                                 w["pool_scale"], dep, tm=tx)
            dp, dcw, dcb = _even_bwd_b(p, du1, dd, dgat, w["conv_w"], tm=tx)
            grads = dict(conv_a_w=dcw[:CONV_K], conv_a_b=dcb, ln_a_g=dlg, ln_a_b=dlb, pool_w=dpw, pool_b=dpb,
                         pool_scale=dps)
        else:
            xc, r, ig, hs = acts
            dxc, dgate, dwrg, dwig, dbrg, dbig, dlam = _odd_bwd_a(p, xc, r, ig, hs, dy, w["w_rg"], w["w_ig"],
                                                                  w["lam"], dep, tm=tx)
            dp, dccw, dccb = _odd_bwd_b(p, dxc, dgate, w["conv_w"], tm=tx)
            grads = dict(conv_c_w=dccw[:LRU_CONV_K], conv_c_b=dccb, w_rg=dwrg, b_rg=dbrg, w_ig=dwig, b_ig=dbig,
                         lru_lambda=dlam)
        grads["w_in_" + sfx] = _in_proj_bwd_w(hn, dp, N_DEV, tm=tw)
        dep = on_grads(layer, grads, dep, False)
        dh, dg = _in_proj_bwd_x(dp, w["w_in"], h_in, w["norm"], dh, dep, tm=tm)
        rest = {"norm_" + sfx: dg}
        if layer == N_LAYERS - 1:
            rest["final_norm"], rest["loss"] = d_final, loss
        dep = on_grads(layer, rest, dep, True)
    return loss, dh


def _slot(px, py, pc):
    return 4 * px + 2 * py + pc


def _peers(x, y, c):
    return [(1 - x if k & 4 else x, 1 - y if k & 2 else y, 1 - c if k & 1 else c) for k in range(1, N_DEV)]


def _all_gather(arrs, name):
    n = len(arrs)

    def body(*refs):
        ins, outs = refs[:n], refs[n:2 * n]
        send_sems, recv_sems, local_sems = refs[2 * n:]
        x, y, c = lax.axis_index("x"), lax.axis_index("y"), lax.axis_index("c")
        me, sibling = (x, y, c), (x, y, 1 - c)
        chips = [(1 - x, y), (x, 1 - y), (1 - x, 1 - y)]

        def copy(a, k, block, to, src=None):
            rows = outs[a].at[_slot(*block)]
            return pltpu.make_async_remote_copy(
                src_ref=rows if src is None else src, dst_ref=rows, send_sem=send_sems.at[a, k],
                recv_sem=recv_sems.at[a, k], device_id=to, device_id_type=MESH)

        mine = [pltpu.make_async_copy(ins[a], outs[a].at[_slot(*me)], local_sems.at[a]) for a in range(n)]
        for cp in mine:
            cp.start()
        first = []
        for a in range(n):
            first.append(copy(a, 0, me, sibling, src=ins[a]))
            first += [copy(a, 1 + j, me, (*chip, c), src=ins[a]) for j, chip in enumerate(chips)]
        for cp in first:
            cp.start()
        passed = []
        for j, chip in enumerate(chips):
            for a in range(n):
                copy(a, 1 + j, (*chip, c), me).wait_recv()
                fwd = copy(a, 4 + j, (*chip, c), sibling)
                fwd.start()
                passed.append(fwd)
        for a in range(n):
            copy(a, 0, sibling, me).wait_recv()
            for j, chip in enumerate(chips):
                copy(a, 4 + j, (*chip, 1 - c), me).wait_recv()
        for cp in first + passed:
            cp.wait_send()
        for cp in mine:
            cp.wait()

    return pl.pallas_call(
        body, name=name,
        in_specs=[ANY] * n, out_specs=[ANY] * n,
        out_shape=[jax.ShapeDtypeStruct((N_DEV,) + a.shape, a.dtype) for a in arrs],
        scratch_shapes=[pltpu.SemaphoreType.DMA((n, 7)), pltpu.SemaphoreType.DMA((n, 7)),
                        pltpu.SemaphoreType.DMA((n,))],
    )(*arrs)


N_COPIES = {"gather": N_PEERS, "scatter": N_PEERS, "chip_gather": 4, "forward": 3, "pair_scatter": 4,
            "chip_scatter": 3}


def _exchange_plan(mode, x, y, c):
    me = _slot(x, y, c)
    chips = [(1 - x, y), (x, 1 - y), (1 - x, 1 - y)]
    if mode == "forward":
        return [((x, y, 1 - c), ("land", _slot(*chip, c)), _slot(*chip, c), _slot(*chip, 1 - c)) for chip in chips]
    if mode == "pair_scatter":
        return [((x, y, 1 - c), ("src", _slot(q // 2, q % 2, 1 - c)), q, q) for q in range(4)]
    if mode == "chip_scatter":
        return [((*chip, c), ("src", 2 * chip[0] + chip[1]), 2 * x + y, 2 * chip[0] + chip[1]) for chip in chips]
    peers = _peers(x, y, c)
    if mode == "chip_gather":
        peers = [(x, y, 1 - c), (1 - x, y, c), (x, 1 - y, c), (1 - x, 1 - y, c)]
    return [(p, ("src", _slot(*p)) if mode == "scatter" else ("src", None), me, _slot(*p)) for p in peers]


def _exchange_copy(src_ref, land_ref, plan, send_sem, recv_sem, start):
    peer, (which, block), there, here = plan
    src = land_ref if which == "land" else src_ref
    return pltpu.make_async_remote_copy(
        src_ref=src if block is None else src.at[block], dst_ref=land_ref.at[there if start else here],
        send_sem=send_sem, recv_sem=recv_sem, device_id=peer, device_id_type=MESH)


def _exchange_start(groups, deps, name):
    flat = [pair for _, g in groups for pair in g]
    n, ng = len(flat), len(groups)

    def body(*refs):
        src_refs, land_refs = refs[:n], refs[n:2 * n]
        outs = refs[2 * n + len(deps):]
        sems, token = outs[:2 * ng], outs[2 * ng + 2 * n]
        x, y, c = lax.axis_index("x"), lax.axis_index("y"), lax.axis_index("c")
        base = 0
        for gi, (mode, g) in enumerate(groups):
            nc = N_COPIES[mode]
            for k, plan in enumerate(_exchange_plan(mode, x, y, c)):
                for ai in range(len(g)):
                    _exchange_copy(src_refs[base + ai], land_refs[base + ai], plan, sems[2 * gi].at[ai * nc + k],
                                   sems[2 * gi + 1].at[ai * nc + k], True).start()
            base += len(g)
        token[...] = jnp.zeros_like(token)

    operands = [pltpu.with_memory_space_constraint(a, pltpu.HBM) for a in
                [s for s, _ in flat] + [l for _, l in flat]]
    out_shape = []
    for mode, g in groups:
        out_shape += [pltpu.SemaphoreType.DMA((len(g) * N_COPIES[mode],))] * 2
    out_shape += [pltpu.HBM(a.shape, a.dtype) for a in operands]
    out_shape.append(jax.ShapeDtypeStruct((8, 128), F32))
    outs = pl.pallas_call(
        body, name=name, out_shape=out_shape,
        in_specs=[HBM] * (2 * n) + [ANY] * len(deps),
        out_specs=[SEM] * (2 * ng) + [HBM] * (2 * n) + [pl.BlockSpec(memory_space=pltpu.VMEM)],
        input_output_aliases={i: 2 * ng + i for i in range(2 * n)},
        compiler_params=pltpu.CompilerParams(has_side_effects=pltpu.SideEffectType.DATAFLOW_SIDE_EFFECTING),
    )(*operands, *deps)
    handles, base = [], 0
    for gi, (mode, g) in enumerate(groups):
        srcs = outs[2 * ng + base:2 * ng + base + len(g)]
        lands = outs[2 * ng + n + base:2 * ng + n + base + len(g)]
        handles.append((mode, outs[2 * gi], outs[2 * gi + 1], list(srcs), list(lands)))
        base += len(g)
    return handles, outs[-1]


def _exchange_wait(handle, after, name):
    mode, send_sems, recv_sems, srcs, lands = handle
    n = len(srcs)
    nc = N_COPIES[mode]

    def body(*refs):
        src_refs, land_refs = refs[:n], refs[n:2 * n]
        send_ref, recv_ref = refs[2 * n], refs[2 * n + 1]
        x, y, c = lax.axis_index("x"), lax.axis_index("y"), lax.axis_index("c")
        for k, plan in enumerate(_exchange_plan(mode, x, y, c)):
            for a in range(n):
                cp = _exchange_copy(src_refs[a], land_refs[a], plan, send_ref.at[a * nc + k], recv_ref.at[a * nc + k],
                                    False)
                cp.wait_send()
                cp.wait_recv()

    outs = pl.pallas_call(
        body, name=name,
        out_shape=[pltpu.HBM(a.shape, a.dtype) for a in srcs + lands],
        in_specs=[HBM] * (2 * n) + [SEM, SEM] + [ANY] * len(after),
        out_specs=[HBM] * (2 * n),
        input_output_aliases={i: i for i in range(2 * n)},
        compiler_params=pltpu.CompilerParams(has_side_effects=pltpu.SideEffectType.DATAFLOW_SIDE_EFFECTING),
    )(*srcs, *lands, send_sems, recv_sems, *after)
    return list(outs[n:])


def _pair_sum(mine, theirs):
    nq, r, c = mine.shape
    tr = r

    def body(a_ref, b_ref, o_ref):
        o_ref[...] = (a_ref[...].astype(F32) + b_ref[...].astype(F32)).astype(BF16)

    blk = pl.BlockSpec((1, tr, c), lambda q, i: (q, i, 0))
    return pl.pallas_call(
        body, name="pair_sum", grid=(nq, r // tr), in_specs=[blk, blk], out_specs=blk,
        out_shape=jax.ShapeDtypeStruct(mine.shape, BF16),
        compiler_params=_params("arbitrary", "arbitrary"),
    )(mine, theirs)


def _adamw_math(w, g, m, v):
    c1 = 1.0 - ADAM_B1 ** ADAM_STEP
    c2 = 1.0 - ADAM_B2 ** ADAM_STEP
    nm = ADAM_B1 * m + (1.0 - ADAM_B1) * g
    nv = ADAM_B2 * v + (1.0 - ADAM_B2) * (g * g)
    delta = -ADAM_LR * ((nm / c1) / (jnp.sqrt(nv / c2) + ADAM_EPS) + ADAM_WD * w)
    return delta, nm, nv


def _row_tile(r):
    for cand in (256, 128, 64, 32, 16, 8):
        if r % cand == 0 and r > cand:
            return cand
    return r


def _adamw(items, layer0, bufs, name):
    ni = len(items)
    nl = items[0][1].shape[1]
    tiles = [_row_tile(w.shape[1]) for w, _, _, _ in items]
    steps = [w.shape[1] // tr for (w, _, _, _), tr in zip(items, tiles)]
    ns = steps[0]
    assert all(s == ns for s in steps)
    nb = 0 if bufs is None else 4 * ni

    def body(*refs):
        ins, outs = refs[:4 * ni], refs[4 * ni + nb:]
        for k in range(ni):
            w_ref, p_ref, m_ref, v_ref = ins[4 * k:4 * k + 4]
            g = p_ref[0, 0].astype(F32)
            for s in range(1, p_ref.shape[0]):
                g = g + p_ref[s, 0].astype(F32)
            delta, nm, nv = _adamw_math(w_ref[0], g, m_ref[0], v_ref[0])
            g_ref, d_ref, nm_ref, nv_ref = outs[4 * k:4 * k + 4]
            g_ref[0], d_ref[0], nm_ref[0], nv_ref[0] = g, delta, nm, nv

    in_specs, out_specs, out_shape, operands = [], [], [], []
    for (w, parts, m, v), tr in zip(items, tiles):
        blk = pl.BlockSpec((1, tr, w.shape[2]), lambda l, i: (layer0 + l, i, 0))
        in_specs += [blk, pl.BlockSpec((parts.shape[0], 1, tr, w.shape[2]), lambda l, i: (0, l, i, 0)), blk, blk]
        operands += [w, parts, m, v]
        out_specs += [blk] * 4
        out_shape += [jax.ShapeDtypeStruct(w.shape, F32)] * 4
    if bufs is not None:
        in_specs += [ANY] * nb
        operands += [b for item in bufs for b in item]
    outs = pl.pallas_call(
        body, name=name, grid=(nl, ns), in_specs=in_specs, out_specs=out_specs, out_shape=out_shape,
        input_output_aliases={4 * ni + i: i for i in range(nb)},
        compiler_params=_params("arbitrary", "arbitrary"),
    )(*operands)
    return [tuple(outs[4 * k:4 * k + 4]) for k in range(ni)]


NAMES = ("norm_even", "w_in_even", "conv_a_w", "conv_a_b", "ln_a_g", "ln_a_b", "pool_w", "pool_b", "pool_scale",
         "w_out_even", "norm_odd", "w_in_odd", "conv_c_w", "conv_c_b", "w_rg", "b_rg", "w_ig", "b_ig", "lru_lambda",
         "w_out_odd", "final_norm")
SMALL_GATHERED = ("conv_a_w", "pool_b", "norm_odd", "conv_c_w", "conv_c_b", "b_rg", "b_ig", "lru_lambda")
BIG = (("w_in_even", "w_out_even"), ("w_in_odd", "w_out_odd"))
SMALL = (("conv_a_w", "pool_b", "pool_w"), ("norm_odd", "conv_c_w", "conv_c_b", "b_rg", "b_ig", "lru_lambda"))
REPLICATED = (("norm_even", "conv_a_b", "ln_a_g", "ln_a_b", "pool_scale"), ("w_rg", "w_ig"))
PACK_ROW = 1024


def _pack_rows(flat2d):
    pad = (-flat2d.shape[1]) % PACK_ROW
    return jnp.pad(flat2d, ((0, 0), (0, pad))).reshape(flat2d.shape[0], -1, 128)


def _unpack(flat, shapes):
    out, off = [], 0
    for s in shapes:
        n = 1
        for d in s:
            n *= d
        out.append(flat[..., off:off + n].reshape(flat.shape[:-1] + tuple(s)))
        off += n
    return out


def _to_global(name, g):
    if name in ("conv_a_w", "pool_b", "conv_c_w"):
        return jnp.transpose(g, (1, 2, 0, 3)).reshape(g.shape[1], g.shape[2], -1)
    if name == "pool_w":
        return jnp.transpose(g, (1, 2, 0, 3, 4)).reshape(2, 4, POOL_GW, POOL_GW)
    return jnp.transpose(g, (1, 0, 2)).reshape(g.shape[1], -1)


def _to_blocks(name, g):
    if name == "conv_a_w":
        return jnp.transpose(g.reshape(CONV_K, N_DEV, -1), (1, 0, 2))
    if name == "conv_c_w":
        return jnp.transpose(g.reshape(LRU_CONV_K, N_DEV, -1), (1, 0, 2))
    if name == "pool_b":
        return jnp.transpose(g.reshape(4, N_DEV, -1), (1, 0, 2))
    if name == "pool_w":
        return jnp.transpose(g.reshape(4, N_DEV, POOL_GW // N_DEV, POOL_GW), (1, 0, 2, 3))
    return g.reshape(N_DEV, -1)


def _as3d(a):
    if a.ndim == 1:
        return a.reshape(1, 1, -1)
    if a.ndim == 2:
        return a.reshape(a.shape[0], 1, a.shape[1])
    return a.reshape(a.shape[0], -1, a.shape[-1])


def kernel(x, norm_even, w_in_even, conv_a_w, conv_a_b, ln_a_g, ln_a_b, pool_w, pool_b, pool_scale, w_out_even, norm_odd, w_in_odd, conv_c_w, conv_c_b, w_rg, b_rg, w_ig, b_ig, lru_lambda, w_out_odd, final_norm, loss_target, m_norm_even, m_w_in_even, m_conv_a_w, m_conv_a_b, m_ln_a_g, m_ln_a_b, m_pool_w, m_pool_b, m_pool_scale, m_w_out_even, m_norm_odd, m_w_in_odd, m_conv_c_w, m_conv_c_b, m_w_rg, m_b_rg, m_w_ig, m_b_ig, m_lru_lambda, m_w_out_odd, m_final_norm, v_norm_even, v_w_in_even, v_conv_a_w, v_conv_a_b, v_ln_a_g, v_ln_a_b, v_pool_w, v_pool_b, v_pool_scale, v_w_out_even, v_norm_odd, v_w_in_odd, v_conv_c_w, v_conv_c_b, v_w_rg, v_b_rg, v_w_ig, v_b_ig, v_lru_lambda, v_w_out_odd, v_final_norm):
    w_loc = dict(zip(NAMES, [norm_even, w_in_even, conv_a_w, conv_a_b, ln_a_g, ln_a_b, pool_w, pool_b, pool_scale,
                             w_out_even, norm_odd, w_in_odd, conv_c_w, conv_c_b, w_rg, b_rg, w_ig, b_ig, lru_lambda,
                             w_out_odd, final_norm]))
    m_loc = dict(zip(NAMES, [m_norm_even, m_w_in_even, m_conv_a_w, m_conv_a_b, m_ln_a_g, m_ln_a_b, m_pool_w, m_pool_b,
                             m_pool_scale, m_w_out_even, m_norm_odd, m_w_in_odd, m_conv_c_w, m_conv_c_b, m_w_rg,
                             m_b_rg, m_w_ig, m_b_ig, m_lru_lambda, m_w_out_odd, m_final_norm]))
    v_loc = dict(zip(NAMES, [v_norm_even, v_w_in_even, v_conv_a_w, v_conv_a_b, v_ln_a_g, v_ln_a_b, v_pool_w, v_pool_b,
                             v_pool_scale, v_w_out_even, v_norm_odd, v_w_in_odd, v_conv_c_w, v_conv_c_b, v_w_rg,
                             v_b_rg, v_w_ig, v_b_ig, v_lru_lambda, v_w_out_odd, v_final_norm]))
    me = _slot(lax.axis_index("x"), lax.axis_index("y"), lax.axis_index("c"))

    def landing(own):
        zone = lax.empty((N_DEV,) + own.shape[1:], own.dtype)
        return lax.dynamic_update_slice(zone, own, (me,) + (0,) * (own.ndim - 1))

    small_shapes = [w_loc[n].shape for n in SMALL_GATHERED]
    small = jnp.concatenate([w_loc[n].reshape(1, -1) for n in SMALL_GATHERED], axis=1)
    first = _all_gather([w_in_even[0].astype(BF16), pool_w.astype(BF16), _pack_rows(small)[0]], "gather_first")
    g_small = dict(zip(SMALL_GATHERED, [_to_global(n, g) for n, g in
                                        zip(SMALL_GATHERED, _unpack(first[2].reshape(N_DEV, -1), small_shapes))]))
    pool_w_all = _to_global("pool_w", first[1])

    def pairs_of(shards):
        return [(s.astype(BF16), landing(s.astype(BF16)[None])) for s in shards]

    shards = {1: [w_in_odd[0], w_out_odd[0]], 2: [w_in_even[1], w_out_even[1]], 3: [w_in_odd[1], w_out_odd[1]]}
    leg_a, leg_b = {}, {}
    (w_out_0, leg_a[1]), token_1 = _exchange_start(
        [("gather", pairs_of([w_out_even[0]])), ("chip_gather", pairs_of(shards[1]))], [first[0]], "gather_start_1")
    unused = jnp.zeros((8, 128), F32)

    def second_leg(layer, y):
        lands = _exchange_wait(leg_a[layer], [y], f"gather_wait_a_{layer}")
        (leg_b[layer],), token = _exchange_start([("forward", [(unused, l) for l in lands])], [],
                                                 f"gather_forward_{layer}")
        return token

    def layer_weights(layer, h):
        j = layer // 2
        dep = h
        if layer == 0:
            w_in, dep = first[0], token_1
            w_out_now = lambda y: _exchange_wait(w_out_0, [y], "gather_wait_out_0")[0]
        else:
            w_in, w_out_got = _exchange_wait(leg_b[layer], [h], f"gather_wait_b_{layer}")
            w_out_now = lambda y: w_out_got
            if layer + 1 in shards:
                (leg_a[layer + 1],), dep = _exchange_start([("chip_gather", pairs_of(shards[layer + 1]))], [w_in],
                                                           f"gather_start_{layer + 1}")

        def w_out(y):
            return w_out_now(y), (second_leg(layer + 1, y) if layer + 1 in shards else y)

        w_in = jnp.transpose(w_in, (1, 0, 2)).reshape(D_MODEL, -1)
        row = lambda a: a[j][None]
        if layer % 2 == 0:
            return dict(dep=dep, norm=row(norm_even), w_in=w_in,
                        w_out=lambda y: (lambda wo, d: (wo.reshape(W_EVEN_MIX, D_MODEL), d))(*w_out(y)),
                        conv_w=jnp.pad(g_small["conv_a_w"][j], ((0, 1), (0, 0))), conv_b=row(conv_a_b),
                        ln_g=row(ln_a_g), ln_b=row(ln_a_b), pool_w=pool_w_all[j],
                        pool_b=g_small["pool_b"][j].reshape(1, W_POOL), pool_scale=row(pool_scale))
        return dict(dep=dep, norm=row(g_small["norm_odd"]), w_in=w_in,
                    w_out=lambda y: (lambda wo, d: (wo.reshape(W_LRU, D_MODEL), d))(*w_out(y)),
                    conv_w=jnp.pad(g_small["conv_c_w"][j], ((0, 4), (0, 0))), conv_b=row(g_small["conv_c_b"]),
                    w_rg=w_rg[j].astype(BF16), b_rg=row(g_small["b_rg"]), w_ig=w_ig[j].astype(BF16),
                    b_ig=row(g_small["b_ig"]), lam=row(g_small["lru_lambda"]))

    pending, exchanges, small_layout = {}, {}, {}
    last_token, pair_leg = [], []

    def on_grads(layer, grads, dep, last):
        par = layer % 2
        w_in_name, w_out_name = BIG[par]
        have = pending.setdefault(layer, {})
        have.update(grads)
        eager = layer == 0
        scatter, gather = {}, {}
        if w_out_name in have and (eager or last):
            scatter["out"] = have.pop(w_out_name).reshape(N_DEV, -1, D_MODEL)
        pair, chip = {}, {}
        if w_in_name in have and eager:
            pair["in"] = have.pop(w_in_name)
        elif w_in_name in have and last:
            scatter["in"] = have.pop(w_in_name)
        deps = []
        if pair:
            whole = pair["in"]
            deps = [lax.dynamic_index_in_dim(whole.reshape((4, 2) + whole.shape[1:]), lax.axis_index("c"), 1,
                                             keepdims=False)]
        if eager and last:
            mine, theirs = pair_leg.pop()
            chip["in"] = _pair_sum(mine, _exchange_wait(theirs, list(grads.values()), "grads_wait_pair")[0])
        if layer not in small_layout and all(n in have for n in SMALL[par]) and (eager or last):
            blocks = [_to_blocks(n, have[n]) for n in SMALL[par]]
            small_layout[layer] = [b.shape[1:] for b in blocks]
            scatter["small"] = _pack_rows(jnp.concatenate([b.reshape(N_DEV, -1) for b in blocks], axis=1))
        for n in REPLICATED[1]:
            if n in have:
                gather[n] = have.pop(n).astype(BF16).reshape(-1, LRU_HD)
        if last:
            vectors = [have[n] for n in REPLICATED[par]] if par == 0 else []
            if "final_norm" in have:
                vectors += [have["final_norm"], jnp.pad(have["loss"].reshape(-1), (0, 127))]
            if vectors:
                gather["rep32"] = jnp.concatenate([v.reshape(-1) for v in vectors]).reshape(-1, 128)
        my_chip = 2 * lax.axis_index("x") + lax.axis_index("y")
        groups, keys = [], []
        for mode, arrays, pairs in (
                ("scatter", scatter, [(s, landing(lax.dynamic_slice_in_dim(s, me, 1, 0))) for s in scatter.values()]),
                ("gather", gather, [(s, landing(s[None])) for s in gather.values()]),
                ("pair_scatter", pair, [(s, lax.empty((4,) + s.shape[1:], s.dtype)) for s in pair.values()]),
                ("chip_scatter", chip, [(s, lax.dynamic_update_slice(
                    lax.empty(s.shape, s.dtype), lax.dynamic_slice_in_dim(s, my_chip, 1, 0),
                    (my_chip,) + (0,) * (s.ndim - 1))) for s in chip.values()])):
            if arrays:
                groups.append((mode, pairs))
                keys.append(list(arrays))
        if not groups:
            return dep
        handles, token = _exchange_start(groups, deps, f"grads_start_{layer}_{'_'.join(k for ks in keys for k in ks)}")
        if pair:
            pair_leg.append((deps[0], handles.pop()))
            keys.pop()
        exchanges.setdefault(layer, []).extend(zip(keys, handles))
        last_token[:] = [token]
        return token

    loss, grad_x = _local_step(x[0], loss_target[0], layer_weights, final_norm[None], on_grads)

    w3 = {n: _as3d(w_loc[n]) for n in NAMES}
    m3 = {n: _as3d(m_loc[n]) for n in NAMES}
    v3 = {n: _as3d(v_loc[n]) for n in NAMES}
    results, small_parts = {}, {}
    after = list(last_token)
    for layer in (3, 2, 1, 0):
        par, j = layer % 2, layer // 2
        got = {}
        for keys, handle in exchanges[layer]:
            got.update(zip(keys, _exchange_wait(handle, after, f"grads_wait_{layer}_{'_'.join(keys)}")))
        parts = {BIG[par][0]: got["in"], BIG[par][1]: got["out"]}
        parts.update(zip(SMALL[par], _unpack(got["small"].reshape(N_DEV, -1), small_layout[layer])))
        if par == 1:
            parts.update({n: got[n] for n in REPLICATED[1]})
        else:
            parts.update(zip(REPLICATED[0], _unpack(got["rep32"].reshape(N_DEV, -1)[:, :len(REPLICATED[0]) * D_MODEL],
                                                    [w_loc[n].shape[1:] for n in REPLICATED[0]])))
        small_parts[layer] = {}
        for n, pt in parts.items():
            pt = pt.reshape((pt.shape[0], 1) + w3[n].shape[1:])
            if w3[n].shape[1] >= 128:
                results[n] = _adamw([(w3[n], pt, m3[n], v3[n])], j, [results[n]] if n in results else None,
                                    f"adamw_{n}_{layer}")[0]
            else:
                small_parts[layer][n] = pt
        if layer + 2 in small_parts:
            snames = list(small_parts[layer])
            items = [(w3[n], jnp.concatenate([small_parts[layer][n], small_parts[layer + 2][n]], axis=1), m3[n], v3[n])
                     for n in snames]
            for n, r in zip(snames, _adamw(items, 0, None, f"adamw_small_{par}")):
                results[n] = r
        if layer == N_LAYERS - 1:
            tail = got["rep32"].reshape(N_DEV, -1)
            item = (w3["final_norm"], tail[:, :D_MODEL].reshape(N_DEV, 1, 1, D_MODEL), m3["final_norm"],
                    v3["final_norm"])
            results["final_norm"] = _adamw([item], 0, None, "adamw_final_norm")[0]
            total = jnp.sum(tail[:, D_MODEL])
        after = [results[BIG[par][0]][1]]

    outs = [[results[n][k].reshape(w_loc[n].shape) for n in NAMES] for k in range(4)]
    return (total, grad_x[None], *outs[0], *outs[1], *outs[2], *outs[3])
```

```python
import functools

import jax
import jax.numpy as jnp
from jax import lax
from jax.experimental import pallas as pl
from jax.experimental.pallas import tpu as pltpu

F32 = jnp.float32
BF16 = jnp.bfloat16

N_DEV = 8
N_PEERS = N_DEV - 1
N_LAYERS = 4
D_MODEL = 1024
EPS_RMS = 1e-6
EPS_LN = 1e-5
W_CONV = 1024
CONV_K = 31
W_POOL = 1024
POOL_WINDOWS = (2, 4, 8, 16)
POOL_GW = 256
W_EVEN_IN = 5120
W_EVEN_MIX = 2048
LRU_HEADS = 12
LRU_HD = 128
W_LRU = 1536
LRU_CONV_K = 4
LRU_C = 8.0
ADAM_LR = 0.001
ADAM_B1 = 0.9
ADAM_B2 = 0.999
ADAM_EPS = 1e-08
ADAM_WD = 0.01
ADAM_STEP = 10

HALO = 32
HALO_C = 8
TM_MATMUL = 512
TM_STREAM = 1024
TM_WGRAD = 2048
TM_MIXER = 256
CONV_ROWS = 128
MIX_ROWS = 32
VMEM_LIMIT = 56 * 1024 * 1024
MESH = pl.DeviceIdType.MESH
ANY = pl.BlockSpec(memory_space=pl.ANY)
HBM = pl.BlockSpec(memory_space=pltpu.HBM)
SEM = pl.BlockSpec(memory_space=pltpu.SEMAPHORE)


def _params(*sem):
    return pltpu.CompilerParams(dimension_semantics=sem, vmem_limit_bytes=VMEM_LIMIT)


def _sigmoid(z):
    return 0.5 * jnp.tanh(0.5 * z) + 0.5


def _dsilu(z, s):
    return s * (1.0 + z * (1.0 - s))


def _full(shape):
    nd = len(shape)
    return pl.BlockSpec(shape, lambda *_: (0,) * nd)


def _norm_matmul(h, g, w, dep, *, tm):
    t, d = h.shape
    n = w.shape[1]
    tn = n // 4

    def body(h_ref, g_ref, w_ref, dep_ref, p_ref, hn_ref):
        @pl.when(pl.program_id(1) == 0)
        def _():
            x = h_ref[...]
            r = lax.rsqrt(jnp.mean(x * x, axis=-1, keepdims=True) + EPS_RMS)
            hn_ref[...] = ((x * r) * g_ref[...]).astype(BF16)

        p_ref[...] = jnp.dot(hn_ref[...], w_ref[...], preferred_element_type=F32)

    return pl.pallas_call(
        body, name="norm_matmul", grid=(t // tm, n // tn),
        in_specs=[pl.BlockSpec((tm, d), lambda i, j: (i, 0)), _full((1, d)),
                  pl.BlockSpec((d, tn), lambda i, j: (0, j)), ANY],
        out_specs=[pl.BlockSpec((tm, tn), lambda i, j: (i, j)), pl.BlockSpec((tm, d), lambda i, j: (i, 0))],
        out_shape=[jax.ShapeDtypeStruct((t, n), F32), jax.ShapeDtypeStruct((t, d), BF16)],
        compiler_params=_params("arbitrary", "arbitrary"),
    )(h, g, w, dep)


def _out_proj(h, y, w, dep, *, tm):
    t, d = h.shape
    k = y.shape[1]

    def body(h_ref, y_ref, w_ref, dep_ref, o_ref):
        o_ref[...] = h_ref[...] + jnp.dot(y_ref[...], w_ref[...], preferred_element_type=F32)

    return pl.pallas_call(
        body, name="out_proj", grid=(t // tm,),
        in_specs=[pl.BlockSpec((tm, d), lambda i: (i, 0)), pl.BlockSpec((tm, k), lambda i: (i, 0)), _full((k, d)),
                  ANY],
        out_specs=pl.BlockSpec((tm, d), lambda i: (i, 0)),
        out_shape=jax.ShapeDtypeStruct((t, d), F32),
        compiler_params=_params("arbitrary"),
    )(h, y, w, dep)


def _out_proj_bwd(dh, y, w, dep, *, tm):
    t, d = dh.shape
    k = y.shape[1]
    nt = t // tm

    def body(dh_ref, y_ref, w_ref, dep_ref, dy_ref, dw_ref, acc):
        i = pl.program_id(0)
        g = dh_ref[...].astype(BF16)
        dy_ref[...] = lax.dot_general(g, w_ref[...], (((1,), (1,)), ((), ())), preferred_element_type=F32)
        part = lax.dot_general(y_ref[...], g, (((0,), (0,)), ((), ())), preferred_element_type=F32)

        @pl.when(i == 0)
        def _():
            acc[...] = part

        @pl.when(i > 0)
        def _():
            acc[...] += part

        @pl.when(i == nt - 1)
        def _():
            dw_ref[...] = acc[...].astype(BF16)

    return pl.pallas_call(
        body, name="out_proj_bwd", grid=(nt,),
        in_specs=[pl.BlockSpec((tm, d), lambda i: (i, 0)), pl.BlockSpec((tm, k), lambda i: (i, 0)), _full((k, d)),
                  ANY],
        out_specs=[pl.BlockSpec((tm, k), lambda i: (i, 0)), _full((k, d))],
        out_shape=[jax.ShapeDtypeStruct((t, k), F32), jax.ShapeDtypeStruct((k, d), BF16)],
        scratch_shapes=[pltpu.VMEM((k, d), F32)],
        compiler_params=_params("arbitrary"),
    )(dh, y, w, dep)


def _in_proj_bwd_x(dp, w, h, g, dh_out, dep, *, tm):
    t, d = h.shape
    n = w.shape[1]
    nt = t // tm

    def body(dp_ref, w_ref, h_ref, g_ref, dho_ref, dep_ref, dh_ref, dg_ref):
        i = pl.program_id(0)
        dy = lax.dot_general(dp_ref[...], w_ref[...], (((1,), (1,)), ((), ())), preferred_element_type=F32)
        x = h_ref[...]
        r = lax.rsqrt(jnp.mean(x * x, axis=-1, keepdims=True) + EPS_RMS)
        gd = dy * g_ref[...]
        m = jnp.mean(gd * x, axis=-1, keepdims=True)
        dh_ref[...] = dho_ref[...] + r * gd - x * (r * r * r * m)
        _acc_out(i, dg_ref, jnp.sum(dy * x * r, axis=0, keepdims=True))

    return pl.pallas_call(
        body, name="in_proj_bwd_x", grid=(nt,),
        in_specs=[pl.BlockSpec((tm, n), lambda i: (i, 0)), _full((d, n)),
                  pl.BlockSpec((tm, d), lambda i: (i, 0)), _full((1, d)), pl.BlockSpec((tm, d), lambda i: (i, 0)),
                  ANY],
        out_specs=[pl.BlockSpec((tm, d), lambda i: (i, 0)), _full((1, d))],
        out_shape=[jax.ShapeDtypeStruct((t, d), F32), jax.ShapeDtypeStruct((1, d), F32)],
        compiler_params=_params("arbitrary"),
    )(dp, w, h, g, dh_out, dep)


def _in_proj_bwd_w(hn, dp, nd, *, tm):
    t, d = hn.shape
    nb = dp.shape[1] // nd
    nt = t // tm

    def body(hn_ref, dp_ref, dw_ref, acc):
        i = pl.program_id(1)
        part = lax.dot_general(hn_ref[...], dp_ref[...], (((0,), (0,)), ((), ())), preferred_element_type=F32)

        @pl.when(i == 0)
        def _():
            acc[...] = part

        @pl.when(i > 0)
        def _():
            acc[...] += part

        @pl.when(i == nt - 1)
        def _():
            dw_ref[0] = acc[...].astype(BF16)

    return pl.pallas_call(
        body, name="in_proj_bwd_w", grid=(nd, nt),
        in_specs=[pl.BlockSpec((tm, d), lambda j, i: (i, 0)), pl.BlockSpec((tm, nb), lambda j, i: (i, j))],
        out_specs=pl.BlockSpec((1, d, nb), lambda j, i: (j, 0, 0)),
        out_shape=jax.ShapeDtypeStruct((nd, d, nb), BF16),
        scratch_shapes=[pltpu.VMEM((d, nb), F32)],
        compiler_params=_params("arbitrary", "arbitrary"),
    )(hn, dp)


def _loss_head(h, g, target, *, tm):
    t, d = h.shape
    nt = t // tm

    def body(h_ref, g_ref, t_ref, loss_ref, dh_ref, dg_ref):
        i = pl.program_id(0)
        x = h_ref[...]
        r = lax.rsqrt(jnp.mean(x * x, axis=-1, keepdims=True) + EPS_RMS)
        xr = x * r
        err = xr * g_ref[...] - t_ref[...]
        lp = 0.5 * jnp.sum(jnp.mean(err * err, axis=-1, keepdims=True), axis=0, keepdims=True)
        dy = err * (1.0 / d)
        gd = dy * g_ref[...]
        m = jnp.mean(gd * x, axis=-1, keepdims=True)
        dh_ref[...] = r * gd - x * (r * r * r * m)
        dgp = jnp.sum(dy * xr, axis=0, keepdims=True)

        @pl.when(i == 0)
        def _():
            loss_ref[...] = lp
            dg_ref[...] = dgp

        @pl.when(i > 0)
        def _():
            loss_ref[...] += lp
            dg_ref[...] += dgp

    return pl.pallas_call(
        body, name="loss_head", grid=(nt,),
        in_specs=[pl.BlockSpec((tm, d), lambda i: (i, 0)), _full((1, d)), pl.BlockSpec((tm, d), lambda i: (i, 0))],
        out_specs=[_full((1, 1)), pl.BlockSpec((tm, d), lambda i: (i, 0)), _full((1, d))],
        out_shape=[jax.ShapeDtypeStruct((1, 1), F32), jax.ShapeDtypeStruct((t, d), F32),
                   jax.ShapeDtypeStruct((1, d), F32)],
        compiler_params=_params("arbitrary"),
    )(h, g, target)


def _col(tm, w, c):
    return pl.BlockSpec((tm, w), lambda i: (i, c))


def _prev_halo(tm, rows, w, c):
    per = tm // rows
    return pl.BlockSpec((rows, w), lambda i: (jnp.maximum(i * per - 1, 0), c))


def _next_halo(tm, rows, w, c, t):
    per = tm // rows
    last = t // rows - 1
    return pl.BlockSpec((rows, w), lambda i: (jnp.minimum((i + 1) * per, last), c))


def _inv_count(first_row, rows, window):
    tpos = first_row + lax.broadcasted_iota(jnp.int32, (rows, 1), 0)
    return 1.0 / jnp.minimum(tpos + 1, window).astype(F32)


def _even_fwd(p, cw, cb, lg, lb, pw, pb, ps, *, tm):
    t = p.shape[0]
    wc = W_CONV

    def body(av, ag, agate, bv, bgate, avh, agh, bvh, cw_ref, cb_ref, lg_ref, lb_ref, pw_ref, pb_ref, ps_ref,
             y_ref, u1_ref, e_ref, d_ref, uext, vext):
        i = pl.program_id(0)
        keep = (i > 0).astype(F32)
        uext[0:HALO, :] = keep * (avh[...] * _sigmoid(agh[...]))
        uext[HALO:, :] = av[...] * _sigmoid(ag[...])
        vext[0:HALO, :] = keep * bvh[...]
        vext[HALO:, :] = bv[...]
        for c in range(0, wc, 128):
            for rb in range(0, tm, CONV_ROWS):
                acc = jnp.broadcast_to(cb_ref[:, c:c + 128], (CONV_ROWS, 128))
                for r in range(8):
                    shifted = uext[pl.ds(rb + 8 - r, CONV_ROWS + HALO - 8), c:c + 128]
                    for q in range(HALO // 8):
                        s = 8 * q + r
                        if s < CONV_K:
                            acc = acc + cw_ref[CONV_K - 1 - s:CONV_K - s, c:c + 128] * shifted[24 - 8 * q:24 - 8 * q + CONV_ROWS]
                u1_ref[rb:rb + CONV_ROWS, c:c + 128] = acc
        u1 = u1_ref[...]
        mu = jnp.mean(u1, axis=-1, keepdims=True)
        xc = u1 - mu
        rs = lax.rsqrt(jnp.mean(xc * xc, axis=-1, keepdims=True) + EPS_LN)
        u2 = (xc * rs) * lg_ref[...] + lb_ref[...]
        u3 = u2 * _sigmoid(u2)
        ga = agate[...]
        y_ref[:, 0:wc] = (u3 * (ga * _sigmoid(ga))).astype(BF16)
        for g, win in enumerate(POOL_WINDOWS):
            cs = slice(g * POOL_GW, (g + 1) * POOL_GW)
            s = vext[pl.ds(HALO, tm), cs]
            for j in range(1, win):
                s = s + vext[pl.ds(HALO - j, tm), cs]
            dg = s * _inv_count(i * tm, tm, win) - vext[pl.ds(HALO, tm), cs]
            dgb = dg.astype(BF16)
            d_ref[:, cs] = dgb
            eg = jnp.dot(dgb, pw_ref[g], preferred_element_type=F32) + pb_ref[:, cs]
            e_ref[:, cs] = eg
            gb = bgate[:, cs]
            y_ref[:, wc + g * POOL_GW:wc + (g + 1) * POOL_GW] = ((eg * ps_ref[:, cs]) * (gb * _sigmoid(gb))).astype(BF16)

    row = lambda w: pl.BlockSpec((tm, w), lambda i: (i, 0))
    return pl.pallas_call(
        body, name="even_fwd", grid=(t // tm,),
        in_specs=[_col(tm, wc, 0), _col(tm, wc, 1), _col(tm, wc, 2), _col(tm, wc, 3), _col(tm, wc, 4),
                  _prev_halo(tm, HALO, wc, 0), _prev_halo(tm, HALO, wc, 1), _prev_halo(tm, HALO, wc, 3),
                  _full((32, wc)), _full((1, wc)), _full((1, wc)), _full((1, wc)),
                  _full((4, POOL_GW, POOL_GW)), _full((1, wc)), _full((1, wc))],
        out_specs=[row(2 * wc), row(wc), row(wc), row(wc)],
        out_shape=[jax.ShapeDtypeStruct((t, 2 * wc), BF16), jax.ShapeDtypeStruct((t, wc), F32),
                   jax.ShapeDtypeStruct((t, wc), F32), jax.ShapeDtypeStruct((t, wc), BF16)],
        scratch_shapes=[pltpu.VMEM((tm + HALO, wc), F32), pltpu.VMEM((tm + HALO, wc), F32)],
        compiler_params=_params("arbitrary"),
    )(p, p, p, p, p, p, p, p, cw, cb, lg, lb, pw, pb, ps)


def _acc_out(i, ref, val):
    @pl.when(i == 0)
    def _():
        ref[...] = val

    @pl.when(i > 0)
    def _():
        ref[...] += val


def _even_bwd_a(p, u1, e, dmat, dy, lg, lb, pw, ps, dep, *, tm):
    t = p.shape[0]
    wc = W_CONV

    def body(agate, bgate, u1_ref, e_ref, d_ref, dya, dyb, lg_ref, lb_ref, pw_ref, ps_ref, dep_ref,
             du1_ref, dd_ref, dgat_ref, dlg_ref, dlb_ref, dpb_ref, dps_ref, dpw_ref, de_s):
        i = pl.program_id(0)

        @pl.when(i == 0)
        def _():
            for ref in (dpw_ref, dlg_ref, dlb_ref, dpb_ref, dps_ref):
                ref[...] = jnp.zeros_like(ref)

        def rows(k, carry):
            rs_ = pl.ds(pl.multiple_of(k * MIX_ROWS, MIX_ROWS), MIX_ROWS)
            u1 = u1_ref[rs_, :]
            mu = jnp.mean(u1, axis=-1, keepdims=True)
            xc = u1 - mu
            rs = lax.rsqrt(jnp.mean(xc * xc, axis=-1, keepdims=True) + EPS_LN)
            xh = xc * rs
            u2 = xh * lg_ref[...] + lb_ref[...]
            s2 = _sigmoid(u2)
            ga = agate[rs_, :]
            sa = _sigmoid(ga)
            dy_a = dya[rs_, :]
            dgat_ref[rs_, 0:wc] = (dy_a * (u2 * s2) * _dsilu(ga, sa)).astype(BF16)
            du2 = dy_a * (ga * sa) * _dsilu(u2, s2)
            dlg_ref[...] += jnp.sum(du2 * xh, axis=0, keepdims=True)
            dlb_ref[...] += jnp.sum(du2, axis=0, keepdims=True)
            dxh = du2 * lg_ref[...]
            m1 = jnp.mean(dxh, axis=-1, keepdims=True)
            m2 = jnp.mean(dxh * xh, axis=-1, keepdims=True)
            du1_ref[rs_, :] = rs * (dxh - m1 - xh * m2)

            gb = bgate[rs_, :]
            sb = _sigmoid(gb)
            ev = e_ref[rs_, :]
            dy_b = dyb[rs_, :]
            dgat_ref[rs_, wc:2 * wc] = (dy_b * (ev * ps_ref[...]) * _dsilu(gb, sb)).astype(BF16)
            dz = dy_b * (gb * sb)
            dps_ref[...] += jnp.sum(dz * ev, axis=0, keepdims=True)
            de = dz * ps_ref[...]
            dpb_ref[...] += jnp.sum(de, axis=0, keepdims=True)
            de_s[rs_, :] = de.astype(BF16)
            return carry

        lax.fori_loop(0, tm // MIX_ROWS, rows, 0)
        for g in range(len(POOL_WINDOWS)):
            cs = slice(g * POOL_GW, (g + 1) * POOL_GW)
            deg = de_s[:, cs]
            dd_ref[:, cs] = lax.dot_general(deg, pw_ref[g], (((1,), (1,)), ((), ())), preferred_element_type=F32)
            dpw_ref[g] += lax.dot_general(d_ref[:, cs], deg, (((0,), (0,)), ((), ())), preferred_element_type=F32)

    row = lambda w: pl.BlockSpec((tm, w), lambda i: (i, 0))
    return pl.pallas_call(
        body, name="even_bwd_a", grid=(t // tm,),
        in_specs=[_col(tm, wc, 2), _col(tm, wc, 4), row(wc), row(wc), row(wc), _col(tm, wc, 0), _col(tm, wc, 1),
                  _full((1, wc)), _full((1, wc)), _full((4, POOL_GW, POOL_GW)), _full((1, wc)), ANY],
        out_specs=[row(wc), row(wc), row(2 * wc), _full((1, wc)), _full((1, wc)), _full((1, wc)), _full((1, wc)),
                   _full((4, POOL_GW, POOL_GW))],
        out_shape=[jax.ShapeDtypeStruct((t, wc), F32), jax.ShapeDtypeStruct((t, wc), F32),
                   jax.ShapeDtypeStruct((t, 2 * wc), BF16)] + [jax.ShapeDtypeStruct((1, wc), F32)] * 4
                  + [jax.ShapeDtypeStruct((4, POOL_GW, POOL_GW), F32)],
        scratch_shapes=[pltpu.VMEM((tm, wc), BF16)],
        compiler_params=_params("arbitrary"),
    )(p, p, u1, e, dmat, dy, dy, lg, lb, pw, ps, dep)


def _even_bwd_b(p, du1, dd, dgat, cw, *, tm):
    t = p.shape[0]
    wc = W_CONV
    nt = t // tm

    def body(av, ag, avh, agh, du1_ref, du1n, dd_ref, ddn, dgat_ref, cw_ref, dp_ref, dcw_ref, dcb_ref,
             uext, gext, dext, du0, dcw8):
        i = pl.program_id(0)
        keep_p = (i > 0).astype(F32)
        keep_n = (i < nt - 1).astype(F32)

        @pl.when(i == 0)
        def _():
            dcw8[...] = jnp.zeros_like(dcw8)

        a = av[...]
        sg = _sigmoid(ag[...])
        uext[0:HALO, :] = keep_p * (avh[...] * _sigmoid(agh[...]))
        uext[HALO:, :] = a * sg
        gext[0:tm, :] = du1_ref[...]
        gext[tm:, :] = keep_n * du1n[...]
        for c in range(0, wc, 128):
            for rb in range(0, tm, CONV_ROWS):
                acc = jnp.zeros((CONV_ROWS, 128), F32)
                for r in range(8):
                    ahead = gext[pl.ds(rb + r, CONV_ROWS + HALO - 8), c:c + 128]
                    for q in range(HALO // 8):
                        s = 8 * q + r
                        if s < CONV_K:
                            acc = acc + cw_ref[CONV_K - 1 - s:CONV_K - s, c:c + 128] * ahead[8 * q:8 * q + CONV_ROWS]
                du0[rb:rb + CONV_ROWS, c:c + 128] = acc
                gcur = du1_ref[rb:rb + CONV_ROWS, c:c + 128]
                for r in range(8):
                    behind = uext[pl.ds(rb + 8 - r, CONV_ROWS + HALO - 8), c:c + 128]
                    for q in range(HALO // 8):
                        s = 8 * q + r
                        if s < CONV_K:
                            prod = gcur * behind[24 - 8 * q:24 - 8 * q + CONV_ROWS]
                            part = prod[0:8]
                            for o in range(8, CONV_ROWS, 8):
                                part = part + prod[o:o + 8]
                            k = CONV_K - 1 - s
                            dcw8[8 * k:8 * k + 8, c:c + 128] += part

        @pl.when(i == nt - 1)
        def _():
            for k in range(CONV_K):
                dcw_ref[k:k + 1, :] = jnp.sum(dcw8[8 * k:8 * k + 8, :], axis=0, keepdims=True)
            dcw_ref[CONV_K:32, :] = jnp.zeros((32 - CONV_K, wc), F32)

        _acc_out(i, dcb_ref, jnp.sum(du1_ref[...], axis=0, keepdims=True))
        g0 = du0[...]
        dp_ref[:, 0:wc] = (g0 * sg).astype(BF16)
        dp_ref[:, wc:2 * wc] = (g0 * a * sg * (1.0 - sg)).astype(BF16)
        dp_ref[:, 2 * wc:3 * wc] = dgat_ref[:, 0:wc]
        dp_ref[:, 4 * wc:5 * wc] = dgat_ref[:, wc:2 * wc]
        for g, win in enumerate(POOL_WINDOWS):
            cs = slice(g * POOL_GW, (g + 1) * POOL_GW)
            dext[0:tm, cs] = dd_ref[:, cs] * _inv_count(i * tm, tm, win)
            dext[tm:, cs] = keep_n * (ddn[:, cs] * _inv_count((i + 1) * tm, HALO, win))
            s = dext[pl.ds(0, tm), cs]
            for j in range(1, win):
                s = s + dext[pl.ds(j, tm), cs]
            dp_ref[:, 3 * wc + g * POOL_GW:3 * wc + (g + 1) * POOL_GW] = (s - dd_ref[:, cs]).astype(BF16)

    row = lambda w: pl.BlockSpec((tm, w), lambda i: (i, 0))
    return pl.pallas_call(
        body, name="even_bwd_b", grid=(nt,),
        in_specs=[_col(tm, wc, 0), _col(tm, wc, 1), _prev_halo(tm, HALO, wc, 0), _prev_halo(tm, HALO, wc, 1),
                  row(wc), _next_halo(tm, HALO, wc, 0, t), row(wc), _next_halo(tm, HALO, wc, 0, t), row(2 * wc),
                  _full((32, wc))],
        out_specs=[row(5 * wc), _full((32, wc)), _full((1, wc))],
        out_shape=[jax.ShapeDtypeStruct((t, 5 * wc), BF16), jax.ShapeDtypeStruct((32, wc), F32),
                   jax.ShapeDtypeStruct((1, wc), F32)],
        scratch_shapes=[pltpu.VMEM((tm + HALO, wc), F32), pltpu.VMEM((tm + HALO, wc), F32),
                        pltpu.VMEM((tm + HALO, wc), F32), pltpu.VMEM((tm, wc), F32), pltpu.VMEM((8 * 32, wc), F32)],
        compiler_params=_params("arbitrary"),
    )(p, p, p, p, du1, du1, dd, dd, dgat, cw)


def _softplus_neg(lam):
    z = -lam
    return jnp.maximum(z, 0.0) + jnp.log1p(jnp.exp(-jnp.abs(z)))


def _one_minus_exp(x):
    series = -x * (1.0 + x * (0.5 + x * (1.0 / 6.0 + x * (1.0 / 24.0))))
    return jnp.where(x > -0.02, series, 1.0 - jnp.exp(x))


def _odd_fwd(p, ccw, ccb, wrg, brg, wig, big, lam, *, tm):
    t = p.shape[0]
    wl = W_LRU
    ng = tm // 8

    def body(xr, gate, xrh, ccw_ref, ccb_ref, wrg_ref, brg_ref, wig_ref, big_ref, lam_ref,
             y_ref, xc_ref, r_ref, i_ref, hs_ref, xext, a_s, b_s, carry):
        i = pl.program_id(0)
        keep = (i > 0).astype(F32)
        xext[0:HALO_C, :] = keep * xrh[...]
        xext[HALO_C:, :] = xr[...]
        xc = jnp.broadcast_to(ccb_ref[...], (tm, wl))
        for k in range(LRU_CONV_K):
            xc = xc + ccw_ref[k:k + 1, :] * xext[pl.ds(HALO_C - (LRU_CONV_K - 1) + k, tm), :]
        xc_ref[...] = xc
        for h in range(LRU_HEADS):
            cs = slice(h * LRU_HD, (h + 1) * LRU_HD)
            xh = xc_ref[:, cs].astype(BF16)
            r_ref[:, cs] = _sigmoid(jnp.dot(xh, wrg_ref[h], preferred_element_type=F32) + brg_ref[:, cs])
            i_ref[:, cs] = _sigmoid(jnp.dot(xh, wig_ref[h], preferred_element_type=F32) + big_ref[:, cs])
        log_a = (-LRU_C * _softplus_neg(lam_ref[...])) * r_ref[...]
        a_s[...] = jnp.exp(log_a)
        b_s[...] = jnp.sqrt(_one_minus_exp(2.0 * log_a)) * (i_ref[...] * xc_ref[...])

        @pl.when(i == 0)
        def _():
            carry[...] = jnp.zeros_like(carry)

        rowi = lax.broadcasted_iota(jnp.int32, (8, wl), 0)

        def step(g, c):
            sl = pl.ds(pl.multiple_of(g * 8, 8), 8)
            aa, bb = a_s[sl, :], b_s[sl, :]
            for s in (1, 2, 4):
                m = rowi >= s
                a_sh = jnp.where(m, pltpu.roll(aa, s, 0), 1.0)
                b_sh = jnp.where(m, pltpu.roll(bb, s, 0), 0.0)
                bb = aa * b_sh + bb
                aa = aa * a_sh
            hv = bb + aa * c
            hs_ref[sl, :] = hv
            return hv[7:8, :]

        carry[...] = lax.fori_loop(0, ng, step, carry[...])
        gt = gate[...]
        y_ref[...] = (hs_ref[...] * (gt * _sigmoid(gt))).astype(BF16)

    row = lambda w: pl.BlockSpec((tm, w), lambda i: (i, 0))
    return pl.pallas_call(
        body, name="odd_fwd", grid=(t // tm,),
        in_specs=[_col(tm, wl, 0), _col(tm, wl, 1), _prev_halo(tm, HALO_C, wl, 0), _full((8, wl)), _full((1, wl)),
                  _full((LRU_HEADS, LRU_HD, LRU_HD)), _full((1, wl)), _full((LRU_HEADS, LRU_HD, LRU_HD)),
                  _full((1, wl)), _full((1, wl))],
        out_specs=[row(wl)] * 5,
        out_shape=[jax.ShapeDtypeStruct((t, wl), BF16)] + [jax.ShapeDtypeStruct((t, wl), F32)] * 4,
        scratch_shapes=[pltpu.VMEM((tm + HALO_C, wl), F32), pltpu.VMEM((tm, wl), F32), pltpu.VMEM((tm, wl), F32),
                        pltpu.VMEM((1, wl), F32)],
        compiler_params=_params("arbitrary"),
    )(p, p, p, ccw, ccb, wrg, brg, wig, big, lam)


def _odd_bwd_a(p, xc, r, ig, hs, dy, wrg, wig, lam, dep, *, tm):
    t = p.shape[0]
    wl = W_LRU
    nt = t // tm
    ng = tm // 8
    per = tm // HALO_C

    def body(gate, xc_ref, r_ref, i_ref, hs_ref, hsh, dy_ref, wrg_ref, wig_ref, lam_ref, dep_ref,
             dxc_ref, dgate_ref, dwrg_ref, dwig_ref, dbrg_ref, dbig_ref, dlam_ref,
             hext, a_s, q_s, g_s, dpr_s, dpi_s, carry):
        i = pl.program_id(0)
        ti = nt - 1 - i
        keep = (ti > 0).astype(F32)
        hext[0:HALO_C, :] = keep * hsh[...]
        hext[HALO_C:, :] = hs_ref[...]
        gt = gate[...]
        sg = _sigmoid(gt)
        dyv = dy_ref[...]
        dgate_ref[...] = (dyv * hs_ref[...] * _dsilu(gt, sg)).astype(BF16)
        q_s[...] = dyv * (gt * sg)
        sp = _softplus_neg(lam_ref[...])
        log_a = (-LRU_C * sp) * r_ref[...]
        a_s[...] = jnp.exp(log_a)

        @pl.when(i == 0)
        def _():
            carry[...] = jnp.zeros_like(carry)
            dwrg_ref[...] = jnp.zeros_like(dwrg_ref)
            dwig_ref[...] = jnp.zeros_like(dwig_ref)

        rowi = lax.broadcasted_iota(jnp.int32, (8, wl), 0)

        def step(gr, c):
            sl = pl.ds(pl.multiple_of((ng - 1 - gr) * 8, 8), 8)
            a0 = a_s[sl, :]
            al = jnp.where(rowi < 7, pltpu.roll(a0, 7, 0), 1.0)
            be = q_s[sl, :]
            for s in (1, 2, 4):
                m = rowi + s <= 7
                al_sh = jnp.where(m, pltpu.roll(al, 8 - s, 0), 1.0)
                be_sh = jnp.where(m, pltpu.roll(be, 8 - s, 0), 0.0)
                be = be + al * be_sh
                al = al * al_sh
            gv = be + al * c
            g_s[sl, :] = gv
            return (a0 * gv)[0:1, :]

        carry[...] = lax.fori_loop(0, ng, step, carry[...])

        gv = g_s[...]
        a = a_s[...]
        mult = jnp.sqrt(_one_minus_exp(2.0 * log_a))
        iv = i_ref[...]
        rv = r_ref[...]
        xcv = xc_ref[...]
        hprev = hext[pl.ds(HALO_C - 1, tm), :]
        dla = gv * hprev * a - (gv * iv * xcv) * (a * a) / mult
        di = gv * mult * xcv
        dpr = (dla * (-LRU_C * sp)) * rv * (1.0 - rv)
        dpi = di * iv * (1.0 - iv)
        dpr_s[...] = dpr
        dpi_s[...] = dpi
        dxc_ref[...] = gv * mult * iv
        dsp = jnp.sum(dla * rv, axis=0, keepdims=True) * (-LRU_C)
        _acc_out(i, dlam_ref, -dsp * jax.nn.sigmoid(-lam_ref[...]))
        _acc_out(i, dbrg_ref, jnp.sum(dpr, axis=0, keepdims=True))
        _acc_out(i, dbig_ref, jnp.sum(dpi, axis=0, keepdims=True))
        for h in range(LRU_HEADS):
            cs = slice(h * LRU_HD, (h + 1) * LRU_HD)
            xh = xc_ref[:, cs].astype(BF16)
            dr_h = dpr_s[:, cs].astype(BF16)
            di_h = dpi_s[:, cs].astype(BF16)
            dxc_ref[:, cs] += (
                lax.dot_general(dr_h, wrg_ref[h], (((1,), (1,)), ((), ())), preferred_element_type=F32)
                + lax.dot_general(di_h, wig_ref[h], (((1,), (1,)), ((), ())), preferred_element_type=F32))
            dwrg_ref[h] += lax.dot_general(xh, dr_h, (((0,), (0,)), ((), ())), preferred_element_type=F32)
            dwig_ref[h] += lax.dot_general(xh, di_h, (((0,), (0,)), ((), ())), preferred_element_type=F32)

    rrow = lambda w: pl.BlockSpec((tm, w), lambda i: (nt - 1 - i, 0))
    hspec = pl.BlockSpec((HALO_C, wl), lambda i: (jnp.maximum((nt - 1 - i) * per - 1, 0), 0))
    wspec = _full((LRU_HEADS, LRU_HD, LRU_HD))
    return pl.pallas_call(
        body, name="odd_bwd_a", grid=(nt,),
        in_specs=[pl.BlockSpec((tm, wl), lambda i: (nt - 1 - i, 1)), rrow(wl), rrow(wl), rrow(wl), rrow(wl), hspec,
                  rrow(wl), wspec, wspec, _full((1, wl)), ANY],
        out_specs=[rrow(wl), rrow(wl), wspec, wspec, _full((1, wl)), _full((1, wl)), _full((1, wl))],
        out_shape=[jax.ShapeDtypeStruct((t, wl), F32), jax.ShapeDtypeStruct((t, wl), BF16),
                   jax.ShapeDtypeStruct((LRU_HEADS, LRU_HD, LRU_HD), F32),
                   jax.ShapeDtypeStruct((LRU_HEADS, LRU_HD, LRU_HD), F32)] + [jax.ShapeDtypeStruct((1, wl), F32)] * 3,
        scratch_shapes=[pltpu.VMEM((tm + HALO_C, wl), F32)] + [pltpu.VMEM((tm, wl), F32)] * 5
                       + [pltpu.VMEM((1, wl), F32)],
        compiler_params=_params("arbitrary"),
    )(p, xc, r, ig, hs, hs, dy, wrg, wig, lam, dep)


def _odd_bwd_b(p, dxc, dgate, ccw, *, tm):
    t = p.shape[0]
    wl = W_LRU
    nt = t // tm

    def body(xr, xrh, dxc_ref, dxcn, dgate_ref, ccw_ref, dp_ref, dcw_ref, dcb_ref, xext, gext):
        i = pl.program_id(0)

        @pl.when(i == 0)
        def _():
            dcw_ref[...] = jnp.zeros_like(dcw_ref)

        xext[0:HALO_C, :] = (i > 0).astype(F32) * xrh[...]
        xext[HALO_C:, :] = xr[...]
        gext[0:tm, :] = dxc_ref[...]
        gext[tm:, :] = (i < nt - 1).astype(F32) * dxcn[...]
        g = dxc_ref[...]
        acc = jnp.zeros((tm, wl), F32)
        for k in range(LRU_CONV_K):
            acc = acc + ccw_ref[k:k + 1, :] * gext[pl.ds(LRU_CONV_K - 1 - k, tm), :]
            dcw_ref[k:k + 1, :] += jnp.sum(
                g * xext[pl.ds(HALO_C - (LRU_CONV_K - 1) + k, tm), :], axis=0, keepdims=True)

        _acc_out(i, dcb_ref, jnp.sum(g, axis=0, keepdims=True))
        dp_ref[:, 0:wl] = acc.astype(BF16)
        dp_ref[:, wl:2 * wl] = dgate_ref[...]

    row = lambda w: pl.BlockSpec((tm, w), lambda i: (i, 0))
    return pl.pallas_call(
        body, name="odd_bwd_b", grid=(nt,),
        in_specs=[_col(tm, wl, 0), _prev_halo(tm, HALO_C, wl, 0), row(wl), _next_halo(tm, HALO_C, wl, 0, t), row(wl),
                  _full((8, wl))],
        out_specs=[row(2 * wl), _full((8, wl)), _full((1, wl))],
        out_shape=[jax.ShapeDtypeStruct((t, 2 * wl), BF16), jax.ShapeDtypeStruct((8, wl), F32),
                   jax.ShapeDtypeStruct((1, wl), F32)],
        scratch_shapes=[pltpu.VMEM((tm + HALO_C, wl), F32), pltpu.VMEM((tm + HALO_C, wl), F32)],
        compiler_params=_params("arbitrary"),
    )(p, p, dxc, dxc, dgate, ccw)


def _local_step(x, target, layer_weights, final_norm, on_grads):
    t = x.shape[0]
    tm, tx, tl, tw = min(TM_MATMUL, t), min(TM_MIXER, t), min(TM_STREAM, t), min(TM_WGRAD, t)
    h = x
    saved = []
    for layer in range(N_LAYERS):
        w = layer_weights(layer, h)
        p, hn = _norm_matmul(h, w["norm"], w["w_in"], w["dep"], tm=tl)
        if layer % 2 == 0:
            y, *acts = _even_fwd(p, w["conv_w"], w["conv_b"], w["ln_g"], w["ln_b"], w["pool_w"], w["pool_b"],
                                 w["pool_scale"], tm=tx)
        else:
            y, *acts = _odd_fwd(p, w["conv_w"], w["conv_b"], w["w_rg"], w["b_rg"], w["w_ig"], w["b_ig"], w["lam"],
                                tm=tx)
        w_out, dep = w["w_out"](y)
        saved.append((w, w_out, h, p, hn, y, acts))
        h = _out_proj(h, y, w_out, dep, tm=tm)
    loss, dh, d_final = _loss_head(h, final_norm, target, tm=tm)

    dep = d_final
    for layer in reversed(range(N_LAYERS)):
        w, w_out, h_in, p, hn, y, acts = saved[layer]
        sfx = "even" if layer % 2 == 0 else "odd"
        dy, dw_out = _out_proj_bwd(dh, y, w_out, dep, tm=tm)
        dep = on_grads(layer, {"w_out_" + sfx: dw_out}, dep, False)
        if layer % 2 == 0:
            u1, e, dmat = acts
            du1, dd, dgat, dlg, dlb, dpb, dps, dpw = _even_bwd_a(p, u1, e, dmat, dy, w["ln_g"], w["ln_b"], w["pool_w"],
                                                                 w["pool_scale"], dep, tm=tx)
            dp, dcw, dcb = _even_bwd_b(p, du1, dd, dgat, w["conv_w"], tm=tx)
            grads = dict(conv_a_w=dcw[:CONV_K], conv_a_b=dcb, ln_a_g=dlg, ln_a_b=dlb, pool_w=dpw, pool_b=dpb,
                         pool_scale=dps)
        else:
            xc, r, ig, hs = acts
            dxc, dgate, dwrg, dwig, dbrg, dbig, dlam = _odd_bwd_a(p, xc, r, ig, hs, dy, w["w_rg"], w["w_ig"],
                                                                  w["lam"], dep, tm=tx)
            dp, dccw, dccb = _odd_bwd_b(p, dxc, dgate, w["conv_w"], tm=tx)
            grads = dict(conv_c_w=dccw[:LRU_CONV_K], conv_c_b=dccb, w_rg=dwrg, b_rg=dbrg, w_ig=dwig, b_ig=dbig,
                         lru_lambda=dlam)
        grads["w_in_" + sfx] = _in_proj_bwd_w(hn, dp, N_DEV, tm=tw)
        dep = on_grads(layer, grads, dep, False)
        dh, dg = _in_proj_bwd_x(dp, w["w_in"], h_in, w["norm"], dh, dep, tm=tm)
        rest = {"norm_" + sfx: dg}
        if layer == N_LAYERS - 1:
            rest["final_norm"], rest["loss"] = d_final, loss
        dep = on_grads(layer, rest, dep, True)
    return loss, dh


def _slot(px, py, pc):
    return 4 * px + 2 * py + pc


def _peers(x, y, c):
    return [(1 - x if k & 4 else x, 1 - y if k & 2 else y, 1 - c if k & 1 else c) for k in range(1, N_DEV)]


def _all_gather(arrs, name):
    n = len(arrs)

    def body(*refs):
        ins, outs = refs[:n], refs[n:2 * n]
        send_sems, recv_sems, local_sems = refs[2 * n:]
        x, y, c = lax.axis_index("x"), lax.axis_index("y"), lax.axis_index("c")
        me, sibling = (x, y, c), (x, y, 1 - c)
        chips = [(1 - x, y), (x, 1 - y), (1 - x, 1 - y)]

        def copy(a, k, block, to, src=None):
            rows = outs[a].at[_slot(*block)]
            return pltpu.make_async_remote_copy(
                src_ref=rows if src is None else src, dst_ref=rows, send_sem=send_sems.at[a, k],
                recv_sem=recv_sems.at[a, k], device_id=to, device_id_type=MESH)

        mine = [pltpu.make_async_copy(ins[a], outs[a].at[_slot(*me)], local_sems.at[a]) for a in range(n)]
        for cp in mine:
            cp.start()
        first = []
        for a in range(n):
            first.append(copy(a, 0, me, sibling, src=ins[a]))
            first += [copy(a, 1 + j, me, (*chip, c), src=ins[a]) for j, chip in enumerate(chips)]
        for cp in first:
            cp.start()
        passed = []
        for j, chip in enumerate(chips):
            for a in range(n):
                copy(a, 1 + j, (*chip, c), me).wait_recv()
                fwd = copy(a, 4 + j, (*chip, c), sibling)
                fwd.start()
                passed.append(fwd)
        for a in range(n):
            copy(a, 0, sibling, me).wait_recv()
            for j, chip in enumerate(chips):
                copy(a, 4 + j, (*chip, 1 - c), me).wait_recv()
        for cp in first + passed:
            cp.wait_send()
        for cp in mine:
            cp.wait()

    return pl.pallas_call(
        body, name=name,
        in_specs=[ANY] * n, out_specs=[ANY] * n,
        out_shape=[jax.ShapeDtypeStruct((N_DEV,) + a.shape, a.dtype) for a in arrs],
        scratch_shapes=[pltpu.SemaphoreType.DMA((n, 7)), pltpu.SemaphoreType.DMA((n, 7)),
                        pltpu.SemaphoreType.DMA((n,))],
    )(*arrs)


N_COPIES = {"gather": N_PEERS, "scatter": N_PEERS, "chip_gather": 4, "forward": 3, "pair_scatter": 4,
            "chip_scatter": 3}


def _exchange_plan(mode, x, y, c):
    me = _slot(x, y, c)
    chips = [(1 - x, y), (x, 1 - y), (1 - x, 1 - y)]
    if mode == "forward":
        return [((x, y, 1 - c), ("land", _slot(*chip, c)), _slot(*chip, c), _slot(*chip, 1 - c)) for chip in chips]
    if mode == "pair_scatter":
        return [((x, y, 1 - c), ("src", _slot(q // 2, q % 2, 1 - c)), q, q) for q in range(4)]
    if mode == "chip_scatter":
        return [((*chip, c), ("src", 2 * chip[0] + chip[1]), 2 * x + y, 2 * chip[0] + chip[1]) for chip in chips]
    peers = _peers(x, y, c)
    if mode == "chip_gather":
        peers = [(x, y, 1 - c), (1 - x, y, c), (x, 1 - y, c), (1 - x, 1 - y, c)]
    return [(p, ("src", _slot(*p)) if mode == "scatter" else ("src", None), me, _slot(*p)) for p in peers]


def _exchange_copy(src_ref, land_ref, plan, send_sem, recv_sem, start):
    peer, (which, block), there, here = plan
    src = land_ref if which == "land" else src_ref
    return pltpu.make_async_remote_copy(
        src_ref=src if block is None else src.at[block], dst_ref=land_ref.at[there if start else here],
        send_sem=send_sem, recv_sem=recv_sem, device_id=peer, device_id_type=MESH)


def _exchange_start(groups, deps, name):
    flat = [pair for _, g in groups for pair in g]
    n, ng = len(flat), len(groups)

    def body(*refs):
        src_refs, land_refs = refs[:n], refs[n:2 * n]
        outs = refs[2 * n + len(deps):]
        sems, token = outs[:2 * ng], outs[2 * ng + 2 * n]
        x, y, c = lax.axis_index("x"), lax.axis_index("y"), lax.axis_index("c")
        base = 0
        for gi, (mode, g) in enumerate(groups):
            nc = N_COPIES[mode]
            for k, plan in enumerate(_exchange_plan(mode, x, y, c)):
                for ai in range(len(g)):
                    _exchange_copy(src_refs[base + ai], land_refs[base + ai], plan, sems[2 * gi].at[ai * nc + k],
                                   sems[2 * gi + 1].at[ai * nc + k], True).start()
            base += len(g)
        token[...] = jnp.zeros_like(token)

    operands = [pltpu.with_memory_space_constraint(a, pltpu.HBM) for a in
                [s for s, _ in flat] + [l for _, l in flat]]
    out_shape = []
    for mode, g in groups:
        out_shape += [pltpu.SemaphoreType.DMA((len(g) * N_COPIES[mode],))] * 2
    out_shape += [pltpu.HBM(a.shape, a.dtype) for a in operands]
    out_shape.append(jax.ShapeDtypeStruct((8, 128), F32))
    outs = pl.pallas_call(
        body, name=name, out_shape=out_shape,
        in_specs=[HBM] * (2 * n) + [ANY] * len(deps),
        out_specs=[SEM] * (2 * ng) + [HBM] * (2 * n) + [pl.BlockSpec(memory_space=pltpu.VMEM)],
        input_output_aliases={i: 2 * ng + i for i in range(2 * n)},
        compiler_params=pltpu.CompilerParams(has_side_effects=pltpu.SideEffectType.DATAFLOW_SIDE_EFFECTING),
    )(*operands, *deps)
    handles, base = [], 0
    for gi, (mode, g) in enumerate(groups):
        srcs = outs[2 * ng + base:2 * ng + base + len(g)]
        lands = outs[2 * ng + n + base:2 * ng + n + base + len(g)]
        handles.append((mode, outs[2 * gi], outs[2 * gi + 1], list(srcs), list(lands)))
        base += len(g)
    return handles, outs[-1]


def _exchange_wait(handle, after, name):
    mode, send_sems, recv_sems, srcs, lands = handle
    n = len(srcs)
    nc = N_COPIES[mode]

    def body(*refs):
        src_refs, land_refs = refs[:n], refs[n:2 * n]
        send_ref, recv_ref = refs[2 * n], refs[2 * n + 1]
        x, y, c = lax.axis_index("x"), lax.axis_index("y"), lax.axis_index("c")
        for k, plan in enumerate(_exchange_plan(mode, x, y, c)):
            for a in range(n):
                cp = _exchange_copy(src_refs[a], land_refs[a], plan, send_ref.at[a * nc + k], recv_ref.at[a * nc + k],
                                    False)
                cp.wait_send()
                cp.wait_recv()

    outs = pl.pallas_call(
        body, name=name,
        out_shape=[pltpu.HBM(a.shape, a.dtype) for a in srcs + lands],
        in_specs=[HBM] * (2 * n) + [SEM, SEM] + [ANY] * len(after),
        out_specs=[HBM] * (2 * n),
        input_output_aliases={i: i for i in range(2 * n)},
        compiler_params=pltpu.CompilerParams(has_side_effects=pltpu.SideEffectType.DATAFLOW_SIDE_EFFECTING),
    )(*srcs, *lands, send_sems, recv_sems, *after)
    return list(outs[n:])


def _pair_sum(mine, theirs):
    nq, r, c = mine.shape
    tr = r

    def body(a_ref, b_ref, o_ref):
        o_ref[...] = (a_ref[...].astype(F32) + b_ref[...].astype(F32)).astype(BF16)

    blk = pl.BlockSpec((1, tr, c), lambda q, i: (q, i, 0))
    return pl.pallas_call(
        body, name="pair_sum", grid=(nq, r // tr), in_specs=[blk, blk], out_specs=blk,
        out_shape=jax.ShapeDtypeStruct(mine.shape, BF16),
        compiler_params=_params("arbitrary", "arbitrary"),
    )(mine, theirs)


def _adamw_math(w, g, m, v):
    c1 = 1.0 - ADAM_B1 ** ADAM_STEP
    c2 = 1.0 - ADAM_B2 ** ADAM_STEP
    nm = ADAM_B1 * m + (1.0 - ADAM_B1) * g
    nv = ADAM_B2 * v + (1.0 - ADAM_B2) * (g * g)
    delta = -ADAM_LR * ((nm / c1) / (jnp.sqrt(nv / c2) + ADAM_EPS) + ADAM_WD * w)
    return delta, nm, nv


def _row_tile(r):
    for cand in (256, 128, 64, 32, 16, 8):
        if r % cand == 0 and r > cand:
            return cand
    return r


def _adamw_whole(items, name):
    ni = len(items)

    def body(*refs):
        ins, outs = refs[:4 * ni], refs[4 * ni:]
        for k in range(ni):
            w_ref, p_ref, m_ref, v_ref = ins[4 * k:4 * k + 4]
            g = p_ref[0].astype(F32)
            for s in range(1, p_ref.shape[0]):
                g = g + p_ref[s].astype(F32)
            delta, nm, nv = _adamw_math(w_ref[...], g, m_ref[...], v_ref[...])
            g_ref, d_ref, nm_ref, nv_ref = outs[4 * k:4 * k + 4]
            g_ref[...], d_ref[...], nm_ref[...], nv_ref[...] = g, delta, nm, nv

    in_specs, out_specs, out_shape, operands = [], [], [], []
    for w, parts, m, v in items:
        in_specs += [_full(w.shape), _full(parts.shape), _full(w.shape), _full(w.shape)]
        operands += [w, parts, m, v]
        out_specs += [_full(w.shape)] * 4
        out_shape += [jax.ShapeDtypeStruct(w.shape, F32)] * 4
    outs = pl.pallas_call(body, name=name, grid=(1,), in_specs=in_specs, out_specs=out_specs, out_shape=out_shape,
                          compiler_params=_params("arbitrary"))(*operands)
    return [tuple(outs[4 * k:4 * k + 4]) for k in range(ni)]


def _adamw(items, layer0, bufs, name):
    ni = len(items)
    nl = items[0][1].shape[1]
    tiles = [_row_tile(w.shape[1]) for w, _, _, _ in items]
    steps = [w.shape[1] // tr for (w, _, _, _), tr in zip(items, tiles)]
    ns = steps[0]
    assert all(s == ns for s in steps)
    nb = 0 if bufs is None else 4 * ni

    def body(*refs):
        ins, outs = refs[:4 * ni], refs[4 * ni + nb:]
        for k in range(ni):
            w_ref, p_ref, m_ref, v_ref = ins[4 * k:4 * k + 4]
            g = p_ref[0, 0].astype(F32)
            for s in range(1, p_ref.shape[0]):
                g = g + p_ref[s, 0].astype(F32)
            delta, nm, nv = _adamw_math(w_ref[0], g, m_ref[0], v_ref[0])
            g_ref, d_ref, nm_ref, nv_ref = outs[4 * k:4 * k + 4]
            g_ref[0], d_ref[0], nm_ref[0], nv_ref[0] = g, delta, nm, nv

    in_specs, out_specs, out_shape, operands = [], [], [], []
    for (w, parts, m, v), tr in zip(items, tiles):
        blk = pl.BlockSpec((1, tr, w.shape[2]), lambda l, i: (layer0 + l, i, 0))
        in_specs += [blk, pl.BlockSpec((parts.shape[0], 1, tr, w.shape[2]), lambda l, i: (0, l, i, 0)), blk, blk]
        operands += [w, parts, m, v]
        out_specs += [blk] * 4
        out_shape += [jax.ShapeDtypeStruct(w.shape, F32)] * 4
    if bufs is not None:
        in_specs += [ANY] * nb
        operands += [b for item in bufs for b in item]
    outs = pl.pallas_call(
        body, name=name, grid=(nl, ns), in_specs=in_specs, out_specs=out_specs, out_shape=out_shape,
        input_output_aliases={4 * ni + i: i for i in range(nb)},
        compiler_params=_params("arbitrary", "arbitrary"),
    )(*operands)
    return [tuple(outs[4 * k:4 * k + 4]) for k in range(ni)]


NAMES = ("norm_even", "w_in_even", "conv_a_w", "conv_a_b", "ln_a_g", "ln_a_b", "pool_w", "pool_b", "pool_scale",
         "w_out_even", "norm_odd", "w_in_odd", "conv_c_w", "conv_c_b", "w_rg", "b_rg", "w_ig", "b_ig", "lru_lambda",
         "w_out_odd", "final_norm")
SMALL_GATHERED = ("conv_a_w", "pool_b", "norm_odd", "conv_c_w", "conv_c_b", "b_rg", "b_ig", "lru_lambda")
BIG = (("w_in_even", "w_out_even"), ("w_in_odd", "w_out_odd"))
SMALL = (("conv_a_w", "pool_b", "pool_w"), ("norm_odd", "conv_c_w", "conv_c_b", "b_rg", "b_ig", "lru_lambda"))
REPLICATED = (("norm_even", "conv_a_b", "ln_a_g", "ln_a_b", "pool_scale"), ("w_rg", "w_ig"))
PACK_ROW = 1024


def _pack_rows(flat2d):
    pad = (-flat2d.shape[1]) % PACK_ROW
    return jnp.pad(flat2d, ((0, 0), (0, pad))).reshape(flat2d.shape[0], -1, 128)


def _unpack(flat, shapes):
    out, off = [], 0
    for s in shapes:
        n = 1
        for d in s:
            n *= d
        out.append(flat[..., off:off + n].reshape(flat.shape[:-1] + tuple(s)))
        off += n
    return out


def _to_global(name, g):
    if name in ("conv_a_w", "pool_b", "conv_c_w"):
        return jnp.transpose(g, (1, 2, 0, 3)).reshape(g.shape[1], g.shape[2], -1)
    if name == "pool_w":
        return jnp.transpose(g, (1, 2, 0, 3, 4)).reshape(2, 4, POOL_GW, POOL_GW)
    return jnp.transpose(g, (1, 0, 2)).reshape(g.shape[1], -1)


def _to_blocks(name, g):
    if name == "conv_a_w":
        return jnp.transpose(g.reshape(CONV_K, N_DEV, -1), (1, 0, 2))
    if name == "conv_c_w":
        return jnp.transpose(g.reshape(LRU_CONV_K, N_DEV, -1), (1, 0, 2))
    if name == "pool_b":
        return jnp.transpose(g.reshape(4, N_DEV, -1), (1, 0, 2))
    if name == "pool_w":
        return jnp.transpose(g.reshape(4, N_DEV, POOL_GW // N_DEV, POOL_GW), (1, 0, 2, 3))
    return g.reshape(N_DEV, -1)


def _as3d(a):
    if a.ndim == 1:
        return a.reshape(1, 1, -1)
    if a.ndim == 2:
        return a.reshape(a.shape[0], 1, a.shape[1])
    return a.reshape(a.shape[0], -1, a.shape[-1])


def kernel(x, norm_even, w_in_even, conv_a_w, conv_a_b, ln_a_g, ln_a_b, pool_w, pool_b, pool_scale, w_out_even, norm_odd, w_in_odd, conv_c_w, conv_c_b, w_rg, b_rg, w_ig, b_ig, lru_lambda, w_out_odd, final_norm, loss_target, m_norm_even, m_w_in_even, m_conv_a_w, m_conv_a_b, m_ln_a_g, m_ln_a_b, m_pool_w, m_pool_b, m_pool_scale, m_w_out_even, m_norm_odd, m_w_in_odd, m_conv_c_w, m_conv_c_b, m_w_rg, m_b_rg, m_w_ig, m_b_ig, m_lru_lambda, m_w_out_odd, m_final_norm, v_norm_even, v_w_in_even, v_conv_a_w, v_conv_a_b, v_ln_a_g, v_ln_a_b, v_pool_w, v_pool_b, v_pool_scale, v_w_out_even, v_norm_odd, v_w_in_odd, v_conv_c_w, v_conv_c_b, v_w_rg, v_b_rg, v_w_ig, v_b_ig, v_lru_lambda, v_w_out_odd, v_final_norm):
    w_loc = dict(zip(NAMES, [norm_even, w_in_even, conv_a_w, conv_a_b, ln_a_g, ln_a_b, pool_w, pool_b, pool_scale,
                             w_out_even, norm_odd, w_in_odd, conv_c_w, conv_c_b, w_rg, b_rg, w_ig, b_ig, lru_lambda,
                             w_out_odd, final_norm]))
    m_loc = dict(zip(NAMES, [m_norm_even, m_w_in_even, m_conv_a_w, m_conv_a_b, m_ln_a_g, m_ln_a_b, m_pool_w, m_pool_b,
                             m_pool_scale, m_w_out_even, m_norm_odd, m_w_in_odd, m_conv_c_w, m_conv_c_b, m_w_rg,
                             m_b_rg, m_w_ig, m_b_ig, m_lru_lambda, m_w_out_odd, m_final_norm]))
    v_loc = dict(zip(NAMES, [v_norm_even, v_w_in_even, v_conv_a_w, v_conv_a_b, v_ln_a_g, v_ln_a_b, v_pool_w, v_pool_b,
                             v_pool_scale, v_w_out_even, v_norm_odd, v_w_in_odd, v_conv_c_w, v_conv_c_b, v_w_rg,
                             v_b_rg, v_w_ig, v_b_ig, v_lru_lambda, v_w_out_odd, v_final_norm]))
    me = _slot(lax.axis_index("x"), lax.axis_index("y"), lax.axis_index("c"))

    def landing(own):
        zone = lax.empty((N_DEV,) + own.shape[1:], own.dtype)
        return lax.dynamic_update_slice(zone, own, (me,) + (0,) * (own.ndim - 1))

    small_shapes = [w_loc[n].shape for n in SMALL_GATHERED]
    small = jnp.concatenate([w_loc[n].reshape(1, -1) for n in SMALL_GATHERED], axis=1)
    first = _all_gather([w_in_even[0].astype(BF16), pool_w.astype(BF16), _pack_rows(small)[0]], "gather_first")
    g_small = dict(zip(SMALL_GATHERED, [_to_global(n, g) for n, g in
                                        zip(SMALL_GATHERED, _unpack(first[2].reshape(N_DEV, -1), small_shapes))]))
    pool_w_all = _to_global("pool_w", first[1])

    def pairs_of(shards):
        return [(s.astype(BF16), landing(s.astype(BF16)[None])) for s in shards]

    shards = {1: [w_in_odd[0], w_out_odd[0]], 2: [w_in_even[1], w_out_even[1]], 3: [w_in_odd[1], w_out_odd[1]]}
    leg_a, leg_b = {}, {}
    (w_out_0, leg_a[1]), token_1 = _exchange_start(
        [("gather", pairs_of([w_out_even[0]])), ("chip_gather", pairs_of(shards[1]))], [first[0]], "gather_start_1")
    unused = jnp.zeros((8, 128), F32)

    def second_leg(layer, y):
        lands = _exchange_wait(leg_a[layer], [y], f"gather_wait_a_{layer}")
        (leg_b[layer],), token = _exchange_start([("forward", [(unused, l) for l in lands])], [],
                                                 f"gather_forward_{layer}")
        return token

    def layer_weights(layer, h):
        j = layer // 2
        dep = h
        if layer == 0:
            w_in, dep = first[0], token_1
            w_out_now = lambda y: _exchange_wait(w_out_0, [y], "gather_wait_out_0")[0]
        else:
            w_in, w_out_got = _exchange_wait(leg_b[layer], [h], f"gather_wait_b_{layer}")
            w_out_now = lambda y: w_out_got
            if layer + 1 in shards:
                (leg_a[layer + 1],), dep = _exchange_start([("chip_gather", pairs_of(shards[layer + 1]))], [w_in],
                                                           f"gather_start_{layer + 1}")

        def w_out(y):
            return w_out_now(y), (second_leg(layer + 1, y) if layer + 1 in shards else y)

        w_in = jnp.transpose(w_in, (1, 0, 2)).reshape(D_MODEL, -1)
        row = lambda a: a[j][None]
        if layer % 2 == 0:
            return dict(dep=dep, norm=row(norm_even), w_in=w_in,
                        w_out=lambda y: (lambda wo, d: (wo.reshape(W_EVEN_MIX, D_MODEL), d))(*w_out(y)),
                        conv_w=jnp.pad(g_small["conv_a_w"][j], ((0, 1), (0, 0))), conv_b=row(conv_a_b),
                        ln_g=row(ln_a_g), ln_b=row(ln_a_b), pool_w=pool_w_all[j],
                        pool_b=g_small["pool_b"][j].reshape(1, W_POOL), pool_scale=row(pool_scale))
        return dict(dep=dep, norm=row(g_small["norm_odd"]), w_in=w_in,
                    w_out=lambda y: (lambda wo, d: (wo.reshape(W_LRU, D_MODEL), d))(*w_out(y)),
                    conv_w=jnp.pad(g_small["conv_c_w"][j], ((0, 4), (0, 0))), conv_b=row(g_small["conv_c_b"]),
                    w_rg=w_rg[j].astype(BF16), b_rg=row(g_small["b_rg"]), w_ig=w_ig[j].astype(BF16),
                    b_ig=row(g_small["b_ig"]), lam=row(g_small["lru_lambda"]))

    pending, exchanges, small_layout = {}, {}, {}
    last_token, pair_leg = [], []

    def on_grads(layer, grads, dep, last):
        par = layer % 2
        w_in_name, w_out_name = BIG[par]
        have = pending.setdefault(layer, {})
        have.update(grads)
        eager = layer == 0
        scatter, gather = {}, {}
        if w_out_name in have and (eager or last):
            scatter["out"] = have.pop(w_out_name).reshape(N_DEV, -1, D_MODEL)
        pair, chip = {}, {}
        if w_in_name in have and eager:
            pair["in"] = have.pop(w_in_name)
        elif w_in_name in have and last:
            scatter["in"] = have.pop(w_in_name)
        deps = []
        if pair:
            whole = pair["in"]
            deps = [lax.dynamic_index_in_dim(whole.reshape((4, 2) + whole.shape[1:]), lax.axis_index("c"), 1,
                                             keepdims=False)]
        if eager and last:
            mine, theirs = pair_leg.pop()
            chip["in"] = _pair_sum(mine, _exchange_wait(theirs, list(grads.values()), "grads_wait_pair")[0])
        if layer not in small_layout and all(n in have for n in SMALL[par]) and (eager or last):
            blocks = [_to_blocks(n, have[n]) for n in SMALL[par]]
            small_layout[layer] = [b.shape[1:] for b in blocks]
            scatter["small"] = _pack_rows(jnp.concatenate([b.reshape(N_DEV, -1) for b in blocks], axis=1))
        for n in REPLICATED[1]:
            if n in have:
                gather[n] = have.pop(n).astype(BF16).reshape(-1, LRU_HD)
        if last:
            vectors = [have[n] for n in REPLICATED[par]] if par == 0 else []
            if "final_norm" in have:
                vectors += [have["final_norm"], jnp.pad(have["loss"].reshape(-1), (0, 127))]
            if vectors:
                gather["rep32"] = jnp.concatenate([v.reshape(-1) for v in vectors]).reshape(-1, 128)
        my_chip = 2 * lax.axis_index("x") + lax.axis_index("y")
        groups, keys = [], []
        for mode, arrays, pairs in (
                ("scatter", scatter, [(s, landing(lax.dynamic_slice_in_dim(s, me, 1, 0))) for s in scatter.values()]),
                ("gather", gather, [(s, landing(s[None])) for s in gather.values()]),
                ("pair_scatter", pair, [(s, lax.empty((4,) + s.shape[1:], s.dtype)) for s in pair.values()]),
                ("chip_scatter", chip, [(s, lax.dynamic_update_slice(
                    lax.empty(s.shape, s.dtype), lax.dynamic_slice_in_dim(s, my_chip, 1, 0),
                    (my_chip,) + (0,) * (s.ndim - 1))) for s in chip.values()])):
            if arrays:
                groups.append((mode, pairs))
                keys.append(list(arrays))
        if not groups:
            return dep
        handles, token = _exchange_start(groups, deps, f"grads_start_{layer}_{'_'.join(k for ks in keys for k in ks)}")
        if pair:
            pair_leg.append((deps[0], handles.pop()))
            keys.pop()
        exchanges.setdefault(layer, []).extend(zip(keys, handles))
        last_token[:] = [token]
        return token

    loss, grad_x = _local_step(x[0], loss_target[0], layer_weights, final_norm[None], on_grads)

    w3 = {n: _as3d(w_loc[n]) for n in NAMES}
    m3 = {n: _as3d(m_loc[n]) for n in NAMES}
    v3 = {n: _as3d(v_loc[n]) for n in NAMES}
    results, small_parts = {}, {}
    after = list(last_token)
    for layer in (3, 2, 1, 0):
        par, j = layer % 2, layer // 2
        got = {}
        for keys, handle in exchanges[layer]:
            got.update(zip(keys, _exchange_wait(handle, after, f"grads_wait_{layer}_{'_'.join(keys)}")))
        parts = {BIG[par][0]: got["in"], BIG[par][1]: got["out"]}
        parts.update(zip(SMALL[par], _unpack(got["small"].reshape(N_DEV, -1), small_layout[layer])))
        if par == 1:
            parts.update({n: got[n] for n in REPLICATED[1]})
        else:
            parts.update(zip(REPLICATED[0], _unpack(got["rep32"].reshape(N_DEV, -1)[:, :len(REPLICATED[0]) * D_MODEL],
                                                    [w_loc[n].shape[1:] for n in REPLICATED[0]])))
        small_parts[layer] = {}
        for n, pt in parts.items():
            if w3[n].shape[1] >= 128:
                pt = pt.reshape((pt.shape[0], 1) + w3[n].shape[1:])
                results[n] = _adamw([(w3[n], pt, m3[n], v3[n])], j, [results[n]] if n in results else None,
                                    f"adamw_{n}_{layer}")[0]
            else:
                small_parts[layer][n] = pt.reshape((pt.shape[0],) + w_loc[n].shape[1:])
        if layer + 2 in small_parts:
            snames = list(small_parts[layer])
            items = [(w_loc[n], jnp.stack([small_parts[layer][n], small_parts[layer + 2][n]], axis=1), m_loc[n],
                      v_loc[n]) for n in snames]
            for n, r in zip(snames, _adamw_whole(items, f"adamw_small_{par}")):
                results[n] = r
        if layer == N_LAYERS - 1:
            tail = got["rep32"].reshape(N_DEV, -1)
            item = (w3["final_norm"], tail[:, :D_MODEL].reshape(N_DEV, 1, 1, D_MODEL), m3["final_norm"],
                    v3["final_norm"])
            results["final_norm"] = _adamw([item], 0, None, "adamw_final_norm")[0]
            total = jnp.sum(tail[:, D_MODEL])
        after = [results[BIG[par][0]][1]]

    outs = [[results[n][k].reshape(w_loc[n].shape) for n in NAMES] for k in range(4)]
    return (total, grad_x[None], *outs[0], *outs[1], *outs[2], *outs[3])
```

```python
import jax
import jax.numpy as jnp
from jax import lax
from jax.experimental import pallas as pl
from jax.experimental.pallas import tpu as pltpu

F32 = jnp.float32
BF16 = jnp.bfloat16

N_DEV = 8
N_PEERS = N_DEV - 1
N_LAYERS = 4
D_MODEL = 1024
EPS_RMS = 1e-6
EPS_LN = 1e-5
W_CONV = 1024
CONV_K = 31
W_POOL = 1024
POOL_WINDOWS = (2, 4, 8, 16)
POOL_GW = 256
W_EVEN_IN = 5120
W_EVEN_MIX = 2048
LRU_HEADS = 12
LRU_HD = 128
W_LRU = 1536
LRU_CONV_K = 4
LRU_C = 8.0
ADAM_LR = 0.001
ADAM_B1 = 0.9
ADAM_B2 = 0.999
ADAM_EPS = 1e-08
ADAM_WD = 0.01
ADAM_STEP = 10

HALO = 32
HALO_C = 8
TM_MATMUL = 512
TM_STREAM = 1024
TM_WGRAD = 2048
TM_MIXER = 256
CONV_ROWS = 128
MIX_ROWS = 32
VMEM_LIMIT = 56 * 1024 * 1024
MESH = pl.DeviceIdType.MESH
ANY = pl.BlockSpec(memory_space=pl.ANY)
HBM = pl.BlockSpec(memory_space=pltpu.HBM)
SEM = pl.BlockSpec(memory_space=pltpu.SEMAPHORE)


def _params(*sem):
    return pltpu.CompilerParams(dimension_semantics=sem, vmem_limit_bytes=VMEM_LIMIT)


def _sigmoid(z):
    return 0.5 * jnp.tanh(0.5 * z) + 0.5


def _dsilu(z, s):
    return s * (1.0 + z * (1.0 - s))


def _full(shape):
    nd = len(shape)
    return pl.BlockSpec(shape, lambda *_: (0,) * nd)


def _norm_matmul(h, g, w, dep, *, tm):
    t, d = h.shape
    n = w.shape[1]
    tn = n // 4

    def body(h_ref, g_ref, w_ref, dep_ref, p_ref, hn_ref):
        @pl.when(pl.program_id(1) == 0)
        def _():
            x = h_ref[...]
            r = lax.rsqrt(jnp.mean(x * x, axis=-1, keepdims=True) + EPS_RMS)
            hn_ref[...] = ((x * r) * g_ref[...]).astype(BF16)

        p_ref[...] = jnp.dot(hn_ref[...], w_ref[...], preferred_element_type=F32)

    return pl.pallas_call(
        body, name="norm_matmul", grid=(t // tm, n // tn),
        in_specs=[pl.BlockSpec((tm, d), lambda i, j: (i, 0)), _full((1, d)),
                  pl.BlockSpec((d, tn), lambda i, j: (0, j)), ANY],
        out_specs=[pl.BlockSpec((tm, tn), lambda i, j: (i, j)), pl.BlockSpec((tm, d), lambda i, j: (i, 0))],
        out_shape=[jax.ShapeDtypeStruct((t, n), F32), jax.ShapeDtypeStruct((t, d), BF16)],
        compiler_params=_params("arbitrary", "arbitrary"),
    )(h, g, w, dep)


def _out_proj(h, y, w, dep, *, tm):
    t, d = h.shape
    k = y.shape[1]

    def body(h_ref, y_ref, w_ref, dep_ref, o_ref):
        o_ref[...] = h_ref[...] + jnp.dot(y_ref[...], w_ref[...], preferred_element_type=F32)

    return pl.pallas_call(
        body, name="out_proj", grid=(t // tm,),
        in_specs=[pl.BlockSpec((tm, d), lambda i: (i, 0)), pl.BlockSpec((tm, k), lambda i: (i, 0)), _full((k, d)),
                  ANY],
        out_specs=pl.BlockSpec((tm, d), lambda i: (i, 0)),
        out_shape=jax.ShapeDtypeStruct((t, d), F32),
        compiler_params=_params("arbitrary"),
    )(h, y, w, dep)


def _out_proj_bwd(dh, y, w, dep, *, tm):
    t, d = dh.shape
    k = y.shape[1]
    nt = t // tm

    def body(dh_ref, y_ref, w_ref, dep_ref, dy_ref, dw_ref, acc):
        i = pl.program_id(0)
        g = dh_ref[...].astype(BF16)
        dy_ref[...] = lax.dot_general(g, w_ref[...], (((1,), (1,)), ((), ())), preferred_element_type=F32)
        part = lax.dot_general(y_ref[...], g, (((0,), (0,)), ((), ())), preferred_element_type=F32)

        @pl.when(i == 0)
        def _():
            acc[...] = part

        @pl.when(i > 0)
        def _():
            acc[...] += part

        @pl.when(i == nt - 1)
        def _():
            dw_ref[...] = acc[...].astype(BF16)

    return pl.pallas_call(
        body, name="out_proj_bwd", grid=(nt,),
        in_specs=[pl.BlockSpec((tm, d), lambda i: (i, 0)), pl.BlockSpec((tm, k), lambda i: (i, 0)), _full((k, d)),
                  ANY],
        out_specs=[pl.BlockSpec((tm, k), lambda i: (i, 0)), _full((k, d))],
        out_shape=[jax.ShapeDtypeStruct((t, k), F32), jax.ShapeDtypeStruct((k, d), BF16)],
        scratch_shapes=[pltpu.VMEM((k, d), F32)],
        compiler_params=_params("arbitrary"),
    )(dh, y, w, dep)


def _in_proj_bwd_x(dp, w, h, g, dh_out, dep, *, tm):
    t, d = h.shape
    n = w.shape[1]
    nt = t // tm

    def body(dp_ref, w_ref, h_ref, g_ref, dho_ref, dep_ref, dh_ref, dg_ref):
        i = pl.program_id(0)
        dy = lax.dot_general(dp_ref[...], w_ref[...], (((1,), (1,)), ((), ())), preferred_element_type=F32)
        x = h_ref[...]
        r = lax.rsqrt(jnp.mean(x * x, axis=-1, keepdims=True) + EPS_RMS)
        gd = dy * g_ref[...]
        m = jnp.mean(gd * x, axis=-1, keepdims=True)
        dh_ref[...] = dho_ref[...] + r * gd - x * (r * r * r * m)
        _acc_out(i, dg_ref, jnp.sum(dy * x * r, axis=0, keepdims=True))

    return pl.pallas_call(
        body, name="in_proj_bwd_x", grid=(nt,),
        in_specs=[pl.BlockSpec((tm, n), lambda i: (i, 0)), _full((d, n)),
                  pl.BlockSpec((tm, d), lambda i: (i, 0)), _full((1, d)), pl.BlockSpec((tm, d), lambda i: (i, 0)),
                  ANY],
        out_specs=[pl.BlockSpec((tm, d), lambda i: (i, 0)), _full((1, d))],
        out_shape=[jax.ShapeDtypeStruct((t, d), F32), jax.ShapeDtypeStruct((1, d), F32)],
        compiler_params=_params("arbitrary"),
    )(dp, w, h, g, dh_out, dep)


def _in_proj_bwd_w(hn, dp, nd, *, tm):
    t, d = hn.shape
    nb = dp.shape[1] // nd
    nt = t // tm

    def body(hn_ref, dp_ref, dw_ref, acc):
        i = pl.program_id(1)
        part = lax.dot_general(hn_ref[...], dp_ref[...], (((0,), (0,)), ((), ())), preferred_element_type=F32)

        @pl.when(i == 0)
        def _():
            acc[...] = part

        @pl.when(i > 0)
        def _():
            acc[...] += part

        @pl.when(i == nt - 1)
        def _():
            dw_ref[0] = acc[...].astype(BF16)

    return pl.pallas_call(
        body, name="in_proj_bwd_w", grid=(nd, nt),
        in_specs=[pl.BlockSpec((tm, d), lambda j, i: (i, 0)), pl.BlockSpec((tm, nb), lambda j, i: (i, j))],
        out_specs=pl.BlockSpec((1, d, nb), lambda j, i: (j, 0, 0)),
        out_shape=jax.ShapeDtypeStruct((nd, d, nb), BF16),
        scratch_shapes=[pltpu.VMEM((d, nb), F32)],
        compiler_params=_params("arbitrary", "arbitrary"),
    )(hn, dp)


def _loss_head(h, g, target, *, tm):
    t, d = h.shape
    nt = t // tm

    def body(h_ref, g_ref, t_ref, loss_ref, dh_ref, dg_ref):
        i = pl.program_id(0)
        x = h_ref[...]
        r = lax.rsqrt(jnp.mean(x * x, axis=-1, keepdims=True) + EPS_RMS)
        xr = x * r
        err = xr * g_ref[...] - t_ref[...]
        lp = 0.5 * jnp.sum(jnp.mean(err * err, axis=-1, keepdims=True), axis=0, keepdims=True)
        dy = err * (1.0 / d)
        gd = dy * g_ref[...]
        m = jnp.mean(gd * x, axis=-1, keepdims=True)
        dh_ref[...] = r * gd - x * (r * r * r * m)
        dgp = jnp.sum(dy * xr, axis=0, keepdims=True)

        @pl.when(i == 0)
        def _():
            loss_ref[...] = lp
            dg_ref[...] = dgp

        @pl.when(i > 0)
        def _():
            loss_ref[...] += lp
            dg_ref[...] += dgp

    return pl.pallas_call(
        body, name="loss_head", grid=(nt,),
        in_specs=[pl.BlockSpec((tm, d), lambda i: (i, 0)), _full((1, d)), pl.BlockSpec((tm, d), lambda i: (i, 0))],
        out_specs=[_full((1, 1)), pl.BlockSpec((tm, d), lambda i: (i, 0)), _full((1, d))],
        out_shape=[jax.ShapeDtypeStruct((1, 1), F32), jax.ShapeDtypeStruct((t, d), F32),
                   jax.ShapeDtypeStruct((1, d), F32)],
        compiler_params=_params("arbitrary"),
    )(h, g, target)


def _col(tm, w, c):
    return pl.BlockSpec((tm, w), lambda i: (i, c))


def _prev_halo(tm, rows, w, c):
    per = tm // rows
    return pl.BlockSpec((rows, w), lambda i: (jnp.maximum(i * per - 1, 0), c))


def _next_halo(tm, rows, w, c, t):
    per = tm // rows
    last = t // rows - 1
    return pl.BlockSpec((rows, w), lambda i: (jnp.minimum((i + 1) * per, last), c))


def _inv_count(first_row, rows, window):
    tpos = first_row + lax.broadcasted_iota(jnp.int32, (rows, 1), 0)
    return 1.0 / jnp.minimum(tpos + 1, window).astype(F32)


def _even_fwd(p, cw, cb, lg, lb, pw, pb, ps, *, tm):
    t = p.shape[0]
    wc = W_CONV

    def body(av, ag, agate, bv, bgate, avh, agh, bvh, cw_ref, cb_ref, lg_ref, lb_ref, pw_ref, pb_ref, ps_ref,
             y_ref, u1_ref, e_ref, d_ref, uext, vext):
        i = pl.program_id(0)
        keep = (i > 0).astype(F32)
        uext[0:HALO, :] = keep * (avh[...] * _sigmoid(agh[...]))
        uext[HALO:, :] = av[...] * _sigmoid(ag[...])
        vext[0:HALO, :] = keep * bvh[...]
        vext[HALO:, :] = bv[...]
        for c in range(0, wc, 128):
            for rb in range(0, tm, CONV_ROWS):
                acc = jnp.broadcast_to(cb_ref[:, c:c + 128], (CONV_ROWS, 128))
                for r in range(8):
                    shifted = uext[pl.ds(rb + 8 - r, CONV_ROWS + HALO - 8), c:c + 128]
                    for q in range(HALO // 8):
                        s = 8 * q + r
                        if s < CONV_K:
                            acc = acc + cw_ref[CONV_K - 1 - s:CONV_K - s, c:c + 128] * shifted[24 - 8 * q:24 - 8 * q + CONV_ROWS]
                u1_ref[rb:rb + CONV_ROWS, c:c + 128] = acc
        u1 = u1_ref[...]
        mu = jnp.mean(u1, axis=-1, keepdims=True)
        xc = u1 - mu
        rs = lax.rsqrt(jnp.mean(xc * xc, axis=-1, keepdims=True) + EPS_LN)
        u2 = (xc * rs) * lg_ref[...] + lb_ref[...]
        u3 = u2 * _sigmoid(u2)
        ga = agate[...]
        y_ref[:, 0:wc] = (u3 * (ga * _sigmoid(ga))).astype(BF16)
        for g, win in enumerate(POOL_WINDOWS):
            cs = slice(g * POOL_GW, (g + 1) * POOL_GW)
            s = vext[pl.ds(HALO, tm), cs]
            for j in range(1, win):
                s = s + vext[pl.ds(HALO - j, tm), cs]
            dg = s * _inv_count(i * tm, tm, win) - vext[pl.ds(HALO, tm), cs]
            dgb = dg.astype(BF16)
            d_ref[:, cs] = dgb
            eg = jnp.dot(dgb, pw_ref[g], preferred_element_type=F32) + pb_ref[:, cs]
            e_ref[:, cs] = eg
            gb = bgate[:, cs]
            y_ref[:, wc + g * POOL_GW:wc + (g + 1) * POOL_GW] = ((eg * ps_ref[:, cs]) * (gb * _sigmoid(gb))).astype(BF16)

    row = lambda w: pl.BlockSpec((tm, w), lambda i: (i, 0))
    return pl.pallas_call(
        body, name="even_fwd", grid=(t // tm,),
        in_specs=[_col(tm, wc, 0), _col(tm, wc, 1), _col(tm, wc, 2), _col(tm, wc, 3), _col(tm, wc, 4),
                  _prev_halo(tm, HALO, wc, 0), _prev_halo(tm, HALO, wc, 1), _prev_halo(tm, HALO, wc, 3),
                  _full((32, wc)), _full((1, wc)), _full((1, wc)), _full((1, wc)),
                  _full((4, POOL_GW, POOL_GW)), _full((1, wc)), _full((1, wc))],
        out_specs=[row(2 * wc), row(wc), row(wc), row(wc)],
        out_shape=[jax.ShapeDtypeStruct((t, 2 * wc), BF16), jax.ShapeDtypeStruct((t, wc), F32),
                   jax.ShapeDtypeStruct((t, wc), F32), jax.ShapeDtypeStruct((t, wc), BF16)],
        scratch_shapes=[pltpu.VMEM((tm + HALO, wc), F32), pltpu.VMEM((tm + HALO, wc), F32)],
        compiler_params=_params("arbitrary"),
    )(p, p, p, p, p, p, p, p, cw, cb, lg, lb, pw, pb, ps)


def _acc_out(i, ref, val):
    @pl.when(i == 0)
    def _():
        ref[...] = val

    @pl.when(i > 0)
    def _():
        ref[...] += val


def _even_bwd_a(p, u1, e, dmat, dy, lg, lb, pw, ps, dep, *, tm):
    t = p.shape[0]
    wc = W_CONV

    def body(agate, bgate, u1_ref, e_ref, d_ref, dya, dyb, lg_ref, lb_ref, pw_ref, ps_ref, dep_ref,
             du1_ref, dd_ref, dgat_ref, dlg_ref, dlb_ref, dpb_ref, dps_ref, dpw_ref, de_s):
        i = pl.program_id(0)

        @pl.when(i == 0)
        def _():
            for ref in (dpw_ref, dlg_ref, dlb_ref, dpb_ref, dps_ref):
                ref[...] = jnp.zeros_like(ref)

        def rows(k, carry):
            rs_ = pl.ds(pl.multiple_of(k * MIX_ROWS, MIX_ROWS), MIX_ROWS)
            u1 = u1_ref[rs_, :]
            mu = jnp.mean(u1, axis=-1, keepdims=True)
            xc = u1 - mu
            rs = lax.rsqrt(jnp.mean(xc * xc, axis=-1, keepdims=True) + EPS_LN)
            xh = xc * rs
            u2 = xh * lg_ref[...] + lb_ref[...]
            s2 = _sigmoid(u2)
            ga = agate[rs_, :]
            sa = _sigmoid(ga)
            dy_a = dya[rs_, :]
            dgat_ref[rs_, 0:wc] = (dy_a * (u2 * s2) * _dsilu(ga, sa)).astype(BF16)
            du2 = dy_a * (ga * sa) * _dsilu(u2, s2)
            dlg_ref[...] += jnp.sum(du2 * xh, axis=0, keepdims=True)
            dlb_ref[...] += jnp.sum(du2, axis=0, keepdims=True)
            dxh = du2 * lg_ref[...]
            m1 = jnp.mean(dxh, axis=-1, keepdims=True)
            m2 = jnp.mean(dxh * xh, axis=-1, keepdims=True)
            du1_ref[rs_, :] = rs * (dxh - m1 - xh * m2)

            gb = bgate[rs_, :]
            sb = _sigmoid(gb)
            ev = e_ref[rs_, :]
            dy_b = dyb[rs_, :]
            dgat_ref[rs_, wc:2 * wc] = (dy_b * (ev * ps_ref[...]) * _dsilu(gb, sb)).astype(BF16)
            dz = dy_b * (gb * sb)
            dps_ref[...] += jnp.sum(dz * ev, axis=0, keepdims=True)
            de = dz * ps_ref[...]
            dpb_ref[...] += jnp.sum(de, axis=0, keepdims=True)
            de_s[rs_, :] = de.astype(BF16)
            return carry

        lax.fori_loop(0, tm // MIX_ROWS, rows, 0)
        for g in range(len(POOL_WINDOWS)):
            cs = slice(g * POOL_GW, (g + 1) * POOL_GW)
            deg = de_s[:, cs]
            dd_ref[:, cs] = lax.dot_general(deg, pw_ref[g], (((1,), (1,)), ((), ())), preferred_element_type=F32)
            dpw_ref[g] += lax.dot_general(d_ref[:, cs], deg, (((0,), (0,)), ((), ())), preferred_element_type=F32)

    row = lambda w: pl.BlockSpec((tm, w), lambda i: (i, 0))
    return pl.pallas_call(
        body, name="even_bwd_a", grid=(t // tm,),
        in_specs=[_col(tm, wc, 2), _col(tm, wc, 4), row(wc), row(wc), row(wc), _col(tm, wc, 0), _col(tm, wc, 1),
                  _full((1, wc)), _full((1, wc)), _full((4, POOL_GW, POOL_GW)), _full((1, wc)), ANY],
        out_specs=[row(wc), row(wc), row(2 * wc), _full((1, wc)), _full((1, wc)), _full((1, wc)), _full((1, wc)),
                   _full((4, POOL_GW, POOL_GW))],
        out_shape=[jax.ShapeDtypeStruct((t, wc), F32), jax.ShapeDtypeStruct((t, wc), F32),
                   jax.ShapeDtypeStruct((t, 2 * wc), BF16)] + [jax.ShapeDtypeStruct((1, wc), F32)] * 4
                  + [jax.ShapeDtypeStruct((4, POOL_GW, POOL_GW), F32)],
        scratch_shapes=[pltpu.VMEM((tm, wc), BF16)],
        compiler_params=_params("arbitrary"),
    )(p, p, u1, e, dmat, dy, dy, lg, lb, pw, ps, dep)


def _even_bwd_b(p, du1, dd, dgat, cw, *, tm):
    t = p.shape[0]
    wc = W_CONV
    nt = t // tm

    def body(av, ag, avh, agh, du1_ref, du1n, dd_ref, ddn, dgat_ref, cw_ref, dp_ref, dcw_ref, dcb_ref,
             uext, gext, dext, du0, dcw8):
        i = pl.program_id(0)
        keep_p = (i > 0).astype(F32)
        keep_n = (i < nt - 1).astype(F32)

        @pl.when(i == 0)
        def _():
            dcw8[...] = jnp.zeros_like(dcw8)

        a = av[...]
        sg = _sigmoid(ag[...])
        uext[0:HALO, :] = keep_p * (avh[...] * _sigmoid(agh[...]))
        uext[HALO:, :] = a * sg
        gext[0:tm, :] = du1_ref[...]
        gext[tm:, :] = keep_n * du1n[...]
        for c in range(0, wc, 128):
            for rb in range(0, tm, CONV_ROWS):
                acc = jnp.zeros((CONV_ROWS, 128), F32)
                for r in range(8):
                    ahead = gext[pl.ds(rb + r, CONV_ROWS + HALO - 8), c:c + 128]
                    for q in range(HALO // 8):
                        s = 8 * q + r
                        if s < CONV_K:
                            acc = acc + cw_ref[CONV_K - 1 - s:CONV_K - s, c:c + 128] * ahead[8 * q:8 * q + CONV_ROWS]
                du0[rb:rb + CONV_ROWS, c:c + 128] = acc
                gcur = du1_ref[rb:rb + CONV_ROWS, c:c + 128]
                for r in range(8):
                    behind = uext[pl.ds(rb + 8 - r, CONV_ROWS + HALO - 8), c:c + 128]
                    for q in range(HALO // 8):
                        s = 8 * q + r
                        if s < CONV_K:
                            prod = gcur * behind[24 - 8 * q:24 - 8 * q + CONV_ROWS]
                            part = prod[0:8]
                            for o in range(8, CONV_ROWS, 8):
                                part = part + prod[o:o + 8]
                            k = CONV_K - 1 - s
                            dcw8[8 * k:8 * k + 8, c:c + 128] += part

        @pl.when(i == nt - 1)
        def _():
            for k in range(CONV_K):
                dcw_ref[k:k + 1, :] = jnp.sum(dcw8[8 * k:8 * k + 8, :], axis=0, keepdims=True)
            dcw_ref[CONV_K:32, :] = jnp.zeros((32 - CONV_K, wc), F32)

        _acc_out(i, dcb_ref, jnp.sum(du1_ref[...], axis=0, keepdims=True))
        g0 = du0[...]
        dp_ref[:, 0:wc] = (g0 * sg).astype(BF16)
        dp_ref[:, wc:2 * wc] = (g0 * a * sg * (1.0 - sg)).astype(BF16)
        dp_ref[:, 2 * wc:3 * wc] = dgat_ref[:, 0:wc]
        dp_ref[:, 4 * wc:5 * wc] = dgat_ref[:, wc:2 * wc]
        for g, win in enumerate(POOL_WINDOWS):
            cs = slice(g * POOL_GW, (g + 1) * POOL_GW)
            dext[0:tm, cs] = dd_ref[:, cs] * _inv_count(i * tm, tm, win)
            dext[tm:, cs] = keep_n * (ddn[:, cs] * _inv_count((i + 1) * tm, HALO, win))
            s = dext[pl.ds(0, tm), cs]
            for j in range(1, win):
                s = s + dext[pl.ds(j, tm), cs]
            dp_ref[:, 3 * wc + g * POOL_GW:3 * wc + (g + 1) * POOL_GW] = (s - dd_ref[:, cs]).astype(BF16)

    row = lambda w: pl.BlockSpec((tm, w), lambda i: (i, 0))
    return pl.pallas_call(
        body, name="even_bwd_b", grid=(nt,),
        in_specs=[_col(tm, wc, 0), _col(tm, wc, 1), _prev_halo(tm, HALO, wc, 0), _prev_halo(tm, HALO, wc, 1),
                  row(wc), _next_halo(tm, HALO, wc, 0, t), row(wc), _next_halo(tm, HALO, wc, 0, t), row(2 * wc),
                  _full((32, wc))],
        out_specs=[row(5 * wc), _full((32, wc)), _full((1, wc))],
        out_shape=[jax.ShapeDtypeStruct((t, 5 * wc), BF16), jax.ShapeDtypeStruct((32, wc), F32),
                   jax.ShapeDtypeStruct((1, wc), F32)],
        scratch_shapes=[pltpu.VMEM((tm + HALO, wc), F32), pltpu.VMEM((tm + HALO, wc), F32),
                        pltpu.VMEM((tm + HALO, wc), F32), pltpu.VMEM((tm, wc), F32), pltpu.VMEM((8 * 32, wc), F32)],
        compiler_params=_params("arbitrary"),
    )(p, p, p, p, du1, du1, dd, dd, dgat, cw)


def _softplus_neg(lam):
    z = -lam
    return jnp.maximum(z, 0.0) + jnp.log1p(jnp.exp(-jnp.abs(z)))


def _one_minus_exp(x):
    series = -x * (1.0 + x * (0.5 + x * (1.0 / 6.0 + x * (1.0 / 24.0))))
    return jnp.where(x > -0.02, series, 1.0 - jnp.exp(x))


def _odd_fwd(p, ccw, ccb, wrg, brg, wig, big, lam, *, tm):
    t = p.shape[0]
    wl = W_LRU
    ng = tm // 8

    def body(xr, gate, xrh, ccw_ref, ccb_ref, wrg_ref, brg_ref, wig_ref, big_ref, lam_ref,
             y_ref, xc_ref, r_ref, i_ref, hs_ref, xext, a_s, b_s, carry):
        i = pl.program_id(0)
        keep = (i > 0).astype(F32)
        xext[0:HALO_C, :] = keep * xrh[...]
        xext[HALO_C:, :] = xr[...]
        xc = jnp.broadcast_to(ccb_ref[...], (tm, wl))
        for k in range(LRU_CONV_K):
            xc = xc + ccw_ref[k:k + 1, :] * xext[pl.ds(HALO_C - (LRU_CONV_K - 1) + k, tm), :]
        xc_ref[...] = xc
        for h in range(LRU_HEADS):
            cs = slice(h * LRU_HD, (h + 1) * LRU_HD)
            xh = xc_ref[:, cs].astype(BF16)
            r_ref[:, cs] = _sigmoid(jnp.dot(xh, wrg_ref[h], preferred_element_type=F32) + brg_ref[:, cs])
            i_ref[:, cs] = _sigmoid(jnp.dot(xh, wig_ref[h], preferred_element_type=F32) + big_ref[:, cs])
        log_a = (-LRU_C * _softplus_neg(lam_ref[...])) * r_ref[...]
        a_s[...] = jnp.exp(log_a)
        b_s[...] = jnp.sqrt(_one_minus_exp(2.0 * log_a)) * (i_ref[...] * xc_ref[...])

        @pl.when(i == 0)
        def _():
            carry[...] = jnp.zeros_like(carry)

        rowi = lax.broadcasted_iota(jnp.int32, (8, wl), 0)

        def step(g, c):
            sl = pl.ds(pl.multiple_of(g * 8, 8), 8)
            aa, bb = a_s[sl, :], b_s[sl, :]
            for s in (1, 2, 4):
                m = rowi >= s
                a_sh = jnp.where(m, pltpu.roll(aa, s, 0), 1.0)
                b_sh = jnp.where(m, pltpu.roll(bb, s, 0), 0.0)
                bb = aa * b_sh + bb
                aa = aa * a_sh
            hv = bb + aa * c
            hs_ref[sl, :] = hv
            return hv[7:8, :]

        carry[...] = lax.fori_loop(0, ng, step, carry[...])
        gt = gate[...]
        y_ref[...] = (hs_ref[...] * (gt * _sigmoid(gt))).astype(BF16)

    row = lambda w: pl.BlockSpec((tm, w), lambda i: (i, 0))
    return pl.pallas_call(
        body, name="odd_fwd", grid=(t // tm,),
        in_specs=[_col(tm, wl, 0), _col(tm, wl, 1), _prev_halo(tm, HALO_C, wl, 0), _full((8, wl)), _full((1, wl)),
                  _full((LRU_HEADS, LRU_HD, LRU_HD)), _full((1, wl)), _full((LRU_HEADS, LRU_HD, LRU_HD)),
                  _full((1, wl)), _full((1, wl))],
        out_specs=[row(wl)] * 5,
        out_shape=[jax.ShapeDtypeStruct((t, wl), BF16)] + [jax.ShapeDtypeStruct((t, wl), F32)] * 4,
        scratch_shapes=[pltpu.VMEM((tm + HALO_C, wl), F32), pltpu.VMEM((tm, wl), F32), pltpu.VMEM((tm, wl), F32),
                        pltpu.VMEM((1, wl), F32)],
        compiler_params=_params("arbitrary"),
    )(p, p, p, ccw, ccb, wrg, brg, wig, big, lam)


def _odd_bwd_a(p, xc, r, ig, hs, dy, wrg, wig, lam, dep, *, tm):
    t = p.shape[0]
    wl = W_LRU
    nt = t // tm
    ng = tm // 8
    per = tm // HALO_C

    def body(gate, xc_ref, r_ref, i_ref, hs_ref, hsh, dy_ref, wrg_ref, wig_ref, lam_ref, dep_ref,
             dxc_ref, dgate_ref, dwrg_ref, dwig_ref, dbrg_ref, dbig_ref, dlam_ref,
             hext, a_s, q_s, g_s, dpr_s, dpi_s, carry):
        i = pl.program_id(0)
        ti = nt - 1 - i
        keep = (ti > 0).astype(F32)
        hext[0:HALO_C, :] = keep * hsh[...]
        hext[HALO_C:, :] = hs_ref[...]
        gt = gate[...]
        sg = _sigmoid(gt)
        dyv = dy_ref[...]
        dgate_ref[...] = (dyv * hs_ref[...] * _dsilu(gt, sg)).astype(BF16)
        q_s[...] = dyv * (gt * sg)
        sp = _softplus_neg(lam_ref[...])
        log_a = (-LRU_C * sp) * r_ref[...]
        a_s[...] = jnp.exp(log_a)

        @pl.when(i == 0)
        def _():
            carry[...] = jnp.zeros_like(carry)
            dwrg_ref[...] = jnp.zeros_like(dwrg_ref)
            dwig_ref[...] = jnp.zeros_like(dwig_ref)

        rowi = lax.broadcasted_iota(jnp.int32, (8, wl), 0)

        def step(gr, c):
            sl = pl.ds(pl.multiple_of((ng - 1 - gr) * 8, 8), 8)
            a0 = a_s[sl, :]
            al = jnp.where(rowi < 7, pltpu.roll(a0, 7, 0), 1.0)
            be = q_s[sl, :]
            for s in (1, 2, 4):
                m = rowi + s <= 7
                al_sh = jnp.where(m, pltpu.roll(al, 8 - s, 0), 1.0)
                be_sh = jnp.where(m, pltpu.roll(be, 8 - s, 0), 0.0)
                be = be + al * be_sh
                al = al * al_sh
            gv = be + al * c
            g_s[sl, :] = gv
            return (a0 * gv)[0:1, :]

        carry[...] = lax.fori_loop(0, ng, step, carry[...])

        gv = g_s[...]
        a = a_s[...]
        mult = jnp.sqrt(_one_minus_exp(2.0 * log_a))
        iv = i_ref[...]
        rv = r_ref[...]
        xcv = xc_ref[...]
        hprev = hext[pl.ds(HALO_C - 1, tm), :]
        dla = gv * hprev * a - (gv * iv * xcv) * (a * a) / mult
        di = gv * mult * xcv
        dpr = (dla * (-LRU_C * sp)) * rv * (1.0 - rv)
        dpi = di * iv * (1.0 - iv)
        dpr_s[...] = dpr
        dpi_s[...] = dpi
        dxc_ref[...] = gv * mult * iv
        dsp = jnp.sum(dla * rv, axis=0, keepdims=True) * (-LRU_C)
        _acc_out(i, dlam_ref, -dsp * jax.nn.sigmoid(-lam_ref[...]))
        _acc_out(i, dbrg_ref, jnp.sum(dpr, axis=0, keepdims=True))
        _acc_out(i, dbig_ref, jnp.sum(dpi, axis=0, keepdims=True))
        for h in range(LRU_HEADS):
            cs = slice(h * LRU_HD, (h + 1) * LRU_HD)
            xh = xc_ref[:, cs].astype(BF16)
            dr_h = dpr_s[:, cs].astype(BF16)
            di_h = dpi_s[:, cs].astype(BF16)
            dxc_ref[:, cs] += (
                lax.dot_general(dr_h, wrg_ref[h], (((1,), (1,)), ((), ())), preferred_element_type=F32)
                + lax.dot_general(di_h, wig_ref[h], (((1,), (1,)), ((), ())), preferred_element_type=F32))
            dwrg_ref[h] += lax.dot_general(xh, dr_h, (((0,), (0,)), ((), ())), preferred_element_type=F32)
            dwig_ref[h] += lax.dot_general(xh, di_h, (((0,), (0,)), ((), ())), preferred_element_type=F32)

    rrow = lambda w: pl.BlockSpec((tm, w), lambda i: (nt - 1 - i, 0))
    hspec = pl.BlockSpec((HALO_C, wl), lambda i: (jnp.maximum((nt - 1 - i) * per - 1, 0), 0))
    wspec = _full((LRU_HEADS, LRU_HD, LRU_HD))
    return pl.pallas_call(
        body, name="odd_bwd_a", grid=(nt,),
        in_specs=[pl.BlockSpec((tm, wl), lambda i: (nt - 1 - i, 1)), rrow(wl), rrow(wl), rrow(wl), rrow(wl), hspec,
                  rrow(wl), wspec, wspec, _full((1, wl)), ANY],
        out_specs=[rrow(wl), rrow(wl), wspec, wspec, _full((1, wl)), _full((1, wl)), _full((1, wl))],
        out_shape=[jax.ShapeDtypeStruct((t, wl), F32), jax.ShapeDtypeStruct((t, wl), BF16),
                   jax.ShapeDtypeStruct((LRU_HEADS, LRU_HD, LRU_HD), F32),
                   jax.ShapeDtypeStruct((LRU_HEADS, LRU_HD, LRU_HD), F32)] + [jax.ShapeDtypeStruct((1, wl), F32)] * 3,
        scratch_shapes=[pltpu.VMEM((tm + HALO_C, wl), F32)] + [pltpu.VMEM((tm, wl), F32)] * 5
                       + [pltpu.VMEM((1, wl), F32)],
        compiler_params=_params("arbitrary"),
    )(p, xc, r, ig, hs, hs, dy, wrg, wig, lam, dep)


def _odd_bwd_b(p, dxc, dgate, ccw, *, tm):
    t = p.shape[0]
    wl = W_LRU
    nt = t // tm

    def body(xr, xrh, dxc_ref, dxcn, dgate_ref, ccw_ref, dp_ref, dcw_ref, dcb_ref, xext, gext):
        i = pl.program_id(0)

        @pl.when(i == 0)
        def _():
            dcw_ref[...] = jnp.zeros_like(dcw_ref)

        xext[0:HALO_C, :] = (i > 0).astype(F32) * xrh[...]
        xext[HALO_C:, :] = xr[...]
        gext[0:tm, :] = dxc_ref[...]
        gext[tm:, :] = (i < nt - 1).astype(F32) * dxcn[...]
        g = dxc_ref[...]
        acc = jnp.zeros((tm, wl), F32)
        for k in range(LRU_CONV_K):
            acc = acc + ccw_ref[k:k + 1, :] * gext[pl.ds(LRU_CONV_K - 1 - k, tm), :]
            dcw_ref[k:k + 1, :] += jnp.sum(
                g * xext[pl.ds(HALO_C - (LRU_CONV_K - 1) + k, tm), :], axis=0, keepdims=True)

        _acc_out(i, dcb_ref, jnp.sum(g, axis=0, keepdims=True))
        dp_ref[:, 0:wl] = acc.astype(BF16)
        dp_ref[:, wl:2 * wl] = dgate_ref[...]

    row = lambda w: pl.BlockSpec((tm, w), lambda i: (i, 0))
    return pl.pallas_call(
        body, name="odd_bwd_b", grid=(nt,),
        in_specs=[_col(tm, wl, 0), _prev_halo(tm, HALO_C, wl, 0), row(wl), _next_halo(tm, HALO_C, wl, 0, t), row(wl),
                  _full((8, wl))],
        out_specs=[row(2 * wl), _full((8, wl)), _full((1, wl))],
        out_shape=[jax.ShapeDtypeStruct((t, 2 * wl), BF16), jax.ShapeDtypeStruct((8, wl), F32),
                   jax.ShapeDtypeStruct((1, wl), F32)],
        scratch_shapes=[pltpu.VMEM((tm + HALO_C, wl), F32), pltpu.VMEM((tm + HALO_C, wl), F32)],
        compiler_params=_params("arbitrary"),
    )(p, p, dxc, dxc, dgate, ccw)


def _local_step(x, target, layer_weights, final_norm, on_grads):
    t = x.shape[0]
    tm, tx, tl, tw = min(TM_MATMUL, t), min(TM_MIXER, t), min(TM_STREAM, t), min(TM_WGRAD, t)
    h = x
    saved = []
    for layer in range(N_LAYERS):
        w = layer_weights(layer, h)
        p, hn = _norm_matmul(h, w["norm"], w["w_in"], w["dep"], tm=tl)
        if layer % 2 == 0:
            y, *acts = _even_fwd(p, w["conv_w"], w["conv_b"], w["ln_g"], w["ln_b"], w["pool_w"], w["pool_b"],
                                 w["pool_scale"], tm=tx)
        else:
            y, *acts = _odd_fwd(p, w["conv_w"], w["conv_b"], w["w_rg"], w["b_rg"], w["w_ig"], w["b_ig"], w["lam"],
                                tm=tx)
        w_out, dep = w["w_out"](y)
        saved.append((w, w_out, h, p, hn, y, acts))
        h = _out_proj(h, y, w_out, dep, tm=tm)
    loss, dh, d_final = _loss_head(h, final_norm, target, tm=tm)

    dep = d_final
    for layer in reversed(range(N_LAYERS)):
        w, w_out, h_in, p, hn, y, acts = saved[layer]
        sfx = "even" if layer % 2 == 0 else "odd"
        dy, dw_out = _out_proj_bwd(dh, y, w_out, dep, tm=tm)
        dep = on_grads(layer, {"w_out_" + sfx: dw_out}, dep, False)
        if layer % 2 == 0:
            u1, e, dmat = acts
            du1, dd, dgat, dlg, dlb, dpb, dps, dpw = _even_bwd_a(p, u1, e, dmat, dy, w["ln_g"], w["ln_b"], w["pool_w"],
                                                                 w["pool_scale"], dep, tm=tx)
            dp, dcw, dcb = _even_bwd_b(p, du1, dd, dgat, w["conv_w"], tm=tx)
            grads = dict(conv_a_w=dcw[:CONV_K], conv_a_b=dcb, ln_a_g=dlg, ln_a_b=dlb, pool_w=dpw, pool_b=dpb,
                         pool_scale=dps)
        else:
            xc, r, ig, hs = acts
            dxc, dgate, dwrg, dwig, dbrg, dbig, dlam = _odd_bwd_a(p, xc, r, ig, hs, dy, w["w_rg"], w["w_ig"],
                                                                  w["lam"], dep, tm=tx)
            dp, dccw, dccb = _odd_bwd_b(p, dxc, dgate, w["conv_w"], tm=tx)
            grads = dict(conv_c_w=dccw[:LRU_CONV_K], conv_c_b=dccb, w_rg=dwrg, b_rg=dbrg, w_ig=dwig, b_ig=dbig,
                         lru_lambda=dlam)
        grads["w_in_" + sfx] = _in_proj_bwd_w(hn, dp, N_DEV, tm=tw)
        dep = on_grads(layer, grads, dep, False)
        dh, dg = _in_proj_bwd_x(dp, w["w_in"], h_in, w["norm"], dh, dep, tm=tm)
        rest = {"norm_" + sfx: dg}
        if layer == N_LAYERS - 1:
            rest["final_norm"], rest["loss"] = d_final, loss
        dep = on_grads(layer, rest, dep, True)
    return loss, dh


def _slot(px, py, pc):
    return 4 * px + 2 * py + pc


def _peers(x, y, c):
    return [(1 - x if k & 4 else x, 1 - y if k & 2 else y, 1 - c if k & 1 else c) for k in range(1, N_DEV)]


def _all_gather(arrs, name):
    n = len(arrs)

    def body(*refs):
        ins, outs = refs[:n], refs[n:2 * n]
        send_sems, recv_sems, local_sems = refs[2 * n:]
        x, y, c = lax.axis_index("x"), lax.axis_index("y"), lax.axis_index("c")
        me, sibling = (x, y, c), (x, y, 1 - c)
        chips = [(1 - x, y), (x, 1 - y), (1 - x, 1 - y)]

        def copy(a, k, block, to, src=None):
            rows = outs[a].at[_slot(*block)]
            return pltpu.make_async_remote_copy(
                src_ref=rows if src is None else src, dst_ref=rows, send_sem=send_sems.at[a, k],
                recv_sem=recv_sems.at[a, k], device_id=to, device_id_type=MESH)

        mine = [pltpu.make_async_copy(ins[a], outs[a].at[_slot(*me)], local_sems.at[a]) for a in range(n)]
        for cp in mine:
            cp.start()
        first = []
        for a in range(n):
            first.append(copy(a, 0, me, sibling, src=ins[a]))
            first += [copy(a, 1 + j, me, (*chip, c), src=ins[a]) for j, chip in enumerate(chips)]
        for cp in first:
            cp.start()
        passed = []
        for j, chip in enumerate(chips):
            for a in range(n):
                copy(a, 1 + j, (*chip, c), me).wait_recv()
                fwd = copy(a, 4 + j, (*chip, c), sibling)
                fwd.start()
                passed.append(fwd)
        for a in range(n):
            copy(a, 0, sibling, me).wait_recv()
            for j, chip in enumerate(chips):
                copy(a, 4 + j, (*chip, 1 - c), me).wait_recv()
        for cp in first + passed:
            cp.wait_send()
        for cp in mine:
            cp.wait()

    return pl.pallas_call(
        body, name=name,
        in_specs=[ANY] * n, out_specs=[ANY] * n,
        out_shape=[jax.ShapeDtypeStruct((N_DEV,) + a.shape, a.dtype) for a in arrs],
        scratch_shapes=[pltpu.SemaphoreType.DMA((n, 7)), pltpu.SemaphoreType.DMA((n, 7)),
                        pltpu.SemaphoreType.DMA((n,))],
    )(*arrs)


N_COPIES = {"gather": N_PEERS, "scatter": N_PEERS, "chip_gather": 4, "forward": 3, "pair_scatter": 4,
            "chip_scatter": 3}


def _exchange_plan(mode, x, y, c):
    me = _slot(x, y, c)
    chips = [(1 - x, y), (x, 1 - y), (1 - x, 1 - y)]
    if mode == "forward":
        return [((x, y, 1 - c), ("land", _slot(*chip, c)), _slot(*chip, c), _slot(*chip, 1 - c)) for chip in chips]
    if mode == "pair_scatter":
        return [((x, y, 1 - c), ("src", _slot(q // 2, q % 2, 1 - c)), q, q) for q in range(4)]
    if mode == "chip_scatter":
        return [((*chip, c), ("src", 2 * chip[0] + chip[1]), 2 * x + y, 2 * chip[0] + chip[1]) for chip in chips]
    peers = _peers(x, y, c)
    if mode == "chip_gather":
        peers = [(x, y, 1 - c), (1 - x, y, c), (x, 1 - y, c), (1 - x, 1 - y, c)]
    return [(p, ("src", _slot(*p)) if mode == "scatter" else ("src", None), me, _slot(*p)) for p in peers]


def _exchange_copy(src_ref, land_ref, plan, send_sem, recv_sem, start):
    peer, (which, block), there, here = plan
    src = land_ref if which == "land" else src_ref
    return pltpu.make_async_remote_copy(
        src_ref=src if block is None else src.at[block], dst_ref=land_ref.at[there if start else here],
        send_sem=send_sem, recv_sem=recv_sem, device_id=peer, device_id_type=MESH)


def _exchange_start(groups, deps, name):
    flat = [pair for _, g in groups for pair in g]
    n, ng = len(flat), len(groups)

    def body(*refs):
        src_refs, land_refs = refs[:n], refs[n:2 * n]
        outs = refs[2 * n + len(deps):]
        sems, token = outs[:2 * ng], outs[2 * ng + 2 * n]
        x, y, c = lax.axis_index("x"), lax.axis_index("y"), lax.axis_index("c")
        base = 0
        for gi, (mode, g) in enumerate(groups):
            nc = N_COPIES[mode]
            for k, plan in enumerate(_exchange_plan(mode, x, y, c)):
                for ai in range(len(g)):
                    _exchange_copy(src_refs[base + ai], land_refs[base + ai], plan, sems[2 * gi].at[ai * nc + k],
                                   sems[2 * gi + 1].at[ai * nc + k], True).start()
            base += len(g)
        token[...] = jnp.zeros_like(token)

    operands = [pltpu.with_memory_space_constraint(a, pltpu.HBM) for a in
                [s for s, _ in flat] + [l for _, l in flat]]
    out_shape = []
    for mode, g in groups:
        out_shape += [pltpu.SemaphoreType.DMA((len(g) * N_COPIES[mode],))] * 2
    out_shape += [pltpu.HBM(a.shape, a.dtype) for a in operands]
    out_shape.append(jax.ShapeDtypeStruct((8, 128), F32))
    outs = pl.pallas_call(
        body, name=name, out_shape=out_shape,
        in_specs=[HBM] * (2 * n) + [ANY] * len(deps),
        out_specs=[SEM] * (2 * ng) + [HBM] * (2 * n) + [pl.BlockSpec(memory_space=pltpu.VMEM)],
        input_output_aliases={i: 2 * ng + i for i in range(2 * n)},
        compiler_params=pltpu.CompilerParams(has_side_effects=pltpu.SideEffectType.DATAFLOW_SIDE_EFFECTING),
    )(*operands, *deps)
    handles, base = [], 0
    for gi, (mode, g) in enumerate(groups):
        srcs = outs[2 * ng + base:2 * ng + base + len(g)]
        lands = outs[2 * ng + n + base:2 * ng + n + base + len(g)]
        handles.append((mode, outs[2 * gi], outs[2 * gi + 1], list(srcs), list(lands)))
        base += len(g)
    return handles, outs[-1]


def _exchange_wait(handle, after, name):
    mode, send_sems, recv_sems, srcs, lands = handle
    n = len(srcs)
    nc = N_COPIES[mode]

    def body(*refs):
        src_refs, land_refs = refs[:n], refs[n:2 * n]
        send_ref, recv_ref = refs[2 * n], refs[2 * n + 1]
        x, y, c = lax.axis_index("x"), lax.axis_index("y"), lax.axis_index("c")
        for k, plan in enumerate(_exchange_plan(mode, x, y, c)):
            for a in range(n):
                cp = _exchange_copy(src_refs[a], land_refs[a], plan, send_ref.at[a * nc + k], recv_ref.at[a * nc + k],
                                    False)
                cp.wait_send()
                cp.wait_recv()

    outs = pl.pallas_call(
        body, name=name,
        out_shape=[pltpu.HBM(a.shape, a.dtype) for a in srcs + lands],
        in_specs=[HBM] * (2 * n) + [SEM, SEM] + [ANY] * len(after),
        out_specs=[HBM] * (2 * n),
        input_output_aliases={i: i for i in range(2 * n)},
        compiler_params=pltpu.CompilerParams(has_side_effects=pltpu.SideEffectType.DATAFLOW_SIDE_EFFECTING),
    )(*srcs, *lands, send_sems, recv_sems, *after)
    return list(outs[n:])


def _pair_sum(mine, theirs):
    nq, r, c = mine.shape
    tr = r

    def body(a_ref, b_ref, o_ref):
        o_ref[...] = (a_ref[...].astype(F32) + b_ref[...].astype(F32)).astype(BF16)

    blk = pl.BlockSpec((1, tr, c), lambda q, i: (q, i, 0))
    return pl.pallas_call(
        body, name="pair_sum", grid=(nq, r // tr), in_specs=[blk, blk], out_specs=blk,
        out_shape=jax.ShapeDtypeStruct(mine.shape, BF16),
        compiler_params=_params("arbitrary", "arbitrary"),
    )(mine, theirs)


def _adamw_math(w, g, m, v):
    c1 = 1.0 - ADAM_B1 ** ADAM_STEP
    c2 = 1.0 - ADAM_B2 ** ADAM_STEP
    nm = ADAM_B1 * m + (1.0 - ADAM_B1) * g
    nv = ADAM_B2 * v + (1.0 - ADAM_B2) * (g * g)
    delta = -ADAM_LR * ((nm / c1) / (jnp.sqrt(nv / c2) + ADAM_EPS) + ADAM_WD * w)
    return delta, nm, nv


ADAM_BLOCK_BYTES = 2 * 1024 * 1024


def _row_tile(r, c):
    for cand in (r, 1024, 512, 256, 128, 64, 32, 16, 8):
        if cand <= r and r % cand == 0 and cand * c * 4 <= ADAM_BLOCK_BYTES:
            return cand
    return r


def _adamw_whole(items, name):
    ni = len(items)

    def body(*refs):
        ins, outs = refs[:4 * ni], refs[4 * ni:]
        for k in range(ni):
            w_ref, p_ref, m_ref, v_ref = ins[4 * k:4 * k + 4]
            g = p_ref[0].astype(F32)
            for s in range(1, p_ref.shape[0]):
                g = g + p_ref[s].astype(F32)
            delta, nm, nv = _adamw_math(w_ref[...], g, m_ref[...], v_ref[...])
            g_ref, d_ref, nm_ref, nv_ref = outs[4 * k:4 * k + 4]
            g_ref[...], d_ref[...], nm_ref[...], nv_ref[...] = g, delta, nm, nv

    in_specs, out_specs, out_shape, operands = [], [], [], []
    for w, parts, m, v in items:
        in_specs += [_full(w.shape), _full(parts.shape), _full(w.shape), _full(w.shape)]
        operands += [w, parts, m, v]
        out_specs += [_full(w.shape)] * 4
        out_shape += [jax.ShapeDtypeStruct(w.shape, F32)] * 4
    outs = pl.pallas_call(body, name=name, grid=(1,), in_specs=in_specs, out_specs=out_specs, out_shape=out_shape,
                          compiler_params=_params("arbitrary"))(*operands)
    return [tuple(outs[4 * k:4 * k + 4]) for k in range(ni)]


def _adamw(items, layer0, bufs, name):
    ni = len(items)
    nl = items[0][1].shape[1]
    tiles = [_row_tile(w.shape[1], w.shape[2]) for w, _, _, _ in items]
    steps = [w.shape[1] // tr for (w, _, _, _), tr in zip(items, tiles)]
    ns = steps[0]
    assert all(s == ns for s in steps)
    nb = 0 if bufs is None else 4 * ni

    def body(*refs):
        ins, outs = refs[:4 * ni], refs[4 * ni + nb:]
        for k in range(ni):
            w_ref, p_ref, m_ref, v_ref = ins[4 * k:4 * k + 4]
            g = p_ref[0, 0].astype(F32)
            for s in range(1, p_ref.shape[0]):
                g = g + p_ref[s, 0].astype(F32)
            delta, nm, nv = _adamw_math(w_ref[0], g, m_ref[0], v_ref[0])
            g_ref, d_ref, nm_ref, nv_ref = outs[4 * k:4 * k + 4]
            g_ref[0], d_ref[0], nm_ref[0], nv_ref[0] = g, delta, nm, nv

    in_specs, out_specs, out_shape, operands = [], [], [], []
    for (w, parts, m, v), tr in zip(items, tiles):
        blk = pl.BlockSpec((1, tr, w.shape[2]), lambda l, i: (layer0 + l, i, 0))
        in_specs += [blk, pl.BlockSpec((parts.shape[0], 1, tr, w.shape[2]), lambda l, i: (0, l, i, 0)), blk, blk]
        operands += [w, parts, m, v]
        out_specs += [blk] * 4
        out_shape += [jax.ShapeDtypeStruct(w.shape, F32)] * 4
    if bufs is not None:
        in_specs += [ANY] * nb
        operands += [b for item in bufs for b in item]
    outs = pl.pallas_call(
        body, name=name, grid=(nl, ns), in_specs=in_specs, out_specs=out_specs, out_shape=out_shape,
        input_output_aliases={4 * ni + i: i for i in range(nb)},
        compiler_params=_params("arbitrary", "arbitrary"),
    )(*operands)
    return [tuple(outs[4 * k:4 * k + 4]) for k in range(ni)]


NAMES = ("norm_even", "w_in_even", "conv_a_w", "conv_a_b", "ln_a_g", "ln_a_b", "pool_w", "pool_b", "pool_scale",
         "w_out_even", "norm_odd", "w_in_odd", "conv_c_w", "conv_c_b", "w_rg", "b_rg", "w_ig", "b_ig", "lru_lambda",
         "w_out_odd", "final_norm")
SMALL_GATHERED = ("conv_a_w", "pool_b", "norm_odd", "conv_c_w", "conv_c_b", "b_rg", "b_ig", "lru_lambda")
BIG = (("w_in_even", "w_out_even"), ("w_in_odd", "w_out_odd"))
SMALL = (("conv_a_w", "pool_b", "pool_w"), ("norm_odd", "conv_c_w", "conv_c_b", "b_rg", "b_ig", "lru_lambda"))
REPLICATED = (("norm_even", "conv_a_b", "ln_a_g", "ln_a_b", "pool_scale"), ("w_rg", "w_ig"))
PACK_ROW = 1024


def _pack_rows(flat2d):
    pad = (-flat2d.shape[1]) % PACK_ROW
    return jnp.pad(flat2d, ((0, 0), (0, pad))).reshape(flat2d.shape[0], -1, 128)


def _unpack(flat, shapes):
    out, off = [], 0
    for s in shapes:
        n = 1
        for d in s:
            n *= d
        out.append(flat[..., off:off + n].reshape(flat.shape[:-1] + tuple(s)))
        off += n
    return out


def _to_global(name, g):
    if name in ("conv_a_w", "pool_b", "conv_c_w"):
        return jnp.transpose(g, (1, 2, 0, 3)).reshape(g.shape[1], g.shape[2], -1)
    if name == "pool_w":
        return jnp.transpose(g, (1, 2, 0, 3, 4)).reshape(2, 4, POOL_GW, POOL_GW)
    return jnp.transpose(g, (1, 0, 2)).reshape(g.shape[1], -1)


def _to_blocks(name, g):
    if name == "conv_a_w":
        return jnp.transpose(g.reshape(CONV_K, N_DEV, -1), (1, 0, 2))
    if name == "conv_c_w":
        return jnp.transpose(g.reshape(LRU_CONV_K, N_DEV, -1), (1, 0, 2))
    if name == "pool_b":
        return jnp.transpose(g.reshape(4, N_DEV, -1), (1, 0, 2))
    if name == "pool_w":
        return jnp.transpose(g.reshape(4, N_DEV, POOL_GW // N_DEV, POOL_GW), (1, 0, 2, 3))
    return g.reshape(N_DEV, -1)


def _as3d(a):
    if a.ndim == 1:
        return a.reshape(1, 1, -1)
    if a.ndim == 2:
        return a.reshape(a.shape[0], 1, a.shape[1])
    return a.reshape(a.shape[0], -1, a.shape[-1])


def kernel(x, norm_even, w_in_even, conv_a_w, conv_a_b, ln_a_g, ln_a_b, pool_w, pool_b, pool_scale, w_out_even, norm_odd, w_in_odd, conv_c_w, conv_c_b, w_rg, b_rg, w_ig, b_ig, lru_lambda, w_out_odd, final_norm, loss_target, m_norm_even, m_w_in_even, m_conv_a_w, m_conv_a_b, m_ln_a_g, m_ln_a_b, m_pool_w, m_pool_b, m_pool_scale, m_w_out_even, m_norm_odd, m_w_in_odd, m_conv_c_w, m_conv_c_b, m_w_rg, m_b_rg, m_w_ig, m_b_ig, m_lru_lambda, m_w_out_odd, m_final_norm, v_norm_even, v_w_in_even, v_conv_a_w, v_conv_a_b, v_ln_a_g, v_ln_a_b, v_pool_w, v_pool_b, v_pool_scale, v_w_out_even, v_norm_odd, v_w_in_odd, v_conv_c_w, v_conv_c_b, v_w_rg, v_b_rg, v_w_ig, v_b_ig, v_lru_lambda, v_w_out_odd, v_final_norm):
    w_loc = dict(zip(NAMES, [norm_even, w_in_even, conv_a_w, conv_a_b, ln_a_g, ln_a_b, pool_w, pool_b, pool_scale,
                             w_out_even, norm_odd, w_in_odd, conv_c_w, conv_c_b, w_rg, b_rg, w_ig, b_ig, lru_lambda,
                             w_out_odd, final_norm]))
    m_loc = dict(zip(NAMES, [m_norm_even, m_w_in_even, m_conv_a_w, m_conv_a_b, m_ln_a_g, m_ln_a_b, m_pool_w, m_pool_b,
                             m_pool_scale, m_w_out_even, m_norm_odd, m_w_in_odd, m_conv_c_w, m_conv_c_b, m_w_rg,
                             m_b_rg, m_w_ig, m_b_ig, m_lru_lambda, m_w_out_odd, m_final_norm]))
    v_loc = dict(zip(NAMES, [v_norm_even, v_w_in_even, v_conv_a_w, v_conv_a_b, v_ln_a_g, v_ln_a_b, v_pool_w, v_pool_b,
                             v_pool_scale, v_w_out_even, v_norm_odd, v_w_in_odd, v_conv_c_w, v_conv_c_b, v_w_rg,
                             v_b_rg, v_w_ig, v_b_ig, v_lru_lambda, v_w_out_odd, v_final_norm]))
    me = _slot(lax.axis_index("x"), lax.axis_index("y"), lax.axis_index("c"))

    def landing(own):
        zone = lax.empty((N_DEV,) + own.shape[1:], own.dtype)
        return lax.dynamic_update_slice(zone, own, (me,) + (0,) * (own.ndim - 1))

    small_shapes = [w_loc[n].shape for n in SMALL_GATHERED]
    small = jnp.concatenate([w_loc[n].reshape(1, -1) for n in SMALL_GATHERED], axis=1)
    first = _all_gather([w_in_even[0].astype(BF16), pool_w.astype(BF16), _pack_rows(small)[0]], "gather_first")
    g_small = dict(zip(SMALL_GATHERED, [_to_global(n, g) for n, g in
                                        zip(SMALL_GATHERED, _unpack(first[2].reshape(N_DEV, -1), small_shapes))]))
    pool_w_all = _to_global("pool_w", first[1])

    def pairs_of(shards):
        return [(s.astype(BF16), landing(s.astype(BF16)[None])) for s in shards]

    shards = {1: [w_in_odd[0], w_out_odd[0]], 2: [w_in_even[1], w_out_even[1]], 3: [w_in_odd[1], w_out_odd[1]]}
    leg_a, leg_b = {}, {}
    (w_out_0, leg_a[1]), token_1 = _exchange_start(
        [("gather", pairs_of([w_out_even[0]])), ("chip_gather", pairs_of(shards[1]))], [first[0]], "gather_start_1")
    unused = jnp.zeros((8, 128), F32)

    def second_leg(layer, y):
        lands = _exchange_wait(leg_a[layer], [y], f"gather_wait_a_{layer}")
        (leg_b[layer],), token = _exchange_start([("forward", [(unused, l) for l in lands])], [],
                                                 f"gather_forward_{layer}")
        return token

    def layer_weights(layer, h):
        j = layer // 2
        dep = h
        if layer == 0:
            w_in, dep = first[0], token_1
            w_out_now = lambda y: _exchange_wait(w_out_0, [y], "gather_wait_out_0")[0]
        else:
            w_in, w_out_got = _exchange_wait(leg_b[layer], [h], f"gather_wait_b_{layer}")
            w_out_now = lambda y: w_out_got
            if layer + 1 in shards:
                (leg_a[layer + 1],), dep = _exchange_start([("chip_gather", pairs_of(shards[layer + 1]))], [w_in],
                                                           f"gather_start_{layer + 1}")

        def w_out(y):
            return w_out_now(y), (second_leg(layer + 1, y) if layer + 1 in shards else y)

        w_in = jnp.transpose(w_in, (1, 0, 2)).reshape(D_MODEL, -1)
        row = lambda a: a[j][None]
        if layer % 2 == 0:
            return dict(dep=dep, norm=row(norm_even), w_in=w_in,
                        w_out=lambda y: (lambda wo, d: (wo.reshape(W_EVEN_MIX, D_MODEL), d))(*w_out(y)),
                        conv_w=jnp.pad(g_small["conv_a_w"][j], ((0, 1), (0, 0))), conv_b=row(conv_a_b),
                        ln_g=row(ln_a_g), ln_b=row(ln_a_b), pool_w=pool_w_all[j],
                        pool_b=g_small["pool_b"][j].reshape(1, W_POOL), pool_scale=row(pool_scale))
        return dict(dep=dep, norm=row(g_small["norm_odd"]), w_in=w_in,
                    w_out=lambda y: (lambda wo, d: (wo.reshape(W_LRU, D_MODEL), d))(*w_out(y)),
                    conv_w=jnp.pad(g_small["conv_c_w"][j], ((0, 4), (0, 0))), conv_b=row(g_small["conv_c_b"]),
                    w_rg=w_rg[j].astype(BF16), b_rg=row(g_small["b_rg"]), w_ig=w_ig[j].astype(BF16),
                    b_ig=row(g_small["b_ig"]), lam=row(g_small["lru_lambda"]))

    pending, exchanges, small_layout = {}, {}, {}
    last_token, pair_leg = [], []

    def on_grads(layer, grads, dep, last):
        par = layer % 2
        w_in_name, w_out_name = BIG[par]
        have = pending.setdefault(layer, {})
        have.update(grads)
        eager = layer == 0
        scatter, gather = {}, {}
        if w_out_name in have and (eager or last):
            scatter["out"] = have.pop(w_out_name).reshape(N_DEV, -1, D_MODEL)
        pair, chip = {}, {}
        if w_in_name in have and eager:
            pair["in"] = have.pop(w_in_name)
        elif w_in_name in have and last:
            scatter["in"] = have.pop(w_in_name)
        deps = []
        if pair:
            whole = pair["in"]
            deps = [lax.dynamic_index_in_dim(whole.reshape((4, 2) + whole.shape[1:]), lax.axis_index("c"), 1,
                                             keepdims=False)]
        if eager and last:
            mine, theirs = pair_leg.pop()
            chip["in"] = _pair_sum(mine, _exchange_wait(theirs, list(grads.values()), "grads_wait_pair")[0])
        if layer not in small_layout and all(n in have for n in SMALL[par]) and (eager or last):
            blocks = [_to_blocks(n, have[n]) for n in SMALL[par]]
            small_layout[layer] = [b.shape[1:] for b in blocks]
            scatter["small"] = _pack_rows(jnp.concatenate([b.reshape(N_DEV, -1) for b in blocks], axis=1))
        for n in REPLICATED[1]:
            if n in have:
                gather[n] = have.pop(n).astype(BF16).reshape(-1, LRU_HD)
        if last:
            vectors = [have[n] for n in REPLICATED[par]] if par == 0 else []
            if "final_norm" in have:
                vectors += [have["final_norm"], jnp.pad(have["loss"].reshape(-1), (0, 127))]
            if vectors:
                gather["rep32"] = jnp.concatenate([v.reshape(-1) for v in vectors]).reshape(-1, 128)
        my_chip = 2 * lax.axis_index("x") + lax.axis_index("y")
        groups, keys = [], []
        for mode, arrays, pairs in (
                ("scatter", scatter, [(s, landing(lax.dynamic_slice_in_dim(s, me, 1, 0))) for s in scatter.values()]),
                ("gather", gather, [(s, landing(s[None])) for s in gather.values()]),
                ("pair_scatter", pair, [(s, lax.empty((4,) + s.shape[1:], s.dtype)) for s in pair.values()]),
                ("chip_scatter", chip, [(s, lax.dynamic_update_slice(
                    lax.empty(s.shape, s.dtype), lax.dynamic_slice_in_dim(s, my_chip, 1, 0),
                    (my_chip,) + (0,) * (s.ndim - 1))) for s in chip.values()])):
            if arrays:
                groups.append((mode, pairs))
                keys.append(list(arrays))
        if not groups:
            return dep
        handles, token = _exchange_start(groups, deps, f"grads_start_{layer}_{'_'.join(k for ks in keys for k in ks)}")
        if pair:
            pair_leg.append((deps[0], handles.pop()))
            keys.pop()
        exchanges.setdefault(layer, []).extend(zip(keys, handles))
        last_token[:] = [token]
        return token

    loss, grad_x = _local_step(x[0], loss_target[0], layer_weights, final_norm[None], on_grads)

    w3 = {n: _as3d(w_loc[n]) for n in NAMES}
    m3 = {n: _as3d(m_loc[n]) for n in NAMES}
    v3 = {n: _as3d(v_loc[n]) for n in NAMES}
    results, small_parts = {}, {}
    after = list(last_token)
    for layer in (3, 2, 1, 0):
        par, j = layer % 2, layer // 2
        got = {}
        for keys, handle in exchanges[layer]:
            got.update(zip(keys, _exchange_wait(handle, after, f"grads_wait_{layer}_{'_'.join(keys)}")))
        parts = {BIG[par][0]: got["in"], BIG[par][1]: got["out"]}
        parts.update(zip(SMALL[par], _unpack(got["small"].reshape(N_DEV, -1), small_layout[layer])))
        if par == 1:
            parts.update({n: got[n] for n in REPLICATED[1]})
        else:
            parts.update(zip(REPLICATED[0], _unpack(got["rep32"].reshape(N_DEV, -1)[:, :len(REPLICATED[0]) * D_MODEL],
                                                    [w_loc[n].shape[1:] for n in REPLICATED[0]])))
        small_parts[layer] = {}
        for n, pt in parts.items():
            if w3[n].shape[1] >= 128:
                pt = pt.reshape((pt.shape[0], 1) + w3[n].shape[1:])
                results[n] = _adamw([(w3[n], pt, m3[n], v3[n])], j, [results[n]] if n in results else None,
                                    f"adamw_{n}_{layer}")[0]
            else:
                small_parts[layer][n] = pt.reshape((pt.shape[0],) + w_loc[n].shape[1:])
        if layer + 2 in small_parts:
            snames = list(small_parts[layer])
            items = [(w_loc[n], jnp.stack([small_parts[layer][n], small_parts[layer + 2][n]], axis=1), m_loc[n],
                      v_loc[n]) for n in snames]
            for n, r in zip(snames, _adamw_whole(items, f"adamw_small_{par}")):
                results[n] = r
        if layer == N_LAYERS - 1:
            tail = got["rep32"].reshape(N_DEV, -1)
            item = (w3["final_norm"], tail[:, :D_MODEL].reshape(N_DEV, 1, 1, D_MODEL), m3["final_norm"],
                    v3["final_norm"])
            results["final_norm"] = _adamw([item], 0, None, "adamw_final_norm")[0]
            total = jnp.sum(tail[:, D_MODEL])
        after = [results[BIG[par][0]][1]]

    outs = [[results[n][k].reshape(w_loc[n].shape) for n in NAMES] for k in range(4)]
    return (total, grad_x[None], *outs[0], *outs[1], *outs[2], *outs[3])
```
